```python
import math
import jax, jax.numpy as jnp
from jax import lax
import numpy as np

D_MODEL = 2048
BATCH = 8
SEQ = 4096
DEPTH = 2

N_META = 16
EPS = 1e-6
D_CONV = D_MODEL // 2
CONV_WIDTH = 3
N_HEADS = D_MODEL // 128
QK_NOPE = 128
QK_ROPE = 64
QK_HEAD = QK_NOPE + QK_ROPE
V_HEAD = 128
Q_LORA = 512
KV_LORA = 512
ROPE_THETA = 10000.0
Q_BLOCK = 128
D_POOL = D_MODEL // 2
POOL_WINDOWS = (2, 4, 8, 16)
POOL_GROUP = D_POOL // len(POOL_WINDOWS)
N_BRANCH = 3
D_FF = 4 * D_MODEL
D_IN = 3 * D_CONV + Q_LORA + KV_LORA + QK_ROPE + D_POOL + N_BRANCH * D_MODEL
IN_SPLITS = (3 * D_CONV,
             3 * D_CONV + Q_LORA,
             3 * D_CONV + Q_LORA + KV_LORA,
             3 * D_CONV + Q_LORA + KV_LORA + QK_ROPE,
             3 * D_CONV + Q_LORA + KV_LORA + QK_ROPE + D_POOL)

kernel_name = 'hybrid_gated_conv_mla_pool_block'


def rms_norm(x, g):
    xf = x.astype(jnp.float32)
    y = xf * lax.rsqrt(jnp.mean(xf * xf, axis=-1, keepdims=True) + EPS)
    return (y * g.astype(jnp.float32)).astype(x.dtype)


def rope_tables(T, dtype):
    pos = jnp.arange(T, dtype=jnp.float32)
    inv = ROPE_THETA ** (-jnp.arange(0, QK_ROPE, 2, dtype=jnp.float32) / QK_ROPE)
    ang = pos[:, None] * inv[None, :]
    return jnp.cos(ang).astype(dtype), jnp.sin(ang).astype(dtype)


def apply_rope_tail(x, cos, sin):
    x_nope, x_rope = x[..., :QK_NOPE], x[..., QK_NOPE:]
    x1, x2 = jnp.split(x_rope, 2, axis=-1)
    c, s = cos[None, :, None, :], sin[None, :, None, :]
    return jnp.concatenate([x_nope, x1 * c - x2 * s, x2 * c + x1 * s], axis=-1)


def causal_short_conv(u, w):
    T = u.shape[1]
    up = jnp.pad(u, ((0, 0), (CONV_WIDTH - 1, 0), (0, 0)))
    return sum(w[j] * up[:, j:j + T] for j in range(CONV_WIDTH))


def causal_block_attention(q, k, v):
    B, T, H, dk = q.shape
    Tp = -(-T // Q_BLOCK) * Q_BLOCK
    pad = ((0, 0), (0, Tp - T), (0, 0), (0, 0))
    q, k, v = jnp.pad(q, pad), jnp.pad(k, pad), jnp.pad(v, pad)
    nb = Tp // Q_BLOCK
    q_blocks = q.reshape(B, nb, Q_BLOCK, H, dk).swapaxes(0, 1)
    k_pos = jnp.arange(Tp)
    scale = dk ** -0.5

    def one_block(args):
        qb, blk = args
        s = jnp.einsum('bqhd,bkhd->bhqk', qb, k).astype(jnp.float32) * scale
        q_pos = blk * Q_BLOCK + jnp.arange(Q_BLOCK)
        s = jnp.where(q_pos[:, None] >= k_pos[None, :], s, -jnp.inf)
        p = jax.nn.softmax(s, axis=-1).astype(v.dtype)
        return jnp.einsum('bhqk,bkhd->bqhd', p, v)

    out = lax.map(one_block, (q_blocks, jnp.arange(nb)))
    return out.swapaxes(0, 1).reshape(B, Tp, H, -1)[:, :T]


def multiscale_pool(u, pool_w, pool_scale):
    B, T, _ = u.shape
    uf = u.astype(jnp.float32)
    groups = jnp.split(uf, len(POOL_WINDOWS), axis=-1)
    seen = jnp.arange(1, T + 1, dtype=jnp.float32)
    outs = []
    for g, w in zip(groups, POOL_WINDOWS):
        cs = jnp.cumsum(g, axis=1)
        lagged = jnp.pad(cs, ((0, 0), (w, 0), (0, 0)))[:, :T]
        count = jnp.minimum(seen, float(w))[None, :, None]
        outs.append((cs - lagged) / count - g)
    pooled = jnp.stack(outs, axis=2).astype(u.dtype)
    mixed = jnp.einsum('btgc,gcd->btgd', pooled, pool_w).reshape(B, T, D_POOL)
    return mixed * pool_scale


def hybrid_layer(x, cos, sin, attn_norm, w_in, conv_w, q_lat_norm, kv_lat_norm, w_uq, w_ukv,
                 q_norm, k_norm, pool_w, pool_scale, w_branch_a, w_branch_b, w_branch_c, w_o,
                 mlp_norm, w_up, w_down):
    B, T, _ = x.shape
    h = rms_norm(x, attn_norm)
    proj = h @ w_in
    a_in, q_lat, kv_lat, k_rope, pool_in, gate_logits = jnp.split(proj, IN_SPLITS, axis=-1)

    u_a, b_a, c_a = jnp.split(a_in, 3, axis=-1)
    y_a = b_a * causal_short_conv(c_a * u_a, conv_w)

    q = (rms_norm(q_lat, q_lat_norm) @ w_uq).reshape(B, T, N_HEADS, QK_HEAD)
    kv = (rms_norm(kv_lat, kv_lat_norm) @ w_ukv).reshape(B, T, N_HEADS, QK_NOPE + V_HEAD)
    k_nope, v = kv[..., :QK_NOPE], kv[..., QK_NOPE:]
    k = jnp.concatenate(
        [k_nope, jnp.broadcast_to(k_rope[:, :, None, :], (B, T, N_HEADS, QK_ROPE))], axis=-1)
    q = apply_rope_tail(rms_norm(q, q_norm), cos, sin)
    k = apply_rope_tail(rms_norm(k, k_norm), cos, sin)
    y_b = causal_block_attention(q, k, v).reshape(B, T, N_HEADS * V_HEAD)

    y_c = multiscale_pool(pool_in, pool_w, pool_scale)

    gates = jax.nn.sigmoid(gate_logits).reshape(B, T, N_BRANCH, D_MODEL)
    merged = (gates[:, :, 0] * (y_a @ w_branch_a)
              + gates[:, :, 1] * (y_b @ w_branch_b)
              + gates[:, :, 2] * (y_c @ w_branch_c))
    x = x + merged @ w_o

    h2 = rms_norm(x, mlp_norm)
    return x + jnp.square(jax.nn.relu(h2 @ w_up)) @ w_down


def _fwd_setup_inputs(seed: int = 0) -> dict:
    key = jax.random.key(seed)
    ks = jax.random.split(key, 24)
    f32 = jnp.float32

    def w(k, shape, fan_in):
        return jax.random.normal(k, shape, f32) * fan_in ** -0.5

    def gain(k, shape):
        return 1.0 + 0.05 * jax.random.normal(k, shape, f32)

    L = DEPTH
    return {
        'x': jax.random.normal(ks[0], (BATCH, SEQ, D_MODEL), f32),
        'meta_tokens': jax.random.normal(ks[1], (N_META, D_MODEL), f32),
        'attn_norm': gain(ks[2], (L, D_MODEL)),
        'w_in': w(ks[3], (L, D_MODEL, D_IN), D_MODEL),
        'conv_w': w(ks[4], (L, CONV_WIDTH, D_CONV), CONV_WIDTH),
        'q_lat_norm': gain(ks[5], (L, Q_LORA)),
        'kv_lat_norm': gain(ks[6], (L, KV_LORA)),
        'w_uq': w(ks[7], (L, Q_LORA, N_HEADS * QK_HEAD), Q_LORA),
        'w_ukv': w(ks[8], (L, KV_LORA, N_HEADS * (QK_NOPE + V_HEAD)), KV_LORA),
        'q_norm': gain(ks[9], (L, QK_HEAD)),
        'k_norm': gain(ks[10], (L, QK_HEAD)),
        'pool_w': w(ks[11], (L, len(POOL_WINDOWS), POOL_GROUP, POOL_GROUP), POOL_GROUP),
        'pool_scale': gain(ks[12], (L, D_POOL)),
        'w_branch_a': w(ks[13], (L, D_CONV, D_MODEL), D_CONV),
        'w_branch_b': w(ks[14], (L, N_HEADS * V_HEAD, D_MODEL), N_HEADS * V_HEAD),
        'w_branch_c': w(ks[15], (L, D_POOL, D_MODEL), D_POOL),
        'w_o': w(ks[16], (L, D_MODEL, D_MODEL), D_MODEL),
        'mlp_norm': gain(ks[17], (L, D_MODEL)),
        'w_up': w(ks[18], (L, D_MODEL, D_FF), D_MODEL),
        'w_down': w(ks[19], (L, D_FF, D_MODEL), D_FF),
    }


def _fwd_reference(x, meta_tokens, attn_norm, w_in, conv_w, q_lat_norm, kv_lat_norm, w_uq, w_ukv,
              q_norm, k_norm, pool_w, pool_scale, w_branch_a, w_branch_b, w_branch_c, w_o,
              mlp_norm, w_up, w_down):
    B = x.shape[0]
    meta = jnp.broadcast_to(meta_tokens[None].astype(x.dtype), (B, N_META, D_MODEL))
    h = jnp.concatenate([meta, x], axis=1)
    cos, sin = rope_tables(h.shape[1], h.dtype)
    for l in range(DEPTH):
        h = hybrid_layer(h, cos, sin, attn_norm[l], w_in[l], conv_w[l], q_lat_norm[l],
                         kv_lat_norm[l], w_uq[l], w_ukv[l], q_norm[l], k_norm[l], pool_w[l],
                         pool_scale[l], w_branch_a[l], w_branch_b[l], w_branch_c[l], w_o[l],
                         mlp_norm[l], w_up[l], w_down[l])
    return h[:, N_META:]


import jax as _jax
import jax.numpy as _jnp

TWIN_FORMAT = 'train_step'
FWD_PARAMS = ['x', 'meta_tokens', 'attn_norm', 'w_in', 'conv_w', 'q_lat_norm', 'kv_lat_norm', 'w_uq', 'w_ukv', 'q_norm', 'k_norm', 'pool_w', 'pool_scale', 'w_branch_a', 'w_branch_b', 'w_branch_c', 'w_o', 'mlp_norm', 'w_up', 'w_down']
TWIN_WEIGHTS = ['meta_tokens', 'attn_norm', 'w_in', 'conv_w', 'q_lat_norm', 'kv_lat_norm', 'w_uq', 'w_ukv', 'q_norm', 'k_norm', 'pool_w', 'pool_scale', 'w_branch_a', 'w_branch_b', 'w_branch_c', 'w_o', 'mlp_norm', 'w_up', 'w_down']
TWIN_DIFF_INPUT = 'x'
TWIN_INPUTS = ['x', 'meta_tokens', 'attn_norm', 'w_in', 'conv_w', 'q_lat_norm', 'kv_lat_norm', 'w_uq', 'w_ukv', 'q_norm', 'k_norm', 'pool_w', 'pool_scale', 'w_branch_a', 'w_branch_b', 'w_branch_c', 'w_o', 'mlp_norm', 'w_up', 'w_down', 'loss_target', 'm_meta_tokens', 'm_attn_norm', 'm_w_in', 'm_conv_w', 'm_q_lat_norm', 'm_kv_lat_norm', 'm_w_uq', 'm_w_ukv', 'm_q_norm', 'm_k_norm', 'm_pool_w', 'm_pool_scale', 'm_w_branch_a', 'm_w_branch_b', 'm_w_branch_c', 'm_w_o', 'm_mlp_norm', 'm_w_up', 'm_w_down', 'v_meta_tokens', 'v_attn_norm', 'v_w_in', 'v_conv_w', 'v_q_lat_norm', 'v_kv_lat_norm', 'v_w_uq', 'v_w_ukv', 'v_q_norm', 'v_k_norm', 'v_pool_w', 'v_pool_scale', 'v_w_branch_a', 'v_w_branch_b', 'v_w_branch_c', 'v_w_o', 'v_mlp_norm', 'v_w_up', 'v_w_down']
TWIN_OUTPUTS = ['loss', 'grad_x', 'grad_meta_tokens', 'grad_attn_norm', 'grad_w_in', 'grad_conv_w', 'grad_q_lat_norm', 'grad_kv_lat_norm', 'grad_w_uq', 'grad_w_ukv', 'grad_q_norm', 'grad_k_norm', 'grad_pool_w', 'grad_pool_scale', 'grad_w_branch_a', 'grad_w_branch_b', 'grad_w_branch_c', 'grad_w_o', 'grad_mlp_norm', 'grad_w_up', 'grad_w_down', 'delta_meta_tokens', 'delta_attn_norm', 'delta_w_in', 'delta_conv_w', 'delta_q_lat_norm', 'delta_kv_lat_norm', 'delta_w_uq', 'delta_w_ukv', 'delta_q_norm', 'delta_k_norm', 'delta_pool_w', 'delta_pool_scale', 'delta_w_branch_a', 'delta_w_branch_b', 'delta_w_branch_c', 'delta_w_o', 'delta_mlp_norm', 'delta_w_up', 'delta_w_down', 'new_m_meta_tokens', 'new_m_attn_norm', 'new_m_w_in', 'new_m_conv_w', 'new_m_q_lat_norm', 'new_m_kv_lat_norm', 'new_m_w_uq', 'new_m_w_ukv', 'new_m_q_norm', 'new_m_k_norm', 'new_m_pool_w', 'new_m_pool_scale', 'new_m_w_branch_a', 'new_m_w_branch_b', 'new_m_w_branch_c', 'new_m_w_o', 'new_m_mlp_norm', 'new_m_w_up', 'new_m_w_down', 'new_v_meta_tokens', 'new_v_attn_norm', 'new_v_w_in', 'new_v_conv_w', 'new_v_q_lat_norm', 'new_v_kv_lat_norm', 'new_v_w_uq', 'new_v_w_ukv', 'new_v_q_norm', 'new_v_k_norm', 'new_v_pool_w', 'new_v_pool_scale', 'new_v_w_branch_a', 'new_v_w_branch_b', 'new_v_w_branch_c', 'new_v_w_o', 'new_v_mlp_norm', 'new_v_w_up', 'new_v_w_down']
TWIN_LEAF_KINDS = {'loss': 'loss', 'grad_x': 'grad_x', 'grad_meta_tokens': 'grad_w', 'grad_attn_norm': 'grad_w', 'grad_w_in': 'grad_w', 'grad_conv_w': 'grad_w', 'grad_q_lat_norm': 'grad_w', 'grad_kv_lat_norm': 'grad_w', 'grad_w_uq': 'grad_w', 'grad_w_ukv': 'grad_w', 'grad_q_norm': 'grad_w', 'grad_k_norm': 'grad_w', 'grad_pool_w': 'grad_w', 'grad_pool_scale': 'grad_w', 'grad_w_branch_a': 'grad_w', 'grad_w_branch_b': 'grad_w', 'grad_w_branch_c': 'grad_w', 'grad_w_o': 'grad_w', 'grad_mlp_norm': 'grad_w', 'grad_w_up': 'grad_w', 'grad_w_down': 'grad_w', 'delta_meta_tokens': 'delta_w', 'delta_attn_norm': 'delta_w', 'delta_w_in': 'delta_w', 'delta_conv_w': 'delta_w', 'delta_q_lat_norm': 'delta_w', 'delta_kv_lat_norm': 'delta_w', 'delta_w_uq': 'delta_w', 'delta_w_ukv': 'delta_w', 'delta_q_norm': 'delta_w', 'delta_k_norm': 'delta_w', 'delta_pool_w': 'delta_w', 'delta_pool_scale': 'delta_w', 'delta_w_branch_a': 'delta_w', 'delta_w_branch_b': 'delta_w', 'delta_w_branch_c': 'delta_w', 'delta_w_o': 'delta_w', 'delta_mlp_norm': 'delta_w', 'delta_w_up': 'delta_w', 'delta_w_down': 'delta_w', 'new_m_meta_tokens': 'new_m', 'new_m_attn_norm': 'new_m', 'new_m_w_in': 'new_m', 'new_m_conv_w': 'new_m', 'new_m_q_lat_norm': 'new_m', 'new_m_kv_lat_norm': 'new_m', 'new_m_w_uq': 'new_m', 'new_m_w_ukv': 'new_m', 'new_m_q_norm': 'new_m', 'new_m_k_norm': 'new_m', 'new_m_pool_w': 'new_m', 'new_m_pool_scale': 'new_m', 'new_m_w_branch_a': 'new_m', 'new_m_w_branch_b': 'new_m', 'new_m_w_branch_c': 'new_m', 'new_m_w_o': 'new_m', 'new_m_mlp_norm': 'new_m', 'new_m_w_up': 'new_m', 'new_m_w_down': 'new_m', 'new_v_meta_tokens': 'new_v', 'new_v_attn_norm': 'new_v', 'new_v_w_in': 'new_v', 'new_v_conv_w': 'new_v', 'new_v_q_lat_norm': 'new_v', 'new_v_kv_lat_norm': 'new_v', 'new_v_w_uq': 'new_v', 'new_v_w_ukv': 'new_v', 'new_v_q_norm': 'new_v', 'new_v_k_norm': 'new_v', 'new_v_pool_w': 'new_v', 'new_v_pool_scale': 'new_v', 'new_v_w_branch_a': 'new_v', 'new_v_w_branch_b': 'new_v', 'new_v_w_branch_c': 'new_v', 'new_v_w_o': 'new_v', 'new_v_mlp_norm': 'new_v', 'new_v_w_up': 'new_v', 'new_v_w_down': 'new_v'}


def _forward(args):
    return _fwd_reference(*[args[k] for k in FWD_PARAMS])


def _output_shape():
    def fwd():
        inp = _fwd_setup_inputs(0)
        return _fwd_reference(*[inp[k] for k in FWD_PARAMS])
    out = _jax.eval_shape(fwd)
    return out.shape, out.dtype

N_MICROBATCH = 1
ADAM_LR = 0.001
ADAM_B1 = 0.9
ADAM_B2 = 0.999
ADAM_EPS = 1e-08
ADAM_WD = 0.01
ADAM_STEP = 10
PER_EXAMPLE_BATCH_AXIS = {'x': 0, 'loss_target': 0}
SHARED_INPUTS = []
_WEIGHT_DTYPES = {'meta_tokens': _jnp.float32, 'attn_norm': _jnp.float32, 'w_in': _jnp.float32, 'conv_w': _jnp.float32, 'q_lat_norm': _jnp.float32, 'kv_lat_norm': _jnp.float32, 'w_uq': _jnp.float32, 'w_ukv': _jnp.float32, 'q_norm': _jnp.float32, 'k_norm': _jnp.float32, 'pool_w': _jnp.float32, 'pool_scale': _jnp.float32, 'w_branch_a': _jnp.float32, 'w_branch_b': _jnp.float32, 'w_branch_c': _jnp.float32, 'w_o': _jnp.float32, 'mlp_norm': _jnp.float32, 'w_up': _jnp.float32, 'w_down': _jnp.float32}
MOMENT_SCALE = {'meta_tokens': 4.889289e-02, 'attn_norm': 1.877739e+01, 'w_in': 7.294104e-01, 'conv_w': 5.474219e+00, 'q_lat_norm': 9.980481e-02, 'kv_lat_norm': 3.304506e+00, 'w_uq': 4.037638e-02, 'w_ukv': 1.038755e+00, 'q_norm': 3.035157e-01, 'k_norm': 3.032183e-01, 'pool_w': 6.950621e-01, 'pool_scale': 7.186242e+00, 'w_branch_a': 3.856024e-01, 'w_branch_b': 1.463492e+00, 'w_branch_c': 4.385934e-01, 'w_o': 1.432097e+00, 'mlp_norm': 4.778747e+01, 'w_up': 1.621665e+00, 'w_down': 7.148456e+00}


def _to_microbatches(a, axis):
    t = _jnp.moveaxis(a, axis, 0)
    t = t.reshape((N_MICROBATCH, t.shape[0] // N_MICROBATCH) + t.shape[1:])
    return _jnp.moveaxis(t, 1, axis + 1)


def setup_inputs(seed: int = 0) -> dict:
    inp = _fwd_setup_inputs(seed)
    key = _jax.random.fold_in(_jax.random.key(seed), 7919)
    shape, _ = _output_shape()
    out = dict(inp)
    out["loss_target"] = _jax.random.normal(_jax.random.fold_in(key, 0), shape, _jnp.float32)
    for i, name in enumerate(TWIN_WEIGHTS):
        w = inp[name].astype(_jnp.float32)
        if MOMENT_SCALE is None:
            s = _jnp.sqrt(_jnp.mean(_jnp.square(w)) + 1e-30)
        else:
            s = MOMENT_SCALE[name]
        km, kv = _jax.random.split(_jax.random.fold_in(key, i + 1))
        out[name] = w
        out["m_" + name] = s * _jax.random.normal(km, w.shape, _jnp.float32)
        out["v_" + name] = (s * s) * _jax.random.uniform(kv, w.shape, _jnp.float32, 0.5, 1.5)
    if N_MICROBATCH > 1:
        for name, axis in PER_EXAMPLE_BATCH_AXIS.items():
            out[name] = _to_microbatches(out[name], axis)
    return {'x': out['x'], 'meta_tokens': out['meta_tokens'], 'attn_norm': out['attn_norm'], 'w_in': out['w_in'], 'conv_w': out['conv_w'], 'q_lat_norm': out['q_lat_norm'], 'kv_lat_norm': out['kv_lat_norm'], 'w_uq': out['w_uq'], 'w_ukv': out['w_ukv'], 'q_norm': out['q_norm'], 'k_norm': out['k_norm'], 'pool_w': out['pool_w'], 'pool_scale': out['pool_scale'], 'w_branch_a': out['w_branch_a'], 'w_branch_b': out['w_branch_b'], 'w_branch_c': out['w_branch_c'], 'w_o': out['w_o'], 'mlp_norm': out['mlp_norm'], 'w_up': out['w_up'], 'w_down': out['w_down'], 'loss_target': out['loss_target'], 'm_meta_tokens': out['m_meta_tokens'], 'm_attn_norm': out['m_attn_norm'], 'm_w_in': out['m_w_in'], 'm_conv_w': out['m_conv_w'], 'm_q_lat_norm': out['m_q_lat_norm'], 'm_kv_lat_norm': out['m_kv_lat_norm'], 'm_w_uq': out['m_w_uq'], 'm_w_ukv': out['m_w_ukv'], 'm_q_norm': out['m_q_norm'], 'm_k_norm': out['m_k_norm'], 'm_pool_w': out['m_pool_w'], 'm_pool_scale': out['m_pool_scale'], 'm_w_branch_a': out['m_w_branch_a'], 'm_w_branch_b': out['m_w_branch_b'], 'm_w_branch_c': out['m_w_branch_c'], 'm_w_o': out['m_w_o'], 'm_mlp_norm': out['m_mlp_norm'], 'm_w_up': out['m_w_up'], 'm_w_down': out['m_w_down'], 'v_meta_tokens': out['v_meta_tokens'], 'v_attn_norm': out['v_attn_norm'], 'v_w_in': out['v_w_in'], 'v_conv_w': out['v_conv_w'], 'v_q_lat_norm': out['v_q_lat_norm'], 'v_kv_lat_norm': out['v_kv_lat_norm'], 'v_w_uq': out['v_w_uq'], 'v_w_ukv': out['v_w_ukv'], 'v_q_norm': out['v_q_norm'], 'v_k_norm': out['v_k_norm'], 'v_pool_w': out['v_pool_w'], 'v_pool_scale': out['v_pool_scale'], 'v_w_branch_a': out['v_w_branch_a'], 'v_w_branch_b': out['v_w_branch_b'], 'v_w_branch_c': out['v_w_branch_c'], 'v_w_o': out['v_w_o'], 'v_mlp_norm': out['v_mlp_norm'], 'v_w_up': out['v_w_up'], 'v_w_down': out['v_w_down']}


def _loss(weights, diff, rest, loss_target):
    with _jax.named_scope("forward"):
        args = {**rest, TWIN_DIFF_INPUT: diff, **{k: w.astype(_WEIGHT_DTYPES[k]) for k, w in weights.items()}}
        y = _forward(args)
    with _jax.named_scope("loss_head"):
        err = _jnp.square(y.astype(_jnp.float32) - loss_target)
        return 0.5 * _jnp.sum(_jnp.mean(err, axis=-1)) if err.ndim else 0.5 * err


def _adamw(w, g, m, v):
    m = ADAM_B1 * m + (1.0 - ADAM_B1) * g
    v = ADAM_B2 * v + (1.0 - ADAM_B2) * _jnp.square(g)
    m_hat = m / (1.0 - ADAM_B1 ** ADAM_STEP)
    v_hat = v / (1.0 - ADAM_B2 ** ADAM_STEP)
    delta = -ADAM_LR * (m_hat / (_jnp.sqrt(v_hat) + ADAM_EPS) + ADAM_WD * w)
    return delta, m, v


def reference(x, meta_tokens, attn_norm, w_in, conv_w, q_lat_norm, kv_lat_norm, w_uq, w_ukv, q_norm, k_norm, pool_w, pool_scale, w_branch_a, w_branch_b, w_branch_c, w_o, mlp_norm, w_up, w_down, loss_target, m_meta_tokens, m_attn_norm, m_w_in, m_conv_w, m_q_lat_norm, m_kv_lat_norm, m_w_uq, m_w_ukv, m_q_norm, m_k_norm, m_pool_w, m_pool_scale, m_w_branch_a, m_w_branch_b, m_w_branch_c, m_w_o, m_mlp_norm, m_w_up, m_w_down, v_meta_tokens, v_attn_norm, v_w_in, v_conv_w, v_q_lat_norm, v_kv_lat_norm, v_w_uq, v_w_ukv, v_q_norm, v_k_norm, v_pool_w, v_pool_scale, v_w_branch_a, v_w_branch_b, v_w_branch_c, v_w_o, v_mlp_norm, v_w_up, v_w_down):
    given = dict(x=x, meta_tokens=meta_tokens, attn_norm=attn_norm, w_in=w_in, conv_w=conv_w, q_lat_norm=q_lat_norm, kv_lat_norm=kv_lat_norm, w_uq=w_uq, w_ukv=w_ukv, q_norm=q_norm, k_norm=k_norm, pool_w=pool_w, pool_scale=pool_scale, w_branch_a=w_branch_a, w_branch_b=w_branch_b, w_branch_c=w_branch_c, w_o=w_o, mlp_norm=mlp_norm, w_up=w_up, w_down=w_down, loss_target=loss_target, m_meta_tokens=m_meta_tokens, m_attn_norm=m_attn_norm, m_w_in=m_w_in, m_conv_w=m_conv_w, m_q_lat_norm=m_q_lat_norm, m_kv_lat_norm=m_kv_lat_norm, m_w_uq=m_w_uq, m_w_ukv=m_w_ukv, m_q_norm=m_q_norm, m_k_norm=m_k_norm, m_pool_w=m_pool_w, m_pool_scale=m_pool_scale, m_w_branch_a=m_w_branch_a, m_w_branch_b=m_w_branch_b, m_w_branch_c=m_w_branch_c, m_w_o=m_w_o, m_mlp_norm=m_mlp_norm, m_w_up=m_w_up, m_w_down=m_w_down, v_meta_tokens=v_meta_tokens, v_attn_norm=v_attn_norm, v_w_in=v_w_in, v_conv_w=v_conv_w, v_q_lat_norm=v_q_lat_norm, v_kv_lat_norm=v_kv_lat_norm, v_w_uq=v_w_uq, v_w_ukv=v_w_ukv, v_q_norm=v_q_norm, v_k_norm=v_k_norm, v_pool_w=v_pool_w, v_pool_scale=v_pool_scale, v_w_branch_a=v_w_branch_a, v_w_branch_b=v_w_branch_b, v_w_branch_c=v_w_branch_c, v_w_o=v_w_o, v_mlp_norm=v_mlp_norm, v_w_up=v_w_up, v_w_down=v_w_down)
    weights = {n: given[n] for n in TWIN_WEIGHTS}
    shared = {n: given[n] for n in SHARED_INPUTS}
    per_example = {n: given[n] for n in ['x']}
    grad_fn = _jax.value_and_grad(_loss, argnums=(0, 1))

    def one_microbatch(ex, loss_target):
        ex = dict(ex)
        diff = ex.pop(TWIN_DIFF_INPUT)
        return grad_fn(weights, diff, {**shared, **ex}, loss_target)

    if N_MICROBATCH == 1:
        loss, (grad_w, grad_x) = one_microbatch(per_example, given["loss_target"])
    else:
        def body(carry, xs):
            loss_sum, grad_sum = carry
            l_k, (gw_k, gx_k) = one_microbatch(xs[0], xs[1])
            with _jax.named_scope("update"):
                return (loss_sum + l_k, _jax.tree.map(_jnp.add, grad_sum, gw_k)), gx_k

        init = (_jnp.zeros((), _jnp.float32), _jax.tree.map(_jnp.zeros_like, weights))
        (loss, grad_w), grad_x = _jax.lax.scan(body, init, (per_example, given["loss_target"]))
    with _jax.named_scope("update"):
        delta_w, new_m, new_v = {}, {}, {}
        for n in TWIN_WEIGHTS:
            delta_w[n], new_m[n], new_v[n] = _adamw(weights[n], grad_w[n], given["m_" + n], given["v_" + n])
    return (loss, grad_x, *[grad_w[n] for n in TWIN_WEIGHTS], *[delta_w[n] for n in TWIN_WEIGHTS],
            *[new_m[n] for n in TWIN_WEIGHTS], *[new_v[n] for n in TWIN_WEIGHTS])
```

```python
import functools
import math

import jax
import jax.numpy as jnp
from jax import lax
from jax.experimental import pallas as pl
from jax.experimental.pallas import tpu as pltpu

F32 = jnp.float32
BF16 = jnp.bfloat16
MESH = pl.DeviceIdType.MESH

EPS = 1e-6
N_META = 16
QK_NOPE = 128
QK_ROPE = 64
QK_HEAD = QK_NOPE + QK_ROPE
V_HEAD = 128
HEAD_PAD = 256
Q_LORA = 512
KV_LORA = 512
ROPE_THETA = 10000.0
POOL_WINDOWS = (2, 4, 8, 16)
HALO = 16
LANES = 128
ADAM_LR = 0.001
ADAM_B1 = 0.9
ADAM_B2 = 0.999
ADAM_EPS = 1e-08
ADAM_WD = 0.01
ADAM_STEP = 10
VMEM_LIMIT = 52 * 1024 * 1024
NEG = -1e30


def _tile(n, target, mult=LANES):
    best = None
    for t in range(mult, min(n, target) + 1, mult):
        if n % t == 0:
            best = t
    return n if best is None else best


def _params(sem=None):
    return pltpu.CompilerParams(dimension_semantics=sem, vmem_limit_bytes=VMEM_LIMIT)


def _mm(a, b, *, name, ta=False, tb=False, add=None, aux=None, epi=None, out_dtype=F32,
        tm=704, tn=1024, tk=None):
    if ta:
        K, M = a.shape
    else:
        M, K = a.shape
    if tb:
        N, kb = b.shape
    else:
        kb, N = b.shape
    assert K == kb, (a.shape, b.shape, ta, tb)
    tm = _tile(M, tm, LANES if ta else 16)
    tn = _tile(N, tn, LANES)
    tk = K if tk is None else _tile(K, tk, LANES if (not ta or tb) else 16)
    nk = K // tk
    grid = (M // tm, N // tn, nk)

    a_spec = pl.BlockSpec((tk, tm), lambda i, j, k: (k, i)) if ta else pl.BlockSpec((tm, tk), lambda i, j, k: (i, k))
    b_spec = pl.BlockSpec((tn, tk), lambda i, j, k: (j, k)) if tb else pl.BlockSpec((tk, tn), lambda i, j, k: (k, j))
    o_spec = pl.BlockSpec((tm, tn), lambda i, j, k: (i, j))
    in_specs = [a_spec, b_spec]
    operands = [a, b]
    if add is not None:
        in_specs.append(o_spec)
        operands.append(add)
    if aux is not None:
        in_specs.append(o_spec)
        operands.append(aux)
    if epi == "relu2":
        out_shape = (jax.ShapeDtypeStruct((M, N), BF16), jax.ShapeDtypeStruct((M, N), BF16))
        out_specs = (o_spec, o_spec)
    else:
        out_shape = jax.ShapeDtypeStruct((M, N), out_dtype)
        out_specs = o_spec
    dims = (((0 if ta else 1,), (1 if tb else 0,)), ((), ()))
    has_add, has_aux = add is not None, aux is not None

    def body(*refs):
        a_ref, b_ref = refs[0], refs[1]
        pos = 2
        add_ref = aux_ref = None
        if has_add:
            add_ref = refs[pos]
            pos += 1
        if has_aux:
            aux_ref = refs[pos]
            pos += 1
        n_out = 2 if epi == "relu2" else 1
        out_refs = refs[pos:pos + n_out]
        acc_ref = refs[pos + n_out] if nk > 1 else None

        part = lax.dot_general(a_ref[...].astype(BF16), b_ref[...].astype(BF16), dims,
                               preferred_element_type=F32)

        def finish(acc):
            if has_add:
                acc = acc + add_ref[...].astype(F32)
            if epi == "relu2":
                r = jnp.maximum(acc, 0.0)
                out_refs[0][...] = acc.astype(BF16)
                out_refs[1][...] = (r * r).astype(BF16)
            elif epi == "drelu2":
                u = aux_ref[...].astype(F32)
                out_refs[0][...] = (acc * (2.0 * jnp.maximum(u, 0.0))).astype(out_dtype)
            else:
                out_refs[0][...] = acc.astype(out_dtype)

        if nk == 1:
            finish(part)
        else:
            k = pl.program_id(2)

            @pl.when(k == 0)
            def _():
                acc_ref[...] = part

            @pl.when(k > 0)
            def _():
                acc_ref[...] += part

            @pl.when(k == nk - 1)
            def _():
                finish(acc_ref[...])

    scratch = [pltpu.VMEM((tm, tn), F32)] if nk > 1 else []
    return pl.pallas_call(
        body, name=name, grid=grid, in_specs=in_specs, out_specs=out_specs, out_shape=out_shape,
        scratch_shapes=scratch, compiler_params=_params(("parallel", "parallel", "arbitrary")),
    )(*operands)


def _rms_fwd(x, g, *, name, width=None, seg=0, tm=384):
    T = x.shape[0]
    width = x.shape[1] if width is None else width
    tm = _tile(T, tm, 16)

    def body(x_ref, g_ref, o_ref):
        xf = x_ref[...].astype(F32)
        r = lax.rsqrt(jnp.mean(xf * xf, axis=-1, keepdims=True) + EPS)
        o_ref[...] = (xf * r * g_ref[...]).astype(BF16)

    return pl.pallas_call(
        body, name=name, grid=(T // tm,),
        in_specs=[pl.BlockSpec((tm, width), lambda i: (i, seg)), pl.BlockSpec((1, width), lambda i: (0, 0))],
        out_specs=pl.BlockSpec((tm, width), lambda i: (i, 0)),
        out_shape=jax.ShapeDtypeStruct((T, width), BF16),
        compiler_params=_params(("parallel",)),
    )(x, g)


def _rms_bwd(dy, x, g, *, name, width=None, seg=0, res=None, out_dtype=F32, tm=384):
    T = x.shape[0]
    width = x.shape[1] if width is None else width
    tm = _tile(T, tm, 16)
    has_res = res is not None

    def body(*refs):
        dy_ref, x_ref, g_ref = refs[:3]
        res_ref = refs[3] if has_res else None
        dx_ref, dg_ref = refs[-2:]
        xf = x_ref[...].astype(F32)
        dyf = dy_ref[...].astype(F32)
        r = lax.rsqrt(jnp.mean(xf * xf, axis=-1, keepdims=True) + EPS)
        xhat = xf * r
        dyh = dyf * g_ref[...]
        dx = r * (dyh - xhat * jnp.mean(dyh * xhat, axis=-1, keepdims=True))
        if has_res:
            dx = dx + res_ref[...].astype(F32)
        dx_ref[...] = dx.astype(out_dtype)
        part = jnp.sum(dyf * xhat, axis=0, keepdims=True)

        @pl.when(pl.program_id(0) == 0)
        def _():
            dg_ref[...] = part

        @pl.when(pl.program_id(0) > 0)
        def _():
            dg_ref[...] += part

    row = pl.BlockSpec((tm, width), lambda i: (i, 0))
    in_specs = [row, pl.BlockSpec((tm, width), lambda i: (i, seg)), pl.BlockSpec((1, width), lambda i: (0, 0))]
    operands = [dy, x, g]
    if has_res:
        in_specs.append(row)
        operands.append(res)
    return pl.pallas_call(
        body, name=name, grid=(T // tm,), in_specs=in_specs,
        out_specs=(row, pl.BlockSpec((1, width), lambda i: (0, 0))),
        out_shape=(jax.ShapeDtypeStruct((T, width), out_dtype), jax.ShapeDtypeStruct((1, width), F32)),
        compiler_params=_params(("arbitrary",)),
    )(*operands)


def _down(ext, k):
    return pltpu.roll(ext, k, 0)


def _up(ext, k):
    return pltpu.roll(ext, ext.shape[0] - k, 0)


def _pre_halo(ref, r, R):
    start = pl.multiple_of(jnp.maximum(r * R - HALO, 0), 8)
    keep = (r > 0).astype(F32)
    return ref[pl.ds(start, HALO), :].astype(F32) * keep


def _post_halo(ref, r, R, n_chunks):
    start = pl.multiple_of(jnp.minimum(r * R + R, (n_chunks - 1) * R + R - HALO), 8)
    keep = (r < n_chunks - 1).astype(F32)
    return ref[pl.ds(start, HALO), :].astype(F32) * keep


def _chunk(ref, r, R):
    return ref[pl.ds(pl.multiple_of(r * R, 8), R), :].astype(F32)


def _conv_fwd(rest, conv_w, *, name, dc, tc=128, rows=1056):
    T = rest.shape[0]
    tc = _tile(dc, tc)
    nb = dc // tc
    R = _tile(T, rows, 16)
    n_chunks = T // R

    def body(u_ref, b_ref, c_ref, w_ref, y_ref):
        w0, w1, w2 = w_ref[0:1, :], w_ref[1:2, :], w_ref[2:3, :]

        def chunk(r, carry):
            cu = _chunk(c_ref, r, R) * _chunk(u_ref, r, R)
            ext = jnp.concatenate([_pre_halo(c_ref, r, R) * _pre_halo(u_ref, r, R), cu], axis=0)
            conv = w0 * _down(ext, 2)[HALO:] + w1 * _down(ext, 1)[HALO:] + w2 * cu
            y_ref[pl.ds(pl.multiple_of(r * R, 8), R), :] = (_chunk(b_ref, r, R) * conv).astype(BF16)
            return carry

        lax.fori_loop(0, n_chunks, chunk, 0)

    col = lambda off: pl.BlockSpec((T, tc), lambda j: (0, off * nb + j))
    return pl.pallas_call(
        body, name=name, grid=(nb,),
        in_specs=[col(0), col(1), col(2), pl.BlockSpec((3, tc), lambda j: (0, j))],
        out_specs=pl.BlockSpec((T, tc), lambda j: (0, j)),
        out_shape=jax.ShapeDtypeStruct((T, dc), BF16),
        compiler_params=_params(("parallel",)),
    )(rest, rest, rest, conv_w)


def _conv_bwd(rest, conv_w, dy, *, name, dc, tc=128, rows=1056):
    T = rest.shape[0]
    tc = _tile(dc, tc)
    nb = dc // tc
    R = _tile(T, rows, 16)
    n_chunks = T // R

    def body(u_ref, b_ref, c_ref, w_ref, dy_ref, du_ref, db_ref, dc_ref, dw_ref):
        w0, w1, w2 = w_ref[0:1, :], w_ref[1:2, :], w_ref[2:3, :]

        def chunk(r, carry):
            a0, a1, a2 = carry
            u, b, c = _chunk(u_ref, r, R), _chunk(b_ref, r, R), _chunk(c_ref, r, R)
            dy_c = _chunk(dy_ref, r, R)
            cu = c * u
            ext = jnp.concatenate([_pre_halo(c_ref, r, R) * _pre_halo(u_ref, r, R), cu], axis=0)
            cu1, cu2 = _down(ext, 1)[HALO:], _down(ext, 2)[HALO:]
            conv = w0 * cu2 + w1 * cu1 + w2 * cu
            dconv = dy_c * b
            dext = jnp.concatenate(
                [dconv, _post_halo(dy_ref, r, R, n_chunks) * _post_halo(b_ref, r, R, n_chunks)], axis=0)
            dcu = w2 * dconv + w1 * _up(dext, 1)[:R] + w0 * _up(dext, 2)[:R]
            rows_at = pl.ds(pl.multiple_of(r * R, 8), R)
            db_ref[rows_at, :] = (dy_c * conv).astype(BF16)
            du_ref[rows_at, :] = (dcu * c).astype(BF16)
            dc_ref[rows_at, :] = (dcu * u).astype(BF16)
            return (a0 + jnp.sum(dconv * cu2, axis=0, keepdims=True),
                    a1 + jnp.sum(dconv * cu1, axis=0, keepdims=True),
                    a2 + jnp.sum(dconv * cu, axis=0, keepdims=True))

        zero = jnp.zeros((1, tc), F32)
        a0, a1, a2 = lax.fori_loop(0, n_chunks, chunk, (zero, zero, zero))
        dw_ref[0:1, :] = a0
        dw_ref[1:2, :] = a1
        dw_ref[2:3, :] = a2

    col = lambda off: pl.BlockSpec((T, tc), lambda j: (0, off * nb + j))
    own = pl.BlockSpec((T, tc), lambda j: (0, j))
    return pl.pallas_call(
        body, name=name, grid=(nb,),
        in_specs=[col(0), col(1), col(2), pl.BlockSpec((3, tc), lambda j: (0, j)), own],
        out_specs=(own, own, own, pl.BlockSpec((3, tc), lambda j: (0, j))),
        out_shape=(jax.ShapeDtypeStruct((T, dc), BF16),) * 3 + (jax.ShapeDtypeStruct((3, dc), F32),),
        compiler_params=_params(("parallel",)),
    )(rest, rest, rest, conv_w, dy)


def _window_count(r, R, n_rows, w, first_row_offset):
    t = lax.broadcasted_iota(jnp.int32, (n_rows, 1), 0) + (r * R + first_row_offset)
    return jnp.minimum(t + 1, w).astype(F32)


def _pool_fwd(rest, pool_w, pool_scale, *, name, seg0, pg, rows=1056):
    T = rest.shape[0]
    R = _tile(T, rows, 16)
    n_chunks = T // R
    n_groups = len(POOL_WINDOWS)

    def body(x_ref, w_ref, s_ref, y_ref):
        def run(window):
            def chunk(r, carry):
                g = _chunk(x_ref, r, R)
                s = jnp.concatenate([_pre_halo(x_ref, r, R), g], axis=0)
                k = 1
                while k < window:
                    s = s + _down(s, k)
                    k *= 2
                pooled = s[HALO:] / _window_count(r, R, R, window, 0) - g
                mixed = jnp.dot(pooled.astype(BF16), w_ref[0], preferred_element_type=F32)
                y_ref[pl.ds(pl.multiple_of(r * R, 8), R), :] = (mixed * s_ref[...]).astype(BF16)
                return carry

            lax.fori_loop(0, n_chunks, chunk, 0)

        for gi, window in enumerate(POOL_WINDOWS):
            pl.when(pl.program_id(0) == gi)(functools.partial(run, window))

    return pl.pallas_call(
        body, name=name, grid=(n_groups,),
        in_specs=[pl.BlockSpec((T, pg), lambda g: (0, seg0 + g)),
                  pl.BlockSpec((1, pg, pg), lambda g: (g, 0, 0)),
                  pl.BlockSpec((1, pg), lambda g: (0, g))],
        out_specs=pl.BlockSpec((T, pg), lambda g: (0, g)),
        out_shape=jax.ShapeDtypeStruct((T, n_groups * pg), BF16),
        compiler_params=_params(("parallel",)),
    )(rest, pool_w, pool_scale)


def _pool_bwd(rest, pool_w, pool_scale, dy, *, name, seg0, pg, rows=1056):
    T = rest.shape[0]
    R = _tile(T, rows, 16)
    n_chunks = T // R
    n_groups = len(POOL_WINDOWS)

    def body(x_ref, w_ref, s_ref, dy_ref, dx_ref, dw_ref, ds_ref):
        def run(window):
            def chunk(r, carry):
                dw_acc, ds_acc = carry
                g = _chunk(x_ref, r, R)
                s = jnp.concatenate([_pre_halo(x_ref, r, R), g], axis=0)
                k = 1
                while k < window:
                    s = s + _down(s, k)
                    k *= 2
                pooled = (s[HALO:] / _window_count(r, R, R, window, 0) - g).astype(BF16)
                mixed = jnp.dot(pooled, w_ref[0], preferred_element_type=F32)
                dy_c = _chunk(dy_ref, r, R)
                dm_ext = (jnp.concatenate([dy_c, _post_halo(dy_ref, r, R, n_chunks)], axis=0)
                          * s_ref[...]).astype(BF16)
                dpool_ext = lax.dot_general(dm_ext, w_ref[0], (((1,), (1,)), ((), ())),
                                            preferred_element_type=F32)
                a = dpool_ext / _window_count(r, R, R + HALO, window, 0)
                k = 1
                while k < window:
                    a = a + _up(a, k)
                    k *= 2
                dx_ref[pl.ds(pl.multiple_of(r * R, 8), R), :] = (a[:R] - dpool_ext[:R]).astype(BF16)
                dw_acc = dw_acc + lax.dot_general(pooled, dm_ext[:R], (((0,), (0,)), ((), ())),
                                                  preferred_element_type=F32)
                ds_acc = ds_acc + jnp.sum(dy_c * mixed, axis=0, keepdims=True)
                return dw_acc, ds_acc

            dw_acc, ds_acc = lax.fori_loop(0, n_chunks, chunk,
                                           (jnp.zeros((pg, pg), F32), jnp.zeros((1, pg), F32)))
            dw_ref[0] = dw_acc
            ds_ref[...] = ds_acc

        for gi, window in enumerate(POOL_WINDOWS):
            pl.when(pl.program_id(0) == gi)(functools.partial(run, window))

    own = pl.BlockSpec((T, pg), lambda g: (0, g))
    return pl.pallas_call(
        body, name=name, grid=(n_groups,),
        in_specs=[pl.BlockSpec((T, pg), lambda g: (0, seg0 + g)),
                  pl.BlockSpec((1, pg, pg), lambda g: (g, 0, 0)),
                  pl.BlockSpec((1, pg), lambda g: (0, g)), own],
        out_specs=(own, pl.BlockSpec((1, pg, pg), lambda g: (g, 0, 0)), pl.BlockSpec((1, pg), lambda g: (0, g))),
        out_shape=(jax.ShapeDtypeStruct((T, n_groups * pg), BF16),
                   jax.ShapeDtypeStruct((n_groups, pg, pg), F32),
                   jax.ShapeDtypeStruct((1, n_groups * pg), F32)),
        compiler_params=_params(("parallel",)),
    )(rest, pool_w, pool_scale, dy)


def _rope(r, cos_t, sin_t):
    return r * cos_t + pltpu.roll(r, LANES // 2, 1) * sin_t


def _rope_t(d, cos_t, sin_t):
    return d * cos_t + pltpu.roll(d * sin_t, LANES // 2, 1)


def _qk_fwd(q_raw, k_nope, rest, cos_t, sin_t, q_norm, k_norm, *, name, heads, kr_seg, tm=192):
    T = q_raw.shape[0]
    tm = _tile(T, tm, 16)

    def body(q_ref, kn_ref, kr_ref, c_ref, s_ref, gq_ref, gk_ref, qo_ref, ko_ref):
        cos_b, sin_b = c_ref[...], s_ref[...]
        kr = kr_ref[:, 0:LANES]
        kr_ss = jnp.sum(kr * kr, axis=-1, keepdims=True)
        gq, gk = gq_ref[...], gk_ref[...]
        for h in range(heads):
            lo = h * HEAD_PAD
            q = q_ref[:, lo:lo + HEAD_PAD]
            rq = lax.rsqrt(jnp.sum(q * q, axis=-1, keepdims=True) / QK_HEAD + EPS)
            qn = q * rq * gq
            qo_ref[:, lo:lo + LANES] = qn[:, :LANES].astype(BF16)
            qo_ref[:, lo + LANES:lo + HEAD_PAD] = _rope(qn[:, LANES:], cos_b, sin_b).astype(BF16)
            kn = kn_ref[:, h * LANES:(h + 1) * LANES]
            rk = lax.rsqrt((jnp.sum(kn * kn, axis=-1, keepdims=True) + kr_ss) / QK_HEAD + EPS)
            ko_ref[:, lo:lo + LANES] = (kn * rk * gk[:, :LANES]).astype(BF16)
            ko_ref[:, lo + LANES:lo + HEAD_PAD] = _rope(kr * rk * gk[:, LANES:], cos_b, sin_b).astype(BF16)

    wq, wk = heads * HEAD_PAD, heads * LANES
    return pl.pallas_call(
        body, name=name, grid=(T // tm,),
        in_specs=[pl.BlockSpec((tm, wq), lambda i: (i, 0)), pl.BlockSpec((tm, wk), lambda i: (i, 0)),
                  pl.BlockSpec((tm, HEAD_PAD), lambda i: (i, kr_seg)),
                  pl.BlockSpec((tm, LANES), lambda i: (i, 0)), pl.BlockSpec((tm, LANES), lambda i: (i, 0)),
                  pl.BlockSpec((1, HEAD_PAD), lambda i: (0, 0)), pl.BlockSpec((1, HEAD_PAD), lambda i: (0, 0))],
        out_specs=(pl.BlockSpec((tm, wq), lambda i: (i, 0)), pl.BlockSpec((tm, wq), lambda i: (i, 0))),
        out_shape=(jax.ShapeDtypeStruct((T, wq), BF16), jax.ShapeDtypeStruct((T, wq), BF16)),
        compiler_params=_params(("parallel",)),
    )(q_raw, k_nope, rest, cos_t, sin_t, q_norm, k_norm)


def _qk_bwd(dq, dk, q_raw, k_nope, rest, cos_t, sin_t, q_norm, k_norm, *, name, heads, kr_seg, tm=128):
    T = q_raw.shape[0]
    tm = _tile(T, tm, 16)

    def body(dq_ref, dk_ref, q_ref, kn_ref, kr_ref, c_ref, s_ref, gq_ref, gk_ref,
             dqr_ref, dkn_ref, dkr_ref, dgq_ref, dgk_ref):
        cos_b, sin_b = c_ref[...], s_ref[...]
        kr = kr_ref[:, 0:LANES]
        kr_ss = jnp.sum(kr * kr, axis=-1, keepdims=True)
        gq, gk = gq_ref[...], gk_ref[...]
        dgq = jnp.zeros((1, HEAD_PAD), F32)
        dgk_n = jnp.zeros((1, LANES), F32)
        dgk_r = jnp.zeros((1, LANES), F32)
        dkr = jnp.zeros((tm, LANES), F32)
        for h in range(heads):
            lo = h * HEAD_PAD
            q = q_ref[:, lo:lo + HEAD_PAD]
            rq = lax.rsqrt(jnp.sum(q * q, axis=-1, keepdims=True) / QK_HEAD + EPS)
            qhat = q * rq
            dqn = jnp.concatenate([dq_ref[:, lo:lo + LANES],
                                   _rope_t(dq_ref[:, lo + LANES:lo + HEAD_PAD], cos_b, sin_b)], axis=1)
            dgq = dgq + jnp.sum(dqn * qhat, axis=0, keepdims=True)
            dqh = dqn * gq
            dqr_ref[:, lo:lo + HEAD_PAD] = (
                rq * (dqh - qhat * (jnp.sum(dqh * qhat, axis=-1, keepdims=True) / QK_HEAD))).astype(BF16)
            kn = kn_ref[:, h * LANES:(h + 1) * LANES]
            rk = lax.rsqrt((jnp.sum(kn * kn, axis=-1, keepdims=True) + kr_ss) / QK_HEAD + EPS)
            khat_n, khat_r = kn * rk, kr * rk
            dkn_n = dk_ref[:, lo:lo + LANES]
            dkn_r = _rope_t(dk_ref[:, lo + LANES:lo + HEAD_PAD], cos_b, sin_b)
            dgk_n = dgk_n + jnp.sum(dkn_n * khat_n, axis=0, keepdims=True)
            dgk_r = dgk_r + jnp.sum(dkn_r * khat_r, axis=0, keepdims=True)
            dkh_n, dkh_r = dkn_n * gk[:, :LANES], dkn_r * gk[:, LANES:]
            proj = (jnp.sum(dkh_n * khat_n, axis=-1, keepdims=True)
                    + jnp.sum(dkh_r * khat_r, axis=-1, keepdims=True)) / QK_HEAD
            dkn_ref[:, h * LANES:(h + 1) * LANES] = (rk * (dkh_n - khat_n * proj)).astype(BF16)
            dkr = dkr + rk * (dkh_r - khat_r * proj)
        dkr_ref[:, 0:LANES] = dkr.astype(BF16)
        dkr_ref[:, LANES:HEAD_PAD] = jnp.zeros((tm, HEAD_PAD - LANES), BF16)
        dgk = jnp.concatenate([dgk_n, dgk_r], axis=1)

        @pl.when(pl.program_id(0) == 0)
        def _():
            dgq_ref[...] = dgq
            dgk_ref[...] = dgk

        @pl.when(pl.program_id(0) > 0)
        def _():
            dgq_ref[...] += dgq
            dgk_ref[...] += dgk

    wq, wk = heads * HEAD_PAD, heads * LANES
    row = lambda w: pl.BlockSpec((tm, w), lambda i: (i, 0))
    vec = pl.BlockSpec((1, HEAD_PAD), lambda i: (0, 0))
    return pl.pallas_call(
        body, name=name, grid=(T // tm,),
        in_specs=[row(wq), row(wq), row(wq), row(wk), pl.BlockSpec((tm, HEAD_PAD), lambda i: (i, kr_seg)),
                  row(LANES), row(LANES), vec, vec],
        out_specs=(row(wq), row(wk), row(HEAD_PAD), vec, vec),
        out_shape=(jax.ShapeDtypeStruct((T, wq), BF16), jax.ShapeDtypeStruct((T, wk), BF16),
                   jax.ShapeDtypeStruct((T, HEAD_PAD), BF16),
                   jax.ShapeDtypeStruct((1, HEAD_PAD), F32), jax.ShapeDtypeStruct((1, HEAD_PAD), F32)),
        compiler_params=_params(("arbitrary",)),
    )(dq, dk, q_raw, k_nope, rest, cos_t, sin_t, q_norm, k_norm)


def _causal_mask(s):
    row = lax.broadcasted_iota(jnp.int32, s.shape, 0)
    col = lax.broadcasted_iota(jnp.int32, s.shape, 1)
    return jnp.where(row >= col, s, NEG)


def _flash_fwd(q, k, v, *, name, heads, tq=384):
    T = q.shape[0]
    tq = _tile(T, tq, LANES)
    nq = T // tq
    scale = QK_HEAD ** -0.5
    nt = (((1,), (1,)), ((), ()))

    def body(q_ref, k_ref, v_ref, o_ref, lse_ref):
        def q_block(i, carry):
            q_at = pl.ds(pl.multiple_of(i * tq, tq), tq)
            qb = q_ref[q_at, :]

            def step(j, state, masked):
                m, l, acc = state
                k_at = pl.ds(pl.multiple_of(j * tq, tq), tq)
                s = lax.dot_general(qb, k_ref[k_at, :], nt, preferred_element_type=F32) * scale
                if masked:
                    s = _causal_mask(s)
                m_new = jnp.maximum(m, jnp.max(s, axis=-1, keepdims=True))
                p = jnp.exp(s - m_new)
                alpha = jnp.exp(m - m_new)
                l = alpha * l + jnp.sum(p, axis=-1, keepdims=True)
                acc = alpha * acc + jnp.dot(p.astype(BF16), v_ref[k_at, :], preferred_element_type=F32)
                return m_new, l, acc

            init = (jnp.full((tq, 1), NEG, F32), jnp.zeros((tq, 1), F32), jnp.zeros((tq, V_HEAD), F32))
            state = lax.fori_loop(0, i, lambda j, st: step(j, st, False), init)
            m, l, acc = step(i, state, True)
            o_ref[q_at, :] = (acc / l).astype(BF16)
            lse_ref[0, q_at, :] = jnp.broadcast_to(m + jnp.log(l), (tq, LANES))
            return carry

        lax.fori_loop(0, nq, q_block, 0)

    return pl.pallas_call(
        body, name=name, grid=(heads,),
        in_specs=[pl.BlockSpec((T, HEAD_PAD), lambda h: (0, h)), pl.BlockSpec((T, HEAD_PAD), lambda h: (0, h)),
                  pl.BlockSpec((T, V_HEAD), lambda h: (0, h))],
        out_specs=(pl.BlockSpec((T, V_HEAD), lambda h: (0, h)), pl.BlockSpec((1, T, LANES), lambda h: (h, 0, 0))),
        out_shape=(jax.ShapeDtypeStruct((T, heads * V_HEAD), BF16), jax.ShapeDtypeStruct((heads, T, LANES), F32)),
        compiler_params=_params(("parallel",)),
    )(q, k, v)


def _flash_bwd(q, k, v, o, do, lse, *, name, heads, tq=384):
    T = q.shape[0]
    tq = _tile(T, tq, LANES)
    nq = T // tq
    scale = QK_HEAD ** -0.5
    nt = (((1,), (1,)), ((), ()))
    tn = (((0,), (0,)), ((), ()))

    def body(q_ref, k_ref, v_ref, o_ref, do_ref, lse_ref, dq_ref, dk_ref, dv_ref, delta_ref):
        def fill_delta(i, carry):
            at = pl.ds(pl.multiple_of(i * tq, tq), tq)
            d = jnp.sum(o_ref[at, :].astype(F32) * do_ref[at, :].astype(F32), axis=-1, keepdims=True)
            delta_ref[at, :] = jnp.broadcast_to(d, (tq, LANES))
            dq_ref[at, :] = jnp.zeros((tq, HEAD_PAD), F32)
            return carry

        lax.fori_loop(0, nq, fill_delta, 0)

        def kv_block(j, carry):
            k_at = pl.ds(pl.multiple_of(j * tq, tq), tq)
            kb, vb = k_ref[k_at, :], v_ref[k_at, :]

            def step(i, state, masked):
                dk_acc, dv_acc = state
                q_at = pl.ds(pl.multiple_of(i * tq, tq), tq)
                qb, dob = q_ref[q_at, :], do_ref[q_at, :]
                s = lax.dot_general(qb, kb, nt, preferred_element_type=F32) * scale
                if masked:
                    s = _causal_mask(s)
                p = jnp.exp(s - lse_ref[0, q_at, :][:, 0:1])
                dv_acc = dv_acc + lax.dot_general(p.astype(BF16), dob, tn, preferred_element_type=F32)
                dp = lax.dot_general(dob, vb, nt, preferred_element_type=F32)
                ds = (p * (dp - delta_ref[q_at, :][:, 0:1]) * scale).astype(BF16)
                dk_acc = dk_acc + lax.dot_general(ds, qb, tn, preferred_element_type=F32)
                dq_ref[q_at, :] += jnp.dot(ds, kb, preferred_element_type=F32)
                return dk_acc, dv_acc

            state = step(j, (jnp.zeros((tq, HEAD_PAD), F32), jnp.zeros((tq, V_HEAD), F32)), True)
            dk_acc, dv_acc = lax.fori_loop(j + 1, nq, lambda i, st: step(i, st, False), state)
            dk_ref[k_at, :] = dk_acc
            dv_ref[k_at, :] = dv_acc.astype(BF16)
            return carry

        lax.fori_loop(0, nq, kv_block, 0)

    qk_spec = pl.BlockSpec((T, HEAD_PAD), lambda h: (0, h))
    v_spec = pl.BlockSpec((T, V_HEAD), lambda h: (0, h))
    return pl.pallas_call(
        body, name=name, grid=(heads,),
        in_specs=[qk_spec, qk_spec, v_spec, v_spec, v_spec, pl.BlockSpec((1, T, LANES), lambda h: (h, 0, 0))],
        out_specs=(qk_spec, qk_spec, v_spec),
        out_shape=(jax.ShapeDtypeStruct((T, heads * HEAD_PAD), F32), jax.ShapeDtypeStruct((T, heads * HEAD_PAD), F32),
                   jax.ShapeDtypeStruct((T, heads * V_HEAD), BF16)),
        scratch_shapes=[pltpu.VMEM((T, LANES), F32)],
        compiler_params=_params(("parallel",)),
    )(q, k, v, o, do, lse)


def _merge_fwd(gl, pa, pb, pc, *, name, d, tm=384, tn=1024):
    T = pa.shape[0]
    tm, tn = _tile(T, tm, 16), _tile(d, tn)
    nb = d // tn

    def body(g0, g1, g2, a, b, c, o_ref):
        o_ref[...] = (jax.nn.sigmoid(g0[...]) * a[...] + jax.nn.sigmoid(g1[...]) * b[...]
                      + jax.nn.sigmoid(g2[...]) * c[...]).astype(BF16)

    gate = lambda n: pl.BlockSpec((tm, tn), lambda i, j: (i, n * nb + j))
    blk = pl.BlockSpec((tm, tn), lambda i, j: (i, j))
    return pl.pallas_call(
        body, name=name, grid=(T // tm, nb), in_specs=[gate(0), gate(1), gate(2), blk, blk, blk],
        out_specs=blk, out_shape=jax.ShapeDtypeStruct((T, d), BF16),
        compiler_params=_params(("parallel", "parallel")),
    )(gl, gl, gl, pa, pb, pc)


def _merge_bwd(dm, gl, pa, pb, pc, *, name, d, tm=384, tn=1024):
    T = pa.shape[0]
    tm, tn = _tile(T, tm, 16), _tile(d, tn)
    nb = d // tn

    def body(dm_ref, g0, g1, g2, a, b, c, da, db, dc, dg0, dg1, dg2):
        dmv = dm_ref[...]
        for g_ref, p_ref, dp_ref, dg_ref in ((g0, a, da, dg0), (g1, b, db, dg1), (g2, c, dc, dg2)):
            sg = jax.nn.sigmoid(g_ref[...])
            dp_ref[...] = (dmv * sg).astype(BF16)
            dg_ref[...] = (dmv * p_ref[...] * sg * (1.0 - sg)).astype(BF16)

    gate = lambda n: pl.BlockSpec((tm, tn), lambda i, j: (i, n * nb + j))
    blk = pl.BlockSpec((tm, tn), lambda i, j: (i, j))
    return pl.pallas_call(
        body, name=name, grid=(T // tm, nb), in_specs=[blk, gate(0), gate(1), gate(2), blk, blk, blk],
        out_specs=(blk,) * 6, out_shape=(jax.ShapeDtypeStruct((T, d), BF16),) * 6,
        compiler_params=_params(("parallel", "parallel")),
    )(dm, gl, gl, gl, pa, pb, pc)


def _loss(y, target, *, name, first, last, tm=384):
    T, d = y.shape
    tm = _tile(T, tm, 16)

    def body(y_ref, t_ref, loss_ref, dy_ref):
        i = pl.program_id(0)
        row = lax.broadcasted_iota(jnp.int32, (tm, 1), 0) + i * tm
        real = jnp.logical_and(row >= first, row < last)
        err = jnp.where(real, y_ref[...] - t_ref[...], 0.0)
        dy_ref[...] = err * (1.0 / d)
        part = jnp.broadcast_to(jnp.sum(err * err, keepdims=True).reshape(1, 1), (1, LANES))

        @pl.when(i == 0)
        def _():
            loss_ref[...] = part

        @pl.when(i > 0)
        def _():
            loss_ref[...] += part

    blk = pl.BlockSpec((tm, d), lambda i: (i, 0))
    return pl.pallas_call(
        body, name=name, grid=(T // tm,), in_specs=[blk, blk],
        out_specs=(pl.BlockSpec((1, LANES), lambda i: (0, 0)), blk),
        out_shape=(jax.ShapeDtypeStruct((1, LANES), F32), jax.ShapeDtypeStruct((T, d), F32)),
        compiler_params=_params(("arbitrary",)),
    )(y, target)


def _as3d(a):
    return a.reshape(a.shape[0], -1, a.shape[-1])


def _sum_stack(parts, *, name, out_dtype, rows=256):
    n, R, C = parts.shape
    tr = _tile(R, rows, 16)

    def body(p_ref, o_ref):
        acc = p_ref[0].astype(F32)
        for s in range(1, n):
            acc = acc + p_ref[s].astype(F32)
        o_ref[...] = acc.astype(out_dtype)

    return pl.pallas_call(
        body, name=name, grid=(R // tr,),
        in_specs=[pl.BlockSpec((n, tr, C), lambda i: (0, i, 0))],
        out_specs=pl.BlockSpec((tr, C), lambda i: (i, 0)),
        out_shape=jax.ShapeDtypeStruct((R, C), out_dtype),
        compiler_params=_params(("parallel",)),
    )(parts)


def _adamw(w, g, m, v, *, name, rows=128):
    R, C = w.shape
    tr = _tile(R, rows, 8)
    c1 = 1.0 - ADAM_B1 ** ADAM_STEP
    c2 = 1.0 - ADAM_B2 ** ADAM_STEP

    def body(w_ref, g_ref, m_ref, v_ref, d_ref, nm_ref, nv_ref):
        gv = g_ref[...]
        nm = ADAM_B1 * m_ref[...] + (1.0 - ADAM_B1) * gv
        nv = ADAM_B2 * v_ref[...] + (1.0 - ADAM_B2) * (gv * gv)
        nm_ref[...] = nm
        nv_ref[...] = nv
        d_ref[...] = -ADAM_LR * ((nm / c1) / (jnp.sqrt(nv / c2) + ADAM_EPS) + ADAM_WD * w_ref[...])

    blk = pl.BlockSpec((tr, C), lambda i: (i, 0))
    return pl.pallas_call(
        body, name=name, grid=(R // tr,), in_specs=[blk] * 4, out_specs=(blk,) * 3,
        out_shape=(jax.ShapeDtypeStruct((R, C), F32),) * 3,
        compiler_params=_params(("parallel",)),
    )(w, g, m, v)


ANY = pl.BlockSpec(memory_space=pl.ANY)


def _coords():
    return lax.axis_index("x"), lax.axis_index("y"), lax.axis_index("c")


def _gather_weight(shard, *, name):
    piece = shard.shape[1:]

    def body(src, out, send_sems, recv_sems, local_sem):
        x, y, c = _coords()
        me = 2 * x + y
        chips = [(1 - x, y), (x, 1 - y), (1 - x, 1 - y)]
        sibling = (x, y, 1 - c)

        def remote(k, src_ref, dst_ref, to):
            return pltpu.make_async_remote_copy(src_ref=src_ref, dst_ref=dst_ref, send_sem=send_sems.at[k],
                                                recv_sem=recv_sems.at[k], device_id=to, device_id_type=MESH)

        mine = pltpu.make_async_copy(src.at[c], out.at[c, me], local_sem)
        mine.start()
        first = [remote(0, src.at[c], out.at[c, me], sibling)]
        first += [remote(1 + j, src.at[c], out.at[c, me], (cx, cy, c)) for j, (cx, cy) in enumerate(chips)]
        for cp in first:
            cp.start()
        passed = []
        for j, (cx, cy) in enumerate(chips):
            landed = out.at[c, 2 * cx + cy]
            remote(1 + j, landed, landed, (cx, cy, c)).wait_recv()
            cp = remote(4 + j, landed, landed, sibling)
            cp.start()
            passed.append(cp)
        other = out.at[1 - c, me]
        remote(0, other, other, sibling).wait_recv()
        for j, (cx, cy) in enumerate(chips):
            other = out.at[1 - c, 2 * cx + cy]
            remote(4 + j, other, other, sibling).wait_recv()
        for cp in first + passed:
            cp.wait_send()
        mine.wait()

    return pl.pallas_call(
        body, name=name, in_specs=[ANY], out_specs=ANY,
        out_shape=jax.ShapeDtypeStruct((2, 4) + piece, shard.dtype),
        scratch_shapes=[pltpu.SemaphoreType.DMA((7,)), pltpu.SemaphoreType.DMA((7,)), pltpu.SemaphoreType.DMA],
    )(shard)


def _swap_layers(grads, *, name):
    def body(src, out, send_sem, recv_sem):
        x, y, c = _coords()
        cp = pltpu.make_async_remote_copy(src_ref=src.at[1 - c], dst_ref=out, send_sem=send_sem, recv_sem=recv_sem,
                                          device_id=(x, y, 1 - c), device_id_type=MESH)
        cp.start()
        cp.wait()

    return pl.pallas_call(
        body, name=name, in_specs=[ANY], out_specs=ANY,
        out_shape=jax.ShapeDtypeStruct(grads.shape[1:], grads.dtype),
        scratch_shapes=[pltpu.SemaphoreType.DMA, pltpu.SemaphoreType.DMA],
    )(grads)


def _scatter_pieces(pieces, *, name):
    def body(src, out, send_sems, recv_sems, local_sem):
        x, y, c = _coords()
        me = 2 * x + y
        chips = [(1 - x, y), (x, 1 - y), (1 - x, 1 - y)]
        mine = pltpu.make_async_copy(src.at[me], out.at[me], local_sem)
        mine.start()
        cps = [pltpu.make_async_remote_copy(src_ref=src.at[2 * cx + cy], dst_ref=out.at[me], send_sem=send_sems.at[j],
                                            recv_sem=recv_sems.at[j], device_id=(cx, cy, c), device_id_type=MESH)
               for j, (cx, cy) in enumerate(chips)]
        for cp in cps:
            cp.start()
        for j, (cx, cy) in enumerate(chips):
            slot = out.at[2 * cx + cy]
            pltpu.make_async_remote_copy(src_ref=slot, dst_ref=slot, send_sem=send_sems.at[j], recv_sem=recv_sems.at[j],
                                         device_id=(cx, cy, c), device_id_type=MESH).wait_recv()
        for cp in cps:
            cp.wait_send()
        mine.wait()

    return pl.pallas_call(
        body, name=name, in_specs=[ANY], out_specs=ANY,
        out_shape=jax.ShapeDtypeStruct(pieces.shape, pieces.dtype),
        scratch_shapes=[pltpu.SemaphoreType.DMA((3,)), pltpu.SemaphoreType.DMA((3,)), pltpu.SemaphoreType.DMA],
    )(pieces)


def _join_layers(total, *, name):
    def body(src, out, send_sem, recv_sem, local_sem):
        x, y, c = _coords()
        mine = pltpu.make_async_copy(src, out.at[c], local_sem)
        mine.start()
        cp = pltpu.make_async_remote_copy(src_ref=src, dst_ref=out.at[c], send_sem=send_sem, recv_sem=recv_sem,
                                          device_id=(x, y, 1 - c), device_id_type=MESH)
        cp.start()
        cp.wait_send()
        other = out.at[1 - c]
        pltpu.make_async_remote_copy(src_ref=other, dst_ref=other, send_sem=send_sem, recv_sem=recv_sem,
                                     device_id=(x, y, 1 - c), device_id_type=MESH).wait_recv()
        mine.wait()

    return pl.pallas_call(
        body, name=name, in_specs=[ANY], out_specs=ANY,
        out_shape=jax.ShapeDtypeStruct((2,) + total.shape, total.dtype),
        scratch_shapes=[pltpu.SemaphoreType.DMA, pltpu.SemaphoreType.DMA, pltpu.SemaphoreType.DMA],
    )(total)


def _gather_all(block, *, name):
    def body(src, out, send_sems, recv_sems, local_sem):
        x, y, c = _coords()
        me = 4 * x + 2 * y + c
        flips = [(fx, fy, fc) for fx in (0, 1) for fy in (0, 1) for fc in (0, 1)][1:]
        mine = pltpu.make_async_copy(src, out.at[me], local_sem)
        mine.start()
        peers = [(x ^ fx, y ^ fy, c ^ fc) for fx, fy, fc in flips]
        cps = [pltpu.make_async_remote_copy(src_ref=src, dst_ref=out.at[me], send_sem=send_sems.at[k],
                                            recv_sem=recv_sems.at[k], device_id=peer, device_id_type=MESH)
               for k, peer in enumerate(peers)]
        for cp in cps:
            cp.start()
        for k, (px, py, pc) in enumerate(peers):
            slot = out.at[4 * px + 2 * py + pc]
            pltpu.make_async_remote_copy(src_ref=slot, dst_ref=slot, send_sem=send_sems.at[k], recv_sem=recv_sems.at[k],
                                         device_id=(px, py, pc), device_id_type=MESH).wait_recv()
        for cp in cps:
            cp.wait_send()
        mine.wait()

    return pl.pallas_call(
        body, name=name, in_specs=[ANY], out_specs=ANY,
        out_shape=jax.ShapeDtypeStruct((8,) + block.shape, block.dtype),
        scratch_shapes=[pltpu.SemaphoreType.DMA((7,)), pltpu.SemaphoreType.DMA((7,)), pltpu.SemaphoreType.DMA],
    )(block)


def _cols(o):
    return jnp.transpose(o, (1, 0, 2)).reshape(o.shape[1], -1)


def _uncols(full):
    return jnp.transpose(full.reshape(full.shape[0], 4, -1), (1, 0, 2))


def _rope_pad(x1, x2):
    z = jnp.zeros_like(x1)
    return jnp.concatenate([x1, z, x2, z], axis=-1)


def _head_pad(w, heads):
    r = w.reshape(w.shape[0], heads, QK_HEAD)
    half = QK_ROPE // 2
    out = jnp.concatenate([r[..., :QK_NOPE], _rope_pad(r[..., QK_NOPE:QK_NOPE + half], r[..., QK_NOPE + half:])], axis=-1)
    return out.reshape(w.shape[0], heads * HEAD_PAD)


def _head_unpad(w, heads):
    r = w.reshape(w.shape[0], heads, HEAD_PAD)
    half = QK_ROPE // 2
    out = jnp.concatenate([r[..., :QK_NOPE], r[..., QK_NOPE:QK_NOPE + half],
                           r[..., QK_NOPE + 2 * half:QK_NOPE + 3 * half]], axis=-1)
    return out.reshape(w.shape[0], heads * QK_HEAD)


class _Dims:
    def __init__(self, d, seq):
        self.d = d
        self.seq = seq
        self.t_real = N_META + seq
        self.t = -(-self.t_real // LANES) * LANES
        self.dc = d // 2
        self.dp = d // 2
        self.pg = self.dp // len(POOL_WINDOWS)
        self.heads = d // 128
        self.dff = 4 * d
        self.a_end = 3 * self.dc
        self.q_end = self.a_end + Q_LORA
        self.kv_end = self.q_end + KV_LORA
        self.kr_end = self.kv_end + QK_ROPE
        self.pool_end = self.kr_end + self.dp
        self.d_in = self.pool_end + 3 * d
        self.r_pool = 3 * self.dc
        self.r_q = self.r_pool + self.dp
        self.r_kv = self.r_q + Q_LORA
        self.r_kr = self.r_kv + KV_LORA
        self.r_width = self.r_kr + HEAD_PAD


def _layer_weights(dm, g, small):
    w_in = _cols(g["w_in"])
    half = QK_ROPE // 2
    kr = w_in[:, dm.kv_end:dm.kr_end]
    kr_p = jnp.concatenate([_rope_pad(kr[:, :half], kr[:, half:]), jnp.zeros((dm.d, HEAD_PAD - LANES), BF16)], axis=1)
    w_ukv = _cols(g["w_ukv"]).reshape(KV_LORA, dm.heads, QK_NOPE + V_HEAD)
    return dict(
        wg=w_in[:, dm.pool_end:],
        wr=jnp.concatenate([w_in[:, :dm.a_end], w_in[:, dm.kr_end:dm.pool_end], w_in[:, dm.a_end:dm.kv_end], kr_p], axis=1),
        wuq=_head_pad(_cols(g["w_uq"]), dm.heads),
        wkn=w_ukv[:, :, :QK_NOPE].reshape(KV_LORA, dm.heads * QK_NOPE),
        wv=w_ukv[:, :, QK_NOPE:].reshape(KV_LORA, dm.heads * V_HEAD),
        wp=jnp.transpose(g["pool_w"], (1, 0, 2, 3)).reshape(len(POOL_WINDOWS), dm.pg, dm.pg),
        wba=_cols(g["w_branch_a"]), wbb=g["w_branch_b"].reshape(-1, dm.d), wbc=_cols(g["w_branch_c"]),
        wo=g["w_o"].reshape(-1, dm.d), wup=_cols(g["w_up"]), wdn=g["w_down"].reshape(-1, dm.d),
        conv_w=small["conv_w"],
        attn_norm=small["attn_norm"][None], mlp_norm=small["mlp_norm"][None],
        q_lat_norm=small["q_lat_norm"][None], kv_lat_norm=small["kv_lat_norm"][None],
        q_norm=_head_pad(small["q_norm"][None], 1), k_norm=_head_pad(small["k_norm"][None], 1),
        pool_scale=small["pool_scale"][None],
    )


def _layer_grad_pieces(dm, dw):
    half = QK_ROPE // 2
    dwr, dwg = dw["wr"], dw["wg"]
    d_in = jnp.concatenate([
        dwr[:, :dm.r_pool], dwr[:, dm.r_q:dm.r_kr], dwr[:, dm.r_kr:dm.r_kr + half],
        dwr[:, dm.r_kr + 2 * half:dm.r_kr + 3 * half], dwr[:, dm.r_pool:dm.r_q], dwg], axis=1)
    d_ukv = jnp.concatenate([dw["wkn"].reshape(KV_LORA, dm.heads, QK_NOPE),
                             dw["wv"].reshape(KV_LORA, dm.heads, V_HEAD)], axis=-1).reshape(KV_LORA, -1)
    rows = lambda a: a.reshape((4, a.shape[0] // 4) + a.shape[1:])
    out = dict(
        w_in=_uncols(d_in), w_uq=_uncols(_head_unpad(dw["wuq"], dm.heads)), w_ukv=_uncols(d_ukv),
        pool_w=jnp.transpose(dw["wp"].reshape(len(POOL_WINDOWS), 4, dm.pg // 4, dm.pg), (1, 0, 2, 3)),
        w_branch_a=_uncols(dw["wba"]), w_branch_b=rows(dw["wbb"]), w_branch_c=_uncols(dw["wbc"]),
        w_o=rows(dw["wo"]), w_up=_uncols(dw["wup"]), w_down=rows(dw["wdn"]),
    )
    return {k: v.astype(BF16) for k, v in out.items()}


def _layer_fwd(dm, W, x, cos_t, sin_t, tag):
    n = lambda s: f"{s}_{tag}"
    h = _rms_fwd(x, W["attn_norm"], name=n("attn_norm"))
    gl = _mm(h, W["wg"], name=n("proj_gates"))
    rest = _mm(h, W["wr"], name=n("proj_rest"))
    y_a = _conv_fwd(rest, W["conv_w"], name=n("conv"), dc=dm.dc)
    y_c = _pool_fwd(rest, W["wp"], W["pool_scale"], name=n("pool"), seg0=dm.r_pool // dm.pg, pg=dm.pg)
    q_lat = _rms_fwd(rest, W["q_lat_norm"], name=n("q_lat_norm"), width=Q_LORA, seg=dm.r_q // Q_LORA)
    kv_lat = _rms_fwd(rest, W["kv_lat_norm"], name=n("kv_lat_norm"), width=KV_LORA, seg=dm.r_kv // KV_LORA)
    q_raw = _mm(q_lat, W["wuq"], name=n("up_q"))
    k_nope = _mm(kv_lat, W["wkn"], name=n("up_k"))
    v = _mm(kv_lat, W["wv"], name=n("up_v"), out_dtype=BF16)
    q, k = _qk_fwd(q_raw, k_nope, rest, cos_t, sin_t, W["q_norm"], W["k_norm"], name=n("qk_norm_rope"),
                   heads=dm.heads, kr_seg=dm.r_kr // HEAD_PAD)
    y_b, lse = _flash_fwd(q, k, v, name=n("attention"), heads=dm.heads)
    pa = _mm(y_a, W["wba"], name=n("branch_a"))
    pb = _mm(y_b, W["wbb"], name=n("branch_b"))
    pc = _mm(y_c, W["wbc"], name=n("branch_c"))
    merged = _merge_fwd(gl, pa, pb, pc, name=n("merge"), d=dm.d)
    x1 = _mm(merged, W["wo"], name=n("out_proj"), add=x)
    h2 = _rms_fwd(x1, W["mlp_norm"], name=n("mlp_norm"))
    up, act = _mm(h2, W["wup"], name=n("mlp_up"), epi="relu2")
    x2 = _mm(act, W["wdn"], name=n("mlp_down"), add=x1, tk=2048)
    saved = dict(x=x, h=h, gl=gl, rest=rest, y_a=y_a, y_c=y_c, q_lat=q_lat, kv_lat=kv_lat, q_raw=q_raw, k_nope=k_nope,
                 v=v, q=q, k=k, y_b=y_b, lse=lse, pa=pa, pb=pb, pc=pc, merged=merged, x1=x1, h2=h2, up=up, act=act)
    return x2, saved


def _layer_bwd(dm, W, S, dx2, cos_t, sin_t, tag):
    n = lambda s: f"{s}_{tag}"
    dw, ds = {}, {}
    dup = _mm(dx2, W["wdn"], name=n("d_mlp_down"), tb=True, aux=S["up"], epi="drelu2", out_dtype=BF16)
    dw["wdn"] = _mm(S["act"], dx2, name=n("dw_mlp_down"), ta=True, tm=1024, tk=1408)
    dh2 = _mm(dup, W["wup"], name=n("d_mlp_up"), tb=True, tk=2048)
    dw["wup"] = _mm(S["h2"], dup, name=n("dw_mlp_up"), ta=True, tm=1024, tk=1408)
    dx1, ds["mlp_norm"] = _rms_bwd(dh2, S["x1"], W["mlp_norm"], name=n("d_mlp_norm"), res=dx2)
    dmerged = _mm(dx1, W["wo"], name=n("d_out_proj"), tb=True)
    dw["wo"] = _mm(S["merged"], dx1, name=n("dw_out_proj"), ta=True, tm=1024, tk=1408)
    dpa, dpb, dpc, dg0, dg1, dg2 = _merge_bwd(dmerged, S["gl"], S["pa"], S["pb"], S["pc"], name=n("d_merge"), d=dm.d)
    dgl = jnp.concatenate([dg0, dg1, dg2], axis=1)
    dy_a = _mm(dpa, W["wba"], name=n("d_branch_a"), tb=True)
    dw["wba"] = _mm(S["y_a"], dpa, name=n("dw_branch_a"), ta=True, tm=1024, tk=1408)
    dy_b = _mm(dpb, W["wbb"], name=n("d_branch_b"), tb=True, out_dtype=BF16)
    dw["wbb"] = _mm(S["y_b"], dpb, name=n("dw_branch_b"), ta=True, tm=1024, tk=1408)
    dy_c = _mm(dpc, W["wbc"], name=n("d_branch_c"), tb=True)
    dw["wbc"] = _mm(S["y_c"], dpc, name=n("dw_branch_c"), ta=True, tm=1024, tk=1408)
    dq, dk, dv = _flash_bwd(S["q"], S["k"], S["v"], S["y_b"], dy_b, S["lse"], name=n("d_attention"), heads=dm.heads)
    dq_raw, dk_nope, dk_rope, dgq, dgk = _qk_bwd(
        dq, dk, S["q_raw"], S["k_nope"], S["rest"], cos_t, sin_t, W["q_norm"], W["k_norm"], name=n("d_qk_norm_rope"),
        heads=dm.heads, kr_seg=dm.r_kr // HEAD_PAD)
    ds["q_norm"] = _head_unpad(dgq, 1)
    ds["k_norm"] = _head_unpad(dgk, 1)
    dq_lat_n = _mm(dq_raw, W["wuq"], name=n("d_up_q"), tb=True, tk=2048)
    dw["wuq"] = _mm(S["q_lat"], dq_raw, name=n("dw_up_q"), ta=True, tm=512, tk=1408)
    dkv_v = _mm(dv, W["wv"], name=n("d_up_v"), tb=True)
    dkv_lat_n = _mm(dk_nope, W["wkn"], name=n("d_up_k"), tb=True, add=dkv_v)
    dw["wkn"] = _mm(S["kv_lat"], dk_nope, name=n("dw_up_k"), ta=True, tm=512, tk=1408)
    dw["wv"] = _mm(S["kv_lat"], dv, name=n("dw_up_v"), ta=True, tm=512, tk=1408)
    dq_lat, ds["q_lat_norm"] = _rms_bwd(dq_lat_n, S["rest"], W["q_lat_norm"], name=n("d_q_lat_norm"), width=Q_LORA,
                                        seg=dm.r_q // Q_LORA, out_dtype=BF16)
    dkv_lat, ds["kv_lat_norm"] = _rms_bwd(dkv_lat_n, S["rest"], W["kv_lat_norm"], name=n("d_kv_lat_norm"), width=KV_LORA,
                                          seg=dm.r_kv // KV_LORA, out_dtype=BF16)
    du, db, dc, ds["conv_w"] = _conv_bwd(S["rest"], W["conv_w"], dy_a, name=n("d_conv"), dc=dm.dc)
    dpool, dw["wp"], ds["pool_scale"] = _pool_bwd(S["rest"], W["wp"], W["pool_scale"], dy_c, name=n("d_pool"),
                                                  seg0=dm.r_pool // dm.pg, pg=dm.pg)
    drest = jnp.concatenate([du, db, dc, dpool, dq_lat, dkv_lat, dk_rope], axis=1)
    dh_g = _mm(dgl, W["wg"], name=n("d_proj_gates"), tb=True, tk=2048)
    dh = _mm(drest, W["wr"], name=n("d_proj_rest"), tb=True, add=dh_g, tk=1792)
    dw["wg"] = _mm(S["h"], dgl, name=n("dw_proj_gates"), ta=True, tm=1024, tk=1408)
    dw["wr"] = _mm(S["h"], drest, name=n("dw_proj_rest"), ta=True, tm=1024, tk=1408)
    dx, ds["attn_norm"] = _rms_bwd(dh, S["x"], W["attn_norm"], name=n("d_attn_norm"), res=dx1)
    return dx, dw, ds


BIG = ("w_in", "w_uq", "w_ukv", "pool_w", "w_branch_a", "w_branch_b", "w_branch_c", "w_o", "w_up", "w_down")
REPLICATED = ("attn_norm", "q_lat_norm", "kv_lat_norm", "q_norm", "k_norm", "pool_scale", "mlp_norm")
WEIGHTS = ("meta_tokens", "attn_norm", "w_in", "conv_w", "q_lat_norm", "kv_lat_norm", "w_uq", "w_ukv", "q_norm",
           "k_norm", "pool_w", "pool_scale", "w_branch_a", "w_branch_b", "w_branch_c", "w_o", "mlp_norm", "w_up",
           "w_down")


def _pack(arrays):
    flat = jnp.concatenate([a.reshape(-1).astype(F32) for a in arrays])
    pad = (-flat.shape[0]) % (8 * LANES)
    return jnp.pad(flat, (0, pad)).reshape(-1, LANES)


def _unpack(flat, shapes):
    out, pos = [], 0
    flat = flat.reshape(-1)
    for shp in shapes:
        size = math.prod(shp)
        out.append(flat[pos:pos + size].reshape(shp))
        pos += size
    return out


def _update(w, g, m, v, name):
    shp = w.shape
    to2 = lambda a: a.reshape(-1, shp[-1])
    delta, nm, nv = _adamw(to2(w), to2(g), to2(m), to2(v), name=name)
    return delta.reshape(shp), nm.reshape(shp), nv.reshape(shp)


def _step(args):
    x = args["x"][0]
    seq, d = x.shape
    dm = _Dims(d, seq)
    xi, yi, ci = _coords()
    chip = 2 * xi + yi

    gathered = {k: _gather_weight(args[k].astype(BF16), name=f"gather_{k}") for k in BIG}
    small_w = _gather_all(_pack([args["conv_w"], args["meta_tokens"]]), name="gather_small_weights")
    conv_shape, meta_shape = args["conv_w"].shape, args["meta_tokens"].shape
    per_chip = [_unpack(small_w[2 * j], [conv_shape, meta_shape]) for j in range(4)]
    conv_full = jnp.concatenate([p[0] for p in per_chip], axis=-1)
    meta_full = jnp.concatenate([p[1] for p in per_chip], axis=-1)

    layers = []
    for l in range(2):
        small = {k: args[k][l] for k in REPLICATED}
        small["conv_w"] = conv_full[l]
        layers.append(_layer_weights(dm, {k: gathered[k][l] for k in BIG}, small))

    pos = jnp.arange(dm.t, dtype=F32)
    inv = ROPE_THETA ** (-jnp.arange(0, QK_ROPE, 2, dtype=F32) / QK_ROPE)
    ang = pos[:, None] * inv[None, :]
    cos_t = _rope_pad(jnp.cos(ang), jnp.cos(ang))
    sin_t = _rope_pad(-jnp.sin(ang), jnp.sin(ang))
    tail = jnp.zeros((dm.t - dm.t_real, d), F32)
    h0 = jnp.concatenate([meta_full, x, tail], axis=0)
    target = jnp.concatenate([jnp.zeros((N_META, d), F32), args["loss_target"][0], tail], axis=0)

    h1, saved0 = _layer_fwd(dm, layers[0], h0, cos_t, sin_t, "l0")
    h2, saved1 = _layer_fwd(dm, layers[1], h1, cos_t, sin_t, "l1")
    sq, dy = _loss(h2, target, name="loss_head", first=N_META, last=dm.t_real)
    loss = lax.psum(0.5 / d * sq[0, 0], ("x", "y", "c"))
    dh1, dw1, ds1 = _layer_bwd(dm, layers[1], saved1, dy, cos_t, sin_t, "l1")
    dh0, dw0, ds0 = _layer_bwd(dm, layers[0], saved0, dh1, cos_t, sin_t, "l0")
    grad_x = dh0[N_META:dm.t_real][None]

    pieces = [_layer_grad_pieces(dm, dw0), _layer_grad_pieces(dm, dw1)]
    grads = {}
    for k in BIG:
        both = jnp.stack([pieces[0][k], pieces[1][k]])
        theirs = _swap_layers(both, name=f"swap_{k}")
        mine = lax.dynamic_index_in_dim(both, ci, 0, keepdims=False)
        pair = _sum_stack(_as3d(jnp.stack([mine, theirs]).reshape((2, -1) + mine.shape[-1:])),
                          name=f"pair_sum_{k}", out_dtype=BF16).reshape(mine.shape)
        landed = _scatter_pieces(pair, name=f"scatter_{k}")
        total = _sum_stack(_as3d(landed), name=f"chip_sum_{k}", out_dtype=F32).reshape(landed.shape[1:])
        grads[k] = _join_layers(total, name=f"join_{k}")

    small_names = REPLICATED + ("conv_w",)
    small_parts = [jnp.stack([ds0[k].reshape(ds0[k].shape[-2:] if k == "conv_w" else (-1,)),
                              ds1[k].reshape(ds1[k].shape[-2:] if k == "conv_w" else (-1,))]) for k in small_names]
    small_parts.append(dh0[:N_META])
    small_all = _gather_all(_pack(small_parts), name="gather_small_grads")
    small_sum = _sum_stack(small_all, name="sum_small_grads", out_dtype=F32)
    small_g = dict(zip(small_names + ("meta_tokens",), _unpack(small_sum, [p.shape for p in small_parts])))
    for k in REPLICATED:
        grads[k] = small_g[k]
    dcw = conv_shape[-1]
    grads["conv_w"] = lax.dynamic_slice_in_dim(small_g["conv_w"], chip * dcw, dcw, axis=2)
    dmeta = meta_shape[-1]
    grads["meta_tokens"] = lax.dynamic_slice_in_dim(small_g["meta_tokens"], chip * dmeta, dmeta, axis=1)

    delta, new_m, new_v = {}, {}, {}
    for k in WEIGHTS:
        grads[k] = grads[k].reshape(args[k].shape)
        delta[k], new_m[k], new_v[k] = _update(args[k], grads[k], args["m_" + k], args["v_" + k], f"adamw_{k}")
    return (loss, grad_x, *[grads[k] for k in WEIGHTS], *[delta[k] for k in WEIGHTS],
            *[new_m[k] for k in WEIGHTS], *[new_v[k] for k in WEIGHTS])


def kernel(x, meta_tokens, attn_norm, w_in, conv_w, q_lat_norm, kv_lat_norm, w_uq, w_ukv, q_norm, k_norm, pool_w, pool_scale, w_branch_a, w_branch_b, w_branch_c, w_o, mlp_norm, w_up, w_down, loss_target, m_meta_tokens, m_attn_norm, m_w_in, m_conv_w, m_q_lat_norm, m_kv_lat_norm, m_w_uq, m_w_ukv, m_q_norm, m_k_norm, m_pool_w, m_pool_scale, m_w_branch_a, m_w_branch_b, m_w_branch_c, m_w_o, m_mlp_norm, m_w_up, m_w_down, v_meta_tokens, v_attn_norm, v_w_in, v_conv_w, v_q_lat_norm, v_kv_lat_norm, v_w_uq, v_w_ukv, v_q_norm, v_k_norm, v_pool_w, v_pool_scale, v_w_branch_a, v_w_branch_b, v_w_branch_c, v_w_o, v_mlp_norm, v_w_up, v_w_down):
    return _step(dict(locals()))
```

```python
import functools
import math

import jax
import jax.numpy as jnp
from jax import lax
from jax.experimental import pallas as pl
from jax.experimental.pallas import tpu as pltpu

F32 = jnp.float32
BF16 = jnp.bfloat16
MESH = pl.DeviceIdType.MESH

EPS = 1e-6
N_META = 16
QK_NOPE = 128
QK_ROPE = 64
QK_HEAD = QK_NOPE + QK_ROPE
V_HEAD = 128
HEAD_PAD = 256
Q_LORA = 512
KV_LORA = 512
ROPE_THETA = 10000.0
POOL_WINDOWS = (2, 4, 8, 16)
HALO = 16
LANES = 128
ADAM_LR = 0.001
ADAM_B1 = 0.9
ADAM_B2 = 0.999
ADAM_EPS = 1e-08
ADAM_WD = 0.01
ADAM_STEP = 10
VMEM_LIMIT = 52 * 1024 * 1024
NEG = -1e30


def _tile(n, target, mult=LANES):
    best = None
    for t in range(mult, min(n, target) + 1, mult):
        if n % t == 0:
            best = t
    return n if best is None else best


def _params(sem=None):
    return pltpu.CompilerParams(dimension_semantics=sem, vmem_limit_bytes=VMEM_LIMIT)


def _mm(a, b, *, name, ta=False, tb=False, add=None, aux=None, epi=None, out_dtype=F32,
        tm=704, tn=1024, tk=None):
    if ta:
        K, M = a.shape
    else:
        M, K = a.shape
    if tb:
        N, kb = b.shape
    else:
        kb, N = b.shape
    assert K == kb, (a.shape, b.shape, ta, tb)
    tm = _tile(M, tm, LANES if ta else 16)
    tn = _tile(N, tn, LANES)
    tk = K if tk is None else _tile(K, tk, LANES if (not ta or tb) else 16)
    nk = K // tk
    grid = (M // tm, N // tn, nk)

    a_spec = pl.BlockSpec((tk, tm), lambda i, j, k: (k, i)) if ta else pl.BlockSpec((tm, tk), lambda i, j, k: (i, k))
    b_spec = pl.BlockSpec((tn, tk), lambda i, j, k: (j, k)) if tb else pl.BlockSpec((tk, tn), lambda i, j, k: (k, j))
    o_spec = pl.BlockSpec((tm, tn), lambda i, j, k: (i, j))
    in_specs = [a_spec, b_spec]
    operands = [a, b]
    if add is not None:
        in_specs.append(o_spec)
        operands.append(add)
    if aux is not None:
        in_specs.append(o_spec)
        operands.append(aux)
    if epi == "relu2":
        out_shape = (jax.ShapeDtypeStruct((M, N), BF16), jax.ShapeDtypeStruct((M, N), BF16))
        out_specs = (o_spec, o_spec)
    else:
        out_shape = jax.ShapeDtypeStruct((M, N), out_dtype)
        out_specs = o_spec
    dims = (((0 if ta else 1,), (1 if tb else 0,)), ((), ()))
    has_add, has_aux = add is not None, aux is not None

    def body(*refs):
        a_ref, b_ref = refs[0], refs[1]
        pos = 2
        add_ref = aux_ref = None
        if has_add:
            add_ref = refs[pos]
            pos += 1
        if has_aux:
            aux_ref = refs[pos]
            pos += 1
        n_out = 2 if epi == "relu2" else 1
        out_refs = refs[pos:pos + n_out]
        acc_ref = refs[pos + n_out] if nk > 1 else None

        part = lax.dot_general(a_ref[...].astype(BF16), b_ref[...].astype(BF16), dims,
                               preferred_element_type=F32)

        def finish(acc):
            if has_add:
                acc = acc + add_ref[...].astype(F32)
            if epi == "relu2":
                r = jnp.maximum(acc, 0.0)
                out_refs[0][...] = acc.astype(BF16)
                out_refs[1][...] = (r * r).astype(BF16)
            elif epi == "drelu2":
                u = aux_ref[...].astype(F32)
                out_refs[0][...] = (acc * (2.0 * jnp.maximum(u, 0.0))).astype(out_dtype)
            else:
                out_refs[0][...] = acc.astype(out_dtype)

        if nk == 1:
            finish(part)
        else:
            k = pl.program_id(2)

            @pl.when(k == 0)
            def _():
                acc_ref[...] = part

            @pl.when(k > 0)
            def _():
                acc_ref[...] += part

            @pl.when(k == nk - 1)
            def _():
                finish(acc_ref[...])

    scratch = [pltpu.VMEM((tm, tn), F32)] if nk > 1 else []
    return pl.pallas_call(
        body, name=name, grid=grid, in_specs=in_specs, out_specs=out_specs, out_shape=out_shape,
        scratch_shapes=scratch, compiler_params=_params(("parallel", "parallel", "arbitrary")),
    )(*operands)


def _rms_fwd(x, g, *, name, width=None, seg=0, tm=384):
    T = x.shape[0]
    width = x.shape[1] if width is None else width
    tm = _tile(T, tm, 16)

    def body(x_ref, g_ref, o_ref):
        xf = x_ref[...].astype(F32)
        r = lax.rsqrt(jnp.mean(xf * xf, axis=-1, keepdims=True) + EPS)
        o_ref[...] = (xf * r * g_ref[...]).astype(BF16)

    return pl.pallas_call(
        body, name=name, grid=(T // tm,),
        in_specs=[pl.BlockSpec((tm, width), lambda i: (i, seg)), pl.BlockSpec((1, width), lambda i: (0, 0))],
        out_specs=pl.BlockSpec((tm, width), lambda i: (i, 0)),
        out_shape=jax.ShapeDtypeStruct((T, width), BF16),
        compiler_params=_params(("parallel",)),
    )(x, g)


def _rms_bwd(dy, x, g, *, name, width=None, seg=0, res=None, out_dtype=F32, tm=384):
    T = x.shape[0]
    width = x.shape[1] if width is None else width
    tm = _tile(T, tm, 16)
    has_res = res is not None

    def body(*refs):
        dy_ref, x_ref, g_ref = refs[:3]
        res_ref = refs[3] if has_res else None
        dx_ref, dg_ref = refs[-2:]
        xf = x_ref[...].astype(F32)
        dyf = dy_ref[...].astype(F32)
        r = lax.rsqrt(jnp.mean(xf * xf, axis=-1, keepdims=True) + EPS)
        xhat = xf * r
        dyh = dyf * g_ref[...]
        dx = r * (dyh - xhat * jnp.mean(dyh * xhat, axis=-1, keepdims=True))
        if has_res:
            dx = dx + res_ref[...].astype(F32)
        dx_ref[...] = dx.astype(out_dtype)
        part = jnp.sum(dyf * xhat, axis=0, keepdims=True)

        @pl.when(pl.program_id(0) == 0)
        def _():
            dg_ref[...] = part

        @pl.when(pl.program_id(0) > 0)
        def _():
            dg_ref[...] += part

    row = pl.BlockSpec((tm, width), lambda i: (i, 0))
    in_specs = [row, pl.BlockSpec((tm, width), lambda i: (i, seg)), pl.BlockSpec((1, width), lambda i: (0, 0))]
    operands = [dy, x, g]
    if has_res:
        in_specs.append(row)
        operands.append(res)
    return pl.pallas_call(
        body, name=name, grid=(T // tm,), in_specs=in_specs,
        out_specs=(row, pl.BlockSpec((1, width), lambda i: (0, 0))),
        out_shape=(jax.ShapeDtypeStruct((T, width), out_dtype), jax.ShapeDtypeStruct((1, width), F32)),
        compiler_params=_params(("arbitrary",)),
    )(*operands)


def _down(ext, k):
    return pltpu.roll(ext, k, 0)


def _up(ext, k):
    return pltpu.roll(ext, ext.shape[0] - k, 0)


def _pre_halo(ref, r, R):
    start = pl.multiple_of(jnp.maximum(r * R - HALO, 0), 8)
    keep = (r > 0).astype(F32)
    return ref[pl.ds(start, HALO), :].astype(F32) * keep


def _post_halo(ref, r, R, n_chunks):
    start = pl.multiple_of(jnp.minimum(r * R + R, (n_chunks - 1) * R + R - HALO), 8)
    keep = (r < n_chunks - 1).astype(F32)
    return ref[pl.ds(start, HALO), :].astype(F32) * keep


def _chunk(ref, r, R):
    return ref[pl.ds(pl.multiple_of(r * R, 8), R), :].astype(F32)


def _conv_fwd(rest, conv_w, *, name, dc, tc=128, rows=1056):
    T = rest.shape[0]
    tc = _tile(dc, tc)
    nb = dc // tc
    R = _tile(T, rows, 16)
    n_chunks = T // R

    def body(u_ref, b_ref, c_ref, w_ref, y_ref):
        w0, w1, w2 = w_ref[0:1, :], w_ref[1:2, :], w_ref[2:3, :]

        def chunk(r, carry):
            cu = _chunk(c_ref, r, R) * _chunk(u_ref, r, R)
            ext = jnp.concatenate([_pre_halo(c_ref, r, R) * _pre_halo(u_ref, r, R), cu], axis=0)
            conv = w0 * _down(ext, 2)[HALO:] + w1 * _down(ext, 1)[HALO:] + w2 * cu
            y_ref[pl.ds(pl.multiple_of(r * R, 8), R), :] = (_chunk(b_ref, r, R) * conv).astype(BF16)
            return carry

        lax.fori_loop(0, n_chunks, chunk, 0)

    col = lambda off: pl.BlockSpec((T, tc), lambda j: (0, off * nb + j))
    return pl.pallas_call(
        body, name=name, grid=(nb,),
        in_specs=[col(0), col(1), col(2), pl.BlockSpec((3, tc), lambda j: (0, j))],
        out_specs=pl.BlockSpec((T, tc), lambda j: (0, j)),
        out_shape=jax.ShapeDtypeStruct((T, dc), BF16),
        compiler_params=_params(("parallel",)),
    )(rest, rest, rest, conv_w)


def _conv_bwd(rest, conv_w, dy, *, name, dc, tc=128, rows=1056):
    T = rest.shape[0]
    tc = _tile(dc, tc)
    nb = dc // tc
    R = _tile(T, rows, 16)
    n_chunks = T // R

    def body(u_ref, b_ref, c_ref, w_ref, dy_ref, du_ref, db_ref, dc_ref, dw_ref):
        w0, w1, w2 = w_ref[0:1, :], w_ref[1:2, :], w_ref[2:3, :]

        def chunk(r, carry):
            a0, a1, a2 = carry
            u, b, c = _chunk(u_ref, r, R), _chunk(b_ref, r, R), _chunk(c_ref, r, R)
            dy_c = _chunk(dy_ref, r, R)
            cu = c * u
            ext = jnp.concatenate([_pre_halo(c_ref, r, R) * _pre_halo(u_ref, r, R), cu], axis=0)
            cu1, cu2 = _down(ext, 1)[HALO:], _down(ext, 2)[HALO:]
            conv = w0 * cu2 + w1 * cu1 + w2 * cu
            dconv = dy_c * b
            dext = jnp.concatenate(
                [dconv, _post_halo(dy_ref, r, R, n_chunks) * _post_halo(b_ref, r, R, n_chunks)], axis=0)
            dcu = w2 * dconv + w1 * _up(dext, 1)[:R] + w0 * _up(dext, 2)[:R]
            rows_at = pl.ds(pl.multiple_of(r * R, 8), R)
            db_ref[rows_at, :] = (dy_c * conv).astype(BF16)
            du_ref[rows_at, :] = (dcu * c).astype(BF16)
            dc_ref[rows_at, :] = (dcu * u).astype(BF16)
            return (a0 + jnp.sum(dconv * cu2, axis=0, keepdims=True),
                    a1 + jnp.sum(dconv * cu1, axis=0, keepdims=True),
                    a2 + jnp.sum(dconv * cu, axis=0, keepdims=True))

        zero = jnp.zeros((1, tc), F32)
        a0, a1, a2 = lax.fori_loop(0, n_chunks, chunk, (zero, zero, zero))
        dw_ref[0:1, :] = a0
        dw_ref[1:2, :] = a1
        dw_ref[2:3, :] = a2

    col = lambda off: pl.BlockSpec((T, tc), lambda j: (0, off * nb + j))
    own = pl.BlockSpec((T, tc), lambda j: (0, j))
    return pl.pallas_call(
        body, name=name, grid=(nb,),
        in_specs=[col(0), col(1), col(2), pl.BlockSpec((3, tc), lambda j: (0, j)), own],
        out_specs=(own, own, own, pl.BlockSpec((3, tc), lambda j: (0, j))),
        out_shape=(jax.ShapeDtypeStruct((T, dc), BF16),) * 3 + (jax.ShapeDtypeStruct((3, dc), F32),),
        compiler_params=_params(("parallel",)),
    )(rest, rest, rest, conv_w, dy)


def _window_count(r, R, n_rows, w, first_row_offset):
    t = lax.broadcasted_iota(jnp.int32, (n_rows, 1), 0) + (r * R + first_row_offset)
    return jnp.minimum(t + 1, w).astype(F32)


def _pool_fwd(rest, pool_w, pool_scale, *, name, seg0, pg, rows=1056):
    T = rest.shape[0]
    R = _tile(T, rows, 16)
    n_chunks = T // R
    n_groups = len(POOL_WINDOWS)

    def body(x_ref, w_ref, s_ref, y_ref):
        def run(window):
            def chunk(r, carry):
                g = _chunk(x_ref, r, R)
                s = jnp.concatenate([_pre_halo(x_ref, r, R), g], axis=0)
                k = 1
                while k < window:
                    s = s + _down(s, k)
                    k *= 2
                pooled = s[HALO:] / _window_count(r, R, R, window, 0) - g
                mixed = jnp.dot(pooled.astype(BF16), w_ref[0], preferred_element_type=F32)
                y_ref[pl.ds(pl.multiple_of(r * R, 8), R), :] = (mixed * s_ref[...]).astype(BF16)
                return carry

            lax.fori_loop(0, n_chunks, chunk, 0)

        for gi, window in enumerate(POOL_WINDOWS):
            pl.when(pl.program_id(0) == gi)(functools.partial(run, window))

    return pl.pallas_call(
        body, name=name, grid=(n_groups,),
        in_specs=[pl.BlockSpec((T, pg), lambda g: (0, seg0 + g)),
                  pl.BlockSpec((1, pg, pg), lambda g: (g, 0, 0)),
                  pl.BlockSpec((1, pg), lambda g: (0, g))],
        out_specs=pl.BlockSpec((T, pg), lambda g: (0, g)),
        out_shape=jax.ShapeDtypeStruct((T, n_groups * pg), BF16),
        compiler_params=_params(("parallel",)),
    )(rest, pool_w, pool_scale)


def _pool_bwd(rest, pool_w, pool_scale, dy, *, name, seg0, pg, rows=1056):
    T = rest.shape[0]
    R = _tile(T, rows, 16)
    n_chunks = T // R
    n_groups = len(POOL_WINDOWS)

    def body(x_ref, w_ref, s_ref, dy_ref, dx_ref, dw_ref, ds_ref):
        def run(window):
            def chunk(r, carry):
                dw_acc, ds_acc = carry
                g = _chunk(x_ref, r, R)
                s = jnp.concatenate([_pre_halo(x_ref, r, R), g], axis=0)
                k = 1
                while k < window:
                    s = s + _down(s, k)
                    k *= 2
                pooled = (s[HALO:] / _window_count(r, R, R, window, 0) - g).astype(BF16)
                mixed = jnp.dot(pooled, w_ref[0], preferred_element_type=F32)
                dy_c = _chunk(dy_ref, r, R)
                dm_ext = (jnp.concatenate([dy_c, _post_halo(dy_ref, r, R, n_chunks)], axis=0)
                          * s_ref[...]).astype(BF16)
                dpool_ext = lax.dot_general(dm_ext, w_ref[0], (((1,), (1,)), ((), ())),
                                            preferred_element_type=F32)
                a = dpool_ext / _window_count(r, R, R + HALO, window, 0)
                k = 1
                while k < window:
                    a = a + _up(a, k)
                    k *= 2
                dx_ref[pl.ds(pl.multiple_of(r * R, 8), R), :] = (a[:R] - dpool_ext[:R]).astype(BF16)
                dw_acc = dw_acc + lax.dot_general(pooled, dm_ext[:R], (((0,), (0,)), ((), ())),
                                                  preferred_element_type=F32)
                ds_acc = ds_acc + jnp.sum(dy_c * mixed, axis=0, keepdims=True)
                return dw_acc, ds_acc

            dw_acc, ds_acc = lax.fori_loop(0, n_chunks, chunk,
                                           (jnp.zeros((pg, pg), F32), jnp.zeros((1, pg), F32)))
            dw_ref[0] = dw_acc
            ds_ref[...] = ds_acc

        for gi, window in enumerate(POOL_WINDOWS):
            pl.when(pl.program_id(0) == gi)(functools.partial(run, window))

    own = pl.BlockSpec((T, pg), lambda g: (0, g))
    return pl.pallas_call(
        body, name=name, grid=(n_groups,),
        in_specs=[pl.BlockSpec((T, pg), lambda g: (0, seg0 + g)),
                  pl.BlockSpec((1, pg, pg), lambda g: (g, 0, 0)),
                  pl.BlockSpec((1, pg), lambda g: (0, g)), own],
        out_specs=(own, pl.BlockSpec((1, pg, pg), lambda g: (g, 0, 0)), pl.BlockSpec((1, pg), lambda g: (0, g))),
        out_shape=(jax.ShapeDtypeStruct((T, n_groups * pg), BF16),
                   jax.ShapeDtypeStruct((n_groups, pg, pg), F32),
                   jax.ShapeDtypeStruct((1, n_groups * pg), F32)),
        compiler_params=_params(("parallel",)),
    )(rest, pool_w, pool_scale, dy)


def _rope(r, cos_t, sin_t):
    return r * cos_t + pltpu.roll(r, LANES // 2, 1) * sin_t


def _rope_t(d, cos_t, sin_t):
    return d * cos_t + pltpu.roll(d * sin_t, LANES // 2, 1)


def _qk_fwd(q_raw, k_nope, rest, cos_t, sin_t, q_norm, k_norm, *, name, heads, kr_seg, tm=192):
    T = q_raw.shape[0]
    tm = _tile(T, tm, 16)

    def body(q_ref, kn_ref, kr_ref, c_ref, s_ref, gq_ref, gk_ref, qo_ref, ko_ref):
        cos_b, sin_b = c_ref[...], s_ref[...]
        kr = kr_ref[:, 0:LANES]
        kr_ss = jnp.sum(kr * kr, axis=-1, keepdims=True)
        gq, gk = gq_ref[...], gk_ref[...]
        for h in range(heads):
            lo = h * HEAD_PAD
            q = q_ref[:, lo:lo + HEAD_PAD]
            rq = lax.rsqrt(jnp.sum(q * q, axis=-1, keepdims=True) / QK_HEAD + EPS)
            qn = q * rq * gq
            qo_ref[:, lo:lo + LANES] = qn[:, :LANES].astype(BF16)
            qo_ref[:, lo + LANES:lo + HEAD_PAD] = _rope(qn[:, LANES:], cos_b, sin_b).astype(BF16)
            kn = kn_ref[:, h * LANES:(h + 1) * LANES]
            rk = lax.rsqrt((jnp.sum(kn * kn, axis=-1, keepdims=True) + kr_ss) / QK_HEAD + EPS)
            ko_ref[:, lo:lo + LANES] = (kn * rk * gk[:, :LANES]).astype(BF16)
            ko_ref[:, lo + LANES:lo + HEAD_PAD] = _rope(kr * rk * gk[:, LANES:], cos_b, sin_b).astype(BF16)

    wq, wk = heads * HEAD_PAD, heads * LANES
    return pl.pallas_call(
        body, name=name, grid=(T // tm,),
        in_specs=[pl.BlockSpec((tm, wq), lambda i: (i, 0)), pl.BlockSpec((tm, wk), lambda i: (i, 0)),
                  pl.BlockSpec((tm, HEAD_PAD), lambda i: (i, kr_seg)),
                  pl.BlockSpec((tm, LANES), lambda i: (i, 0)), pl.BlockSpec((tm, LANES), lambda i: (i, 0)),
                  pl.BlockSpec((1, HEAD_PAD), lambda i: (0, 0)), pl.BlockSpec((1, HEAD_PAD), lambda i: (0, 0))],
        out_specs=(pl.BlockSpec((tm, wq), lambda i: (i, 0)), pl.BlockSpec((tm, wq), lambda i: (i, 0))),
        out_shape=(jax.ShapeDtypeStruct((T, wq), BF16), jax.ShapeDtypeStruct((T, wq), BF16)),
        compiler_params=_params(("parallel",)),
    )(q_raw, k_nope, rest, cos_t, sin_t, q_norm, k_norm)


def _qk_bwd(dq, dk, q_raw, k_nope, rest, cos_t, sin_t, q_norm, k_norm, *, name, heads, kr_seg, tm=128):
    T = q_raw.shape[0]
    tm = _tile(T, tm, 16)

    def body(dq_ref, dk_ref, q_ref, kn_ref, kr_ref, c_ref, s_ref, gq_ref, gk_ref,
             dqr_ref, dkn_ref, dkr_ref, dgq_ref, dgk_ref):
        cos_b, sin_b = c_ref[...], s_ref[...]
        kr = kr_ref[:, 0:LANES]
        kr_ss = jnp.sum(kr * kr, axis=-1, keepdims=True)
        gq, gk = gq_ref[...], gk_ref[...]
        dgq = jnp.zeros((1, HEAD_PAD), F32)
        dgk_n = jnp.zeros((1, LANES), F32)
        dgk_r = jnp.zeros((1, LANES), F32)
        dkr = jnp.zeros((tm, LANES), F32)
        for h in range(heads):
            lo = h * HEAD_PAD
            q = q_ref[:, lo:lo + HEAD_PAD]
            rq = lax.rsqrt(jnp.sum(q * q, axis=-1, keepdims=True) / QK_HEAD + EPS)
            qhat = q * rq
            dqn = jnp.concatenate([dq_ref[:, lo:lo + LANES],
                                   _rope_t(dq_ref[:, lo + LANES:lo + HEAD_PAD], cos_b, sin_b)], axis=1)
            dgq = dgq + jnp.sum(dqn * qhat, axis=0, keepdims=True)
            dqh = dqn * gq
            dqr_ref[:, lo:lo + HEAD_PAD] = (
                rq * (dqh - qhat * (jnp.sum(dqh * qhat, axis=-1, keepdims=True) / QK_HEAD))).astype(BF16)
            kn = kn_ref[:, h * LANES:(h + 1) * LANES]
            rk = lax.rsqrt((jnp.sum(kn * kn, axis=-1, keepdims=True) + kr_ss) / QK_HEAD + EPS)
            khat_n, khat_r = kn * rk, kr * rk
            dkn_n = dk_ref[:, lo:lo + LANES]
            dkn_r = _rope_t(dk_ref[:, lo + LANES:lo + HEAD_PAD], cos_b, sin_b)
            dgk_n = dgk_n + jnp.sum(dkn_n * khat_n, axis=0, keepdims=True)
            dgk_r = dgk_r + jnp.sum(dkn_r * khat_r, axis=0, keepdims=True)
            dkh_n, dkh_r = dkn_n * gk[:, :LANES], dkn_r * gk[:, LANES:]
            proj = (jnp.sum(dkh_n * khat_n, axis=-1, keepdims=True)
                    + jnp.sum(dkh_r * khat_r, axis=-1, keepdims=True)) / QK_HEAD
            dkn_ref[:, h * LANES:(h + 1) * LANES] = (rk * (dkh_n - khat_n * proj)).astype(BF16)
            dkr = dkr + rk * (dkh_r - khat_r * proj)
        dkr_ref[:, 0:LANES] = dkr.astype(BF16)
        dkr_ref[:, LANES:HEAD_PAD] = jnp.zeros((tm, HEAD_PAD - LANES), BF16)
        dgk = jnp.concatenate([dgk_n, dgk_r], axis=1)

        @pl.when(pl.program_id(0) == 0)
        def _():
            dgq_ref[...] = dgq
            dgk_ref[...] = dgk

        @pl.when(pl.program_id(0) > 0)
        def _():
            dgq_ref[...] += dgq
            dgk_ref[...] += dgk

    wq, wk = heads * HEAD_PAD, heads * LANES
    row = lambda w: pl.BlockSpec((tm, w), lambda i: (i, 0))
    vec = pl.BlockSpec((1, HEAD_PAD), lambda i: (0, 0))
    return pl.pallas_call(
        body, name=name, grid=(T // tm,),
        in_specs=[row(wq), row(wq), row(wq), row(wk), pl.BlockSpec((tm, HEAD_PAD), lambda i: (i, kr_seg)),
                  row(LANES), row(LANES), vec, vec],
        out_specs=(row(wq), row(wk), row(HEAD_PAD), vec, vec),
        out_shape=(jax.ShapeDtypeStruct((T, wq), BF16), jax.ShapeDtypeStruct((T, wk), BF16),
                   jax.ShapeDtypeStruct((T, HEAD_PAD), BF16),
                   jax.ShapeDtypeStruct((1, HEAD_PAD), F32), jax.ShapeDtypeStruct((1, HEAD_PAD), F32)),
        compiler_params=_params(("arbitrary",)),
    )(dq, dk, q_raw, k_nope, rest, cos_t, sin_t, q_norm, k_norm)


def _causal_mask(s):
    row = lax.broadcasted_iota(jnp.int32, s.shape, 0)
    col = lax.broadcasted_iota(jnp.int32, s.shape, 1)
    return jnp.where(row >= col, s, NEG)


def _flash_fwd(q, k, v, *, name, heads, tq=384, hp=2):
    T = q.shape[0]
    tq = _tile(T, tq, LANES)
    nq = T // tq
    scale = QK_HEAD ** -0.5
    nt = (((1,), (1,)), ((), ()))

    def body(q_ref, k_ref, v_ref, o_ref, lse_ref):
        def q_block(i, carry):
            q_at = pl.ds(pl.multiple_of(i * tq, tq), tq)
            qbs = [q_ref[q_at, h * HEAD_PAD:(h + 1) * HEAD_PAD] for h in range(hp)]

            def step(j, state, masked):
                k_at = pl.ds(pl.multiple_of(j * tq, tq), tq)
                new = []
                for h in range(hp):
                    m, l, acc = state[h]
                    s = lax.dot_general(qbs[h], k_ref[k_at, h * HEAD_PAD:(h + 1) * HEAD_PAD], nt,
                                        preferred_element_type=F32) * scale
                    if masked:
                        s = _causal_mask(s)
                    m_new = jnp.maximum(m, jnp.max(s, axis=-1, keepdims=True))
                    p = jnp.exp(s - m_new)
                    alpha = jnp.exp(m - m_new)
                    l = alpha * l + jnp.sum(p, axis=-1, keepdims=True)
                    acc = alpha * acc + jnp.dot(p.astype(BF16), v_ref[k_at, h * V_HEAD:(h + 1) * V_HEAD],
                                                preferred_element_type=F32)
                    new.append((m_new, l, acc))
                return tuple(new)

            init = tuple((jnp.full((tq, 1), NEG, F32), jnp.zeros((tq, 1), F32), jnp.zeros((tq, V_HEAD), F32))
                         for _ in range(hp))
            state = lax.fori_loop(0, i, lambda j, st: step(j, st, False), init)
            state = step(i, state, True)
            for h in range(hp):
                m, l, acc = state[h]
                o_ref[q_at, h * V_HEAD:(h + 1) * V_HEAD] = (acc / l).astype(BF16)
                lse_ref[h, q_at, :] = jnp.broadcast_to(m + jnp.log(l), (tq, LANES))
            return carry

        lax.fori_loop(0, nq, q_block, 0)

    qk_spec = pl.BlockSpec((T, hp * HEAD_PAD), lambda g: (0, g))
    v_spec = pl.BlockSpec((T, hp * V_HEAD), lambda g: (0, g))
    return pl.pallas_call(
        body, name=name, grid=(heads // hp,), in_specs=[qk_spec, qk_spec, v_spec],
        out_specs=(v_spec, pl.BlockSpec((hp, T, LANES), lambda g: (g, 0, 0))),
        out_shape=(jax.ShapeDtypeStruct((T, heads * V_HEAD), BF16), jax.ShapeDtypeStruct((heads, T, LANES), F32)),
        compiler_params=_params(("parallel",)),
    )(q, k, v)


def _flash_bwd(q, k, v, o, do, lse, *, name, heads, tq=384):
    T = q.shape[0]
    tq = _tile(T, tq, LANES)
    nq = T // tq
    scale = QK_HEAD ** -0.5
    nt = (((1,), (1,)), ((), ()))
    tn = (((0,), (0,)), ((), ()))

    def body(q_ref, k_ref, v_ref, o_ref, do_ref, lse_ref, dq_ref, dk_ref, dv_ref, delta_ref):
        def fill_delta(i, carry):
            at = pl.ds(pl.multiple_of(i * tq, tq), tq)
            d = jnp.sum(o_ref[at, :].astype(F32) * do_ref[at, :].astype(F32), axis=-1, keepdims=True)
            delta_ref[at, :] = jnp.broadcast_to(d, (tq, LANES))
            dq_ref[at, :] = jnp.zeros((tq, HEAD_PAD), F32)
            return carry

        lax.fori_loop(0, nq, fill_delta, 0)

        def kv_block(j, carry):
            k_at = pl.ds(pl.multiple_of(j * tq, tq), tq)
            kb, vb = k_ref[k_at, :], v_ref[k_at, :]

            def step(i, state, masked):
                dk_acc, dv_acc = state
                q_at = pl.ds(pl.multiple_of(i * tq, tq), tq)
                qb, dob = q_ref[q_at, :], do_ref[q_at, :]
                s = lax.dot_general(qb, kb, nt, preferred_element_type=F32) * scale
                if masked:
                    s = _causal_mask(s)
                p = jnp.exp(s - lse_ref[0, q_at, :][:, 0:1])
                dv_acc = dv_acc + lax.dot_general(p.astype(BF16), dob, tn, preferred_element_type=F32)
                dp = lax.dot_general(dob, vb, nt, preferred_element_type=F32)
                ds = (p * (dp - delta_ref[q_at, :][:, 0:1]) * scale).astype(BF16)
                dk_acc = dk_acc + lax.dot_general(ds, qb, tn, preferred_element_type=F32)
                dq_ref[q_at, :] += jnp.dot(ds, kb, preferred_element_type=F32)
                return dk_acc, dv_acc

            state = step(j, (jnp.zeros((tq, HEAD_PAD), F32), jnp.zeros((tq, V_HEAD), F32)), True)
            rest = nq - 1 - j

            def two_steps(t, st):
                i0 = j + 1 + 2 * t
                return step(i0 + 1, step(i0, st, False), False)

            state = lax.fori_loop(0, rest // 2, two_steps, state)
            dk_acc, dv_acc = lax.cond(rest % 2 == 1, lambda st: step(nq - 1, st, False), lambda st: st, state)
            dk_ref[k_at, :] = dk_acc
            dv_ref[k_at, :] = dv_acc.astype(BF16)
            return carry

        lax.fori_loop(0, nq, kv_block, 0)

    qk_spec = pl.BlockSpec((T, HEAD_PAD), lambda h: (0, h))
    v_spec = pl.BlockSpec((T, V_HEAD), lambda h: (0, h))
    return pl.pallas_call(
        body, name=name, grid=(heads,),
        in_specs=[qk_spec, qk_spec, v_spec, v_spec, v_spec, pl.BlockSpec((1, T, LANES), lambda h: (h, 0, 0))],
        out_specs=(qk_spec, qk_spec, v_spec),
        out_shape=(jax.ShapeDtypeStruct((T, heads * HEAD_PAD), F32), jax.ShapeDtypeStruct((T, heads * HEAD_PAD), F32),
                   jax.ShapeDtypeStruct((T, heads * V_HEAD), BF16)),
        scratch_shapes=[pltpu.VMEM((T, LANES), F32)],
        compiler_params=_params(("parallel",)),
    )(q, k, v, o, do, lse)


def _merge_fwd(gl, pa, pb, pc, *, name, d, tm=384, tn=1024):
    T = pa.shape[0]
    tm, tn = _tile(T, tm, 16), _tile(d, tn)
    nb = d // tn

    def body(g0, g1, g2, a, b, c, o_ref):
        o_ref[...] = (jax.nn.sigmoid(g0[...]) * a[...] + jax.nn.sigmoid(g1[...]) * b[...]
                      + jax.nn.sigmoid(g2[...]) * c[...]).astype(BF16)

    gate = lambda n: pl.BlockSpec((tm, tn), lambda i, j: (i, n * nb + j))
    blk = pl.BlockSpec((tm, tn), lambda i, j: (i, j))
    return pl.pallas_call(
        body, name=name, grid=(T // tm, nb), in_specs=[gate(0), gate(1), gate(2), blk, blk, blk],
        out_specs=blk, out_shape=jax.ShapeDtypeStruct((T, d), BF16),
        compiler_params=_params(("parallel", "parallel")),
    )(gl, gl, gl, pa, pb, pc)


def _merge_bwd(dm, gl, pa, pb, pc, *, name, d, tm=384, tn=1024):
    T = pa.shape[0]
    tm, tn = _tile(T, tm, 16), _tile(d, tn)
    nb = d // tn

    def body(dm_ref, g0, g1, g2, a, b, c, da, db, dc, dg0, dg1, dg2):
        dmv = dm_ref[...]
        for g_ref, p_ref, dp_ref, dg_ref in ((g0, a, da, dg0), (g1, b, db, dg1), (g2, c, dc, dg2)):
            sg = jax.nn.sigmoid(g_ref[...])
            dp_ref[...] = (dmv * sg).astype(BF16)
            dg_ref[...] = (dmv * p_ref[...] * sg * (1.0 - sg)).astype(BF16)

    gate = lambda n: pl.BlockSpec((tm, tn), lambda i, j: (i, n * nb + j))
    blk = pl.BlockSpec((tm, tn), lambda i, j: (i, j))
    return pl.pallas_call(
        body, name=name, grid=(T // tm, nb), in_specs=[blk, gate(0), gate(1), gate(2), blk, blk, blk],
        out_specs=(blk,) * 6, out_shape=(jax.ShapeDtypeStruct((T, d), BF16),) * 6,
        compiler_params=_params(("parallel", "parallel")),
    )(dm, gl, gl, gl, pa, pb, pc)


def _loss(y, target, *, name, first, last, tm=384):
    T, d = y.shape
    tm = _tile(T, tm, 16)

    def body(y_ref, t_ref, loss_ref, dy_ref):
        i = pl.program_id(0)
        row = lax.broadcasted_iota(jnp.int32, (tm, 1), 0) + i * tm
        real = jnp.logical_and(row >= first, row < last)
        err = jnp.where(real, y_ref[...] - t_ref[...], 0.0)
        dy_ref[...] = err * (1.0 / d)
        part = jnp.broadcast_to(jnp.sum(err * err, keepdims=True).reshape(1, 1), (1, LANES))

        @pl.when(i == 0)
        def _():
            loss_ref[...] = part

        @pl.when(i > 0)
        def _():
            loss_ref[...] += part

    blk = pl.BlockSpec((tm, d), lambda i: (i, 0))
    return pl.pallas_call(
        body, name=name, grid=(T // tm,), in_specs=[blk, blk],
        out_specs=(pl.BlockSpec((1, LANES), lambda i: (0, 0)), blk),
        out_shape=(jax.ShapeDtypeStruct((1, LANES), F32), jax.ShapeDtypeStruct((T, d), F32)),
        compiler_params=_params(("arbitrary",)),
    )(y, target)


def _as3d(a):
    return a.reshape(a.shape[0], -1, a.shape[-1])


def _sum_stack(parts, *, name, out_dtype, rows=256):
    n, R, C = parts.shape
    tr = _tile(R, rows, 16)

    def body(p_ref, o_ref):
        acc = p_ref[0].astype(F32)
        for s in range(1, n):
            acc = acc + p_ref[s].astype(F32)
        o_ref[...] = acc.astype(out_dtype)

    return pl.pallas_call(
        body, name=name, grid=(R // tr,),
        in_specs=[pl.BlockSpec((n, tr, C), lambda i: (0, i, 0))],
        out_specs=pl.BlockSpec((tr, C), lambda i: (i, 0)),
        out_shape=jax.ShapeDtypeStruct((R, C), out_dtype),
        compiler_params=_params(("parallel",)),
    )(parts)


def _adamw(w, g, m, v, *, name, rows=128):
    R, C = w.shape
    tr = _tile(R, rows, 8)
    c1 = 1.0 - ADAM_B1 ** ADAM_STEP
    c2 = 1.0 - ADAM_B2 ** ADAM_STEP

    def body(w_ref, g_ref, m_ref, v_ref, d_ref, nm_ref, nv_ref):
        gv = g_ref[...]
        nm = ADAM_B1 * m_ref[...] + (1.0 - ADAM_B1) * gv
        nv = ADAM_B2 * v_ref[...] + (1.0 - ADAM_B2) * (gv * gv)
        nm_ref[...] = nm
        nv_ref[...] = nv
        d_ref[...] = -ADAM_LR * ((nm / c1) / (jnp.sqrt(nv / c2) + ADAM_EPS) + ADAM_WD * w_ref[...])

    blk = pl.BlockSpec((tr, C), lambda i: (i, 0))
    return pl.pallas_call(
        body, name=name, grid=(R // tr,), in_specs=[blk] * 4, out_specs=(blk,) * 3,
        out_shape=(jax.ShapeDtypeStruct((R, C), F32),) * 3,
        compiler_params=_params(("parallel",)),
    )(w, g, m, v)


def _adamw_layers(w, g_mine, g_theirs, m, v, *, name, rows=128):
    _, R, C = w.shape
    tr = _tile(R, rows, 8)
    c1 = 1.0 - ADAM_B1 ** ADAM_STEP
    c2 = 1.0 - ADAM_B2 ** ADAM_STEP

    def body(w_ref, gm_ref, gt_ref, m_ref, v_ref, g_ref, d_ref, nm_ref, nv_ref):
        gv = jnp.where(pl.program_id(0) == lax.axis_index("c"), gm_ref[...], gt_ref[...])
        nm = ADAM_B1 * m_ref[0] + (1.0 - ADAM_B1) * gv
        nv = ADAM_B2 * v_ref[0] + (1.0 - ADAM_B2) * (gv * gv)
        g_ref[0] = gv
        nm_ref[0] = nm
        nv_ref[0] = nv
        d_ref[0] = -ADAM_LR * ((nm / c1) / (jnp.sqrt(nv / c2) + ADAM_EPS) + ADAM_WD * w_ref[0])

    lay = pl.BlockSpec((1, tr, C), lambda l, i: (l, i, 0))
    one = pl.BlockSpec((tr, C), lambda l, i: (i, 0))
    return pl.pallas_call(
        body, name=name, grid=(2, R // tr), in_specs=[lay, one, one, lay, lay], out_specs=(lay,) * 4,
        out_shape=(jax.ShapeDtypeStruct((2, R, C), F32),) * 4,
        compiler_params=_params(("parallel", "parallel")),
    )(w, g_mine, g_theirs, m, v)


ANY = pl.BlockSpec(memory_space=pl.ANY)


def _coords():
    return lax.axis_index("x"), lax.axis_index("y"), lax.axis_index("c")


def _gather_weight(shard, *, name):
    piece = shard.shape[1:]

    def body(src, out, send_sems, recv_sems, local_sem):
        x, y, c = _coords()
        me = 2 * x + y
        chips = [(1 - x, y), (x, 1 - y), (1 - x, 1 - y)]
        sibling = (x, y, 1 - c)

        def remote(k, src_ref, dst_ref, to):
            return pltpu.make_async_remote_copy(src_ref=src_ref, dst_ref=dst_ref, send_sem=send_sems.at[k],
                                                recv_sem=recv_sems.at[k], device_id=to, device_id_type=MESH)

        mine = pltpu.make_async_copy(src.at[c], out.at[c, me], local_sem)
        mine.start()
        first = [remote(0, src.at[c], out.at[c, me], sibling)]
        first += [remote(1 + j, src.at[c], out.at[c, me], (cx, cy, c)) for j, (cx, cy) in enumerate(chips)]
        for cp in first:
            cp.start()
        passed = []
        for j, (cx, cy) in enumerate(chips):
            landed = out.at[c, 2 * cx + cy]
            remote(1 + j, landed, landed, (cx, cy, c)).wait_recv()
            cp = remote(4 + j, landed, landed, sibling)
            cp.start()
            passed.append(cp)
        other = out.at[1 - c, me]
        remote(0, other, other, sibling).wait_recv()
        for j, (cx, cy) in enumerate(chips):
            other = out.at[1 - c, 2 * cx + cy]
            remote(4 + j, other, other, sibling).wait_recv()
        for cp in first + passed:
            cp.wait_send()
        mine.wait()

    return pl.pallas_call(
        body, name=name, in_specs=[ANY], out_specs=ANY,
        out_shape=jax.ShapeDtypeStruct((2, 4) + piece, shard.dtype),
        scratch_shapes=[pltpu.SemaphoreType.DMA((7,)), pltpu.SemaphoreType.DMA((7,)), pltpu.SemaphoreType.DMA],
    )(shard)


def _swap_layers(grads, *, name):
    def body(src, out, send_sem, recv_sem):
        x, y, c = _coords()
        cp = pltpu.make_async_remote_copy(src_ref=src.at[1 - c], dst_ref=out, send_sem=send_sem, recv_sem=recv_sem,
                                          device_id=(x, y, 1 - c), device_id_type=MESH)
        cp.start()
        cp.wait()

    return pl.pallas_call(
        body, name=name, in_specs=[ANY], out_specs=ANY,
        out_shape=jax.ShapeDtypeStruct(grads.shape[1:], grads.dtype),
        scratch_shapes=[pltpu.SemaphoreType.DMA, pltpu.SemaphoreType.DMA],
    )(grads)


def _scatter_pieces(pieces, *, name):
    def body(src, out, send_sems, recv_sems, local_sem):
        x, y, c = _coords()
        me = 2 * x + y
        chips = [(1 - x, y), (x, 1 - y), (1 - x, 1 - y)]
        mine = pltpu.make_async_copy(src.at[me], out.at[me], local_sem)
        mine.start()
        cps = [pltpu.make_async_remote_copy(src_ref=src.at[2 * cx + cy], dst_ref=out.at[me], send_sem=send_sems.at[j],
                                            recv_sem=recv_sems.at[j], device_id=(cx, cy, c), device_id_type=MESH)
               for j, (cx, cy) in enumerate(chips)]
        for cp in cps:
            cp.start()
        for j, (cx, cy) in enumerate(chips):
            slot = out.at[2 * cx + cy]
            pltpu.make_async_remote_copy(src_ref=slot, dst_ref=slot, send_sem=send_sems.at[j], recv_sem=recv_sems.at[j],
                                         device_id=(cx, cy, c), device_id_type=MESH).wait_recv()
        for cp in cps:
            cp.wait_send()
        mine.wait()

    return pl.pallas_call(
        body, name=name, in_specs=[ANY], out_specs=ANY,
        out_shape=jax.ShapeDtypeStruct(pieces.shape, pieces.dtype),
        scratch_shapes=[pltpu.SemaphoreType.DMA((3,)), pltpu.SemaphoreType.DMA((3,)), pltpu.SemaphoreType.DMA],
    )(pieces)


def _swap_totals(total, *, name):
    def body(src, out, send_sem, recv_sem):
        x, y, c = _coords()
        cp = pltpu.make_async_remote_copy(src_ref=src, dst_ref=out, send_sem=send_sem, recv_sem=recv_sem,
                                          device_id=(x, y, 1 - c), device_id_type=MESH)
        cp.start()
        cp.wait()

    return pl.pallas_call(
        body, name=name, in_specs=[ANY], out_specs=ANY,
        out_shape=jax.ShapeDtypeStruct(total.shape, total.dtype),
        scratch_shapes=[pltpu.SemaphoreType.DMA, pltpu.SemaphoreType.DMA],
    )(total)


def _gather_all(block, *, name):
    def body(src, out, send_sems, recv_sems, local_sem):
        x, y, c = _coords()
        me = 4 * x + 2 * y + c
        flips = [(fx, fy, fc) for fx in (0, 1) for fy in (0, 1) for fc in (0, 1)][1:]
        mine = pltpu.make_async_copy(src, out.at[me], local_sem)
        mine.start()
        peers = [(x ^ fx, y ^ fy, c ^ fc) for fx, fy, fc in flips]
        cps = [pltpu.make_async_remote_copy(src_ref=src, dst_ref=out.at[me], send_sem=send_sems.at[k],
                                            recv_sem=recv_sems.at[k], device_id=peer, device_id_type=MESH)
               for k, peer in enumerate(peers)]
        for cp in cps:
            cp.start()
        for k, (px, py, pc) in enumerate(peers):
            slot = out.at[4 * px + 2 * py + pc]
            pltpu.make_async_remote_copy(src_ref=slot, dst_ref=slot, send_sem=send_sems.at[k], recv_sem=recv_sems.at[k],
                                         device_id=(px, py, pc), device_id_type=MESH).wait_recv()
        for cp in cps:
            cp.wait_send()
        mine.wait()

    return pl.pallas_call(
        body, name=name, in_specs=[ANY], out_specs=ANY,
        out_shape=jax.ShapeDtypeStruct((8,) + block.shape, block.dtype),
        scratch_shapes=[pltpu.SemaphoreType.DMA((7,)), pltpu.SemaphoreType.DMA((7,)), pltpu.SemaphoreType.DMA],
    )(block)


def _cols(o):
    return jnp.transpose(o, (1, 0, 2)).reshape(o.shape[1], -1)


def _uncols(full):
    return jnp.transpose(full.reshape(full.shape[0], 4, -1), (1, 0, 2))


def _rope_pad(x1, x2):
    z = jnp.zeros_like(x1)
    return jnp.concatenate([x1, z, x2, z], axis=-1)


def _head_pad(w, heads):
    r = w.reshape(w.shape[0], heads, QK_HEAD)
    half = QK_ROPE // 2
    out = jnp.concatenate([r[..., :QK_NOPE], _rope_pad(r[..., QK_NOPE:QK_NOPE + half], r[..., QK_NOPE + half:])], axis=-1)
    return out.reshape(w.shape[0], heads * HEAD_PAD)


def _head_unpad(w, heads):
    r = w.reshape(w.shape[0], heads, HEAD_PAD)
    half = QK_ROPE // 2
    out = jnp.concatenate([r[..., :QK_NOPE], r[..., QK_NOPE:QK_NOPE + half],
                           r[..., QK_NOPE + 2 * half:QK_NOPE + 3 * half]], axis=-1)
    return out.reshape(w.shape[0], heads * QK_HEAD)


class _Dims:
    def __init__(self, d, seq):
        self.d = d
        self.seq = seq
        self.t_real = N_META + seq
        self.t = -(-self.t_real // LANES) * LANES
        self.dc = d // 2
        self.dp = d // 2
        self.pg = self.dp // len(POOL_WINDOWS)
        self.heads = d // 128
        self.dff = 4 * d
        self.a_end = 3 * self.dc
        self.q_end = self.a_end + Q_LORA
        self.kv_end = self.q_end + KV_LORA
        self.kr_end = self.kv_end + QK_ROPE
        self.pool_end = self.kr_end + self.dp
        self.d_in = self.pool_end + 3 * d
        self.r_pool = 3 * self.dc
        self.r_q = self.r_pool + self.dp
        self.r_kv = self.r_q + Q_LORA
        self.r_kr = self.r_kv + KV_LORA
        self.r_width = self.r_kr + HEAD_PAD


def _layer_weights(dm, g, small):
    w_in = _cols(g["w_in"])
    half = QK_ROPE // 2
    kr = w_in[:, dm.kv_end:dm.kr_end]
    kr_p = jnp.concatenate([_rope_pad(kr[:, :half], kr[:, half:]), jnp.zeros((dm.d, HEAD_PAD - LANES), BF16)], axis=1)
    w_ukv = _cols(g["w_ukv"]).reshape(KV_LORA, dm.heads, QK_NOPE + V_HEAD)
    return dict(
        wg=w_in[:, dm.pool_end:],
        wr=jnp.concatenate([w_in[:, :dm.a_end], w_in[:, dm.kr_end:dm.pool_end], w_in[:, dm.a_end:dm.kv_end], kr_p], axis=1),
        wuq=_head_pad(_cols(g["w_uq"]), dm.heads),
        wkn=w_ukv[:, :, :QK_NOPE].reshape(KV_LORA, dm.heads * QK_NOPE),
        wv=w_ukv[:, :, QK_NOPE:].reshape(KV_LORA, dm.heads * V_HEAD),
        wp=jnp.transpose(g["pool_w"], (1, 0, 2, 3)).reshape(len(POOL_WINDOWS), dm.pg, dm.pg),
        wba=_cols(g["w_branch_a"]), wbb=g["w_branch_b"].reshape(-1, dm.d), wbc=_cols(g["w_branch_c"]),
        wo=g["w_o"].reshape(-1, dm.d), wup=_cols(g["w_up"]), wdn=g["w_down"].reshape(-1, dm.d),
        conv_w=small["conv_w"],
        attn_norm=small["attn_norm"][None], mlp_norm=small["mlp_norm"][None],
        q_lat_norm=small["q_lat_norm"][None], kv_lat_norm=small["kv_lat_norm"][None],
        q_norm=_head_pad(small["q_norm"][None], 1), k_norm=_head_pad(small["k_norm"][None], 1),
        pool_scale=small["pool_scale"][None],
    )


def _layer_grad_pieces(dm, dw):
    half = QK_ROPE // 2
    dwr, dwg = dw["wr"], dw["wg"]
    d_in = jnp.concatenate([
        dwr[:, :dm.r_pool], dwr[:, dm.r_q:dm.r_kr], dwr[:, dm.r_kr:dm.r_kr + half],
        dwr[:, dm.r_kr + 2 * half:dm.r_kr + 3 * half], dwr[:, dm.r_pool:dm.r_q], dwg], axis=1)
    d_ukv = jnp.concatenate([dw["wkn"].reshape(KV_LORA, dm.heads, QK_NOPE),
                             dw["wv"].reshape(KV_LORA, dm.heads, V_HEAD)], axis=-1).reshape(KV_LORA, -1)
    rows = lambda a: a.reshape((4, a.shape[0] // 4) + a.shape[1:])
    out = dict(
        w_in=_uncols(d_in), w_uq=_uncols(_head_unpad(dw["wuq"], dm.heads)), w_ukv=_uncols(d_ukv),
        pool_w=jnp.transpose(dw["wp"].reshape(len(POOL_WINDOWS), 4, dm.pg // 4, dm.pg), (1, 0, 2, 3)),
        w_branch_a=_uncols(dw["wba"]), w_branch_b=rows(dw["wbb"]), w_branch_c=_uncols(dw["wbc"]),
        w_o=rows(dw["wo"]), w_up=_uncols(dw["wup"]), w_down=rows(dw["wdn"]),
    )
    return {k: v.astype(BF16) for k, v in out.items()}


def _layer_fwd(dm, W, x, cos_t, sin_t, tag):
    n = lambda s: f"{s}_{tag}"
    h = _rms_fwd(x, W["attn_norm"], name=n("attn_norm"))
    gl = _mm(h, W["wg"], name=n("proj_gates"))
    rest = _mm(h, W["wr"], name=n("proj_rest"))
    y_a = _conv_fwd(rest, W["conv_w"], name=n("conv"), dc=dm.dc)
    y_c = _pool_fwd(rest, W["wp"], W["pool_scale"], name=n("pool"), seg0=dm.r_pool // dm.pg, pg=dm.pg)
    q_lat = _rms_fwd(rest, W["q_lat_norm"], name=n("q_lat_norm"), width=Q_LORA, seg=dm.r_q // Q_LORA)
    kv_lat = _rms_fwd(rest, W["kv_lat_norm"], name=n("kv_lat_norm"), width=KV_LORA, seg=dm.r_kv // KV_LORA)
    q_raw = _mm(q_lat, W["wuq"], name=n("up_q"))
    k_nope = _mm(kv_lat, W["wkn"], name=n("up_k"))
    v = _mm(kv_lat, W["wv"], name=n("up_v"), out_dtype=BF16)
    q, k = _qk_fwd(q_raw, k_nope, rest, cos_t, sin_t, W["q_norm"], W["k_norm"], name=n("qk_norm_rope"),
                   heads=dm.heads, kr_seg=dm.r_kr // HEAD_PAD)
    y_b, lse = _flash_fwd(q, k, v, name=n("attention"), heads=dm.heads)
    pa = _mm(y_a, W["wba"], name=n("branch_a"))
    pb = _mm(y_b, W["wbb"], name=n("branch_b"))
    pc = _mm(y_c, W["wbc"], name=n("branch_c"))
    merged = _merge_fwd(gl, pa, pb, pc, name=n("merge"), d=dm.d)
    x1 = _mm(merged, W["wo"], name=n("out_proj"), add=x)
    h2 = _rms_fwd(x1, W["mlp_norm"], name=n("mlp_norm"))
    up, act = _mm(h2, W["wup"], name=n("mlp_up"), epi="relu2")
    x2 = _mm(act, W["wdn"], name=n("mlp_down"), add=x1, tk=2048)
    saved = dict(x=x, h=h, gl=gl, rest=rest, y_a=y_a, y_c=y_c, q_lat=q_lat, kv_lat=kv_lat, q_raw=q_raw, k_nope=k_nope,
                 v=v, q=q, k=k, y_b=y_b, lse=lse, pa=pa, pb=pb, pc=pc, merged=merged, x1=x1, h2=h2, up=up, act=act)
    return x2, saved


def _layer_bwd(dm, W, S, dx2, cos_t, sin_t, tag):
    n = lambda s: f"{s}_{tag}"
    dw, ds = {}, {}
    dup = _mm(dx2, W["wdn"], name=n("d_mlp_down"), tb=True, aux=S["up"], epi="drelu2", out_dtype=BF16)
    dw["wdn"] = _mm(S["act"], dx2, name=n("dw_mlp_down"), ta=True, tm=1024, tk=1408)
    dh2 = _mm(dup, W["wup"], name=n("d_mlp_up"), tb=True, tk=2048)
    dw["wup"] = _mm(S["h2"], dup, name=n("dw_mlp_up"), ta=True, tm=1024, tk=1408)
    dx1, ds["mlp_norm"] = _rms_bwd(dh2, S["x1"], W["mlp_norm"], name=n("d_mlp_norm"), res=dx2)
    dmerged = _mm(dx1, W["wo"], name=n("d_out_proj"), tb=True)
    dw["wo"] = _mm(S["merged"], dx1, name=n("dw_out_proj"), ta=True, tm=1024, tk=1408)
    dpa, dpb, dpc, dg0, dg1, dg2 = _merge_bwd(dmerged, S["gl"], S["pa"], S["pb"], S["pc"], name=n("d_merge"), d=dm.d)
    dgl = jnp.concatenate([dg0, dg1, dg2], axis=1)
    dy_a = _mm(dpa, W["wba"], name=n("d_branch_a"), tb=True)
    dw["wba"] = _mm(S["y_a"], dpa, name=n("dw_branch_a"), ta=True, tm=1024, tk=1408)
    dy_b = _mm(dpb, W["wbb"], name=n("d_branch_b"), tb=True, out_dtype=BF16)
    dw["wbb"] = _mm(S["y_b"], dpb, name=n("dw_branch_b"), ta=True, tm=1024, tk=1408)
    dy_c = _mm(dpc, W["wbc"], name=n("d_branch_c"), tb=True)
    dw["wbc"] = _mm(S["y_c"], dpc, name=n("dw_branch_c"), ta=True, tm=1024, tk=1408)
    dq, dk, dv = _flash_bwd(S["q"], S["k"], S["v"], S["y_b"], dy_b, S["lse"], name=n("d_attention"), heads=dm.heads)
    dq_raw, dk_nope, dk_rope, dgq, dgk = _qk_bwd(
        dq, dk, S["q_raw"], S["k_nope"], S["rest"], cos_t, sin_t, W["q_norm"], W["k_norm"], name=n("d_qk_norm_rope"),
        heads=dm.heads, kr_seg=dm.r_kr // HEAD_PAD)
    ds["q_norm"] = _head_unpad(dgq, 1)
    ds["k_norm"] = _head_unpad(dgk, 1)
    dq_lat_n = _mm(dq_raw, W["wuq"], name=n("d_up_q"), tb=True, tk=2048)
    dw["wuq"] = _mm(S["q_lat"], dq_raw, name=n("dw_up_q"), ta=True, tm=512, tk=1408)
    dkv_v = _mm(dv, W["wv"], name=n("d_up_v"), tb=True)
    dkv_lat_n = _mm(dk_nope, W["wkn"], name=n("d_up_k"), tb=True, add=dkv_v)
    dw["wkn"] = _mm(S["kv_lat"], dk_nope, name=n("dw_up_k"), ta=True, tm=512, tk=1408)
    dw["wv"] = _mm(S["kv_lat"], dv, name=n("dw_up_v"), ta=True, tm=512, tk=1408)
    dq_lat, ds["q_lat_norm"] = _rms_bwd(dq_lat_n, S["rest"], W["q_lat_norm"], name=n("d_q_lat_norm"), width=Q_LORA,
                                        seg=dm.r_q // Q_LORA, out_dtype=BF16)
    dkv_lat, ds["kv_lat_norm"] = _rms_bwd(dkv_lat_n, S["rest"], W["kv_lat_norm"], name=n("d_kv_lat_norm"), width=KV_LORA,
                                          seg=dm.r_kv // KV_LORA, out_dtype=BF16)
    du, db, dc, ds["conv_w"] = _conv_bwd(S["rest"], W["conv_w"], dy_a, name=n("d_conv"), dc=dm.dc)
    dpool, dw["wp"], ds["pool_scale"] = _pool_bwd(S["rest"], W["wp"], W["pool_scale"], dy_c, name=n("d_pool"),
                                                  seg0=dm.r_pool // dm.pg, pg=dm.pg)
    drest = jnp.concatenate([du, db, dc, dpool, dq_lat, dkv_lat, dk_rope], axis=1)
    dh_g = _mm(dgl, W["wg"], name=n("d_proj_gates"), tb=True, tk=2048)
    dh = _mm(drest, W["wr"], name=n("d_proj_rest"), tb=True, add=dh_g, tk=1792)
    dw["wg"] = _mm(S["h"], dgl, name=n("dw_proj_gates"), ta=True, tm=1024, tk=1408)
    dw["wr"] = _mm(S["h"], drest, name=n("dw_proj_rest"), ta=True, tm=1024, tk=1408)
    dx, ds["attn_norm"] = _rms_bwd(dh, S["x"], W["attn_norm"], name=n("d_attn_norm"), res=dx1)
    return dx, dw, ds


BIG = ("w_in", "w_uq", "w_ukv", "pool_w", "w_branch_a", "w_branch_b", "w_branch_c", "w_o", "w_up", "w_down")
REPLICATED = ("attn_norm", "q_lat_norm", "kv_lat_norm", "q_norm", "k_norm", "pool_scale", "mlp_norm")
WEIGHTS = ("meta_tokens", "attn_norm", "w_in", "conv_w", "q_lat_norm", "kv_lat_norm", "w_uq", "w_ukv", "q_norm",
           "k_norm", "pool_w", "pool_scale", "w_branch_a", "w_branch_b", "w_branch_c", "w_o", "mlp_norm", "w_up",
           "w_down")


def _pack(arrays):
    flat = jnp.concatenate([a.reshape(-1).astype(F32) for a in arrays])
    pad = (-flat.shape[0]) % (8 * LANES)
    return jnp.pad(flat, (0, pad)).reshape(-1, LANES)


def _unpack(flat, shapes):
    out, pos = [], 0
    flat = flat.reshape(-1)
    for shp in shapes:
        size = math.prod(shp)
        out.append(flat[pos:pos + size].reshape(shp))
        pos += size
    return out


def _update(w, g, m, v, name):
    shp = w.shape
    to2 = lambda a: a.reshape(-1, shp[-1])
    delta, nm, nv = _adamw(to2(w), to2(g), to2(m), to2(v), name=name)
    return delta.reshape(shp), nm.reshape(shp), nv.reshape(shp)


def _step(args):
    x = args["x"][0]
    seq, d = x.shape
    dm = _Dims(d, seq)
    xi, yi, ci = _coords()
    chip = 2 * xi + yi

    gathered = {k: _gather_weight(args[k].astype(BF16), name=f"gather_{k}") for k in BIG}
    small_w = _gather_all(_pack([args["conv_w"], args["meta_tokens"]]), name="gather_small_weights")
    conv_shape, meta_shape = args["conv_w"].shape, args["meta_tokens"].shape
    per_chip = [_unpack(small_w[2 * j], [conv_shape, meta_shape]) for j in range(4)]
    conv_full = jnp.concatenate([p[0] for p in per_chip], axis=-1)
    meta_full = jnp.concatenate([p[1] for p in per_chip], axis=-1)

    layers = []
    for l in range(2):
        small = {k: args[k][l] for k in REPLICATED}
        small["conv_w"] = conv_full[l]
        layers.append(_layer_weights(dm, {k: gathered[k][l] for k in BIG}, small))

    pos = jnp.arange(dm.t, dtype=F32)
    inv = ROPE_THETA ** (-jnp.arange(0, QK_ROPE, 2, dtype=F32) / QK_ROPE)
    ang = pos[:, None] * inv[None, :]
    cos_t = _rope_pad(jnp.cos(ang), jnp.cos(ang))
    sin_t = _rope_pad(-jnp.sin(ang), jnp.sin(ang))
    tail = jnp.zeros((dm.t - dm.t_real, d), F32)
    h0 = jnp.concatenate([meta_full, x, tail], axis=0)
    target = jnp.concatenate([jnp.zeros((N_META, d), F32), args["loss_target"][0], tail], axis=0)

    h1, saved0 = _layer_fwd(dm, layers[0], h0, cos_t, sin_t, "l0")
    h2, saved1 = _layer_fwd(dm, layers[1], h1, cos_t, sin_t, "l1")
    sq, dy = _loss(h2, target, name="loss_head", first=N_META, last=dm.t_real)
    loss = lax.psum(0.5 / d * sq[0, 0], ("x", "y", "c"))
    dh1, dw1, ds1 = _layer_bwd(dm, layers[1], saved1, dy, cos_t, sin_t, "l1")
    dh0, dw0, ds0 = _layer_bwd(dm, layers[0], saved0, dh1, cos_t, sin_t, "l0")
    grad_x = dh0[N_META:dm.t_real][None]

    pieces = [_layer_grad_pieces(dm, dw0), _layer_grad_pieces(dm, dw1)]
    grads = {}
    for k in BIG:
        both = jnp.stack([pieces[0][k], pieces[1][k]])
        theirs = _swap_layers(both, name=f"swap_{k}")
        mine = lax.dynamic_index_in_dim(both, ci, 0, keepdims=False)
        pair = _sum_stack(_as3d(jnp.stack([mine, theirs]).reshape((2, -1) + mine.shape[-1:])),
                          name=f"pair_sum_{k}", out_dtype=BF16).reshape(mine.shape)
        landed = _scatter_pieces(pair, name=f"scatter_{k}")
        total = _sum_stack(_as3d(landed), name=f"chip_sum_{k}", out_dtype=F32)
        grads[k] = (total, _swap_totals(total, name=f"swap_total_{k}"))

    small_names = REPLICATED + ("conv_w",)
    small_parts = [jnp.stack([ds0[k].reshape(ds0[k].shape[-2:] if k == "conv_w" else (-1,)),
                              ds1[k].reshape(ds1[k].shape[-2:] if k == "conv_w" else (-1,))]) for k in small_names]
    small_parts.append(dh0[:N_META])
    small_all = _gather_all(_pack(small_parts), name="gather_small_grads")
    small_sum = _sum_stack(small_all, name="sum_small_grads", out_dtype=F32)
    small_g = dict(zip(small_names + ("meta_tokens",), _unpack(small_sum, [p.shape for p in small_parts])))
    for k in REPLICATED:
        grads[k] = small_g[k]
    dcw = conv_shape[-1]
    grads["conv_w"] = lax.dynamic_slice_in_dim(small_g["conv_w"], chip * dcw, dcw, axis=2)
    dmeta = meta_shape[-1]
    grads["meta_tokens"] = lax.dynamic_slice_in_dim(small_g["meta_tokens"], chip * dmeta, dmeta, axis=1)

    delta, new_m, new_v = {}, {}, {}
    for k in WEIGHTS:
        shp = args[k].shape
        if k in BIG:
            lay = lambda a: a.reshape(2, -1, shp[-1])
            mine, theirs = grads[k]
            out = _adamw_layers(lay(args[k]), mine, theirs, lay(args["m_" + k]), lay(args["v_" + k]), name=f"adamw_{k}")
            grads[k], delta[k], new_m[k], new_v[k] = (o.reshape(shp) for o in out)
        else:
            grads[k] = grads[k].reshape(shp)
            delta[k], new_m[k], new_v[k] = _update(args[k], grads[k], args["m_" + k], args["v_" + k], f"adamw_{k}")
    return (loss, grad_x, *[grads[k] for k in WEIGHTS], *[delta[k] for k in WEIGHTS],
            *[new_m[k] for k in WEIGHTS], *[new_v[k] for k in WEIGHTS])


def kernel(x, meta_tokens, attn_norm, w_in, conv_w, q_lat_norm, kv_lat_norm, w_uq, w_ukv, q_norm, k_norm, pool_w, pool_scale, w_branch_a, w_branch_b, w_branch_c, w_o, mlp_norm, w_up, w_down, loss_target, m_meta_tokens, m_attn_norm, m_w_in, m_conv_w, m_q_lat_norm, m_kv_lat_norm, m_w_uq, m_w_ukv, m_q_norm, m_k_norm, m_pool_w, m_pool_scale, m_w_branch_a, m_w_branch_b, m_w_branch_c, m_w_o, m_mlp_norm, m_w_up, m_w_down, v_meta_tokens, v_attn_norm, v_w_in, v_conv_w, v_q_lat_norm, v_kv_lat_norm, v_w_uq, v_w_ukv, v_q_norm, v_k_norm, v_pool_w, v_pool_scale, v_w_branch_a, v_w_branch_b, v_w_branch_c, v_w_o, v_mlp_norm, v_w_up, v_w_down):
    return _step(dict(locals()))
```

```python
import functools
import math

import jax
import jax.numpy as jnp
from jax import lax
from jax.experimental import pallas as pl
from jax.experimental.pallas import tpu as pltpu

F32 = jnp.float32
BF16 = jnp.bfloat16
MESH = pl.DeviceIdType.MESH

EPS = 1e-6
N_META = 16
QK_NOPE = 128
QK_ROPE = 64
QK_HEAD = QK_NOPE + QK_ROPE
V_HEAD = 128
HEAD_PAD = 256
Q_LORA = 512
KV_LORA = 512
ROPE_THETA = 10000.0
POOL_WINDOWS = (2, 4, 8, 16)
HALO = 16
LANES = 128
ADAM_LR = 0.001
ADAM_B1 = 0.9
ADAM_B2 = 0.999
ADAM_EPS = 1e-08
ADAM_WD = 0.01
ADAM_STEP = 10
VMEM_LIMIT = 52 * 1024 * 1024
NEG = -1e30


def _tile(n, target, mult=LANES):
    best = None
    for t in range(mult, min(n, target) + 1, mult):
        if n % t == 0:
            best = t
    return n if best is None else best


def _params(sem=None):
    return pltpu.CompilerParams(dimension_semantics=sem, vmem_limit_bytes=VMEM_LIMIT)


def _mm(a, b, *, name, ta=False, tb=False, add=None, aux=None, epi=None, out_dtype=F32,
        tm=704, tn=1024, tk=None):
    if ta:
        K, M = a.shape
    else:
        M, K = a.shape
    if tb:
        N, kb = b.shape
    else:
        kb, N = b.shape
    assert K == kb, (a.shape, b.shape, ta, tb)
    tm = _tile(M, tm, LANES if ta else 16)
    tn = _tile(N, tn, LANES)
    tk = K if tk is None else _tile(K, tk, LANES if (not ta or tb) else 16)
    nk = K // tk
    grid = (M // tm, N // tn, nk)

    a_spec = pl.BlockSpec((tk, tm), lambda i, j, k: (k, i)) if ta else pl.BlockSpec((tm, tk), lambda i, j, k: (i, k))
    b_spec = pl.BlockSpec((tn, tk), lambda i, j, k: (j, k)) if tb else pl.BlockSpec((tk, tn), lambda i, j, k: (k, j))
    o_spec = pl.BlockSpec((tm, tn), lambda i, j, k: (i, j))
    in_specs = [a_spec, b_spec]
    operands = [a, b]
    if add is not None:
        in_specs.append(o_spec)
        operands.append(add)
    if aux is not None:
        in_specs.append(o_spec)
        operands.append(aux)
    if epi == "relu2":
        out_shape = (jax.ShapeDtypeStruct((M, N), BF16), jax.ShapeDtypeStruct((M, N), BF16))
        out_specs = (o_spec, o_spec)
    else:
        out_shape = jax.ShapeDtypeStruct((M, N), out_dtype)
        out_specs = o_spec
    dims = (((0 if ta else 1,), (1 if tb else 0,)), ((), ()))
    has_add, has_aux = add is not None, aux is not None

    def body(*refs):
        a_ref, b_ref = refs[0], refs[1]
        pos = 2
        add_ref = aux_ref = None
        if has_add:
            add_ref = refs[pos]
            pos += 1
        if has_aux:
            aux_ref = refs[pos]
            pos += 1
        n_out = 2 if epi == "relu2" else 1
        out_refs = refs[pos:pos + n_out]
        acc_ref = refs[pos + n_out] if nk > 1 else None

        part = lax.dot_general(a_ref[...].astype(BF16), b_ref[...].astype(BF16), dims,
                               preferred_element_type=F32)

        def finish(acc):
            if has_add:
                acc = acc + add_ref[...].astype(F32)
            if epi == "relu2":
                r = jnp.maximum(acc, 0.0)
                out_refs[0][...] = acc.astype(BF16)
                out_refs[1][...] = (r * r).astype(BF16)
            elif epi == "drelu2":
                u = aux_ref[...].astype(F32)
                out_refs[0][...] = (acc * (2.0 * jnp.maximum(u, 0.0))).astype(out_dtype)
            else:
                out_refs[0][...] = acc.astype(out_dtype)

        if nk == 1:
            finish(part)
        else:
            k = pl.program_id(2)

            @pl.when(k == 0)
            def _():
                acc_ref[...] = part

            @pl.when(k > 0)
            def _():
                acc_ref[...] += part

            @pl.when(k == nk - 1)
            def _():
                finish(acc_ref[...])

    scratch = [pltpu.VMEM((tm, tn), F32)] if nk > 1 else []
    return pl.pallas_call(
        body, name=name, grid=grid, in_specs=in_specs, out_specs=out_specs, out_shape=out_shape,
        scratch_shapes=scratch, compiler_params=_params(("parallel", "parallel", "arbitrary")),
    )(*operands)


def _rms_fwd(x, g, *, name, width=None, seg=0, tm=384):
    T = x.shape[0]
    width = x.shape[1] if width is None else width
    tm = _tile(T, tm, 16)

    def body(x_ref, g_ref, o_ref):
        xf = x_ref[...].astype(F32)
        r = lax.rsqrt(jnp.mean(xf * xf, axis=-1, keepdims=True) + EPS)
        o_ref[...] = (xf * r * g_ref[...]).astype(BF16)

    return pl.pallas_call(
        body, name=name, grid=(T // tm,),
        in_specs=[pl.BlockSpec((tm, width), lambda i: (i, seg)), pl.BlockSpec((1, width), lambda i: (0, 0))],
        out_specs=pl.BlockSpec((tm, width), lambda i: (i, 0)),
        out_shape=jax.ShapeDtypeStruct((T, width), BF16),
        compiler_params=_params(("parallel",)),
    )(x, g)


def _rms_bwd(dy, x, g, *, name, width=None, seg=0, res=None, out_dtype=F32, tm=384):
    T = x.shape[0]
    width = x.shape[1] if width is None else width
    tm = _tile(T, tm, 16)
    has_res = res is not None

    def body(*refs):
        dy_ref, x_ref, g_ref = refs[:3]
        res_ref = refs[3] if has_res else None
        dx_ref, dg_ref = refs[-2:]
        xf = x_ref[...].astype(F32)
        dyf = dy_ref[...].astype(F32)
        r = lax.rsqrt(jnp.mean(xf * xf, axis=-1, keepdims=True) + EPS)
        xhat = xf * r
        dyh = dyf * g_ref[...]
        dx = r * (dyh - xhat * jnp.mean(dyh * xhat, axis=-1, keepdims=True))
        if has_res:
            dx = dx + res_ref[...].astype(F32)
        dx_ref[...] = dx.astype(out_dtype)
        part = jnp.sum(dyf * xhat, axis=0, keepdims=True)

        @pl.when(pl.program_id(0) == 0)
        def _():
            dg_ref[...] = part

        @pl.when(pl.program_id(0) > 0)
        def _():
            dg_ref[...] += part

    row = pl.BlockSpec((tm, width), lambda i: (i, 0))
    in_specs = [row, pl.BlockSpec((tm, width), lambda i: (i, seg)), pl.BlockSpec((1, width), lambda i: (0, 0))]
    operands = [dy, x, g]
    if has_res:
        in_specs.append(row)
        operands.append(res)
    return pl.pallas_call(
        body, name=name, grid=(T // tm,), in_specs=in_specs,
        out_specs=(row, pl.BlockSpec((1, width), lambda i: (0, 0))),
        out_shape=(jax.ShapeDtypeStruct((T, width), out_dtype), jax.ShapeDtypeStruct((1, width), F32)),
        compiler_params=_params(("arbitrary",)),
    )(*operands)


def _down(ext, k):
    return pltpu.roll(ext, k, 0)


def _up(ext, k):
    return pltpu.roll(ext, ext.shape[0] - k, 0)


def _pre_halo(ref, r, R):
    start = pl.multiple_of(jnp.maximum(r * R - HALO, 0), 8)
    keep = (r > 0).astype(F32)
    return ref[pl.ds(start, HALO), :].astype(F32) * keep


def _post_halo(ref, r, R, n_chunks):
    start = pl.multiple_of(jnp.minimum(r * R + R, (n_chunks - 1) * R + R - HALO), 8)
    keep = (r < n_chunks - 1).astype(F32)
    return ref[pl.ds(start, HALO), :].astype(F32) * keep


def _chunk(ref, r, R):
    return ref[pl.ds(pl.multiple_of(r * R, 8), R), :].astype(F32)


def _conv_fwd(rest, conv_w, *, name, dc, tc=128, rows=1056):
    T = rest.shape[0]
    tc = _tile(dc, tc)
    nb = dc // tc
    R = _tile(T, rows, 16)
    n_chunks = T // R

    def body(u_ref, b_ref, c_ref, w_ref, y_ref):
        w0, w1, w2 = w_ref[0:1, :], w_ref[1:2, :], w_ref[2:3, :]

        def chunk(r, carry):
            cu = _chunk(c_ref, r, R) * _chunk(u_ref, r, R)
            ext = jnp.concatenate([_pre_halo(c_ref, r, R) * _pre_halo(u_ref, r, R), cu], axis=0)
            conv = w0 * _down(ext, 2)[HALO:] + w1 * _down(ext, 1)[HALO:] + w2 * cu
            y_ref[pl.ds(pl.multiple_of(r * R, 8), R), :] = (_chunk(b_ref, r, R) * conv).astype(BF16)
            return carry

        lax.fori_loop(0, n_chunks, chunk, 0)

    col = lambda off: pl.BlockSpec((T, tc), lambda j: (0, off * nb + j))
    return pl.pallas_call(
        body, name=name, grid=(nb,),
        in_specs=[col(0), col(1), col(2), pl.BlockSpec((3, tc), lambda j: (0, j))],
        out_specs=pl.BlockSpec((T, tc), lambda j: (0, j)),
        out_shape=jax.ShapeDtypeStruct((T, dc), BF16),
        compiler_params=_params(("parallel",)),
    )(rest, rest, rest, conv_w)


def _conv_bwd(rest, conv_w, dy, *, name, dc, tc=128, rows=1056):
    T = rest.shape[0]
    tc = _tile(dc, tc)
    nb = dc // tc
    R = _tile(T, rows, 16)
    n_chunks = T // R

    def body(u_ref, b_ref, c_ref, w_ref, dy_ref, du_ref, db_ref, dc_ref, dw_ref):
        w0, w1, w2 = w_ref[0:1, :], w_ref[1:2, :], w_ref[2:3, :]

        def chunk(r, carry):
            a0, a1, a2 = carry
            u, b, c = _chunk(u_ref, r, R), _chunk(b_ref, r, R), _chunk(c_ref, r, R)
            dy_c = _chunk(dy_ref, r, R)
            cu = c * u
            ext = jnp.concatenate([_pre_halo(c_ref, r, R) * _pre_halo(u_ref, r, R), cu], axis=0)
            cu1, cu2 = _down(ext, 1)[HALO:], _down(ext, 2)[HALO:]
            conv = w0 * cu2 + w1 * cu1 + w2 * cu
            dconv = dy_c * b
            dext = jnp.concatenate(
                [dconv, _post_halo(dy_ref, r, R, n_chunks) * _post_halo(b_ref, r, R, n_chunks)], axis=0)
            dcu = w2 * dconv + w1 * _up(dext, 1)[:R] + w0 * _up(dext, 2)[:R]
            rows_at = pl.ds(pl.multiple_of(r * R, 8), R)
            db_ref[rows_at, :] = (dy_c * conv).astype(BF16)
            du_ref[rows_at, :] = (dcu * c).astype(BF16)
            dc_ref[rows_at, :] = (dcu * u).astype(BF16)
            return (a0 + jnp.sum(dconv * cu2, axis=0, keepdims=True),
                    a1 + jnp.sum(dconv * cu1, axis=0, keepdims=True),
                    a2 + jnp.sum(dconv * cu, axis=0, keepdims=True))

        zero = jnp.zeros((1, tc), F32)
        a0, a1, a2 = lax.fori_loop(0, n_chunks, chunk, (zero, zero, zero))
        dw_ref[0:1, :] = a0
        dw_ref[1:2, :] = a1
        dw_ref[2:3, :] = a2

    col = lambda off: pl.BlockSpec((T, tc), lambda j: (0, off * nb + j))
    own = pl.BlockSpec((T, tc), lambda j: (0, j))
    return pl.pallas_call(
        body, name=name, grid=(nb,),
        in_specs=[col(0), col(1), col(2), pl.BlockSpec((3, tc), lambda j: (0, j)), own],
        out_specs=(own, own, own, pl.BlockSpec((3, tc), lambda j: (0, j))),
        out_shape=(jax.ShapeDtypeStruct((T, dc), BF16),) * 3 + (jax.ShapeDtypeStruct((3, dc), F32),),
        compiler_params=_params(("parallel",)),
    )(rest, rest, rest, conv_w, dy)


def _window_count(r, R, n_rows, w, first_row_offset):
    t = lax.broadcasted_iota(jnp.int32, (n_rows, 1), 0) + (r * R + first_row_offset)
    return jnp.minimum(t + 1, w).astype(F32)


def _pool_fwd(rest, pool_w, pool_scale, *, name, seg0, pg, rows=1056):
    T = rest.shape[0]
    R = _tile(T, rows, 16)
    n_chunks = T // R
    n_groups = len(POOL_WINDOWS)

    def body(x_ref, w_ref, s_ref, y_ref):
        def run(window):
            def chunk(r, carry):
                g = _chunk(x_ref, r, R)
                s = jnp.concatenate([_pre_halo(x_ref, r, R), g], axis=0)
                k = 1
                while k < window:
                    s = s + _down(s, k)
                    k *= 2
                pooled = s[HALO:] / _window_count(r, R, R, window, 0) - g
                mixed = jnp.dot(pooled.astype(BF16), w_ref[0], preferred_element_type=F32)
                y_ref[pl.ds(pl.multiple_of(r * R, 8), R), :] = (mixed * s_ref[...]).astype(BF16)
                return carry

            lax.fori_loop(0, n_chunks, chunk, 0)

        for gi, window in enumerate(POOL_WINDOWS):
            pl.when(pl.program_id(0) == gi)(functools.partial(run, window))

    return pl.pallas_call(
        body, name=name, grid=(n_groups,),
        in_specs=[pl.BlockSpec((T, pg), lambda g: (0, seg0 + g)),
                  pl.BlockSpec((1, pg, pg), lambda g: (g, 0, 0)),
                  pl.BlockSpec((1, pg), lambda g: (0, g))],
        out_specs=pl.BlockSpec((T, pg), lambda g: (0, g)),
        out_shape=jax.ShapeDtypeStruct((T, n_groups * pg), BF16),
        compiler_params=_params(("parallel",)),
    )(rest, pool_w, pool_scale)


def _pool_bwd(rest, pool_w, pool_scale, dy, *, name, seg0, pg, rows=1056):
    T = rest.shape[0]
    R = _tile(T, rows, 16)
    n_chunks = T // R
    n_groups = len(POOL_WINDOWS)

    def body(x_ref, w_ref, s_ref, dy_ref, dx_ref, dw_ref, ds_ref):
        def run(window):
            def chunk(r, carry):
                dw_acc, ds_acc = carry
                g = _chunk(x_ref, r, R)
                s = jnp.concatenate([_pre_halo(x_ref, r, R), g], axis=0)
                k = 1
                while k < window:
                    s = s + _down(s, k)
                    k *= 2
                pooled = (s[HALO:] / _window_count(r, R, R, window, 0) - g).astype(BF16)
                mixed = jnp.dot(pooled, w_ref[0], preferred_element_type=F32)
                dy_c = _chunk(dy_ref, r, R)
                dm_ext = (jnp.concatenate([dy_c, _post_halo(dy_ref, r, R, n_chunks)], axis=0)
                          * s_ref[...]).astype(BF16)
                dpool_ext = lax.dot_general(dm_ext, w_ref[0], (((1,), (1,)), ((), ())),
                                            preferred_element_type=F32)
                a = dpool_ext / _window_count(r, R, R + HALO, window, 0)
                k = 1
                while k < window:
                    a = a + _up(a, k)
                    k *= 2
                dx_ref[pl.ds(pl.multiple_of(r * R, 8), R), :] = (a[:R] - dpool_ext[:R]).astype(BF16)
                dw_acc = dw_acc + lax.dot_general(pooled, dm_ext[:R], (((0,), (0,)), ((), ())),
                                                  preferred_element_type=F32)
                ds_acc = ds_acc + jnp.sum(dy_c * mixed, axis=0, keepdims=True)
                return dw_acc, ds_acc

            dw_acc, ds_acc = lax.fori_loop(0, n_chunks, chunk,
                                           (jnp.zeros((pg, pg), F32), jnp.zeros((1, pg), F32)))
            dw_ref[0] = dw_acc
            ds_ref[...] = ds_acc

        for gi, window in enumerate(POOL_WINDOWS):
            pl.when(pl.program_id(0) == gi)(functools.partial(run, window))

    own = pl.BlockSpec((T, pg), lambda g: (0, g))
    return pl.pallas_call(
        body, name=name, grid=(n_groups,),
        in_specs=[pl.BlockSpec((T, pg), lambda g: (0, seg0 + g)),
                  pl.BlockSpec((1, pg, pg), lambda g: (g, 0, 0)),
                  pl.BlockSpec((1, pg), lambda g: (0, g)), own],
        out_specs=(own, pl.BlockSpec((1, pg, pg), lambda g: (g, 0, 0)), pl.BlockSpec((1, pg), lambda g: (0, g))),
        out_shape=(jax.ShapeDtypeStruct((T, n_groups * pg), BF16),
                   jax.ShapeDtypeStruct((n_groups, pg, pg), F32),
                   jax.ShapeDtypeStruct((1, n_groups * pg), F32)),
        compiler_params=_params(("parallel",)),
    )(rest, pool_w, pool_scale, dy)


def _rope(r, cos_t, sin_t):
    return r * cos_t + pltpu.roll(r, LANES // 2, 1) * sin_t


def _rope_t(d, cos_t, sin_t):
    return d * cos_t + pltpu.roll(d * sin_t, LANES // 2, 1)


def _qk_fwd(q_raw, k_nope, rest, cos_t, sin_t, q_norm, k_norm, *, name, heads, kr_seg, tm=192):
    T = q_raw.shape[0]
    tm = _tile(T, tm, 16)

    def body(q_ref, kn_ref, kr_ref, c_ref, s_ref, gq_ref, gk_ref, qo_ref, ko_ref):
        cos_b, sin_b = c_ref[...], s_ref[...]
        kr = kr_ref[:, 0:LANES]
        kr_ss = jnp.sum(kr * kr, axis=-1, keepdims=True)
        gq, gk = gq_ref[...], gk_ref[...]
        for h in range(heads):
            lo = h * HEAD_PAD
            q = q_ref[:, lo:lo + HEAD_PAD]
            rq = lax.rsqrt(jnp.sum(q * q, axis=-1, keepdims=True) / QK_HEAD + EPS)
            qn = q * rq * gq
            qo_ref[:, lo:lo + LANES] = qn[:, :LANES].astype(BF16)
            qo_ref[:, lo + LANES:lo + HEAD_PAD] = _rope(qn[:, LANES:], cos_b, sin_b).astype(BF16)
            kn = kn_ref[:, h * LANES:(h + 1) * LANES]
            rk = lax.rsqrt((jnp.sum(kn * kn, axis=-1, keepdims=True) + kr_ss) / QK_HEAD + EPS)
            ko_ref[:, lo:lo + LANES] = (kn * rk * gk[:, :LANES]).astype(BF16)
            ko_ref[:, lo + LANES:lo + HEAD_PAD] = _rope(kr * rk * gk[:, LANES:], cos_b, sin_b).astype(BF16)

    wq, wk = heads * HEAD_PAD, heads * LANES
    return pl.pallas_call(
        body, name=name, grid=(T // tm,),
        in_specs=[pl.BlockSpec((tm, wq), lambda i: (i, 0)), pl.BlockSpec((tm, wk), lambda i: (i, 0)),
                  pl.BlockSpec((tm, HEAD_PAD), lambda i: (i, kr_seg)),
                  pl.BlockSpec((tm, LANES), lambda i: (i, 0)), pl.BlockSpec((tm, LANES), lambda i: (i, 0)),
                  pl.BlockSpec((1, HEAD_PAD), lambda i: (0, 0)), pl.BlockSpec((1, HEAD_PAD), lambda i: (0, 0))],
        out_specs=(pl.BlockSpec((tm, wq), lambda i: (i, 0)), pl.BlockSpec((tm, wq), lambda i: (i, 0))),
        out_shape=(jax.ShapeDtypeStruct((T, wq), BF16), jax.ShapeDtypeStruct((T, wq), BF16)),
        compiler_params=_params(("parallel",)),
    )(q_raw, k_nope, rest, cos_t, sin_t, q_norm, k_norm)


def _qk_bwd(dq, dk, q_raw, k_nope, rest, cos_t, sin_t, q_norm, k_norm, *, name, heads, kr_seg, tm=128):
    T = q_raw.shape[0]
    tm = _tile(T, tm, 16)

    def body(dq_ref, dk_ref, q_ref, kn_ref, kr_ref, c_ref, s_ref, gq_ref, gk_ref,
             dqr_ref, dkn_ref, dkr_ref, dgq_ref, dgk_ref):
        cos_b, sin_b = c_ref[...], s_ref[...]
        kr = kr_ref[:, 0:LANES]
        kr_ss = jnp.sum(kr * kr, axis=-1, keepdims=True)
        gq, gk = gq_ref[...], gk_ref[...]
        dgq = jnp.zeros((1, HEAD_PAD), F32)
        dgk_n = jnp.zeros((1, LANES), F32)
        dgk_r = jnp.zeros((1, LANES), F32)
        dkr = jnp.zeros((tm, LANES), F32)
        for h in range(heads):
            lo = h * HEAD_PAD
            q = q_ref[:, lo:lo + HEAD_PAD]
            rq = lax.rsqrt(jnp.sum(q * q, axis=-1, keepdims=True) / QK_HEAD + EPS)
            qhat = q * rq
            dqn = jnp.concatenate([dq_ref[:, lo:lo + LANES],
                                   _rope_t(dq_ref[:, lo + LANES:lo + HEAD_PAD], cos_b, sin_b)], axis=1)
            dgq = dgq + jnp.sum(dqn * qhat, axis=0, keepdims=True)
            dqh = dqn * gq
            dqr_ref[:, lo:lo + HEAD_PAD] = (
                rq * (dqh - qhat * (jnp.sum(dqh * qhat, axis=-1, keepdims=True) / QK_HEAD))).astype(BF16)
            kn = kn_ref[:, h * LANES:(h + 1) * LANES]
            rk = lax.rsqrt((jnp.sum(kn * kn, axis=-1, keepdims=True) + kr_ss) / QK_HEAD + EPS)
            khat_n, khat_r = kn * rk, kr * rk
            dkn_n = dk_ref[:, lo:lo + LANES]
            dkn_r = _rope_t(dk_ref[:, lo + LANES:lo + HEAD_PAD], cos_b, sin_b)
            dgk_n = dgk_n + jnp.sum(dkn_n * khat_n, axis=0, keepdims=True)
            dgk_r = dgk_r + jnp.sum(dkn_r * khat_r, axis=0, keepdims=True)
            dkh_n, dkh_r = dkn_n * gk[:, :LANES], dkn_r * gk[:, LANES:]
            proj = (jnp.sum(dkh_n * khat_n, axis=-1, keepdims=True)
                    + jnp.sum(dkh_r * khat_r, axis=-1, keepdims=True)) / QK_HEAD
            dkn_ref[:, h * LANES:(h + 1) * LANES] = (rk * (dkh_n - khat_n * proj)).astype(BF16)
            dkr = dkr + rk * (dkh_r - khat_r * proj)
        dkr_ref[:, 0:LANES] = dkr.astype(BF16)
        dkr_ref[:, LANES:HEAD_PAD] = jnp.zeros((tm, HEAD_PAD - LANES), BF16)
        dgk = jnp.concatenate([dgk_n, dgk_r], axis=1)

        @pl.when(pl.program_id(0) == 0)
        def _():
            dgq_ref[...] = dgq
            dgk_ref[...] = dgk

        @pl.when(pl.program_id(0) > 0)
        def _():
            dgq_ref[...] += dgq
            dgk_ref[...] += dgk

    wq, wk = heads * HEAD_PAD, heads * LANES
    row = lambda w: pl.BlockSpec((tm, w), lambda i: (i, 0))
    vec = pl.BlockSpec((1, HEAD_PAD), lambda i: (0, 0))
    return pl.pallas_call(
        body, name=name, grid=(T // tm,),
        in_specs=[row(wq), row(wq), row(wq), row(wk), pl.BlockSpec((tm, HEAD_PAD), lambda i: (i, kr_seg)),
                  row(LANES), row(LANES), vec, vec],
        out_specs=(row(wq), row(wk), row(HEAD_PAD), vec, vec),
        out_shape=(jax.ShapeDtypeStruct((T, wq), BF16), jax.ShapeDtypeStruct((T, wk), BF16),
                   jax.ShapeDtypeStruct((T, HEAD_PAD), BF16),
                   jax.ShapeDtypeStruct((1, HEAD_PAD), F32), jax.ShapeDtypeStruct((1, HEAD_PAD), F32)),
        compiler_params=_params(("arbitrary",)),
    )(dq, dk, q_raw, k_nope, rest, cos_t, sin_t, q_norm, k_norm)


def _causal_mask(s):
    row = lax.broadcasted_iota(jnp.int32, s.shape, 0)
    col = lax.broadcasted_iota(jnp.int32, s.shape, 1)
    return jnp.where(row >= col, s, NEG)


def _flash_fwd(q, k, v, *, name, heads, tq=384, hp=2):
    T = q.shape[0]
    tq = _tile(T, tq, LANES)
    nq = T // tq
    scale = QK_HEAD ** -0.5
    nt = (((1,), (1,)), ((), ()))

    def body(q_ref, k_ref, v_ref, o_ref, lse_ref):
        def q_block(i, carry):
            q_at = pl.ds(pl.multiple_of(i * tq, tq), tq)
            qbs = [q_ref[q_at, h * HEAD_PAD:(h + 1) * HEAD_PAD] for h in range(hp)]

            def step(j, state, masked):
                k_at = pl.ds(pl.multiple_of(j * tq, tq), tq)
                new = []
                for h in range(hp):
                    m, l, acc = state[h]
                    s = lax.dot_general(qbs[h], k_ref[k_at, h * HEAD_PAD:(h + 1) * HEAD_PAD], nt,
                                        preferred_element_type=F32) * scale
                    if masked:
                        s = _causal_mask(s)
                    m_new = jnp.maximum(m, jnp.max(s, axis=-1, keepdims=True))
                    p = jnp.exp(s - m_new)
                    alpha = jnp.exp(m - m_new)
                    l = alpha * l + jnp.sum(p, axis=-1, keepdims=True)
                    acc = alpha * acc + jnp.dot(p.astype(BF16), v_ref[k_at, h * V_HEAD:(h + 1) * V_HEAD],
                                                preferred_element_type=F32)
                    new.append((m_new, l, acc))
                return tuple(new)

            init = tuple((jnp.full((tq, 1), NEG, F32), jnp.zeros((tq, 1), F32), jnp.zeros((tq, V_HEAD), F32))
                         for _ in range(hp))
            state = lax.fori_loop(0, i, lambda j, st: step(j, st, False), init)
            state = step(i, state, True)
            for h in range(hp):
                m, l, acc = state[h]
                o_ref[q_at, h * V_HEAD:(h + 1) * V_HEAD] = (acc / l).astype(BF16)
                lse_ref[h, q_at, :] = jnp.broadcast_to(m + jnp.log(l), (tq, LANES))
            return carry

        lax.fori_loop(0, nq, q_block, 0)

    qk_spec = pl.BlockSpec((T, hp * HEAD_PAD), lambda g: (0, g))
    v_spec = pl.BlockSpec((T, hp * V_HEAD), lambda g: (0, g))
    return pl.pallas_call(
        body, name=name, grid=(heads // hp,), in_specs=[qk_spec, qk_spec, v_spec],
        out_specs=(v_spec, pl.BlockSpec((hp, T, LANES), lambda g: (g, 0, 0))),
        out_shape=(jax.ShapeDtypeStruct((T, heads * V_HEAD), BF16), jax.ShapeDtypeStruct((heads, T, LANES), F32)),
        compiler_params=_params(("parallel",)),
    )(q, k, v)


def _flash_bwd(q, k, v, o, do, lse, *, name, heads, tq=384):
    T = q.shape[0]
    tq = _tile(T, tq, LANES)
    nq = T // tq
    scale = QK_HEAD ** -0.5
    nt = (((1,), (1,)), ((), ()))
    tn = (((0,), (0,)), ((), ()))

    def body(q_ref, k_ref, v_ref, o_ref, do_ref, lse_ref, dq_ref, dk_ref, dv_ref, delta_ref):
        def fill_delta(i, carry):
            at = pl.ds(pl.multiple_of(i * tq, tq), tq)
            d = jnp.sum(o_ref[at, :].astype(F32) * do_ref[at, :].astype(F32), axis=-1, keepdims=True)
            delta_ref[at, :] = jnp.broadcast_to(d, (tq, LANES))
            dq_ref[at, :] = jnp.zeros((tq, HEAD_PAD), F32)
            return carry

        lax.fori_loop(0, nq, fill_delta, 0)

        def kv_block(j, carry):
            k_at = pl.ds(pl.multiple_of(j * tq, tq), tq)
            kb, vb = k_ref[k_at, :], v_ref[k_at, :]

            def step(i, state, masked):
                dk_acc, dv_acc = state
                q_at = pl.ds(pl.multiple_of(i * tq, tq), tq)
                qb, dob = q_ref[q_at, :], do_ref[q_at, :]
                s = lax.dot_general(qb, kb, nt, preferred_element_type=F32) * scale
                if masked:
                    s = _causal_mask(s)
                p = jnp.exp(s - lse_ref[0, q_at, :][:, 0:1])
                dv_acc = dv_acc + lax.dot_general(p.astype(BF16), dob, tn, preferred_element_type=F32)
                dp = lax.dot_general(dob, vb, nt, preferred_element_type=F32)
                ds = (p * (dp - delta_ref[q_at, :][:, 0:1]) * scale).astype(BF16)
                dk_acc = dk_acc + lax.dot_general(ds, qb, tn, preferred_element_type=F32)
                dq_ref[q_at, :] += jnp.dot(ds, kb, preferred_element_type=F32)
                return dk_acc, dv_acc

            state = step(j, (jnp.zeros((tq, HEAD_PAD), F32), jnp.zeros((tq, V_HEAD), F32)), True)
            rest = nq - 1 - j

            def two_steps(t, st):
                i0 = j + 1 + 2 * t
                return step(i0 + 1, step(i0, st, False), False)

            state = lax.fori_loop(0, rest // 2, two_steps, state)
            dk_acc, dv_acc = lax.cond(rest % 2 == 1, lambda st: step(nq - 1, st, False), lambda st: st, state)
            dk_ref[k_at, :] = dk_acc
            dv_ref[k_at, :] = dv_acc.astype(BF16)
            return carry

        lax.fori_loop(0, nq, kv_block, 0)

    qk_spec = pl.BlockSpec((T, HEAD_PAD), lambda h: (0, h))
    v_spec = pl.BlockSpec((T, V_HEAD), lambda h: (0, h))
    return pl.pallas_call(
        body, name=name, grid=(heads,),
        in_specs=[qk_spec, qk_spec, v_spec, v_spec, v_spec, pl.BlockSpec((1, T, LANES), lambda h: (h, 0, 0))],
        out_specs=(qk_spec, qk_spec, v_spec),
        out_shape=(jax.ShapeDtypeStruct((T, heads * HEAD_PAD), F32), jax.ShapeDtypeStruct((T, heads * HEAD_PAD), F32),
                   jax.ShapeDtypeStruct((T, heads * V_HEAD), BF16)),
        scratch_shapes=[pltpu.VMEM((T, LANES), F32)],
        compiler_params=_params(("parallel",)),
    )(q, k, v, o, do, lse)


def _merge_fwd(gl, pa, pb, pc, *, name, d, tm=384, tn=1024):
    T = pa.shape[0]
    tm, tn = _tile(T, tm, 16), _tile(d, tn)
    nb = d // tn

    def body(g0, g1, g2, a, b, c, o_ref):
        o_ref[...] = (jax.nn.sigmoid(g0[...]) * a[...] + jax.nn.sigmoid(g1[...]) * b[...]
                      + jax.nn.sigmoid(g2[...]) * c[...]).astype(BF16)

    gate = lambda n: pl.BlockSpec((tm, tn), lambda i, j: (i, n * nb + j))
    blk = pl.BlockSpec((tm, tn), lambda i, j: (i, j))
    return pl.pallas_call(
        body, name=name, grid=(T // tm, nb), in_specs=[gate(0), gate(1), gate(2), blk, blk, blk],
        out_specs=blk, out_shape=jax.ShapeDtypeStruct((T, d), BF16),
        compiler_params=_params(("parallel", "parallel")),
    )(gl, gl, gl, pa, pb, pc)


def _merge_bwd(dm, gl, pa, pb, pc, *, name, d, tm=384, tn=1024):
    T = pa.shape[0]
    tm, tn = _tile(T, tm, 16), _tile(d, tn)
    nb = d // tn

    def body(dm_ref, g0, g1, g2, a, b, c, da, db, dc, dg0, dg1, dg2):
        dmv = dm_ref[...]
        for g_ref, p_ref, dp_ref, dg_ref in ((g0, a, da, dg0), (g1, b, db, dg1), (g2, c, dc, dg2)):
            sg = jax.nn.sigmoid(g_ref[...])
            dp_ref[...] = (dmv * sg).astype(BF16)
            dg_ref[...] = (dmv * p_ref[...] * sg * (1.0 - sg)).astype(BF16)

    gate = lambda n: pl.BlockSpec((tm, tn), lambda i, j: (i, n * nb + j))
    blk = pl.BlockSpec((tm, tn), lambda i, j: (i, j))
    return pl.pallas_call(
        body, name=name, grid=(T // tm, nb), in_specs=[blk, gate(0), gate(1), gate(2), blk, blk, blk],
        out_specs=(blk,) * 6, out_shape=(jax.ShapeDtypeStruct((T, d), BF16),) * 6,
        compiler_params=_params(("parallel", "parallel")),
    )(dm, gl, gl, gl, pa, pb, pc)


def _loss(y, target, *, name, first, last, tm=384):
    T, d = y.shape
    tm = _tile(T, tm, 16)

    def body(y_ref, t_ref, loss_ref, dy_ref):
        i = pl.program_id(0)
        row = lax.broadcasted_iota(jnp.int32, (tm, 1), 0) + i * tm
        real = jnp.logical_and(row >= first, row < last)
        err = jnp.where(real, y_ref[...] - t_ref[...], 0.0)
        dy_ref[...] = err * (1.0 / d)
        part = jnp.broadcast_to(jnp.sum(err * err, keepdims=True).reshape(1, 1), (1, LANES))

        @pl.when(i == 0)
        def _():
            loss_ref[...] = part

        @pl.when(i > 0)
        def _():
            loss_ref[...] += part

    blk = pl.BlockSpec((tm, d), lambda i: (i, 0))
    return pl.pallas_call(
        body, name=name, grid=(T // tm,), in_specs=[blk, blk],
        out_specs=(pl.BlockSpec((1, LANES), lambda i: (0, 0)), blk),
        out_shape=(jax.ShapeDtypeStruct((1, LANES), F32), jax.ShapeDtypeStruct((T, d), F32)),
        compiler_params=_params(("arbitrary",)),
    )(y, target)


def _as3d(a):
    return a.reshape(a.shape[0], -1, a.shape[-1])


def _sum_stack(parts, *, name, out_dtype, rows=256):
    n, R, C = parts.shape
    tr = _tile(R, rows, 16)

    def body(p_ref, o_ref):
        acc = p_ref[0].astype(F32)
        for s in range(1, n):
            acc = acc + p_ref[s].astype(F32)
        o_ref[...] = acc.astype(out_dtype)

    return pl.pallas_call(
        body, name=name, grid=(R // tr,),
        in_specs=[pl.BlockSpec((n, tr, C), lambda i: (0, i, 0))],
        out_specs=pl.BlockSpec((tr, C), lambda i: (i, 0)),
        out_shape=jax.ShapeDtypeStruct((R, C), out_dtype),
        compiler_params=_params(("parallel",)),
    )(parts)


def _adamw(w, g, m, v, *, name, rows=128):
    R, C = w.shape
    tr = _tile(R, rows, 8)
    c1 = 1.0 - ADAM_B1 ** ADAM_STEP
    c2 = 1.0 - ADAM_B2 ** ADAM_STEP

    def body(w_ref, g_ref, m_ref, v_ref, d_ref, nm_ref, nv_ref):
        gv = g_ref[...]
        nm = ADAM_B1 * m_ref[...] + (1.0 - ADAM_B1) * gv
        nv = ADAM_B2 * v_ref[...] + (1.0 - ADAM_B2) * (gv * gv)
        nm_ref[...] = nm
        nv_ref[...] = nv
        d_ref[...] = -ADAM_LR * ((nm / c1) / (jnp.sqrt(nv / c2) + ADAM_EPS) + ADAM_WD * w_ref[...])

    blk = pl.BlockSpec((tr, C), lambda i: (i, 0))
    return pl.pallas_call(
        body, name=name, grid=(R // tr,), in_specs=[blk] * 4, out_specs=(blk,) * 3,
        out_shape=(jax.ShapeDtypeStruct((R, C), F32),) * 3,
        compiler_params=_params(("parallel",)),
    )(w, g, m, v)


def _adamw_layers(w, g_mine, g_theirs, m, v, *, name, rows=128):
    _, R, C = w.shape
    tr = _tile(R, rows, 8)
    c1 = 1.0 - ADAM_B1 ** ADAM_STEP
    c2 = 1.0 - ADAM_B2 ** ADAM_STEP

    def body(w_ref, gm_ref, gt_ref, m_ref, v_ref, g_ref, d_ref, nm_ref, nv_ref):
        gv = jnp.where(pl.program_id(0) == lax.axis_index("c"), gm_ref[...], gt_ref[...])
        nm = ADAM_B1 * m_ref[0] + (1.0 - ADAM_B1) * gv
        nv = ADAM_B2 * v_ref[0] + (1.0 - ADAM_B2) * (gv * gv)
        g_ref[0] = gv
        nm_ref[0] = nm
        nv_ref[0] = nv
        d_ref[0] = -ADAM_LR * ((nm / c1) / (jnp.sqrt(nv / c2) + ADAM_EPS) + ADAM_WD * w_ref[0])

    lay = pl.BlockSpec((1, tr, C), lambda l, i: (l, i, 0))
    one = pl.BlockSpec((tr, C), lambda l, i: (i, 0))
    return pl.pallas_call(
        body, name=name, grid=(2, R // tr), in_specs=[lay, one, one, lay, lay], out_specs=(lay,) * 4,
        out_shape=(jax.ShapeDtypeStruct((2, R, C), F32),) * 4,
        compiler_params=_params(("parallel", "parallel")),
    )(w, g_mine, g_theirs, m, v)


ANY = pl.BlockSpec(memory_space=pl.ANY)


def _coords():
    return lax.axis_index("x"), lax.axis_index("y"), lax.axis_index("c")


HBM = pl.BlockSpec(memory_space=pltpu.HBM)
SEM = pl.BlockSpec(memory_space=pltpu.SEMAPHORE)
EFFECT = pltpu.SideEffectType.DATAFLOW_SIDE_EFFECTING


def _copies(plan, bufs, send_sems, recv_sems):
    return [pltpu.make_async_remote_copy(src_ref=s, dst_ref=d, send_sem=send_sems.at[i], recv_sem=recv_sems.at[i],
                                         device_id=to, device_id_type=MESH)
            for i, (s, d, to) in enumerate(plan(bufs))]


def _start_copies(bufs, groups, *, name):
    nb, ng = len(bufs), len(groups)

    def body(*refs):
        buf_refs = refs[:nb]
        sems = refs[nb:nb + 2 * ng]
        token = refs[-1]
        for g, (plan, _) in enumerate(groups):
            for cp in _copies(plan, buf_refs, sems[2 * g], sems[2 * g + 1]):
                cp.start()
        token[...] = jnp.zeros_like(token)

    sem_shapes = []
    for _, n in groups:
        sem_shapes += [pltpu.SemaphoreType.DMA((n,)), pltpu.SemaphoreType.DMA((n,))]
    out = pl.pallas_call(
        body, name=name, in_specs=[HBM] * nb,
        out_specs=tuple([SEM] * (2 * ng) + [HBM] * nb + [pl.BlockSpec(memory_space=pltpu.VMEM)]),
        out_shape=tuple(sem_shapes + [pltpu.HBM(b.shape, b.dtype) for b in bufs] + [jax.ShapeDtypeStruct((8, LANES), F32)]),
        input_output_aliases={i: 2 * ng + i for i in range(nb)},
        compiler_params=pltpu.CompilerParams(has_side_effects=EFFECT),
    )(*[pltpu.with_memory_space_constraint(b, pltpu.HBM) for b in bufs])
    sems = [(out[2 * g], out[2 * g + 1]) for g in range(ng)]
    return sems, list(out[2 * ng:2 * ng + nb]), out[-1]


def _wait_copies(bufs, sems, plan, after, *, name):
    nb = len(bufs)

    def body(*refs):
        buf_refs = refs[:nb]
        for cp in _copies(plan, buf_refs, refs[nb], refs[nb + 1]):
            cp.wait_send()
            cp.wait_recv()

    out = pl.pallas_call(
        body, name=name, in_specs=[HBM] * nb + [SEM, SEM, ANY], out_specs=tuple([HBM] * nb),
        out_shape=tuple(pltpu.HBM(b.shape, b.dtype) for b in bufs),
        input_output_aliases={i: i for i in range(nb)},
        compiler_params=pltpu.CompilerParams(has_side_effects=EFFECT),
    )(*bufs, sems[0], sems[1], after)
    return list(out)


def _half(ref, c):
    h = ref.shape[0] // 2
    return ref.at[pl.ds(c * h, h)]


def _ici_gather_plan(pairs):
    def plan(refs):
        x, y, c = _coords()
        me = 2 * x + y
        out = []
        for s, d in pairs:
            for cx, cy in [(1 - x, y), (x, 1 - y), (1 - x, 1 - y)]:
                out.append((_half(refs[s], c), _half(refs[d].at[me], c), (cx, cy, c)))
        return out
    return plan, 3 * len(pairs)


def _d2d_forward_plan(lands):
    def plan(refs):
        x, y, c = _coords()
        out = []
        for d in lands:
            for cx, cy in [(1 - x, y), (x, 1 - y), (1 - x, 1 - y)]:
                got = _half(refs[d].at[2 * cx + cy], c)
                out.append((got, got, (x, y, 1 - c)))
        return out
    return plan, 3 * len(lands)


def _swap_layers(grads, *, name):
    def body(src, out, send_sem, recv_sem):
        x, y, c = _coords()
        cp = pltpu.make_async_remote_copy(src_ref=src.at[1 - c], dst_ref=out, send_sem=send_sem, recv_sem=recv_sem,
                                          device_id=(x, y, 1 - c), device_id_type=MESH)
        cp.start()
        cp.wait()

    return pl.pallas_call(
        body, name=name, in_specs=[ANY], out_specs=ANY,
        out_shape=jax.ShapeDtypeStruct(grads.shape[1:], grads.dtype),
        scratch_shapes=[pltpu.SemaphoreType.DMA, pltpu.SemaphoreType.DMA],
    )(grads)


def _scatter_pieces(pieces, *, name):
    def body(src, out, send_sems, recv_sems, local_sem):
        x, y, c = _coords()
        me = 2 * x + y
        chips = [(1 - x, y), (x, 1 - y), (1 - x, 1 - y)]
        mine = pltpu.make_async_copy(src.at[me], out.at[me], local_sem)
        mine.start()
        cps = [pltpu.make_async_remote_copy(src_ref=src.at[2 * cx + cy], dst_ref=out.at[me], send_sem=send_sems.at[j],
                                            recv_sem=recv_sems.at[j], device_id=(cx, cy, c), device_id_type=MESH)
               for j, (cx, cy) in enumerate(chips)]
        for cp in cps:
            cp.start()
        for j, (cx, cy) in enumerate(chips):
            slot = out.at[2 * cx + cy]
            pltpu.make_async_remote_copy(src_ref=slot, dst_ref=slot, send_sem=send_sems.at[j], recv_sem=recv_sems.at[j],
                                         device_id=(cx, cy, c), device_id_type=MESH).wait_recv()
        for cp in cps:
            cp.wait_send()
        mine.wait()

    return pl.pallas_call(
        body, name=name, in_specs=[ANY], out_specs=ANY,
        out_shape=jax.ShapeDtypeStruct(pieces.shape, pieces.dtype),
        scratch_shapes=[pltpu.SemaphoreType.DMA((3,)), pltpu.SemaphoreType.DMA((3,)), pltpu.SemaphoreType.DMA],
    )(pieces)


def _swap_totals(total, *, name):
    def body(src, out, send_sem, recv_sem):
        x, y, c = _coords()
        cp = pltpu.make_async_remote_copy(src_ref=src, dst_ref=out, send_sem=send_sem, recv_sem=recv_sem,
                                          device_id=(x, y, 1 - c), device_id_type=MESH)
        cp.start()
        cp.wait()

    return pl.pallas_call(
        body, name=name, in_specs=[ANY], out_specs=ANY,
        out_shape=jax.ShapeDtypeStruct(total.shape, total.dtype),
        scratch_shapes=[pltpu.SemaphoreType.DMA, pltpu.SemaphoreType.DMA],
    )(total)


def _gather_all(block, *, name):
    def body(src, out, send_sems, recv_sems, local_sem):
        x, y, c = _coords()
        me = 4 * x + 2 * y + c
        flips = [(fx, fy, fc) for fx in (0, 1) for fy in (0, 1) for fc in (0, 1)][1:]
        mine = pltpu.make_async_copy(src, out.at[me], local_sem)
        mine.start()
        peers = [(x ^ fx, y ^ fy, c ^ fc) for fx, fy, fc in flips]
        cps = [pltpu.make_async_remote_copy(src_ref=src, dst_ref=out.at[me], send_sem=send_sems.at[k],
                                            recv_sem=recv_sems.at[k], device_id=peer, device_id_type=MESH)
               for k, peer in enumerate(peers)]
        for cp in cps:
            cp.start()
        for k, (px, py, pc) in enumerate(peers):
            slot = out.at[4 * px + 2 * py + pc]
            pltpu.make_async_remote_copy(src_ref=slot, dst_ref=slot, send_sem=send_sems.at[k], recv_sem=recv_sems.at[k],
                                         device_id=(px, py, pc), device_id_type=MESH).wait_recv()
        for cp in cps:
            cp.wait_send()
        mine.wait()

    return pl.pallas_call(
        body, name=name, in_specs=[ANY], out_specs=ANY,
        out_shape=jax.ShapeDtypeStruct((8,) + block.shape, block.dtype),
        scratch_shapes=[pltpu.SemaphoreType.DMA((7,)), pltpu.SemaphoreType.DMA((7,)), pltpu.SemaphoreType.DMA],
    )(block)


def _cols(o):
    return jnp.transpose(o, (1, 0, 2)).reshape(o.shape[1], -1)


def _uncols(full):
    return jnp.transpose(full.reshape(full.shape[0], 4, -1), (1, 0, 2))


def _rope_pad(x1, x2):
    z = jnp.zeros_like(x1)
    return jnp.concatenate([x1, z, x2, z], axis=-1)


def _head_pad(w, heads):
    r = w.reshape(w.shape[0], heads, QK_HEAD)
    half = QK_ROPE // 2
    out = jnp.concatenate([r[..., :QK_NOPE], _rope_pad(r[..., QK_NOPE:QK_NOPE + half], r[..., QK_NOPE + half:])], axis=-1)
    return out.reshape(w.shape[0], heads * HEAD_PAD)


def _head_unpad(w, heads):
    r = w.reshape(w.shape[0], heads, HEAD_PAD)
    half = QK_ROPE // 2
    out = jnp.concatenate([r[..., :QK_NOPE], r[..., QK_NOPE:QK_NOPE + half],
                           r[..., QK_NOPE + 2 * half:QK_NOPE + 3 * half]], axis=-1)
    return out.reshape(w.shape[0], heads * QK_HEAD)


class _Dims:
    def __init__(self, d, seq):
        self.d = d
        self.seq = seq
        self.t_real = N_META + seq
        self.t = -(-self.t_real // LANES) * LANES
        self.dc = d // 2
        self.dp = d // 2
        self.pg = self.dp // len(POOL_WINDOWS)
        self.heads = d // 128
        self.dff = 4 * d
        self.a_end = 3 * self.dc
        self.q_end = self.a_end + Q_LORA
        self.kv_end = self.q_end + KV_LORA
        self.kr_end = self.kv_end + QK_ROPE
        self.pool_end = self.kr_end + self.dp
        self.d_in = self.pool_end + 3 * d
        self.r_pool = 3 * self.dc
        self.r_q = self.r_pool + self.dp
        self.r_kv = self.r_q + Q_LORA
        self.r_kr = self.r_kv + KV_LORA
        self.r_width = self.r_kr + HEAD_PAD


def _in_weights(dm, w_in_pieces):
    w_in = _cols(w_in_pieces)
    half = QK_ROPE // 2
    kr = w_in[:, dm.kv_end:dm.kr_end]
    kr_p = jnp.concatenate([_rope_pad(kr[:, :half], kr[:, half:]), jnp.zeros((dm.d, HEAD_PAD - LANES), BF16)], axis=1)
    return dict(
        wg=w_in[:, dm.pool_end:],
        wr=jnp.concatenate([w_in[:, :dm.a_end], w_in[:, dm.kr_end:dm.pool_end], w_in[:, dm.a_end:dm.kv_end], kr_p], axis=1))


def _other_weights(dm, g):
    w_ukv = _cols(g["w_ukv"]).reshape(KV_LORA, dm.heads, QK_NOPE + V_HEAD)
    return dict(
        wuq=_head_pad(_cols(g["w_uq"]), dm.heads),
        wkn=w_ukv[:, :, :QK_NOPE].reshape(KV_LORA, dm.heads * QK_NOPE),
        wv=w_ukv[:, :, QK_NOPE:].reshape(KV_LORA, dm.heads * V_HEAD),
        wp=jnp.transpose(g["pool_w"], (1, 0, 2, 3)).reshape(len(POOL_WINDOWS), dm.pg, dm.pg),
        wba=_cols(g["w_branch_a"]), wbb=g["w_branch_b"].reshape(-1, dm.d), wbc=_cols(g["w_branch_c"]),
        wo=g["w_o"].reshape(-1, dm.d), wup=_cols(g["w_up"]), wdn=g["w_down"].reshape(-1, dm.d))


def _small_weights(small):
    return dict(
        conv_w=small["conv_w"],
        attn_norm=small["attn_norm"][None], mlp_norm=small["mlp_norm"][None],
        q_lat_norm=small["q_lat_norm"][None], kv_lat_norm=small["kv_lat_norm"][None],
        q_norm=_head_pad(small["q_norm"][None], 1), k_norm=_head_pad(small["k_norm"][None], 1),
        pool_scale=small["pool_scale"][None],
    )


def _layer_grad_pieces(dm, dw):
    half = QK_ROPE // 2
    dwr, dwg = dw["wr"], dw["wg"]
    d_in = jnp.concatenate([
        dwr[:, :dm.r_pool], dwr[:, dm.r_q:dm.r_kr], dwr[:, dm.r_kr:dm.r_kr + half],
        dwr[:, dm.r_kr + 2 * half:dm.r_kr + 3 * half], dwr[:, dm.r_pool:dm.r_q], dwg], axis=1)
    d_ukv = jnp.concatenate([dw["wkn"].reshape(KV_LORA, dm.heads, QK_NOPE),
                             dw["wv"].reshape(KV_LORA, dm.heads, V_HEAD)], axis=-1).reshape(KV_LORA, -1)
    rows = lambda a: a.reshape((4, a.shape[0] // 4) + a.shape[1:])
    out = dict(
        w_in=_uncols(d_in), w_uq=_uncols(_head_unpad(dw["wuq"], dm.heads)), w_ukv=_uncols(d_ukv),
        pool_w=jnp.transpose(dw["wp"].reshape(len(POOL_WINDOWS), 4, dm.pg // 4, dm.pg), (1, 0, 2, 3)),
        w_branch_a=_uncols(dw["wba"]), w_branch_b=rows(dw["wbb"]), w_branch_c=_uncols(dw["wbc"]),
        w_o=rows(dw["wo"]), w_up=_uncols(dw["wup"]), w_down=rows(dw["wdn"]),
    )
    return {k: v.astype(BF16) for k, v in out.items()}


def _layer_fwd(dm, W, x, cos_t, sin_t, tag, more=None):
    n = lambda s: f"{s}_{tag}"
    h = _rms_fwd(x, W["attn_norm"], name=n("attn_norm"))
    gl = _mm(h, W["wg"], name=n("proj_gates"))
    rest = _mm(h, W["wr"], name=n("proj_rest"))
    if more is not None:
        W.update(more(rest))
    y_a = _conv_fwd(rest, W["conv_w"], name=n("conv"), dc=dm.dc)
    y_c = _pool_fwd(rest, W["wp"], W["pool_scale"], name=n("pool"), seg0=dm.r_pool // dm.pg, pg=dm.pg)
    q_lat = _rms_fwd(rest, W["q_lat_norm"], name=n("q_lat_norm"), width=Q_LORA, seg=dm.r_q // Q_LORA)
    kv_lat = _rms_fwd(rest, W["kv_lat_norm"], name=n("kv_lat_norm"), width=KV_LORA, seg=dm.r_kv // KV_LORA)
    q_raw = _mm(q_lat, W["wuq"], name=n("up_q"))
    k_nope = _mm(kv_lat, W["wkn"], name=n("up_k"))
    v = _mm(kv_lat, W["wv"], name=n("up_v"), out_dtype=BF16)
    q, k = _qk_fwd(q_raw, k_nope, rest, cos_t, sin_t, W["q_norm"], W["k_norm"], name=n("qk_norm_rope"),
                   heads=dm.heads, kr_seg=dm.r_kr // HEAD_PAD)
    y_b, lse = _flash_fwd(q, k, v, name=n("attention"), heads=dm.heads)
    pa = _mm(y_a, W["wba"], name=n("branch_a"))
    pb = _mm(y_b, W["wbb"], name=n("branch_b"))
    pc = _mm(y_c, W["wbc"], name=n("branch_c"))
    merged = _merge_fwd(gl, pa, pb, pc, name=n("merge"), d=dm.d)
    x1 = _mm(merged, W["wo"], name=n("out_proj"), add=x)
    h2 = _rms_fwd(x1, W["mlp_norm"], name=n("mlp_norm"))
    up, act = _mm(h2, W["wup"], name=n("mlp_up"), epi="relu2")
    x2 = _mm(act, W["wdn"], name=n("mlp_down"), add=x1, tk=2048)
    saved = dict(x=x, h=h, gl=gl, rest=rest, y_a=y_a, y_c=y_c, q_lat=q_lat, kv_lat=kv_lat, q_raw=q_raw, k_nope=k_nope,
                 v=v, q=q, k=k, y_b=y_b, lse=lse, pa=pa, pb=pb, pc=pc, merged=merged, x1=x1, h2=h2, up=up, act=act)
    return x2, saved


def _layer_bwd(dm, W, S, dx2, cos_t, sin_t, tag):
    n = lambda s: f"{s}_{tag}"
    dw, ds = {}, {}
    dup = _mm(dx2, W["wdn"], name=n("d_mlp_down"), tb=True, aux=S["up"], epi="drelu2", out_dtype=BF16)
    dw["wdn"] = _mm(S["act"], dx2, name=n("dw_mlp_down"), ta=True, tm=1024, tk=1408)
    dh2 = _mm(dup, W["wup"], name=n("d_mlp_up"), tb=True, tk=2048)
    dw["wup"] = _mm(S["h2"], dup, name=n("dw_mlp_up"), ta=True, tm=1024, tk=1408)
    dx1, ds["mlp_norm"] = _rms_bwd(dh2, S["x1"], W["mlp_norm"], name=n("d_mlp_norm"), res=dx2)
    dmerged = _mm(dx1, W["wo"], name=n("d_out_proj"), tb=True)
    dw["wo"] = _mm(S["merged"], dx1, name=n("dw_out_proj"), ta=True, tm=1024, tk=1408)
    dpa, dpb, dpc, dg0, dg1, dg2 = _merge_bwd(dmerged, S["gl"], S["pa"], S["pb"], S["pc"], name=n("d_merge"), d=dm.d)
    dgl = jnp.concatenate([dg0, dg1, dg2], axis=1)
    dy_a = _mm(dpa, W["wba"], name=n("d_branch_a"), tb=True)
    dw["wba"] = _mm(S["y_a"], dpa, name=n("dw_branch_a"), ta=True, tm=1024, tk=1408)
    dy_b = _mm(dpb, W["wbb"], name=n("d_branch_b"), tb=True, out_dtype=BF16)
    dw["wbb"] = _mm(S["y_b"], dpb, name=n("dw_branch_b"), ta=True, tm=1024, tk=1408)
    dy_c = _mm(dpc, W["wbc"], name=n("d_branch_c"), tb=True)
    dw["wbc"] = _mm(S["y_c"], dpc, name=n("dw_branch_c"), ta=True, tm=1024, tk=1408)
    dq, dk, dv = _flash_bwd(S["q"], S["k"], S["v"], S["y_b"], dy_b, S["lse"], name=n("d_attention"), heads=dm.heads)
    dq_raw, dk_nope, dk_rope, dgq, dgk = _qk_bwd(
        dq, dk, S["q_raw"], S["k_nope"], S["rest"], cos_t, sin_t, W["q_norm"], W["k_norm"], name=n("d_qk_norm_rope"),
        heads=dm.heads, kr_seg=dm.r_kr // HEAD_PAD)
    ds["q_norm"] = _head_unpad(dgq, 1)
    ds["k_norm"] = _head_unpad(dgk, 1)
    dq_lat_n = _mm(dq_raw, W["wuq"], name=n("d_up_q"), tb=True, tk=2048)
    dw["wuq"] = _mm(S["q_lat"], dq_raw, name=n("dw_up_q"), ta=True, tm=512, tk=1408)
    dkv_v = _mm(dv, W["wv"], name=n("d_up_v"), tb=True)
    dkv_lat_n = _mm(dk_nope, W["wkn"], name=n("d_up_k"), tb=True, add=dkv_v)
    dw["wkn"] = _mm(S["kv_lat"], dk_nope, name=n("dw_up_k"), ta=True, tm=512, tk=1408)
    dw["wv"] = _mm(S["kv_lat"], dv, name=n("dw_up_v"), ta=True, tm=512, tk=1408)
    dq_lat, ds["q_lat_norm"] = _rms_bwd(dq_lat_n, S["rest"], W["q_lat_norm"], name=n("d_q_lat_norm"), width=Q_LORA,
                                        seg=dm.r_q // Q_LORA, out_dtype=BF16)
    dkv_lat, ds["kv_lat_norm"] = _rms_bwd(dkv_lat_n, S["rest"], W["kv_lat_norm"], name=n("d_kv_lat_norm"), width=KV_LORA,
                                          seg=dm.r_kv // KV_LORA, out_dtype=BF16)
    du, db, dc, ds["conv_w"] = _conv_bwd(S["rest"], W["conv_w"], dy_a, name=n("d_conv"), dc=dm.dc)
    dpool, dw["wp"], ds["pool_scale"] = _pool_bwd(S["rest"], W["wp"], W["pool_scale"], dy_c, name=n("d_pool"),
                                                  seg0=dm.r_pool // dm.pg, pg=dm.pg)
    drest = jnp.concatenate([du, db, dc, dpool, dq_lat, dkv_lat, dk_rope], axis=1)
    dh_g = _mm(dgl, W["wg"], name=n("d_proj_gates"), tb=True, tk=2048)
    dh = _mm(drest, W["wr"], name=n("d_proj_rest"), tb=True, add=dh_g, tk=1792)
    dw["wg"] = _mm(S["h"], dgl, name=n("dw_proj_gates"), ta=True, tm=1024, tk=1408)
    dw["wr"] = _mm(S["h"], drest, name=n("dw_proj_rest"), ta=True, tm=1024, tk=1408)
    dx, ds["attn_norm"] = _rms_bwd(dh, S["x"], W["attn_norm"], name=n("d_attn_norm"), res=dx1)
    return dx, dw, ds


BIG = ("w_in", "w_uq", "w_ukv", "pool_w", "w_branch_a", "w_branch_b", "w_branch_c", "w_o", "w_up", "w_down")
REPLICATED = ("attn_norm", "q_lat_norm", "kv_lat_norm", "q_norm", "k_norm", "pool_scale", "mlp_norm")
WEIGHTS = ("meta_tokens", "attn_norm", "w_in", "conv_w", "q_lat_norm", "kv_lat_norm", "w_uq", "w_ukv", "q_norm",
           "k_norm", "pool_w", "pool_scale", "w_branch_a", "w_branch_b", "w_branch_c", "w_o", "mlp_norm", "w_up",
           "w_down")


def _pack(arrays):
    flat = jnp.concatenate([a.reshape(-1).astype(F32) for a in arrays])
    pad = (-flat.shape[0]) % (8 * LANES)
    return jnp.pad(flat, (0, pad)).reshape(-1, LANES)


def _unpack(flat, shapes):
    out, pos = [], 0
    flat = flat.reshape(-1)
    for shp in shapes:
        size = math.prod(shp)
        out.append(flat[pos:pos + size].reshape(shp))
        pos += size
    return out


def _update(w, g, m, v, name):
    shp = w.shape
    to2 = lambda a: a.reshape(-1, shp[-1])
    delta, nm, nv = _adamw(to2(w), to2(g), to2(m), to2(v), name=name)
    return delta.reshape(shp), nm.reshape(shp), nv.reshape(shp)


def _step(args):
    x = args["x"][0]
    seq, d = x.shape
    dm = _Dims(d, seq)
    xi, yi, ci = _coords()
    chip = 2 * xi + yi

    order = [(k, l) for l in range(2) for k in BIG]
    shards = {n: args[n[0]][n[1]].astype(BF16) for n in order}
    lands = {n: lax.empty((4,) + shards[n].shape, BF16) for n in order}
    group_names = [[("w_in", 0)], [(k, 0) for k in BIG[1:]], [(k, 1) for k in BIG]]
    at = {n: i for i, n in enumerate(order)}
    sems, thru, token = _start_copies(
        [shards[n] for n in order] + [lands[n] for n in order],
        [_ici_gather_plan([(at[n], len(order) + at[n]) for n in g]) for g in group_names], name="start_gather_ici")
    for i, n in enumerate(order):
        shards[n], lands[n] = thru[i], thru[len(order) + i]

    def finish_gather(g, after, tag):
        names = group_names[g]
        k = len(names)
        plan, _ = _ici_gather_plan([(i, k + i) for i in range(k)])
        got = _wait_copies([shards[n] for n in names] + [lands[n] for n in names], sems[g], plan, after,
                           name=f"wait_gather_ici_{tag}")
        for i, n in enumerate(names):
            shards[n] = got[i]
        fwd = _d2d_forward_plan(list(range(k)))
        sems2, bufs2, tok2 = _start_copies(got[k:], [fwd], name=f"start_gather_d2d_{tag}")
        return names, bufs2, sems2[0], fwd[0], tok2

    def land_gather(pending, after, tag):
        names, bufs2, sems2, plan, tok2 = pending
        done = _wait_copies(bufs2, sems2, plan, tok2 if after is None else after, name=f"wait_gather_d2d_{tag}")
        own = jnp.arange(4) == chip
        return {n[0]: jnp.where(own.reshape((4,) + (1,) * (buf.ndim - 1)), shards[n][None], buf)
                for n, buf in zip(names, done)}

    small_w = _gather_all(_pack([args["conv_w"], args["meta_tokens"]]) + token[0:1, :], name="gather_small_weights")
    conv_shape, meta_shape = args["conv_w"].shape, args["meta_tokens"].shape
    per_chip = [_unpack(small_w[2 * j], [conv_shape, meta_shape]) for j in range(4)]
    conv_full = jnp.concatenate([p[0] for p in per_chip], axis=-1)
    meta_full = jnp.concatenate([p[1] for p in per_chip], axis=-1)

    layers = []
    for l in range(2):
        small = {k: args[k][l] for k in REPLICATED}
        small["conv_w"] = conv_full[l]
        layers.append(_small_weights(small))
    layers[0].update(_in_weights(dm, land_gather(finish_gather(0, token, "l0_in"), None, "l0_in")["w_in"]))

    pos = jnp.arange(dm.t, dtype=F32)
    inv = ROPE_THETA ** (-jnp.arange(0, QK_ROPE, 2, dtype=F32) / QK_ROPE)
    ang = pos[:, None] * inv[None, :]
    cos_t = _rope_pad(jnp.cos(ang), jnp.cos(ang))
    sin_t = _rope_pad(-jnp.sin(ang), jnp.sin(ang))
    tail = jnp.zeros((dm.t - dm.t_real, d), F32)
    h0 = jnp.concatenate([meta_full, x, tail], axis=0)
    target = jnp.concatenate([jnp.zeros((N_META, d), F32), args["loss_target"][0], tail], axis=0)

    def rest_of_layer0(after):
        return _other_weights(dm, land_gather(finish_gather(1, after, "l0_rest"), None, "l0_rest"))

    h1, saved0 = _layer_fwd(dm, layers[0], h0, cos_t, sin_t, "l0", more=rest_of_layer0)
    g1 = land_gather(finish_gather(2, saved0["y_b"], "l1"), h1, "l1")
    layers[1].update(_in_weights(dm, g1["w_in"]))
    layers[1].update(_other_weights(dm, g1))
    h2, saved1 = _layer_fwd(dm, layers[1], h1, cos_t, sin_t, "l1")
    sq, dy = _loss(h2, target, name="loss_head", first=N_META, last=dm.t_real)
    loss = lax.psum(0.5 / d * sq[0, 0], ("x", "y", "c"))
    dh1, dw1, ds1 = _layer_bwd(dm, layers[1], saved1, dy, cos_t, sin_t, "l1")
    dh0, dw0, ds0 = _layer_bwd(dm, layers[0], saved0, dh1, cos_t, sin_t, "l0")
    grad_x = dh0[N_META:dm.t_real][None]

    pieces = [_layer_grad_pieces(dm, dw0), _layer_grad_pieces(dm, dw1)]
    grads = {}
    for k in BIG:
        both = jnp.stack([pieces[0][k], pieces[1][k]])
        theirs = _swap_layers(both, name=f"swap_{k}")
        mine = lax.dynamic_index_in_dim(both, ci, 0, keepdims=False)
        pair = _sum_stack(_as3d(jnp.stack([mine, theirs]).reshape((2, -1) + mine.shape[-1:])),
                          name=f"pair_sum_{k}", out_dtype=BF16).reshape(mine.shape)
        landed = _scatter_pieces(pair, name=f"scatter_{k}")
        total = _sum_stack(_as3d(landed), name=f"chip_sum_{k}", out_dtype=F32)
        grads[k] = (total, _swap_totals(total, name=f"swap_total_{k}"))

    small_names = REPLICATED + ("conv_w",)
    small_parts = [jnp.stack([ds0[k].reshape(ds0[k].shape[-2:] if k == "conv_w" else (-1,)),
                              ds1[k].reshape(ds1[k].shape[-2:] if k == "conv_w" else (-1,))]) for k in small_names]
    small_parts.append(dh0[:N_META])
    small_all = _gather_all(_pack(small_parts), name="gather_small_grads")
    small_sum = _sum_stack(small_all, name="sum_small_grads", out_dtype=F32)
    small_g = dict(zip(small_names + ("meta_tokens",), _unpack(small_sum, [p.shape for p in small_parts])))
    for k in REPLICATED:
        grads[k] = small_g[k]
    dcw = conv_shape[-1]
    grads["conv_w"] = lax.dynamic_slice_in_dim(small_g["conv_w"], chip * dcw, dcw, axis=2)
    dmeta = meta_shape[-1]
    grads["meta_tokens"] = lax.dynamic_slice_in_dim(small_g["meta_tokens"], chip * dmeta, dmeta, axis=1)

    delta, new_m, new_v = {}, {}, {}
    for k in WEIGHTS:
        shp = args[k].shape
        if k in BIG:
            lay = lambda a: a.reshape(2, -1, shp[-1])
            mine, theirs = grads[k]
            out = _adamw_layers(lay(args[k]), mine, theirs, lay(args["m_" + k]), lay(args["v_" + k]), name=f"adamw_{k}")
            grads[k], delta[k], new_m[k], new_v[k] = (o.reshape(shp) for o in out)
        else:
            grads[k] = grads[k].reshape(shp)
            delta[k], new_m[k], new_v[k] = _update(args[k], grads[k], args["m_" + k], args["v_" + k], f"adamw_{k}")
    return (loss, grad_x, *[grads[k] for k in WEIGHTS], *[delta[k] for k in WEIGHTS],
            *[new_m[k] for k in WEIGHTS], *[new_v[k] for k in WEIGHTS])


def kernel(x, meta_tokens, attn_norm, w_in, conv_w, q_lat_norm, kv_lat_norm, w_uq, w_ukv, q_norm, k_norm, pool_w, pool_scale, w_branch_a, w_branch_b, w_branch_c, w_o, mlp_norm, w_up, w_down, loss_target, m_meta_tokens, m_attn_norm, m_w_in, m_conv_w, m_q_lat_norm, m_kv_lat_norm, m_w_uq, m_w_ukv, m_q_norm, m_k_norm, m_pool_w, m_pool_scale, m_w_branch_a, m_w_branch_b, m_w_branch_c, m_w_o, m_mlp_norm, m_w_up, m_w_down, v_meta_tokens, v_attn_norm, v_w_in, v_conv_w, v_q_lat_norm, v_kv_lat_norm, v_w_uq, v_w_ukv, v_q_norm, v_k_norm, v_pool_w, v_pool_scale, v_w_branch_a, v_w_branch_b, v_w_branch_c, v_w_o, v_mlp_norm, v_w_up, v_w_down):
    return _step(dict(locals()))
```

```python
import functools
import math

import jax
import jax.numpy as jnp
from jax import lax
from jax.experimental import pallas as pl
from jax.experimental.pallas import tpu as pltpu

F32 = jnp.float32
BF16 = jnp.bfloat16
MESH = pl.DeviceIdType.MESH

EPS = 1e-6
N_META = 16
QK_NOPE = 128
QK_ROPE = 64
QK_HEAD = QK_NOPE + QK_ROPE
V_HEAD = 128
HEAD_PAD = 256
Q_LORA = 512
KV_LORA = 512
ROPE_THETA = 10000.0
POOL_WINDOWS = (2, 4, 8, 16)
HALO = 16
LANES = 128
ADAM_LR = 0.001
ADAM_B1 = 0.9
ADAM_B2 = 0.999
ADAM_EPS = 1e-08
ADAM_WD = 0.01
ADAM_STEP = 10
VMEM_LIMIT = 52 * 1024 * 1024
NEG = -1e30


def _tile(n, target, mult=LANES):
    best = None
    for t in range(mult, min(n, target) + 1, mult):
        if n % t == 0:
            best = t
    return n if best is None else best


def _params(sem=None):
    return pltpu.CompilerParams(dimension_semantics=sem, vmem_limit_bytes=VMEM_LIMIT)


def _mm(a, b, *, name, ta=False, tb=False, add=None, aux=None, epi=None, out_dtype=F32,
        tm=704, tn=1024, tk=None):
    if ta:
        K, M = a.shape
    else:
        M, K = a.shape
    if tb:
        N, kb = b.shape
    else:
        kb, N = b.shape
    assert K == kb, (a.shape, b.shape, ta, tb)
    tm = _tile(M, tm, LANES if ta else 16)
    tn = _tile(N, tn, LANES)
    tk = K if tk is None else _tile(K, tk, LANES if (not ta or tb) else 16)
    nk = K // tk
    grid = (M // tm, N // tn, nk)

    a_spec = pl.BlockSpec((tk, tm), lambda i, j, k: (k, i)) if ta else pl.BlockSpec((tm, tk), lambda i, j, k: (i, k))
    b_spec = pl.BlockSpec((tn, tk), lambda i, j, k: (j, k)) if tb else pl.BlockSpec((tk, tn), lambda i, j, k: (k, j))
    o_spec = pl.BlockSpec((tm, tn), lambda i, j, k: (i, j))
    in_specs = [a_spec, b_spec]
    operands = [a, b]
    if add is not None:
        in_specs.append(o_spec)
        operands.append(add)
    if aux is not None:
        in_specs.append(o_spec)
        operands.append(aux)
    if epi == "relu2":
        out_shape = (jax.ShapeDtypeStruct((M, N), BF16), jax.ShapeDtypeStruct((M, N), BF16))
        out_specs = (o_spec, o_spec)
    else:
        out_shape = jax.ShapeDtypeStruct((M, N), out_dtype)
        out_specs = o_spec
    dims = (((0 if ta else 1,), (1 if tb else 0,)), ((), ()))
    has_add, has_aux = add is not None, aux is not None

    def body(*refs):
        a_ref, b_ref = refs[0], refs[1]
        pos = 2
        add_ref = aux_ref = None
        if has_add:
            add_ref = refs[pos]
            pos += 1
        if has_aux:
            aux_ref = refs[pos]
            pos += 1
        n_out = 2 if epi == "relu2" else 1
        out_refs = refs[pos:pos + n_out]
        acc_ref = refs[pos + n_out] if nk > 1 else None

        part = lax.dot_general(a_ref[...].astype(BF16), b_ref[...].astype(BF16), dims,
                               preferred_element_type=F32)

        def finish(acc):
            if has_add:
                acc = acc + add_ref[...].astype(F32)
            if epi == "relu2":
                r = jnp.maximum(acc, 0.0)
                out_refs[0][...] = acc.astype(BF16)
                out_refs[1][...] = (r * r).astype(BF16)
            elif epi == "drelu2":
                u = aux_ref[...].astype(F32)
                out_refs[0][...] = (acc * (2.0 * jnp.maximum(u, 0.0))).astype(out_dtype)
            else:
                out_refs[0][...] = acc.astype(out_dtype)

        if nk == 1:
            finish(part)
        else:
            k = pl.program_id(2)

            @pl.when(k == 0)
            def _():
                acc_ref[...] = part

            @pl.when(k > 0)
            def _():
                acc_ref[...] += part

            @pl.when(k == nk - 1)
            def _():
                finish(acc_ref[...])

    scratch = [pltpu.VMEM((tm, tn), F32)] if nk > 1 else []
    return pl.pallas_call(
        body, name=name, grid=grid, in_specs=in_specs, out_specs=out_specs, out_shape=out_shape,
        scratch_shapes=scratch, compiler_params=_params(("parallel", "parallel", "arbitrary")),
    )(*operands)


def _rms_fwd(x, g, *, name, width=None, seg=0, tm=384):
    T = x.shape[0]
    width = x.shape[1] if width is None else width
    tm = _tile(T, tm, 16)

    def body(x_ref, g_ref, o_ref):
        xf = x_ref[...].astype(F32)
        r = lax.rsqrt(jnp.mean(xf * xf, axis=-1, keepdims=True) + EPS)
        o_ref[...] = (xf * r * g_ref[...]).astype(BF16)

    return pl.pallas_call(
        body, name=name, grid=(T // tm,),
        in_specs=[pl.BlockSpec((tm, width), lambda i: (i, seg)), pl.BlockSpec((1, width), lambda i: (0, 0))],
        out_specs=pl.BlockSpec((tm, width), lambda i: (i, 0)),
        out_shape=jax.ShapeDtypeStruct((T, width), BF16),
        compiler_params=_params(("parallel",)),
    )(x, g)


def _rms_bwd(dy, x, g, *, name, width=None, seg=0, res=None, out_dtype=F32, tm=384):
    T = x.shape[0]
    width = x.shape[1] if width is None else width
    tm = _tile(T, tm, 16)
    has_res = res is not None

    def body(*refs):
        dy_ref, x_ref, g_ref = refs[:3]
        res_ref = refs[3] if has_res else None
        dx_ref, dg_ref = refs[-2:]
        xf = x_ref[...].astype(F32)
        dyf = dy_ref[...].astype(F32)
        r = lax.rsqrt(jnp.mean(xf * xf, axis=-1, keepdims=True) + EPS)
        xhat = xf * r
        dyh = dyf * g_ref[...]
        dx = r * (dyh - xhat * jnp.mean(dyh * xhat, axis=-1, keepdims=True))
        if has_res:
            dx = dx + res_ref[...].astype(F32)
        dx_ref[...] = dx.astype(out_dtype)
        part = jnp.sum(dyf * xhat, axis=0, keepdims=True)

        @pl.when(pl.program_id(0) == 0)
        def _():
            dg_ref[...] = part

        @pl.when(pl.program_id(0) > 0)
        def _():
            dg_ref[...] += part

    row = pl.BlockSpec((tm, width), lambda i: (i, 0))
    in_specs = [row, pl.BlockSpec((tm, width), lambda i: (i, seg)), pl.BlockSpec((1, width), lambda i: (0, 0))]
    operands = [dy, x, g]
    if has_res:
        in_specs.append(row)
        operands.append(res)
    return pl.pallas_call(
        body, name=name, grid=(T // tm,), in_specs=in_specs,
        out_specs=(row, pl.BlockSpec((1, width), lambda i: (0, 0))),
        out_shape=(jax.ShapeDtypeStruct((T, width), out_dtype), jax.ShapeDtypeStruct((1, width), F32)),
        compiler_params=_params(("arbitrary",)),
    )(*operands)


def _down(ext, k):
    return pltpu.roll(ext, k, 0)


def _up(ext, k):
    return pltpu.roll(ext, ext.shape[0] - k, 0)


def _pre_halo(ref, r, R):
    start = pl.multiple_of(jnp.maximum(r * R - HALO, 0), 8)
    keep = (r > 0).astype(F32)
    return ref[pl.ds(start, HALO), :].astype(F32) * keep


def _post_halo(ref, r, R, n_chunks):
    start = pl.multiple_of(jnp.minimum(r * R + R, (n_chunks - 1) * R + R - HALO), 8)
    keep = (r < n_chunks - 1).astype(F32)
    return ref[pl.ds(start, HALO), :].astype(F32) * keep


def _chunk(ref, r, R):
    return ref[pl.ds(pl.multiple_of(r * R, 8), R), :].astype(F32)


def _conv_fwd(rest, conv_w, *, name, dc, tc=128, rows=1056):
    T = rest.shape[0]
    tc = _tile(dc, tc)
    nb = dc // tc
    R = _tile(T, rows, 16)
    n_chunks = T // R

    def body(u_ref, b_ref, c_ref, w_ref, y_ref):
        w0, w1, w2 = w_ref[0:1, :], w_ref[1:2, :], w_ref[2:3, :]

        def chunk(r, carry):
            cu = _chunk(c_ref, r, R) * _chunk(u_ref, r, R)
            ext = jnp.concatenate([_pre_halo(c_ref, r, R) * _pre_halo(u_ref, r, R), cu], axis=0)
            conv = w0 * _down(ext, 2)[HALO:] + w1 * _down(ext, 1)[HALO:] + w2 * cu
            y_ref[pl.ds(pl.multiple_of(r * R, 8), R), :] = (_chunk(b_ref, r, R) * conv).astype(BF16)
            return carry

        lax.fori_loop(0, n_chunks, chunk, 0)

    col = lambda off: pl.BlockSpec((T, tc), lambda j: (0, off * nb + j))
    return pl.pallas_call(
        body, name=name, grid=(nb,),
        in_specs=[col(0), col(1), col(2), pl.BlockSpec((3, tc), lambda j: (0, j))],
        out_specs=pl.BlockSpec((T, tc), lambda j: (0, j)),
        out_shape=jax.ShapeDtypeStruct((T, dc), BF16),
        compiler_params=_params(("parallel",)),
    )(rest, rest, rest, conv_w)


def _conv_bwd(rest, conv_w, dy, *, name, dc, tc=128, rows=1056):
    T = rest.shape[0]
    tc = _tile(dc, tc)
    nb = dc // tc
    R = _tile(T, rows, 16)
    n_chunks = T // R

    def body(u_ref, b_ref, c_ref, w_ref, dy_ref, du_ref, db_ref, dc_ref, dw_ref):
        w0, w1, w2 = w_ref[0:1, :], w_ref[1:2, :], w_ref[2:3, :]

        def chunk(r, carry):
            a0, a1, a2 = carry
            u, b, c = _chunk(u_ref, r, R), _chunk(b_ref, r, R), _chunk(c_ref, r, R)
            dy_c = _chunk(dy_ref, r, R)
            cu = c * u
            ext = jnp.concatenate([_pre_halo(c_ref, r, R) * _pre_halo(u_ref, r, R), cu], axis=0)
            cu1, cu2 = _down(ext, 1)[HALO:], _down(ext, 2)[HALO:]
            conv = w0 * cu2 + w1 * cu1 + w2 * cu
            dconv = dy_c * b
            dext = jnp.concatenate(
                [dconv, _post_halo(dy_ref, r, R, n_chunks) * _post_halo(b_ref, r, R, n_chunks)], axis=0)
            dcu = w2 * dconv + w1 * _up(dext, 1)[:R] + w0 * _up(dext, 2)[:R]
            rows_at = pl.ds(pl.multiple_of(r * R, 8), R)
            db_ref[rows_at, :] = (dy_c * conv).astype(BF16)
            du_ref[rows_at, :] = (dcu * c).astype(BF16)
            dc_ref[rows_at, :] = (dcu * u).astype(BF16)
            return (a0 + jnp.sum(dconv * cu2, axis=0, keepdims=True),
                    a1 + jnp.sum(dconv * cu1, axis=0, keepdims=True),
                    a2 + jnp.sum(dconv * cu, axis=0, keepdims=True))

        zero = jnp.zeros((1, tc), F32)
        a0, a1, a2 = lax.fori_loop(0, n_chunks, chunk, (zero, zero, zero))
        dw_ref[0:1, :] = a0
        dw_ref[1:2, :] = a1
        dw_ref[2:3, :] = a2

    col = lambda off: pl.BlockSpec((T, tc), lambda j: (0, off * nb + j))
    own = pl.BlockSpec((T, tc), lambda j: (0, j))
    return pl.pallas_call(
        body, name=name, grid=(nb,),
        in_specs=[col(0), col(1), col(2), pl.BlockSpec((3, tc), lambda j: (0, j)), own],
        out_specs=(own, own, own, pl.BlockSpec((3, tc), lambda j: (0, j))),
        out_shape=(jax.ShapeDtypeStruct((T, dc), BF16),) * 3 + (jax.ShapeDtypeStruct((3, dc), F32),),
        compiler_params=_params(("parallel",)),
    )(rest, rest, rest, conv_w, dy)


def _window_count(r, R, n_rows, w, first_row_offset):
    t = lax.broadcasted_iota(jnp.int32, (n_rows, 1), 0) + (r * R + first_row_offset)
    return jnp.minimum(t + 1, w).astype(F32)


def _pool_fwd(rest, pool_w, pool_scale, *, name, seg0, pg, rows=1056):
    T = rest.shape[0]
    R = _tile(T, rows, 16)
    n_chunks = T // R
    n_groups = len(POOL_WINDOWS)

    def body(x_ref, w_ref, s_ref, y_ref):
        def run(window):
            def chunk(r, carry):
                g = _chunk(x_ref, r, R)
                s = jnp.concatenate([_pre_halo(x_ref, r, R), g], axis=0)
                k = 1
                while k < window:
                    s = s + _down(s, k)
                    k *= 2
                pooled = s[HALO:] / _window_count(r, R, R, window, 0) - g
                mixed = jnp.dot(pooled.astype(BF16), w_ref[0], preferred_element_type=F32)
                y_ref[pl.ds(pl.multiple_of(r * R, 8), R), :] = (mixed * s_ref[...]).astype(BF16)
                return carry

            lax.fori_loop(0, n_chunks, chunk, 0)

        for gi, window in enumerate(POOL_WINDOWS):
            pl.when(pl.program_id(0) == gi)(functools.partial(run, window))

    return pl.pallas_call(
        body, name=name, grid=(n_groups,),
        in_specs=[pl.BlockSpec((T, pg), lambda g: (0, seg0 + g)),
                  pl.BlockSpec((1, pg, pg), lambda g: (g, 0, 0)),
                  pl.BlockSpec((1, pg), lambda g: (0, g))],
        out_specs=pl.BlockSpec((T, pg), lambda g: (0, g)),
        out_shape=jax.ShapeDtypeStruct((T, n_groups * pg), BF16),
        compiler_params=_params(("parallel",)),
    )(rest, pool_w, pool_scale)


def _pool_bwd(rest, pool_w, pool_scale, dy, *, name, seg0, pg, rows=1056):
    T = rest.shape[0]
    R = _tile(T, rows, 16)
    n_chunks = T // R
    n_groups = len(POOL_WINDOWS)

    def body(x_ref, w_ref, s_ref, dy_ref, dx_ref, dw_ref, ds_ref):
        def run(window):
            def chunk(r, carry):
                dw_acc, ds_acc = carry
                g = _chunk(x_ref, r, R)
                s = jnp.concatenate([_pre_halo(x_ref, r, R), g], axis=0)
                k = 1
                while k < window:
                    s = s + _down(s, k)
                    k *= 2
                pooled = (s[HALO:] / _window_count(r, R, R, window, 0) - g).astype(BF16)
                mixed = jnp.dot(pooled, w_ref[0], preferred_element_type=F32)
                dy_c = _chunk(dy_ref, r, R)
                dm_ext = (jnp.concatenate([dy_c, _post_halo(dy_ref, r, R, n_chunks)], axis=0)
                          * s_ref[...]).astype(BF16)
                dpool_ext = lax.dot_general(dm_ext, w_ref[0], (((1,), (1,)), ((), ())),
                                            preferred_element_type=F32)
                a = dpool_ext / _window_count(r, R, R + HALO, window, 0)
                k = 1
                while k < window:
                    a = a + _up(a, k)
                    k *= 2
                dx_ref[pl.ds(pl.multiple_of(r * R, 8), R), :] = (a[:R] - dpool_ext[:R]).astype(BF16)
                dw_acc = dw_acc + lax.dot_general(pooled, dm_ext[:R], (((0,), (0,)), ((), ())),
                                                  preferred_element_type=F32)
                ds_acc = ds_acc + jnp.sum(dy_c * mixed, axis=0, keepdims=True)
                return dw_acc, ds_acc

            dw_acc, ds_acc = lax.fori_loop(0, n_chunks, chunk,
                                           (jnp.zeros((pg, pg), F32), jnp.zeros((1, pg), F32)))
            dw_ref[0] = dw_acc
            ds_ref[...] = ds_acc

        for gi, window in enumerate(POOL_WINDOWS):
            pl.when(pl.program_id(0) == gi)(functools.partial(run, window))

    own = pl.BlockSpec((T, pg), lambda g: (0, g))
    return pl.pallas_call(
        body, name=name, grid=(n_groups,),
        in_specs=[pl.BlockSpec((T, pg), lambda g: (0, seg0 + g)),
                  pl.BlockSpec((1, pg, pg), lambda g: (g, 0, 0)),
                  pl.BlockSpec((1, pg), lambda g: (0, g)), own],
        out_specs=(own, pl.BlockSpec((1, pg, pg), lambda g: (g, 0, 0)), pl.BlockSpec((1, pg), lambda g: (0, g))),
        out_shape=(jax.ShapeDtypeStruct((T, n_groups * pg), BF16),
                   jax.ShapeDtypeStruct((n_groups, pg, pg), F32),
                   jax.ShapeDtypeStruct((1, n_groups * pg), F32)),
        compiler_params=_params(("parallel",)),
    )(rest, pool_w, pool_scale, dy)


def _rope(r, cos_t, sin_t):
    return r * cos_t + pltpu.roll(r, LANES // 2, 1) * sin_t


def _rope_t(d, cos_t, sin_t):
    return d * cos_t + pltpu.roll(d * sin_t, LANES // 2, 1)


def _qk_fwd(q_raw, k_nope, rest, cos_t, sin_t, q_norm, k_norm, *, name, heads, kr_seg, tm=192):
    T = q_raw.shape[0]
    tm = _tile(T, tm, 16)

    def body(q_ref, kn_ref, kr_ref, c_ref, s_ref, gq_ref, gk_ref, qo_ref, ko_ref):
        cos_b, sin_b = c_ref[...], s_ref[...]
        kr = kr_ref[:, 0:LANES]
        kr_ss = jnp.sum(kr * kr, axis=-1, keepdims=True)
        gq, gk = gq_ref[...], gk_ref[...]
        for h in range(heads):
            lo = h * HEAD_PAD
            q = q_ref[:, lo:lo + HEAD_PAD]
            rq = lax.rsqrt(jnp.sum(q * q, axis=-1, keepdims=True) / QK_HEAD + EPS)
            qn = q * rq * gq
            qo_ref[:, lo:lo + LANES] = qn[:, :LANES].astype(BF16)
            qo_ref[:, lo + LANES:lo + HEAD_PAD] = _rope(qn[:, LANES:], cos_b, sin_b).astype(BF16)
            kn = kn_ref[:, h * LANES:(h + 1) * LANES]
            rk = lax.rsqrt((jnp.sum(kn * kn, axis=-1, keepdims=True) + kr_ss) / QK_HEAD + EPS)
            ko_ref[:, lo:lo + LANES] = (kn * rk * gk[:, :LANES]).astype(BF16)
            ko_ref[:, lo + LANES:lo + HEAD_PAD] = _rope(kr * rk * gk[:, LANES:], cos_b, sin_b).astype(BF16)

    wq, wk = heads * HEAD_PAD, heads * LANES
    return pl.pallas_call(
        body, name=name, grid=(T // tm,),
        in_specs=[pl.BlockSpec((tm, wq), lambda i: (i, 0)), pl.BlockSpec((tm, wk), lambda i: (i, 0)),
                  pl.BlockSpec((tm, HEAD_PAD), lambda i: (i, kr_seg)),
                  pl.BlockSpec((tm, LANES), lambda i: (i, 0)), pl.BlockSpec((tm, LANES), lambda i: (i, 0)),
                  pl.BlockSpec((1, HEAD_PAD), lambda i: (0, 0)), pl.BlockSpec((1, HEAD_PAD), lambda i: (0, 0))],
        out_specs=(pl.BlockSpec((tm, wq), lambda i: (i, 0)), pl.BlockSpec((tm, wq), lambda i: (i, 0))),
        out_shape=(jax.ShapeDtypeStruct((T, wq), BF16), jax.ShapeDtypeStruct((T, wq), BF16)),
        compiler_params=_params(("parallel",)),
    )(q_raw, k_nope, rest, cos_t, sin_t, q_norm, k_norm)


def _qk_bwd(dq, dk, q_raw, k_nope, rest, cos_t, sin_t, q_norm, k_norm, *, name, heads, kr_seg, tm=128):
    T = q_raw.shape[0]
    tm = _tile(T, tm, 16)

    def body(dq_ref, dk_ref, q_ref, kn_ref, kr_ref, c_ref, s_ref, gq_ref, gk_ref,
             dqr_ref, dkn_ref, dkr_ref, dgq_ref, dgk_ref):
        cos_b, sin_b = c_ref[...], s_ref[...]
        kr = kr_ref[:, 0:LANES]
        kr_ss = jnp.sum(kr * kr, axis=-1, keepdims=True)
        gq, gk = gq_ref[...], gk_ref[...]
        dgq = jnp.zeros((1, HEAD_PAD), F32)
        dgk_n = jnp.zeros((1, LANES), F32)
        dgk_r = jnp.zeros((1, LANES), F32)
        dkr = jnp.zeros((tm, LANES), F32)
        for h in range(heads):
            lo = h * HEAD_PAD
            q = q_ref[:, lo:lo + HEAD_PAD]
            rq = lax.rsqrt(jnp.sum(q * q, axis=-1, keepdims=True) / QK_HEAD + EPS)
            qhat = q * rq
            dqn = jnp.concatenate([dq_ref[:, lo:lo + LANES],
                                   _rope_t(dq_ref[:, lo + LANES:lo + HEAD_PAD], cos_b, sin_b)], axis=1)
            dgq = dgq + jnp.sum(dqn * qhat, axis=0, keepdims=True)
            dqh = dqn * gq
            dqr_ref[:, lo:lo + HEAD_PAD] = (
                rq * (dqh - qhat * (jnp.sum(dqh * qhat, axis=-1, keepdims=True) / QK_HEAD))).astype(BF16)
            kn = kn_ref[:, h * LANES:(h + 1) * LANES]
            rk = lax.rsqrt((jnp.sum(kn * kn, axis=-1, keepdims=True) + kr_ss) / QK_HEAD + EPS)
            khat_n, khat_r = kn * rk, kr * rk
            dkn_n = dk_ref[:, lo:lo + LANES]
            dkn_r = _rope_t(dk_ref[:, lo + LANES:lo + HEAD_PAD], cos_b, sin_b)
            dgk_n = dgk_n + jnp.sum(dkn_n * khat_n, axis=0, keepdims=True)
            dgk_r = dgk_r + jnp.sum(dkn_r * khat_r, axis=0, keepdims=True)
            dkh_n, dkh_r = dkn_n * gk[:, :LANES], dkn_r * gk[:, LANES:]
            proj = (jnp.sum(dkh_n * khat_n, axis=-1, keepdims=True)
                    + jnp.sum(dkh_r * khat_r, axis=-1, keepdims=True)) / QK_HEAD
            dkn_ref[:, h * LANES:(h + 1) * LANES] = (rk * (dkh_n - khat_n * proj)).astype(BF16)
            dkr = dkr + rk * (dkh_r - khat_r * proj)
        dkr_ref[:, 0:LANES] = dkr.astype(BF16)
        dkr_ref[:, LANES:HEAD_PAD] = jnp.zeros((tm, HEAD_PAD - LANES), BF16)
        dgk = jnp.concatenate([dgk_n, dgk_r], axis=1)

        @pl.when(pl.program_id(0) == 0)
        def _():
            dgq_ref[...] = dgq
            dgk_ref[...] = dgk

        @pl.when(pl.program_id(0) > 0)
        def _():
            dgq_ref[...] += dgq
            dgk_ref[...] += dgk

    wq, wk = heads * HEAD_PAD, heads * LANES
    row = lambda w: pl.BlockSpec((tm, w), lambda i: (i, 0))
    vec = pl.BlockSpec((1, HEAD_PAD), lambda i: (0, 0))
    return pl.pallas_call(
        body, name=name, grid=(T // tm,),
        in_specs=[row(wq), row(wq), row(wq), row(wk), pl.BlockSpec((tm, HEAD_PAD), lambda i: (i, kr_seg)),
                  row(LANES), row(LANES), vec, vec],
        out_specs=(row(wq), row(wk), row(HEAD_PAD), vec, vec),
        out_shape=(jax.ShapeDtypeStruct((T, wq), BF16), jax.ShapeDtypeStruct((T, wk), BF16),
                   jax.ShapeDtypeStruct((T, HEAD_PAD), BF16),
                   jax.ShapeDtypeStruct((1, HEAD_PAD), F32), jax.ShapeDtypeStruct((1, HEAD_PAD), F32)),
        compiler_params=_params(("arbitrary",)),
    )(dq, dk, q_raw, k_nope, rest, cos_t, sin_t, q_norm, k_norm)


def _causal_mask(s):
    row = lax.broadcasted_iota(jnp.int32, s.shape, 0)
    col = lax.broadcasted_iota(jnp.int32, s.shape, 1)
    return jnp.where(row >= col, s, NEG)


def _flash_fwd(q, k, v, *, name, heads, tq=384, hp=2):
    T = q.shape[0]
    tq = _tile(T, tq, LANES)
    nq = T // tq
    scale = QK_HEAD ** -0.5
    nt = (((1,), (1,)), ((), ()))

    def body(q_ref, k_ref, v_ref, o_ref, lse_ref):
        def q_block(i, carry):
            q_at = pl.ds(pl.multiple_of(i * tq, tq), tq)
            qbs = [q_ref[q_at, h * HEAD_PAD:(h + 1) * HEAD_PAD] for h in range(hp)]

            def step(j, state, masked):
                k_at = pl.ds(pl.multiple_of(j * tq, tq), tq)
                new = []
                for h in range(hp):
                    m, l, acc = state[h]
                    s = lax.dot_general(qbs[h], k_ref[k_at, h * HEAD_PAD:(h + 1) * HEAD_PAD], nt,
                                        preferred_element_type=F32) * scale
                    if masked:
                        s = _causal_mask(s)
                    m_new = jnp.maximum(m, jnp.max(s, axis=-1, keepdims=True))
                    p = jnp.exp(s - m_new)
                    alpha = jnp.exp(m - m_new)
                    l = alpha * l + jnp.sum(p, axis=-1, keepdims=True)
                    acc = alpha * acc + jnp.dot(p.astype(BF16), v_ref[k_at, h * V_HEAD:(h + 1) * V_HEAD],
                                                preferred_element_type=F32)
                    new.append((m_new, l, acc))
                return tuple(new)

            init = tuple((jnp.full((tq, 1), NEG, F32), jnp.zeros((tq, 1), F32), jnp.zeros((tq, V_HEAD), F32))
                         for _ in range(hp))
            state = lax.fori_loop(0, i, lambda j, st: step(j, st, False), init)
            state = step(i, state, True)
            for h in range(hp):
                m, l, acc = state[h]
                o_ref[q_at, h * V_HEAD:(h + 1) * V_HEAD] = (acc / l).astype(BF16)
                lse_ref[h, q_at, :] = jnp.broadcast_to(m + jnp.log(l), (tq, LANES))
            return carry

        lax.fori_loop(0, nq, q_block, 0)

    qk_spec = pl.BlockSpec((T, hp * HEAD_PAD), lambda g: (0, g))
    v_spec = pl.BlockSpec((T, hp * V_HEAD), lambda g: (0, g))
    return pl.pallas_call(
        body, name=name, grid=(heads // hp,), in_specs=[qk_spec, qk_spec, v_spec],
        out_specs=(v_spec, pl.BlockSpec((hp, T, LANES), lambda g: (g, 0, 0))),
        out_shape=(jax.ShapeDtypeStruct((T, heads * V_HEAD), BF16), jax.ShapeDtypeStruct((heads, T, LANES), F32)),
        compiler_params=_params(("parallel",)),
    )(q, k, v)


def _flash_bwd(q, k, v, o, do, lse, *, name, heads, tq=384):
    T = q.shape[0]
    tq = _tile(T, tq, LANES)
    nq = T // tq
    scale = QK_HEAD ** -0.5
    nt = (((1,), (1,)), ((), ()))
    tn = (((0,), (0,)), ((), ()))

    def body(q_ref, k_ref, v_ref, o_ref, do_ref, lse_ref, dq_ref, dk_ref, dv_ref, delta_ref):
        def fill_delta(i, carry):
            at = pl.ds(pl.multiple_of(i * tq, tq), tq)
            d = jnp.sum(o_ref[at, :].astype(F32) * do_ref[at, :].astype(F32), axis=-1, keepdims=True)
            delta_ref[at, :] = jnp.broadcast_to(d, (tq, LANES))
            dq_ref[at, :] = jnp.zeros((tq, HEAD_PAD), F32)
            return carry

        lax.fori_loop(0, nq, fill_delta, 0)

        def kv_block(j, carry):
            k_at = pl.ds(pl.multiple_of(j * tq, tq), tq)
            kb, vb = k_ref[k_at, :], v_ref[k_at, :]

            def step(i, state, masked):
                dk_acc, dv_acc = state
                q_at = pl.ds(pl.multiple_of(i * tq, tq), tq)
                qb, dob = q_ref[q_at, :], do_ref[q_at, :]
                s = lax.dot_general(qb, kb, nt, preferred_element_type=F32) * scale
                if masked:
                    s = _causal_mask(s)
                p = jnp.exp(s - lse_ref[0, q_at, :][:, 0:1])
                dv_acc = dv_acc + lax.dot_general(p.astype(BF16), dob, tn, preferred_element_type=F32)
                dp = lax.dot_general(dob, vb, nt, preferred_element_type=F32)
                ds = (p * (dp - delta_ref[q_at, :][:, 0:1]) * scale).astype(BF16)
                dk_acc = dk_acc + lax.dot_general(ds, qb, tn, preferred_element_type=F32)
                dq_ref[q_at, :] += jnp.dot(ds, kb, preferred_element_type=F32)
                return dk_acc, dv_acc

            state = step(j, (jnp.zeros((tq, HEAD_PAD), F32), jnp.zeros((tq, V_HEAD), F32)), True)
            rest = nq - 1 - j

            def two_steps(t, st):
                i0 = j + 1 + 2 * t
                return step(i0 + 1, step(i0, st, False), False)

            state = lax.fori_loop(0, rest // 2, two_steps, state)
            dk_acc, dv_acc = lax.cond(rest % 2 == 1, lambda st: step(nq - 1, st, False), lambda st: st, state)
            dk_ref[k_at, :] = dk_acc
            dv_ref[k_at, :] = dv_acc.astype(BF16)
            return carry

        lax.fori_loop(0, nq, kv_block, 0)

    qk_spec = pl.BlockSpec((T, HEAD_PAD), lambda h: (0, h))
    v_spec = pl.BlockSpec((T, V_HEAD), lambda h: (0, h))
    return pl.pallas_call(
        body, name=name, grid=(heads,),
        in_specs=[qk_spec, qk_spec, v_spec, v_spec, v_spec, pl.BlockSpec((1, T, LANES), lambda h: (h, 0, 0))],
        out_specs=(qk_spec, qk_spec, v_spec),
        out_shape=(jax.ShapeDtypeStruct((T, heads * HEAD_PAD), F32), jax.ShapeDtypeStruct((T, heads * HEAD_PAD), F32),
                   jax.ShapeDtypeStruct((T, heads * V_HEAD), BF16)),
        scratch_shapes=[pltpu.VMEM((T, LANES), F32)],
        compiler_params=_params(("parallel",)),
    )(q, k, v, o, do, lse)


def _merge_fwd(gl, pa, pb, pc, *, name, d, tm=384, tn=1024):
    T = pa.shape[0]
    tm, tn = _tile(T, tm, 16), _tile(d, tn)
    nb = d // tn

    def body(g0, g1, g2, a, b, c, o_ref):
        o_ref[...] = (jax.nn.sigmoid(g0[...]) * a[...] + jax.nn.sigmoid(g1[...]) * b[...]
                      + jax.nn.sigmoid(g2[...]) * c[...]).astype(BF16)

    gate = lambda n: pl.BlockSpec((tm, tn), lambda i, j: (i, n * nb + j))
    blk = pl.BlockSpec((tm, tn), lambda i, j: (i, j))
    return pl.pallas_call(
        body, name=name, grid=(T // tm, nb), in_specs=[gate(0), gate(1), gate(2), blk, blk, blk],
        out_specs=blk, out_shape=jax.ShapeDtypeStruct((T, d), BF16),
        compiler_params=_params(("parallel", "parallel")),
    )(gl, gl, gl, pa, pb, pc)


def _merge_bwd(dm, gl, pa, pb, pc, *, name, d, tm=384, tn=1024):
    T = pa.shape[0]
    tm, tn = _tile(T, tm, 16), _tile(d, tn)
    nb = d // tn

    def body(dm_ref, g0, g1, g2, a, b, c, da, db, dc, dg0, dg1, dg2):
        dmv = dm_ref[...]
        for g_ref, p_ref, dp_ref, dg_ref in ((g0, a, da, dg0), (g1, b, db, dg1), (g2, c, dc, dg2)):
            sg = jax.nn.sigmoid(g_ref[...])
            dp_ref[...] = (dmv * sg).astype(BF16)
            dg_ref[...] = (dmv * p_ref[...] * sg * (1.0 - sg)).astype(BF16)

    gate = lambda n: pl.BlockSpec((tm, tn), lambda i, j: (i, n * nb + j))
    blk = pl.BlockSpec((tm, tn), lambda i, j: (i, j))
    return pl.pallas_call(
        body, name=name, grid=(T // tm, nb), in_specs=[blk, gate(0), gate(1), gate(2), blk, blk, blk],
        out_specs=(blk,) * 6, out_shape=(jax.ShapeDtypeStruct((T, d), BF16),) * 6,
        compiler_params=_params(("parallel", "parallel")),
    )(dm, gl, gl, gl, pa, pb, pc)


def _loss(y, target, *, name, first, last, tm=384):
    T, d = y.shape
    tm = _tile(T, tm, 16)

    def body(y_ref, t_ref, loss_ref, dy_ref):
        i = pl.program_id(0)
        row = lax.broadcasted_iota(jnp.int32, (tm, 1), 0) + i * tm
        real = jnp.logical_and(row >= first, row < last)
        err = jnp.where(real, y_ref[...] - t_ref[...], 0.0)
        dy_ref[...] = err * (1.0 / d)
        part = jnp.broadcast_to(jnp.sum(err * err, keepdims=True).reshape(1, 1), (1, LANES))

        @pl.when(i == 0)
        def _():
            loss_ref[...] = part

        @pl.when(i > 0)
        def _():
            loss_ref[...] += part

    blk = pl.BlockSpec((tm, d), lambda i: (i, 0))
    return pl.pallas_call(
        body, name=name, grid=(T // tm,), in_specs=[blk, blk],
        out_specs=(pl.BlockSpec((1, LANES), lambda i: (0, 0)), blk),
        out_shape=(jax.ShapeDtypeStruct((1, LANES), F32), jax.ShapeDtypeStruct((T, d), F32)),
        compiler_params=_params(("arbitrary",)),
    )(y, target)


def _as3d(a):
    return a.reshape(a.shape[0], -1, a.shape[-1])


def _sum_stack(parts, *, name, out_dtype, rows=256):
    n, R, C = parts.shape
    tr = _tile(R, rows, 16)

    def body(p_ref, o_ref):
        acc = p_ref[0].astype(F32)
        for s in range(1, n):
            acc = acc + p_ref[s].astype(F32)
        o_ref[...] = acc.astype(out_dtype)

    return pl.pallas_call(
        body, name=name, grid=(R // tr,),
        in_specs=[pl.BlockSpec((n, tr, C), lambda i: (0, i, 0))],
        out_specs=pl.BlockSpec((tr, C), lambda i: (i, 0)),
        out_shape=jax.ShapeDtypeStruct((R, C), out_dtype),
        compiler_params=_params(("parallel",)),
    )(parts)


def _adamw(w, g, m, v, *, name, rows=128):
    R, C = w.shape
    tr = _tile(R, rows, 8)
    c1 = 1.0 - ADAM_B1 ** ADAM_STEP
    c2 = 1.0 - ADAM_B2 ** ADAM_STEP

    def body(w_ref, g_ref, m_ref, v_ref, d_ref, nm_ref, nv_ref):
        gv = g_ref[...]
        nm = ADAM_B1 * m_ref[...] + (1.0 - ADAM_B1) * gv
        nv = ADAM_B2 * v_ref[...] + (1.0 - ADAM_B2) * (gv * gv)
        nm_ref[...] = nm
        nv_ref[...] = nv
        d_ref[...] = -ADAM_LR * ((nm / c1) / (jnp.sqrt(nv / c2) + ADAM_EPS) + ADAM_WD * w_ref[...])

    blk = pl.BlockSpec((tr, C), lambda i: (i, 0))
    return pl.pallas_call(
        body, name=name, grid=(R // tr,), in_specs=[blk] * 4, out_specs=(blk,) * 3,
        out_shape=(jax.ShapeDtypeStruct((R, C), F32),) * 3,
        compiler_params=_params(("parallel",)),
    )(w, g, m, v)


def _one_hot(index, n):
    return jnp.broadcast_to((jnp.arange(n) == index).astype(F32)[:, None, None], (n, 8, LANES))


def _is_set(flags_ref, s):
    return flags_ref[s, 0:1, 0:1] > 0.5


def _pair_sum(pieces, recv, core, *, name, rows=256):
    _, H, C = recv.shape
    tr = _tile(H, rows, 16)
    nh = H // tr

    def body(lo_ref, hi_ref, r_ref, core_ref, o_ref):
        mine = jnp.where(_is_set(core_ref, 0), lo_ref[0], hi_ref[0])
        o_ref[0] = (mine.astype(F32) + r_ref[0].astype(F32)).astype(BF16)

    blk = pl.BlockSpec((1, tr, C), lambda j, i: (j, i, 0))
    return pl.pallas_call(
        body, name=name, grid=(4, nh),
        in_specs=[blk, pl.BlockSpec((1, tr, C), lambda j, i: (j, nh + i, 0)), blk,
                  pl.BlockSpec((2, 8, LANES), lambda j, i: (0, 0, 0))],
        out_specs=blk, out_shape=jax.ShapeDtypeStruct((4, H, C), BF16),
        compiler_params=_params(("parallel", "parallel")),
    )(pieces, pieces, recv, core)


def _chip_sum(pair, landed, chip_flags, *, name, rows=256):
    _, H, C = pair.shape
    tr = _tile(H, rows, 16)

    def body(p_ref, l_ref, chip_ref, o_ref):
        acc = None
        for s in range(4):
            part = jnp.where(_is_set(chip_ref, s), p_ref[s], l_ref[s]).astype(F32)
            acc = part if acc is None else acc + part
        o_ref[...] = acc

    blk = pl.BlockSpec((4, tr, C), lambda i: (0, i, 0))
    return pl.pallas_call(
        body, name=name, grid=(H // tr,),
        in_specs=[blk, blk, pl.BlockSpec((4, 8, LANES), lambda i: (0, 0, 0))],
        out_specs=pl.BlockSpec((tr, C), lambda i: (i, 0)), out_shape=jax.ShapeDtypeStruct((H, C), F32),
        compiler_params=_params(("parallel",)),
    )(pair, landed, chip_flags)


def _adamw_layer(w, m, v, total, recv, core, layer, prev, *, name, rows=128):
    _, R, C = w.shape
    H = R // 2
    tr = _tile(H, rows, 8)
    nh = H // tr
    c1 = 1.0 - ADAM_B1 ** ADAM_STEP
    c2 = 1.0 - ADAM_B2 ** ADAM_STEP
    n_prev = 0 if prev is None else 4

    def body(*refs):
        w_ref, m_ref, v_ref, t_ref, r_ref, core_ref = refs[:6]
        g_ref, d_ref, nm_ref, nv_ref = refs[6 + n_prev:]
        half_is_mine = jnp.where(pl.program_id(0) == 0, core_ref[0, 0:1, 0:1], core_ref[1, 0:1, 0:1]) > 0.5
        gv = jnp.where(half_is_mine, t_ref[...], r_ref[...])
        nm = ADAM_B1 * m_ref[0] + (1.0 - ADAM_B1) * gv
        nv = ADAM_B2 * v_ref[0] + (1.0 - ADAM_B2) * (gv * gv)
        g_ref[0] = gv
        nm_ref[0] = nm
        nv_ref[0] = nv
        d_ref[0] = -ADAM_LR * ((nm / c1) / (jnp.sqrt(nv / c2) + ADAM_EPS) + ADAM_WD * w_ref[0])

    lay = pl.BlockSpec((1, tr, C), lambda hf, i: (layer, hf * nh + i, 0))
    one = pl.BlockSpec((tr, C), lambda hf, i: (i, 0))
    operands = [w, m, v, total, recv, core] + ([] if prev is None else list(prev))
    return pl.pallas_call(
        body, name=name, grid=(2, nh),
        in_specs=[lay, lay, lay, one, one, pl.BlockSpec((2, 8, LANES), lambda hf, i: (0, 0, 0))] + [ANY] * n_prev,
        out_specs=(lay,) * 4, out_shape=(jax.ShapeDtypeStruct((2, R, C), F32),) * 4,
        input_output_aliases={6 + i: i for i in range(n_prev)},
        compiler_params=_params(("parallel", "parallel")),
    )(*operands)


ANY = pl.BlockSpec(memory_space=pl.ANY)


def _coords():
    return lax.axis_index("x"), lax.axis_index("y"), lax.axis_index("c")


HBM = pl.BlockSpec(memory_space=pltpu.HBM)
SEM = pl.BlockSpec(memory_space=pltpu.SEMAPHORE)
EFFECT = pltpu.SideEffectType.DATAFLOW_SIDE_EFFECTING


def _copies(plan, bufs, send_sems, recv_sems):
    return [pltpu.make_async_remote_copy(src_ref=s, dst_ref=d, send_sem=send_sems.at[i], recv_sem=recv_sems.at[i],
                                         device_id=to, device_id_type=MESH)
            for i, (s, d, to) in enumerate(plan(bufs))]


def _start_copies(bufs, groups, *, name):
    nb, ng = len(bufs), len(groups)

    def body(*refs):
        buf_refs = refs[:nb]
        sems = refs[nb:nb + 2 * ng]
        token = refs[-1]
        for g, (plan, _) in enumerate(groups):
            for cp in _copies(plan, buf_refs, sems[2 * g], sems[2 * g + 1]):
                cp.start()
        token[...] = jnp.zeros_like(token)

    sem_shapes = []
    for _, n in groups:
        sem_shapes += [pltpu.SemaphoreType.DMA((n,)), pltpu.SemaphoreType.DMA((n,))]
    out = pl.pallas_call(
        body, name=name, in_specs=[HBM] * nb,
        out_specs=tuple([SEM] * (2 * ng) + [HBM] * nb + [pl.BlockSpec(memory_space=pltpu.VMEM)]),
        out_shape=tuple(sem_shapes + [pltpu.HBM(b.shape, b.dtype) for b in bufs] + [jax.ShapeDtypeStruct((8, LANES), F32)]),
        input_output_aliases={i: 2 * ng + i for i in range(nb)},
        compiler_params=pltpu.CompilerParams(has_side_effects=EFFECT),
    )(*[pltpu.with_memory_space_constraint(b, pltpu.HBM) for b in bufs])
    sems = [(out[2 * g], out[2 * g + 1]) for g in range(ng)]
    return sems, list(out[2 * ng:2 * ng + nb]), out[-1]


def _wait_copies(bufs, sems, plan, after, *, name):
    nb = len(bufs)

    def body(*refs):
        buf_refs = refs[:nb]
        for cp in _copies(plan, buf_refs, refs[nb], refs[nb + 1]):
            cp.wait_send()
            cp.wait_recv()

    out = pl.pallas_call(
        body, name=name, in_specs=[HBM] * nb + [SEM, SEM, ANY], out_specs=tuple([HBM] * nb),
        out_shape=tuple(pltpu.HBM(b.shape, b.dtype) for b in bufs),
        input_output_aliases={i: i for i in range(nb)},
        compiler_params=pltpu.CompilerParams(has_side_effects=EFFECT),
    )(*bufs, sems[0], sems[1], after)
    return list(out)


def _half(ref, c):
    h = ref.shape[0] // 2
    return ref.at[pl.ds(c * h, h)]


def _ici_gather_plan(pairs):
    def plan(refs):
        x, y, c = _coords()
        me = 2 * x + y
        out = []
        for s, d in pairs:
            for cx, cy in [(1 - x, y), (x, 1 - y), (1 - x, 1 - y)]:
                out.append((_half(refs[s], c), _half(refs[d].at[me], c), (cx, cy, c)))
            out.append((refs[s], refs[d].at[me], (x, y, 1 - c)))
        return out
    return plan, 4 * len(pairs)


def _d2d_forward_plan(lands):
    def plan(refs):
        x, y, c = _coords()
        out = []
        for d in lands:
            for cx, cy in [(1 - x, y), (x, 1 - y), (1 - x, 1 - y)]:
                got = _half(refs[d].at[2 * cx + cy], c)
                out.append((got, got, (x, y, 1 - c)))
        return out
    return plan, 3 * len(lands)


def _swap_half_plan(pairs):
    def plan(refs):
        x, y, c = _coords()
        out = []
        for s, d in pairs:
            h = refs[d].shape[1]
            out.append((refs[s].at[:, pl.ds((1 - c) * h, h)], refs[d], (x, y, 1 - c)))
        return out
    return plan, len(pairs)


def _scatter_plan(pairs):
    def plan(refs):
        x, y, c = _coords()
        me = 2 * x + y
        out = []
        for s, d in pairs:
            for cx, cy in [(1 - x, y), (x, 1 - y), (1 - x, 1 - y)]:
                out.append((refs[s].at[2 * cx + cy], refs[d].at[me], (cx, cy, c)))
        return out
    return plan, 3 * len(pairs)


def _swap_total_plan(pairs):
    def plan(refs):
        x, y, c = _coords()
        return [(refs[s], refs[d], (x, y, 1 - c)) for s, d in pairs]
    return plan, len(pairs)


def _gather_all(block, *, name):
    def body(src, out, send_sems, recv_sems, local_sem):
        x, y, c = _coords()
        me = 4 * x + 2 * y + c
        flips = [(fx, fy, fc) for fx in (0, 1) for fy in (0, 1) for fc in (0, 1)][1:]
        mine = pltpu.make_async_copy(src, out.at[me], local_sem)
        mine.start()
        peers = [(x ^ fx, y ^ fy, c ^ fc) for fx, fy, fc in flips]
        cps = [pltpu.make_async_remote_copy(src_ref=src, dst_ref=out.at[me], send_sem=send_sems.at[k],
                                            recv_sem=recv_sems.at[k], device_id=peer, device_id_type=MESH)
               for k, peer in enumerate(peers)]
        for cp in cps:
            cp.start()
        for k, (px, py, pc) in enumerate(peers):
            slot = out.at[4 * px + 2 * py + pc]
            pltpu.make_async_remote_copy(src_ref=slot, dst_ref=slot, send_sem=send_sems.at[k], recv_sem=recv_sems.at[k],
                                         device_id=(px, py, pc), device_id_type=MESH).wait_recv()
        for cp in cps:
            cp.wait_send()
        mine.wait()

    return pl.pallas_call(
        body, name=name, in_specs=[ANY], out_specs=ANY,
        out_shape=jax.ShapeDtypeStruct((8,) + block.shape, block.dtype),
        scratch_shapes=[pltpu.SemaphoreType.DMA((7,)), pltpu.SemaphoreType.DMA((7,)), pltpu.SemaphoreType.DMA],
    )(block)


def _cols(o):
    return jnp.transpose(o, (1, 0, 2)).reshape(o.shape[1], -1)


def _uncols(full):
    return jnp.transpose(full.reshape(full.shape[0], 4, -1), (1, 0, 2))


def _rope_pad(x1, x2):
    z = jnp.zeros_like(x1)
    return jnp.concatenate([x1, z, x2, z], axis=-1)


def _head_pad(w, heads):
    r = w.reshape(w.shape[0], heads, QK_HEAD)
    half = QK_ROPE // 2
    out = jnp.concatenate([r[..., :QK_NOPE], _rope_pad(r[..., QK_NOPE:QK_NOPE + half], r[..., QK_NOPE + half:])], axis=-1)
    return out.reshape(w.shape[0], heads * HEAD_PAD)


def _head_unpad(w, heads):
    r = w.reshape(w.shape[0], heads, HEAD_PAD)
    half = QK_ROPE // 2
    out = jnp.concatenate([r[..., :QK_NOPE], r[..., QK_NOPE:QK_NOPE + half],
                           r[..., QK_NOPE + 2 * half:QK_NOPE + 3 * half]], axis=-1)
    return out.reshape(w.shape[0], heads * QK_HEAD)


class _Dims:
    def __init__(self, d, seq):
        self.d = d
        self.seq = seq
        self.t_real = N_META + seq
        self.t = -(-self.t_real // LANES) * LANES
        self.dc = d // 2
        self.dp = d // 2
        self.pg = self.dp // len(POOL_WINDOWS)
        self.heads = d // 128
        self.dff = 4 * d
        self.a_end = 3 * self.dc
        self.q_end = self.a_end + Q_LORA
        self.kv_end = self.q_end + KV_LORA
        self.kr_end = self.kv_end + QK_ROPE
        self.pool_end = self.kr_end + self.dp
        self.d_in = self.pool_end + 3 * d
        self.r_pool = 3 * self.dc
        self.r_q = self.r_pool + self.dp
        self.r_kv = self.r_q + Q_LORA
        self.r_kr = self.r_kv + KV_LORA
        self.r_width = self.r_kr + HEAD_PAD


def _in_weights(dm, w_in_pieces):
    w_in = _cols(w_in_pieces)
    half = QK_ROPE // 2
    kr = w_in[:, dm.kv_end:dm.kr_end]
    kr_p = jnp.concatenate([_rope_pad(kr[:, :half], kr[:, half:]), jnp.zeros((dm.d, HEAD_PAD - LANES), BF16)], axis=1)
    return dict(
        wg=w_in[:, dm.pool_end:],
        wr=jnp.concatenate([w_in[:, :dm.a_end], w_in[:, dm.kr_end:dm.pool_end], w_in[:, dm.a_end:dm.kv_end], kr_p], axis=1))


def _other_weights(dm, g):
    w_ukv = _cols(g["w_ukv"]).reshape(KV_LORA, dm.heads, QK_NOPE + V_HEAD)
    return dict(
        wuq=_head_pad(_cols(g["w_uq"]), dm.heads),
        wkn=w_ukv[:, :, :QK_NOPE].reshape(KV_LORA, dm.heads * QK_NOPE),
        wv=w_ukv[:, :, QK_NOPE:].reshape(KV_LORA, dm.heads * V_HEAD),
        wp=jnp.transpose(g["pool_w"], (1, 0, 2, 3)).reshape(len(POOL_WINDOWS), dm.pg, dm.pg),
        wba=_cols(g["w_branch_a"]), wbb=g["w_branch_b"].reshape(-1, dm.d), wbc=_cols(g["w_branch_c"]),
        wo=g["w_o"].reshape(-1, dm.d), wup=_cols(g["w_up"]), wdn=g["w_down"].reshape(-1, dm.d))


def _small_weights(small):
    return dict(
        conv_w=small["conv_w"],
        attn_norm=small["attn_norm"][None], mlp_norm=small["mlp_norm"][None],
        q_lat_norm=small["q_lat_norm"][None], kv_lat_norm=small["kv_lat_norm"][None],
        q_norm=_head_pad(small["q_norm"][None], 1), k_norm=_head_pad(small["k_norm"][None], 1),
        pool_scale=small["pool_scale"][None],
    )


def _layer_grad_pieces(dm, dw):
    half = QK_ROPE // 2
    dwr, dwg = dw["wr"], dw["wg"]
    d_in = jnp.concatenate([
        dwr[:, :dm.r_pool], dwr[:, dm.r_q:dm.r_kr], dwr[:, dm.r_kr:dm.r_kr + half],
        dwr[:, dm.r_kr + 2 * half:dm.r_kr + 3 * half], dwr[:, dm.r_pool:dm.r_q], dwg], axis=1)
    d_ukv = jnp.concatenate([dw["wkn"].reshape(KV_LORA, dm.heads, QK_NOPE),
                             dw["wv"].reshape(KV_LORA, dm.heads, V_HEAD)], axis=-1).reshape(KV_LORA, -1)
    rows = lambda a: a.reshape((4, a.shape[0] // 4) + a.shape[1:])
    out = dict(
        w_in=_uncols(d_in), w_uq=_uncols(_head_unpad(dw["wuq"], dm.heads)), w_ukv=_uncols(d_ukv),
        pool_w=jnp.transpose(dw["wp"].reshape(len(POOL_WINDOWS), 4, dm.pg // 4, dm.pg), (1, 0, 2, 3)),
        w_branch_a=_uncols(dw["wba"]), w_branch_b=rows(dw["wbb"]), w_branch_c=_uncols(dw["wbc"]),
        w_o=rows(dw["wo"]), w_up=_uncols(dw["wup"]), w_down=rows(dw["wdn"]),
    )
    return {k: v.astype(BF16) for k, v in out.items()}


def _layer_fwd(dm, W, x, cos_t, sin_t, tag, more=None):
    n = lambda s: f"{s}_{tag}"
    h = _rms_fwd(x, W["attn_norm"], name=n("attn_norm"))
    gl = _mm(h, W["wg"], name=n("proj_gates"))
    rest = _mm(h, W["wr"], name=n("proj_rest"))
    if more is not None:
        W.update(more(rest))
    y_a = _conv_fwd(rest, W["conv_w"], name=n("conv"), dc=dm.dc)
    y_c = _pool_fwd(rest, W["wp"], W["pool_scale"], name=n("pool"), seg0=dm.r_pool // dm.pg, pg=dm.pg)
    q_lat = _rms_fwd(rest, W["q_lat_norm"], name=n("q_lat_norm"), width=Q_LORA, seg=dm.r_q // Q_LORA)
    kv_lat = _rms_fwd(rest, W["kv_lat_norm"], name=n("kv_lat_norm"), width=KV_LORA, seg=dm.r_kv // KV_LORA)
    q_raw = _mm(q_lat, W["wuq"], name=n("up_q"))
    k_nope = _mm(kv_lat, W["wkn"], name=n("up_k"))
    v = _mm(kv_lat, W["wv"], name=n("up_v"), out_dtype=BF16)
    q, k = _qk_fwd(q_raw, k_nope, rest, cos_t, sin_t, W["q_norm"], W["k_norm"], name=n("qk_norm_rope"),
                   heads=dm.heads, kr_seg=dm.r_kr // HEAD_PAD)
    y_b, lse = _flash_fwd(q, k, v, name=n("attention"), heads=dm.heads)
    pa = _mm(y_a, W["wba"], name=n("branch_a"))
    pb = _mm(y_b, W["wbb"], name=n("branch_b"))
    pc = _mm(y_c, W["wbc"], name=n("branch_c"))
    merged = _merge_fwd(gl, pa, pb, pc, name=n("merge"), d=dm.d)
    x1 = _mm(merged, W["wo"], name=n("out_proj"), add=x)
    h2 = _rms_fwd(x1, W["mlp_norm"], name=n("mlp_norm"))
    up, act = _mm(h2, W["wup"], name=n("mlp_up"), epi="relu2")
    x2 = _mm(act, W["wdn"], name=n("mlp_down"), add=x1, tk=2048)
    saved = dict(x=x, h=h, gl=gl, rest=rest, y_a=y_a, y_c=y_c, q_lat=q_lat, kv_lat=kv_lat, q_raw=q_raw, k_nope=k_nope,
                 v=v, q=q, k=k, y_b=y_b, lse=lse, pa=pa, pb=pb, pc=pc, merged=merged, x1=x1, h2=h2, up=up, act=act)
    return x2, saved


def _layer_bwd(dm, W, S, dx2, cos_t, sin_t, tag):
    n = lambda s: f"{s}_{tag}"
    dw, ds = {}, {}
    dup = _mm(dx2, W["wdn"], name=n("d_mlp_down"), tb=True, aux=S["up"], epi="drelu2", out_dtype=BF16)
    dw["wdn"] = _mm(S["act"], dx2, name=n("dw_mlp_down"), ta=True, tm=1024, tk=1408)
    dh2 = _mm(dup, W["wup"], name=n("d_mlp_up"), tb=True, tk=2048)
    dw["wup"] = _mm(S["h2"], dup, name=n("dw_mlp_up"), ta=True, tm=1024, tk=1408)
    dx1, ds["mlp_norm"] = _rms_bwd(dh2, S["x1"], W["mlp_norm"], name=n("d_mlp_norm"), res=dx2)
    dmerged = _mm(dx1, W["wo"], name=n("d_out_proj"), tb=True)
    dw["wo"] = _mm(S["merged"], dx1, name=n("dw_out_proj"), ta=True, tm=1024, tk=1408)
    dpa, dpb, dpc, dg0, dg1, dg2 = _merge_bwd(dmerged, S["gl"], S["pa"], S["pb"], S["pc"], name=n("d_merge"), d=dm.d)
    dgl = jnp.concatenate([dg0, dg1, dg2], axis=1)
    dy_a = _mm(dpa, W["wba"], name=n("d_branch_a"), tb=True)
    dw["wba"] = _mm(S["y_a"], dpa, name=n("dw_branch_a"), ta=True, tm=1024, tk=1408)
    dy_b = _mm(dpb, W["wbb"], name=n("d_branch_b"), tb=True, out_dtype=BF16)
    dw["wbb"] = _mm(S["y_b"], dpb, name=n("dw_branch_b"), ta=True, tm=1024, tk=1408)
    dy_c = _mm(dpc, W["wbc"], name=n("d_branch_c"), tb=True)
    dw["wbc"] = _mm(S["y_c"], dpc, name=n("dw_branch_c"), ta=True, tm=1024, tk=1408)
    dq, dk, dv = _flash_bwd(S["q"], S["k"], S["v"], S["y_b"], dy_b, S["lse"], name=n("d_attention"), heads=dm.heads)
    dq_raw, dk_nope, dk_rope, dgq, dgk = _qk_bwd(
        dq, dk, S["q_raw"], S["k_nope"], S["rest"], cos_t, sin_t, W["q_norm"], W["k_norm"], name=n("d_qk_norm_rope"),
        heads=dm.heads, kr_seg=dm.r_kr // HEAD_PAD)
    ds["q_norm"] = _head_unpad(dgq, 1)
    ds["k_norm"] = _head_unpad(dgk, 1)
    dq_lat_n = _mm(dq_raw, W["wuq"], name=n("d_up_q"), tb=True, tk=2048)
    dw["wuq"] = _mm(S["q_lat"], dq_raw, name=n("dw_up_q"), ta=True, tm=512, tk=1408)
    dkv_v = _mm(dv, W["wv"], name=n("d_up_v"), tb=True)
    dkv_lat_n = _mm(dk_nope, W["wkn"], name=n("d_up_k"), tb=True, add=dkv_v)
    dw["wkn"] = _mm(S["kv_lat"], dk_nope, name=n("dw_up_k"), ta=True, tm=512, tk=1408)
    dw["wv"] = _mm(S["kv_lat"], dv, name=n("dw_up_v"), ta=True, tm=512, tk=1408)
    dq_lat, ds["q_lat_norm"] = _rms_bwd(dq_lat_n, S["rest"], W["q_lat_norm"], name=n("d_q_lat_norm"), width=Q_LORA,
                                        seg=dm.r_q // Q_LORA, out_dtype=BF16)
    dkv_lat, ds["kv_lat_norm"] = _rms_bwd(dkv_lat_n, S["rest"], W["kv_lat_norm"], name=n("d_kv_lat_norm"), width=KV_LORA,
                                          seg=dm.r_kv // KV_LORA, out_dtype=BF16)
    du, db, dc, ds["conv_w"] = _conv_bwd(S["rest"], W["conv_w"], dy_a, name=n("d_conv"), dc=dm.dc)
    dpool, dw["wp"], ds["pool_scale"] = _pool_bwd(S["rest"], W["wp"], W["pool_scale"], dy_c, name=n("d_pool"),
                                                  seg0=dm.r_pool // dm.pg, pg=dm.pg)
    drest = jnp.concatenate([du, db, dc, dpool, dq_lat, dkv_lat, dk_rope], axis=1)
    dh_g = _mm(dgl, W["wg"], name=n("d_proj_gates"), tb=True, tk=2048)
    dh = _mm(drest, W["wr"], name=n("d_proj_rest"), tb=True, add=dh_g, tk=1792)
    dw["wg"] = _mm(S["h"], dgl, name=n("dw_proj_gates"), ta=True, tm=1024, tk=1408)
    dw["wr"] = _mm(S["h"], drest, name=n("dw_proj_rest"), ta=True, tm=1024, tk=1408)
    dx, ds["attn_norm"] = _rms_bwd(dh, S["x"], W["attn_norm"], name=n("d_attn_norm"), res=dx1)
    return dx, dw, ds


BIG = ("w_in", "w_uq", "w_ukv", "pool_w", "w_branch_a", "w_branch_b", "w_branch_c", "w_o", "w_up", "w_down")
REPLICATED = ("attn_norm", "q_lat_norm", "kv_lat_norm", "q_norm", "k_norm", "pool_scale", "mlp_norm")
WEIGHTS = ("meta_tokens", "attn_norm", "w_in", "conv_w", "q_lat_norm", "kv_lat_norm", "w_uq", "w_ukv", "q_norm",
           "k_norm", "pool_w", "pool_scale", "w_branch_a", "w_branch_b", "w_branch_c", "w_o", "mlp_norm", "w_up",
           "w_down")


def _pack(arrays):
    flat = jnp.concatenate([a.reshape(-1).astype(F32) for a in arrays])
    pad = (-flat.shape[0]) % (8 * LANES)
    return jnp.pad(flat, (0, pad)).reshape(-1, LANES)


def _unpack(flat, shapes):
    out, pos = [], 0
    flat = flat.reshape(-1)
    for shp in shapes:
        size = math.prod(shp)
        out.append(flat[pos:pos + size].reshape(shp))
        pos += size
    return out


def _update(w, g, m, v, name):
    shp = w.shape
    to2 = lambda a: a.reshape(-1, shp[-1])
    delta, nm, nv = _adamw(to2(w), to2(g), to2(m), to2(v), name=name)
    return delta.reshape(shp), nm.reshape(shp), nv.reshape(shp)


def _step(args):
    x = args["x"][0]
    seq, d = x.shape
    dm = _Dims(d, seq)
    xi, yi, ci = _coords()
    chip = 2 * xi + yi

    small_w = _gather_all(_pack([args["conv_w"], args["meta_tokens"]]), name="gather_small_weights")
    order = [(k, l) for l in range(2) for k in BIG]
    shards = {n: args[n[0]][n[1]].astype(BF16) for n in order}
    small_w, shards[order[0]] = lax.optimization_barrier((small_w, shards[order[0]]))
    lands = {n: lax.empty((4,) + shards[n].shape, BF16) for n in order}
    group_names = [[("w_in", 0)], [(k, 0) for k in BIG[1:]], [(k, 1) for k in BIG]]
    at = {n: i for i, n in enumerate(order)}
    sems, thru, token = _start_copies(
        [shards[n] for n in order] + [lands[n] for n in order],
        [_ici_gather_plan([(at[n], len(order) + at[n]) for n in g]) for g in group_names], name="start_gather_ici")
    for i, n in enumerate(order):
        shards[n], lands[n] = thru[i], thru[len(order) + i]

    def finish_gather(g, after, tag):
        names = group_names[g]
        k = len(names)
        plan, _ = _ici_gather_plan([(i, k + i) for i in range(k)])
        got = _wait_copies([shards[n] for n in names] + [lands[n] for n in names], sems[g], plan, after,
                           name=f"wait_gather_ici_{tag}")
        for i, n in enumerate(names):
            shards[n] = got[i]
        fwd = _d2d_forward_plan(list(range(k)))
        sems2, bufs2, tok2 = _start_copies(got[k:], [fwd], name=f"start_gather_d2d_{tag}")
        return names, bufs2, sems2[0], fwd[0], tok2

    def land_gather(pending, after, tag):
        names, bufs2, sems2, plan, tok2 = pending
        done = _wait_copies(bufs2, sems2, plan, tok2 if after is None else after, name=f"wait_gather_d2d_{tag}")
        return {n[0]: buf for n, buf in zip(names, done)}

    conv_shape, meta_shape = args["conv_w"].shape, args["meta_tokens"].shape
    per_chip = [_unpack(small_w[2 * j], [conv_shape, meta_shape]) for j in range(4)]
    conv_full = jnp.concatenate([p[0] for p in per_chip], axis=-1)
    meta_full = jnp.concatenate([p[1] for p in per_chip], axis=-1)

    layers = []
    for l in range(2):
        small = {k: args[k][l] for k in REPLICATED}
        small["conv_w"] = conv_full[l]
        layers.append(_small_weights(small))
    layers[0].update(_in_weights(dm, land_gather(finish_gather(0, token, "l0_in"), None, "l0_in")["w_in"]))

    pos = jnp.arange(dm.t, dtype=F32)
    inv = ROPE_THETA ** (-jnp.arange(0, QK_ROPE, 2, dtype=F32) / QK_ROPE)
    ang = pos[:, None] * inv[None, :]
    cos_t = _rope_pad(jnp.cos(ang), jnp.cos(ang))
    sin_t = _rope_pad(-jnp.sin(ang), jnp.sin(ang))
    tail = jnp.zeros((dm.t - dm.t_real, d), F32)
    h0 = jnp.concatenate([meta_full, x, tail], axis=0)
    target = jnp.concatenate([jnp.zeros((N_META, d), F32), args["loss_target"][0], tail], axis=0)

    def rest_of_layer0(after):
        return _other_weights(dm, land_gather(finish_gather(1, after, "l0_rest"), None, "l0_rest"))

    h1, saved0 = _layer_fwd(dm, layers[0], h0, cos_t, sin_t, "l0", more=rest_of_layer0)
    g1 = land_gather(finish_gather(2, saved0["y_b"], "l1"), h1, "l1")
    layers[1].update(_in_weights(dm, g1["w_in"]))
    layers[1].update(_other_weights(dm, g1))
    h2, saved1 = _layer_fwd(dm, layers[1], h1, cos_t, sin_t, "l1")
    sq, dy = _loss(h2, target, name="loss_head", first=N_META, last=dm.t_real)
    loss = lax.psum(0.5 / d * sq[0, 0], ("x", "y", "c"))
    dh1, dw1, ds1 = _layer_bwd(dm, layers[1], saved1, dy, cos_t, sin_t, "l1")
    dh0, dw0, ds0 = _layer_bwd(dm, layers[0], saved0, dh1, cos_t, sin_t, "l0")
    grad_x = dh0[N_META:dm.t_real][None]

    core, chip_flags = _one_hot(ci, 2), _one_hot(chip, 4)
    nb = len(BIG)

    def exchange(bufs, plan, after, tag):
        sems_x, bufs_x, tok_x = _start_copies(bufs, [plan], name=f"start_{tag}")
        return _wait_copies(bufs_x, sems_x[0], plan[0], tok_x if after is None else after, name=f"wait_{tag}")

    def reduce_layer(dw, afters, tag):
        pcs = _layer_grad_pieces(dm, dw)
        parts = [_as3d(pcs[k]) for k in BIG]
        recv = [lax.empty((4, p.shape[1] // 2, p.shape[2]), BF16) for p in parts]
        got = exchange(parts + recv, _swap_half_plan([(i, nb + i) for i in range(nb)]), afters[0], f"swap_{tag}")
        pairs = [_pair_sum(got[i], got[nb + i], core, name=f"pair_sum_{k}_{tag}") for i, k in enumerate(BIG)]
        landed = [lax.empty(p.shape, BF16) for p in pairs]
        got = exchange(pairs + landed, _scatter_plan([(i, nb + i) for i in range(nb)]), afters[1], f"scatter_{tag}")
        totals = [_chip_sum(got[i], got[nb + i], chip_flags, name=f"chip_sum_{k}_{tag}") for i, k in enumerate(BIG)]
        other = [lax.empty(t.shape, F32) for t in totals]
        got = exchange(totals + other, _swap_total_plan([(i, nb + i) for i in range(nb)]), afters[2], f"swap_total_{tag}")
        return {k: (got[i], got[nb + i]) for i, k in enumerate(BIG)}

    red1 = reduce_layer(dw1, (dw0["wdn"], dw0["wuq"], dh0), "l1")
    red0 = reduce_layer(dw0, (None, None, None), "l0")
    grads = {}

    small_names = REPLICATED + ("conv_w",)
    small_parts = [jnp.stack([ds0[k].reshape(ds0[k].shape[-2:] if k == "conv_w" else (-1,)),
                              ds1[k].reshape(ds1[k].shape[-2:] if k == "conv_w" else (-1,))]) for k in small_names]
    small_parts.append(dh0[:N_META])
    small_all = _gather_all(_pack(small_parts), name="gather_small_grads")
    small_sum = _sum_stack(small_all, name="sum_small_grads", out_dtype=F32)
    small_g = dict(zip(small_names + ("meta_tokens",), _unpack(small_sum, [p.shape for p in small_parts])))
    for k in REPLICATED:
        grads[k] = small_g[k]
    dcw = conv_shape[-1]
    grads["conv_w"] = lax.dynamic_slice_in_dim(small_g["conv_w"], chip * dcw, dcw, axis=2)
    dmeta = meta_shape[-1]
    grads["meta_tokens"] = lax.dynamic_slice_in_dim(small_g["meta_tokens"], chip * dmeta, dmeta, axis=1)

    delta, new_m, new_v = {}, {}, {}
    for k in WEIGHTS:
        shp = args[k].shape
        if k in BIG:
            wmv = [args[p + k].reshape(2, -1, shp[-1]) for p in ("", "m_", "v_")]
            out = _adamw_layer(*wmv, *red1[k], core, 1, None, name=f"adamw_{k}_l1")
            out = _adamw_layer(*wmv, *red0[k], core, 0, out, name=f"adamw_{k}_l0")
            grads[k], delta[k], new_m[k], new_v[k] = (o.reshape(shp) for o in out)
        else:
            grads[k] = grads[k].reshape(shp)
            delta[k], new_m[k], new_v[k] = _update(args[k], grads[k], args["m_" + k], args["v_" + k], f"adamw_{k}")
    return (loss, grad_x, *[grads[k] for k in WEIGHTS], *[delta[k] for k in WEIGHTS],
            *[new_m[k] for k in WEIGHTS], *[new_v[k] for k in WEIGHTS])


def kernel(x, meta_tokens, attn_norm, w_in, conv_w, q_lat_norm, kv_lat_norm, w_uq, w_ukv, q_norm, k_norm, pool_w, pool_scale, w_branch_a, w_branch_b, w_branch_c, w_o, mlp_norm, w_up, w_down, loss_target, m_meta_tokens, m_attn_norm, m_w_in, m_conv_w, m_q_lat_norm, m_kv_lat_norm, m_w_uq, m_w_ukv, m_q_norm, m_k_norm, m_pool_w, m_pool_scale, m_w_branch_a, m_w_branch_b, m_w_branch_c, m_w_o, m_mlp_norm, m_w_up, m_w_down, v_meta_tokens, v_attn_norm, v_w_in, v_conv_w, v_q_lat_norm, v_kv_lat_norm, v_w_uq, v_w_ukv, v_q_norm, v_k_norm, v_pool_w, v_pool_scale, v_w_branch_a, v_w_branch_b, v_w_branch_c, v_w_o, v_mlp_norm, v_w_up, v_w_down):
    return _step(dict(locals()))
```

```python
import functools
import math

import jax
import jax.numpy as jnp
from jax import lax
from jax.experimental import pallas as pl
from jax.experimental.pallas import tpu as pltpu

F32 = jnp.float32
BF16 = jnp.bfloat16
MESH = pl.DeviceIdType.MESH

EPS = 1e-6
N_META = 16
QK_NOPE = 128
QK_ROPE = 64
QK_HEAD = QK_NOPE + QK_ROPE
V_HEAD = 128
HEAD_PAD = 256
Q_LORA = 512
KV_LORA = 512
ROPE_THETA = 10000.0
POOL_WINDOWS = (2, 4, 8, 16)
HALO = 16
LANES = 128
ADAM_LR = 0.001
ADAM_B1 = 0.9
ADAM_B2 = 0.999
ADAM_EPS = 1e-08
ADAM_WD = 0.01
ADAM_STEP = 10
VMEM_LIMIT = 52 * 1024 * 1024
NEG = -1e30


def _tile(n, target, mult=LANES):
    best = None
    for t in range(mult, min(n, target) + 1, mult):
        if n % t == 0:
            best = t
    return n if best is None else best


def _params(sem=None):
    return pltpu.CompilerParams(dimension_semantics=sem, vmem_limit_bytes=VMEM_LIMIT)


def _mm(a, b, *, name, ta=False, tb=False, add=None, aux=None, epi=None, out_dtype=F32,
        tm=704, tn=1024, tk=None, after=()):
    if ta:
        K, M = a.shape
    else:
        M, K = a.shape
    if tb:
        N, kb = b.shape
    else:
        kb, N = b.shape
    assert K == kb, (a.shape, b.shape, ta, tb)
    tm = _tile(M, tm, LANES if ta else 16)
    tn = _tile(N, tn, LANES)
    tk = K if tk is None else _tile(K, tk, LANES if (not ta or tb) else 16)
    nk = K // tk
    grid = (M // tm, N // tn, nk)

    a_spec = pl.BlockSpec((tk, tm), lambda i, j, k: (k, i)) if ta else pl.BlockSpec((tm, tk), lambda i, j, k: (i, k))
    b_spec = pl.BlockSpec((tn, tk), lambda i, j, k: (j, k)) if tb else pl.BlockSpec((tk, tn), lambda i, j, k: (k, j))
    o_spec = pl.BlockSpec((tm, tn), lambda i, j, k: (i, j))
    in_specs = [a_spec, b_spec]
    operands = [a, b]
    if add is not None:
        in_specs.append(o_spec)
        operands.append(add)
    if aux is not None:
        in_specs.append(o_spec)
        operands.append(aux)
    after = tuple(after)
    in_specs += [pl.BlockSpec(memory_space=pl.ANY)] * len(after)
    operands += list(after)
    if epi == "relu2":
        out_shape = (jax.ShapeDtypeStruct((M, N), BF16), jax.ShapeDtypeStruct((M, N), BF16))
        out_specs = (o_spec, o_spec)
    else:
        out_shape = jax.ShapeDtypeStruct((M, N), out_dtype)
        out_specs = o_spec
    dims = (((0 if ta else 1,), (1 if tb else 0,)), ((), ()))
    has_add, has_aux = add is not None, aux is not None

    def body(*refs):
        a_ref, b_ref = refs[0], refs[1]
        pos = 2
        add_ref = aux_ref = None
        if has_add:
            add_ref = refs[pos]
            pos += 1
        if has_aux:
            aux_ref = refs[pos]
            pos += 1
        pos += len(after)
        n_out = 2 if epi == "relu2" else 1
        out_refs = refs[pos:pos + n_out]
        acc_ref = refs[pos + n_out] if nk > 1 else None

        part = lax.dot_general(a_ref[...].astype(BF16), b_ref[...].astype(BF16), dims,
                               preferred_element_type=F32)

        def finish(acc):
            if has_add:
                acc = acc + add_ref[...].astype(F32)
            if epi == "relu2":
                r = jnp.maximum(acc, 0.0)
                out_refs[0][...] = acc.astype(BF16)
                out_refs[1][...] = (r * r).astype(BF16)
            elif epi == "drelu2":
                u = aux_ref[...].astype(F32)
                out_refs[0][...] = (acc * (2.0 * jnp.maximum(u, 0.0))).astype(out_dtype)
            else:
                out_refs[0][...] = acc.astype(out_dtype)

        if nk == 1:
            finish(part)
        else:
            k = pl.program_id(2)

            @pl.when(k == 0)
            def _():
                acc_ref[...] = part

            @pl.when(k > 0)
            def _():
                acc_ref[...] += part

            @pl.when(k == nk - 1)
            def _():
                finish(acc_ref[...])

    scratch = [pltpu.VMEM((tm, tn), F32)] if nk > 1 else []
    return pl.pallas_call(
        body, name=name, grid=grid, in_specs=in_specs, out_specs=out_specs, out_shape=out_shape,
        scratch_shapes=scratch, compiler_params=_params(("parallel", "parallel", "arbitrary")),
    )(*operands)


def _rms_fwd(x, g, *, name, width=None, seg=0, tm=384):
    T = x.shape[0]
    width = x.shape[1] if width is None else width
    tm = _tile(T, tm, 16)

    def body(x_ref, g_ref, o_ref):
        xf = x_ref[...].astype(F32)
        r = lax.rsqrt(jnp.mean(xf * xf, axis=-1, keepdims=True) + EPS)
        o_ref[...] = (xf * r * g_ref[...]).astype(BF16)

    return pl.pallas_call(
        body, name=name, grid=(T // tm,),
        in_specs=[pl.BlockSpec((tm, width), lambda i: (i, seg)), pl.BlockSpec((1, width), lambda i: (0, 0))],
        out_specs=pl.BlockSpec((tm, width), lambda i: (i, 0)),
        out_shape=jax.ShapeDtypeStruct((T, width), BF16),
        compiler_params=_params(("parallel",)),
    )(x, g)


def _rms_bwd(dy, x, g, *, name, width=None, seg=0, res=None, out_dtype=F32, tm=384):
    T = x.shape[0]
    width = x.shape[1] if width is None else width
    tm = _tile(T, tm, 16)
    has_res = res is not None

    def body(*refs):
        dy_ref, x_ref, g_ref = refs[:3]
        res_ref = refs[3] if has_res else None
        dx_ref, dg_ref = refs[-2:]
        xf = x_ref[...].astype(F32)
        dyf = dy_ref[...].astype(F32)
        r = lax.rsqrt(jnp.mean(xf * xf, axis=-1, keepdims=True) + EPS)
        xhat = xf * r
        dyh = dyf * g_ref[...]
        dx = r * (dyh - xhat * jnp.mean(dyh * xhat, axis=-1, keepdims=True))
        if has_res:
            dx = dx + res_ref[...].astype(F32)
        dx_ref[...] = dx.astype(out_dtype)
        part = jnp.sum(dyf * xhat, axis=0, keepdims=True)

        @pl.when(pl.program_id(0) == 0)
        def _():
            dg_ref[...] = part

        @pl.when(pl.program_id(0) > 0)
        def _():
            dg_ref[...] += part

    row = pl.BlockSpec((tm, width), lambda i: (i, 0))
    in_specs = [row, pl.BlockSpec((tm, width), lambda i: (i, seg)), pl.BlockSpec((1, width), lambda i: (0, 0))]
    operands = [dy, x, g]
    if has_res:
        in_specs.append(row)
        operands.append(res)
    return pl.pallas_call(
        body, name=name, grid=(T // tm,), in_specs=in_specs,
        out_specs=(row, pl.BlockSpec((1, width), lambda i: (0, 0))),
        out_shape=(jax.ShapeDtypeStruct((T, width), out_dtype), jax.ShapeDtypeStruct((1, width), F32)),
        compiler_params=_params(("arbitrary",)),
    )(*operands)


def _down(ext, k):
    return pltpu.roll(ext, k, 0)


def _up(ext, k):
    return pltpu.roll(ext, ext.shape[0] - k, 0)


def _pre_halo(ref, r, R):
    start = pl.multiple_of(jnp.maximum(r * R - HALO, 0), 8)
    keep = (r > 0).astype(F32)
    return ref[pl.ds(start, HALO), :].astype(F32) * keep


def _post_halo(ref, r, R, n_chunks):
    start = pl.multiple_of(jnp.minimum(r * R + R, (n_chunks - 1) * R + R - HALO), 8)
    keep = (r < n_chunks - 1).astype(F32)
    return ref[pl.ds(start, HALO), :].astype(F32) * keep


def _chunk(ref, r, R):
    return ref[pl.ds(pl.multiple_of(r * R, 8), R), :].astype(F32)


def _conv_fwd(rest, conv_w, *, name, dc, tc=128, rows=1056):
    T = rest.shape[0]
    tc = _tile(dc, tc)
    nb = dc // tc
    R = _tile(T, rows, 16)
    n_chunks = T // R

    def body(u_ref, b_ref, c_ref, w_ref, y_ref):
        w0, w1, w2 = w_ref[0:1, :], w_ref[1:2, :], w_ref[2:3, :]

        def chunk(r, carry):
            cu = _chunk(c_ref, r, R) * _chunk(u_ref, r, R)
            ext = jnp.concatenate([_pre_halo(c_ref, r, R) * _pre_halo(u_ref, r, R), cu], axis=0)
            conv = w0 * _down(ext, 2)[HALO:] + w1 * _down(ext, 1)[HALO:] + w2 * cu
            y_ref[pl.ds(pl.multiple_of(r * R, 8), R), :] = (_chunk(b_ref, r, R) * conv).astype(BF16)
            return carry

        lax.fori_loop(0, n_chunks, chunk, 0)

    col = lambda off: pl.BlockSpec((T, tc), lambda j: (0, off * nb + j))
    return pl.pallas_call(
        body, name=name, grid=(nb,),
        in_specs=[col(0), col(1), col(2), pl.BlockSpec((3, tc), lambda j: (0, j))],
        out_specs=pl.BlockSpec((T, tc), lambda j: (0, j)),
        out_shape=jax.ShapeDtypeStruct((T, dc), BF16),
        compiler_params=_params(("parallel",)),
    )(rest, rest, rest, conv_w)


def _conv_bwd(rest, conv_w, dy, *, name, dc, tc=128, rows=1056):
    T = rest.shape[0]
    tc = _tile(dc, tc)
    nb = dc // tc
    R = _tile(T, rows, 16)
    n_chunks = T // R

    def body(u_ref, b_ref, c_ref, w_ref, dy_ref, du_ref, db_ref, dc_ref, dw_ref):
        w0, w1, w2 = w_ref[0:1, :], w_ref[1:2, :], w_ref[2:3, :]

        def chunk(r, carry):
            a0, a1, a2 = carry
            u, b, c = _chunk(u_ref, r, R), _chunk(b_ref, r, R), _chunk(c_ref, r, R)
            dy_c = _chunk(dy_ref, r, R)
            cu = c * u
            ext = jnp.concatenate([_pre_halo(c_ref, r, R) * _pre_halo(u_ref, r, R), cu], axis=0)
            cu1, cu2 = _down(ext, 1)[HALO:], _down(ext, 2)[HALO:]
            conv = w0 * cu2 + w1 * cu1 + w2 * cu
            dconv = dy_c * b
            dext = jnp.concatenate(
                [dconv, _post_halo(dy_ref, r, R, n_chunks) * _post_halo(b_ref, r, R, n_chunks)], axis=0)
            dcu = w2 * dconv + w1 * _up(dext, 1)[:R] + w0 * _up(dext, 2)[:R]
            rows_at = pl.ds(pl.multiple_of(r * R, 8), R)
            db_ref[rows_at, :] = (dy_c * conv).astype(BF16)
            du_ref[rows_at, :] = (dcu * c).astype(BF16)
            dc_ref[rows_at, :] = (dcu * u).astype(BF16)
            return (a0 + jnp.sum(dconv * cu2, axis=0, keepdims=True),
                    a1 + jnp.sum(dconv * cu1, axis=0, keepdims=True),
                    a2 + jnp.sum(dconv * cu, axis=0, keepdims=True))

        zero = jnp.zeros((1, tc), F32)
        a0, a1, a2 = lax.fori_loop(0, n_chunks, chunk, (zero, zero, zero))
        dw_ref[0:1, :] = a0
        dw_ref[1:2, :] = a1
        dw_ref[2:3, :] = a2

    col = lambda off: pl.BlockSpec((T, tc), lambda j: (0, off * nb + j))
    own = pl.BlockSpec((T, tc), lambda j: (0, j))
    return pl.pallas_call(
        body, name=name, grid=(nb,),
        in_specs=[col(0), col(1), col(2), pl.BlockSpec((3, tc), lambda j: (0, j)), own],
        out_specs=(own, own, own, pl.BlockSpec((3, tc), lambda j: (0, j))),
        out_shape=(jax.ShapeDtypeStruct((T, dc), BF16),) * 3 + (jax.ShapeDtypeStruct((3, dc), F32),),
        compiler_params=_params(("parallel",)),
    )(rest, rest, rest, conv_w, dy)


def _window_count(r, R, n_rows, w, first_row_offset):
    t = lax.broadcasted_iota(jnp.int32, (n_rows, 1), 0) + (r * R + first_row_offset)
    return jnp.minimum(t + 1, w).astype(F32)


def _pool_fwd(rest, pool_w, pool_scale, *, name, seg0, pg, rows=1056):
    T = rest.shape[0]
    R = _tile(T, rows, 16)
    n_chunks = T // R
    n_groups = len(POOL_WINDOWS)

    def body(x_ref, w_ref, s_ref, y_ref):
        def run(window):
            def chunk(r, carry):
                g = _chunk(x_ref, r, R)
                s = jnp.concatenate([_pre_halo(x_ref, r, R), g], axis=0)
                k = 1
                while k < window:
                    s = s + _down(s, k)
                    k *= 2
                pooled = s[HALO:] / _window_count(r, R, R, window, 0) - g
                mixed = jnp.dot(pooled.astype(BF16), w_ref[0], preferred_element_type=F32)
                y_ref[pl.ds(pl.multiple_of(r * R, 8), R), :] = (mixed * s_ref[...]).astype(BF16)
                return carry

            lax.fori_loop(0, n_chunks, chunk, 0)

        for gi, window in enumerate(POOL_WINDOWS):
            pl.when(pl.program_id(0) == gi)(functools.partial(run, window))

    return pl.pallas_call(
        body, name=name, grid=(n_groups,),
        in_specs=[pl.BlockSpec((T, pg), lambda g: (0, seg0 + g)),
                  pl.BlockSpec((1, pg, pg), lambda g: (g, 0, 0)),
                  pl.BlockSpec((1, pg), lambda g: (0, g))],
        out_specs=pl.BlockSpec((T, pg), lambda g: (0, g)),
        out_shape=jax.ShapeDtypeStruct((T, n_groups * pg), BF16),
        compiler_params=_params(("parallel",)),
    )(rest, pool_w, pool_scale)


def _pool_bwd(rest, pool_w, pool_scale, dy, *, name, seg0, pg, rows=1056):
    T = rest.shape[0]
    R = _tile(T, rows, 16)
    n_chunks = T // R
    n_groups = len(POOL_WINDOWS)

    def body(x_ref, w_ref, s_ref, dy_ref, dx_ref, dw_ref, ds_ref):
        def run(window):
            def chunk(r, carry):
                dw_acc, ds_acc = carry
                g = _chunk(x_ref, r, R)
                s = jnp.concatenate([_pre_halo(x_ref, r, R), g], axis=0)
                k = 1
                while k < window:
                    s = s + _down(s, k)
                    k *= 2
                pooled = (s[HALO:] / _window_count(r, R, R, window, 0) - g).astype(BF16)
                mixed = jnp.dot(pooled, w_ref[0], preferred_element_type=F32)
                dy_c = _chunk(dy_ref, r, R)
                dm_ext = (jnp.concatenate([dy_c, _post_halo(dy_ref, r, R, n_chunks)], axis=0)
                          * s_ref[...]).astype(BF16)
                dpool_ext = lax.dot_general(dm_ext, w_ref[0], (((1,), (1,)), ((), ())),
                                            preferred_element_type=F32)
                a = dpool_ext / _window_count(r, R, R + HALO, window, 0)
                k = 1
                while k < window:
                    a = a + _up(a, k)
                    k *= 2
                dx_ref[pl.ds(pl.multiple_of(r * R, 8), R), :] = (a[:R] - dpool_ext[:R]).astype(BF16)
                dw_acc = dw_acc + lax.dot_general(pooled, dm_ext[:R], (((0,), (0,)), ((), ())),
                                                  preferred_element_type=F32)
                ds_acc = ds_acc + jnp.sum(dy_c * mixed, axis=0, keepdims=True)
                return dw_acc, ds_acc

            dw_acc, ds_acc = lax.fori_loop(0, n_chunks, chunk,
                                           (jnp.zeros((pg, pg), F32), jnp.zeros((1, pg), F32)))
            dw_ref[0] = dw_acc
            ds_ref[...] = ds_acc

        for gi, window in enumerate(POOL_WINDOWS):
            pl.when(pl.program_id(0) == gi)(functools.partial(run, window))

    own = pl.BlockSpec((T, pg), lambda g: (0, g))
    return pl.pallas_call(
        body, name=name, grid=(n_groups,),
        in_specs=[pl.BlockSpec((T, pg), lambda g: (0, seg0 + g)),
                  pl.BlockSpec((1, pg, pg), lambda g: (g, 0, 0)),
                  pl.BlockSpec((1, pg), lambda g: (0, g)), own],
        out_specs=(own, pl.BlockSpec((1, pg, pg), lambda g: (g, 0, 0)), pl.BlockSpec((1, pg), lambda g: (0, g))),
        out_shape=(jax.ShapeDtypeStruct((T, n_groups * pg), BF16),
                   jax.ShapeDtypeStruct((n_groups, pg, pg), F32),
                   jax.ShapeDtypeStruct((1, n_groups * pg), F32)),
        compiler_params=_params(("parallel",)),
    )(rest, pool_w, pool_scale, dy)


def _rope(r, cos_t, sin_t):
    return r * cos_t + pltpu.roll(r, LANES // 2, 1) * sin_t


def _rope_t(d, cos_t, sin_t):
    return d * cos_t + pltpu.roll(d * sin_t, LANES // 2, 1)


def _qk_fwd(q_raw, k_nope, rest, cos_t, sin_t, q_norm, k_norm, *, name, heads, kr_seg, tm=192):
    T = q_raw.shape[0]
    tm = _tile(T, tm, 16)

    def body(q_ref, kn_ref, kr_ref, c_ref, s_ref, gq_ref, gk_ref, qo_ref, ko_ref):
        cos_b, sin_b = c_ref[...], s_ref[...]
        kr = kr_ref[:, 0:LANES]
        kr_ss = jnp.sum(kr * kr, axis=-1, keepdims=True)
        gq, gk = gq_ref[...], gk_ref[...]
        for h in range(heads):
            lo = h * HEAD_PAD
            q = q_ref[:, lo:lo + HEAD_PAD]
            rq = lax.rsqrt(jnp.sum(q * q, axis=-1, keepdims=True) / QK_HEAD + EPS)
            qn = q * rq * gq
            qo_ref[:, lo:lo + LANES] = qn[:, :LANES].astype(BF16)
            qo_ref[:, lo + LANES:lo + HEAD_PAD] = _rope(qn[:, LANES:], cos_b, sin_b).astype(BF16)
            kn = kn_ref[:, h * LANES:(h + 1) * LANES]
            rk = lax.rsqrt((jnp.sum(kn * kn, axis=-1, keepdims=True) + kr_ss) / QK_HEAD + EPS)
            ko_ref[:, lo:lo + LANES] = (kn * rk * gk[:, :LANES]).astype(BF16)
            ko_ref[:, lo + LANES:lo + HEAD_PAD] = _rope(kr * rk * gk[:, LANES:], cos_b, sin_b).astype(BF16)

    wq, wk = heads * HEAD_PAD, heads * LANES
    return pl.pallas_call(
        body, name=name, grid=(T // tm,),
        in_specs=[pl.BlockSpec((tm, wq), lambda i: (i, 0)), pl.BlockSpec((tm, wk), lambda i: (i, 0)),
                  pl.BlockSpec((tm, HEAD_PAD), lambda i: (i, kr_seg)),
                  pl.BlockSpec((tm, LANES), lambda i: (i, 0)), pl.BlockSpec((tm, LANES), lambda i: (i, 0)),
                  pl.BlockSpec((1, HEAD_PAD), lambda i: (0, 0)), pl.BlockSpec((1, HEAD_PAD), lambda i: (0, 0))],
        out_specs=(pl.BlockSpec((tm, wq), lambda i: (i, 0)), pl.BlockSpec((tm, wq), lambda i: (i, 0))),
        out_shape=(jax.ShapeDtypeStruct((T, wq), BF16), jax.ShapeDtypeStruct((T, wq), BF16)),
        compiler_params=_params(("parallel",)),
    )(q_raw, k_nope, rest, cos_t, sin_t, q_norm, k_norm)


def _qk_bwd(dq, dk, q_raw, k_nope, rest, cos_t, sin_t, q_norm, k_norm, *, name, heads, kr_seg, tm=128):
    T = q_raw.shape[0]
    tm = _tile(T, tm, 16)

    def body(dq_ref, dk_ref, q_ref, kn_ref, kr_ref, c_ref, s_ref, gq_ref, gk_ref,
             dqr_ref, dkn_ref, dkr_ref, dgq_ref, dgk_ref):
        cos_b, sin_b = c_ref[...], s_ref[...]
        kr = kr_ref[:, 0:LANES]
        kr_ss = jnp.sum(kr * kr, axis=-1, keepdims=True)
        gq, gk = gq_ref[...], gk_ref[...]
        dgq = jnp.zeros((1, HEAD_PAD), F32)
        dgk_n = jnp.zeros((1, LANES), F32)
        dgk_r = jnp.zeros((1, LANES), F32)
        dkr = jnp.zeros((tm, LANES), F32)
        for h in range(heads):
            lo = h * HEAD_PAD
            q = q_ref[:, lo:lo + HEAD_PAD]
            rq = lax.rsqrt(jnp.sum(q * q, axis=-1, keepdims=True) / QK_HEAD + EPS)
            qhat = q * rq
            dqn = jnp.concatenate([dq_ref[:, lo:lo + LANES],
                                   _rope_t(dq_ref[:, lo + LANES:lo + HEAD_PAD], cos_b, sin_b)], axis=1)
            dgq = dgq + jnp.sum(dqn * qhat, axis=0, keepdims=True)
            dqh = dqn * gq
            dqr_ref[:, lo:lo + HEAD_PAD] = (
                rq * (dqh - qhat * (jnp.sum(dqh * qhat, axis=-1, keepdims=True) / QK_HEAD))).astype(BF16)
            kn = kn_ref[:, h * LANES:(h + 1) * LANES]
            rk = lax.rsqrt((jnp.sum(kn * kn, axis=-1, keepdims=True) + kr_ss) / QK_HEAD + EPS)
            khat_n, khat_r = kn * rk, kr * rk
            dkn_n = dk_ref[:, lo:lo + LANES]
            dkn_r = _rope_t(dk_ref[:, lo + LANES:lo + HEAD_PAD], cos_b, sin_b)
            dgk_n = dgk_n + jnp.sum(dkn_n * khat_n, axis=0, keepdims=True)
            dgk_r = dgk_r + jnp.sum(dkn_r * khat_r, axis=0, keepdims=True)
            dkh_n, dkh_r = dkn_n * gk[:, :LANES], dkn_r * gk[:, LANES:]
            proj = (jnp.sum(dkh_n * khat_n, axis=-1, keepdims=True)
                    + jnp.sum(dkh_r * khat_r, axis=-1, keepdims=True)) / QK_HEAD
            dkn_ref[:, h * LANES:(h + 1) * LANES] = (rk * (dkh_n - khat_n * proj)).astype(BF16)
            dkr = dkr + rk * (dkh_r - khat_r * proj)
        dkr_ref[:, 0:LANES] = dkr.astype(BF16)
        dkr_ref[:, LANES:HEAD_PAD] = jnp.zeros((tm, HEAD_PAD - LANES), BF16)
        dgk = jnp.concatenate([dgk_n, dgk_r], axis=1)

        @pl.when(pl.program_id(0) == 0)
        def _():
            dgq_ref[...] = dgq
            dgk_ref[...] = dgk

        @pl.when(pl.program_id(0) > 0)
        def _():
            dgq_ref[...] += dgq
            dgk_ref[...] += dgk

    wq, wk = heads * HEAD_PAD, heads * LANES
    row = lambda w: pl.BlockSpec((tm, w), lambda i: (i, 0))
    vec = pl.BlockSpec((1, HEAD_PAD), lambda i: (0, 0))
    return pl.pallas_call(
        body, name=name, grid=(T // tm,),
        in_specs=[row(wq), row(wq), row(wq), row(wk), pl.BlockSpec((tm, HEAD_PAD), lambda i: (i, kr_seg)),
                  row(LANES), row(LANES), vec, vec],
        out_specs=(row(wq), row(wk), row(HEAD_PAD), vec, vec),
        out_shape=(jax.ShapeDtypeStruct((T, wq), BF16), jax.ShapeDtypeStruct((T, wk), BF16),
                   jax.ShapeDtypeStruct((T, HEAD_PAD), BF16),
                   jax.ShapeDtypeStruct((1, HEAD_PAD), F32), jax.ShapeDtypeStruct((1, HEAD_PAD), F32)),
        compiler_params=_params(("arbitrary",)),
    )(dq, dk, q_raw, k_nope, rest, cos_t, sin_t, q_norm, k_norm)


def _causal_mask(s):
    row = lax.broadcasted_iota(jnp.int32, s.shape, 0)
    col = lax.broadcasted_iota(jnp.int32, s.shape, 1)
    return jnp.where(row >= col, s, NEG)


def _flash_fwd(q, k, v, *, name, heads, tq=384, hp=2):
    T = q.shape[0]
    tq = _tile(T, tq, LANES)
    nq = T // tq
    scale = QK_HEAD ** -0.5
    nt = (((1,), (1,)), ((), ()))

    def body(q_ref, k_ref, v_ref, o_ref, lse_ref):
        def q_block(i, carry):
            q_at = pl.ds(pl.multiple_of(i * tq, tq), tq)
            qbs = [q_ref[q_at, h * HEAD_PAD:(h + 1) * HEAD_PAD] for h in range(hp)]

            def step(j, state, masked):
                k_at = pl.ds(pl.multiple_of(j * tq, tq), tq)
                new = []
                for h in range(hp):
                    m, l, acc = state[h]
                    s = lax.dot_general(qbs[h], k_ref[k_at, h * HEAD_PAD:(h + 1) * HEAD_PAD], nt,
                                        preferred_element_type=F32) * scale
                    if masked:
                        s = _causal_mask(s)
                    m_new = jnp.maximum(m, jnp.max(s, axis=-1, keepdims=True))
                    p = jnp.exp(s - m_new)
                    alpha = jnp.exp(m - m_new)
                    l = alpha * l + jnp.sum(p, axis=-1, keepdims=True)
                    acc = alpha * acc + jnp.dot(p.astype(BF16), v_ref[k_at, h * V_HEAD:(h + 1) * V_HEAD],
                                                preferred_element_type=F32)
                    new.append((m_new, l, acc))
                return tuple(new)

            init = tuple((jnp.full((tq, 1), NEG, F32), jnp.zeros((tq, 1), F32), jnp.zeros((tq, V_HEAD), F32))
                         for _ in range(hp))
            state = lax.fori_loop(0, i, lambda j, st: step(j, st, False), init)
            state = step(i, state, True)
            for h in range(hp):
                m, l, acc = state[h]
                o_ref[q_at, h * V_HEAD:(h + 1) * V_HEAD] = (acc / l).astype(BF16)
                lse_ref[h, q_at, :] = jnp.broadcast_to(m + jnp.log(l), (tq, LANES))
            return carry

        lax.fori_loop(0, nq, q_block, 0)

    qk_spec = pl.BlockSpec((T, hp * HEAD_PAD), lambda g: (0, g))
    v_spec = pl.BlockSpec((T, hp * V_HEAD), lambda g: (0, g))
    return pl.pallas_call(
        body, name=name, grid=(heads // hp,), in_specs=[qk_spec, qk_spec, v_spec],
        out_specs=(v_spec, pl.BlockSpec((hp, T, LANES), lambda g: (g, 0, 0))),
        out_shape=(jax.ShapeDtypeStruct((T, heads * V_HEAD), BF16), jax.ShapeDtypeStruct((heads, T, LANES), F32)),
        compiler_params=_params(("parallel",)),
    )(q, k, v)


def _flash_bwd(q, k, v, o, do, lse, *, name, heads, tq=384):
    T = q.shape[0]
    tq = _tile(T, tq, LANES)
    nq = T // tq
    scale = QK_HEAD ** -0.5
    nt = (((1,), (1,)), ((), ()))
    tn = (((0,), (0,)), ((), ()))

    def body(q_ref, k_ref, v_ref, o_ref, do_ref, lse_ref, dq_ref, dk_ref, dv_ref, delta_ref):
        def fill_delta(i, carry):
            at = pl.ds(pl.multiple_of(i * tq, tq), tq)
            d = jnp.sum(o_ref[at, :].astype(F32) * do_ref[at, :].astype(F32), axis=-1, keepdims=True)
            delta_ref[at, :] = jnp.broadcast_to(d, (tq, LANES))
            dq_ref[at, :] = jnp.zeros((tq, HEAD_PAD), F32)
            return carry

        lax.fori_loop(0, nq, fill_delta, 0)

        def kv_block(j, carry):
            k_at = pl.ds(pl.multiple_of(j * tq, tq), tq)
            kb, vb = k_ref[k_at, :], v_ref[k_at, :]

            def step(i, state, masked):
                dk_acc, dv_acc = state
                q_at = pl.ds(pl.multiple_of(i * tq, tq), tq)
                qb, dob = q_ref[q_at, :], do_ref[q_at, :]
                s = lax.dot_general(qb, kb, nt, preferred_element_type=F32) * scale
                if masked:
                    s = _causal_mask(s)
                p = jnp.exp(s - lse_ref[0, q_at, :][:, 0:1])
                dv_acc = dv_acc + lax.dot_general(p.astype(BF16), dob, tn, preferred_element_type=F32)
                dp = lax.dot_general(dob, vb, nt, preferred_element_type=F32)
                ds = (p * (dp - delta_ref[q_at, :][:, 0:1]) * scale).astype(BF16)
                dk_acc = dk_acc + lax.dot_general(ds, qb, tn, preferred_element_type=F32)
                dq_ref[q_at, :] += jnp.dot(ds, kb, preferred_element_type=F32)
                return dk_acc, dv_acc

            state = step(j, (jnp.zeros((tq, HEAD_PAD), F32), jnp.zeros((tq, V_HEAD), F32)), True)
            rest = nq - 1 - j

            def two_steps(t, st):
                i0 = j + 1 + 2 * t
                return step(i0 + 1, step(i0, st, False), False)

            state = lax.fori_loop(0, rest // 2, two_steps, state)
            dk_acc, dv_acc = lax.cond(rest % 2 == 1, lambda st: step(nq - 1, st, False), lambda st: st, state)
            dk_ref[k_at, :] = dk_acc
            dv_ref[k_at, :] = dv_acc.astype(BF16)
            return carry

        lax.fori_loop(0, nq, kv_block, 0)

    qk_spec = pl.BlockSpec((T, HEAD_PAD), lambda h: (0, h))
    v_spec = pl.BlockSpec((T, V_HEAD), lambda h: (0, h))
    return pl.pallas_call(
        body, name=name, grid=(heads,),
        in_specs=[qk_spec, qk_spec, v_spec, v_spec, v_spec, pl.BlockSpec((1, T, LANES), lambda h: (h, 0, 0))],
        out_specs=(qk_spec, qk_spec, v_spec),
        out_shape=(jax.ShapeDtypeStruct((T, heads * HEAD_PAD), F32), jax.ShapeDtypeStruct((T, heads * HEAD_PAD), F32),
                   jax.ShapeDtypeStruct((T, heads * V_HEAD), BF16)),
        scratch_shapes=[pltpu.VMEM((T, LANES), F32)],
        compiler_params=_params(("parallel",)),
    )(q, k, v, o, do, lse)


def _merge_fwd(gl, pa, pb, pc, *, name, d, tm=384, tn=1024):
    T = pa.shape[0]
    tm, tn = _tile(T, tm, 16), _tile(d, tn)
    nb = d // tn

    def body(g0, g1, g2, a, b, c, o_ref):
        o_ref[...] = (jax.nn.sigmoid(g0[...]) * a[...] + jax.nn.sigmoid(g1[...]) * b[...]
                      + jax.nn.sigmoid(g2[...]) * c[...]).astype(BF16)

    gate = lambda n: pl.BlockSpec((tm, tn), lambda i, j: (i, n * nb + j))
    blk = pl.BlockSpec((tm, tn), lambda i, j: (i, j))
    return pl.pallas_call(
        body, name=name, grid=(T // tm, nb), in_specs=[gate(0), gate(1), gate(2), blk, blk, blk],
        out_specs=blk, out_shape=jax.ShapeDtypeStruct((T, d), BF16),
        compiler_params=_params(("parallel", "parallel")),
    )(gl, gl, gl, pa, pb, pc)


def _merge_bwd(dm, gl, pa, pb, pc, *, name, d, tm=384, tn=1024):
    T = pa.shape[0]
    tm, tn = _tile(T, tm, 16), _tile(d, tn)
    nb = d // tn

    def body(dm_ref, g0, g1, g2, a, b, c, da, db, dc, dg0, dg1, dg2):
        dmv = dm_ref[...]
        for g_ref, p_ref, dp_ref, dg_ref in ((g0, a, da, dg0), (g1, b, db, dg1), (g2, c, dc, dg2)):
            sg = jax.nn.sigmoid(g_ref[...])
            dp_ref[...] = (dmv * sg).astype(BF16)
            dg_ref[...] = (dmv * p_ref[...] * sg * (1.0 - sg)).astype(BF16)

    gate = lambda n: pl.BlockSpec((tm, tn), lambda i, j: (i, n * nb + j))
    blk = pl.BlockSpec((tm, tn), lambda i, j: (i, j))
    return pl.pallas_call(
        body, name=name, grid=(T // tm, nb), in_specs=[blk, gate(0), gate(1), gate(2), blk, blk, blk],
        out_specs=(blk,) * 6, out_shape=(jax.ShapeDtypeStruct((T, d), BF16),) * 6,
        compiler_params=_params(("parallel", "parallel")),
    )(dm, gl, gl, gl, pa, pb, pc)


def _loss(y, target, *, name, first, last, tm=384):
    T, d = y.shape
    tm = _tile(T, tm, 16)

    def body(y_ref, t_ref, loss_ref, dy_ref):
        i = pl.program_id(0)
        row = lax.broadcasted_iota(jnp.int32, (tm, 1), 0) + i * tm
        real = jnp.logical_and(row >= first, row < last)
        err = jnp.where(real, y_ref[...] - t_ref[...], 0.0)
        dy_ref[...] = err * (1.0 / d)
        part = jnp.broadcast_to(jnp.sum(err * err, keepdims=True).reshape(1, 1), (1, LANES))

        @pl.when(i == 0)
        def _():
            loss_ref[...] = part

        @pl.when(i > 0)
        def _():
            loss_ref[...] += part

    blk = pl.BlockSpec((tm, d), lambda i: (i, 0))
    return pl.pallas_call(
        body, name=name, grid=(T // tm,), in_specs=[blk, blk],
        out_specs=(pl.BlockSpec((1, LANES), lambda i: (0, 0)), blk),
        out_shape=(jax.ShapeDtypeStruct((1, LANES), F32), jax.ShapeDtypeStruct((T, d), F32)),
        compiler_params=_params(("arbitrary",)),
    )(y, target)


def _as3d(a):
    return a.reshape(a.shape[0], -1, a.shape[-1])


def _sum_stack(parts, *, name, out_dtype, rows=256):
    n, R, C = parts.shape
    tr = _tile(R, rows, 16)

    def body(p_ref, o_ref):
        acc = p_ref[0].astype(F32)
        for s in range(1, n):
            acc = acc + p_ref[s].astype(F32)
        o_ref[...] = acc.astype(out_dtype)

    return pl.pallas_call(
        body, name=name, grid=(R // tr,),
        in_specs=[pl.BlockSpec((n, tr, C), lambda i: (0, i, 0))],
        out_specs=pl.BlockSpec((tr, C), lambda i: (i, 0)),
        out_shape=jax.ShapeDtypeStruct((R, C), out_dtype),
        compiler_params=_params(("parallel",)),
    )(parts)


def _adamw(w, g, m, v, *, name, rows=128):
    R, C = w.shape
    tr = _tile(R, rows, 8)
    c1 = 1.0 - ADAM_B1 ** ADAM_STEP
    c2 = 1.0 - ADAM_B2 ** ADAM_STEP

    def body(w_ref, g_ref, m_ref, v_ref, d_ref, nm_ref, nv_ref):
        gv = g_ref[...]
        nm = ADAM_B1 * m_ref[...] + (1.0 - ADAM_B1) * gv
        nv = ADAM_B2 * v_ref[...] + (1.0 - ADAM_B2) * (gv * gv)
        nm_ref[...] = nm
        nv_ref[...] = nv
        d_ref[...] = -ADAM_LR * ((nm / c1) / (jnp.sqrt(nv / c2) + ADAM_EPS) + ADAM_WD * w_ref[...])

    blk = pl.BlockSpec((tr, C), lambda i: (i, 0))
    return pl.pallas_call(
        body, name=name, grid=(R // tr,), in_specs=[blk] * 4, out_specs=(blk,) * 3,
        out_shape=(jax.ShapeDtypeStruct((R, C), F32),) * 3,
        compiler_params=_params(("parallel",)),
    )(w, g, m, v)


def _one_hot(index, n):
    return jnp.broadcast_to((jnp.arange(n) == index).astype(F32)[:, None, None], (n, 8, LANES))


def _is_set(flags_ref, s):
    return flags_ref[s, 0:1, 0:1] > 0.5


def _pair_sum(pieces, recv, core, *, name, rows=256):
    _, H, C = recv.shape
    tr = _tile(H, rows, 16)
    nh = H // tr

    def body(lo_ref, hi_ref, r_ref, core_ref, o_ref):
        mine = jnp.where(_is_set(core_ref, 0), lo_ref[0], hi_ref[0])
        o_ref[0] = (mine.astype(F32) + r_ref[0].astype(F32)).astype(BF16)

    blk = pl.BlockSpec((1, tr, C), lambda j, i: (j, i, 0))
    return pl.pallas_call(
        body, name=name, grid=(4, nh),
        in_specs=[blk, pl.BlockSpec((1, tr, C), lambda j, i: (j, nh + i, 0)), blk,
                  pl.BlockSpec((2, 8, LANES), lambda j, i: (0, 0, 0))],
        out_specs=blk, out_shape=jax.ShapeDtypeStruct((4, H, C), BF16),
        compiler_params=_params(("parallel", "parallel")),
    )(pieces, pieces, recv, core)


def _chip_sum(pair, landed, chip_flags, *, name, rows=256):
    _, H, C = pair.shape
    tr = _tile(H, rows, 16)

    def body(p_ref, l_ref, chip_ref, o_ref):
        acc = None
        for s in range(4):
            part = jnp.where(_is_set(chip_ref, s), p_ref[s], l_ref[s]).astype(F32)
            acc = part if acc is None else acc + part
        o_ref[...] = acc

    blk = pl.BlockSpec((4, tr, C), lambda i: (0, i, 0))
    return pl.pallas_call(
        body, name=name, grid=(H // tr,),
        in_specs=[blk, blk, pl.BlockSpec((4, 8, LANES), lambda i: (0, 0, 0))],
        out_specs=pl.BlockSpec((tr, C), lambda i: (i, 0)), out_shape=jax.ShapeDtypeStruct((H, C), F32),
        compiler_params=_params(("parallel",)),
    )(pair, landed, chip_flags)


def _adamw_layer(w, m, v, total, recv, core, layer, prev, *, name, rows=128):
    _, R, C = w.shape
    H = R // 2
    tr = _tile(H, rows, 8)
    nh = H // tr
    c1 = 1.0 - ADAM_B1 ** ADAM_STEP
    c2 = 1.0 - ADAM_B2 ** ADAM_STEP
    n_prev = 0 if prev is None else 4

    def body(*refs):
        w_ref, m_ref, v_ref, t_ref, r_ref, core_ref = refs[:6]
        g_ref, d_ref, nm_ref, nv_ref = refs[6 + n_prev:]
        half_is_mine = jnp.where(pl.program_id(0) == 0, core_ref[0, 0:1, 0:1], core_ref[1, 0:1, 0:1]) > 0.5
        gv = jnp.where(half_is_mine, t_ref[...], r_ref[...])
        nm = ADAM_B1 * m_ref[0] + (1.0 - ADAM_B1) * gv
        nv = ADAM_B2 * v_ref[0] + (1.0 - ADAM_B2) * (gv * gv)
        g_ref[0] = gv
        nm_ref[0] = nm
        nv_ref[0] = nv
        d_ref[0] = -ADAM_LR * ((nm / c1) / (jnp.sqrt(nv / c2) + ADAM_EPS) + ADAM_WD * w_ref[0])

    lay = pl.BlockSpec((1, tr, C), lambda hf, i: (layer, hf * nh + i, 0))
    one = pl.BlockSpec((tr, C), lambda hf, i: (i, 0))
    operands = [w, m, v, total, recv, core] + ([] if prev is None else list(prev))
    return pl.pallas_call(
        body, name=name, grid=(2, nh),
        in_specs=[lay, lay, lay, one, one, pl.BlockSpec((2, 8, LANES), lambda hf, i: (0, 0, 0))] + [ANY] * n_prev,
        out_specs=(lay,) * 4, out_shape=(jax.ShapeDtypeStruct((2, R, C), F32),) * 4,
        input_output_aliases={6 + i: i for i in range(n_prev)},
        compiler_params=_params(("parallel", "parallel")),
    )(*operands)


ANY = pl.BlockSpec(memory_space=pl.ANY)


def _coords():
    return lax.axis_index("x"), lax.axis_index("y"), lax.axis_index("c")


HBM = pl.BlockSpec(memory_space=pltpu.HBM)
SEM = pl.BlockSpec(memory_space=pltpu.SEMAPHORE)
EFFECT = pltpu.SideEffectType.DATAFLOW_SIDE_EFFECTING


def _copies(plan, bufs, send_sems, recv_sems):
    return [pltpu.make_async_remote_copy(src_ref=s, dst_ref=d, send_sem=send_sems.at[i], recv_sem=recv_sems.at[i],
                                         device_id=to, device_id_type=MESH)
            for i, (s, d, to) in enumerate(plan(bufs))]


def _start_copies(bufs, groups, *, name):
    nb, ng = len(bufs), len(groups)

    def body(*refs):
        buf_refs = refs[:nb]
        sems = refs[nb:nb + 2 * ng]
        token = refs[-1]
        for g, (plan, _) in enumerate(groups):
            for cp in _copies(plan, buf_refs, sems[2 * g], sems[2 * g + 1]):
                cp.start()
        token[...] = jnp.zeros_like(token)

    sem_shapes = []
    for _, n in groups:
        sem_shapes += [pltpu.SemaphoreType.DMA((n,)), pltpu.SemaphoreType.DMA((n,))]
    out = pl.pallas_call(
        body, name=name, in_specs=[HBM] * nb,
        out_specs=tuple([SEM] * (2 * ng) + [HBM] * nb + [pl.BlockSpec(memory_space=pltpu.VMEM)]),
        out_shape=tuple(sem_shapes + [pltpu.HBM(b.shape, b.dtype) for b in bufs] + [jax.ShapeDtypeStruct((8, LANES), F32)]),
        input_output_aliases={i: 2 * ng + i for i in range(nb)},
        compiler_params=pltpu.CompilerParams(has_side_effects=EFFECT),
    )(*[pltpu.with_memory_space_constraint(b, pltpu.HBM) for b in bufs])
    sems = [(out[2 * g], out[2 * g + 1]) for g in range(ng)]
    return sems, list(out[2 * ng:2 * ng + nb]), out[-1]


def _wait_copies(bufs, sems, plan, after, *, name):
    nb = len(bufs)

    def body(*refs):
        buf_refs = refs[:nb]
        for cp in _copies(plan, buf_refs, refs[nb], refs[nb + 1]):
            cp.wait_send()
            cp.wait_recv()

    out = pl.pallas_call(
        body, name=name, in_specs=[HBM] * nb + [SEM, SEM, ANY], out_specs=tuple([HBM] * nb),
        out_shape=tuple(pltpu.HBM(b.shape, b.dtype) for b in bufs),
        input_output_aliases={i: i for i in range(nb)},
        compiler_params=pltpu.CompilerParams(has_side_effects=EFFECT),
    )(*bufs, sems[0], sems[1], after)
    return list(out)


def _half(ref, c):
    h = ref.shape[0] // 2
    return ref.at[pl.ds(c * h, h)]


def _ici_gather_plan(pairs):
    def plan(refs):
        x, y, c = _coords()
        me = 2 * x + y
        out = []
        for s, d in pairs:
            for cx, cy in [(1 - x, y), (x, 1 - y), (1 - x, 1 - y)]:
                out.append((_half(refs[s], c), _half(refs[d].at[me], c), (cx, cy, c)))
            out.append((refs[s], refs[d].at[me], (x, y, 1 - c)))
        return out
    return plan, 4 * len(pairs)


def _d2d_forward_plan(lands):
    def plan(refs):
        x, y, c = _coords()
        out = []
        for d in lands:
            for cx, cy in [(1 - x, y), (x, 1 - y), (1 - x, 1 - y)]:
                got = _half(refs[d].at[2 * cx + cy], c)
                out.append((got, got, (x, y, 1 - c)))
        return out
    return plan, 3 * len(lands)


def _swap_half_plan(pairs):
    def plan(refs):
        x, y, c = _coords()
        out = []
        for s, d in pairs:
            h = refs[d].shape[1]
            out.append((refs[s].at[:, pl.ds((1 - c) * h, h)], refs[d], (x, y, 1 - c)))
        return out
    return plan, len(pairs)


def _scatter_plan(pairs):
    def plan(refs):
        x, y, c = _coords()
        me = 2 * x + y
        out = []
        for s, d in pairs:
            for cx, cy in [(1 - x, y), (x, 1 - y), (1 - x, 1 - y)]:
                out.append((refs[s].at[2 * cx + cy], refs[d].at[me], (cx, cy, c)))
        return out
    return plan, 3 * len(pairs)


def _swap_total_plan(pairs):
    def plan(refs):
        x, y, c = _coords()
        return [(refs[s], refs[d], (x, y, 1 - c)) for s, d in pairs]
    return plan, len(pairs)


def _gather_all(block, *, name):
    def body(src, out, send_sems, recv_sems, local_sem):
        x, y, c = _coords()
        me = 4 * x + 2 * y + c
        flips = [(fx, fy, fc) for fx in (0, 1) for fy in (0, 1) for fc in (0, 1)][1:]
        mine = pltpu.make_async_copy(src, out.at[me], local_sem)
        mine.start()
        peers = [(x ^ fx, y ^ fy, c ^ fc) for fx, fy, fc in flips]
        cps = [pltpu.make_async_remote_copy(src_ref=src, dst_ref=out.at[me], send_sem=send_sems.at[k],
                                            recv_sem=recv_sems.at[k], device_id=peer, device_id_type=MESH)
               for k, peer in enumerate(peers)]
        for cp in cps:
            cp.start()
        for k, (px, py, pc) in enumerate(peers):
            slot = out.at[4 * px + 2 * py + pc]
            pltpu.make_async_remote_copy(src_ref=slot, dst_ref=slot, send_sem=send_sems.at[k], recv_sem=recv_sems.at[k],
                                         device_id=(px, py, pc), device_id_type=MESH).wait_recv()
        for cp in cps:
            cp.wait_send()
        mine.wait()

    return pl.pallas_call(
        body, name=name, in_specs=[ANY], out_specs=ANY,
        out_shape=jax.ShapeDtypeStruct((8,) + block.shape, block.dtype),
        scratch_shapes=[pltpu.SemaphoreType.DMA((7,)), pltpu.SemaphoreType.DMA((7,)), pltpu.SemaphoreType.DMA],
    )(block)


def _cols(o):
    return jnp.transpose(o, (1, 0, 2)).reshape(o.shape[1], -1)


def _uncols(full):
    return jnp.transpose(full.reshape(full.shape[0], 4, -1), (1, 0, 2))


def _rope_pad(x1, x2):
    z = jnp.zeros_like(x1)
    return jnp.concatenate([x1, z, x2, z], axis=-1)


def _head_pad(w, heads):
    r = w.reshape(w.shape[0], heads, QK_HEAD)
    half = QK_ROPE // 2
    out = jnp.concatenate([r[..., :QK_NOPE], _rope_pad(r[..., QK_NOPE:QK_NOPE + half], r[..., QK_NOPE + half:])], axis=-1)
    return out.reshape(w.shape[0], heads * HEAD_PAD)


def _head_unpad(w, heads):
    r = w.reshape(w.shape[0], heads, HEAD_PAD)
    half = QK_ROPE // 2
    out = jnp.concatenate([r[..., :QK_NOPE], r[..., QK_NOPE:QK_NOPE + half],
                           r[..., QK_NOPE + 2 * half:QK_NOPE + 3 * half]], axis=-1)
    return out.reshape(w.shape[0], heads * QK_HEAD)


class _Dims:
    def __init__(self, d, seq):
        self.d = d
        self.seq = seq
        self.t_real = N_META + seq
        self.t = -(-self.t_real // LANES) * LANES
        self.dc = d // 2
        self.dp = d // 2
        self.pg = self.dp // len(POOL_WINDOWS)
        self.heads = d // 128
        self.dff = 4 * d
        self.a_end = 3 * self.dc
        self.q_end = self.a_end + Q_LORA
        self.kv_end = self.q_end + KV_LORA
        self.kr_end = self.kv_end + QK_ROPE
        self.pool_end = self.kr_end + self.dp
        self.d_in = self.pool_end + 3 * d
        self.r_pool = 3 * self.dc
        self.r_q = self.r_pool + self.dp
        self.r_kv = self.r_q + Q_LORA
        self.r_kr = self.r_kv + KV_LORA
        self.r_width = self.r_kr + HEAD_PAD


def _in_weights(dm, w_in_pieces):
    w_in = _cols(w_in_pieces)
    half = QK_ROPE // 2
    kr = w_in[:, dm.kv_end:dm.kr_end]
    kr_p = jnp.concatenate([_rope_pad(kr[:, :half], kr[:, half:]), jnp.zeros((dm.d, HEAD_PAD - LANES), BF16)], axis=1)
    return dict(
        wg=w_in[:, dm.pool_end:],
        wr=jnp.concatenate([w_in[:, :dm.a_end], w_in[:, dm.kr_end:dm.pool_end], w_in[:, dm.a_end:dm.kv_end], kr_p], axis=1))


def _other_weights(dm, g):
    w_ukv = _cols(g["w_ukv"]).reshape(KV_LORA, dm.heads, QK_NOPE + V_HEAD)
    return dict(
        wuq=_head_pad(_cols(g["w_uq"]), dm.heads),
        wkn=w_ukv[:, :, :QK_NOPE].reshape(KV_LORA, dm.heads * QK_NOPE),
        wv=w_ukv[:, :, QK_NOPE:].reshape(KV_LORA, dm.heads * V_HEAD),
        wp=jnp.transpose(g["pool_w"], (1, 0, 2, 3)).reshape(len(POOL_WINDOWS), dm.pg, dm.pg),
        wba=_cols(g["w_branch_a"]), wbb=g["w_branch_b"].reshape(-1, dm.d), wbc=_cols(g["w_branch_c"]),
        wo=g["w_o"].reshape(-1, dm.d), wup=_cols(g["w_up"]), wdn=g["w_down"].reshape(-1, dm.d))


def _small_weights(small):
    return dict(
        conv_w=small["conv_w"],
        attn_norm=small["attn_norm"][None], mlp_norm=small["mlp_norm"][None],
        q_lat_norm=small["q_lat_norm"][None], kv_lat_norm=small["kv_lat_norm"][None],
        q_norm=_head_pad(small["q_norm"][None], 1), k_norm=_head_pad(small["k_norm"][None], 1),
        pool_scale=small["pool_scale"][None],
    )


def _grad_piece(dm, dw, name):
    half = QK_ROPE // 2
    rows = lambda a: a.reshape((4, a.shape[0] // 4) + a.shape[1:])
    if name == "w_in":
        dwr, dwg = dw["wr"], dw["wg"]
        out = _uncols(jnp.concatenate([
            dwr[:, :dm.r_pool], dwr[:, dm.r_q:dm.r_kr], dwr[:, dm.r_kr:dm.r_kr + half],
            dwr[:, dm.r_kr + 2 * half:dm.r_kr + 3 * half], dwr[:, dm.r_pool:dm.r_q], dwg], axis=1))
    elif name == "w_ukv":
        out = _uncols(jnp.concatenate([dw["wkn"].reshape(KV_LORA, dm.heads, QK_NOPE),
                                       dw["wv"].reshape(KV_LORA, dm.heads, V_HEAD)], axis=-1).reshape(KV_LORA, -1))
    elif name == "w_uq":
        out = _uncols(_head_unpad(dw["wuq"], dm.heads))
    elif name == "pool_w":
        out = jnp.transpose(dw["wp"].reshape(len(POOL_WINDOWS), 4, dm.pg // 4, dm.pg), (1, 0, 2, 3))
    elif name in ("w_branch_a", "w_branch_c", "w_up"):
        out = _uncols(dw[{"w_branch_a": "wba", "w_branch_c": "wbc", "w_up": "wup"}[name]])
    else:
        out = rows(dw[{"w_branch_b": "wbb", "w_o": "wo", "w_down": "wdn"}[name]])
    return out.astype(BF16)


def _layer_fwd(dm, W, x, cos_t, sin_t, tag, more=None):
    n = lambda s: f"{s}_{tag}"
    h = _rms_fwd(x, W["attn_norm"], name=n("attn_norm"))
    gl = _mm(h, W["wg"], name=n("proj_gates"))
    rest = _mm(h, W["wr"], name=n("proj_rest"))
    if more is not None:
        W.update(more(rest))
    y_a = _conv_fwd(rest, W["conv_w"], name=n("conv"), dc=dm.dc)
    y_c = _pool_fwd(rest, W["wp"], W["pool_scale"], name=n("pool"), seg0=dm.r_pool // dm.pg, pg=dm.pg)
    q_lat = _rms_fwd(rest, W["q_lat_norm"], name=n("q_lat_norm"), width=Q_LORA, seg=dm.r_q // Q_LORA)
    kv_lat = _rms_fwd(rest, W["kv_lat_norm"], name=n("kv_lat_norm"), width=KV_LORA, seg=dm.r_kv // KV_LORA)
    q_raw = _mm(q_lat, W["wuq"], name=n("up_q"))
    k_nope = _mm(kv_lat, W["wkn"], name=n("up_k"))
    v = _mm(kv_lat, W["wv"], name=n("up_v"), out_dtype=BF16)
    q, k = _qk_fwd(q_raw, k_nope, rest, cos_t, sin_t, W["q_norm"], W["k_norm"], name=n("qk_norm_rope"),
                   heads=dm.heads, kr_seg=dm.r_kr // HEAD_PAD)
    y_b, lse = _flash_fwd(q, k, v, name=n("attention"), heads=dm.heads)
    pa = _mm(y_a, W["wba"], name=n("branch_a"))
    pb = _mm(y_b, W["wbb"], name=n("branch_b"))
    pc = _mm(y_c, W["wbc"], name=n("branch_c"))
    merged = _merge_fwd(gl, pa, pb, pc, name=n("merge"), d=dm.d)
    x1 = _mm(merged, W["wo"], name=n("out_proj"), add=x)
    h2 = _rms_fwd(x1, W["mlp_norm"], name=n("mlp_norm"))
    up, act = _mm(h2, W["wup"], name=n("mlp_up"), epi="relu2")
    x2 = _mm(act, W["wdn"], name=n("mlp_down"), add=x1, tk=2048)
    saved = dict(x=x, h=h, gl=gl, rest=rest, y_a=y_a, y_c=y_c, q_lat=q_lat, kv_lat=kv_lat, q_raw=q_raw, k_nope=k_nope,
                 v=v, q=q, k=k, y_b=y_b, lse=lse, pa=pa, pb=pb, pc=pc, merged=merged, x1=x1, h2=h2, up=up, act=act)
    return x2, saved


def _layer_bwd(dm, W, S, dx2, cos_t, sin_t, tag, hook=None):
    n = lambda s: f"{s}_{tag}"
    dw, ds = {}, {}
    if hook is None:
        hook = lambda point, t, dw_so_far: ()
    dup = _mm(dx2, W["wdn"], name=n("d_mlp_down"), tb=True, aux=S["up"], epi="drelu2", out_dtype=BF16,
              after=hook("start", dx2, dw))
    dw["wdn"] = _mm(S["act"], dx2, name=n("dw_mlp_down"), ta=True, tm=1024, tk=1408)
    dh2 = _mm(dup, W["wup"], name=n("d_mlp_up"), tb=True, tk=2048)
    dw["wup"] = _mm(S["h2"], dup, name=n("dw_mlp_up"), ta=True, tm=1024, tk=1408)
    dx1, ds["mlp_norm"] = _rms_bwd(dh2, S["x1"], W["mlp_norm"], name=n("d_mlp_norm"), res=dx2)
    dmerged = _mm(dx1, W["wo"], name=n("d_out_proj"), tb=True, after=hook("after_mlp", dx1, dw))
    dw["wo"] = _mm(S["merged"], dx1, name=n("dw_out_proj"), ta=True, tm=1024, tk=1408)
    dpa, dpb, dpc, dg0, dg1, dg2 = _merge_bwd(dmerged, S["gl"], S["pa"], S["pb"], S["pc"], name=n("d_merge"), d=dm.d)
    dgl = jnp.concatenate([dg0, dg1, dg2], axis=1)
    dy_a = _mm(dpa, W["wba"], name=n("d_branch_a"), tb=True)
    dw["wba"] = _mm(S["y_a"], dpa, name=n("dw_branch_a"), ta=True, tm=1024, tk=1408)
    dy_b = _mm(dpb, W["wbb"], name=n("d_branch_b"), tb=True, out_dtype=BF16)
    dw["wbb"] = _mm(S["y_b"], dpb, name=n("dw_branch_b"), ta=True, tm=1024, tk=1408)
    dy_c = _mm(dpc, W["wbc"], name=n("d_branch_c"), tb=True)
    dw["wbc"] = _mm(S["y_c"], dpc, name=n("dw_branch_c"), ta=True, tm=1024, tk=1408)
    dq, dk, dv = _flash_bwd(S["q"], S["k"], S["v"], S["y_b"], dy_b, S["lse"], name=n("d_attention"), heads=dm.heads)
    after_attention = hook("after_attention", dq, dw)
    dq_raw, dk_nope, dk_rope, dgq, dgk = _qk_bwd(
        dq, dk, S["q_raw"], S["k_nope"], S["rest"], cos_t, sin_t, W["q_norm"], W["k_norm"], name=n("d_qk_norm_rope"),
        heads=dm.heads, kr_seg=dm.r_kr // HEAD_PAD)
    ds["q_norm"] = _head_unpad(dgq, 1)
    ds["k_norm"] = _head_unpad(dgk, 1)
    dkv_v = _mm(dv, W["wv"], name=n("d_up_v"), tb=True, after=after_attention)
    dq_lat_n = _mm(dq_raw, W["wuq"], name=n("d_up_q"), tb=True, tk=2048, after=hook("after_qk", dq_raw, dw))
    dw["wuq"] = _mm(S["q_lat"], dq_raw, name=n("dw_up_q"), ta=True, tm=512, tk=1408)
    dkv_lat_n = _mm(dk_nope, W["wkn"], name=n("d_up_k"), tb=True, add=dkv_v)
    dw["wkn"] = _mm(S["kv_lat"], dk_nope, name=n("dw_up_k"), ta=True, tm=512, tk=1408)
    dw["wv"] = _mm(S["kv_lat"], dv, name=n("dw_up_v"), ta=True, tm=512, tk=1408)
    dq_lat, ds["q_lat_norm"] = _rms_bwd(dq_lat_n, S["rest"], W["q_lat_norm"], name=n("d_q_lat_norm"), width=Q_LORA,
                                        seg=dm.r_q // Q_LORA, out_dtype=BF16)
    dkv_lat, ds["kv_lat_norm"] = _rms_bwd(dkv_lat_n, S["rest"], W["kv_lat_norm"], name=n("d_kv_lat_norm"), width=KV_LORA,
                                          seg=dm.r_kv // KV_LORA, out_dtype=BF16)
    du, db, dc, ds["conv_w"] = _conv_bwd(S["rest"], W["conv_w"], dy_a, name=n("d_conv"), dc=dm.dc)
    dpool, dw["wp"], ds["pool_scale"] = _pool_bwd(S["rest"], W["wp"], W["pool_scale"], dy_c, name=n("d_pool"),
                                                  seg0=dm.r_pool // dm.pg, pg=dm.pg)
    drest = jnp.concatenate([du, db, dc, dpool, dq_lat, dkv_lat, dk_rope], axis=1)
    dh_g = _mm(dgl, W["wg"], name=n("d_proj_gates"), tb=True, tk=2048)
    dh = _mm(drest, W["wr"], name=n("d_proj_rest"), tb=True, add=dh_g, tk=1792)
    dw["wg"] = _mm(S["h"], dgl, name=n("dw_proj_gates"), ta=True, tm=1024, tk=1408)
    dw["wr"] = _mm(S["h"], drest, name=n("dw_proj_rest"), ta=True, tm=1024, tk=1408)
    dx, ds["attn_norm"] = _rms_bwd(dh, S["x"], W["attn_norm"], name=n("d_attn_norm"), res=dx1)
    return dx, dw, ds


BIG = ("w_in", "w_uq", "w_ukv", "pool_w", "w_branch_a", "w_branch_b", "w_branch_c", "w_o", "w_up", "w_down")
REPLICATED = ("attn_norm", "q_lat_norm", "kv_lat_norm", "q_norm", "k_norm", "pool_scale", "mlp_norm")
WEIGHTS = ("meta_tokens", "attn_norm", "w_in", "conv_w", "q_lat_norm", "kv_lat_norm", "w_uq", "w_ukv", "q_norm",
           "k_norm", "pool_w", "pool_scale", "w_branch_a", "w_branch_b", "w_branch_c", "w_o", "mlp_norm", "w_up",
           "w_down")


def _pack(arrays):
    flat = jnp.concatenate([a.reshape(-1).astype(F32) for a in arrays])
    pad = (-flat.shape[0]) % (8 * LANES)
    return jnp.pad(flat, (0, pad)).reshape(-1, LANES)


def _unpack(flat, shapes):
    out, pos = [], 0
    flat = flat.reshape(-1)
    for shp in shapes:
        size = math.prod(shp)
        out.append(flat[pos:pos + size].reshape(shp))
        pos += size
    return out


def _update(w, g, m, v, name):
    shp = w.shape
    to2 = lambda a: a.reshape(-1, shp[-1])
    delta, nm, nv = _adamw(to2(w), to2(g), to2(m), to2(v), name=name)
    return delta.reshape(shp), nm.reshape(shp), nv.reshape(shp)


def _step(args):
    x = args["x"][0]
    seq, d = x.shape
    dm = _Dims(d, seq)
    xi, yi, ci = _coords()
    chip = 2 * xi + yi

    small_w = _gather_all(_pack([args["conv_w"], args["meta_tokens"]]), name="gather_small_weights")
    order = [(k, l) for l in range(2) for k in BIG]
    shards = {n: args[n[0]][n[1]].astype(BF16) for n in order}
    small_w, shards[order[0]] = lax.optimization_barrier((small_w, shards[order[0]]))
    lands = {n: lax.empty((4,) + shards[n].shape, BF16) for n in order}
    group_names = [[("w_in", 0)], [(k, 0) for k in BIG[1:]], [(k, 1) for k in BIG]]
    at = {n: i for i, n in enumerate(order)}
    sems, thru, token = _start_copies(
        [shards[n] for n in order] + [lands[n] for n in order],
        [_ici_gather_plan([(at[n], len(order) + at[n]) for n in g]) for g in group_names], name="start_gather_ici")
    for i, n in enumerate(order):
        shards[n], lands[n] = thru[i], thru[len(order) + i]

    def finish_gather(g, after, tag):
        names = group_names[g]
        k = len(names)
        plan, _ = _ici_gather_plan([(i, k + i) for i in range(k)])
        got = _wait_copies([shards[n] for n in names] + [lands[n] for n in names], sems[g], plan, after,
                           name=f"wait_gather_ici_{tag}")
        for i, n in enumerate(names):
            shards[n] = got[i]
        fwd = _d2d_forward_plan(list(range(k)))
        sems2, bufs2, tok2 = _start_copies(got[k:], [fwd], name=f"start_gather_d2d_{tag}")
        return names, bufs2, sems2[0], fwd[0], tok2

    def land_gather(pending, after, tag):
        names, bufs2, sems2, plan, tok2 = pending
        done = _wait_copies(bufs2, sems2, plan, tok2 if after is None else after, name=f"wait_gather_d2d_{tag}")
        return {n[0]: buf for n, buf in zip(names, done)}

    conv_shape, meta_shape = args["conv_w"].shape, args["meta_tokens"].shape
    per_chip = [_unpack(small_w[2 * j], [conv_shape, meta_shape]) for j in range(4)]
    conv_full = jnp.concatenate([p[0] for p in per_chip], axis=-1)
    meta_full = jnp.concatenate([p[1] for p in per_chip], axis=-1)

    layers = []
    for l in range(2):
        small = {k: args[k][l] for k in REPLICATED}
        small["conv_w"] = conv_full[l]
        layers.append(_small_weights(small))
    layers[0].update(_in_weights(dm, land_gather(finish_gather(0, token, "l0_in"), None, "l0_in")["w_in"]))

    pos = jnp.arange(dm.t, dtype=F32)
    inv = ROPE_THETA ** (-jnp.arange(0, QK_ROPE, 2, dtype=F32) / QK_ROPE)
    ang = pos[:, None] * inv[None, :]
    cos_t = _rope_pad(jnp.cos(ang), jnp.cos(ang))
    sin_t = _rope_pad(-jnp.sin(ang), jnp.sin(ang))
    tail = jnp.zeros((dm.t - dm.t_real, d), F32)
    h0 = jnp.concatenate([meta_full, x, tail], axis=0)
    target = jnp.concatenate([jnp.zeros((N_META, d), F32), args["loss_target"][0], tail], axis=0)

    def rest_of_layer0(after):
        return _other_weights(dm, land_gather(finish_gather(1, after, "l0_rest"), None, "l0_rest"))

    h1, saved0 = _layer_fwd(dm, layers[0], h0, cos_t, sin_t, "l0", more=rest_of_layer0)
    g1 = land_gather(finish_gather(2, saved0["y_b"], "l1"), h1, "l1")
    layers[1].update(_in_weights(dm, g1["w_in"]))
    layers[1].update(_other_weights(dm, g1))
    h2, saved1 = _layer_fwd(dm, layers[1], h1, cos_t, sin_t, "l1")
    sq, dy = _loss(h2, target, name="loss_head", first=N_META, last=dm.t_real)
    loss = lax.psum(0.5 / d * sq[0, 0], ("x", "y", "c"))
    core, chip_flags = _one_hot(ci, 2), _one_hot(chip, 4)

    class Reduce:
        def __init__(self, names, dw, tag):
            self.names, self.tag, self.nb = names, tag, len(names)
            self.idx = [(i, self.nb + i) for i in range(self.nb)]
            parts = [_as3d(_grad_piece(dm, dw, k)) for k in names]
            recv = [lax.empty((4, p.shape[1] // 2, p.shape[2]), BF16) for p in parts]
            self.plan = _swap_half_plan(self.idx)
            self.sems, self.bufs, self.token = _start_copies(parts + recv, [self.plan], name=f"start_swap_{tag}")

        def _land(self, after, what):
            return _wait_copies(self.bufs, self.sems[0], self.plan[0], self.token if after is None else after,
                                name=f"wait_{what}_{self.tag}")

        def scatter(self, after=None):
            got = self._land(after, "swap")
            pairs = [_pair_sum(got[i], got[j], core, name=f"pair_sum_{k}_{self.tag}")
                     for (i, j), k in zip(self.idx, self.names)]
            self.plan = _scatter_plan(self.idx)
            self.sems, self.bufs, self.token = _start_copies(pairs + [lax.empty(p.shape, BF16) for p in pairs],
                                                             [self.plan], name=f"start_scatter_{self.tag}")
            return self.token

        def totals(self, after=None):
            got = self._land(after, "scatter")
            sums = [_chip_sum(got[i], got[j], chip_flags, name=f"chip_sum_{k}_{self.tag}")
                    for (i, j), k in zip(self.idx, self.names)]
            self.plan = _swap_total_plan(self.idx)
            self.sems, self.bufs, self.token = _start_copies(sums + [lax.empty(t.shape, F32) for t in sums],
                                                             [self.plan], name=f"start_swap_total_{self.tag}")
            return self.token

        def finish(self, after=None):
            got = self._land(after, "swap_total")
            return {k: (got[i], got[j]) for (i, j), k in zip(self.idx, self.names)}

    dh1, dw1, ds1 = _layer_bwd(dm, layers[1], saved1, dy, cos_t, sin_t, "l1")
    early = ("w_down", "w_up", "w_o", "w_branch_a", "w_branch_b", "w_branch_c")
    late = tuple(k for k in BIG if k not in early)
    stage = {}

    def during_layer0(point, t, dw):
        if point == "start":
            stage["l1"] = Reduce(BIG, dw1, "l1")
            return (stage["l1"].token,)
        if point == "after_mlp":
            return (stage["l1"].scatter(after=t),)
        if point == "after_attention":
            tok = stage["l1"].totals(after=t)
            stage["l0a"] = Reduce(early, dw, "l0a")
            return (tok, stage["l0a"].token)
        stage["red1"] = stage["l1"].finish(after=t)
        return (stage["l0a"].scatter(after=t),)

    dh0, dw0, ds0 = _layer_bwd(dm, layers[0], saved0, dh1, cos_t, sin_t, "l0", hook=during_layer0)
    grad_x = dh0[N_META:dm.t_real][None]
    stage["l0a"].totals(after=dh0)
    stage["l0b"] = Reduce(late, dw0, "l0b")
    stage["l0b"].scatter()
    stage["l0b"].totals()
    red1 = stage["red1"]
    red0 = {**stage["l0a"].finish(), **stage["l0b"].finish()}
    grads = {}

    small_names = REPLICATED + ("conv_w",)
    small_parts = [jnp.stack([ds0[k].reshape(ds0[k].shape[-2:] if k == "conv_w" else (-1,)),
                              ds1[k].reshape(ds1[k].shape[-2:] if k == "conv_w" else (-1,))]) for k in small_names]
    small_parts.append(dh0[:N_META])
    small_all = _gather_all(_pack(small_parts), name="gather_small_grads")
    small_sum = _sum_stack(small_all, name="sum_small_grads", out_dtype=F32)
    small_g = dict(zip(small_names + ("meta_tokens",), _unpack(small_sum, [p.shape for p in small_parts])))
    for k in REPLICATED:
        grads[k] = small_g[k]
    dcw = conv_shape[-1]
    grads["conv_w"] = lax.dynamic_slice_in_dim(small_g["conv_w"], chip * dcw, dcw, axis=2)
    dmeta = meta_shape[-1]
    grads["meta_tokens"] = lax.dynamic_slice_in_dim(small_g["meta_tokens"], chip * dmeta, dmeta, axis=1)

    delta, new_m, new_v = {}, {}, {}
    for k in WEIGHTS:
        shp = args[k].shape
        if k in BIG:
            wmv = [args[p + k].reshape(2, -1, shp[-1]) for p in ("", "m_", "v_")]
            out = _adamw_layer(*wmv, *red1[k], core, 1, None, name=f"adamw_{k}_l1")
            out = _adamw_layer(*wmv, *red0[k], core, 0, out, name=f"adamw_{k}_l0")
            grads[k], delta[k], new_m[k], new_v[k] = (o.reshape(shp) for o in out)
        else:
            grads[k] = grads[k].reshape(shp)
            delta[k], new_m[k], new_v[k] = _update(args[k], grads[k], args["m_" + k], args["v_" + k], f"adamw_{k}")
    return (loss, grad_x, *[grads[k] for k in WEIGHTS], *[delta[k] for k in WEIGHTS],
            *[new_m[k] for k in WEIGHTS], *[new_v[k] for k in WEIGHTS])


def kernel(x, meta_tokens, attn_norm, w_in, conv_w, q_lat_norm, kv_lat_norm, w_uq, w_ukv, q_norm, k_norm, pool_w, pool_scale, w_branch_a, w_branch_b, w_branch_c, w_o, mlp_norm, w_up, w_down, loss_target, m_meta_tokens, m_attn_norm, m_w_in, m_conv_w, m_q_lat_norm, m_kv_lat_norm, m_w_uq, m_w_ukv, m_q_norm, m_k_norm, m_pool_w, m_pool_scale, m_w_branch_a, m_w_branch_b, m_w_branch_c, m_w_o, m_mlp_norm, m_w_up, m_w_down, v_meta_tokens, v_attn_norm, v_w_in, v_conv_w, v_q_lat_norm, v_kv_lat_norm, v_w_uq, v_w_ukv, v_q_norm, v_k_norm, v_pool_w, v_pool_scale, v_w_branch_a, v_w_branch_b, v_w_branch_c, v_w_o, v_mlp_norm, v_w_up, v_w_down):
    return _step(dict(locals()))
```

```python
import functools
import math

import jax
import jax.numpy as jnp
from jax import lax
from jax.experimental import pallas as pl
from jax.experimental.pallas import tpu as pltpu

F32 = jnp.float32
BF16 = jnp.bfloat16
MESH = pl.DeviceIdType.MESH

EPS = 1e-6
N_META = 16
QK_NOPE = 128
QK_ROPE = 64
QK_HEAD = QK_NOPE + QK_ROPE
V_HEAD = 128
HEAD_PAD = 256
Q_LORA = 512
KV_LORA = 512
ROPE_THETA = 10000.0
POOL_WINDOWS = (2, 4, 8, 16)
HALO = 16
LANES = 128
ADAM_LR = 0.001
ADAM_B1 = 0.9
ADAM_B2 = 0.999
ADAM_EPS = 1e-08
ADAM_WD = 0.01
ADAM_STEP = 10
VMEM_LIMIT = 52 * 1024 * 1024
NEG = -1e30


def _tile(n, target, mult=LANES):
    best = None
    for t in range(mult, min(n, target) + 1, mult):
        if n % t == 0:
            best = t
    return n if best is None else best


def _params(sem=None):
    return pltpu.CompilerParams(dimension_semantics=sem, vmem_limit_bytes=VMEM_LIMIT)


def _mm(a, b, *, name, ta=False, tb=False, add=None, aux=None, epi=None, out_dtype=F32,
        tm=704, tn=1024, tk=None, after=()):
    if ta:
        K, M = a.shape
    else:
        M, K = a.shape
    if tb:
        N, kb = b.shape
    else:
        kb, N = b.shape
    assert K == kb, (a.shape, b.shape, ta, tb)
    tm = _tile(M, tm, LANES if ta else 16)
    tn = _tile(N, tn, LANES)
    tk = K if tk is None else _tile(K, tk, LANES if (not ta or tb) else 16)
    nk = K // tk
    grid = (M // tm, N // tn, nk)

    a_spec = pl.BlockSpec((tk, tm), lambda i, j, k: (k, i)) if ta else pl.BlockSpec((tm, tk), lambda i, j, k: (i, k))
    b_spec = pl.BlockSpec((tn, tk), lambda i, j, k: (j, k)) if tb else pl.BlockSpec((tk, tn), lambda i, j, k: (k, j))
    o_spec = pl.BlockSpec((tm, tn), lambda i, j, k: (i, j))
    in_specs = [a_spec, b_spec]
    operands = [a, b]
    if add is not None:
        in_specs.append(o_spec)
        operands.append(add)
    if aux is not None:
        in_specs.append(o_spec)
        operands.append(aux)
    after = tuple(after)
    in_specs += [pl.BlockSpec(memory_space=pl.ANY)] * len(after)
    operands += list(after)
    if epi == "relu2":
        out_shape = (jax.ShapeDtypeStruct((M, N), BF16), jax.ShapeDtypeStruct((M, N), BF16))
        out_specs = (o_spec, o_spec)
    else:
        out_shape = jax.ShapeDtypeStruct((M, N), out_dtype)
        out_specs = o_spec
    dims = (((0 if ta else 1,), (1 if tb else 0,)), ((), ()))
    has_add, has_aux = add is not None, aux is not None

    def body(*refs):
        a_ref, b_ref = refs[0], refs[1]
        pos = 2
        add_ref = aux_ref = None
        if has_add:
            add_ref = refs[pos]
            pos += 1
        if has_aux:
            aux_ref = refs[pos]
            pos += 1
        pos += len(after)
        n_out = 2 if epi == "relu2" else 1
        out_refs = refs[pos:pos + n_out]
        acc_ref = refs[pos + n_out] if nk > 1 else None

        part = lax.dot_general(a_ref[...].astype(BF16), b_ref[...].astype(BF16), dims,
                               preferred_element_type=F32)

        def finish(acc):
            if has_add:
                acc = acc + add_ref[...].astype(F32)
            if epi == "relu2":
                r = jnp.maximum(acc, 0.0)
                out_refs[0][...] = acc.astype(BF16)
                out_refs[1][...] = (r * r).astype(BF16)
            elif epi == "drelu2":
                u = aux_ref[...].astype(F32)
                out_refs[0][...] = (acc * (2.0 * jnp.maximum(u, 0.0))).astype(out_dtype)
            else:
                out_refs[0][...] = acc.astype(out_dtype)

        if nk == 1:
            finish(part)
        else:
            k = pl.program_id(2)

            @pl.when(k == 0)
            def _():
                acc_ref[...] = part

            @pl.when(k > 0)
            def _():
                acc_ref[...] += part

            @pl.when(k == nk - 1)
            def _():
                finish(acc_ref[...])

    scratch = [pltpu.VMEM((tm, tn), F32)] if nk > 1 else []
    return pl.pallas_call(
        body, name=name, grid=grid, in_specs=in_specs, out_specs=out_specs, out_shape=out_shape,
        scratch_shapes=scratch, compiler_params=_params(("parallel", "parallel", "arbitrary")),
    )(*operands)


def _rms_fwd(x, g, *, name, width=None, seg=0, tm=384):
    T = x.shape[0]
    width = x.shape[1] if width is None else width
    tm = _tile(T, tm, 16)

    def body(x_ref, g_ref, o_ref):
        xf = x_ref[...].astype(F32)
        r = lax.rsqrt(jnp.mean(xf * xf, axis=-1, keepdims=True) + EPS)
        o_ref[...] = (xf * r * g_ref[...]).astype(BF16)

    return pl.pallas_call(
        body, name=name, grid=(T // tm,),
        in_specs=[pl.BlockSpec((tm, width), lambda i: (i, seg)), pl.BlockSpec((1, width), lambda i: (0, 0))],
        out_specs=pl.BlockSpec((tm, width), lambda i: (i, 0)),
        out_shape=jax.ShapeDtypeStruct((T, width), BF16),
        compiler_params=_params(("parallel",)),
    )(x, g)


def _rms_bwd(dy, x, g, *, name, width=None, seg=0, res=None, out_dtype=F32, tm=384):
    T = x.shape[0]
    width = x.shape[1] if width is None else width
    tm = _tile(T, tm, 16)
    has_res = res is not None

    def body(*refs):
        dy_ref, x_ref, g_ref = refs[:3]
        res_ref = refs[3] if has_res else None
        dx_ref, dg_ref = refs[-2:]
        xf = x_ref[...].astype(F32)
        dyf = dy_ref[...].astype(F32)
        r = lax.rsqrt(jnp.mean(xf * xf, axis=-1, keepdims=True) + EPS)
        xhat = xf * r
        dyh = dyf * g_ref[...]
        dx = r * (dyh - xhat * jnp.mean(dyh * xhat, axis=-1, keepdims=True))
        if has_res:
            dx = dx + res_ref[...].astype(F32)
        dx_ref[...] = dx.astype(out_dtype)
        part = jnp.sum(dyf * xhat, axis=0, keepdims=True)

        @pl.when(pl.program_id(0) == 0)
        def _():
            dg_ref[...] = part

        @pl.when(pl.program_id(0) > 0)
        def _():
            dg_ref[...] += part

    row = pl.BlockSpec((tm, width), lambda i: (i, 0))
    in_specs = [row, pl.BlockSpec((tm, width), lambda i: (i, seg)), pl.BlockSpec((1, width), lambda i: (0, 0))]
    operands = [dy, x, g]
    if has_res:
        in_specs.append(row)
        operands.append(res)
    return pl.pallas_call(
        body, name=name, grid=(T // tm,), in_specs=in_specs,
        out_specs=(row, pl.BlockSpec((1, width), lambda i: (0, 0))),
        out_shape=(jax.ShapeDtypeStruct((T, width), out_dtype), jax.ShapeDtypeStruct((1, width), F32)),
        compiler_params=_params(("arbitrary",)),
    )(*operands)


def _down(ext, k):
    return pltpu.roll(ext, k, 0)


def _up(ext, k):
    return pltpu.roll(ext, ext.shape[0] - k, 0)


def _pre_halo(ref, r, R):
    start = pl.multiple_of(jnp.maximum(r * R - HALO, 0), 8)
    keep = (r > 0).astype(F32)
    return ref[pl.ds(start, HALO), :].astype(F32) * keep


def _post_halo(ref, r, R, n_chunks):
    start = pl.multiple_of(jnp.minimum(r * R + R, (n_chunks - 1) * R + R - HALO), 8)
    keep = (r < n_chunks - 1).astype(F32)
    return ref[pl.ds(start, HALO), :].astype(F32) * keep


def _chunk(ref, r, R):
    return ref[pl.ds(pl.multiple_of(r * R, 8), R), :].astype(F32)


def _conv_fwd(rest, conv_w, *, name, dc, tc=128, rows=1056):
    T = rest.shape[0]
    tc = _tile(dc, tc)
    nb = dc // tc
    R = _tile(T, rows, 16)
    n_chunks = T // R

    def body(u_ref, b_ref, c_ref, w_ref, y_ref):
        w0, w1, w2 = w_ref[0:1, :], w_ref[1:2, :], w_ref[2:3, :]

        def chunk(r, carry):
            cu = _chunk(c_ref, r, R) * _chunk(u_ref, r, R)
            ext = jnp.concatenate([_pre_halo(c_ref, r, R) * _pre_halo(u_ref, r, R), cu], axis=0)
            conv = w0 * _down(ext, 2)[HALO:] + w1 * _down(ext, 1)[HALO:] + w2 * cu
            y_ref[pl.ds(pl.multiple_of(r * R, 8), R), :] = (_chunk(b_ref, r, R) * conv).astype(BF16)
            return carry

        lax.fori_loop(0, n_chunks, chunk, 0)

    col = lambda off: pl.BlockSpec((T, tc), lambda j: (0, off * nb + j))
    return pl.pallas_call(
        body, name=name, grid=(nb,),
        in_specs=[col(0), col(1), col(2), pl.BlockSpec((3, tc), lambda j: (0, j))],
        out_specs=pl.BlockSpec((T, tc), lambda j: (0, j)),
        out_shape=jax.ShapeDtypeStruct((T, dc), BF16),
        compiler_params=_params(("parallel",)),
    )(rest, rest, rest, conv_w)


def _conv_bwd(rest, conv_w, dy, *, name, dc, tc=128, rows=1056):
    T = rest.shape[0]
    tc = _tile(dc, tc)
    nb = dc // tc
    R = _tile(T, rows, 16)
    n_chunks = T // R

    def body(u_ref, b_ref, c_ref, w_ref, dy_ref, du_ref, db_ref, dc_ref, dw_ref):
        w0, w1, w2 = w_ref[0:1, :], w_ref[1:2, :], w_ref[2:3, :]

        def chunk(r, carry):
            a0, a1, a2 = carry
            u, b, c = _chunk(u_ref, r, R), _chunk(b_ref, r, R), _chunk(c_ref, r, R)
            dy_c = _chunk(dy_ref, r, R)
            cu = c * u
            ext = jnp.concatenate([_pre_halo(c_ref, r, R) * _pre_halo(u_ref, r, R), cu], axis=0)
            cu1, cu2 = _down(ext, 1)[HALO:], _down(ext, 2)[HALO:]
            conv = w0 * cu2 + w1 * cu1 + w2 * cu
            dconv = dy_c * b
            dext = jnp.concatenate(
                [dconv, _post_halo(dy_ref, r, R, n_chunks) * _post_halo(b_ref, r, R, n_chunks)], axis=0)
            dcu = w2 * dconv + w1 * _up(dext, 1)[:R] + w0 * _up(dext, 2)[:R]
            rows_at = pl.ds(pl.multiple_of(r * R, 8), R)
            db_ref[rows_at, :] = (dy_c * conv).astype(BF16)
            du_ref[rows_at, :] = (dcu * c).astype(BF16)
            dc_ref[rows_at, :] = (dcu * u).astype(BF16)
            return (a0 + jnp.sum(dconv * cu2, axis=0, keepdims=True),
                    a1 + jnp.sum(dconv * cu1, axis=0, keepdims=True),
                    a2 + jnp.sum(dconv * cu, axis=0, keepdims=True))

        zero = jnp.zeros((1, tc), F32)
        a0, a1, a2 = lax.fori_loop(0, n_chunks, chunk, (zero, zero, zero))
        dw_ref[0:1, :] = a0
        dw_ref[1:2, :] = a1
        dw_ref[2:3, :] = a2

    col = lambda off: pl.BlockSpec((T, tc), lambda j: (0, off * nb + j))
    own = pl.BlockSpec((T, tc), lambda j: (0, j))
    return pl.pallas_call(
        body, name=name, grid=(nb,),
        in_specs=[col(0), col(1), col(2), pl.BlockSpec((3, tc), lambda j: (0, j)), own],
        out_specs=(own, own, own, pl.BlockSpec((3, tc), lambda j: (0, j))),
        out_shape=(jax.ShapeDtypeStruct((T, dc), BF16),) * 3 + (jax.ShapeDtypeStruct((3, dc), F32),),
        compiler_params=_params(("parallel",)),
    )(rest, rest, rest, conv_w, dy)


def _window_count(r, R, n_rows, w, first_row_offset):
    t = lax.broadcasted_iota(jnp.int32, (n_rows, 1), 0) + (r * R + first_row_offset)
    return jnp.minimum(t + 1, w).astype(F32)


def _pool_fwd(rest, pool_w, pool_scale, *, name, seg0, pg, rows=1056):
    T = rest.shape[0]
    R = _tile(T, rows, 16)
    n_chunks = T // R
    n_groups = len(POOL_WINDOWS)

    def body(x_ref, w_ref, s_ref, y_ref):
        def run(window):
            def chunk(r, carry):
                g = _chunk(x_ref, r, R)
                s = jnp.concatenate([_pre_halo(x_ref, r, R), g], axis=0)
                k = 1
                while k < window:
                    s = s + _down(s, k)
                    k *= 2
                pooled = s[HALO:] / _window_count(r, R, R, window, 0) - g
                mixed = jnp.dot(pooled.astype(BF16), w_ref[0], preferred_element_type=F32)
                y_ref[pl.ds(pl.multiple_of(r * R, 8), R), :] = (mixed * s_ref[...]).astype(BF16)
                return carry

            lax.fori_loop(0, n_chunks, chunk, 0)

        for gi, window in enumerate(POOL_WINDOWS):
            pl.when(pl.program_id(0) == gi)(functools.partial(run, window))

    return pl.pallas_call(
        body, name=name, grid=(n_groups,),
        in_specs=[pl.BlockSpec((T, pg), lambda g: (0, seg0 + g)),
                  pl.BlockSpec((1, pg, pg), lambda g: (g, 0, 0)),
                  pl.BlockSpec((1, pg), lambda g: (0, g))],
        out_specs=pl.BlockSpec((T, pg), lambda g: (0, g)),
        out_shape=jax.ShapeDtypeStruct((T, n_groups * pg), BF16),
        compiler_params=_params(("parallel",)),
    )(rest, pool_w, pool_scale)


def _pool_bwd(rest, pool_w, pool_scale, dy, *, name, seg0, pg, rows=1056):
    T = rest.shape[0]
    R = _tile(T, rows, 16)
    n_chunks = T // R
    n_groups = len(POOL_WINDOWS)

    def body(x_ref, w_ref, s_ref, dy_ref, dx_ref, dw_ref, ds_ref):
        def run(window):
            def chunk(r, carry):
                dw_acc, ds_acc = carry
                g = _chunk(x_ref, r, R)
                s = jnp.concatenate([_pre_halo(x_ref, r, R), g], axis=0)
                k = 1
                while k < window:
                    s = s + _down(s, k)
                    k *= 2
                pooled = (s[HALO:] / _window_count(r, R, R, window, 0) - g).astype(BF16)
                mixed = jnp.dot(pooled, w_ref[0], preferred_element_type=F32)
                dy_c = _chunk(dy_ref, r, R)
                dm_ext = (jnp.concatenate([dy_c, _post_halo(dy_ref, r, R, n_chunks)], axis=0)
                          * s_ref[...]).astype(BF16)
                dpool_ext = lax.dot_general(dm_ext, w_ref[0], (((1,), (1,)), ((), ())),
                                            preferred_element_type=F32)
                a = dpool_ext / _window_count(r, R, R + HALO, window, 0)
                k = 1
                while k < window:
                    a = a + _up(a, k)
                    k *= 2
                dx_ref[pl.ds(pl.multiple_of(r * R, 8), R), :] = (a[:R] - dpool_ext[:R]).astype(BF16)
                dw_acc = dw_acc + lax.dot_general(pooled, dm_ext[:R], (((0,), (0,)), ((), ())),
                                                  preferred_element_type=F32)
                ds_acc = ds_acc + jnp.sum(dy_c * mixed, axis=0, keepdims=True)
                return dw_acc, ds_acc

            dw_acc, ds_acc = lax.fori_loop(0, n_chunks, chunk,
                                           (jnp.zeros((pg, pg), F32), jnp.zeros((1, pg), F32)))
            dw_ref[0] = dw_acc
            ds_ref[...] = ds_acc

        for gi, window in enumerate(POOL_WINDOWS):
            pl.when(pl.program_id(0) == gi)(functools.partial(run, window))

    own = pl.BlockSpec((T, pg), lambda g: (0, g))
    return pl.pallas_call(
        body, name=name, grid=(n_groups,),
        in_specs=[pl.BlockSpec((T, pg), lambda g: (0, seg0 + g)),
                  pl.BlockSpec((1, pg, pg), lambda g: (g, 0, 0)),
                  pl.BlockSpec((1, pg), lambda g: (0, g)), own],
        out_specs=(own, pl.BlockSpec((1, pg, pg), lambda g: (g, 0, 0)), pl.BlockSpec((1, pg), lambda g: (0, g))),
        out_shape=(jax.ShapeDtypeStruct((T, n_groups * pg), BF16),
                   jax.ShapeDtypeStruct((n_groups, pg, pg), F32),
                   jax.ShapeDtypeStruct((1, n_groups * pg), F32)),
        compiler_params=_params(("parallel",)),
    )(rest, pool_w, pool_scale, dy)


def _rope(r, cos_t, sin_t):
    return r * cos_t + pltpu.roll(r, LANES // 2, 1) * sin_t


def _rope_t(d, cos_t, sin_t):
    return d * cos_t + pltpu.roll(d * sin_t, LANES // 2, 1)


def _qk_fwd(q_raw, k_nope, rest, cos_t, sin_t, q_norm, k_norm, *, name, heads, kr_seg, tm=192):
    T = q_raw.shape[0]
    tm = _tile(T, tm, 16)

    def body(q_ref, kn_ref, kr_ref, c_ref, s_ref, gq_ref, gk_ref, qo_ref, ko_ref):
        cos_b, sin_b = c_ref[...], s_ref[...]
        kr = kr_ref[:, 0:LANES]
        kr_ss = jnp.sum(kr * kr, axis=-1, keepdims=True)
        gq, gk = gq_ref[...], gk_ref[...]
        for h in range(heads):
            lo = h * HEAD_PAD
            q = q_ref[:, lo:lo + HEAD_PAD]
            rq = lax.rsqrt(jnp.sum(q * q, axis=-1, keepdims=True) / QK_HEAD + EPS)
            qn = q * rq * gq
            qo_ref[:, lo:lo + LANES] = qn[:, :LANES].astype(BF16)
            qo_ref[:, lo + LANES:lo + HEAD_PAD] = _rope(qn[:, LANES:], cos_b, sin_b).astype(BF16)
            kn = kn_ref[:, h * LANES:(h + 1) * LANES]
            rk = lax.rsqrt((jnp.sum(kn * kn, axis=-1, keepdims=True) + kr_ss) / QK_HEAD + EPS)
            ko_ref[:, lo:lo + LANES] = (kn * rk * gk[:, :LANES]).astype(BF16)
            ko_ref[:, lo + LANES:lo + HEAD_PAD] = _rope(kr * rk * gk[:, LANES:], cos_b, sin_b).astype(BF16)

    wq, wk = heads * HEAD_PAD, heads * LANES
    return pl.pallas_call(
        body, name=name, grid=(T // tm,),
        in_specs=[pl.BlockSpec((tm, wq), lambda i: (i, 0)), pl.BlockSpec((tm, wk), lambda i: (i, 0)),
                  pl.BlockSpec((tm, HEAD_PAD), lambda i: (i, kr_seg)),
                  pl.BlockSpec((tm, LANES), lambda i: (i, 0)), pl.BlockSpec((tm, LANES), lambda i: (i, 0)),
                  pl.BlockSpec((1, HEAD_PAD), lambda i: (0, 0)), pl.BlockSpec((1, HEAD_PAD), lambda i: (0, 0))],
        out_specs=(pl.BlockSpec((tm, wq), lambda i: (i, 0)), pl.BlockSpec((tm, wq), lambda i: (i, 0))),
        out_shape=(jax.ShapeDtypeStruct((T, wq), BF16), jax.ShapeDtypeStruct((T, wq), BF16)),
        compiler_params=_params(("parallel",)),
    )(q_raw, k_nope, rest, cos_t, sin_t, q_norm, k_norm)


def _qk_bwd(dq, dk, q_raw, k_nope, rest, cos_t, sin_t, q_norm, k_norm, *, name, heads, kr_seg, tm=128):
    T = q_raw.shape[0]
    tm = _tile(T, tm, 16)

    def body(dq_ref, dk_ref, q_ref, kn_ref, kr_ref, c_ref, s_ref, gq_ref, gk_ref,
             dqr_ref, dkn_ref, dkr_ref, dgq_ref, dgk_ref):
        cos_b, sin_b = c_ref[...], s_ref[...]
        kr = kr_ref[:, 0:LANES]
        kr_ss = jnp.sum(kr * kr, axis=-1, keepdims=True)
        gq, gk = gq_ref[...], gk_ref[...]
        dgq = jnp.zeros((1, HEAD_PAD), F32)
        dgk_n = jnp.zeros((1, LANES), F32)
        dgk_r = jnp.zeros((1, LANES), F32)
        dkr = jnp.zeros((tm, LANES), F32)
        for h in range(heads):
            lo = h * HEAD_PAD
            q = q_ref[:, lo:lo + HEAD_PAD]
            rq = lax.rsqrt(jnp.sum(q * q, axis=-1, keepdims=True) / QK_HEAD + EPS)
            qhat = q * rq
            dqn = jnp.concatenate([dq_ref[:, lo:lo + LANES],
                                   _rope_t(dq_ref[:, lo + LANES:lo + HEAD_PAD], cos_b, sin_b)], axis=1)
            dgq = dgq + jnp.sum(dqn * qhat, axis=0, keepdims=True)
            dqh = dqn * gq
            dqr_ref[:, lo:lo + HEAD_PAD] = (
                rq * (dqh - qhat * (jnp.sum(dqh * qhat, axis=-1, keepdims=True) / QK_HEAD))).astype(BF16)
            kn = kn_ref[:, h * LANES:(h + 1) * LANES]
            rk = lax.rsqrt((jnp.sum(kn * kn, axis=-1, keepdims=True) + kr_ss) / QK_HEAD + EPS)
            khat_n, khat_r = kn * rk, kr * rk
            dkn_n = dk_ref[:, lo:lo + LANES]
            dkn_r = _rope_t(dk_ref[:, lo + LANES:lo + HEAD_PAD], cos_b, sin_b)
            dgk_n = dgk_n + jnp.sum(dkn_n * khat_n, axis=0, keepdims=True)
            dgk_r = dgk_r + jnp.sum(dkn_r * khat_r, axis=0, keepdims=True)
            dkh_n, dkh_r = dkn_n * gk[:, :LANES], dkn_r * gk[:, LANES:]
            proj = (jnp.sum(dkh_n * khat_n, axis=-1, keepdims=True)
                    + jnp.sum(dkh_r * khat_r, axis=-1, keepdims=True)) / QK_HEAD
            dkn_ref[:, h * LANES:(h + 1) * LANES] = (rk * (dkh_n - khat_n * proj)).astype(BF16)
            dkr = dkr + rk * (dkh_r - khat_r * proj)
        dkr_ref[:, 0:LANES] = dkr.astype(BF16)
        dkr_ref[:, LANES:HEAD_PAD] = jnp.zeros((tm, HEAD_PAD - LANES), BF16)
        dgk = jnp.concatenate([dgk_n, dgk_r], axis=1)

        @pl.when(pl.program_id(0) == 0)
        def _():
            dgq_ref[...] = dgq
            dgk_ref[...] = dgk

        @pl.when(pl.program_id(0) > 0)
        def _():
            dgq_ref[...] += dgq
            dgk_ref[...] += dgk

    wq, wk = heads * HEAD_PAD, heads * LANES
    row = lambda w: pl.BlockSpec((tm, w), lambda i: (i, 0))
    vec = pl.BlockSpec((1, HEAD_PAD), lambda i: (0, 0))
    return pl.pallas_call(
        body, name=name, grid=(T // tm,),
        in_specs=[row(wq), row(wq), row(wq), row(wk), pl.BlockSpec((tm, HEAD_PAD), lambda i: (i, kr_seg)),
                  row(LANES), row(LANES), vec, vec],
        out_specs=(row(wq), row(wk), row(HEAD_PAD), vec, vec),
        out_shape=(jax.ShapeDtypeStruct((T, wq), BF16), jax.ShapeDtypeStruct((T, wk), BF16),
                   jax.ShapeDtypeStruct((T, HEAD_PAD), BF16),
                   jax.ShapeDtypeStruct((1, HEAD_PAD), F32), jax.ShapeDtypeStruct((1, HEAD_PAD), F32)),
        compiler_params=_params(("arbitrary",)),
    )(dq, dk, q_raw, k_nope, rest, cos_t, sin_t, q_norm, k_norm)


def _causal_mask(s):
    row = lax.broadcasted_iota(jnp.int32, s.shape, 0)
    col = lax.broadcasted_iota(jnp.int32, s.shape, 1)
    return jnp.where(row >= col, s, NEG)


def _flash_fwd(q, k, v, *, name, heads, tq=384, hp=2):
    T = q.shape[0]
    tq = _tile(T, tq, LANES)
    nq = T // tq
    scale = QK_HEAD ** -0.5
    nt = (((1,), (1,)), ((), ()))

    def body(q_ref, k_ref, v_ref, o_ref, lse_ref):
        def q_block(i, carry):
            q_at = pl.ds(pl.multiple_of(i * tq, tq), tq)
            qbs = [q_ref[q_at, h * HEAD_PAD:(h + 1) * HEAD_PAD] for h in range(hp)]

            def step(j, state, masked):
                k_at = pl.ds(pl.multiple_of(j * tq, tq), tq)
                new = []
                for h in range(hp):
                    m, l, acc = state[h]
                    s = lax.dot_general(qbs[h], k_ref[k_at, h * HEAD_PAD:(h + 1) * HEAD_PAD], nt,
                                        preferred_element_type=F32) * scale
                    if masked:
                        s = _causal_mask(s)
                    m_new = jnp.maximum(m, jnp.max(s, axis=-1, keepdims=True))
                    p = jnp.exp(s - m_new)
                    alpha = jnp.exp(m - m_new)
                    l = alpha * l + jnp.sum(p, axis=-1, keepdims=True)
                    acc = alpha * acc + jnp.dot(p.astype(BF16), v_ref[k_at, h * V_HEAD:(h + 1) * V_HEAD],
                                                preferred_element_type=F32)
                    new.append((m_new, l, acc))
                return tuple(new)

            init = tuple((jnp.full((tq, 1), NEG, F32), jnp.zeros((tq, 1), F32), jnp.zeros((tq, V_HEAD), F32))
                         for _ in range(hp))
            state = lax.fori_loop(0, i, lambda j, st: step(j, st, False), init)
            state = step(i, state, True)
            for h in range(hp):
                m, l, acc = state[h]
                o_ref[q_at, h * V_HEAD:(h + 1) * V_HEAD] = (acc / l).astype(BF16)
                lse_ref[h, q_at, :] = jnp.broadcast_to(m + jnp.log(l), (tq, LANES))
            return carry

        lax.fori_loop(0, nq, q_block, 0)

    qk_spec = pl.BlockSpec((T, hp * HEAD_PAD), lambda g: (0, g))
    v_spec = pl.BlockSpec((T, hp * V_HEAD), lambda g: (0, g))
    return pl.pallas_call(
        body, name=name, grid=(heads // hp,), in_specs=[qk_spec, qk_spec, v_spec],
        out_specs=(v_spec, pl.BlockSpec((hp, T, LANES), lambda g: (g, 0, 0))),
        out_shape=(jax.ShapeDtypeStruct((T, heads * V_HEAD), BF16), jax.ShapeDtypeStruct((heads, T, LANES), F32)),
        compiler_params=_params(("parallel",)),
    )(q, k, v)


def _flash_bwd(q, k, v, o, do, lse, *, name, heads, tq=384):
    T = q.shape[0]
    tq = _tile(T, tq, LANES)
    nq = T // tq
    scale = QK_HEAD ** -0.5
    nt = (((1,), (1,)), ((), ()))
    tn = (((0,), (0,)), ((), ()))

    def body(q_ref, k_ref, v_ref, o_ref, do_ref, lse_ref, dq_ref, dk_ref, dv_ref, delta_ref):
        def fill_delta(i, carry):
            at = pl.ds(pl.multiple_of(i * tq, tq), tq)
            d = jnp.sum(o_ref[at, :].astype(F32) * do_ref[at, :].astype(F32), axis=-1, keepdims=True)
            delta_ref[at, :] = jnp.broadcast_to(d, (tq, LANES))
            dq_ref[at, :] = jnp.zeros((tq, HEAD_PAD), F32)
            return carry

        lax.fori_loop(0, nq, fill_delta, 0)

        def kv_block(j, carry):
            k_at = pl.ds(pl.multiple_of(j * tq, tq), tq)
            kb, vb = k_ref[k_at, :], v_ref[k_at, :]

            def step(i, state, masked):
                dk_acc, dv_acc = state
                q_at = pl.ds(pl.multiple_of(i * tq, tq), tq)
                qb, dob = q_ref[q_at, :], do_ref[q_at, :]
                s = lax.dot_general(qb, kb, nt, preferred_element_type=F32) * scale
                if masked:
                    s = _causal_mask(s)
                p = jnp.exp(s - lse_ref[0, q_at, :][:, 0:1])
                dv_acc = dv_acc + lax.dot_general(p.astype(BF16), dob, tn, preferred_element_type=F32)
                dp = lax.dot_general(dob, vb, nt, preferred_element_type=F32)
                ds = (p * (dp - delta_ref[q_at, :][:, 0:1]) * scale).astype(BF16)
                dk_acc = dk_acc + lax.dot_general(ds, qb, tn, preferred_element_type=F32)
                dq_ref[q_at, :] += jnp.dot(ds, kb, preferred_element_type=F32)
                return dk_acc, dv_acc

            state = step(j, (jnp.zeros((tq, HEAD_PAD), F32), jnp.zeros((tq, V_HEAD), F32)), True)
            rest = nq - 1 - j

            def two_steps(t, st):
                i0 = j + 1 + 2 * t
                return step(i0 + 1, step(i0, st, False), False)

            state = lax.fori_loop(0, rest // 2, two_steps, state)
            dk_acc, dv_acc = lax.cond(rest % 2 == 1, lambda st: step(nq - 1, st, False), lambda st: st, state)
            dk_ref[k_at, :] = dk_acc
            dv_ref[k_at, :] = dv_acc.astype(BF16)
            return carry

        lax.fori_loop(0, nq, kv_block, 0)

    qk_spec = pl.BlockSpec((T, HEAD_PAD), lambda h: (0, h))
    v_spec = pl.BlockSpec((T, V_HEAD), lambda h: (0, h))
    return pl.pallas_call(
        body, name=name, grid=(heads,),
        in_specs=[qk_spec, qk_spec, v_spec, v_spec, v_spec, pl.BlockSpec((1, T, LANES), lambda h: (h, 0, 0))],
        out_specs=(qk_spec, qk_spec, v_spec),
        out_shape=(jax.ShapeDtypeStruct((T, heads * HEAD_PAD), F32), jax.ShapeDtypeStruct((T, heads * HEAD_PAD), F32),
                   jax.ShapeDtypeStruct((T, heads * V_HEAD), BF16)),
        scratch_shapes=[pltpu.VMEM((T, LANES), F32)],
        compiler_params=_params(("parallel",)),
    )(q, k, v, o, do, lse)


def _merge_fwd(gl, pa, pb, pc, *, name, d, tm=384, tn=1024):
    T = pa.shape[0]
    tm, tn = _tile(T, tm, 16), _tile(d, tn)
    nb = d // tn

    def body(g0, g1, g2, a, b, c, o_ref):
        o_ref[...] = (jax.nn.sigmoid(g0[...]) * a[...] + jax.nn.sigmoid(g1[...]) * b[...]
                      + jax.nn.sigmoid(g2[...]) * c[...]).astype(BF16)

    gate = lambda n: pl.BlockSpec((tm, tn), lambda i, j: (i, n * nb + j))
    blk = pl.BlockSpec((tm, tn), lambda i, j: (i, j))
    return pl.pallas_call(
        body, name=name, grid=(T // tm, nb), in_specs=[gate(0), gate(1), gate(2), blk, blk, blk],
        out_specs=blk, out_shape=jax.ShapeDtypeStruct((T, d), BF16),
        compiler_params=_params(("parallel", "parallel")),
    )(gl, gl, gl, pa, pb, pc)


def _merge_bwd(dm, gl, pa, pb, pc, *, name, d, tm=384, tn=1024):
    T = pa.shape[0]
    tm, tn = _tile(T, tm, 16), _tile(d, tn)
    nb = d // tn

    def body(dm_ref, g0, g1, g2, a, b, c, da, db, dc, dg0, dg1, dg2):
        dmv = dm_ref[...]
        for g_ref, p_ref, dp_ref, dg_ref in ((g0, a, da, dg0), (g1, b, db, dg1), (g2, c, dc, dg2)):
            sg = jax.nn.sigmoid(g_ref[...])
            dp_ref[...] = (dmv * sg).astype(BF16)
            dg_ref[...] = (dmv * p_ref[...] * sg * (1.0 - sg)).astype(BF16)

    gate = lambda n: pl.BlockSpec((tm, tn), lambda i, j: (i, n * nb + j))
    blk = pl.BlockSpec((tm, tn), lambda i, j: (i, j))
    return pl.pallas_call(
        body, name=name, grid=(T // tm, nb), in_specs=[blk, gate(0), gate(1), gate(2), blk, blk, blk],
        out_specs=(blk,) * 6, out_shape=(jax.ShapeDtypeStruct((T, d), BF16),) * 6,
        compiler_params=_params(("parallel", "parallel")),
    )(dm, gl, gl, gl, pa, pb, pc)


def _loss(y, target, *, name, first, last, tm=384):
    T, d = y.shape
    tm = _tile(T, tm, 16)

    def body(y_ref, t_ref, loss_ref, dy_ref):
        i = pl.program_id(0)
        row = lax.broadcasted_iota(jnp.int32, (tm, 1), 0) + i * tm
        real = jnp.logical_and(row >= first, row < last)
        err = jnp.where(real, y_ref[...] - t_ref[...], 0.0)
        dy_ref[...] = err * (1.0 / d)
        part = jnp.broadcast_to(jnp.sum(err * err, keepdims=True).reshape(1, 1), (1, LANES))

        @pl.when(i == 0)
        def _():
            loss_ref[...] = part

        @pl.when(i > 0)
        def _():
            loss_ref[...] += part

    blk = pl.BlockSpec((tm, d), lambda i: (i, 0))
    return pl.pallas_call(
        body, name=name, grid=(T // tm,), in_specs=[blk, blk],
        out_specs=(pl.BlockSpec((1, LANES), lambda i: (0, 0)), blk),
        out_shape=(jax.ShapeDtypeStruct((1, LANES), F32), jax.ShapeDtypeStruct((T, d), F32)),
        compiler_params=_params(("arbitrary",)),
    )(y, target)


def _as3d(a):
    return a.reshape(a.shape[0], -1, a.shape[-1])


def _sum_stack(parts, *, name, out_dtype, rows=256):
    n, R, C = parts.shape
    tr = _tile(R, rows, 16)

    def body(p_ref, o_ref):
        acc = p_ref[0].astype(F32)
        for s in range(1, n):
            acc = acc + p_ref[s].astype(F32)
        o_ref[...] = acc.astype(out_dtype)

    return pl.pallas_call(
        body, name=name, grid=(R // tr,),
        in_specs=[pl.BlockSpec((n, tr, C), lambda i: (0, i, 0))],
        out_specs=pl.BlockSpec((tr, C), lambda i: (i, 0)),
        out_shape=jax.ShapeDtypeStruct((R, C), out_dtype),
        compiler_params=_params(("parallel",)),
    )(parts)


def _adamw(w, g, m, v, *, name, rows=128):
    R, C = w.shape
    tr = _tile(R, rows, 8)
    c1 = 1.0 - ADAM_B1 ** ADAM_STEP
    c2 = 1.0 - ADAM_B2 ** ADAM_STEP

    def body(w_ref, g_ref, m_ref, v_ref, d_ref, nm_ref, nv_ref):
        gv = g_ref[...]
        nm = ADAM_B1 * m_ref[...] + (1.0 - ADAM_B1) * gv
        nv = ADAM_B2 * v_ref[...] + (1.0 - ADAM_B2) * (gv * gv)
        nm_ref[...] = nm
        nv_ref[...] = nv
        d_ref[...] = -ADAM_LR * ((nm / c1) / (jnp.sqrt(nv / c2) + ADAM_EPS) + ADAM_WD * w_ref[...])

    blk = pl.BlockSpec((tr, C), lambda i: (i, 0))
    return pl.pallas_call(
        body, name=name, grid=(R // tr,), in_specs=[blk] * 4, out_specs=(blk,) * 3,
        out_shape=(jax.ShapeDtypeStruct((R, C), F32),) * 3,
        compiler_params=_params(("parallel",)),
    )(w, g, m, v)


def _one_hot(index, n):
    return jnp.broadcast_to((jnp.arange(n) == index).astype(F32)[:, None, None], (n, 8, LANES))


def _is_set(flags_ref, s):
    return flags_ref[s, 0:1, 0:1] > 0.5


def _pair_sum(pieces, recv, core, *, name, rows=256):
    _, H, C = recv.shape
    tr = _tile(H, rows, 16)
    nh = H // tr

    def body(lo_ref, hi_ref, r_ref, core_ref, o_ref):
        mine = jnp.where(_is_set(core_ref, 0), lo_ref[0], hi_ref[0])
        o_ref[0] = (mine.astype(F32) + r_ref[0].astype(F32)).astype(BF16)

    blk = pl.BlockSpec((1, tr, C), lambda j, i: (j, i, 0))
    return pl.pallas_call(
        body, name=name, grid=(4, nh),
        in_specs=[blk, pl.BlockSpec((1, tr, C), lambda j, i: (j, nh + i, 0)), blk,
                  pl.BlockSpec((2, 8, LANES), lambda j, i: (0, 0, 0))],
        out_specs=blk, out_shape=jax.ShapeDtypeStruct((4, H, C), BF16),
        compiler_params=_params(("parallel", "parallel")),
    )(pieces, pieces, recv, core)


def _chip_sum(pair, landed, chip_flags, *, name, rows=256):
    _, H, C = pair.shape
    tr = _tile(H, rows, 16)

    def body(p_ref, l_ref, chip_ref, o_ref):
        acc = None
        for s in range(4):
            part = jnp.where(_is_set(chip_ref, s), p_ref[s], l_ref[s]).astype(F32)
            acc = part if acc is None else acc + part
        o_ref[...] = acc

    blk = pl.BlockSpec((4, tr, C), lambda i: (0, i, 0))
    return pl.pallas_call(
        body, name=name, grid=(H // tr,),
        in_specs=[blk, blk, pl.BlockSpec((4, 8, LANES), lambda i: (0, 0, 0))],
        out_specs=pl.BlockSpec((tr, C), lambda i: (i, 0)), out_shape=jax.ShapeDtypeStruct((H, C), F32),
        compiler_params=_params(("parallel",)),
    )(pair, landed, chip_flags)


def _adamw_layer(w, m, v, total, recv, core, layer, prev, *, name, rows=128):
    _, R, C = w.shape
    H = R // 2
    tr = _tile(H, rows, 8)
    nh = H // tr
    c1 = 1.0 - ADAM_B1 ** ADAM_STEP
    c2 = 1.0 - ADAM_B2 ** ADAM_STEP
    n_prev = 0 if prev is None else 4

    def body(*refs):
        w_ref, m_ref, v_ref, t_ref, r_ref, core_ref = refs[:6]
        g_ref, d_ref, nm_ref, nv_ref = refs[6 + n_prev:]
        half_is_mine = jnp.where(pl.program_id(0) == 0, core_ref[0, 0:1, 0:1], core_ref[1, 0:1, 0:1]) > 0.5
        gv = jnp.where(half_is_mine, t_ref[...], r_ref[...])
        nm = ADAM_B1 * m_ref[0] + (1.0 - ADAM_B1) * gv
        nv = ADAM_B2 * v_ref[0] + (1.0 - ADAM_B2) * (gv * gv)
        g_ref[0] = gv
        nm_ref[0] = nm
        nv_ref[0] = nv
        d_ref[0] = -ADAM_LR * ((nm / c1) / (jnp.sqrt(nv / c2) + ADAM_EPS) + ADAM_WD * w_ref[0])

    lay = pl.BlockSpec((1, tr, C), lambda hf, i: (layer, hf * nh + i, 0))
    one = pl.BlockSpec((tr, C), lambda hf, i: (i, 0))
    operands = [w, m, v, total, recv, core] + ([] if prev is None else list(prev))
    return pl.pallas_call(
        body, name=name, grid=(2, nh),
        in_specs=[lay, lay, lay, one, one, pl.BlockSpec((2, 8, LANES), lambda hf, i: (0, 0, 0))] + [ANY] * n_prev,
        out_specs=(lay,) * 4, out_shape=(jax.ShapeDtypeStruct((2, R, C), F32),) * 4,
        input_output_aliases={6 + i: i for i in range(n_prev)},
        compiler_params=_params(("parallel", "parallel")),
    )(*operands)


ANY = pl.BlockSpec(memory_space=pl.ANY)


def _coords():
    return lax.axis_index("x"), lax.axis_index("y"), lax.axis_index("c")


HBM = pl.BlockSpec(memory_space=pltpu.HBM)
SEM = pl.BlockSpec(memory_space=pltpu.SEMAPHORE)
EFFECT = pltpu.SideEffectType.DATAFLOW_SIDE_EFFECTING


def _copies(plan, bufs, send_sems, recv_sems):
    return [pltpu.make_async_remote_copy(src_ref=s, dst_ref=d, send_sem=send_sems.at[i], recv_sem=recv_sems.at[i],
                                         device_id=to, device_id_type=MESH)
            for i, (s, d, to) in enumerate(plan(bufs))]


def _start_copies(bufs, groups, *, name):
    nb, ng = len(bufs), len(groups)

    def body(*refs):
        buf_refs = refs[:nb]
        sems = refs[nb:nb + 2 * ng]
        token = refs[-1]
        for g, (plan, _) in enumerate(groups):
            for cp in _copies(plan, buf_refs, sems[2 * g], sems[2 * g + 1]):
                cp.start()
        token[...] = jnp.zeros_like(token)

    sem_shapes = []
    for _, n in groups:
        sem_shapes += [pltpu.SemaphoreType.DMA((n,)), pltpu.SemaphoreType.DMA((n,))]
    out = pl.pallas_call(
        body, name=name, in_specs=[HBM] * nb,
        out_specs=tuple([SEM] * (2 * ng) + [HBM] * nb + [pl.BlockSpec(memory_space=pltpu.VMEM)]),
        out_shape=tuple(sem_shapes + [pltpu.HBM(b.shape, b.dtype) for b in bufs] + [jax.ShapeDtypeStruct((8, LANES), F32)]),
        input_output_aliases={i: 2 * ng + i for i in range(nb)},
        compiler_params=pltpu.CompilerParams(has_side_effects=EFFECT),
    )(*[pltpu.with_memory_space_constraint(b, pltpu.HBM) for b in bufs])
    sems = [(out[2 * g], out[2 * g + 1]) for g in range(ng)]
    return sems, list(out[2 * ng:2 * ng + nb]), out[-1]


def _wait_copies(bufs, sems, plan, after, *, name):
    nb = len(bufs)

    def body(*refs):
        buf_refs = refs[:nb]
        for cp in _copies(plan, buf_refs, refs[nb], refs[nb + 1]):
            cp.wait_send()
            cp.wait_recv()

    out = pl.pallas_call(
        body, name=name, in_specs=[HBM] * nb + [SEM, SEM, ANY], out_specs=tuple([HBM] * nb),
        out_shape=tuple(pltpu.HBM(b.shape, b.dtype) for b in bufs),
        input_output_aliases={i: i for i in range(nb)},
        compiler_params=pltpu.CompilerParams(has_side_effects=EFFECT),
    )(*bufs, sems[0], sems[1], after)
    return list(out)


def _half(ref, c):
    h = ref.shape[0] // 2
    return ref.at[pl.ds(c * h, h)]


def _ici_gather_plan(pairs):
    def plan(refs):
        x, y, c = _coords()
        me = 2 * x + y
        out = []
        for s, d in pairs:
            for cx, cy in [(1 - x, y), (x, 1 - y), (1 - x, 1 - y)]:
                out.append((_half(refs[s], c), _half(refs[d].at[me], c), (cx, cy, c)))
            out.append((refs[s], refs[d].at[me], (x, y, 1 - c)))
        return out
    return plan, 4 * len(pairs)


def _d2d_forward_plan(lands):
    def plan(refs):
        x, y, c = _coords()
        out = []
        for d in lands:
            for cx, cy in [(1 - x, y), (x, 1 - y), (1 - x, 1 - y)]:
                got = _half(refs[d].at[2 * cx + cy], c)
                out.append((got, got, (x, y, 1 - c)))
        return out
    return plan, 3 * len(lands)


def _swap_half_plan(pairs):
    def plan(refs):
        x, y, c = _coords()
        out = []
        for s, d in pairs:
            h = refs[d].shape[1]
            out.append((refs[s].at[:, pl.ds((1 - c) * h, h)], refs[d], (x, y, 1 - c)))
        return out
    return plan, len(pairs)


def _scatter_plan(pairs):
    def plan(refs):
        x, y, c = _coords()
        me = 2 * x + y
        out = []
        for s, d in pairs:
            for cx, cy in [(1 - x, y), (x, 1 - y), (1 - x, 1 - y)]:
                out.append((refs[s].at[2 * cx + cy], refs[d].at[me], (cx, cy, c)))
        return out
    return plan, 3 * len(pairs)


def _swap_total_plan(pairs):
    def plan(refs):
        x, y, c = _coords()
        return [(refs[s], refs[d], (x, y, 1 - c)) for s, d in pairs]
    return plan, len(pairs)


def _gather_all(block, *, name):
    def body(src, out, send_sems, recv_sems, local_sem):
        x, y, c = _coords()
        me = 4 * x + 2 * y + c
        flips = [(fx, fy, fc) for fx in (0, 1) for fy in (0, 1) for fc in (0, 1)][1:]
        mine = pltpu.make_async_copy(src, out.at[me], local_sem)
        mine.start()
        peers = [(x ^ fx, y ^ fy, c ^ fc) for fx, fy, fc in flips]
        cps = [pltpu.make_async_remote_copy(src_ref=src, dst_ref=out.at[me], send_sem=send_sems.at[k],
                                            recv_sem=recv_sems.at[k], device_id=peer, device_id_type=MESH)
               for k, peer in enumerate(peers)]
        for cp in cps:
            cp.start()
        for k, (px, py, pc) in enumerate(peers):
            slot = out.at[4 * px + 2 * py + pc]
            pltpu.make_async_remote_copy(src_ref=slot, dst_ref=slot, send_sem=send_sems.at[k], recv_sem=recv_sems.at[k],
                                         device_id=(px, py, pc), device_id_type=MESH).wait_recv()
        for cp in cps:
            cp.wait_send()
        mine.wait()

    return pl.pallas_call(
        body, name=name, in_specs=[ANY], out_specs=ANY,
        out_shape=jax.ShapeDtypeStruct((8,) + block.shape, block.dtype),
        scratch_shapes=[pltpu.SemaphoreType.DMA((7,)), pltpu.SemaphoreType.DMA((7,)), pltpu.SemaphoreType.DMA],
    )(block)


def _cols(o):
    return jnp.transpose(o, (1, 0, 2)).reshape(o.shape[1], -1)


def _uncols(full):
    return jnp.transpose(full.reshape(full.shape[0], 4, -1), (1, 0, 2))


def _rope_pad(x1, x2):
    z = jnp.zeros_like(x1)
    return jnp.concatenate([x1, z, x2, z], axis=-1)


def _head_pad(w, heads):
    r = w.reshape(w.shape[0], heads, QK_HEAD)
    half = QK_ROPE // 2
    out = jnp.concatenate([r[..., :QK_NOPE], _rope_pad(r[..., QK_NOPE:QK_NOPE + half], r[..., QK_NOPE + half:])], axis=-1)
    return out.reshape(w.shape[0], heads * HEAD_PAD)


def _head_unpad(w, heads):
    r = w.reshape(w.shape[0], heads, HEAD_PAD)
    half = QK_ROPE // 2
    out = jnp.concatenate([r[..., :QK_NOPE], r[..., QK_NOPE:QK_NOPE + half],
                           r[..., QK_NOPE + 2 * half:QK_NOPE + 3 * half]], axis=-1)
    return out.reshape(w.shape[0], heads * QK_HEAD)


class _Dims:
    def __init__(self, d, seq):
        self.d = d
        self.seq = seq
        self.t_real = N_META + seq
        self.t = -(-self.t_real // LANES) * LANES
        self.dc = d // 2
        self.dp = d // 2
        self.pg = self.dp // len(POOL_WINDOWS)
        self.heads = d // 128
        self.dff = 4 * d
        self.a_end = 3 * self.dc
        self.q_end = self.a_end + Q_LORA
        self.kv_end = self.q_end + KV_LORA
        self.kr_end = self.kv_end + QK_ROPE
        self.pool_end = self.kr_end + self.dp
        self.d_in = self.pool_end + 3 * d
        self.r_pool = 3 * self.dc
        self.r_q = self.r_pool + self.dp
        self.r_kv = self.r_q + Q_LORA
        self.r_kr = self.r_kv + KV_LORA
        self.r_width = self.r_kr + HEAD_PAD


def _in_weights(dm, w_in_pieces):
    w_in = _cols(w_in_pieces)
    half = QK_ROPE // 2
    kr = w_in[:, dm.kv_end:dm.kr_end]
    kr_p = jnp.concatenate([_rope_pad(kr[:, :half], kr[:, half:]), jnp.zeros((dm.d, HEAD_PAD - LANES), BF16)], axis=1)
    return dict(
        wg=w_in[:, dm.pool_end:],
        wr=jnp.concatenate([w_in[:, :dm.a_end], w_in[:, dm.kr_end:dm.pool_end], w_in[:, dm.a_end:dm.kv_end], kr_p], axis=1))


def _other_weights(dm, g):
    out = {}
    if "w_ukv" in g:
        w_ukv = _cols(g["w_ukv"]).reshape(KV_LORA, dm.heads, QK_NOPE + V_HEAD)
        out["wkn"] = w_ukv[:, :, :QK_NOPE].reshape(KV_LORA, dm.heads * QK_NOPE)
        out["wv"] = w_ukv[:, :, QK_NOPE:].reshape(KV_LORA, dm.heads * V_HEAD)
    if "w_uq" in g:
        out["wuq"] = _head_pad(_cols(g["w_uq"]), dm.heads)
    if "pool_w" in g:
        out["wp"] = jnp.transpose(g["pool_w"], (1, 0, 2, 3)).reshape(len(POOL_WINDOWS), dm.pg, dm.pg)
    for name, key in (("w_branch_a", "wba"), ("w_branch_c", "wbc"), ("w_up", "wup")):
        if name in g:
            out[key] = _cols(g[name])
    for name, key in (("w_branch_b", "wbb"), ("w_o", "wo"), ("w_down", "wdn")):
        if name in g:
            out[key] = g[name].reshape(-1, dm.d)
    return out


def _small_weights(small):
    return dict(
        conv_w=small["conv_w"],
        attn_norm=small["attn_norm"][None], mlp_norm=small["mlp_norm"][None],
        q_lat_norm=small["q_lat_norm"][None], kv_lat_norm=small["kv_lat_norm"][None],
        q_norm=_head_pad(small["q_norm"][None], 1), k_norm=_head_pad(small["k_norm"][None], 1),
        pool_scale=small["pool_scale"][None],
    )


def _grad_piece(dm, dw, name):
    half = QK_ROPE // 2
    rows = lambda a: a.reshape((4, a.shape[0] // 4) + a.shape[1:])
    if name == "w_in":
        dwr, dwg = dw["wr"], dw["wg"]
        out = _uncols(jnp.concatenate([
            dwr[:, :dm.r_pool], dwr[:, dm.r_q:dm.r_kr], dwr[:, dm.r_kr:dm.r_kr + half],
            dwr[:, dm.r_kr + 2 * half:dm.r_kr + 3 * half], dwr[:, dm.r_pool:dm.r_q], dwg], axis=1))
    elif name == "w_ukv":
        out = _uncols(jnp.concatenate([dw["wkn"].reshape(KV_LORA, dm.heads, QK_NOPE),
                                       dw["wv"].reshape(KV_LORA, dm.heads, V_HEAD)], axis=-1).reshape(KV_LORA, -1))
    elif name == "w_uq":
        out = _uncols(_head_unpad(dw["wuq"], dm.heads))
    elif name == "pool_w":
        out = jnp.transpose(dw["wp"].reshape(len(POOL_WINDOWS), 4, dm.pg // 4, dm.pg), (1, 0, 2, 3))
    elif name in ("w_branch_a", "w_branch_c", "w_up"):
        out = _uncols(dw[{"w_branch_a": "wba", "w_branch_c": "wbc", "w_up": "wup"}[name]])
    else:
        out = rows(dw[{"w_branch_b": "wbb", "w_o": "wo", "w_down": "wdn"}[name]])
    return out.astype(BF16)


def _layer_fwd(dm, W, x, cos_t, sin_t, tag, more=None):
    n = lambda s: f"{s}_{tag}"
    h = _rms_fwd(x, W["attn_norm"], name=n("attn_norm"))
    gl = _mm(h, W["wg"], name=n("proj_gates"))
    rest = _mm(h, W["wr"], name=n("proj_rest"))
    if more is not None:
        W.update(more("after_proj", rest))
    y_a = _conv_fwd(rest, W["conv_w"], name=n("conv"), dc=dm.dc)
    y_c = _pool_fwd(rest, W["wp"], W["pool_scale"], name=n("pool"), seg0=dm.r_pool // dm.pg, pg=dm.pg)
    q_lat = _rms_fwd(rest, W["q_lat_norm"], name=n("q_lat_norm"), width=Q_LORA, seg=dm.r_q // Q_LORA)
    kv_lat = _rms_fwd(rest, W["kv_lat_norm"], name=n("kv_lat_norm"), width=KV_LORA, seg=dm.r_kv // KV_LORA)
    q_raw = _mm(q_lat, W["wuq"], name=n("up_q"))
    k_nope = _mm(kv_lat, W["wkn"], name=n("up_k"))
    v = _mm(kv_lat, W["wv"], name=n("up_v"), out_dtype=BF16)
    q, k = _qk_fwd(q_raw, k_nope, rest, cos_t, sin_t, W["q_norm"], W["k_norm"], name=n("qk_norm_rope"),
                   heads=dm.heads, kr_seg=dm.r_kr // HEAD_PAD)
    if more is not None:
        W.update(more("after_qk", q))
    y_b, lse = _flash_fwd(q, k, v, name=n("attention"), heads=dm.heads)
    pa = _mm(y_a, W["wba"], name=n("branch_a"))
    pb = _mm(y_b, W["wbb"], name=n("branch_b"))
    pc = _mm(y_c, W["wbc"], name=n("branch_c"))
    merged = _merge_fwd(gl, pa, pb, pc, name=n("merge"), d=dm.d)
    x1 = _mm(merged, W["wo"], name=n("out_proj"), add=x)
    h2 = _rms_fwd(x1, W["mlp_norm"], name=n("mlp_norm"))
    up, act = _mm(h2, W["wup"], name=n("mlp_up"), epi="relu2")
    x2 = _mm(act, W["wdn"], name=n("mlp_down"), add=x1, tk=2048)
    saved = dict(x=x, h=h, gl=gl, rest=rest, y_a=y_a, y_c=y_c, q_lat=q_lat, kv_lat=kv_lat, q_raw=q_raw, k_nope=k_nope,
                 v=v, q=q, k=k, y_b=y_b, lse=lse, pa=pa, pb=pb, pc=pc, merged=merged, x1=x1, h2=h2, up=up, act=act)
    return x2, saved


def _layer_bwd(dm, W, S, dx2, cos_t, sin_t, tag, hook=None):
    n = lambda s: f"{s}_{tag}"
    dw, ds = {}, {}
    if hook is None:
        hook = lambda point, t, dw_so_far: ()
    dup = _mm(dx2, W["wdn"], name=n("d_mlp_down"), tb=True, aux=S["up"], epi="drelu2", out_dtype=BF16,
              after=hook("start", dx2, dw))
    dw["wdn"] = _mm(S["act"], dx2, name=n("dw_mlp_down"), ta=True, tm=1024, tk=1408)
    dh2 = _mm(dup, W["wup"], name=n("d_mlp_up"), tb=True, tk=2048)
    dw["wup"] = _mm(S["h2"], dup, name=n("dw_mlp_up"), ta=True, tm=1024, tk=1408)
    dx1, ds["mlp_norm"] = _rms_bwd(dh2, S["x1"], W["mlp_norm"], name=n("d_mlp_norm"), res=dx2)
    dmerged = _mm(dx1, W["wo"], name=n("d_out_proj"), tb=True, after=hook("after_mlp", dx1, dw))
    dw["wo"] = _mm(S["merged"], dx1, name=n("dw_out_proj"), ta=True, tm=1024, tk=1408)
    dpa, dpb, dpc, dg0, dg1, dg2 = _merge_bwd(dmerged, S["gl"], S["pa"], S["pb"], S["pc"], name=n("d_merge"), d=dm.d)
    dgl = jnp.concatenate([dg0, dg1, dg2], axis=1)
    dy_a = _mm(dpa, W["wba"], name=n("d_branch_a"), tb=True)
    dw["wba"] = _mm(S["y_a"], dpa, name=n("dw_branch_a"), ta=True, tm=1024, tk=1408)
    dy_b = _mm(dpb, W["wbb"], name=n("d_branch_b"), tb=True, out_dtype=BF16)
    dw["wbb"] = _mm(S["y_b"], dpb, name=n("dw_branch_b"), ta=True, tm=1024, tk=1408)
    dy_c = _mm(dpc, W["wbc"], name=n("d_branch_c"), tb=True)
    dw["wbc"] = _mm(S["y_c"], dpc, name=n("dw_branch_c"), ta=True, tm=1024, tk=1408)
    dq, dk, dv = _flash_bwd(S["q"], S["k"], S["v"], S["y_b"], dy_b, S["lse"], name=n("d_attention"), heads=dm.heads)
    after_attention = hook("after_attention", dq, dw)
    dq_raw, dk_nope, dk_rope, dgq, dgk = _qk_bwd(
        dq, dk, S["q_raw"], S["k_nope"], S["rest"], cos_t, sin_t, W["q_norm"], W["k_norm"], name=n("d_qk_norm_rope"),
        heads=dm.heads, kr_seg=dm.r_kr // HEAD_PAD)
    ds["q_norm"] = _head_unpad(dgq, 1)
    ds["k_norm"] = _head_unpad(dgk, 1)
    dkv_v = _mm(dv, W["wv"], name=n("d_up_v"), tb=True, after=after_attention)
    dq_lat_n = _mm(dq_raw, W["wuq"], name=n("d_up_q"), tb=True, tk=2048, after=hook("after_qk", dq_raw, dw))
    dw["wuq"] = _mm(S["q_lat"], dq_raw, name=n("dw_up_q"), ta=True, tm=512, tk=1408)
    dkv_lat_n = _mm(dk_nope, W["wkn"], name=n("d_up_k"), tb=True, add=dkv_v)
    dw["wkn"] = _mm(S["kv_lat"], dk_nope, name=n("dw_up_k"), ta=True, tm=512, tk=1408)
    dw["wv"] = _mm(S["kv_lat"], dv, name=n("dw_up_v"), ta=True, tm=512, tk=1408)
    dq_lat, ds["q_lat_norm"] = _rms_bwd(dq_lat_n, S["rest"], W["q_lat_norm"], name=n("d_q_lat_norm"), width=Q_LORA,
                                        seg=dm.r_q // Q_LORA, out_dtype=BF16)
    dkv_lat, ds["kv_lat_norm"] = _rms_bwd(dkv_lat_n, S["rest"], W["kv_lat_norm"], name=n("d_kv_lat_norm"), width=KV_LORA,
                                          seg=dm.r_kv // KV_LORA, out_dtype=BF16)
    du, db, dc, ds["conv_w"] = _conv_bwd(S["rest"], W["conv_w"], dy_a, name=n("d_conv"), dc=dm.dc)
    dpool, dw["wp"], ds["pool_scale"] = _pool_bwd(S["rest"], W["wp"], W["pool_scale"], dy_c, name=n("d_pool"),
                                                  seg0=dm.r_pool // dm.pg, pg=dm.pg)
    drest = jnp.concatenate([du, db, dc, dpool, dq_lat, dkv_lat, dk_rope], axis=1)
    dw["wg"] = _mm(S["h"], dgl, name=n("dw_proj_gates"), ta=True, tm=1024, tk=1408)
    dw["wr"] = _mm(S["h"], drest, name=n("dw_proj_rest"), ta=True, tm=1024, tk=1408)
    dh_g = _mm(dgl, W["wg"], name=n("d_proj_gates"), tb=True, tk=2048, after=hook("after_dw_in", dw["wr"], dw))
    dh = _mm(drest, W["wr"], name=n("d_proj_rest"), tb=True, add=dh_g, tk=1792, after=hook("after_dh_gates", dh_g, dw))
    dx, ds["attn_norm"] = _rms_bwd(dh, S["x"], W["attn_norm"], name=n("d_attn_norm"), res=dx1)
    return dx, dw, ds


BIG = ("w_in", "w_uq", "w_ukv", "pool_w", "w_branch_a", "w_branch_b", "w_branch_c", "w_o", "w_up", "w_down")
REPLICATED = ("attn_norm", "q_lat_norm", "kv_lat_norm", "q_norm", "k_norm", "pool_scale", "mlp_norm")
WEIGHTS = ("meta_tokens", "attn_norm", "w_in", "conv_w", "q_lat_norm", "kv_lat_norm", "w_uq", "w_ukv", "q_norm",
           "k_norm", "pool_w", "pool_scale", "w_branch_a", "w_branch_b", "w_branch_c", "w_o", "mlp_norm", "w_up",
           "w_down")


def _pack(arrays):
    flat = jnp.concatenate([a.reshape(-1).astype(F32) for a in arrays])
    pad = (-flat.shape[0]) % (8 * LANES)
    return jnp.pad(flat, (0, pad)).reshape(-1, LANES)


def _unpack(flat, shapes):
    out, pos = [], 0
    flat = flat.reshape(-1)
    for shp in shapes:
        size = math.prod(shp)
        out.append(flat[pos:pos + size].reshape(shp))
        pos += size
    return out


def _update(w, g, m, v, name):
    shp = w.shape
    to2 = lambda a: a.reshape(-1, shp[-1])
    delta, nm, nv = _adamw(to2(w), to2(g), to2(m), to2(v), name=name)
    return delta.reshape(shp), nm.reshape(shp), nv.reshape(shp)


def _step(args):
    x = args["x"][0]
    seq, d = x.shape
    dm = _Dims(d, seq)
    xi, yi, ci = _coords()
    chip = 2 * xi + yi

    small_w = _gather_all(_pack([args["conv_w"], args["meta_tokens"]]), name="gather_small_weights")
    order = [(k, l) for l in range(2) for k in BIG]
    shards = {n: args[n[0]][n[1]].astype(BF16) for n in order}
    small_w, shards[order[0]] = lax.optimization_barrier((small_w, shards[order[0]]))
    lands = {n: lax.empty((4,) + shards[n].shape, BF16) for n in order}
    last = ("w_up", "w_down")
    group_names = [[("w_in", 0)], [(k, 0) for k in BIG[1:] if k not in last], [(k, 0) for k in last],
                   [(k, 1) for k in BIG]]
    at = {n: i for i, n in enumerate(order)}
    sems, thru, token = _start_copies(
        [shards[n] for n in order] + [lands[n] for n in order],
        [_ici_gather_plan([(at[n], len(order) + at[n]) for n in g]) for g in group_names], name="start_gather_ici")
    for i, n in enumerate(order):
        shards[n], lands[n] = thru[i], thru[len(order) + i]

    def finish_gather(g, after, tag):
        names = group_names[g]
        k = len(names)
        plan, _ = _ici_gather_plan([(i, k + i) for i in range(k)])
        got = _wait_copies([shards[n] for n in names] + [lands[n] for n in names], sems[g], plan, after,
                           name=f"wait_gather_ici_{tag}")
        for i, n in enumerate(names):
            shards[n] = got[i]
        fwd = _d2d_forward_plan(list(range(k)))
        sems2, bufs2, tok2 = _start_copies(got[k:], [fwd], name=f"start_gather_d2d_{tag}")
        return names, bufs2, sems2[0], fwd[0], tok2

    def land_gather(pending, after, tag):
        names, bufs2, sems2, plan, tok2 = pending
        done = _wait_copies(bufs2, sems2, plan, tok2 if after is None else after, name=f"wait_gather_d2d_{tag}")
        return {n[0]: buf for n, buf in zip(names, done)}

    conv_shape, meta_shape = args["conv_w"].shape, args["meta_tokens"].shape
    per_chip = [_unpack(small_w[2 * j], [conv_shape, meta_shape]) for j in range(4)]
    conv_full = jnp.concatenate([p[0] for p in per_chip], axis=-1)
    meta_full = jnp.concatenate([p[1] for p in per_chip], axis=-1)

    layers = []
    for l in range(2):
        small = {k: args[k][l] for k in REPLICATED}
        small["conv_w"] = conv_full[l]
        layers.append(_small_weights(small))
    layers[0].update(_in_weights(dm, land_gather(finish_gather(0, token, "l0_in"), None, "l0_in")["w_in"]))

    pos = jnp.arange(dm.t, dtype=F32)
    inv = ROPE_THETA ** (-jnp.arange(0, QK_ROPE, 2, dtype=F32) / QK_ROPE)
    ang = pos[:, None] * inv[None, :]
    cos_t = _rope_pad(jnp.cos(ang), jnp.cos(ang))
    sin_t = _rope_pad(-jnp.sin(ang), jnp.sin(ang))
    tail = jnp.zeros((dm.t - dm.t_real, d), F32)
    h0 = jnp.concatenate([meta_full, x, tail], axis=0)
    target = jnp.concatenate([jnp.zeros((N_META, d), F32), args["loss_target"][0], tail], axis=0)

    def rest_of_layer0(point, after):
        g, tag = (1, "l0_mid") if point == "after_proj" else (2, "l0_mlp")
        return _other_weights(dm, land_gather(finish_gather(g, after, tag), None, tag))

    h1, saved0 = _layer_fwd(dm, layers[0], h0, cos_t, sin_t, "l0", more=rest_of_layer0)
    g1 = land_gather(finish_gather(3, saved0["y_b"], "l1"), h1, "l1")
    layers[1].update(_in_weights(dm, g1["w_in"]))
    layers[1].update(_other_weights(dm, g1))
    h2, saved1 = _layer_fwd(dm, layers[1], h1, cos_t, sin_t, "l1")
    sq, dy = _loss(h2, target, name="loss_head", first=N_META, last=dm.t_real)
    loss = lax.psum(0.5 / d * sq[0, 0], ("x", "y", "c"))
    core, chip_flags = _one_hot(ci, 2), _one_hot(chip, 4)

    class Reduce:
        def __init__(self, names, dw, tag):
            self.names, self.tag, self.nb = names, tag, len(names)
            self.idx = [(i, self.nb + i) for i in range(self.nb)]
            parts = [_as3d(_grad_piece(dm, dw, k)) for k in names]
            recv = [lax.empty((4, p.shape[1] // 2, p.shape[2]), BF16) for p in parts]
            self.plan = _swap_half_plan(self.idx)
            self.sems, self.bufs, self.token = _start_copies(parts + recv, [self.plan], name=f"start_swap_{tag}")

        def _land(self, after, what):
            return _wait_copies(self.bufs, self.sems[0], self.plan[0], self.token if after is None else after,
                                name=f"wait_{what}_{self.tag}")

        def scatter(self, after=None):
            got = self._land(after, "swap")
            pairs = [_pair_sum(got[i], got[j], core, name=f"pair_sum_{k}_{self.tag}")
                     for (i, j), k in zip(self.idx, self.names)]
            self.plan = _scatter_plan(self.idx)
            self.sems, self.bufs, self.token = _start_copies(pairs + [lax.empty(p.shape, BF16) for p in pairs],
                                                             [self.plan], name=f"start_scatter_{self.tag}")
            return self.token

        def totals(self, after=None):
            got = self._land(after, "scatter")
            sums = [_chip_sum(got[i], got[j], chip_flags, name=f"chip_sum_{k}_{self.tag}")
                    for (i, j), k in zip(self.idx, self.names)]
            self.plan = _swap_total_plan(self.idx)
            self.sems, self.bufs, self.token = _start_copies(sums + [lax.empty(t.shape, F32) for t in sums],
                                                             [self.plan], name=f"start_swap_total_{self.tag}")
            return self.token

        def finish(self, after=None):
            got = self._land(after, "swap_total")
            return {k: (got[i], got[j]) for (i, j), k in zip(self.idx, self.names)}

    dh1, dw1, ds1 = _layer_bwd(dm, layers[1], saved1, dy, cos_t, sin_t, "l1")
    early = ("w_down", "w_up", "w_o", "w_branch_a", "w_branch_b", "w_branch_c")
    late = tuple(k for k in BIG if k not in early)
    stage = {}

    def during_layer0(point, t, dw):
        if point == "start":
            stage["l1"] = Reduce(BIG, dw1, "l1")
            return (stage["l1"].token,)
        if point == "after_mlp":
            return (stage["l1"].scatter(after=t),)
        if point == "after_attention":
            tok = stage["l1"].totals(after=t)
            stage["l0a"] = Reduce(early, dw, "l0a")
            return (tok, stage["l0a"].token)
        if point == "after_qk":
            stage["red1"] = stage["l1"].finish(after=t)
            return (stage["l0a"].scatter(after=t),)
        if point == "after_dw_in":
            tok = stage["l0a"].totals(after=t)
            stage["l0b"] = Reduce(late, dw, "l0b")
            return (tok, stage["l0b"].token)
        return (stage["l0b"].scatter(after=t),)

    dh0, dw0, ds0 = _layer_bwd(dm, layers[0], saved0, dh1, cos_t, sin_t, "l0", hook=during_layer0)
    grad_x = dh0[N_META:dm.t_real][None]
    stage["l0b"].totals(after=dh0)
    red1 = stage["red1"]
    red0 = {**stage["l0a"].finish(), **stage["l0b"].finish()}
    grads = {}

    small_names = REPLICATED + ("conv_w",)
    small_parts = [jnp.stack([ds0[k].reshape(ds0[k].shape[-2:] if k == "conv_w" else (-1,)),
                              ds1[k].reshape(ds1[k].shape[-2:] if k == "conv_w" else (-1,))]) for k in small_names]
    small_parts.append(dh0[:N_META])
    small_all = _gather_all(_pack(small_parts), name="gather_small_grads")
    small_sum = _sum_stack(small_all, name="sum_small_grads", out_dtype=F32)
    small_g = dict(zip(small_names + ("meta_tokens",), _unpack(small_sum, [p.shape for p in small_parts])))
    for k in REPLICATED:
        grads[k] = small_g[k]
    dcw = conv_shape[-1]
    grads["conv_w"] = lax.dynamic_slice_in_dim(small_g["conv_w"], chip * dcw, dcw, axis=2)
    dmeta = meta_shape[-1]
    grads["meta_tokens"] = lax.dynamic_slice_in_dim(small_g["meta_tokens"], chip * dmeta, dmeta, axis=1)

    delta, new_m, new_v = {}, {}, {}
    for k in WEIGHTS:
        shp = args[k].shape
        if k in BIG:
            wmv = [args[p + k].reshape(2, -1, shp[-1]) for p in ("", "m_", "v_")]
            out = _adamw_layer(*wmv, *red1[k], core, 1, None, name=f"adamw_{k}_l1")
            out = _adamw_layer(*wmv, *red0[k], core, 0, out, name=f"adamw_{k}_l0")
            grads[k], delta[k], new_m[k], new_v[k] = (o.reshape(shp) for o in out)
        else:
            grads[k] = grads[k].reshape(shp)
            delta[k], new_m[k], new_v[k] = _update(args[k], grads[k], args["m_" + k], args["v_" + k], f"adamw_{k}")
    return (loss, grad_x, *[grads[k] for k in WEIGHTS], *[delta[k] for k in WEIGHTS],
            *[new_m[k] for k in WEIGHTS], *[new_v[k] for k in WEIGHTS])


def kernel(x, meta_tokens, attn_norm, w_in, conv_w, q_lat_norm, kv_lat_norm, w_uq, w_ukv, q_norm, k_norm, pool_w, pool_scale, w_branch_a, w_branch_b, w_branch_c, w_o, mlp_norm, w_up, w_down, loss_target, m_meta_tokens, m_attn_norm, m_w_in, m_conv_w, m_q_lat_norm, m_kv_lat_norm, m_w_uq, m_w_ukv, m_q_norm, m_k_norm, m_pool_w, m_pool_scale, m_w_branch_a, m_w_branch_b, m_w_branch_c, m_w_o, m_mlp_norm, m_w_up, m_w_down, v_meta_tokens, v_attn_norm, v_w_in, v_conv_w, v_q_lat_norm, v_kv_lat_norm, v_w_uq, v_w_ukv, v_q_norm, v_k_norm, v_pool_w, v_pool_scale, v_w_branch_a, v_w_branch_b, v_w_branch_c, v_w_o, v_mlp_norm, v_w_up, v_w_down):
    return _step(dict(locals()))
```

```python
import functools
import math

import jax
import jax.numpy as jnp
from jax import lax
from jax.experimental import pallas as pl
from jax.experimental.pallas import tpu as pltpu

F32 = jnp.float32
BF16 = jnp.bfloat16
MESH = pl.DeviceIdType.MESH

EPS = 1e-6
N_META = 16
QK_NOPE = 128
QK_ROPE = 64
QK_HEAD = QK_NOPE + QK_ROPE
V_HEAD = 128
HEAD_PAD = 256
Q_LORA = 512
KV_LORA = 512
ROPE_THETA = 10000.0
POOL_WINDOWS = (2, 4, 8, 16)
HALO = 16
LANES = 128
ADAM_LR = 0.001
ADAM_B1 = 0.9
ADAM_B2 = 0.999
ADAM_EPS = 1e-08
ADAM_WD = 0.01
ADAM_STEP = 10
VMEM_LIMIT = 52 * 1024 * 1024
NEG = -1e30


def _tile(n, target, mult=LANES):
    best = None
    for t in range(mult, min(n, target) + 1, mult):
        if n % t == 0:
            best = t
    return n if best is None else best


def _params(sem=None):
    return pltpu.CompilerParams(dimension_semantics=sem, vmem_limit_bytes=VMEM_LIMIT)


def _mm(a, b, *, name, ta=False, tb=False, add=None, aux=None, epi=None, out_dtype=F32,
        tm=704, tn=1024, tk=None, after=()):
    if ta:
        K, M = a.shape
    else:
        M, K = a.shape
    if tb:
        N, kb = b.shape
    else:
        kb, N = b.shape
    assert K == kb, (a.shape, b.shape, ta, tb)
    tm = _tile(M, tm, LANES if ta else 16)
    tn = _tile(N, tn, LANES)
    tk = K if tk is None else _tile(K, tk, LANES if (not ta or tb) else 16)
    nk = K // tk
    grid = (M // tm, N // tn, nk)

    a_spec = pl.BlockSpec((tk, tm), lambda i, j, k: (k, i)) if ta else pl.BlockSpec((tm, tk), lambda i, j, k: (i, k))
    b_spec = pl.BlockSpec((tn, tk), lambda i, j, k: (j, k)) if tb else pl.BlockSpec((tk, tn), lambda i, j, k: (k, j))
    o_spec = pl.BlockSpec((tm, tn), lambda i, j, k: (i, j))
    in_specs = [a_spec, b_spec]
    operands = [a, b]
    if add is not None:
        in_specs.append(o_spec)
        operands.append(add)
    if aux is not None:
        in_specs.append(o_spec)
        operands.append(aux)
    after = tuple(after)
    in_specs += [pl.BlockSpec(memory_space=pl.ANY)] * len(after)
    operands += list(after)
    if epi == "relu2":
        out_shape = (jax.ShapeDtypeStruct((M, N), BF16), jax.ShapeDtypeStruct((M, N), BF16))
        out_specs = (o_spec, o_spec)
    else:
        out_shape = jax.ShapeDtypeStruct((M, N), out_dtype)
        out_specs = o_spec
    dims = (((0 if ta else 1,), (1 if tb else 0,)), ((), ()))
    has_add, has_aux = add is not None, aux is not None

    def body(*refs):
        a_ref, b_ref = refs[0], refs[1]
        pos = 2
        add_ref = aux_ref = None
        if has_add:
            add_ref = refs[pos]
            pos += 1
        if has_aux:
            aux_ref = refs[pos]
            pos += 1
        pos += len(after)
        n_out = 2 if epi == "relu2" else 1
        out_refs = refs[pos:pos + n_out]
        acc_ref = refs[pos + n_out] if nk > 1 else None

        part = lax.dot_general(a_ref[...].astype(BF16), b_ref[...].astype(BF16), dims,
                               preferred_element_type=F32)

        def finish(acc):
            if has_add:
                acc = acc + add_ref[...].astype(F32)
            if epi == "relu2":
                r = jnp.maximum(acc, 0.0)
                out_refs[0][...] = acc.astype(BF16)
                out_refs[1][...] = (r * r).astype(BF16)
            elif epi == "drelu2":
                u = aux_ref[...].astype(F32)
                out_refs[0][...] = (acc * (2.0 * jnp.maximum(u, 0.0))).astype(out_dtype)
            else:
                out_refs[0][...] = acc.astype(out_dtype)

        if nk == 1:
            finish(part)
        else:
            k = pl.program_id(2)

            @pl.when(k == 0)
            def _():
                acc_ref[...] = part

            @pl.when(k > 0)
            def _():
                acc_ref[...] += part

            @pl.when(k == nk - 1)
            def _():
                finish(acc_ref[...])

    scratch = [pltpu.VMEM((tm, tn), F32)] if nk > 1 else []
    return pl.pallas_call(
        body, name=name, grid=grid, in_specs=in_specs, out_specs=out_specs, out_shape=out_shape,
        scratch_shapes=scratch, compiler_params=_params(("parallel", "parallel", "arbitrary")),
    )(*operands)


def _rms_fwd(x, g, *, name, width=None, seg=0, tm=384, after=()):
    T = x.shape[0]
    width = x.shape[1] if width is None else width
    tm = _tile(T, tm, 16)
    after = tuple(after)

    def body(x_ref, g_ref, *rest):
        xf = x_ref[...].astype(F32)
        r = lax.rsqrt(jnp.mean(xf * xf, axis=-1, keepdims=True) + EPS)
        rest[-1][...] = (xf * r * g_ref[...]).astype(BF16)

    return pl.pallas_call(
        body, name=name, grid=(T // tm,),
        in_specs=[pl.BlockSpec((tm, width), lambda i: (i, seg)), pl.BlockSpec((1, width), lambda i: (0, 0))]
        + [pl.BlockSpec(memory_space=pl.ANY)] * len(after),
        out_specs=pl.BlockSpec((tm, width), lambda i: (i, 0)),
        out_shape=jax.ShapeDtypeStruct((T, width), BF16),
        compiler_params=_params(("parallel",)),
    )(x, g, *after)


def _rms_bwd(dy, x, g, *, name, width=None, seg=0, res=None, out_dtype=F32, tm=384):
    T = x.shape[0]
    width = x.shape[1] if width is None else width
    tm = _tile(T, tm, 16)
    has_res = res is not None

    def body(*refs):
        dy_ref, x_ref, g_ref = refs[:3]
        res_ref = refs[3] if has_res else None
        dx_ref, dg_ref = refs[-2:]
        xf = x_ref[...].astype(F32)
        dyf = dy_ref[...].astype(F32)
        r = lax.rsqrt(jnp.mean(xf * xf, axis=-1, keepdims=True) + EPS)
        xhat = xf * r
        dyh = dyf * g_ref[...]
        dx = r * (dyh - xhat * jnp.mean(dyh * xhat, axis=-1, keepdims=True))
        if has_res:
            dx = dx + res_ref[...].astype(F32)
        dx_ref[...] = dx.astype(out_dtype)
        part = jnp.sum(dyf * xhat, axis=0, keepdims=True)

        @pl.when(pl.program_id(0) == 0)
        def _():
            dg_ref[...] = part

        @pl.when(pl.program_id(0) > 0)
        def _():
            dg_ref[...] += part

    row = pl.BlockSpec((tm, width), lambda i: (i, 0))
    in_specs = [row, pl.BlockSpec((tm, width), lambda i: (i, seg)), pl.BlockSpec((1, width), lambda i: (0, 0))]
    operands = [dy, x, g]
    if has_res:
        in_specs.append(row)
        operands.append(res)
    return pl.pallas_call(
        body, name=name, grid=(T // tm,), in_specs=in_specs,
        out_specs=(row, pl.BlockSpec((1, width), lambda i: (0, 0))),
        out_shape=(jax.ShapeDtypeStruct((T, width), out_dtype), jax.ShapeDtypeStruct((1, width), F32)),
        compiler_params=_params(("arbitrary",)),
    )(*operands)


def _down(ext, k):
    return pltpu.roll(ext, k, 0)


def _up(ext, k):
    return pltpu.roll(ext, ext.shape[0] - k, 0)


def _pre_halo(ref, r, R):
    start = pl.multiple_of(jnp.maximum(r * R - HALO, 0), 8)
    keep = (r > 0).astype(F32)
    return ref[pl.ds(start, HALO), :].astype(F32) * keep


def _post_halo(ref, r, R, n_chunks):
    start = pl.multiple_of(jnp.minimum(r * R + R, (n_chunks - 1) * R + R - HALO), 8)
    keep = (r < n_chunks - 1).astype(F32)
    return ref[pl.ds(start, HALO), :].astype(F32) * keep


def _chunk(ref, r, R):
    return ref[pl.ds(pl.multiple_of(r * R, 8), R), :].astype(F32)


def _conv_fwd(rest, conv_w, *, name, dc, tc=128, rows=1056):
    T = rest.shape[0]
    tc = _tile(dc, tc)
    nb = dc // tc
    R = _tile(T, rows, 16)
    n_chunks = T // R

    def body(u_ref, b_ref, c_ref, w_ref, y_ref):
        w0, w1, w2 = w_ref[0:1, :], w_ref[1:2, :], w_ref[2:3, :]

        def chunk(r, carry):
            cu = _chunk(c_ref, r, R) * _chunk(u_ref, r, R)
            ext = jnp.concatenate([_pre_halo(c_ref, r, R) * _pre_halo(u_ref, r, R), cu], axis=0)
            conv = w0 * _down(ext, 2)[HALO:] + w1 * _down(ext, 1)[HALO:] + w2 * cu
            y_ref[pl.ds(pl.multiple_of(r * R, 8), R), :] = (_chunk(b_ref, r, R) * conv).astype(BF16)
            return carry

        lax.fori_loop(0, n_chunks, chunk, 0)

    col = lambda off: pl.BlockSpec((T, tc), lambda j: (0, off * nb + j))
    return pl.pallas_call(
        body, name=name, grid=(nb,),
        in_specs=[col(0), col(1), col(2), pl.BlockSpec((3, tc), lambda j: (0, j))],
        out_specs=pl.BlockSpec((T, tc), lambda j: (0, j)),
        out_shape=jax.ShapeDtypeStruct((T, dc), BF16),
        compiler_params=_params(("parallel",)),
    )(rest, rest, rest, conv_w)


def _conv_bwd(rest, conv_w, dy, *, name, dc, tc=128, rows=1056):
    T = rest.shape[0]
    tc = _tile(dc, tc)
    nb = dc // tc
    R = _tile(T, rows, 16)
    n_chunks = T // R

    def body(u_ref, b_ref, c_ref, w_ref, dy_ref, du_ref, db_ref, dc_ref, dw_ref):
        w0, w1, w2 = w_ref[0:1, :], w_ref[1:2, :], w_ref[2:3, :]

        def chunk(r, carry):
            a0, a1, a2 = carry
            u, b, c = _chunk(u_ref, r, R), _chunk(b_ref, r, R), _chunk(c_ref, r, R)
            dy_c = _chunk(dy_ref, r, R)
            cu = c * u
            ext = jnp.concatenate([_pre_halo(c_ref, r, R) * _pre_halo(u_ref, r, R), cu], axis=0)
            cu1, cu2 = _down(ext, 1)[HALO:], _down(ext, 2)[HALO:]
            conv = w0 * cu2 + w1 * cu1 + w2 * cu
            dconv = dy_c * b
            dext = jnp.concatenate(
                [dconv, _post_halo(dy_ref, r, R, n_chunks) * _post_halo(b_ref, r, R, n_chunks)], axis=0)
            dcu = w2 * dconv + w1 * _up(dext, 1)[:R] + w0 * _up(dext, 2)[:R]
            rows_at = pl.ds(pl.multiple_of(r * R, 8), R)
            db_ref[rows_at, :] = (dy_c * conv).astype(BF16)
            du_ref[rows_at, :] = (dcu * c).astype(BF16)
            dc_ref[rows_at, :] = (dcu * u).astype(BF16)
            return (a0 + jnp.sum(dconv * cu2, axis=0, keepdims=True),
                    a1 + jnp.sum(dconv * cu1, axis=0, keepdims=True),
                    a2 + jnp.sum(dconv * cu, axis=0, keepdims=True))

        zero = jnp.zeros((1, tc), F32)
        a0, a1, a2 = lax.fori_loop(0, n_chunks, chunk, (zero, zero, zero))
        dw_ref[0:1, :] = a0
        dw_ref[1:2, :] = a1
        dw_ref[2:3, :] = a2

    col = lambda off: pl.BlockSpec((T, tc), lambda j: (0, off * nb + j))
    own = pl.BlockSpec((T, tc), lambda j: (0, j))
    return pl.pallas_call(
        body, name=name, grid=(nb,),
        in_specs=[col(0), col(1), col(2), pl.BlockSpec((3, tc), lambda j: (0, j)), own],
        out_specs=(own, own, own, pl.BlockSpec((3, tc), lambda j: (0, j))),
        out_shape=(jax.ShapeDtypeStruct((T, dc), BF16),) * 3 + (jax.ShapeDtypeStruct((3, dc), F32),),
        compiler_params=_params(("parallel",)),
    )(rest, rest, rest, conv_w, dy)


def _window_count(r, R, n_rows, w, first_row_offset):
    t = lax.broadcasted_iota(jnp.int32, (n_rows, 1), 0) + (r * R + first_row_offset)
    return jnp.minimum(t + 1, w).astype(F32)


def _pool_fwd(rest, pool_w, pool_scale, *, name, seg0, pg, rows=1056):
    T = rest.shape[0]
    R = _tile(T, rows, 16)
    n_chunks = T // R
    n_groups = len(POOL_WINDOWS)

    def body(x_ref, w_ref, s_ref, y_ref):
        def run(window):
            def chunk(r, carry):
                g = _chunk(x_ref, r, R)
                s = jnp.concatenate([_pre_halo(x_ref, r, R), g], axis=0)
                k = 1
                while k < window:
                    s = s + _down(s, k)
                    k *= 2
                pooled = s[HALO:] / _window_count(r, R, R, window, 0) - g
                mixed = jnp.dot(pooled.astype(BF16), w_ref[0], preferred_element_type=F32)
                y_ref[pl.ds(pl.multiple_of(r * R, 8), R), :] = (mixed * s_ref[...]).astype(BF16)
                return carry

            lax.fori_loop(0, n_chunks, chunk, 0)

        for gi, window in enumerate(POOL_WINDOWS):
            pl.when(pl.program_id(0) == gi)(functools.partial(run, window))

    return pl.pallas_call(
        body, name=name, grid=(n_groups,),
        in_specs=[pl.BlockSpec((T, pg), lambda g: (0, seg0 + g)),
                  pl.BlockSpec((1, pg, pg), lambda g: (g, 0, 0)),
                  pl.BlockSpec((1, pg), lambda g: (0, g))],
        out_specs=pl.BlockSpec((T, pg), lambda g: (0, g)),
        out_shape=jax.ShapeDtypeStruct((T, n_groups * pg), BF16),
        compiler_params=_params(("parallel",)),
    )(rest, pool_w, pool_scale)


def _pool_bwd(rest, pool_w, pool_scale, dy, *, name, seg0, pg, rows=1056):
    T = rest.shape[0]
    R = _tile(T, rows, 16)
    n_chunks = T // R
    n_groups = len(POOL_WINDOWS)

    def body(x_ref, w_ref, s_ref, dy_ref, dx_ref, dw_ref, ds_ref):
        def run(window):
            def chunk(r, carry):
                dw_acc, ds_acc = carry
                g = _chunk(x_ref, r, R)
                s = jnp.concatenate([_pre_halo(x_ref, r, R), g], axis=0)
                k = 1
                while k < window:
                    s = s + _down(s, k)
                    k *= 2
                pooled = (s[HALO:] / _window_count(r, R, R, window, 0) - g).astype(BF16)
                mixed = jnp.dot(pooled, w_ref[0], preferred_element_type=F32)
                dy_c = _chunk(dy_ref, r, R)
                dm_ext = (jnp.concatenate([dy_c, _post_halo(dy_ref, r, R, n_chunks)], axis=0)
                          * s_ref[...]).astype(BF16)
                dpool_ext = lax.dot_general(dm_ext, w_ref[0], (((1,), (1,)), ((), ())),
                                            preferred_element_type=F32)
                a = dpool_ext / _window_count(r, R, R + HALO, window, 0)
                k = 1
                while k < window:
                    a = a + _up(a, k)
                    k *= 2
                dx_ref[pl.ds(pl.multiple_of(r * R, 8), R), :] = (a[:R] - dpool_ext[:R]).astype(BF16)
                dw_acc = dw_acc + lax.dot_general(pooled, dm_ext[:R], (((0,), (0,)), ((), ())),
                                                  preferred_element_type=F32)
                ds_acc = ds_acc + jnp.sum(dy_c * mixed, axis=0, keepdims=True)
                return dw_acc, ds_acc

            dw_acc, ds_acc = lax.fori_loop(0, n_chunks, chunk,
                                           (jnp.zeros((pg, pg), F32), jnp.zeros((1, pg), F32)))
            dw_ref[0] = dw_acc
            ds_ref[...] = ds_acc

        for gi, window in enumerate(POOL_WINDOWS):
            pl.when(pl.program_id(0) == gi)(functools.partial(run, window))

    own = pl.BlockSpec((T, pg), lambda g: (0, g))
    return pl.pallas_call(
        body, name=name, grid=(n_groups,),
        in_specs=[pl.BlockSpec((T, pg), lambda g: (0, seg0 + g)),
                  pl.BlockSpec((1, pg, pg), lambda g: (g, 0, 0)),
                  pl.BlockSpec((1, pg), lambda g: (0, g)), own],
        out_specs=(own, pl.BlockSpec((1, pg, pg), lambda g: (g, 0, 0)), pl.BlockSpec((1, pg), lambda g: (0, g))),
        out_shape=(jax.ShapeDtypeStruct((T, n_groups * pg), BF16),
                   jax.ShapeDtypeStruct((n_groups, pg, pg), F32),
                   jax.ShapeDtypeStruct((1, n_groups * pg), F32)),
        compiler_params=_params(("parallel",)),
    )(rest, pool_w, pool_scale, dy)


def _rope(r, cos_t, sin_t):
    return r * cos_t + pltpu.roll(r, LANES // 2, 1) * sin_t


def _rope_t(d, cos_t, sin_t):
    return d * cos_t + pltpu.roll(d * sin_t, LANES // 2, 1)


def _qk_fwd(q_raw, k_nope, rest, cos_t, sin_t, q_norm, k_norm, *, name, heads, kr_seg, tm=192):
    T = q_raw.shape[0]
    tm = _tile(T, tm, 16)

    def body(q_ref, kn_ref, kr_ref, c_ref, s_ref, gq_ref, gk_ref, qo_ref, ko_ref):
        cos_b, sin_b = c_ref[...], s_ref[...]
        kr = kr_ref[:, 0:LANES]
        kr_ss = jnp.sum(kr * kr, axis=-1, keepdims=True)
        gq, gk = gq_ref[...], gk_ref[...]
        for h in range(heads):
            lo = h * HEAD_PAD
            q = q_ref[:, lo:lo + HEAD_PAD]
            rq = lax.rsqrt(jnp.sum(q * q, axis=-1, keepdims=True) / QK_HEAD + EPS)
            qn = q * rq * gq
            qo_ref[:, lo:lo + LANES] = qn[:, :LANES].astype(BF16)
            qo_ref[:, lo + LANES:lo + HEAD_PAD] = _rope(qn[:, LANES:], cos_b, sin_b).astype(BF16)
            kn = kn_ref[:, h * LANES:(h + 1) * LANES]
            rk = lax.rsqrt((jnp.sum(kn * kn, axis=-1, keepdims=True) + kr_ss) / QK_HEAD + EPS)
            ko_ref[:, lo:lo + LANES] = (kn * rk * gk[:, :LANES]).astype(BF16)
            ko_ref[:, lo + LANES:lo + HEAD_PAD] = _rope(kr * rk * gk[:, LANES:], cos_b, sin_b).astype(BF16)

    wq, wk = heads * HEAD_PAD, heads * LANES
    return pl.pallas_call(
        body, name=name, grid=(T // tm,),
        in_specs=[pl.BlockSpec((tm, wq), lambda i: (i, 0)), pl.BlockSpec((tm, wk), lambda i: (i, 0)),
                  pl.BlockSpec((tm, HEAD_PAD), lambda i: (i, kr_seg)),
                  pl.BlockSpec((tm, LANES), lambda i: (i, 0)), pl.BlockSpec((tm, LANES), lambda i: (i, 0)),
                  pl.BlockSpec((1, HEAD_PAD), lambda i: (0, 0)), pl.BlockSpec((1, HEAD_PAD), lambda i: (0, 0))],
        out_specs=(pl.BlockSpec((tm, wq), lambda i: (i, 0)), pl.BlockSpec((tm, wq), lambda i: (i, 0))),
        out_shape=(jax.ShapeDtypeStruct((T, wq), BF16), jax.ShapeDtypeStruct((T, wq), BF16)),
        compiler_params=_params(("parallel",)),
    )(q_raw, k_nope, rest, cos_t, sin_t, q_norm, k_norm)


def _qk_bwd(dq, dk, q_raw, k_nope, rest, cos_t, sin_t, q_norm, k_norm, *, name, heads, kr_seg, tm=128):
    T = q_raw.shape[0]
    tm = _tile(T, tm, 16)

    def body(dq_ref, dk_ref, q_ref, kn_ref, kr_ref, c_ref, s_ref, gq_ref, gk_ref,
             dqr_ref, dkn_ref, dkr_ref, dgq_ref, dgk_ref):
        cos_b, sin_b = c_ref[...], s_ref[...]
        kr = kr_ref[:, 0:LANES]
        kr_ss = jnp.sum(kr * kr, axis=-1, keepdims=True)
        gq, gk = gq_ref[...], gk_ref[...]
        dgq = jnp.zeros((1, HEAD_PAD), F32)
        dgk_n = jnp.zeros((1, LANES), F32)
        dgk_r = jnp.zeros((1, LANES), F32)
        dkr = jnp.zeros((tm, LANES), F32)
        for h in range(heads):
            lo = h * HEAD_PAD
            q = q_ref[:, lo:lo + HEAD_PAD]
            rq = lax.rsqrt(jnp.sum(q * q, axis=-1, keepdims=True) / QK_HEAD + EPS)
            qhat = q * rq
            dqn = jnp.concatenate([dq_ref[:, lo:lo + LANES],
                                   _rope_t(dq_ref[:, lo + LANES:lo + HEAD_PAD], cos_b, sin_b)], axis=1)
            dgq = dgq + jnp.sum(dqn * qhat, axis=0, keepdims=True)
            dqh = dqn * gq
            dqr_ref[:, lo:lo + HEAD_PAD] = (
                rq * (dqh - qhat * (jnp.sum(dqh * qhat, axis=-1, keepdims=True) / QK_HEAD))).astype(BF16)
            kn = kn_ref[:, h * LANES:(h + 1) * LANES]
            rk = lax.rsqrt((jnp.sum(kn * kn, axis=-1, keepdims=True) + kr_ss) / QK_HEAD + EPS)
            khat_n, khat_r = kn * rk, kr * rk
            dkn_n = dk_ref[:, lo:lo + LANES]
            dkn_r = _rope_t(dk_ref[:, lo + LANES:lo + HEAD_PAD], cos_b, sin_b)
            dgk_n = dgk_n + jnp.sum(dkn_n * khat_n, axis=0, keepdims=True)
            dgk_r = dgk_r + jnp.sum(dkn_r * khat_r, axis=0, keepdims=True)
            dkh_n, dkh_r = dkn_n * gk[:, :LANES], dkn_r * gk[:, LANES:]
            proj = (jnp.sum(dkh_n * khat_n, axis=-1, keepdims=True)
                    + jnp.sum(dkh_r * khat_r, axis=-1, keepdims=True)) / QK_HEAD
            dkn_ref[:, h * LANES:(h + 1) * LANES] = (rk * (dkh_n - khat_n * proj)).astype(BF16)
            dkr = dkr + rk * (dkh_r - khat_r * proj)
        dkr_ref[:, 0:LANES] = dkr.astype(BF16)
        dkr_ref[:, LANES:HEAD_PAD] = jnp.zeros((tm, HEAD_PAD - LANES), BF16)
        dgk = jnp.concatenate([dgk_n, dgk_r], axis=1)

        @pl.when(pl.program_id(0) == 0)
        def _():
            dgq_ref[...] = dgq
            dgk_ref[...] = dgk

        @pl.when(pl.program_id(0) > 0)
        def _():
            dgq_ref[...] += dgq
            dgk_ref[...] += dgk

    wq, wk = heads * HEAD_PAD, heads * LANES
    row = lambda w: pl.BlockSpec((tm, w), lambda i: (i, 0))
    vec = pl.BlockSpec((1, HEAD_PAD), lambda i: (0, 0))
    return pl.pallas_call(
        body, name=name, grid=(T // tm,),
        in_specs=[row(wq), row(wq), row(wq), row(wk), pl.BlockSpec((tm, HEAD_PAD), lambda i: (i, kr_seg)),
                  row(LANES), row(LANES), vec, vec],
        out_specs=(row(wq), row(wk), row(HEAD_PAD), vec, vec),
        out_shape=(jax.ShapeDtypeStruct((T, wq), BF16), jax.ShapeDtypeStruct((T, wk), BF16),
                   jax.ShapeDtypeStruct((T, HEAD_PAD), BF16),
                   jax.ShapeDtypeStruct((1, HEAD_PAD), F32), jax.ShapeDtypeStruct((1, HEAD_PAD), F32)),
        compiler_params=_params(("arbitrary",)),
    )(dq, dk, q_raw, k_nope, rest, cos_t, sin_t, q_norm, k_norm)


def _causal_mask(s):
    row = lax.broadcasted_iota(jnp.int32, s.shape, 0)
    col = lax.broadcasted_iota(jnp.int32, s.shape, 1)
    return jnp.where(row >= col, s, NEG)


def _flash_fwd(q, k, v, *, name, heads, tq=384, hp=2):
    T = q.shape[0]
    tq = _tile(T, tq, LANES)
    nq = T // tq
    scale = QK_HEAD ** -0.5
    nt = (((1,), (1,)), ((), ()))

    def body(q_ref, k_ref, v_ref, o_ref, lse_ref):
        def q_block(i, carry):
            q_at = pl.ds(pl.multiple_of(i * tq, tq), tq)
            qbs = [q_ref[q_at, h * HEAD_PAD:(h + 1) * HEAD_PAD] for h in range(hp)]

            def step(j, state, masked):
                k_at = pl.ds(pl.multiple_of(j * tq, tq), tq)
                new = []
                for h in range(hp):
                    m, l, acc = state[h]
                    s = lax.dot_general(qbs[h], k_ref[k_at, h * HEAD_PAD:(h + 1) * HEAD_PAD], nt,
                                        preferred_element_type=F32) * scale
                    if masked:
                        s = _causal_mask(s)
                    m_new = jnp.maximum(m, jnp.max(s, axis=-1, keepdims=True))
                    p = jnp.exp(s - m_new)
                    alpha = jnp.exp(m - m_new)
                    l = alpha * l + jnp.sum(p, axis=-1, keepdims=True)
                    acc = alpha * acc + jnp.dot(p.astype(BF16), v_ref[k_at, h * V_HEAD:(h + 1) * V_HEAD],
                                                preferred_element_type=F32)
                    new.append((m_new, l, acc))
                return tuple(new)

            init = tuple((jnp.full((tq, 1), NEG, F32), jnp.zeros((tq, 1), F32), jnp.zeros((tq, V_HEAD), F32))
                         for _ in range(hp))
            state = lax.fori_loop(0, i, lambda j, st: step(j, st, False), init)
            state = step(i, state, True)
            for h in range(hp):
                m, l, acc = state[h]
                o_ref[q_at, h * V_HEAD:(h + 1) * V_HEAD] = (acc / l).astype(BF16)
                lse_ref[h, q_at, :] = jnp.broadcast_to(m + jnp.log(l), (tq, LANES))
            return carry

        lax.fori_loop(0, nq, q_block, 0)

    qk_spec = pl.BlockSpec((T, hp * HEAD_PAD), lambda g: (0, g))
    v_spec = pl.BlockSpec((T, hp * V_HEAD), lambda g: (0, g))
    return pl.pallas_call(
        body, name=name, grid=(heads // hp,), in_specs=[qk_spec, qk_spec, v_spec],
        out_specs=(v_spec, pl.BlockSpec((hp, T, LANES), lambda g: (g, 0, 0))),
        out_shape=(jax.ShapeDtypeStruct((T, heads * V_HEAD), BF16), jax.ShapeDtypeStruct((heads, T, LANES), F32)),
        compiler_params=_params(("parallel",)),
    )(q, k, v)


def _flash_bwd(q, k, v, o, do, lse, *, name, heads, tq=384):
    T = q.shape[0]
    tq = _tile(T, tq, LANES)
    nq = T // tq
    scale = QK_HEAD ** -0.5
    nt = (((1,), (1,)), ((), ()))
    tn = (((0,), (0,)), ((), ()))

    def body(q_ref, k_ref, v_ref, o_ref, do_ref, lse_ref, dq_ref, dk_ref, dv_ref, delta_ref):
        def fill_delta(i, carry):
            at = pl.ds(pl.multiple_of(i * tq, tq), tq)
            d = jnp.sum(o_ref[at, :].astype(F32) * do_ref[at, :].astype(F32), axis=-1, keepdims=True)
            delta_ref[at, :] = jnp.broadcast_to(d, (tq, LANES))
            dq_ref[at, :] = jnp.zeros((tq, HEAD_PAD), F32)
            return carry

        lax.fori_loop(0, nq, fill_delta, 0)

        def kv_block(j, carry):
            k_at = pl.ds(pl.multiple_of(j * tq, tq), tq)
            kb, vb = k_ref[k_at, :], v_ref[k_at, :]

            def step(i, state, masked):
                dk_acc, dv_acc = state
                q_at = pl.ds(pl.multiple_of(i * tq, tq), tq)
                qb, dob = q_ref[q_at, :], do_ref[q_at, :]
                s = lax.dot_general(qb, kb, nt, preferred_element_type=F32) * scale
                if masked:
                    s = _causal_mask(s)
                p = jnp.exp(s - lse_ref[0, q_at, :][:, 0:1])
                dv_acc = dv_acc + lax.dot_general(p.astype(BF16), dob, tn, preferred_element_type=F32)
                dp = lax.dot_general(dob, vb, nt, preferred_element_type=F32)
                ds = (p * (dp - delta_ref[q_at, :][:, 0:1]) * scale).astype(BF16)
                dk_acc = dk_acc + lax.dot_general(ds, qb, tn, preferred_element_type=F32)
                dq_ref[q_at, :] += jnp.dot(ds, kb, preferred_element_type=F32)
                return dk_acc, dv_acc

            state = step(j, (jnp.zeros((tq, HEAD_PAD), F32), jnp.zeros((tq, V_HEAD), F32)), True)
            rest = nq - 1 - j

            def two_steps(t, st):
                i0 = j + 1 + 2 * t
                return step(i0 + 1, step(i0, st, False), False)

            state = lax.fori_loop(0, rest // 2, two_steps, state)
            dk_acc, dv_acc = lax.cond(rest % 2 == 1, lambda st: step(nq - 1, st, False), lambda st: st, state)
            dk_ref[k_at, :] = dk_acc
            dv_ref[k_at, :] = dv_acc.astype(BF16)
            return carry

        lax.fori_loop(0, nq, kv_block, 0)

    qk_spec = pl.BlockSpec((T, HEAD_PAD), lambda h: (0, h))
    v_spec = pl.BlockSpec((T, V_HEAD), lambda h: (0, h))
    return pl.pallas_call(
        body, name=name, grid=(heads,),
        in_specs=[qk_spec, qk_spec, v_spec, v_spec, v_spec, pl.BlockSpec((1, T, LANES), lambda h: (h, 0, 0))],
        out_specs=(qk_spec, qk_spec, v_spec),
        out_shape=(jax.ShapeDtypeStruct((T, heads * HEAD_PAD), F32), jax.ShapeDtypeStruct((T, heads * HEAD_PAD), F32),
                   jax.ShapeDtypeStruct((T, heads * V_HEAD), BF16)),
        scratch_shapes=[pltpu.VMEM((T, LANES), F32)],
        compiler_params=_params(("parallel",)),
    )(q, k, v, o, do, lse)


def _merge_fwd(gl, pa, pb, pc, *, name, d, tm=384, tn=1024):
    T = pa.shape[0]
    tm, tn = _tile(T, tm, 16), _tile(d, tn)
    nb = d // tn

    def body(g0, g1, g2, a, b, c, o_ref):
        o_ref[...] = (jax.nn.sigmoid(g0[...]) * a[...] + jax.nn.sigmoid(g1[...]) * b[...]
                      + jax.nn.sigmoid(g2[...]) * c[...]).astype(BF16)

    gate = lambda n: pl.BlockSpec((tm, tn), lambda i, j: (i, n * nb + j))
    blk = pl.BlockSpec((tm, tn), lambda i, j: (i, j))
    return pl.pallas_call(
        body, name=name, grid=(T // tm, nb), in_specs=[gate(0), gate(1), gate(2), blk, blk, blk],
        out_specs=blk, out_shape=jax.ShapeDtypeStruct((T, d), BF16),
        compiler_params=_params(("parallel", "parallel")),
    )(gl, gl, gl, pa, pb, pc)


def _merge_bwd(dm, gl, pa, pb, pc, *, name, d, tm=384, tn=1024):
    T = pa.shape[0]
    tm, tn = _tile(T, tm, 16), _tile(d, tn)
    nb = d // tn

    def body(dm_ref, g0, g1, g2, a, b, c, da, db, dc, dg0, dg1, dg2):
        dmv = dm_ref[...]
        for g_ref, p_ref, dp_ref, dg_ref in ((g0, a, da, dg0), (g1, b, db, dg1), (g2, c, dc, dg2)):
            sg = jax.nn.sigmoid(g_ref[...])
            dp_ref[...] = (dmv * sg).astype(BF16)
            dg_ref[...] = (dmv * p_ref[...] * sg * (1.0 - sg)).astype(BF16)

    gate = lambda n: pl.BlockSpec((tm, tn), lambda i, j: (i, n * nb + j))
    blk = pl.BlockSpec((tm, tn), lambda i, j: (i, j))
    return pl.pallas_call(
        body, name=name, grid=(T // tm, nb), in_specs=[blk, gate(0), gate(1), gate(2), blk, blk, blk],
        out_specs=(blk,) * 6, out_shape=(jax.ShapeDtypeStruct((T, d), BF16),) * 6,
        compiler_params=_params(("parallel", "parallel")),
    )(dm, gl, gl, gl, pa, pb, pc)


def _loss(y, target, *, name, first, last, tm=384):
    T, d = y.shape
    tm = _tile(T, tm, 16)

    def body(y_ref, t_ref, loss_ref, dy_ref):
        i = pl.program_id(0)
        row = lax.broadcasted_iota(jnp.int32, (tm, 1), 0) + i * tm
        real = jnp.logical_and(row >= first, row < last)
        err = jnp.where(real, y_ref[...] - t_ref[...], 0.0)
        dy_ref[...] = err * (1.0 / d)
        part = jnp.broadcast_to(jnp.sum(err * err, keepdims=True).reshape(1, 1), (1, LANES))

        @pl.when(i == 0)
        def _():
            loss_ref[...] = part

        @pl.when(i > 0)
        def _():
            loss_ref[...] += part

    blk = pl.BlockSpec((tm, d), lambda i: (i, 0))
    return pl.pallas_call(
        body, name=name, grid=(T // tm,), in_specs=[blk, blk],
        out_specs=(pl.BlockSpec((1, LANES), lambda i: (0, 0)), blk),
        out_shape=(jax.ShapeDtypeStruct((1, LANES), F32), jax.ShapeDtypeStruct((T, d), F32)),
        compiler_params=_params(("arbitrary",)),
    )(y, target)


def _as3d(a):
    return a.reshape(a.shape[0], -1, a.shape[-1])


def _sum_stack(parts, *, name, out_dtype, rows=256):
    n, R, C = parts.shape
    tr = _tile(R, rows, 16)

    def body(p_ref, o_ref):
        acc = p_ref[0].astype(F32)
        for s in range(1, n):
            acc = acc + p_ref[s].astype(F32)
        o_ref[...] = acc.astype(out_dtype)

    return pl.pallas_call(
        body, name=name, grid=(R // tr,),
        in_specs=[pl.BlockSpec((n, tr, C), lambda i: (0, i, 0))],
        out_specs=pl.BlockSpec((tr, C), lambda i: (i, 0)),
        out_shape=jax.ShapeDtypeStruct((R, C), out_dtype),
        compiler_params=_params(("parallel",)),
    )(parts)


def _adamw(w, g, m, v, *, name, rows=128):
    R, C = w.shape
    tr = _tile(R, rows, 8)
    c1 = 1.0 - ADAM_B1 ** ADAM_STEP
    c2 = 1.0 - ADAM_B2 ** ADAM_STEP

    def body(w_ref, g_ref, m_ref, v_ref, d_ref, nm_ref, nv_ref):
        gv = g_ref[...]
        nm = ADAM_B1 * m_ref[...] + (1.0 - ADAM_B1) * gv
        nv = ADAM_B2 * v_ref[...] + (1.0 - ADAM_B2) * (gv * gv)
        nm_ref[...] = nm
        nv_ref[...] = nv
        d_ref[...] = -ADAM_LR * ((nm / c1) / (jnp.sqrt(nv / c2) + ADAM_EPS) + ADAM_WD * w_ref[...])

    blk = pl.BlockSpec((tr, C), lambda i: (i, 0))
    return pl.pallas_call(
        body, name=name, grid=(R // tr,), in_specs=[blk] * 4, out_specs=(blk,) * 3,
        out_shape=(jax.ShapeDtypeStruct((R, C), F32),) * 3,
        compiler_params=_params(("parallel",)),
    )(w, g, m, v)


def _one_hot(index, n):
    return jnp.broadcast_to((jnp.arange(n) == index).astype(F32)[:, None, None], (n, 8, LANES))


def _is_set(flags_ref, s):
    return flags_ref[s, 0:1, 0:1] > 0.5


def _pair_sum(pieces, recv, core, *, name, rows=256):
    _, H, C = recv.shape
    tr = _tile(H, rows, 16)
    nh = H // tr

    def body(lo_ref, hi_ref, r_ref, core_ref, o_ref):
        mine = jnp.where(_is_set(core_ref, 0), lo_ref[0], hi_ref[0])
        o_ref[0] = (mine.astype(F32) + r_ref[0].astype(F32)).astype(BF16)

    blk = pl.BlockSpec((1, tr, C), lambda j, i: (j, i, 0))
    return pl.pallas_call(
        body, name=name, grid=(4, nh),
        in_specs=[blk, pl.BlockSpec((1, tr, C), lambda j, i: (j, nh + i, 0)), blk,
                  pl.BlockSpec((2, 8, LANES), lambda j, i: (0, 0, 0))],
        out_specs=blk, out_shape=jax.ShapeDtypeStruct((4, H, C), BF16),
        compiler_params=_params(("parallel", "parallel")),
    )(pieces, pieces, recv, core)


def _chip_sum(pair, landed, chip_flags, *, name, rows=256):
    _, H, C = pair.shape
    tr = _tile(H, rows, 16)

    def body(p_ref, l_ref, chip_ref, o_ref):
        acc = None
        for s in range(4):
            part = jnp.where(_is_set(chip_ref, s), p_ref[s], l_ref[s]).astype(F32)
            acc = part if acc is None else acc + part
        o_ref[...] = acc

    blk = pl.BlockSpec((4, tr, C), lambda i: (0, i, 0))
    return pl.pallas_call(
        body, name=name, grid=(H // tr,),
        in_specs=[blk, blk, pl.BlockSpec((4, 8, LANES), lambda i: (0, 0, 0))],
        out_specs=pl.BlockSpec((tr, C), lambda i: (i, 0)), out_shape=jax.ShapeDtypeStruct((H, C), F32),
        compiler_params=_params(("parallel",)),
    )(pair, landed, chip_flags)


def _adamw_layer(w, m, v, total, recv, core, layer, prev, *, name, rows=128, col_halves=False):
    _, R, C = w.shape
    H, wd = total.shape
    tr = _tile(H, rows, 8)
    nh = H // tr
    c1 = 1.0 - ADAM_B1 ** ADAM_STEP
    c2 = 1.0 - ADAM_B2 ** ADAM_STEP
    n_prev = 0 if prev is None else 4

    def body(*refs):
        w_ref, m_ref, v_ref, t_ref, r_ref, core_ref = refs[:6]
        g_ref, d_ref, nm_ref, nv_ref = refs[6 + n_prev:]
        half_is_mine = jnp.where(pl.program_id(0) == 0, core_ref[0, 0:1, 0:1], core_ref[1, 0:1, 0:1]) > 0.5
        gv = jnp.where(half_is_mine, t_ref[...], r_ref[...])
        nm = ADAM_B1 * m_ref[0] + (1.0 - ADAM_B1) * gv
        nv = ADAM_B2 * v_ref[0] + (1.0 - ADAM_B2) * (gv * gv)
        g_ref[0] = gv
        nm_ref[0] = nm
        nv_ref[0] = nv
        d_ref[0] = -ADAM_LR * ((nm / c1) / (jnp.sqrt(nv / c2) + ADAM_EPS) + ADAM_WD * w_ref[0])

    if col_halves:
        lay = pl.BlockSpec((1, tr, wd), lambda hf, i: (layer, i, hf))
    else:
        lay = pl.BlockSpec((1, tr, wd), lambda hf, i: (layer, hf * nh + i, 0))
    one = pl.BlockSpec((tr, wd), lambda hf, i: (i, 0))
    operands = [w, m, v, total, recv, core] + ([] if prev is None else list(prev))
    return pl.pallas_call(
        body, name=name, grid=(2, nh),
        in_specs=[lay, lay, lay, one, one, pl.BlockSpec((2, 8, LANES), lambda hf, i: (0, 0, 0))] + [ANY] * n_prev,
        out_specs=(lay,) * 4, out_shape=(jax.ShapeDtypeStruct((2, R, C), F32),) * 4,
        input_output_aliases={6 + i: i for i in range(n_prev)},
        compiler_params=_params(("parallel", "parallel")),
    )(*operands)


ANY = pl.BlockSpec(memory_space=pl.ANY)


def _coords():
    return lax.axis_index("x"), lax.axis_index("y"), lax.axis_index("c")


HBM = pl.BlockSpec(memory_space=pltpu.HBM)
SEM = pl.BlockSpec(memory_space=pltpu.SEMAPHORE)
EFFECT = pltpu.SideEffectType.DATAFLOW_SIDE_EFFECTING


def _copies(plan, bufs, send_sems, recv_sems):
    return [pltpu.make_async_remote_copy(src_ref=s, dst_ref=d, send_sem=send_sems.at[i], recv_sem=recv_sems.at[i],
                                         device_id=to, device_id_type=MESH)
            for i, (s, d, to) in enumerate(plan(bufs))]


def _start_copies(bufs, groups, *, name):
    nb, ng = len(bufs), len(groups)

    def body(*refs):
        buf_refs = refs[:nb]
        sems = refs[nb:nb + 2 * ng]
        token = refs[-1]
        for g, (plan, _) in enumerate(groups):
            for cp in _copies(plan, buf_refs, sems[2 * g], sems[2 * g + 1]):
                cp.start()
        token[...] = jnp.zeros_like(token)

    sem_shapes = []
    for _, n in groups:
        sem_shapes += [pltpu.SemaphoreType.DMA((n,)), pltpu.SemaphoreType.DMA((n,))]
    out = pl.pallas_call(
        body, name=name, in_specs=[HBM] * nb,
        out_specs=tuple([SEM] * (2 * ng) + [HBM] * nb + [pl.BlockSpec(memory_space=pltpu.VMEM)]),
        out_shape=tuple(sem_shapes + [pltpu.HBM(b.shape, b.dtype) for b in bufs] + [jax.ShapeDtypeStruct((8, LANES), F32)]),
        input_output_aliases={i: 2 * ng + i for i in range(nb)},
        compiler_params=pltpu.CompilerParams(has_side_effects=EFFECT),
    )(*[pltpu.with_memory_space_constraint(b, pltpu.HBM) for b in bufs])
    sems = [(out[2 * g], out[2 * g + 1]) for g in range(ng)]
    return sems, list(out[2 * ng:2 * ng + nb]), out[-1]


def _wait_copies(bufs, sems, plan, after, *, name):
    nb = len(bufs)

    def body(*refs):
        buf_refs = refs[:nb]
        for cp in _copies(plan, buf_refs, refs[nb], refs[nb + 1]):
            cp.wait_send()
            cp.wait_recv()

    out = pl.pallas_call(
        body, name=name, in_specs=[HBM] * nb + [SEM, SEM, ANY], out_specs=tuple([HBM] * nb),
        out_shape=tuple(pltpu.HBM(b.shape, b.dtype) for b in bufs),
        input_output_aliases={i: i for i in range(nb)},
        compiler_params=pltpu.CompilerParams(has_side_effects=EFFECT),
    )(*bufs, sems[0], sems[1], after)
    return list(out)


def _half(ref, c):
    h = ref.shape[0] // 2
    return ref.at[pl.ds(c * h, h)]


def _ici_gather_plan(pairs):
    def plan(refs):
        x, y, c = _coords()
        me = 2 * x + y
        out = []
        for s, d in pairs:
            for cx, cy in [(1 - x, y), (x, 1 - y), (1 - x, 1 - y)]:
                out.append((_half(refs[s], c), _half(refs[d].at[me], c), (cx, cy, c)))
            out.append((refs[s], refs[d].at[me], (x, y, 1 - c)))
        return out
    return plan, 4 * len(pairs)


def _d2d_forward_plan(lands):
    def plan(refs):
        x, y, c = _coords()
        out = []
        for d in lands:
            for cx, cy in [(1 - x, y), (x, 1 - y), (1 - x, 1 - y)]:
                got = _half(refs[d].at[2 * cx + cy], c)
                out.append((got, got, (x, y, 1 - c)))
        return out
    return plan, 3 * len(lands)


def _swap_half_plan(pairs):
    def plan(refs):
        x, y, c = _coords()
        out = []
        for s, d in pairs:
            h = refs[d].shape[1]
            out.append((refs[s].at[:, pl.ds((1 - c) * h, h)], refs[d], (x, y, 1 - c)))
        return out
    return plan, len(pairs)


def _scatter_plan(pairs):
    def plan(refs):
        x, y, c = _coords()
        me = 2 * x + y
        out = []
        for s, d in pairs:
            for cx, cy in [(1 - x, y), (x, 1 - y), (1 - x, 1 - y)]:
                out.append((refs[s].at[2 * cx + cy], refs[d].at[me], (cx, cy, c)))
        return out
    return plan, 3 * len(pairs)


def _swap_total_plan(pairs):
    def plan(refs):
        x, y, c = _coords()
        return [(refs[s], refs[d], (x, y, 1 - c)) for s, d in pairs]
    return plan, len(pairs)


def _gather_all(block, *, name):
    def body(src, out, send_sems, recv_sems, local_sem):
        x, y, c = _coords()
        me = 4 * x + 2 * y + c
        flips = [(fx, fy, fc) for fx in (0, 1) for fy in (0, 1) for fc in (0, 1)][1:]
        mine = pltpu.make_async_copy(src, out.at[me], local_sem)
        mine.start()
        peers = [(x ^ fx, y ^ fy, c ^ fc) for fx, fy, fc in flips]
        cps = [pltpu.make_async_remote_copy(src_ref=src, dst_ref=out.at[me], send_sem=send_sems.at[k],
                                            recv_sem=recv_sems.at[k], device_id=peer, device_id_type=MESH)
               for k, peer in enumerate(peers)]
        for cp in cps:
            cp.start()
        for k, (px, py, pc) in enumerate(peers):
            slot = out.at[4 * px + 2 * py + pc]
            pltpu.make_async_remote_copy(src_ref=slot, dst_ref=slot, send_sem=send_sems.at[k], recv_sem=recv_sems.at[k],
                                         device_id=(px, py, pc), device_id_type=MESH).wait_recv()
        for cp in cps:
            cp.wait_send()
        mine.wait()

    return pl.pallas_call(
        body, name=name, in_specs=[ANY], out_specs=ANY,
        out_shape=jax.ShapeDtypeStruct((8,) + block.shape, block.dtype),
        scratch_shapes=[pltpu.SemaphoreType.DMA((7,)), pltpu.SemaphoreType.DMA((7,)), pltpu.SemaphoreType.DMA],
    )(block)


def _cols(o):
    return jnp.transpose(o, (1, 0, 2)).reshape(o.shape[1], -1)


def _uncols(full):
    return jnp.transpose(full.reshape(full.shape[0], 4, -1), (1, 0, 2))


def _rope_pad(x1, x2):
    z = jnp.zeros_like(x1)
    return jnp.concatenate([x1, z, x2, z], axis=-1)


def _head_pad(w, heads):
    r = w.reshape(w.shape[0], heads, QK_HEAD)
    half = QK_ROPE // 2
    out = jnp.concatenate([r[..., :QK_NOPE], _rope_pad(r[..., QK_NOPE:QK_NOPE + half], r[..., QK_NOPE + half:])], axis=-1)
    return out.reshape(w.shape[0], heads * HEAD_PAD)


def _head_unpad(w, heads):
    r = w.reshape(w.shape[0], heads, HEAD_PAD)
    half = QK_ROPE // 2
    out = jnp.concatenate([r[..., :QK_NOPE], r[..., QK_NOPE:QK_NOPE + half],
                           r[..., QK_NOPE + 2 * half:QK_NOPE + 3 * half]], axis=-1)
    return out.reshape(w.shape[0], heads * QK_HEAD)


class _Dims:
    def __init__(self, d, seq):
        self.d = d
        self.seq = seq
        self.t_real = N_META + seq
        self.t = -(-self.t_real // LANES) * LANES
        self.dc = d // 2
        self.dp = d // 2
        self.pg = self.dp // len(POOL_WINDOWS)
        self.heads = d // 128
        self.dff = 4 * d
        self.a_end = 3 * self.dc
        self.q_end = self.a_end + Q_LORA
        self.kv_end = self.q_end + KV_LORA
        self.kr_end = self.kv_end + QK_ROPE
        self.pool_end = self.kr_end + self.dp
        self.d_in = self.pool_end + 3 * d
        self.r_pool = 3 * self.dc
        self.r_q = self.r_pool + self.dp
        self.r_kv = self.r_q + Q_LORA
        self.r_kr = self.r_kv + KV_LORA
        self.r_width = self.r_kr + HEAD_PAD


def _split_cols(a):
    return jnp.moveaxis(a.reshape(a.shape[:-1] + (2, a.shape[-1] // 2)), -2, -3)


def _join_cols(a):
    a = jnp.moveaxis(a, -3, -2)
    return a.reshape(a.shape[:-2] + (a.shape[-2] * a.shape[-1],))


def _in_weights(dm, pieces):
    w_t = _join_cols(pieces).reshape(dm.d_in, dm.d)
    half = QK_ROPE // 2
    kr = w_t[dm.kv_end:dm.kr_end]
    zeros = jnp.zeros((half, dm.d), BF16)
    kr_p = jnp.concatenate([kr[:half], zeros, kr[half:], zeros, jnp.zeros((HEAD_PAD - LANES, dm.d), BF16)], axis=0)
    return dict(
        wg_t=w_t[dm.pool_end:],
        wr_t=jnp.concatenate([w_t[:dm.a_end], w_t[dm.kr_end:dm.pool_end], w_t[dm.a_end:dm.kv_end], kr_p], axis=0))


def _other_weights(dm, g):
    out = {}
    if "w_ukv" in g:
        w_ukv = _cols(g["w_ukv"]).reshape(KV_LORA, dm.heads, QK_NOPE + V_HEAD)
        out["wkn"] = w_ukv[:, :, :QK_NOPE].reshape(KV_LORA, dm.heads * QK_NOPE)
        out["wv"] = w_ukv[:, :, QK_NOPE:].reshape(KV_LORA, dm.heads * V_HEAD)
    if "w_uq" in g:
        out["wuq"] = _head_pad(_cols(g["w_uq"]), dm.heads)
    if "pool_w" in g:
        out["wp"] = jnp.transpose(g["pool_w"], (1, 0, 2, 3)).reshape(len(POOL_WINDOWS), dm.pg, dm.pg)
    for name, key in (("w_branch_a", "wba"), ("w_branch_c", "wbc"), ("w_up", "wup")):
        if name in g:
            out[key] = _cols(g[name])
    for name, key in (("w_branch_b", "wbb"), ("w_o", "wo"), ("w_down", "wdn")):
        if name in g:
            out[key] = g[name].reshape(-1, dm.d)
    return out


def _small_weights(small):
    return dict(
        conv_w=small["conv_w"],
        attn_norm=small["attn_norm"][None], mlp_norm=small["mlp_norm"][None],
        q_lat_norm=small["q_lat_norm"][None], kv_lat_norm=small["kv_lat_norm"][None],
        q_norm=_head_pad(small["q_norm"][None], 1), k_norm=_head_pad(small["k_norm"][None], 1),
        pool_scale=small["pool_scale"][None],
    )


def _grad_piece(dm, dw, name):
    half = QK_ROPE // 2
    rows = lambda a: a.reshape((4, a.shape[0] // 4) + a.shape[1:])
    if name == "w_in":
        dwr, dwg = dw["wr_t"], dw["wg_t"]
        d_t = jnp.concatenate([
            dwr[:dm.r_pool], dwr[dm.r_q:dm.r_kr], dwr[dm.r_kr:dm.r_kr + half],
            dwr[dm.r_kr + 2 * half:dm.r_kr + 3 * half], dwr[dm.r_pool:dm.r_q], dwg], axis=0)
        out = _split_cols(rows(d_t))
    elif name == "w_ukv":
        out = _uncols(jnp.concatenate([dw["wkn"].reshape(KV_LORA, dm.heads, QK_NOPE),
                                       dw["wv"].reshape(KV_LORA, dm.heads, V_HEAD)], axis=-1).reshape(KV_LORA, -1))
    elif name == "w_uq":
        out = _uncols(_head_unpad(dw["wuq"], dm.heads))
    elif name == "pool_w":
        out = jnp.transpose(dw["wp"].reshape(len(POOL_WINDOWS), 4, dm.pg // 4, dm.pg), (1, 0, 2, 3))
    elif name in ("w_branch_a", "w_branch_c", "w_up"):
        out = _uncols(dw[{"w_branch_a": "wba", "w_branch_c": "wbc", "w_up": "wup"}[name]])
    else:
        out = rows(dw[{"w_branch_b": "wbb", "w_o": "wo", "w_down": "wdn"}[name]])
    return out.astype(BF16)


def _layer_fwd(dm, W, x, cos_t, sin_t, tag, more=None, h=None):
    n = lambda s: f"{s}_{tag}"
    if h is None:
        h = _rms_fwd(x, W["attn_norm"], name=n("attn_norm"))
    gl = _mm(h, W["wg_t"], name=n("proj_gates"), tb=True)
    rest = _mm(h, W["wr_t"], name=n("proj_rest"), tb=True)
    if more is not None:
        W.update(more("after_proj", rest))
    y_a = _conv_fwd(rest, W["conv_w"], name=n("conv"), dc=dm.dc)
    y_c = _pool_fwd(rest, W["wp"], W["pool_scale"], name=n("pool"), seg0=dm.r_pool // dm.pg, pg=dm.pg)
    q_lat = _rms_fwd(rest, W["q_lat_norm"], name=n("q_lat_norm"), width=Q_LORA, seg=dm.r_q // Q_LORA)
    kv_lat = _rms_fwd(rest, W["kv_lat_norm"], name=n("kv_lat_norm"), width=KV_LORA, seg=dm.r_kv // KV_LORA)
    q_raw = _mm(q_lat, W["wuq"], name=n("up_q"))
    k_nope = _mm(kv_lat, W["wkn"], name=n("up_k"))
    v = _mm(kv_lat, W["wv"], name=n("up_v"), out_dtype=BF16)
    q, k = _qk_fwd(q_raw, k_nope, rest, cos_t, sin_t, W["q_norm"], W["k_norm"], name=n("qk_norm_rope"),
                   heads=dm.heads, kr_seg=dm.r_kr // HEAD_PAD)
    if more is not None:
        W.update(more("after_qk", q))
    y_b, lse = _flash_fwd(q, k, v, name=n("attention"), heads=dm.heads)
    pa = _mm(y_a, W["wba"], name=n("branch_a"))
    pb = _mm(y_b, W["wbb"], name=n("branch_b"))
    pc = _mm(y_c, W["wbc"], name=n("branch_c"))
    merged = _merge_fwd(gl, pa, pb, pc, name=n("merge"), d=dm.d)
    x1 = _mm(merged, W["wo"], name=n("out_proj"), add=x)
    h2 = _rms_fwd(x1, W["mlp_norm"], name=n("mlp_norm"))
    up, act = _mm(h2, W["wup"], name=n("mlp_up"), epi="relu2")
    x2 = _mm(act, W["wdn"], name=n("mlp_down"), add=x1, tk=2048)
    saved = dict(x=x, h=h, gl=gl, rest=rest, y_a=y_a, y_c=y_c, q_lat=q_lat, kv_lat=kv_lat, q_raw=q_raw, k_nope=k_nope,
                 v=v, q=q, k=k, y_b=y_b, lse=lse, pa=pa, pb=pb, pc=pc, merged=merged, x1=x1, h2=h2, up=up, act=act)
    return x2, saved


def _layer_bwd(dm, W, S, dx2, cos_t, sin_t, tag, hook=None):
    n = lambda s: f"{s}_{tag}"
    dw, ds = {}, {}
    if hook is None:
        hook = lambda point, t, dw_so_far: ()
    dup = _mm(dx2, W["wdn"], name=n("d_mlp_down"), tb=True, aux=S["up"], epi="drelu2", out_dtype=BF16,
              after=hook("start", dx2, dw))
    dw["wdn"] = _mm(S["act"], dx2, name=n("dw_mlp_down"), ta=True, tm=1024, tk=1408)
    dh2 = _mm(dup, W["wup"], name=n("d_mlp_up"), tb=True, tk=2048)
    dw["wup"] = _mm(S["h2"], dup, name=n("dw_mlp_up"), ta=True, tm=1024, tk=1408)
    dx1, ds["mlp_norm"] = _rms_bwd(dh2, S["x1"], W["mlp_norm"], name=n("d_mlp_norm"), res=dx2)
    dmerged = _mm(dx1, W["wo"], name=n("d_out_proj"), tb=True, after=hook("after_mlp", dx1, dw))
    dw["wo"] = _mm(S["merged"], dx1, name=n("dw_out_proj"), ta=True, tm=1024, tk=1408)
    dpa, dpb, dpc, dg0, dg1, dg2 = _merge_bwd(dmerged, S["gl"], S["pa"], S["pb"], S["pc"], name=n("d_merge"), d=dm.d)
    dgl = jnp.concatenate([dg0, dg1, dg2], axis=1)
    dy_a = _mm(dpa, W["wba"], name=n("d_branch_a"), tb=True)
    dw["wba"] = _mm(S["y_a"], dpa, name=n("dw_branch_a"), ta=True, tm=1024, tk=1408)
    dy_b = _mm(dpb, W["wbb"], name=n("d_branch_b"), tb=True, out_dtype=BF16)
    dw["wbb"] = _mm(S["y_b"], dpb, name=n("dw_branch_b"), ta=True, tm=1024, tk=1408)
    dy_c = _mm(dpc, W["wbc"], name=n("d_branch_c"), tb=True)
    dw["wbc"] = _mm(S["y_c"], dpc, name=n("dw_branch_c"), ta=True, tm=1024, tk=1408)
    dq, dk, dv = _flash_bwd(S["q"], S["k"], S["v"], S["y_b"], dy_b, S["lse"], name=n("d_attention"), heads=dm.heads)
    after_attention = hook("after_attention", dq, dw)
    dq_raw, dk_nope, dk_rope, dgq, dgk = _qk_bwd(
        dq, dk, S["q_raw"], S["k_nope"], S["rest"], cos_t, sin_t, W["q_norm"], W["k_norm"], name=n("d_qk_norm_rope"),
        heads=dm.heads, kr_seg=dm.r_kr // HEAD_PAD)
    ds["q_norm"] = _head_unpad(dgq, 1)
    ds["k_norm"] = _head_unpad(dgk, 1)
    dkv_v = _mm(dv, W["wv"], name=n("d_up_v"), tb=True, after=after_attention)
    dq_lat_n = _mm(dq_raw, W["wuq"], name=n("d_up_q"), tb=True, tk=2048, after=hook("after_qk", dq_raw, dw))
    dw["wuq"] = _mm(S["q_lat"], dq_raw, name=n("dw_up_q"), ta=True, tm=512, tk=1408)
    dkv_lat_n = _mm(dk_nope, W["wkn"], name=n("d_up_k"), tb=True, add=dkv_v)
    dw["wkn"] = _mm(S["kv_lat"], dk_nope, name=n("dw_up_k"), ta=True, tm=512, tk=1408)
    dw["wv"] = _mm(S["kv_lat"], dv, name=n("dw_up_v"), ta=True, tm=512, tk=1408)
    dq_lat, ds["q_lat_norm"] = _rms_bwd(dq_lat_n, S["rest"], W["q_lat_norm"], name=n("d_q_lat_norm"), width=Q_LORA,
                                        seg=dm.r_q // Q_LORA, out_dtype=BF16)
    dkv_lat, ds["kv_lat_norm"] = _rms_bwd(dkv_lat_n, S["rest"], W["kv_lat_norm"], name=n("d_kv_lat_norm"), width=KV_LORA,
                                          seg=dm.r_kv // KV_LORA, out_dtype=BF16)
    du, db, dc, ds["conv_w"] = _conv_bwd(S["rest"], W["conv_w"], dy_a, name=n("d_conv"), dc=dm.dc)
    dpool, dw["wp"], ds["pool_scale"] = _pool_bwd(S["rest"], W["wp"], W["pool_scale"], dy_c, name=n("d_pool"),
                                                  seg0=dm.r_pool // dm.pg, pg=dm.pg)
    drest = jnp.concatenate([du, db, dc, dpool, dq_lat, dkv_lat, dk_rope], axis=1)
    dw["wg_t"] = _mm(dgl, S["h"], name=n("dw_proj_gates"), ta=True, tm=1024, tk=1408)
    dw["wr_t"] = _mm(drest, S["h"], name=n("dw_proj_rest"), ta=True, tm=1024, tk=1408)
    dh_g = _mm(dgl, W["wg_t"], name=n("d_proj_gates"), tk=2048, after=hook("after_dw_in", dw["wr_t"], dw))
    dh = _mm(drest, W["wr_t"], name=n("d_proj_rest"), add=dh_g, tk=1792, after=hook("after_dh_gates", dh_g, dw))
    dx, ds["attn_norm"] = _rms_bwd(dh, S["x"], W["attn_norm"], name=n("d_attn_norm"), res=dx1)
    return dx, dw, ds


BIG = ("w_in", "w_uq", "w_ukv", "pool_w", "w_branch_a", "w_branch_b", "w_branch_c", "w_o", "w_up", "w_down")
REPLICATED = ("attn_norm", "q_lat_norm", "kv_lat_norm", "q_norm", "k_norm", "pool_scale", "mlp_norm")
WEIGHTS = ("meta_tokens", "attn_norm", "w_in", "conv_w", "q_lat_norm", "kv_lat_norm", "w_uq", "w_ukv", "q_norm",
           "k_norm", "pool_w", "pool_scale", "w_branch_a", "w_branch_b", "w_branch_c", "w_o", "mlp_norm", "w_up",
           "w_down")


def _pack(arrays):
    flat = jnp.concatenate([a.reshape(-1).astype(F32) for a in arrays])
    pad = (-flat.shape[0]) % (8 * LANES)
    return jnp.pad(flat, (0, pad)).reshape(-1, LANES)


def _unpack(flat, shapes):
    out, pos = [], 0
    flat = flat.reshape(-1)
    for shp in shapes:
        size = math.prod(shp)
        out.append(flat[pos:pos + size].reshape(shp))
        pos += size
    return out


def _update(w, g, m, v, name):
    shp = w.shape
    to2 = lambda a: a.reshape(-1, shp[-1])
    delta, nm, nv = _adamw(to2(w), to2(g), to2(m), to2(v), name=name)
    return delta.reshape(shp), nm.reshape(shp), nv.reshape(shp)


def _step(args):
    x = args["x"][0]
    seq, d = x.shape
    dm = _Dims(d, seq)
    xi, yi, ci = _coords()
    chip = 2 * xi + yi

    small_w = _gather_all(_pack([args["conv_w"], args["meta_tokens"]]), name="gather_small_weights")
    args = dict(args)
    for p in ("", "m_", "v_"):
        args[p + "w_in"] = jnp.swapaxes(args[p + "w_in"], 1, 2)
    order = [(k, l) for l in range(2) for k in BIG]
    shards = {n: args[n[0]][n[1]].astype(BF16) for n in order}
    for l in range(2):
        shards[("w_in", l)] = _split_cols(shards[("w_in", l)])
    small_w, shards[order[0]] = lax.optimization_barrier((small_w, shards[order[0]]))
    lands = {n: lax.empty((4,) + shards[n].shape, BF16) for n in order}
    last = ("w_up", "w_down")
    group_names = [[("w_in", 0)], [(k, 0) for k in BIG[1:] if k not in last], [(k, 0) for k in last],
                   [(k, 1) for k in BIG]]
    first, others = order[0], order[1:]
    sems, thru, token = _start_copies([shards[first], lands[first]], [_ici_gather_plan([(0, 1)])],
                                      name="start_gather_ici_first")
    shards[first], lands[first] = thru
    at = {n: i for i, n in enumerate(others)}
    sems_b, thru, token_b = _start_copies(
        [shards[n] for n in others] + [lands[n] for n in others] + [token],
        [_ici_gather_plan([(at[n], len(others) + at[n]) for n in g]) for g in group_names[1:]], name="start_gather_ici")
    sems = sems + sems_b
    for i, n in enumerate(others):
        shards[n], lands[n] = thru[i], thru[len(others) + i]

    def finish_gather(g, after, tag):
        names = group_names[g]
        k = len(names)
        plan, _ = _ici_gather_plan([(i, k + i) for i in range(k)])
        got = _wait_copies([shards[n] for n in names] + [lands[n] for n in names], sems[g], plan, after,
                           name=f"wait_gather_ici_{tag}")
        for i, n in enumerate(names):
            shards[n] = got[i]
        fwd = _d2d_forward_plan(list(range(k)))
        sems2, bufs2, tok2 = _start_copies(got[k:], [fwd], name=f"start_gather_d2d_{tag}")
        return names, bufs2, sems2[0], fwd[0], tok2

    def land_gather(pending, after, tag):
        names, bufs2, sems2, plan, tok2 = pending
        done = _wait_copies(bufs2, sems2, plan, tok2 if after is None else after, name=f"wait_gather_d2d_{tag}")
        return {n[0]: buf for n, buf in zip(names, done)}

    conv_shape, meta_shape = args["conv_w"].shape, args["meta_tokens"].shape
    per_chip = [_unpack(small_w[2 * j], [conv_shape, meta_shape]) for j in range(4)]
    conv_full = jnp.concatenate([p[0] for p in per_chip], axis=-1)
    meta_full = jnp.concatenate([p[1] for p in per_chip], axis=-1)

    layers = []
    for l in range(2):
        small = {k: args[k][l] for k in REPLICATED}
        small["conv_w"] = conv_full[l]
        layers.append(_small_weights(small))

    pos = jnp.arange(dm.t, dtype=F32)
    inv = ROPE_THETA ** (-jnp.arange(0, QK_ROPE, 2, dtype=F32) / QK_ROPE)
    ang = pos[:, None] * inv[None, :]
    cos_t = _rope_pad(jnp.cos(ang), jnp.cos(ang))
    sin_t = _rope_pad(-jnp.sin(ang), jnp.sin(ang))
    tail = jnp.zeros((dm.t - dm.t_real, d), F32)
    h0 = jnp.concatenate([meta_full, x, tail], axis=0)
    target = jnp.concatenate([jnp.zeros((N_META, d), F32), args["loss_target"][0], tail], axis=0)

    h_first = _rms_fwd(h0, layers[0]["attn_norm"], name="attn_norm_l0", after=(token, token_b))
    layers[0].update(_in_weights(dm, land_gather(finish_gather(0, h_first, "l0_in"), None, "l0_in")["w_in"]))
    def rest_of_layer0(point, after):
        g, tag = (1, "l0_mid") if point == "after_proj" else (2, "l0_mlp")
        return _other_weights(dm, land_gather(finish_gather(g, after, tag), None, tag))

    h1, saved0 = _layer_fwd(dm, layers[0], h0, cos_t, sin_t, "l0", more=rest_of_layer0, h=h_first)
    g1 = land_gather(finish_gather(3, saved0["y_b"], "l1"), h1, "l1")
    layers[1].update(_in_weights(dm, g1["w_in"]))
    layers[1].update(_other_weights(dm, g1))
    h2, saved1 = _layer_fwd(dm, layers[1], h1, cos_t, sin_t, "l1")
    sq, dy = _loss(h2, target, name="loss_head", first=N_META, last=dm.t_real)
    loss = lax.psum(0.5 / d * sq[0, 0], ("x", "y", "c"))
    core, chip_flags = _one_hot(ci, 2), _one_hot(chip, 4)

    class Reduce:
        def __init__(self, names, dw, tag):
            self.names, self.tag, self.nb = names, tag, len(names)
            self.idx = [(i, self.nb + i) for i in range(self.nb)]
            parts = [_as3d(_grad_piece(dm, dw, k)) for k in names]
            recv = [lax.empty((4, p.shape[1] // 2, p.shape[2]), BF16) for p in parts]
            self.plan = _swap_half_plan(self.idx)
            self.sems, self.bufs, self.token = _start_copies(parts + recv, [self.plan], name=f"start_swap_{tag}")

        def _land(self, after, what):
            return _wait_copies(self.bufs, self.sems[0], self.plan[0], self.token if after is None else after,
                                name=f"wait_{what}_{self.tag}")

        def scatter(self, after=None):
            got = self._land(after, "swap")
            pairs = [_pair_sum(got[i], got[j], core, name=f"pair_sum_{k}_{self.tag}")
                     for (i, j), k in zip(self.idx, self.names)]
            self.plan = _scatter_plan(self.idx)
            self.sems, self.bufs, self.token = _start_copies(pairs + [lax.empty(p.shape, BF16) for p in pairs],
                                                             [self.plan], name=f"start_scatter_{self.tag}")
            return self.token

        def totals(self, after=None):
            got = self._land(after, "scatter")
            sums = [_chip_sum(got[i], got[j], chip_flags, name=f"chip_sum_{k}_{self.tag}")
                    for (i, j), k in zip(self.idx, self.names)]
            self.plan = _swap_total_plan(self.idx)
            self.sems, self.bufs, self.token = _start_copies(sums + [lax.empty(t.shape, F32) for t in sums],
                                                             [self.plan], name=f"start_swap_total_{self.tag}")
            return self.token

        def finish(self, after=None):
            got = self._land(after, "swap_total")
            return {k: (got[i], got[j]) for (i, j), k in zip(self.idx, self.names)}

    dh1, dw1, ds1 = _layer_bwd(dm, layers[1], saved1, dy, cos_t, sin_t, "l1")
    early = ("w_down", "w_up", "w_o", "w_branch_a", "w_branch_b", "w_branch_c")
    late = tuple(k for k in BIG if k not in early)
    stage = {}

    def during_layer0(point, t, dw):
        if point == "start":
            stage["l1"] = Reduce(BIG, dw1, "l1")
            return (stage["l1"].token,)
        if point == "after_mlp":
            return (stage["l1"].scatter(after=t),)
        if point == "after_attention":
            tok = stage["l1"].totals(after=t)
            stage["l0a"] = Reduce(early, dw, "l0a")
            return (tok, stage["l0a"].token)
        if point == "after_qk":
            stage["red1"] = stage["l1"].finish(after=t)
            return (stage["l0a"].scatter(after=t),)
        if point == "after_dw_in":
            tok = stage["l0a"].totals(after=t)
            stage["l0b"] = Reduce(late, dw, "l0b")
            return (tok, stage["l0b"].token)
        return (stage["l0b"].scatter(after=t),)

    dh0, dw0, ds0 = _layer_bwd(dm, layers[0], saved0, dh1, cos_t, sin_t, "l0", hook=during_layer0)
    grad_x = dh0[N_META:dm.t_real][None]
    stage["l0b"].totals(after=dh0)
    red1 = stage["red1"]
    red0 = {**stage["l0a"].finish(), **stage["l0b"].finish()}
    grads = {}

    small_names = REPLICATED + ("conv_w",)
    small_parts = [jnp.stack([ds0[k].reshape(ds0[k].shape[-2:] if k == "conv_w" else (-1,)),
                              ds1[k].reshape(ds1[k].shape[-2:] if k == "conv_w" else (-1,))]) for k in small_names]
    small_parts.append(dh0[:N_META])
    small_all = _gather_all(_pack(small_parts), name="gather_small_grads")
    small_sum = _sum_stack(small_all, name="sum_small_grads", out_dtype=F32)
    small_g = dict(zip(small_names + ("meta_tokens",), _unpack(small_sum, [p.shape for p in small_parts])))
    for k in REPLICATED:
        grads[k] = small_g[k]
    dcw = conv_shape[-1]
    grads["conv_w"] = lax.dynamic_slice_in_dim(small_g["conv_w"], chip * dcw, dcw, axis=2)
    dmeta = meta_shape[-1]
    grads["meta_tokens"] = lax.dynamic_slice_in_dim(small_g["meta_tokens"], chip * dmeta, dmeta, axis=1)

    delta, new_m, new_v = {}, {}, {}
    for k in WEIGHTS:
        shp = args[k].shape
        if k in BIG:
            wmv = [args[p + k].reshape(2, -1, shp[-1]) for p in ("", "m_", "v_")]
            by_cols = k == "w_in"
            out = _adamw_layer(*wmv, *red1[k], core, 1, None, name=f"adamw_{k}_l1", col_halves=by_cols)
            out = _adamw_layer(*wmv, *red0[k], core, 0, out, name=f"adamw_{k}_l0", col_halves=by_cols)
            out = [o.reshape(shp) for o in out]
            grads[k], delta[k], new_m[k], new_v[k] = [jnp.swapaxes(o, 1, 2) for o in out] if by_cols else out
        else:
            grads[k] = grads[k].reshape(shp)
            delta[k], new_m[k], new_v[k] = _update(args[k], grads[k], args["m_" + k], args["v_" + k], f"adamw_{k}")
    return (loss, grad_x, *[grads[k] for k in WEIGHTS], *[delta[k] for k in WEIGHTS],
            *[new_m[k] for k in WEIGHTS], *[new_v[k] for k in WEIGHTS])


def kernel(x, meta_tokens, attn_norm, w_in, conv_w, q_lat_norm, kv_lat_norm, w_uq, w_ukv, q_norm, k_norm, pool_w, pool_scale, w_branch_a, w_branch_b, w_branch_c, w_o, mlp_norm, w_up, w_down, loss_target, m_meta_tokens, m_attn_norm, m_w_in, m_conv_w, m_q_lat_norm, m_kv_lat_norm, m_w_uq, m_w_ukv, m_q_norm, m_k_norm, m_pool_w, m_pool_scale, m_w_branch_a, m_w_branch_b, m_w_branch_c, m_w_o, m_mlp_norm, m_w_up, m_w_down, v_meta_tokens, v_attn_norm, v_w_in, v_conv_w, v_q_lat_norm, v_kv_lat_norm, v_w_uq, v_w_ukv, v_q_norm, v_k_norm, v_pool_w, v_pool_scale, v_w_branch_a, v_w_branch_b, v_w_branch_c, v_w_o, v_mlp_norm, v_w_up, v_w_down):
    return _step(dict(locals()))
```

```python
import functools
import math

import jax
import jax.numpy as jnp
from jax import lax
from jax.experimental import pallas as pl
from jax.experimental.pallas import tpu as pltpu

F32 = jnp.float32
BF16 = jnp.bfloat16
MESH = pl.DeviceIdType.MESH

EPS = 1e-6
N_META = 16
QK_NOPE = 128
QK_ROPE = 64
QK_HEAD = QK_NOPE + QK_ROPE
V_HEAD = 128
HEAD_PAD = 256
Q_LORA = 512
KV_LORA = 512
ROPE_THETA = 10000.0
POOL_WINDOWS = (2, 4, 8, 16)
HALO = 16
LANES = 128
ADAM_LR = 0.001
ADAM_B1 = 0.9
ADAM_B2 = 0.999
ADAM_EPS = 1e-08
ADAM_WD = 0.01
ADAM_STEP = 10
VMEM_LIMIT = 52 * 1024 * 1024
NEG = -1e30


def _tile(n, target, mult=LANES):
    best = None
    for t in range(mult, min(n, target) + 1, mult):
        if n % t == 0:
            best = t
    return n if best is None else best


def _params(sem=None):
    return pltpu.CompilerParams(dimension_semantics=sem, vmem_limit_bytes=VMEM_LIMIT)


def _mm(a, b, *, name, ta=False, tb=False, add=None, aux=None, epi=None, out_dtype=F32,
        tm=704, tn=1024, tk=None, after=()):
    if ta:
        K, M = a.shape
    else:
        M, K = a.shape
    if tb:
        N, kb = b.shape
    else:
        kb, N = b.shape
    assert K == kb, (a.shape, b.shape, ta, tb)
    tm = _tile(M, tm, LANES if ta else 16)
    tn = _tile(N, tn, LANES)
    tk = K if tk is None else _tile(K, tk, LANES if (not ta or tb) else 16)
    nk = K // tk
    grid = (M // tm, N // tn, nk)

    a_spec = pl.BlockSpec((tk, tm), lambda i, j, k: (k, i)) if ta else pl.BlockSpec((tm, tk), lambda i, j, k: (i, k))
    b_spec = pl.BlockSpec((tn, tk), lambda i, j, k: (j, k)) if tb else pl.BlockSpec((tk, tn), lambda i, j, k: (k, j))
    o_spec = pl.BlockSpec((tm, tn), lambda i, j, k: (i, j))
    in_specs = [a_spec, b_spec]
    operands = [a, b]
    if add is not None:
        in_specs.append(o_spec)
        operands.append(add)
    if aux is not None:
        in_specs.append(o_spec)
        operands.append(aux)
    after = tuple(after)
    in_specs += [pl.BlockSpec(memory_space=pl.ANY)] * len(after)
    operands += list(after)
    if epi == "relu2":
        out_shape = (jax.ShapeDtypeStruct((M, N), BF16), jax.ShapeDtypeStruct((M, N), BF16))
        out_specs = (o_spec, o_spec)
    else:
        out_shape = jax.ShapeDtypeStruct((M, N), out_dtype)
        out_specs = o_spec
    dims = (((0 if ta else 1,), (1 if tb else 0,)), ((), ()))
    has_add, has_aux = add is not None, aux is not None

    def body(*refs):
        a_ref, b_ref = refs[0], refs[1]
        pos = 2
        add_ref = aux_ref = None
        if has_add:
            add_ref = refs[pos]
            pos += 1
        if has_aux:
            aux_ref = refs[pos]
            pos += 1
        pos += len(after)
        n_out = 2 if epi == "relu2" else 1
        out_refs = refs[pos:pos + n_out]
        acc_ref = refs[pos + n_out] if nk > 1 else None

        part = lax.dot_general(a_ref[...].astype(BF16), b_ref[...].astype(BF16), dims,
                               preferred_element_type=F32)

        def finish(acc):
            if has_add:
                acc = acc + add_ref[...].astype(F32)
            if epi == "relu2":
                r = jnp.maximum(acc, 0.0)
                out_refs[0][...] = acc.astype(BF16)
                out_refs[1][...] = (r * r).astype(BF16)
            elif epi == "drelu2":
                u = aux_ref[...].astype(F32)
                out_refs[0][...] = (acc * (2.0 * jnp.maximum(u, 0.0))).astype(out_dtype)
            else:
                out_refs[0][...] = acc.astype(out_dtype)

        if nk == 1:
            finish(part)
        else:
            k = pl.program_id(2)

            @pl.when(k == 0)
            def _():
                acc_ref[...] = part

            @pl.when(k > 0)
            def _():
                acc_ref[...] += part

            @pl.when(k == nk - 1)
            def _():
                finish(acc_ref[...])

    scratch = [pltpu.VMEM((tm, tn), F32)] if nk > 1 else []
    return pl.pallas_call(
        body, name=name, grid=grid, in_specs=in_specs, out_specs=out_specs, out_shape=out_shape,
        scratch_shapes=scratch, compiler_params=_params(("parallel", "parallel", "arbitrary")),
    )(*operands)


def _rms_fwd(x, g, *, name, width=None, seg=0, tm=384, after=()):
    T = x.shape[0]
    width = x.shape[1] if width is None else width
    tm = _tile(T, tm, 16)
    after = tuple(after)

    def body(x_ref, g_ref, *rest):
        xf = x_ref[...].astype(F32)
        r = lax.rsqrt(jnp.mean(xf * xf, axis=-1, keepdims=True) + EPS)
        rest[-1][...] = (xf * r * g_ref[...]).astype(BF16)

    return pl.pallas_call(
        body, name=name, grid=(T // tm,),
        in_specs=[pl.BlockSpec((tm, width), lambda i: (i, seg)), pl.BlockSpec((1, width), lambda i: (0, 0))]
        + [pl.BlockSpec(memory_space=pl.ANY)] * len(after),
        out_specs=pl.BlockSpec((tm, width), lambda i: (i, 0)),
        out_shape=jax.ShapeDtypeStruct((T, width), BF16),
        compiler_params=_params(("parallel",)),
    )(x, g, *after)


def _rms_bwd(dy, x, g, *, name, width=None, seg=0, res=None, out_dtype=F32, tm=384):
    T = x.shape[0]
    width = x.shape[1] if width is None else width
    tm = _tile(T, tm, 16)
    has_res = res is not None

    def body(*refs):
        dy_ref, x_ref, g_ref = refs[:3]
        res_ref = refs[3] if has_res else None
        dx_ref, dg_ref = refs[-2:]
        xf = x_ref[...].astype(F32)
        dyf = dy_ref[...].astype(F32)
        r = lax.rsqrt(jnp.mean(xf * xf, axis=-1, keepdims=True) + EPS)
        xhat = xf * r
        dyh = dyf * g_ref[...]
        dx = r * (dyh - xhat * jnp.mean(dyh * xhat, axis=-1, keepdims=True))
        if has_res:
            dx = dx + res_ref[...].astype(F32)
        dx_ref[...] = dx.astype(out_dtype)
        part = jnp.sum(dyf * xhat, axis=0, keepdims=True)

        @pl.when(pl.program_id(0) == 0)
        def _():
            dg_ref[...] = part

        @pl.when(pl.program_id(0) > 0)
        def _():
            dg_ref[...] += part

    row = pl.BlockSpec((tm, width), lambda i: (i, 0))
    in_specs = [row, pl.BlockSpec((tm, width), lambda i: (i, seg)), pl.BlockSpec((1, width), lambda i: (0, 0))]
    operands = [dy, x, g]
    if has_res:
        in_specs.append(row)
        operands.append(res)
    return pl.pallas_call(
        body, name=name, grid=(T // tm,), in_specs=in_specs,
        out_specs=(row, pl.BlockSpec((1, width), lambda i: (0, 0))),
        out_shape=(jax.ShapeDtypeStruct((T, width), out_dtype), jax.ShapeDtypeStruct((1, width), F32)),
        compiler_params=_params(("arbitrary",)),
    )(*operands)


def _down(ext, k):
    return pltpu.roll(ext, k, 0)


def _up(ext, k):
    return pltpu.roll(ext, ext.shape[0] - k, 0)


def _pre_halo(ref, r, R):
    start = pl.multiple_of(jnp.maximum(r * R - HALO, 0), 8)
    keep = (r > 0).astype(F32)
    return ref[pl.ds(start, HALO), :].astype(F32) * keep


def _post_halo(ref, r, R, n_chunks):
    start = pl.multiple_of(jnp.minimum(r * R + R, (n_chunks - 1) * R + R - HALO), 8)
    keep = (r < n_chunks - 1).astype(F32)
    return ref[pl.ds(start, HALO), :].astype(F32) * keep


def _chunk(ref, r, R):
    return ref[pl.ds(pl.multiple_of(r * R, 8), R), :].astype(F32)


def _conv_fwd(rest, conv_w, *, name, dc, tc=128, rows=1056):
    T = rest.shape[0]
    tc = _tile(dc, tc)
    nb = dc // tc
    R = _tile(T, rows, 16)
    n_chunks = T // R

    def body(u_ref, b_ref, c_ref, w_ref, y_ref):
        w0, w1, w2 = w_ref[0:1, :], w_ref[1:2, :], w_ref[2:3, :]

        def chunk(r, carry):
            cu = _chunk(c_ref, r, R) * _chunk(u_ref, r, R)
            ext = jnp.concatenate([_pre_halo(c_ref, r, R) * _pre_halo(u_ref, r, R), cu], axis=0)
            conv = w0 * _down(ext, 2)[HALO:] + w1 * _down(ext, 1)[HALO:] + w2 * cu
            y_ref[pl.ds(pl.multiple_of(r * R, 8), R), :] = (_chunk(b_ref, r, R) * conv).astype(BF16)
            return carry

        lax.fori_loop(0, n_chunks, chunk, 0)

    col = lambda off: pl.BlockSpec((T, tc), lambda j: (0, off * nb + j))
    return pl.pallas_call(
        body, name=name, grid=(nb,),
        in_specs=[col(0), col(1), col(2), pl.BlockSpec((3, tc), lambda j: (0, j))],
        out_specs=pl.BlockSpec((T, tc), lambda j: (0, j)),
        out_shape=jax.ShapeDtypeStruct((T, dc), BF16),
        compiler_params=_params(("parallel",)),
    )(rest, rest, rest, conv_w)


def _conv_bwd(rest, conv_w, dy, *, name, dc, tc=128, rows=1056):
    T = rest.shape[0]
    tc = _tile(dc, tc)
    nb = dc // tc
    R = _tile(T, rows, 16)
    n_chunks = T // R

    def body(u_ref, b_ref, c_ref, w_ref, dy_ref, du_ref, db_ref, dc_ref, dw_ref):
        w0, w1, w2 = w_ref[0:1, :], w_ref[1:2, :], w_ref[2:3, :]

        def chunk(r, carry):
            a0, a1, a2 = carry
            u, b, c = _chunk(u_ref, r, R), _chunk(b_ref, r, R), _chunk(c_ref, r, R)
            dy_c = _chunk(dy_ref, r, R)
            cu = c * u
            ext = jnp.concatenate([_pre_halo(c_ref, r, R) * _pre_halo(u_ref, r, R), cu], axis=0)
            cu1, cu2 = _down(ext, 1)[HALO:], _down(ext, 2)[HALO:]
            conv = w0 * cu2 + w1 * cu1 + w2 * cu
            dconv = dy_c * b
            dext = jnp.concatenate(
                [dconv, _post_halo(dy_ref, r, R, n_chunks) * _post_halo(b_ref, r, R, n_chunks)], axis=0)
            dcu = w2 * dconv + w1 * _up(dext, 1)[:R] + w0 * _up(dext, 2)[:R]
            rows_at = pl.ds(pl.multiple_of(r * R, 8), R)
            db_ref[rows_at, :] = (dy_c * conv).astype(BF16)
            du_ref[rows_at, :] = (dcu * c).astype(BF16)
            dc_ref[rows_at, :] = (dcu * u).astype(BF16)
            return (a0 + jnp.sum(dconv * cu2, axis=0, keepdims=True),
                    a1 + jnp.sum(dconv * cu1, axis=0, keepdims=True),
                    a2 + jnp.sum(dconv * cu, axis=0, keepdims=True))

        zero = jnp.zeros((1, tc), F32)
        a0, a1, a2 = lax.fori_loop(0, n_chunks, chunk, (zero, zero, zero))
        dw_ref[0:1, :] = a0
        dw_ref[1:2, :] = a1
        dw_ref[2:3, :] = a2

    col = lambda off: pl.BlockSpec((T, tc), lambda j: (0, off * nb + j))
    own = pl.BlockSpec((T, tc), lambda j: (0, j))
    return pl.pallas_call(
        body, name=name, grid=(nb,),
        in_specs=[col(0), col(1), col(2), pl.BlockSpec((3, tc), lambda j: (0, j)), own],
        out_specs=(own, own, own, pl.BlockSpec((3, tc), lambda j: (0, j))),
        out_shape=(jax.ShapeDtypeStruct((T, dc), BF16),) * 3 + (jax.ShapeDtypeStruct((3, dc), F32),),
        compiler_params=_params(("parallel",)),
    )(rest, rest, rest, conv_w, dy)


def _window_count(r, R, n_rows, w, first_row_offset):
    t = lax.broadcasted_iota(jnp.int32, (n_rows, 1), 0) + (r * R + first_row_offset)
    return jnp.minimum(t + 1, w).astype(F32)


def _pool_fwd(rest, pool_w, pool_scale, *, name, seg0, pg, rows=1056):
    T = rest.shape[0]
    R = _tile(T, rows, 16)
    n_chunks = T // R
    n_groups = len(POOL_WINDOWS)

    def body(x_ref, w_ref, s_ref, y_ref):
        def run(window):
            def chunk(r, carry):
                g = _chunk(x_ref, r, R)
                s = jnp.concatenate([_pre_halo(x_ref, r, R), g], axis=0)
                k = 1
                while k < window:
                    s = s + _down(s, k)
                    k *= 2
                pooled = s[HALO:] / _window_count(r, R, R, window, 0) - g
                mixed = jnp.dot(pooled.astype(BF16), w_ref[0], preferred_element_type=F32)
                y_ref[pl.ds(pl.multiple_of(r * R, 8), R), :] = (mixed * s_ref[...]).astype(BF16)
                return carry

            lax.fori_loop(0, n_chunks, chunk, 0)

        for gi, window in enumerate(POOL_WINDOWS):
            pl.when(pl.program_id(0) == gi)(functools.partial(run, window))

    return pl.pallas_call(
        body, name=name, grid=(n_groups,),
        in_specs=[pl.BlockSpec((T, pg), lambda g: (0, seg0 + g)),
                  pl.BlockSpec((1, pg, pg), lambda g: (g, 0, 0)),
                  pl.BlockSpec((1, pg), lambda g: (0, g))],
        out_specs=pl.BlockSpec((T, pg), lambda g: (0, g)),
        out_shape=jax.ShapeDtypeStruct((T, n_groups * pg), BF16),
        compiler_params=_params(("parallel",)),
    )(rest, pool_w, pool_scale)


def _pool_bwd(rest, pool_w, pool_scale, dy, *, name, seg0, pg, rows=1056):
    T = rest.shape[0]
    R = _tile(T, rows, 16)
    n_chunks = T // R
    n_groups = len(POOL_WINDOWS)

    def body(x_ref, w_ref, s_ref, dy_ref, dx_ref, dw_ref, ds_ref):
        def run(window):
            def chunk(r, carry):
                dw_acc, ds_acc = carry
                g = _chunk(x_ref, r, R)
                s = jnp.concatenate([_pre_halo(x_ref, r, R), g], axis=0)
                k = 1
                while k < window:
                    s = s + _down(s, k)
                    k *= 2
                pooled = (s[HALO:] / _window_count(r, R, R, window, 0) - g).astype(BF16)
                mixed = jnp.dot(pooled, w_ref[0], preferred_element_type=F32)
                dy_c = _chunk(dy_ref, r, R)
                dm_ext = (jnp.concatenate([dy_c, _post_halo(dy_ref, r, R, n_chunks)], axis=0)
                          * s_ref[...]).astype(BF16)
                dpool_ext = lax.dot_general(dm_ext, w_ref[0], (((1,), (1,)), ((), ())),
                                            preferred_element_type=F32)
                a = dpool_ext / _window_count(r, R, R + HALO, window, 0)
                k = 1
                while k < window:
                    a = a + _up(a, k)
                    k *= 2
                dx_ref[pl.ds(pl.multiple_of(r * R, 8), R), :] = (a[:R] - dpool_ext[:R]).astype(BF16)
                dw_acc = dw_acc + lax.dot_general(pooled, dm_ext[:R], (((0,), (0,)), ((), ())),
                                                  preferred_element_type=F32)
                ds_acc = ds_acc + jnp.sum(dy_c * mixed, axis=0, keepdims=True)
                return dw_acc, ds_acc

            dw_acc, ds_acc = lax.fori_loop(0, n_chunks, chunk,
                                           (jnp.zeros((pg, pg), F32), jnp.zeros((1, pg), F32)))
            dw_ref[0] = dw_acc
            ds_ref[...] = ds_acc

        for gi, window in enumerate(POOL_WINDOWS):
            pl.when(pl.program_id(0) == gi)(functools.partial(run, window))

    own = pl.BlockSpec((T, pg), lambda g: (0, g))
    return pl.pallas_call(
        body, name=name, grid=(n_groups,),
        in_specs=[pl.BlockSpec((T, pg), lambda g: (0, seg0 + g)),
                  pl.BlockSpec((1, pg, pg), lambda g: (g, 0, 0)),
                  pl.BlockSpec((1, pg), lambda g: (0, g)), own],
        out_specs=(own, pl.BlockSpec((1, pg, pg), lambda g: (g, 0, 0)), pl.BlockSpec((1, pg), lambda g: (0, g))),
        out_shape=(jax.ShapeDtypeStruct((T, n_groups * pg), BF16),
                   jax.ShapeDtypeStruct((n_groups, pg, pg), F32),
                   jax.ShapeDtypeStruct((1, n_groups * pg), F32)),
        compiler_params=_params(("parallel",)),
    )(rest, pool_w, pool_scale, dy)


def _rope(r, cos_t, sin_t):
    return r * cos_t + pltpu.roll(r, LANES // 2, 1) * sin_t


def _rope_t(d, cos_t, sin_t):
    return d * cos_t + pltpu.roll(d * sin_t, LANES // 2, 1)


def _qk_fwd(q_raw, k_nope, rest, cos_t, sin_t, q_norm, k_norm, *, name, heads, kr_seg, tm=192):
    T = q_raw.shape[0]
    tm = _tile(T, tm, 16)

    def body(q_ref, kn_ref, kr_ref, c_ref, s_ref, gq_ref, gk_ref, qo_ref, ko_ref):
        cos_b, sin_b = c_ref[...], s_ref[...]
        kr = kr_ref[:, 0:LANES]
        kr_ss = jnp.sum(kr * kr, axis=-1, keepdims=True)
        gq, gk = gq_ref[...], gk_ref[...]
        for h in range(heads):
            lo = h * HEAD_PAD
            q = q_ref[:, lo:lo + HEAD_PAD]
            rq = lax.rsqrt(jnp.sum(q * q, axis=-1, keepdims=True) / QK_HEAD + EPS)
            qn = q * rq * gq
            qo_ref[:, lo:lo + LANES] = qn[:, :LANES].astype(BF16)
            qo_ref[:, lo + LANES:lo + HEAD_PAD] = _rope(qn[:, LANES:], cos_b, sin_b).astype(BF16)
            kn = kn_ref[:, h * LANES:(h + 1) * LANES]
            rk = lax.rsqrt((jnp.sum(kn * kn, axis=-1, keepdims=True) + kr_ss) / QK_HEAD + EPS)
            ko_ref[:, lo:lo + LANES] = (kn * rk * gk[:, :LANES]).astype(BF16)
            ko_ref[:, lo + LANES:lo + HEAD_PAD] = _rope(kr * rk * gk[:, LANES:], cos_b, sin_b).astype(BF16)

    wq, wk = heads * HEAD_PAD, heads * LANES
    return pl.pallas_call(
        body, name=name, grid=(T // tm,),
        in_specs=[pl.BlockSpec((tm, wq), lambda i: (i, 0)), pl.BlockSpec((tm, wk), lambda i: (i, 0)),
                  pl.BlockSpec((tm, HEAD_PAD), lambda i: (i, kr_seg)),
                  pl.BlockSpec((tm, LANES), lambda i: (i, 0)), pl.BlockSpec((tm, LANES), lambda i: (i, 0)),
                  pl.BlockSpec((1, HEAD_PAD), lambda i: (0, 0)), pl.BlockSpec((1, HEAD_PAD), lambda i: (0, 0))],
        out_specs=(pl.BlockSpec((tm, wq), lambda i: (i, 0)), pl.BlockSpec((tm, wq), lambda i: (i, 0))),
        out_shape=(jax.ShapeDtypeStruct((T, wq), BF16), jax.ShapeDtypeStruct((T, wq), BF16)),
        compiler_params=_params(("parallel",)),
    )(q_raw, k_nope, rest, cos_t, sin_t, q_norm, k_norm)


def _qk_bwd(dq, dk, q_raw, k_nope, rest, cos_t, sin_t, q_norm, k_norm, *, name, heads, kr_seg, tm=128):
    T = q_raw.shape[0]
    tm = _tile(T, tm, 16)

    def body(dq_ref, dk_ref, q_ref, kn_ref, kr_ref, c_ref, s_ref, gq_ref, gk_ref,
             dqr_ref, dkn_ref, dkr_ref, dgq_ref, dgk_ref):
        cos_b, sin_b = c_ref[...], s_ref[...]
        kr = kr_ref[:, 0:LANES]
        kr_ss = jnp.sum(kr * kr, axis=-1, keepdims=True)
        gq, gk = gq_ref[...], gk_ref[...]
        dgq = jnp.zeros((1, HEAD_PAD), F32)
        dgk_n = jnp.zeros((1, LANES), F32)
        dgk_r = jnp.zeros((1, LANES), F32)
        dkr = jnp.zeros((tm, LANES), F32)
        for h in range(heads):
            lo = h * HEAD_PAD
            q = q_ref[:, lo:lo + HEAD_PAD]
            rq = lax.rsqrt(jnp.sum(q * q, axis=-1, keepdims=True) / QK_HEAD + EPS)
            qhat = q * rq
            dqn = jnp.concatenate([dq_ref[:, lo:lo + LANES],
                                   _rope_t(dq_ref[:, lo + LANES:lo + HEAD_PAD], cos_b, sin_b)], axis=1)
            dgq = dgq + jnp.sum(dqn * qhat, axis=0, keepdims=True)
            dqh = dqn * gq
            dqr_ref[:, lo:lo + HEAD_PAD] = (
                rq * (dqh - qhat * (jnp.sum(dqh * qhat, axis=-1, keepdims=True) / QK_HEAD))).astype(BF16)
            kn = kn_ref[:, h * LANES:(h + 1) * LANES]
            rk = lax.rsqrt((jnp.sum(kn * kn, axis=-1, keepdims=True) + kr_ss) / QK_HEAD + EPS)
            khat_n, khat_r = kn * rk, kr * rk
            dkn_n = dk_ref[:, lo:lo + LANES]
            dkn_r = _rope_t(dk_ref[:, lo + LANES:lo + HEAD_PAD], cos_b, sin_b)
            dgk_n = dgk_n + jnp.sum(dkn_n * khat_n, axis=0, keepdims=True)
            dgk_r = dgk_r + jnp.sum(dkn_r * khat_r, axis=0, keepdims=True)
            dkh_n, dkh_r = dkn_n * gk[:, :LANES], dkn_r * gk[:, LANES:]
            proj = (jnp.sum(dkh_n * khat_n, axis=-1, keepdims=True)
                    + jnp.sum(dkh_r * khat_r, axis=-1, keepdims=True)) / QK_HEAD
            dkn_ref[:, h * LANES:(h + 1) * LANES] = (rk * (dkh_n - khat_n * proj)).astype(BF16)
            dkr = dkr + rk * (dkh_r - khat_r * proj)
        dkr_ref[:, 0:LANES] = dkr.astype(BF16)
        dkr_ref[:, LANES:HEAD_PAD] = jnp.zeros((tm, HEAD_PAD - LANES), BF16)
        dgk = jnp.concatenate([dgk_n, dgk_r], axis=1)

        @pl.when(pl.program_id(0) == 0)
        def _():
            dgq_ref[...] = dgq
            dgk_ref[...] = dgk

        @pl.when(pl.program_id(0) > 0)
        def _():
            dgq_ref[...] += dgq
            dgk_ref[...] += dgk

    wq, wk = heads * HEAD_PAD, heads * LANES
    row = lambda w: pl.BlockSpec((tm, w), lambda i: (i, 0))
    vec = pl.BlockSpec((1, HEAD_PAD), lambda i: (0, 0))
    return pl.pallas_call(
        body, name=name, grid=(T // tm,),
        in_specs=[row(wq), row(wq), row(wq), row(wk), pl.BlockSpec((tm, HEAD_PAD), lambda i: (i, kr_seg)),
                  row(LANES), row(LANES), vec, vec],
        out_specs=(row(wq), row(wk), row(HEAD_PAD), vec, vec),
        out_shape=(jax.ShapeDtypeStruct((T, wq), BF16), jax.ShapeDtypeStruct((T, wk), BF16),
                   jax.ShapeDtypeStruct((T, HEAD_PAD), BF16),
                   jax.ShapeDtypeStruct((1, HEAD_PAD), F32), jax.ShapeDtypeStruct((1, HEAD_PAD), F32)),
        compiler_params=_params(("arbitrary",)),
    )(dq, dk, q_raw, k_nope, rest, cos_t, sin_t, q_norm, k_norm)


def _causal_mask(s):
    row = lax.broadcasted_iota(jnp.int32, s.shape, 0)
    col = lax.broadcasted_iota(jnp.int32, s.shape, 1)
    return jnp.where(row >= col, s, NEG)


def _flash_fwd(q, k, v, *, name, heads, tq=384, hp=2):
    T = q.shape[0]
    tq = _tile(T, tq, LANES)
    nq = T // tq
    scale = QK_HEAD ** -0.5
    nt = (((1,), (1,)), ((), ()))

    def body(q_ref, k_ref, v_ref, o_ref, lse_ref):
        def q_block(i, carry):
            q_at = pl.ds(pl.multiple_of(i * tq, tq), tq)
            qbs = [q_ref[q_at, h * HEAD_PAD:(h + 1) * HEAD_PAD] for h in range(hp)]

            def step(j, state, masked):
                k_at = pl.ds(pl.multiple_of(j * tq, tq), tq)
                new = []
                for h in range(hp):
                    m, l, acc = state[h]
                    s = lax.dot_general(qbs[h], k_ref[k_at, h * HEAD_PAD:(h + 1) * HEAD_PAD], nt,
                                        preferred_element_type=F32) * scale
                    if masked:
                        s = _causal_mask(s)
                    m_new = jnp.maximum(m, jnp.max(s, axis=-1, keepdims=True))
                    p = jnp.exp(s - m_new)
                    alpha = jnp.exp(m - m_new)
                    l = alpha * l + jnp.sum(p, axis=-1, keepdims=True)
                    acc = alpha * acc + jnp.dot(p.astype(BF16), v_ref[k_at, h * V_HEAD:(h + 1) * V_HEAD],
                                                preferred_element_type=F32)
                    new.append((m_new, l, acc))
                return tuple(new)

            init = tuple((jnp.full((tq, 1), NEG, F32), jnp.zeros((tq, 1), F32), jnp.zeros((tq, V_HEAD), F32))
                         for _ in range(hp))
            state = lax.fori_loop(0, i, lambda j, st: step(j, st, False), init)
            state = step(i, state, True)
            for h in range(hp):
                m, l, acc = state[h]
                o_ref[q_at, h * V_HEAD:(h + 1) * V_HEAD] = (acc / l).astype(BF16)
                lse_ref[h, q_at, :] = jnp.broadcast_to(m + jnp.log(l), (tq, LANES))
            return carry

        lax.fori_loop(0, nq, q_block, 0)

    qk_spec = pl.BlockSpec((T, hp * HEAD_PAD), lambda g: (0, g))
    v_spec = pl.BlockSpec((T, hp * V_HEAD), lambda g: (0, g))
    return pl.pallas_call(
        body, name=name, grid=(heads // hp,), in_specs=[qk_spec, qk_spec, v_spec],
        out_specs=(v_spec, pl.BlockSpec((hp, T, LANES), lambda g: (g, 0, 0))),
        out_shape=(jax.ShapeDtypeStruct((T, heads * V_HEAD), BF16), jax.ShapeDtypeStruct((heads, T, LANES), F32)),
        compiler_params=_params(("parallel",)),
    )(q, k, v)


def _flash_bwd(q, k, v, o, do, lse, *, name, heads, tq=384):
    T = q.shape[0]
    tq = _tile(T, tq, LANES)
    nq = T // tq
    scale = QK_HEAD ** -0.5
    nt = (((1,), (1,)), ((), ()))
    tn = (((0,), (0,)), ((), ()))

    def body(q_ref, k_ref, v_ref, o_ref, do_ref, lse_ref, dq_ref, dk_ref, dv_ref, delta_ref):
        def fill_delta(i, carry):
            at = pl.ds(pl.multiple_of(i * tq, tq), tq)
            d = jnp.sum(o_ref[at, :].astype(F32) * do_ref[at, :].astype(F32), axis=-1, keepdims=True)
            delta_ref[at, :] = jnp.broadcast_to(d, (tq, LANES))
            dq_ref[at, :] = jnp.zeros((tq, HEAD_PAD), F32)
            return carry

        lax.fori_loop(0, nq, fill_delta, 0)

        def kv_block(j, carry):
            k_at = pl.ds(pl.multiple_of(j * tq, tq), tq)
            kb, vb = k_ref[k_at, :], v_ref[k_at, :]

            def step(i, state, masked):
                dk_acc, dv_acc = state
                q_at = pl.ds(pl.multiple_of(i * tq, tq), tq)
                qb, dob = q_ref[q_at, :], do_ref[q_at, :]
                s = lax.dot_general(qb, kb, nt, preferred_element_type=F32) * scale
                if masked:
                    s = _causal_mask(s)
                p = jnp.exp(s - lse_ref[0, q_at, :][:, 0:1])
                dv_acc = dv_acc + lax.dot_general(p.astype(BF16), dob, tn, preferred_element_type=F32)
                dp = lax.dot_general(dob, vb, nt, preferred_element_type=F32)
                ds = (p * (dp - delta_ref[q_at, :][:, 0:1]) * scale).astype(BF16)
                dk_acc = dk_acc + lax.dot_general(ds, qb, tn, preferred_element_type=F32)
                dq_ref[q_at, :] += jnp.dot(ds, kb, preferred_element_type=F32)
                return dk_acc, dv_acc

            state = step(j, (jnp.zeros((tq, HEAD_PAD), F32), jnp.zeros((tq, V_HEAD), F32)), True)
            rest = nq - 1 - j

            def two_steps(t, st):
                i0 = j + 1 + 2 * t
                return step(i0 + 1, step(i0, st, False), False)

            state = lax.fori_loop(0, rest // 2, two_steps, state)
            dk_acc, dv_acc = lax.cond(rest % 2 == 1, lambda st: step(nq - 1, st, False), lambda st: st, state)
            dk_ref[k_at, :] = dk_acc
            dv_ref[k_at, :] = dv_acc.astype(BF16)
            return carry

        lax.fori_loop(0, nq, kv_block, 0)

    qk_spec = pl.BlockSpec((T, HEAD_PAD), lambda h: (0, h))
    v_spec = pl.BlockSpec((T, V_HEAD), lambda h: (0, h))
    return pl.pallas_call(
        body, name=name, grid=(heads,),
        in_specs=[qk_spec, qk_spec, v_spec, v_spec, v_spec, pl.BlockSpec((1, T, LANES), lambda h: (h, 0, 0))],
        out_specs=(qk_spec, qk_spec, v_spec),
        out_shape=(jax.ShapeDtypeStruct((T, heads * HEAD_PAD), F32), jax.ShapeDtypeStruct((T, heads * HEAD_PAD), F32),
                   jax.ShapeDtypeStruct((T, heads * V_HEAD), BF16)),
        scratch_shapes=[pltpu.VMEM((T, LANES), F32)],
        compiler_params=_params(("parallel",)),
    )(q, k, v, o, do, lse)


def _merge_fwd(gl, pa, pb, pc, *, name, d, tm=384, tn=1024):
    T = pa.shape[0]
    tm, tn = _tile(T, tm, 16), _tile(d, tn)
    nb = d // tn

    def body(g0, g1, g2, a, b, c, o_ref):
        o_ref[...] = (jax.nn.sigmoid(g0[...]) * a[...] + jax.nn.sigmoid(g1[...]) * b[...]
                      + jax.nn.sigmoid(g2[...]) * c[...]).astype(BF16)

    gate = lambda n: pl.BlockSpec((tm, tn), lambda i, j: (i, n * nb + j))
    blk = pl.BlockSpec((tm, tn), lambda i, j: (i, j))
    return pl.pallas_call(
        body, name=name, grid=(T // tm, nb), in_specs=[gate(0), gate(1), gate(2), blk, blk, blk],
        out_specs=blk, out_shape=jax.ShapeDtypeStruct((T, d), BF16),
        compiler_params=_params(("parallel", "parallel")),
    )(gl, gl, gl, pa, pb, pc)


def _merge_bwd(dm, gl, pa, pb, pc, *, name, d, tm=384, tn=1024):
    T = pa.shape[0]
    tm, tn = _tile(T, tm, 16), _tile(d, tn)
    nb = d // tn

    def body(dm_ref, g0, g1, g2, a, b, c, da, db, dc, dg0, dg1, dg2):
        dmv = dm_ref[...]
        for g_ref, p_ref, dp_ref, dg_ref in ((g0, a, da, dg0), (g1, b, db, dg1), (g2, c, dc, dg2)):
            sg = jax.nn.sigmoid(g_ref[...])
            dp_ref[...] = (dmv * sg).astype(BF16)
            dg_ref[...] = (dmv * p_ref[...] * sg * (1.0 - sg)).astype(BF16)

    gate = lambda n: pl.BlockSpec((tm, tn), lambda i, j: (i, n * nb + j))
    blk = pl.BlockSpec((tm, tn), lambda i, j: (i, j))
    return pl.pallas_call(
        body, name=name, grid=(T // tm, nb), in_specs=[blk, gate(0), gate(1), gate(2), blk, blk, blk],
        out_specs=(blk,) * 6, out_shape=(jax.ShapeDtypeStruct((T, d), BF16),) * 6,
        compiler_params=_params(("parallel", "parallel")),
    )(dm, gl, gl, gl, pa, pb, pc)


def _loss(y, target, *, name, first, last, tm=384):
    T, d = y.shape
    tm = _tile(T, tm, 16)

    def body(y_ref, t_ref, loss_ref, dy_ref):
        i = pl.program_id(0)
        row = lax.broadcasted_iota(jnp.int32, (tm, 1), 0) + i * tm
        real = jnp.logical_and(row >= first, row < last)
        err = jnp.where(real, y_ref[...] - t_ref[...], 0.0)
        dy_ref[...] = err * (1.0 / d)
        part = jnp.broadcast_to(jnp.sum(err * err, keepdims=True).reshape(1, 1), (1, LANES))

        @pl.when(i == 0)
        def _():
            loss_ref[...] = part

        @pl.when(i > 0)
        def _():
            loss_ref[...] += part

    blk = pl.BlockSpec((tm, d), lambda i: (i, 0))
    return pl.pallas_call(
        body, name=name, grid=(T // tm,), in_specs=[blk, blk],
        out_specs=(pl.BlockSpec((1, LANES), lambda i: (0, 0)), blk),
        out_shape=(jax.ShapeDtypeStruct((1, LANES), F32), jax.ShapeDtypeStruct((T, d), F32)),
        compiler_params=_params(("arbitrary",)),
    )(y, target)


def _as3d(a):
    return a.reshape(a.shape[0], -1, a.shape[-1])


def _sum_stack(parts, *, name, out_dtype, rows=256):
    n, R, C = parts.shape
    tr = _tile(R, rows, 16)

    def body(p_ref, o_ref):
        acc = p_ref[0].astype(F32)
        for s in range(1, n):
            acc = acc + p_ref[s].astype(F32)
        o_ref[...] = acc.astype(out_dtype)

    return pl.pallas_call(
        body, name=name, grid=(R // tr,),
        in_specs=[pl.BlockSpec((n, tr, C), lambda i: (0, i, 0))],
        out_specs=pl.BlockSpec((tr, C), lambda i: (i, 0)),
        out_shape=jax.ShapeDtypeStruct((R, C), out_dtype),
        compiler_params=_params(("parallel",)),
    )(parts)


def _adamw(w, g, m, v, *, name, rows=128):
    R, C = w.shape
    tr = _tile(R, rows, 8)
    c1 = 1.0 - ADAM_B1 ** ADAM_STEP
    c2 = 1.0 - ADAM_B2 ** ADAM_STEP

    def body(w_ref, g_ref, m_ref, v_ref, d_ref, nm_ref, nv_ref):
        gv = g_ref[...]
        nm = ADAM_B1 * m_ref[...] + (1.0 - ADAM_B1) * gv
        nv = ADAM_B2 * v_ref[...] + (1.0 - ADAM_B2) * (gv * gv)
        nm_ref[...] = nm
        nv_ref[...] = nv
        d_ref[...] = -ADAM_LR * ((nm / c1) / (jnp.sqrt(nv / c2) + ADAM_EPS) + ADAM_WD * w_ref[...])

    blk = pl.BlockSpec((tr, C), lambda i: (i, 0))
    return pl.pallas_call(
        body, name=name, grid=(R // tr,), in_specs=[blk] * 4, out_specs=(blk,) * 3,
        out_shape=(jax.ShapeDtypeStruct((R, C), F32),) * 3,
        compiler_params=_params(("parallel",)),
    )(w, g, m, v)


def _one_hot(index, n):
    return jnp.broadcast_to((jnp.arange(n) == index).astype(F32)[:, None, None], (n, 8, LANES))


def _is_set(flags_ref, s):
    return flags_ref[s, 0:1, 0:1] > 0.5


def _rows_for(h, width, itemsize, n_stacked, budget, mult):
    return _tile(h, max(mult, budget // (n_stacked * width * itemsize)), mult)


def _pair_sum(pieces, recv, core, *, name):
    _, H, C = recv.shape
    tr = _rows_for(H, C, 2, 1, 2 << 20, 16)
    nh = H // tr

    def body(lo_ref, hi_ref, r_ref, core_ref, o_ref):
        mine = jnp.where(_is_set(core_ref, 0), lo_ref[0], hi_ref[0])
        o_ref[0] = (mine.astype(F32) + r_ref[0].astype(F32)).astype(BF16)

    blk = pl.BlockSpec((1, tr, C), lambda j, i: (j, i, 0))
    return pl.pallas_call(
        body, name=name, grid=(4, nh),
        in_specs=[blk, pl.BlockSpec((1, tr, C), lambda j, i: (j, nh + i, 0)), blk,
                  pl.BlockSpec((2, 8, LANES), lambda j, i: (0, 0, 0))],
        out_specs=blk, out_shape=jax.ShapeDtypeStruct((4, H, C), BF16),
        compiler_params=_params(("parallel", "parallel")),
    )(pieces, pieces, recv, core)


def _chip_sum(pair, landed, chip_flags, *, name):
    _, H, C = pair.shape
    tr = _rows_for(H, C, 2, 4, 8 << 20, 16)

    def body(p_ref, l_ref, chip_ref, o_ref):
        acc = None
        for s in range(4):
            part = jnp.where(_is_set(chip_ref, s), p_ref[s], l_ref[s]).astype(F32)
            acc = part if acc is None else acc + part
        o_ref[...] = acc

    blk = pl.BlockSpec((4, tr, C), lambda i: (0, i, 0))
    return pl.pallas_call(
        body, name=name, grid=(H // tr,),
        in_specs=[blk, blk, pl.BlockSpec((4, 8, LANES), lambda i: (0, 0, 0))],
        out_specs=pl.BlockSpec((tr, C), lambda i: (i, 0)), out_shape=jax.ShapeDtypeStruct((H, C), F32),
        compiler_params=_params(("parallel",)),
    )(pair, landed, chip_flags)


def _adamw_layer(w, m, v, total, recv, core, layer, prev, *, name, col_halves=False):
    _, R, C = w.shape
    H, wd = total.shape
    tr = _rows_for(H, wd, 4, 1, 2 << 20, 8)
    nh = H // tr
    c1 = 1.0 - ADAM_B1 ** ADAM_STEP
    c2 = 1.0 - ADAM_B2 ** ADAM_STEP
    n_prev = 0 if prev is None else 4

    def body(*refs):
        w_ref, m_ref, v_ref, t_ref, r_ref, core_ref = refs[:6]
        g_ref, d_ref, nm_ref, nv_ref = refs[6 + n_prev:]
        half_is_mine = jnp.where(pl.program_id(0) == 0, core_ref[0, 0:1, 0:1], core_ref[1, 0:1, 0:1]) > 0.5
        gv = jnp.where(half_is_mine, t_ref[...], r_ref[...])
        nm = ADAM_B1 * m_ref[0] + (1.0 - ADAM_B1) * gv
        nv = ADAM_B2 * v_ref[0] + (1.0 - ADAM_B2) * (gv * gv)
        g_ref[0] = gv
        nm_ref[0] = nm
        nv_ref[0] = nv
        d_ref[0] = -ADAM_LR * ((nm / c1) / (jnp.sqrt(nv / c2) + ADAM_EPS) + ADAM_WD * w_ref[0])

    if col_halves:
        lay = pl.BlockSpec((1, tr, wd), lambda hf, i: (layer, i, hf))
    else:
        lay = pl.BlockSpec((1, tr, wd), lambda hf, i: (layer, hf * nh + i, 0))
    one = pl.BlockSpec((tr, wd), lambda hf, i: (i, 0))
    operands = [w, m, v, total, recv, core] + ([] if prev is None else list(prev))
    return pl.pallas_call(
        body, name=name, grid=(2, nh),
        in_specs=[lay, lay, lay, one, one, pl.BlockSpec((2, 8, LANES), lambda hf, i: (0, 0, 0))] + [ANY] * n_prev,
        out_specs=(lay,) * 4, out_shape=(jax.ShapeDtypeStruct((2, R, C), F32),) * 4,
        input_output_aliases={6 + i: i for i in range(n_prev)},
        compiler_params=_params(("parallel", "parallel")),
    )(*operands)


ANY = pl.BlockSpec(memory_space=pl.ANY)


def _coords():
    return lax.axis_index("x"), lax.axis_index("y"), lax.axis_index("c")


HBM = pl.BlockSpec(memory_space=pltpu.HBM)
SEM = pl.BlockSpec(memory_space=pltpu.SEMAPHORE)
EFFECT = pltpu.SideEffectType.DATAFLOW_SIDE_EFFECTING


def _copies(plan, bufs, send_sems, recv_sems):
    return [pltpu.make_async_remote_copy(src_ref=s, dst_ref=d, send_sem=send_sems.at[i], recv_sem=recv_sems.at[i],
                                         device_id=to, device_id_type=MESH)
            for i, (s, d, to) in enumerate(plan(bufs))]


def _start_copies(bufs, groups, *, name):
    nb, ng = len(bufs), len(groups)

    def body(*refs):
        buf_refs = refs[:nb]
        sems = refs[nb:nb + 2 * ng]
        token = refs[-1]
        for g, (plan, _) in enumerate(groups):
            for cp in _copies(plan, buf_refs, sems[2 * g], sems[2 * g + 1]):
                cp.start()
        token[...] = jnp.zeros_like(token)

    sem_shapes = []
    for _, n in groups:
        sem_shapes += [pltpu.SemaphoreType.DMA((n,)), pltpu.SemaphoreType.DMA((n,))]
    out = pl.pallas_call(
        body, name=name, in_specs=[HBM] * nb,
        out_specs=tuple([SEM] * (2 * ng) + [HBM] * nb + [pl.BlockSpec(memory_space=pltpu.VMEM)]),
        out_shape=tuple(sem_shapes + [pltpu.HBM(b.shape, b.dtype) for b in bufs] + [jax.ShapeDtypeStruct((8, LANES), F32)]),
        input_output_aliases={i: 2 * ng + i for i in range(nb)},
        compiler_params=pltpu.CompilerParams(has_side_effects=EFFECT),
    )(*[pltpu.with_memory_space_constraint(b, pltpu.HBM) for b in bufs])
    sems = [(out[2 * g], out[2 * g + 1]) for g in range(ng)]
    return sems, list(out[2 * ng:2 * ng + nb]), out[-1]


def _wait_copies(bufs, sems, plan, after, *, name):
    nb = len(bufs)

    def body(*refs):
        buf_refs = refs[:nb]
        for cp in _copies(plan, buf_refs, refs[nb], refs[nb + 1]):
            cp.wait_send()
            cp.wait_recv()

    out = pl.pallas_call(
        body, name=name, in_specs=[HBM] * nb + [SEM, SEM, ANY], out_specs=tuple([HBM] * nb),
        out_shape=tuple(pltpu.HBM(b.shape, b.dtype) for b in bufs),
        input_output_aliases={i: i for i in range(nb)},
        compiler_params=pltpu.CompilerParams(has_side_effects=EFFECT),
    )(*bufs, sems[0], sems[1], after)
    return list(out)


def _half(ref, c):
    h = ref.shape[0] // 2
    return ref.at[pl.ds(c * h, h)]


def _ici_gather_plan(pairs):
    def plan(refs):
        x, y, c = _coords()
        me = 2 * x + y
        out = []
        for s, d in pairs:
            for cx, cy in [(1 - x, y), (x, 1 - y), (1 - x, 1 - y)]:
                out.append((_half(refs[s], c), _half(refs[d].at[me], c), (cx, cy, c)))
            out.append((refs[s], refs[d].at[me], (x, y, 1 - c)))
        return out
    return plan, 4 * len(pairs)


def _d2d_forward_plan(lands):
    def plan(refs):
        x, y, c = _coords()
        out = []
        for d in lands:
            for cx, cy in [(1 - x, y), (x, 1 - y), (1 - x, 1 - y)]:
                got = _half(refs[d].at[2 * cx + cy], c)
                out.append((got, got, (x, y, 1 - c)))
        return out
    return plan, 3 * len(lands)


def _swap_half_plan(pairs):
    def plan(refs):
        x, y, c = _coords()
        out = []
        for s, d in pairs:
            h = refs[d].shape[1]
            out.append((refs[s].at[:, pl.ds((1 - c) * h, h)], refs[d], (x, y, 1 - c)))
        return out
    return plan, len(pairs)


def _scatter_plan(pairs):
    def plan(refs):
        x, y, c = _coords()
        me = 2 * x + y
        out = []
        for s, d in pairs:
            for cx, cy in [(1 - x, y), (x, 1 - y), (1 - x, 1 - y)]:
                out.append((refs[s].at[2 * cx + cy], refs[d].at[me], (cx, cy, c)))
        return out
    return plan, 3 * len(pairs)


def _swap_total_plan(pairs):
    def plan(refs):
        x, y, c = _coords()
        return [(refs[s], refs[d], (x, y, 1 - c)) for s, d in pairs]
    return plan, len(pairs)


def _gather_all(block, *, name):
    def body(src, out, send_sems, recv_sems, local_sem):
        x, y, c = _coords()
        me = 4 * x + 2 * y + c
        flips = [(fx, fy, fc) for fx in (0, 1) for fy in (0, 1) for fc in (0, 1)][1:]
        mine = pltpu.make_async_copy(src, out.at[me], local_sem)
        mine.start()
        peers = [(x ^ fx, y ^ fy, c ^ fc) for fx, fy, fc in flips]
        cps = [pltpu.make_async_remote_copy(src_ref=src, dst_ref=out.at[me], send_sem=send_sems.at[k],
                                            recv_sem=recv_sems.at[k], device_id=peer, device_id_type=MESH)
               for k, peer in enumerate(peers)]
        for cp in cps:
            cp.start()
        for k, (px, py, pc) in enumerate(peers):
            slot = out.at[4 * px + 2 * py + pc]
            pltpu.make_async_remote_copy(src_ref=slot, dst_ref=slot, send_sem=send_sems.at[k], recv_sem=recv_sems.at[k],
                                         device_id=(px, py, pc), device_id_type=MESH).wait_recv()
        for cp in cps:
            cp.wait_send()
        mine.wait()

    return pl.pallas_call(
        body, name=name, in_specs=[ANY], out_specs=ANY,
        out_shape=jax.ShapeDtypeStruct((8,) + block.shape, block.dtype),
        scratch_shapes=[pltpu.SemaphoreType.DMA((7,)), pltpu.SemaphoreType.DMA((7,)), pltpu.SemaphoreType.DMA],
    )(block)


def _cols(o):
    return jnp.transpose(o, (1, 0, 2)).reshape(o.shape[1], -1)


def _uncols(full):
    return jnp.transpose(full.reshape(full.shape[0], 4, -1), (1, 0, 2))


def _rope_pad(x1, x2):
    z = jnp.zeros_like(x1)
    return jnp.concatenate([x1, z, x2, z], axis=-1)


def _head_pad(w, heads):
    r = w.reshape(w.shape[0], heads, QK_HEAD)
    half = QK_ROPE // 2
    out = jnp.concatenate([r[..., :QK_NOPE], _rope_pad(r[..., QK_NOPE:QK_NOPE + half], r[..., QK_NOPE + half:])], axis=-1)
    return out.reshape(w.shape[0], heads * HEAD_PAD)


def _head_unpad(w, heads):
    r = w.reshape(w.shape[0], heads, HEAD_PAD)
    half = QK_ROPE // 2
    out = jnp.concatenate([r[..., :QK_NOPE], r[..., QK_NOPE:QK_NOPE + half],
                           r[..., QK_NOPE + 2 * half:QK_NOPE + 3 * half]], axis=-1)
    return out.reshape(w.shape[0], heads * QK_HEAD)


class _Dims:
    def __init__(self, d, seq):
        self.d = d
        self.seq = seq
        self.t_real = N_META + seq
        self.t = -(-self.t_real // LANES) * LANES
        self.dc = d // 2
        self.dp = d // 2
        self.pg = self.dp // len(POOL_WINDOWS)
        self.heads = d // 128
        self.dff = 4 * d
        self.a_end = 3 * self.dc
        self.q_end = self.a_end + Q_LORA
        self.kv_end = self.q_end + KV_LORA
        self.kr_end = self.kv_end + QK_ROPE
        self.pool_end = self.kr_end + self.dp
        self.d_in = self.pool_end + 3 * d
        self.r_pool = 3 * self.dc
        self.r_q = self.r_pool + self.dp
        self.r_kv = self.r_q + Q_LORA
        self.r_kr = self.r_kv + KV_LORA
        self.r_width = self.r_kr + HEAD_PAD


def _split_cols(a):
    return jnp.moveaxis(a.reshape(a.shape[:-1] + (2, a.shape[-1] // 2)), -2, -3)


def _join_cols(a):
    a = jnp.moveaxis(a, -3, -2)
    return a.reshape(a.shape[:-2] + (a.shape[-2] * a.shape[-1],))


def _in_weights(dm, pieces):
    w_t = _join_cols(pieces).reshape(dm.d_in, dm.d)
    half = QK_ROPE // 2
    kr = w_t[dm.kv_end:dm.kr_end]
    zeros = jnp.zeros((half, dm.d), BF16)
    kr_p = jnp.concatenate([kr[:half], zeros, kr[half:], zeros, jnp.zeros((HEAD_PAD - LANES, dm.d), BF16)], axis=0)
    return dict(
        wg_t=w_t[dm.pool_end:],
        wr_t=jnp.concatenate([w_t[:dm.a_end], w_t[dm.kr_end:dm.pool_end], w_t[dm.a_end:dm.kv_end], kr_p], axis=0))


def _other_weights(dm, g):
    out = {}
    if "w_ukv" in g:
        w_ukv = _cols(g["w_ukv"]).reshape(KV_LORA, dm.heads, QK_NOPE + V_HEAD)
        out["wkn"] = w_ukv[:, :, :QK_NOPE].reshape(KV_LORA, dm.heads * QK_NOPE)
        out["wv"] = w_ukv[:, :, QK_NOPE:].reshape(KV_LORA, dm.heads * V_HEAD)
    if "w_uq" in g:
        out["wuq"] = _head_pad(_cols(g["w_uq"]), dm.heads)
    if "pool_w" in g:
        out["wp"] = jnp.transpose(g["pool_w"], (1, 0, 2, 3)).reshape(len(POOL_WINDOWS), dm.pg, dm.pg)
    for name, key in (("w_branch_a", "wba"), ("w_branch_c", "wbc"), ("w_up", "wup")):
        if name in g:
            out[key] = _cols(g[name])
    for name, key in (("w_branch_b", "wbb"), ("w_o", "wo"), ("w_down", "wdn")):
        if name in g:
            out[key] = g[name].reshape(-1, dm.d)
    return out


def _small_weights(small):
    return dict(
        conv_w=small["conv_w"],
        attn_norm=small["attn_norm"][None], mlp_norm=small["mlp_norm"][None],
        q_lat_norm=small["q_lat_norm"][None], kv_lat_norm=small["kv_lat_norm"][None],
        q_norm=_head_pad(small["q_norm"][None], 1), k_norm=_head_pad(small["k_norm"][None], 1),
        pool_scale=small["pool_scale"][None],
    )


def _grad_piece(dm, dw, name):
    half = QK_ROPE // 2
    rows = lambda a: a.reshape((4, a.shape[0] // 4) + a.shape[1:])
    if name == "w_in":
        dwr, dwg = dw["wr_t"], dw["wg_t"]
        d_t = jnp.concatenate([
            dwr[:dm.r_pool], dwr[dm.r_q:dm.r_kr], dwr[dm.r_kr:dm.r_kr + half],
            dwr[dm.r_kr + 2 * half:dm.r_kr + 3 * half], dwr[dm.r_pool:dm.r_q], dwg], axis=0)
        out = _split_cols(rows(d_t))
    elif name == "w_ukv":
        out = _uncols(jnp.concatenate([dw["wkn"].reshape(KV_LORA, dm.heads, QK_NOPE),
                                       dw["wv"].reshape(KV_LORA, dm.heads, V_HEAD)], axis=-1).reshape(KV_LORA, -1))
    elif name == "w_uq":
        out = _uncols(_head_unpad(dw["wuq"], dm.heads))
    elif name == "pool_w":
        out = jnp.transpose(dw["wp"].reshape(len(POOL_WINDOWS), 4, dm.pg // 4, dm.pg), (1, 0, 2, 3))
    elif name in ("w_branch_a", "w_branch_c", "w_up"):
        out = _uncols(dw[{"w_branch_a": "wba", "w_branch_c": "wbc", "w_up": "wup"}[name]])
    else:
        out = rows(dw[{"w_branch_b": "wbb", "w_o": "wo", "w_down": "wdn"}[name]])
    return out.astype(BF16)


def _layer_fwd(dm, W, x, cos_t, sin_t, tag, more=None, h=None):
    n = lambda s: f"{s}_{tag}"
    if h is None:
        h = _rms_fwd(x, W["attn_norm"], name=n("attn_norm"))
    gl = _mm(h, W["wg_t"], name=n("proj_gates"), tb=True)
    rest = _mm(h, W["wr_t"], name=n("proj_rest"), tb=True)
    if more is not None:
        W.update(more("after_proj", rest))
    y_a = _conv_fwd(rest, W["conv_w"], name=n("conv"), dc=dm.dc)
    y_c = _pool_fwd(rest, W["wp"], W["pool_scale"], name=n("pool"), seg0=dm.r_pool // dm.pg, pg=dm.pg)
    q_lat = _rms_fwd(rest, W["q_lat_norm"], name=n("q_lat_norm"), width=Q_LORA, seg=dm.r_q // Q_LORA)
    kv_lat = _rms_fwd(rest, W["kv_lat_norm"], name=n("kv_lat_norm"), width=KV_LORA, seg=dm.r_kv // KV_LORA)
    q_raw = _mm(q_lat, W["wuq"], name=n("up_q"))
    k_nope = _mm(kv_lat, W["wkn"], name=n("up_k"))
    v = _mm(kv_lat, W["wv"], name=n("up_v"), out_dtype=BF16)
    q, k = _qk_fwd(q_raw, k_nope, rest, cos_t, sin_t, W["q_norm"], W["k_norm"], name=n("qk_norm_rope"),
                   heads=dm.heads, kr_seg=dm.r_kr // HEAD_PAD)
    if more is not None:
        W.update(more("after_qk", q))
    y_b, lse = _flash_fwd(q, k, v, name=n("attention"), heads=dm.heads)
    pa = _mm(y_a, W["wba"], name=n("branch_a"))
    pb = _mm(y_b, W["wbb"], name=n("branch_b"))
    pc = _mm(y_c, W["wbc"], name=n("branch_c"))
    merged = _merge_fwd(gl, pa, pb, pc, name=n("merge"), d=dm.d)
    x1 = _mm(merged, W["wo"], name=n("out_proj"), add=x)
    h2 = _rms_fwd(x1, W["mlp_norm"], name=n("mlp_norm"))
    up, act = _mm(h2, W["wup"], name=n("mlp_up"), epi="relu2")
    x2 = _mm(act, W["wdn"], name=n("mlp_down"), add=x1, tk=2048)
    saved = dict(x=x, h=h, gl=gl, rest=rest, y_a=y_a, y_c=y_c, q_lat=q_lat, kv_lat=kv_lat, q_raw=q_raw, k_nope=k_nope,
                 v=v, q=q, k=k, y_b=y_b, lse=lse, pa=pa, pb=pb, pc=pc, merged=merged, x1=x1, h2=h2, up=up, act=act)
    return x2, saved


def _layer_bwd(dm, W, S, dx2, cos_t, sin_t, tag, hook=None):
    n = lambda s: f"{s}_{tag}"
    dw, ds = {}, {}
    if hook is None:
        hook = lambda point, t, dw_so_far: ()
    dup = _mm(dx2, W["wdn"], name=n("d_mlp_down"), tb=True, aux=S["up"], epi="drelu2", out_dtype=BF16,
              after=hook("start", dx2, dw))
    dw["wdn"] = _mm(S["act"], dx2, name=n("dw_mlp_down"), ta=True, tm=1024, tk=1408)
    dh2 = _mm(dup, W["wup"], name=n("d_mlp_up"), tb=True, tk=2048)
    dw["wup"] = _mm(S["h2"], dup, name=n("dw_mlp_up"), ta=True, tm=1024, tk=1408)
    dx1, ds["mlp_norm"] = _rms_bwd(dh2, S["x1"], W["mlp_norm"], name=n("d_mlp_norm"), res=dx2)
    dmerged = _mm(dx1, W["wo"], name=n("d_out_proj"), tb=True, after=hook("after_mlp", dx1, dw))
    dw["wo"] = _mm(S["merged"], dx1, name=n("dw_out_proj"), ta=True, tm=1024, tk=1408)
    dpa, dpb, dpc, dg0, dg1, dg2 = _merge_bwd(dmerged, S["gl"], S["pa"], S["pb"], S["pc"], name=n("d_merge"), d=dm.d)
    dgl = jnp.concatenate([dg0, dg1, dg2], axis=1)
    dy_a = _mm(dpa, W["wba"], name=n("d_branch_a"), tb=True)
    dw["wba"] = _mm(S["y_a"], dpa, name=n("dw_branch_a"), ta=True, tm=1024, tk=1408)
    dy_b = _mm(dpb, W["wbb"], name=n("d_branch_b"), tb=True, out_dtype=BF16)
    dw["wbb"] = _mm(S["y_b"], dpb, name=n("dw_branch_b"), ta=True, tm=1024, tk=1408)
    dy_c = _mm(dpc, W["wbc"], name=n("d_branch_c"), tb=True)
    dw["wbc"] = _mm(S["y_c"], dpc, name=n("dw_branch_c"), ta=True, tm=1024, tk=1408)
    dq, dk, dv = _flash_bwd(S["q"], S["k"], S["v"], S["y_b"], dy_b, S["lse"], name=n("d_attention"), heads=dm.heads)
    after_attention = hook("after_attention", dq, dw)
    dq_raw, dk_nope, dk_rope, dgq, dgk = _qk_bwd(
        dq, dk, S["q_raw"], S["k_nope"], S["rest"], cos_t, sin_t, W["q_norm"], W["k_norm"], name=n("d_qk_norm_rope"),
        heads=dm.heads, kr_seg=dm.r_kr // HEAD_PAD)
    ds["q_norm"] = _head_unpad(dgq, 1)
    ds["k_norm"] = _head_unpad(dgk, 1)
    dkv_v = _mm(dv, W["wv"], name=n("d_up_v"), tb=True, after=after_attention)
    dq_lat_n = _mm(dq_raw, W["wuq"], name=n("d_up_q"), tb=True, tk=2048, after=hook("after_qk", dq_raw, dw))
    dw["wuq"] = _mm(S["q_lat"], dq_raw, name=n("dw_up_q"), ta=True, tm=512, tk=1408)
    dkv_lat_n = _mm(dk_nope, W["wkn"], name=n("d_up_k"), tb=True, add=dkv_v)
    dw["wkn"] = _mm(S["kv_lat"], dk_nope, name=n("dw_up_k"), ta=True, tm=512, tk=1408)
    dw["wv"] = _mm(S["kv_lat"], dv, name=n("dw_up_v"), ta=True, tm=512, tk=1408)
    dq_lat, ds["q_lat_norm"] = _rms_bwd(dq_lat_n, S["rest"], W["q_lat_norm"], name=n("d_q_lat_norm"), width=Q_LORA,
                                        seg=dm.r_q // Q_LORA, out_dtype=BF16)
    dkv_lat, ds["kv_lat_norm"] = _rms_bwd(dkv_lat_n, S["rest"], W["kv_lat_norm"], name=n("d_kv_lat_norm"), width=KV_LORA,
                                          seg=dm.r_kv // KV_LORA, out_dtype=BF16)
    du, db, dc, ds["conv_w"] = _conv_bwd(S["rest"], W["conv_w"], dy_a, name=n("d_conv"), dc=dm.dc)
    dpool, dw["wp"], ds["pool_scale"] = _pool_bwd(S["rest"], W["wp"], W["pool_scale"], dy_c, name=n("d_pool"),
                                                  seg0=dm.r_pool // dm.pg, pg=dm.pg)
    drest = jnp.concatenate([du, db, dc, dpool, dq_lat, dkv_lat, dk_rope], axis=1)
    dw["wg_t"] = _mm(dgl, S["h"], name=n("dw_proj_gates"), ta=True, tm=1024, tk=1408)
    dw["wr_t"] = _mm(drest, S["h"], name=n("dw_proj_rest"), ta=True, tm=1024, tk=1408)
    dh_g = _mm(dgl, W["wg_t"], name=n("d_proj_gates"), tk=2048, after=hook("after_dw_in", dw["wr_t"], dw))
    dh = _mm(drest, W["wr_t"], name=n("d_proj_rest"), add=dh_g, tk=1792, after=hook("after_dh_gates", dh_g, dw))
    dx, ds["attn_norm"] = _rms_bwd(dh, S["x"], W["attn_norm"], name=n("d_attn_norm"), res=dx1)
    return dx, dw, ds


BIG = ("w_in", "w_uq", "w_ukv", "pool_w", "w_branch_a", "w_branch_b", "w_branch_c", "w_o", "w_up", "w_down")
REPLICATED = ("attn_norm", "q_lat_norm", "kv_lat_norm", "q_norm", "k_norm", "pool_scale", "mlp_norm")
WEIGHTS = ("meta_tokens", "attn_norm", "w_in", "conv_w", "q_lat_norm", "kv_lat_norm", "w_uq", "w_ukv", "q_norm",
           "k_norm", "pool_w", "pool_scale", "w_branch_a", "w_branch_b", "w_branch_c", "w_o", "mlp_norm", "w_up",
           "w_down")


def _pack(arrays):
    flat = jnp.concatenate([a.reshape(-1).astype(F32) for a in arrays])
    pad = (-flat.shape[0]) % (8 * LANES)
    return jnp.pad(flat, (0, pad)).reshape(-1, LANES)


def _unpack(flat, shapes):
    out, pos = [], 0
    flat = flat.reshape(-1)
    for shp in shapes:
        size = math.prod(shp)
        out.append(flat[pos:pos + size].reshape(shp))
        pos += size
    return out


def _update(w, g, m, v, name):
    shp = w.shape
    to2 = lambda a: a.reshape(-1, shp[-1])
    delta, nm, nv = _adamw(to2(w), to2(g), to2(m), to2(v), name=name)
    return delta.reshape(shp), nm.reshape(shp), nv.reshape(shp)


def _step(args):
    x = args["x"][0]
    seq, d = x.shape
    dm = _Dims(d, seq)
    xi, yi, ci = _coords()
    chip = 2 * xi + yi

    small_w = _gather_all(_pack([args["conv_w"], args["meta_tokens"]]), name="gather_small_weights")
    args = dict(args)
    for p in ("", "m_", "v_"):
        args[p + "w_in"] = jnp.swapaxes(args[p + "w_in"], 1, 2)
    order = [(k, l) for l in range(2) for k in BIG]
    shards = {n: args[n[0]][n[1]].astype(BF16) for n in order}
    for l in range(2):
        shards[("w_in", l)] = _split_cols(shards[("w_in", l)])
    small_w, shards[order[0]] = lax.optimization_barrier((small_w, shards[order[0]]))
    lands = {n: lax.empty((4,) + shards[n].shape, BF16) for n in order}
    last = ("w_up", "w_down")
    group_names = [[("w_in", 0)], [(k, 0) for k in BIG[1:] if k not in last], [(k, 0) for k in last],
                   [(k, 1) for k in BIG]]
    first, others = order[0], order[1:]
    sems, thru, token = _start_copies([shards[first], lands[first]], [_ici_gather_plan([(0, 1)])],
                                      name="start_gather_ici_first")
    shards[first], lands[first] = thru
    at = {n: i for i, n in enumerate(others)}
    sems_b, thru, token_b = _start_copies(
        [shards[n] for n in others] + [lands[n] for n in others] + [token],
        [_ici_gather_plan([(at[n], len(others) + at[n]) for n in g]) for g in group_names[1:]], name="start_gather_ici")
    sems = sems + sems_b
    for i, n in enumerate(others):
        shards[n], lands[n] = thru[i], thru[len(others) + i]

    def finish_gather(g, after, tag):
        names = group_names[g]
        k = len(names)
        plan, _ = _ici_gather_plan([(i, k + i) for i in range(k)])
        got = _wait_copies([shards[n] for n in names] + [lands[n] for n in names], sems[g], plan, after,
                           name=f"wait_gather_ici_{tag}")
        for i, n in enumerate(names):
            shards[n] = got[i]
        fwd = _d2d_forward_plan(list(range(k)))
        sems2, bufs2, tok2 = _start_copies(got[k:], [fwd], name=f"start_gather_d2d_{tag}")
        return names, bufs2, sems2[0], fwd[0], tok2

    def land_gather(pending, after, tag):
        names, bufs2, sems2, plan, tok2 = pending
        done = _wait_copies(bufs2, sems2, plan, tok2 if after is None else after, name=f"wait_gather_d2d_{tag}")
        return {n[0]: buf for n, buf in zip(names, done)}

    conv_shape, meta_shape = args["conv_w"].shape, args["meta_tokens"].shape
    per_chip = [_unpack(small_w[2 * j], [conv_shape, meta_shape]) for j in range(4)]
    conv_full = jnp.concatenate([p[0] for p in per_chip], axis=-1)
    meta_full = jnp.concatenate([p[1] for p in per_chip], axis=-1)

    layers = []
    for l in range(2):
        small = {k: args[k][l] for k in REPLICATED}
        small["conv_w"] = conv_full[l]
        layers.append(_small_weights(small))

    pos = jnp.arange(dm.t, dtype=F32)
    inv = ROPE_THETA ** (-jnp.arange(0, QK_ROPE, 2, dtype=F32) / QK_ROPE)
    ang = pos[:, None] * inv[None, :]
    cos_t = _rope_pad(jnp.cos(ang), jnp.cos(ang))
    sin_t = _rope_pad(-jnp.sin(ang), jnp.sin(ang))
    tail = jnp.zeros((dm.t - dm.t_real, d), F32)
    h0 = jnp.concatenate([meta_full, x, tail], axis=0)
    target = jnp.concatenate([jnp.zeros((N_META, d), F32), args["loss_target"][0], tail], axis=0)

    h_first = _rms_fwd(h0, layers[0]["attn_norm"], name="attn_norm_l0", after=(token, token_b))
    layers[0].update(_in_weights(dm, land_gather(finish_gather(0, h_first, "l0_in"), None, "l0_in")["w_in"]))
    def rest_of_layer0(point, after):
        g, tag = (1, "l0_mid") if point == "after_proj" else (2, "l0_mlp")
        return _other_weights(dm, land_gather(finish_gather(g, after, tag), None, tag))

    h1, saved0 = _layer_fwd(dm, layers[0], h0, cos_t, sin_t, "l0", more=rest_of_layer0, h=h_first)
    g1 = land_gather(finish_gather(3, saved0["y_b"], "l1"), h1, "l1")
    layers[1].update(_in_weights(dm, g1["w_in"]))
    layers[1].update(_other_weights(dm, g1))
    h2, saved1 = _layer_fwd(dm, layers[1], h1, cos_t, sin_t, "l1")
    sq, dy = _loss(h2, target, name="loss_head", first=N_META, last=dm.t_real)
    loss = lax.psum(0.5 / d * sq[0, 0], ("x", "y", "c"))
    core, chip_flags = _one_hot(ci, 2), _one_hot(chip, 4)

    class Reduce:
        def __init__(self, names, dw, tag):
            self.names, self.tag, self.nb = names, tag, len(names)
            self.idx = [(i, self.nb + i) for i in range(self.nb)]
            parts = [_as3d(_grad_piece(dm, dw, k)) for k in names]
            recv = [lax.empty((4, p.shape[1] // 2, p.shape[2]), BF16) for p in parts]
            self.plan = _swap_half_plan(self.idx)
            self.sems, self.bufs, self.token = _start_copies(parts + recv, [self.plan], name=f"start_swap_{tag}")

        def _land(self, after, what):
            return _wait_copies(self.bufs, self.sems[0], self.plan[0], self.token if after is None else after,
                                name=f"wait_{what}_{self.tag}")

        def scatter(self, after=None):
            got = self._land(after, "swap")
            pairs = [_pair_sum(got[i], got[j], core, name=f"pair_sum_{k}_{self.tag}")
                     for (i, j), k in zip(self.idx, self.names)]
            self.plan = _scatter_plan(self.idx)
            self.sems, self.bufs, self.token = _start_copies(pairs + [lax.empty(p.shape, BF16) for p in pairs],
                                                             [self.plan], name=f"start_scatter_{self.tag}")
            return self.token

        def totals(self, after=None):
            got = self._land(after, "scatter")
            sums = [_chip_sum(got[i], got[j], chip_flags, name=f"chip_sum_{k}_{self.tag}")
                    for (i, j), k in zip(self.idx, self.names)]
            self.plan = _swap_total_plan(self.idx)
            self.sems, self.bufs, self.token = _start_copies(sums + [lax.empty(t.shape, F32) for t in sums],
                                                             [self.plan], name=f"start_swap_total_{self.tag}")
            return self.token

        def finish(self, after=None):
            got = self._land(after, "swap_total")
            return {k: (got[i], got[j]) for (i, j), k in zip(self.idx, self.names)}

    dh1, dw1, ds1 = _layer_bwd(dm, layers[1], saved1, dy, cos_t, sin_t, "l1",
                               hook=lambda point, t, dw: (loss.reshape(1, 1),) if point == "start" else ())
    early = ("w_down", "w_up", "w_o", "w_branch_a", "w_branch_b", "w_branch_c")
    late = tuple(k for k in BIG if k not in early)
    stage = {}

    def during_layer0(point, t, dw):
        if point == "start":
            stage["l1"] = Reduce(BIG, dw1, "l1")
            return (stage["l1"].token,)
        if point == "after_mlp":
            return (stage["l1"].scatter(after=t),)
        if point == "after_attention":
            tok = stage["l1"].totals(after=t)
            stage["l0a"] = Reduce(early, dw, "l0a")
            return (tok, stage["l0a"].token)
        if point == "after_qk":
            stage["red1"] = stage["l1"].finish(after=t)
            return (stage["l0a"].scatter(after=t),)
        if point == "after_dw_in":
            tok = stage["l0a"].totals(after=t)
            stage["l0b"] = Reduce(late, dw, "l0b")
            return (tok, stage["l0b"].token)
        return (stage["l0b"].scatter(after=t),)

    dh0, dw0, ds0 = _layer_bwd(dm, layers[0], saved0, dh1, cos_t, sin_t, "l0", hook=during_layer0)
    grad_x = dh0[N_META:dm.t_real][None]
    stage["l0b"].totals(after=dh0)
    red1 = stage["red1"]
    red0 = {**stage["l0a"].finish(), **stage["l0b"].finish()}
    grads = {}

    small_names = REPLICATED + ("conv_w",)
    small_parts = [jnp.stack([ds0[k].reshape(ds0[k].shape[-2:] if k == "conv_w" else (-1,)),
                              ds1[k].reshape(ds1[k].shape[-2:] if k == "conv_w" else (-1,))]) for k in small_names]
    small_parts.append(dh0[:N_META])
    small_all = _gather_all(_pack(small_parts), name="gather_small_grads")
    small_sum = _sum_stack(small_all, name="sum_small_grads", out_dtype=F32)
    small_g = dict(zip(small_names + ("meta_tokens",), _unpack(small_sum, [p.shape for p in small_parts])))
    for k in REPLICATED:
        grads[k] = small_g[k]
    dcw = conv_shape[-1]
    grads["conv_w"] = lax.dynamic_slice_in_dim(small_g["conv_w"], chip * dcw, dcw, axis=2)
    dmeta = meta_shape[-1]
    grads["meta_tokens"] = lax.dynamic_slice_in_dim(small_g["meta_tokens"], chip * dmeta, dmeta, axis=1)

    delta, new_m, new_v = {}, {}, {}
    for k in WEIGHTS:
        shp = args[k].shape
        if k in BIG:
            wmv = [args[p + k].reshape(2, -1, shp[-1]) for p in ("", "m_", "v_")]
            by_cols = k == "w_in"
            out = _adamw_layer(*wmv, *red1[k], core, 1, None, name=f"adamw_{k}_l1", col_halves=by_cols)
            out = _adamw_layer(*wmv, *red0[k], core, 0, out, name=f"adamw_{k}_l0", col_halves=by_cols)
            out = [o.reshape(shp) for o in out]
            grads[k], delta[k], new_m[k], new_v[k] = [jnp.swapaxes(o, 1, 2) for o in out] if by_cols else out
        else:
            grads[k] = grads[k].reshape(shp)
            delta[k], new_m[k], new_v[k] = _update(args[k], grads[k], args["m_" + k], args["v_" + k], f"adamw_{k}")
    return (loss, grad_x, *[grads[k] for k in WEIGHTS], *[delta[k] for k in WEIGHTS],
            *[new_m[k] for k in WEIGHTS], *[new_v[k] for k in WEIGHTS])


def kernel(x, meta_tokens, attn_norm, w_in, conv_w, q_lat_norm, kv_lat_norm, w_uq, w_ukv, q_norm, k_norm, pool_w, pool_scale, w_branch_a, w_branch_b, w_branch_c, w_o, mlp_norm, w_up, w_down, loss_target, m_meta_tokens, m_attn_norm, m_w_in, m_conv_w, m_q_lat_norm, m_kv_lat_norm, m_w_uq, m_w_ukv, m_q_norm, m_k_norm, m_pool_w, m_pool_scale, m_w_branch_a, m_w_branch_b, m_w_branch_c, m_w_o, m_mlp_norm, m_w_up, m_w_down, v_meta_tokens, v_attn_norm, v_w_in, v_conv_w, v_q_lat_norm, v_kv_lat_norm, v_w_uq, v_w_ukv, v_q_norm, v_k_norm, v_pool_w, v_pool_scale, v_w_branch_a, v_w_branch_b, v_w_branch_c, v_w_o, v_mlp_norm, v_w_up, v_w_down):
    return _step(dict(locals()))
```

```python
import functools
import math

import jax
import jax.numpy as jnp
from jax import lax
from jax.experimental import pallas as pl
from jax.experimental.pallas import tpu as pltpu

F32 = jnp.float32
BF16 = jnp.bfloat16
MESH = pl.DeviceIdType.MESH

EPS = 1e-6
N_META = 16
QK_NOPE = 128
QK_ROPE = 64
QK_HEAD = QK_NOPE + QK_ROPE
V_HEAD = 128
HEAD_PAD = 256
Q_LORA = 512
KV_LORA = 512
ROPE_THETA = 10000.0
POOL_WINDOWS = (2, 4, 8, 16)
HALO = 16
LANES = 128
ADAM_LR = 0.001
ADAM_B1 = 0.9
ADAM_B2 = 0.999
ADAM_EPS = 1e-08
ADAM_WD = 0.01
ADAM_STEP = 10
VMEM_LIMIT = 52 * 1024 * 1024
NEG = -1e30
ATTN_SCALE = QK_HEAD ** -0.5
LOG2_E = 1.4426950408889634
Q_FOLD = ATTN_SCALE * LOG2_E


def _tile(n, target, mult=LANES):
    best = None
    for t in range(mult, min(n, target) + 1, mult):
        if n % t == 0:
            best = t
    return n if best is None else best


def _params(sem=None):
    return pltpu.CompilerParams(dimension_semantics=sem, vmem_limit_bytes=VMEM_LIMIT)


def _mm(a, b, *, name, ta=False, tb=False, add=None, aux=None, epi=None, out_dtype=F32,
        tm=704, tn=1024, tk=None, after=()):
    if ta:
        K, M = a.shape
    else:
        M, K = a.shape
    if tb:
        N, kb = b.shape
    else:
        kb, N = b.shape
    assert K == kb, (a.shape, b.shape, ta, tb)
    tm = _tile(M, tm, LANES if ta else 16)
    tn = _tile(N, tn, LANES)
    tk = K if tk is None else _tile(K, tk, LANES if (not ta or tb) else 16)
    nk = K // tk
    grid = (M // tm, N // tn, nk)

    a_spec = pl.BlockSpec((tk, tm), lambda i, j, k: (k, i)) if ta else pl.BlockSpec((tm, tk), lambda i, j, k: (i, k))
    b_spec = pl.BlockSpec((tn, tk), lambda i, j, k: (j, k)) if tb else pl.BlockSpec((tk, tn), lambda i, j, k: (k, j))
    o_spec = pl.BlockSpec((tm, tn), lambda i, j, k: (i, j))
    in_specs = [a_spec, b_spec]
    operands = [a, b]
    if add is not None:
        in_specs.append(o_spec)
        operands.append(add)
    if aux is not None:
        in_specs.append(o_spec)
        operands.append(aux)
    after = tuple(after)
    in_specs += [pl.BlockSpec(memory_space=pl.ANY)] * len(after)
    operands += list(after)
    if epi == "relu2":
        out_shape = (jax.ShapeDtypeStruct((M, N), BF16), jax.ShapeDtypeStruct((M, N), BF16))
        out_specs = (o_spec, o_spec)
    else:
        out_shape = jax.ShapeDtypeStruct((M, N), out_dtype)
        out_specs = o_spec
    dims = (((0 if ta else 1,), (1 if tb else 0,)), ((), ()))
    has_add, has_aux = add is not None, aux is not None

    def body(*refs):
        a_ref, b_ref = refs[0], refs[1]
        pos = 2
        add_ref = aux_ref = None
        if has_add:
            add_ref = refs[pos]
            pos += 1
        if has_aux:
            aux_ref = refs[pos]
            pos += 1
        pos += len(after)
        n_out = 2 if epi == "relu2" else 1
        out_refs = refs[pos:pos + n_out]
        acc_ref = refs[pos + n_out] if nk > 1 else None

        part = lax.dot_general(a_ref[...].astype(BF16), b_ref[...].astype(BF16), dims,
                               preferred_element_type=F32)

        def finish(acc):
            if has_add:
                acc = acc + add_ref[...].astype(F32)
            if epi == "relu2":
                r = jnp.maximum(acc, 0.0)
                out_refs[0][...] = acc.astype(BF16)
                out_refs[1][...] = (r * r).astype(BF16)
            elif epi == "drelu2":
                u = aux_ref[...].astype(F32)
                out_refs[0][...] = (acc * (2.0 * jnp.maximum(u, 0.0))).astype(out_dtype)
            else:
                out_refs[0][...] = acc.astype(out_dtype)

        if nk == 1:
            finish(part)
        else:
            k = pl.program_id(2)

            @pl.when(k == 0)
            def _():
                acc_ref[...] = part

            @pl.when(k > 0)
            def _():
                acc_ref[...] += part

            @pl.when(k == nk - 1)
            def _():
                finish(acc_ref[...])

    scratch = [pltpu.VMEM((tm, tn), F32)] if nk > 1 else []
    return pl.pallas_call(
        body, name=name, grid=grid, in_specs=in_specs, out_specs=out_specs, out_shape=out_shape,
        scratch_shapes=scratch, compiler_params=_params(("parallel", "parallel", "arbitrary")),
    )(*operands)


def _rms_fwd(x, g, *, name, width=None, seg=0, tm=384, after=()):
    T = x.shape[0]
    width = x.shape[1] if width is None else width
    tm = _tile(T, tm, 16)
    after = tuple(after)

    def body(x_ref, g_ref, *rest):
        xf = x_ref[...].astype(F32)
        r = lax.rsqrt(jnp.mean(xf * xf, axis=-1, keepdims=True) + EPS)
        rest[-1][...] = (xf * r * g_ref[...]).astype(BF16)

    return pl.pallas_call(
        body, name=name, grid=(T // tm,),
        in_specs=[pl.BlockSpec((tm, width), lambda i: (i, seg)), pl.BlockSpec((1, width), lambda i: (0, 0))]
        + [pl.BlockSpec(memory_space=pl.ANY)] * len(after),
        out_specs=pl.BlockSpec((tm, width), lambda i: (i, 0)),
        out_shape=jax.ShapeDtypeStruct((T, width), BF16),
        compiler_params=_params(("parallel",)),
    )(x, g, *after)


def _rms_bwd(dy, x, g, *, name, width=None, seg=0, res=None, out_dtype=F32, tm=384):
    T = x.shape[0]
    width = x.shape[1] if width is None else width
    tm = _tile(T, tm, 16)
    has_res = res is not None

    def body(*refs):
        dy_ref, x_ref, g_ref = refs[:3]
        res_ref = refs[3] if has_res else None
        dx_ref, dg_ref = refs[-2:]
        xf = x_ref[...].astype(F32)
        dyf = dy_ref[...].astype(F32)
        r = lax.rsqrt(jnp.mean(xf * xf, axis=-1, keepdims=True) + EPS)
        xhat = xf * r
        dyh = dyf * g_ref[...]
        dx = r * (dyh - xhat * jnp.mean(dyh * xhat, axis=-1, keepdims=True))
        if has_res:
            dx = dx + res_ref[...].astype(F32)
        dx_ref[...] = dx.astype(out_dtype)
        part = jnp.sum(dyf * xhat, axis=0, keepdims=True)

        @pl.when(pl.program_id(0) == 0)
        def _():
            dg_ref[...] = part

        @pl.when(pl.program_id(0) > 0)
        def _():
            dg_ref[...] += part

    row = pl.BlockSpec((tm, width), lambda i: (i, 0))
    in_specs = [row, pl.BlockSpec((tm, width), lambda i: (i, seg)), pl.BlockSpec((1, width), lambda i: (0, 0))]
    operands = [dy, x, g]
    if has_res:
        in_specs.append(row)
        operands.append(res)
    return pl.pallas_call(
        body, name=name, grid=(T // tm,), in_specs=in_specs,
        out_specs=(row, pl.BlockSpec((1, width), lambda i: (0, 0))),
        out_shape=(jax.ShapeDtypeStruct((T, width), out_dtype), jax.ShapeDtypeStruct((1, width), F32)),
        compiler_params=_params(("arbitrary",)),
    )(*operands)


def _down(ext, k):
    return pltpu.roll(ext, k, 0)


def _up(ext, k):
    return pltpu.roll(ext, ext.shape[0] - k, 0)


def _pre_halo(ref, r, R):
    start = pl.multiple_of(jnp.maximum(r * R - HALO, 0), 8)
    keep = (r > 0).astype(F32)
    return ref[pl.ds(start, HALO), :].astype(F32) * keep


def _post_halo(ref, r, R, n_chunks):
    start = pl.multiple_of(jnp.minimum(r * R + R, (n_chunks - 1) * R + R - HALO), 8)
    keep = (r < n_chunks - 1).astype(F32)
    return ref[pl.ds(start, HALO), :].astype(F32) * keep


def _chunk(ref, r, R):
    return ref[pl.ds(pl.multiple_of(r * R, 8), R), :].astype(F32)


def _conv_fwd(rest, conv_w, *, name, dc, tc=128, rows=1056):
    T = rest.shape[0]
    tc = _tile(dc, tc)
    nb = dc // tc
    R = _tile(T, rows, 16)
    n_chunks = T // R

    def body(u_ref, b_ref, c_ref, w_ref, y_ref):
        w0, w1, w2 = w_ref[0:1, :], w_ref[1:2, :], w_ref[2:3, :]

        def chunk(r, carry):
            cu = _chunk(c_ref, r, R) * _chunk(u_ref, r, R)
            ext = jnp.concatenate([_pre_halo(c_ref, r, R) * _pre_halo(u_ref, r, R), cu], axis=0)
            conv = w0 * _down(ext, 2)[HALO:] + w1 * _down(ext, 1)[HALO:] + w2 * cu
            y_ref[pl.ds(pl.multiple_of(r * R, 8), R), :] = (_chunk(b_ref, r, R) * conv).astype(BF16)
            return carry

        lax.fori_loop(0, n_chunks, chunk, 0)

    col = lambda off: pl.BlockSpec((T, tc), lambda j: (0, off * nb + j))
    return pl.pallas_call(
        body, name=name, grid=(nb,),
        in_specs=[col(0), col(1), col(2), pl.BlockSpec((3, tc), lambda j: (0, j))],
        out_specs=pl.BlockSpec((T, tc), lambda j: (0, j)),
        out_shape=jax.ShapeDtypeStruct((T, dc), BF16),
        compiler_params=_params(("parallel",)),
    )(rest, rest, rest, conv_w)


def _conv_bwd(rest, conv_w, dy, *, name, dc, tc=128, rows=1056):
    T = rest.shape[0]
    tc = _tile(dc, tc)
    nb = dc // tc
    R = _tile(T, rows, 16)
    n_chunks = T // R

    def body(u_ref, b_ref, c_ref, w_ref, dy_ref, du_ref, db_ref, dc_ref, dw_ref):
        w0, w1, w2 = w_ref[0:1, :], w_ref[1:2, :], w_ref[2:3, :]

        def chunk(r, carry):
            a0, a1, a2 = carry
            u, b, c = _chunk(u_ref, r, R), _chunk(b_ref, r, R), _chunk(c_ref, r, R)
            dy_c = _chunk(dy_ref, r, R)
            cu = c * u
            ext = jnp.concatenate([_pre_halo(c_ref, r, R) * _pre_halo(u_ref, r, R), cu], axis=0)
            cu1, cu2 = _down(ext, 1)[HALO:], _down(ext, 2)[HALO:]
            conv = w0 * cu2 + w1 * cu1 + w2 * cu
            dconv = dy_c * b
            dext = jnp.concatenate(
                [dconv, _post_halo(dy_ref, r, R, n_chunks) * _post_halo(b_ref, r, R, n_chunks)], axis=0)
            dcu = w2 * dconv + w1 * _up(dext, 1)[:R] + w0 * _up(dext, 2)[:R]
            rows_at = pl.ds(pl.multiple_of(r * R, 8), R)
            db_ref[rows_at, :] = (dy_c * conv).astype(BF16)
            du_ref[rows_at, :] = (dcu * c).astype(BF16)
            dc_ref[rows_at, :] = (dcu * u).astype(BF16)
            return (a0 + jnp.sum(dconv * cu2, axis=0, keepdims=True),
                    a1 + jnp.sum(dconv * cu1, axis=0, keepdims=True),
                    a2 + jnp.sum(dconv * cu, axis=0, keepdims=True))

        zero = jnp.zeros((1, tc), F32)
        a0, a1, a2 = lax.fori_loop(0, n_chunks, chunk, (zero, zero, zero))
        dw_ref[0:1, :] = a0
        dw_ref[1:2, :] = a1
        dw_ref[2:3, :] = a2

    col = lambda off: pl.BlockSpec((T, tc), lambda j: (0, off * nb + j))
    own = pl.BlockSpec((T, tc), lambda j: (0, j))
    return pl.pallas_call(
        body, name=name, grid=(nb,),
        in_specs=[col(0), col(1), col(2), pl.BlockSpec((3, tc), lambda j: (0, j)), own],
        out_specs=(own, own, own, pl.BlockSpec((3, tc), lambda j: (0, j))),
        out_shape=(jax.ShapeDtypeStruct((T, dc), BF16),) * 3 + (jax.ShapeDtypeStruct((3, dc), F32),),
        compiler_params=_params(("parallel",)),
    )(rest, rest, rest, conv_w, dy)


def _window_count(r, R, n_rows, w, first_row_offset):
    t = lax.broadcasted_iota(jnp.int32, (n_rows, 1), 0) + (r * R + first_row_offset)
    return jnp.minimum(t + 1, w).astype(F32)


def _pool_fwd(rest, pool_w, pool_scale, *, name, seg0, pg, rows=1056):
    T = rest.shape[0]
    R = _tile(T, rows, 16)
    n_chunks = T // R
    n_groups = len(POOL_WINDOWS)

    def body(x_ref, w_ref, s_ref, y_ref):
        def run(window):
            def chunk(r, carry):
                g = _chunk(x_ref, r, R)
                s = jnp.concatenate([_pre_halo(x_ref, r, R), g], axis=0)
                k = 1
                while k < window:
                    s = s + _down(s, k)
                    k *= 2
                pooled = s[HALO:] / _window_count(r, R, R, window, 0) - g
                mixed = jnp.dot(pooled.astype(BF16), w_ref[0], preferred_element_type=F32)
                y_ref[pl.ds(pl.multiple_of(r * R, 8), R), :] = (mixed * s_ref[...]).astype(BF16)
                return carry

            lax.fori_loop(0, n_chunks, chunk, 0)

        for gi, window in enumerate(POOL_WINDOWS):
            pl.when(pl.program_id(0) == gi)(functools.partial(run, window))

    return pl.pallas_call(
        body, name=name, grid=(n_groups,),
        in_specs=[pl.BlockSpec((T, pg), lambda g: (0, seg0 + g)),
                  pl.BlockSpec((1, pg, pg), lambda g: (g, 0, 0)),
                  pl.BlockSpec((1, pg), lambda g: (0, g))],
        out_specs=pl.BlockSpec((T, pg), lambda g: (0, g)),
        out_shape=jax.ShapeDtypeStruct((T, n_groups * pg), BF16),
        compiler_params=_params(("parallel",)),
    )(rest, pool_w, pool_scale)


def _pool_bwd(rest, pool_w, pool_scale, dy, *, name, seg0, pg, rows=1056):
    T = rest.shape[0]
    R = _tile(T, rows, 16)
    n_chunks = T // R
    n_groups = len(POOL_WINDOWS)

    def body(x_ref, w_ref, s_ref, dy_ref, dx_ref, dw_ref, ds_ref):
        def run(window):
            def chunk(r, carry):
                dw_acc, ds_acc = carry
                g = _chunk(x_ref, r, R)
                s = jnp.concatenate([_pre_halo(x_ref, r, R), g], axis=0)
                k = 1
                while k < window:
                    s = s + _down(s, k)
                    k *= 2
                pooled = (s[HALO:] / _window_count(r, R, R, window, 0) - g).astype(BF16)
                mixed = jnp.dot(pooled, w_ref[0], preferred_element_type=F32)
                dy_c = _chunk(dy_ref, r, R)
                dm_ext = (jnp.concatenate([dy_c, _post_halo(dy_ref, r, R, n_chunks)], axis=0)
                          * s_ref[...]).astype(BF16)
                dpool_ext = lax.dot_general(dm_ext, w_ref[0], (((1,), (1,)), ((), ())),
                                            preferred_element_type=F32)
                a = dpool_ext / _window_count(r, R, R + HALO, window, 0)
                k = 1
                while k < window:
                    a = a + _up(a, k)
                    k *= 2
                dx_ref[pl.ds(pl.multiple_of(r * R, 8), R), :] = (a[:R] - dpool_ext[:R]).astype(BF16)
                dw_acc = dw_acc + lax.dot_general(pooled, dm_ext[:R], (((0,), (0,)), ((), ())),
                                                  preferred_element_type=F32)
                ds_acc = ds_acc + jnp.sum(dy_c * mixed, axis=0, keepdims=True)
                return dw_acc, ds_acc

            dw_acc, ds_acc = lax.fori_loop(0, n_chunks, chunk,
                                           (jnp.zeros((pg, pg), F32), jnp.zeros((1, pg), F32)))
            dw_ref[0] = dw_acc
            ds_ref[...] = ds_acc

        for gi, window in enumerate(POOL_WINDOWS):
            pl.when(pl.program_id(0) == gi)(functools.partial(run, window))

    own = pl.BlockSpec((T, pg), lambda g: (0, g))
    return pl.pallas_call(
        body, name=name, grid=(n_groups,),
        in_specs=[pl.BlockSpec((T, pg), lambda g: (0, seg0 + g)),
                  pl.BlockSpec((1, pg, pg), lambda g: (g, 0, 0)),
                  pl.BlockSpec((1, pg), lambda g: (0, g)), own],
        out_specs=(own, pl.BlockSpec((1, pg, pg), lambda g: (g, 0, 0)), pl.BlockSpec((1, pg), lambda g: (0, g))),
        out_shape=(jax.ShapeDtypeStruct((T, n_groups * pg), BF16),
                   jax.ShapeDtypeStruct((n_groups, pg, pg), F32),
                   jax.ShapeDtypeStruct((1, n_groups * pg), F32)),
        compiler_params=_params(("parallel",)),
    )(rest, pool_w, pool_scale, dy)


def _rope(r, cos_t, sin_t):
    return r * cos_t + pltpu.roll(r, LANES // 2, 1) * sin_t


def _rope_t(d, cos_t, sin_t):
    return d * cos_t + pltpu.roll(d * sin_t, LANES // 2, 1)


def _qk_fwd(q_raw, k_nope, rest, cos_t, sin_t, q_norm, k_norm, *, name, heads, kr_seg, tm=192):
    T = q_raw.shape[0]
    tm = _tile(T, tm, 16)

    def body(q_ref, kn_ref, kr_ref, c_ref, s_ref, gq_ref, gk_ref, qo_ref, ko_ref):
        cos_b, sin_b = c_ref[...], s_ref[...]
        kr = kr_ref[:, 0:LANES]
        kr_ss = jnp.sum(kr * kr, axis=-1, keepdims=True)
        gq, gk = gq_ref[...], gk_ref[...]
        for h in range(heads):
            lo = h * HEAD_PAD
            q = q_ref[:, lo:lo + HEAD_PAD]
            rq = lax.rsqrt(jnp.sum(q * q, axis=-1, keepdims=True) / QK_HEAD + EPS)
            qn = q * (rq * Q_FOLD) * gq
            qo_ref[:, lo:lo + LANES] = qn[:, :LANES].astype(BF16)
            qo_ref[:, lo + LANES:lo + HEAD_PAD] = _rope(qn[:, LANES:], cos_b, sin_b).astype(BF16)
            kn = kn_ref[:, h * LANES:(h + 1) * LANES]
            rk = lax.rsqrt((jnp.sum(kn * kn, axis=-1, keepdims=True) + kr_ss) / QK_HEAD + EPS)
            ko_ref[:, lo:lo + LANES] = (kn * rk * gk[:, :LANES]).astype(BF16)
            ko_ref[:, lo + LANES:lo + HEAD_PAD] = _rope(kr * rk * gk[:, LANES:], cos_b, sin_b).astype(BF16)

    wq, wk = heads * HEAD_PAD, heads * LANES
    return pl.pallas_call(
        body, name=name, grid=(T // tm,),
        in_specs=[pl.BlockSpec((tm, wq), lambda i: (i, 0)), pl.BlockSpec((tm, wk), lambda i: (i, 0)),
                  pl.BlockSpec((tm, HEAD_PAD), lambda i: (i, kr_seg)),
                  pl.BlockSpec((tm, LANES), lambda i: (i, 0)), pl.BlockSpec((tm, LANES), lambda i: (i, 0)),
                  pl.BlockSpec((1, HEAD_PAD), lambda i: (0, 0)), pl.BlockSpec((1, HEAD_PAD), lambda i: (0, 0))],
        out_specs=(pl.BlockSpec((tm, wq), lambda i: (i, 0)), pl.BlockSpec((tm, wq), lambda i: (i, 0))),
        out_shape=(jax.ShapeDtypeStruct((T, wq), BF16), jax.ShapeDtypeStruct((T, wq), BF16)),
        compiler_params=_params(("parallel",)),
    )(q_raw, k_nope, rest, cos_t, sin_t, q_norm, k_norm)


def _qk_bwd(dq, dk, q_raw, k_nope, rest, cos_t, sin_t, q_norm, k_norm, *, name, heads, kr_seg, tm=128):
    T = q_raw.shape[0]
    tm = _tile(T, tm, 16)

    def body(dq_ref, dk_ref, q_ref, kn_ref, kr_ref, c_ref, s_ref, gq_ref, gk_ref,
             dqr_ref, dkn_ref, dkr_ref, dgq_ref, dgk_ref):
        cos_b, sin_b = c_ref[...], s_ref[...]
        kr = kr_ref[:, 0:LANES]
        kr_ss = jnp.sum(kr * kr, axis=-1, keepdims=True)
        gq, gk = gq_ref[...], gk_ref[...]
        dgq = jnp.zeros((1, HEAD_PAD), F32)
        dgk_n = jnp.zeros((1, LANES), F32)
        dgk_r = jnp.zeros((1, LANES), F32)
        dkr = jnp.zeros((tm, LANES), F32)
        for h in range(heads):
            lo = h * HEAD_PAD
            q = q_ref[:, lo:lo + HEAD_PAD]
            rq = lax.rsqrt(jnp.sum(q * q, axis=-1, keepdims=True) / QK_HEAD + EPS)
            qhat = q * rq
            dqn = jnp.concatenate([dq_ref[:, lo:lo + LANES],
                                   _rope_t(dq_ref[:, lo + LANES:lo + HEAD_PAD], cos_b, sin_b)], axis=1) * ATTN_SCALE
            dgq = dgq + jnp.sum(dqn * qhat, axis=0, keepdims=True)
            dqh = dqn * gq
            dqr_ref[:, lo:lo + HEAD_PAD] = (
                rq * (dqh - qhat * (jnp.sum(dqh * qhat, axis=-1, keepdims=True) / QK_HEAD))).astype(BF16)
            kn = kn_ref[:, h * LANES:(h + 1) * LANES]
            rk = lax.rsqrt((jnp.sum(kn * kn, axis=-1, keepdims=True) + kr_ss) / QK_HEAD + EPS)
            khat_n, khat_r = kn * rk, kr * rk
            dkn_n = dk_ref[:, lo:lo + LANES] * (1.0 / LOG2_E)
            dkn_r = _rope_t(dk_ref[:, lo + LANES:lo + HEAD_PAD], cos_b, sin_b) * (1.0 / LOG2_E)
            dgk_n = dgk_n + jnp.sum(dkn_n * khat_n, axis=0, keepdims=True)
            dgk_r = dgk_r + jnp.sum(dkn_r * khat_r, axis=0, keepdims=True)
            dkh_n, dkh_r = dkn_n * gk[:, :LANES], dkn_r * gk[:, LANES:]
            proj = (jnp.sum(dkh_n * khat_n, axis=-1, keepdims=True)
                    + jnp.sum(dkh_r * khat_r, axis=-1, keepdims=True)) / QK_HEAD
            dkn_ref[:, h * LANES:(h + 1) * LANES] = (rk * (dkh_n - khat_n * proj)).astype(BF16)
            dkr = dkr + rk * (dkh_r - khat_r * proj)
        dkr_ref[:, 0:LANES] = dkr.astype(BF16)
        dkr_ref[:, LANES:HEAD_PAD] = jnp.zeros((tm, HEAD_PAD - LANES), BF16)
        dgk = jnp.concatenate([dgk_n, dgk_r], axis=1)

        @pl.when(pl.program_id(0) == 0)
        def _():
            dgq_ref[...] = dgq
            dgk_ref[...] = dgk

        @pl.when(pl.program_id(0) > 0)
        def _():
            dgq_ref[...] += dgq
            dgk_ref[...] += dgk

    wq, wk = heads * HEAD_PAD, heads * LANES
    row = lambda w: pl.BlockSpec((tm, w), lambda i: (i, 0))
    vec = pl.BlockSpec((1, HEAD_PAD), lambda i: (0, 0))
    return pl.pallas_call(
        body, name=name, grid=(T // tm,),
        in_specs=[row(wq), row(wq), row(wq), row(wk), pl.BlockSpec((tm, HEAD_PAD), lambda i: (i, kr_seg)),
                  row(LANES), row(LANES), vec, vec],
        out_specs=(row(wq), row(wk), row(HEAD_PAD), vec, vec),
        out_shape=(jax.ShapeDtypeStruct((T, wq), BF16), jax.ShapeDtypeStruct((T, wk), BF16),
                   jax.ShapeDtypeStruct((T, HEAD_PAD), BF16),
                   jax.ShapeDtypeStruct((1, HEAD_PAD), F32), jax.ShapeDtypeStruct((1, HEAD_PAD), F32)),
        compiler_params=_params(("arbitrary",)),
    )(dq, dk, q_raw, k_nope, rest, cos_t, sin_t, q_norm, k_norm)


def _causal_mask(s):
    row = lax.broadcasted_iota(jnp.int32, s.shape, 0)
    col = lax.broadcasted_iota(jnp.int32, s.shape, 1)
    return jnp.where(row >= col, s, NEG)


def _flash_fwd(q, k, v, *, name, heads, tq=384, hp=2):
    T = q.shape[0]
    tq = _tile(T, tq, LANES)
    nq = T // tq
    nt = (((1,), (1,)), ((), ()))

    def body(q_ref, k_ref, v_ref, o_ref, lse_ref):
        def q_block(i, carry):
            q_at = pl.ds(pl.multiple_of(i * tq, tq), tq)
            qbs = [q_ref[q_at, h * HEAD_PAD:(h + 1) * HEAD_PAD] for h in range(hp)]

            def step(j, state, masked):
                k_at = pl.ds(pl.multiple_of(j * tq, tq), tq)
                new = []
                scores = [lax.dot_general(qbs[h], k_ref[k_at, h * HEAD_PAD:(h + 1) * HEAD_PAD], nt,
                                          preferred_element_type=F32) for h in range(hp)]
                for h in range(hp):
                    m, l, acc = state[h]
                    s = scores[h]
                    if masked:
                        s = _causal_mask(s)
                    m_new = jnp.maximum(m, jnp.max(s, axis=-1, keepdims=True))
                    p = jnp.exp2(s - m_new)
                    alpha = jnp.exp2(m - m_new)
                    l = alpha * l + jnp.sum(p, axis=-1, keepdims=True)
                    acc = alpha * acc + jnp.dot(p.astype(BF16), v_ref[k_at, h * V_HEAD:(h + 1) * V_HEAD],
                                                preferred_element_type=F32)
                    new.append((m_new, l, acc))
                return tuple(new)

            init = tuple((jnp.full((tq, 1), NEG, F32), jnp.zeros((tq, 1), F32), jnp.zeros((tq, V_HEAD), F32))
                         for _ in range(hp))
            state = lax.fori_loop(0, i, lambda j, st: step(j, st, False), init)
            state = step(i, state, True)
            for h in range(hp):
                m, l, acc = state[h]
                o_ref[q_at, h * V_HEAD:(h + 1) * V_HEAD] = (acc / l).astype(BF16)
                lse_ref[h, q_at, :] = jnp.broadcast_to(m + jnp.log2(l), (tq, LANES))
            return carry

        lax.fori_loop(0, nq, q_block, 0)

    qk_spec = pl.BlockSpec((T, hp * HEAD_PAD), lambda g: (0, g))
    v_spec = pl.BlockSpec((T, hp * V_HEAD), lambda g: (0, g))
    return pl.pallas_call(
        body, name=name, grid=(heads // hp,), in_specs=[qk_spec, qk_spec, v_spec],
        out_specs=(v_spec, pl.BlockSpec((hp, T, LANES), lambda g: (g, 0, 0))),
        out_shape=(jax.ShapeDtypeStruct((T, heads * V_HEAD), BF16), jax.ShapeDtypeStruct((heads, T, LANES), F32)),
        compiler_params=_params(("parallel",)),
    )(q, k, v)


def _flash_bwd(q, k, v, o, do, lse, *, name, heads, tq=384):
    T = q.shape[0]
    tq = _tile(T, tq, LANES)
    nq = T // tq
    nt = (((1,), (1,)), ((), ()))
    tn = (((0,), (0,)), ((), ()))

    def body(q_ref, k_ref, v_ref, o_ref, do_ref, lse_ref, dq_ref, dk_ref, dv_ref, delta_ref):
        def fill_delta(i, carry):
            at = pl.ds(pl.multiple_of(i * tq, tq), tq)
            d = jnp.sum(o_ref[at, :].astype(F32) * do_ref[at, :].astype(F32), axis=-1, keepdims=True)
            delta_ref[at, :] = jnp.broadcast_to(d, (tq, LANES))
            dq_ref[at, :] = jnp.zeros((tq, HEAD_PAD), F32)
            return carry

        lax.fori_loop(0, nq, fill_delta, 0)

        def kv_block(j, carry):
            k_at = pl.ds(pl.multiple_of(j * tq, tq), tq)
            kb, vb = k_ref[k_at, :], v_ref[k_at, :]

            def steps(blocks, state, masked):
                dk_acc, dv_acc = state
                at = [pl.ds(pl.multiple_of(i * tq, tq), tq) for i in blocks]
                qbs = [q_ref[a, :] for a in at]
                dobs = [do_ref[a, :] for a in at]
                scores = [lax.dot_general(qb, kb, nt, preferred_element_type=F32) for qb in qbs]
                dps = [lax.dot_general(dob, vb, nt, preferred_element_type=F32) for dob in dobs]
                for a, qb, dob, sc, dp in zip(at, qbs, dobs, scores, dps):
                    if masked:
                        sc = _causal_mask(sc)
                    p = jnp.exp2(sc - lse_ref[0, a, :][:, 0:1])
                    ds = (p * (dp - delta_ref[a, :][:, 0:1])).astype(BF16)
                    dv_acc = dv_acc + lax.dot_general(p.astype(BF16), dob, tn, preferred_element_type=F32)
                    dk_acc = dk_acc + lax.dot_general(ds, qb, tn, preferred_element_type=F32)
                    dq_ref[a, :] += jnp.dot(ds, kb, preferred_element_type=F32)
                return dk_acc, dv_acc

            state = steps([j], (jnp.zeros((tq, HEAD_PAD), F32), jnp.zeros((tq, V_HEAD), F32)), True)
            rest = nq - 1 - j
            state = lax.fori_loop(0, rest // 2, lambda t, st: steps([j + 1 + 2 * t, j + 2 + 2 * t], st, False), state)
            dk_acc, dv_acc = lax.cond(rest % 2 == 1, lambda st: steps([nq - 1], st, False), lambda st: st, state)
            dk_ref[k_at, :] = dk_acc
            dv_ref[k_at, :] = dv_acc.astype(BF16)
            return carry

        lax.fori_loop(0, nq, kv_block, 0)

    qk_spec = pl.BlockSpec((T, HEAD_PAD), lambda h: (0, h))
    v_spec = pl.BlockSpec((T, V_HEAD), lambda h: (0, h))
    return pl.pallas_call(
        body, name=name, grid=(heads,),
        in_specs=[qk_spec, qk_spec, v_spec, v_spec, v_spec, pl.BlockSpec((1, T, LANES), lambda h: (h, 0, 0))],
        out_specs=(qk_spec, qk_spec, v_spec),
        out_shape=(jax.ShapeDtypeStruct((T, heads * HEAD_PAD), F32), jax.ShapeDtypeStruct((T, heads * HEAD_PAD), F32),
                   jax.ShapeDtypeStruct((T, heads * V_HEAD), BF16)),
        scratch_shapes=[pltpu.VMEM((T, LANES), F32)],
        compiler_params=_params(("parallel",)),
    )(q, k, v, o, do, lse)


def _merge_fwd(gl, pa, pb, pc, *, name, d, tm=384, tn=1024):
    T = pa.shape[0]
    tm, tn = _tile(T, tm, 16), _tile(d, tn)
    nb = d // tn

    def body(g0, g1, g2, a, b, c, o_ref):
        o_ref[...] = (jax.nn.sigmoid(g0[...]) * a[...] + jax.nn.sigmoid(g1[...]) * b[...]
                      + jax.nn.sigmoid(g2[...]) * c[...]).astype(BF16)

    gate = lambda n: pl.BlockSpec((tm, tn), lambda i, j: (i, n * nb + j))
    blk = pl.BlockSpec((tm, tn), lambda i, j: (i, j))
    return pl.pallas_call(
        body, name=name, grid=(T // tm, nb), in_specs=[gate(0), gate(1), gate(2), blk, blk, blk],
        out_specs=blk, out_shape=jax.ShapeDtypeStruct((T, d), BF16),
        compiler_params=_params(("parallel", "parallel")),
    )(gl, gl, gl, pa, pb, pc)


def _merge_bwd(dm, gl, pa, pb, pc, *, name, d, tm=384, tn=1024):
    T = pa.shape[0]
    tm, tn = _tile(T, tm, 16), _tile(d, tn)
    nb = d // tn

    def body(dm_ref, g0, g1, g2, a, b, c, da, db, dc, dg0, dg1, dg2):
        dmv = dm_ref[...]
        for g_ref, p_ref, dp_ref, dg_ref in ((g0, a, da, dg0), (g1, b, db, dg1), (g2, c, dc, dg2)):
            sg = jax.nn.sigmoid(g_ref[...])
            dp_ref[...] = (dmv * sg).astype(BF16)
            dg_ref[...] = (dmv * p_ref[...] * sg * (1.0 - sg)).astype(BF16)

    gate = lambda n: pl.BlockSpec((tm, tn), lambda i, j: (i, n * nb + j))
    blk = pl.BlockSpec((tm, tn), lambda i, j: (i, j))
    return pl.pallas_call(
        body, name=name, grid=(T // tm, nb), in_specs=[blk, gate(0), gate(1), gate(2), blk, blk, blk],
        out_specs=(blk,) * 6, out_shape=(jax.ShapeDtypeStruct((T, d), BF16),) * 6,
        compiler_params=_params(("parallel", "parallel")),
    )(dm, gl, gl, gl, pa, pb, pc)


def _loss(y, target, *, name, first, last, tm=384):
    T, d = y.shape
    tm = _tile(T, tm, 16)

    def body(y_ref, t_ref, loss_ref, dy_ref):
        i = pl.program_id(0)
        row = lax.broadcasted_iota(jnp.int32, (tm, 1), 0) + i * tm
        real = jnp.logical_and(row >= first, row < last)
        err = jnp.where(real, y_ref[...] - t_ref[...], 0.0)
        dy_ref[...] = err * (1.0 / d)
        part = jnp.broadcast_to(jnp.sum(err * err, keepdims=True).reshape(1, 1), (1, LANES))

        @pl.when(i == 0)
        def _():
            loss_ref[...] = part

        @pl.when(i > 0)
        def _():
            loss_ref[...] += part

    blk = pl.BlockSpec((tm, d), lambda i: (i, 0))
    return pl.pallas_call(
        body, name=name, grid=(T // tm,), in_specs=[blk, blk],
        out_specs=(pl.BlockSpec((1, LANES), lambda i: (0, 0)), blk),
        out_shape=(jax.ShapeDtypeStruct((1, LANES), F32), jax.ShapeDtypeStruct((T, d), F32)),
        compiler_params=_params(("arbitrary",)),
    )(y, target)


def _as3d(a):
    return a.reshape(a.shape[0], -1, a.shape[-1])


def _sum_stack(parts, *, name, out_dtype, rows=256):
    n, R, C = parts.shape
    tr = _tile(R, rows, 16)

    def body(p_ref, o_ref):
        acc = p_ref[0].astype(F32)
        for s in range(1, n):
            acc = acc + p_ref[s].astype(F32)
        o_ref[...] = acc.astype(out_dtype)

    return pl.pallas_call(
        body, name=name, grid=(R // tr,),
        in_specs=[pl.BlockSpec((n, tr, C), lambda i: (0, i, 0))],
        out_specs=pl.BlockSpec((tr, C), lambda i: (i, 0)),
        out_shape=jax.ShapeDtypeStruct((R, C), out_dtype),
        compiler_params=_params(("parallel",)),
    )(parts)


def _adamw(w, g, m, v, *, name, rows=128):
    R, C = w.shape
    tr = _tile(R, rows, 8)
    c1 = 1.0 - ADAM_B1 ** ADAM_STEP
    c2 = 1.0 - ADAM_B2 ** ADAM_STEP

    def body(w_ref, g_ref, m_ref, v_ref, d_ref, nm_ref, nv_ref):
        gv = g_ref[...]
        nm = ADAM_B1 * m_ref[...] + (1.0 - ADAM_B1) * gv
        nv = ADAM_B2 * v_ref[...] + (1.0 - ADAM_B2) * (gv * gv)
        nm_ref[...] = nm
        nv_ref[...] = nv
        d_ref[...] = -ADAM_LR * ((nm / c1) / (jnp.sqrt(nv / c2) + ADAM_EPS) + ADAM_WD * w_ref[...])

    blk = pl.BlockSpec((tr, C), lambda i: (i, 0))
    return pl.pallas_call(
        body, name=name, grid=(R // tr,), in_specs=[blk] * 4, out_specs=(blk,) * 3,
        out_shape=(jax.ShapeDtypeStruct((R, C), F32),) * 3,
        compiler_params=_params(("parallel",)),
    )(w, g, m, v)


def _one_hot(index, n):
    return jnp.broadcast_to((jnp.arange(n) == index).astype(F32)[:, None, None], (n, 8, LANES))


def _is_set(flags_ref, s):
    return flags_ref[s, 0:1, 0:1] > 0.5


def _rows_for(h, width, itemsize, n_stacked, budget, mult):
    return _tile(h, max(mult, budget // (n_stacked * width * itemsize)), mult)


def _pair_sum(pieces, recv, core, *, name):
    _, H, C = recv.shape
    tr = _rows_for(H, C, 2, 1, 2 << 20, 16)
    nh = H // tr

    def body(lo_ref, hi_ref, r_ref, core_ref, o_ref):
        mine = jnp.where(_is_set(core_ref, 0), lo_ref[0], hi_ref[0])
        o_ref[0] = (mine.astype(F32) + r_ref[0].astype(F32)).astype(BF16)

    blk = pl.BlockSpec((1, tr, C), lambda j, i: (j, i, 0))
    return pl.pallas_call(
        body, name=name, grid=(4, nh),
        in_specs=[blk, pl.BlockSpec((1, tr, C), lambda j, i: (j, nh + i, 0)), blk,
                  pl.BlockSpec((2, 8, LANES), lambda j, i: (0, 0, 0))],
        out_specs=blk, out_shape=jax.ShapeDtypeStruct((4, H, C), BF16),
        compiler_params=_params(("parallel", "parallel")),
    )(pieces, pieces, recv, core)


def _chip_sum(pair, landed, chip_flags, *, name):
    _, H, C = pair.shape
    tr = _rows_for(H, C, 2, 4, 8 << 20, 16)

    def body(p_ref, l_ref, chip_ref, o_ref):
        acc = None
        for s in range(4):
            part = jnp.where(_is_set(chip_ref, s), p_ref[s], l_ref[s]).astype(F32)
            acc = part if acc is None else acc + part
        o_ref[...] = acc

    blk = pl.BlockSpec((4, tr, C), lambda i: (0, i, 0))
    return pl.pallas_call(
        body, name=name, grid=(H // tr,),
        in_specs=[blk, blk, pl.BlockSpec((4, 8, LANES), lambda i: (0, 0, 0))],
        out_specs=pl.BlockSpec((tr, C), lambda i: (i, 0)), out_shape=jax.ShapeDtypeStruct((H, C), F32),
        compiler_params=_params(("parallel",)),
    )(pair, landed, chip_flags)


def _adamw_layer(w, m, v, total, recv, core, layer, prev, *, name, col_halves=False):
    _, R, C = w.shape
    H, wd = total.shape
    tr = _rows_for(H, wd, 4, 1, 2 << 20, 8)
    nh = H // tr
    c1 = 1.0 - ADAM_B1 ** ADAM_STEP
    c2 = 1.0 - ADAM_B2 ** ADAM_STEP
    n_prev = 0 if prev is None else 4

    def body(*refs):
        w_ref, m_ref, v_ref, t_ref, r_ref, core_ref = refs[:6]
        g_ref, d_ref, nm_ref, nv_ref = refs[6 + n_prev:]
        half_is_mine = jnp.where(pl.program_id(0) == 0, core_ref[0, 0:1, 0:1], core_ref[1, 0:1, 0:1]) > 0.5
        gv = jnp.where(half_is_mine, t_ref[...], r_ref[...])
        nm = ADAM_B1 * m_ref[0] + (1.0 - ADAM_B1) * gv
        nv = ADAM_B2 * v_ref[0] + (1.0 - ADAM_B2) * (gv * gv)
        g_ref[0] = gv
        nm_ref[0] = nm
        nv_ref[0] = nv
        d_ref[0] = -ADAM_LR * ((nm / c1) / (jnp.sqrt(nv / c2) + ADAM_EPS) + ADAM_WD * w_ref[0])

    if col_halves:
        lay = pl.BlockSpec((1, tr, wd), lambda hf, i: (layer, i, hf))
    else:
        lay = pl.BlockSpec((1, tr, wd), lambda hf, i: (layer, hf * nh + i, 0))
    one = pl.BlockSpec((tr, wd), lambda hf, i: (i, 0))
    operands = [w, m, v, total, recv, core] + ([] if prev is None else list(prev))
    return pl.pallas_call(
        body, name=name, grid=(2, nh),
        in_specs=[lay, lay, lay, one, one, pl.BlockSpec((2, 8, LANES), lambda hf, i: (0, 0, 0))] + [ANY] * n_prev,
        out_specs=(lay,) * 4, out_shape=(jax.ShapeDtypeStruct((2, R, C), F32),) * 4,
        input_output_aliases={6 + i: i for i in range(n_prev)},
        compiler_params=_params(("parallel", "parallel")),
    )(*operands)


ANY = pl.BlockSpec(memory_space=pl.ANY)


def _coords():
    return lax.axis_index("x"), lax.axis_index("y"), lax.axis_index("c")


HBM = pl.BlockSpec(memory_space=pltpu.HBM)
SEM = pl.BlockSpec(memory_space=pltpu.SEMAPHORE)
EFFECT = pltpu.SideEffectType.DATAFLOW_SIDE_EFFECTING


def _copies(plan, bufs, send_sems, recv_sems):
    return [pltpu.make_async_remote_copy(src_ref=s, dst_ref=d, send_sem=send_sems.at[i], recv_sem=recv_sems.at[i],
                                         device_id=to, device_id_type=MESH)
            for i, (s, d, to) in enumerate(plan(bufs))]


def _start_copies(bufs, groups, *, name):
    nb, ng = len(bufs), len(groups)

    def body(*refs):
        buf_refs = refs[:nb]
        sems = refs[nb:nb + 2 * ng]
        token = refs[-1]
        for g, (plan, _) in enumerate(groups):
            for cp in _copies(plan, buf_refs, sems[2 * g], sems[2 * g + 1]):
                cp.start()
        token[...] = jnp.zeros_like(token)

    sem_shapes = []
    for _, n in groups:
        sem_shapes += [pltpu.SemaphoreType.DMA((n,)), pltpu.SemaphoreType.DMA((n,))]
    out = pl.pallas_call(
        body, name=name, in_specs=[HBM] * nb,
        out_specs=tuple([SEM] * (2 * ng) + [HBM] * nb + [pl.BlockSpec(memory_space=pltpu.VMEM)]),
        out_shape=tuple(sem_shapes + [pltpu.HBM(b.shape, b.dtype) for b in bufs] + [jax.ShapeDtypeStruct((8, LANES), F32)]),
        input_output_aliases={i: 2 * ng + i for i in range(nb)},
        compiler_params=pltpu.CompilerParams(has_side_effects=EFFECT),
    )(*[pltpu.with_memory_space_constraint(b, pltpu.HBM) for b in bufs])
    sems = [(out[2 * g], out[2 * g + 1]) for g in range(ng)]
    return sems, list(out[2 * ng:2 * ng + nb]), out[-1]


def _wait_copies(bufs, sems, plan, after, *, name):
    nb = len(bufs)

    def body(*refs):
        buf_refs = refs[:nb]
        for cp in _copies(plan, buf_refs, refs[nb], refs[nb + 1]):
            cp.wait_send()
            cp.wait_recv()

    out = pl.pallas_call(
        body, name=name, in_specs=[HBM] * nb + [SEM, SEM, ANY], out_specs=tuple([HBM] * nb),
        out_shape=tuple(pltpu.HBM(b.shape, b.dtype) for b in bufs),
        input_output_aliases={i: i for i in range(nb)},
        compiler_params=pltpu.CompilerParams(has_side_effects=EFFECT),
    )(*bufs, sems[0], sems[1], after)
    return list(out)


def _half(ref, c):
    h = ref.shape[0] // 2
    return ref.at[pl.ds(c * h, h)]


def _ici_gather_plan(pairs):
    def plan(refs):
        x, y, c = _coords()
        me = 2 * x + y
        out = []
        for s, d in pairs:
            for cx, cy in [(1 - x, y), (x, 1 - y), (1 - x, 1 - y)]:
                out.append((_half(refs[s], c), _half(refs[d].at[me], c), (cx, cy, c)))
            out.append((refs[s], refs[d].at[me], (x, y, 1 - c)))
        return out
    return plan, 4 * len(pairs)


def _d2d_forward_plan(lands):
    def plan(refs):
        x, y, c = _coords()
        out = []
        for d in lands:
            for cx, cy in [(1 - x, y), (x, 1 - y), (1 - x, 1 - y)]:
                got = _half(refs[d].at[2 * cx + cy], c)
                out.append((got, got, (x, y, 1 - c)))
        return out
    return plan, 3 * len(lands)


def _swap_half_plan(pairs):
    def plan(refs):
        x, y, c = _coords()
        out = []
        for s, d in pairs:
            h = refs[d].shape[1]
            out.append((refs[s].at[:, pl.ds((1 - c) * h, h)], refs[d], (x, y, 1 - c)))
        return out
    return plan, len(pairs)


def _scatter_plan(pairs):
    def plan(refs):
        x, y, c = _coords()
        me = 2 * x + y
        out = []
        for s, d in pairs:
            for cx, cy in [(1 - x, y), (x, 1 - y), (1 - x, 1 - y)]:
                out.append((refs[s].at[2 * cx + cy], refs[d].at[me], (cx, cy, c)))
        return out
    return plan, 3 * len(pairs)


def _swap_total_plan(pairs):
    def plan(refs):
        x, y, c = _coords()
        return [(refs[s], refs[d], (x, y, 1 - c)) for s, d in pairs]
    return plan, len(pairs)


def _gather_all(block, *, name):
    def body(src, out, send_sems, recv_sems, local_sem):
        x, y, c = _coords()
        me = 4 * x + 2 * y + c
        flips = [(fx, fy, fc) for fx in (0, 1) for fy in (0, 1) for fc in (0, 1)][1:]
        mine = pltpu.make_async_copy(src, out.at[me], local_sem)
        mine.start()
        peers = [(x ^ fx, y ^ fy, c ^ fc) for fx, fy, fc in flips]
        cps = [pltpu.make_async_remote_copy(src_ref=src, dst_ref=out.at[me], send_sem=send_sems.at[k],
                                            recv_sem=recv_sems.at[k], device_id=peer, device_id_type=MESH)
               for k, peer in enumerate(peers)]
        for cp in cps:
            cp.start()
        for k, (px, py, pc) in enumerate(peers):
            slot = out.at[4 * px + 2 * py + pc]
            pltpu.make_async_remote_copy(src_ref=slot, dst_ref=slot, send_sem=send_sems.at[k], recv_sem=recv_sems.at[k],
                                         device_id=(px, py, pc), device_id_type=MESH).wait_recv()
        for cp in cps:
            cp.wait_send()
        mine.wait()

    return pl.pallas_call(
        body, name=name, in_specs=[ANY], out_specs=ANY,
        out_shape=jax.ShapeDtypeStruct((8,) + block.shape, block.dtype),
        scratch_shapes=[pltpu.SemaphoreType.DMA((7,)), pltpu.SemaphoreType.DMA((7,)), pltpu.SemaphoreType.DMA],
    )(block)


def _cols(o):
    return jnp.transpose(o, (1, 0, 2)).reshape(o.shape[1], -1)


def _uncols(full):
    return jnp.transpose(full.reshape(full.shape[0], 4, -1), (1, 0, 2))


def _rope_pad(x1, x2):
    z = jnp.zeros_like(x1)
    return jnp.concatenate([x1, z, x2, z], axis=-1)


def _head_pad(w, heads):
    r = w.reshape(w.shape[0], heads, QK_HEAD)
    half = QK_ROPE // 2
    out = jnp.concatenate([r[..., :QK_NOPE], _rope_pad(r[..., QK_NOPE:QK_NOPE + half], r[..., QK_NOPE + half:])], axis=-1)
    return out.reshape(w.shape[0], heads * HEAD_PAD)


def _head_unpad(w, heads):
    r = w.reshape(w.shape[0], heads, HEAD_PAD)
    half = QK_ROPE // 2
    out = jnp.concatenate([r[..., :QK_NOPE], r[..., QK_NOPE:QK_NOPE + half],
                           r[..., QK_NOPE + 2 * half:QK_NOPE + 3 * half]], axis=-1)
    return out.reshape(w.shape[0], heads * QK_HEAD)


class _Dims:
    def __init__(self, d, seq):
        self.d = d
        self.seq = seq
        self.t_real = N_META + seq
        self.t = -(-self.t_real // LANES) * LANES
        self.dc = d // 2
        self.dp = d // 2
        self.pg = self.dp // len(POOL_WINDOWS)
        self.heads = d // 128
        self.dff = 4 * d
        self.a_end = 3 * self.dc
        self.q_end = self.a_end + Q_LORA
        self.kv_end = self.q_end + KV_LORA
        self.kr_end = self.kv_end + QK_ROPE
        self.pool_end = self.kr_end + self.dp
        self.d_in = self.pool_end + 3 * d
        self.r_pool = 3 * self.dc
        self.r_q = self.r_pool + self.dp
        self.r_kv = self.r_q + Q_LORA
        self.r_kr = self.r_kv + KV_LORA
        self.r_width = self.r_kr + HEAD_PAD


def _split_cols(a):
    return jnp.moveaxis(a.reshape(a.shape[:-1] + (2, a.shape[-1] // 2)), -2, -3)


def _join_cols(a):
    a = jnp.moveaxis(a, -3, -2)
    return a.reshape(a.shape[:-2] + (a.shape[-2] * a.shape[-1],))


def _in_weights(dm, pieces):
    w_t = _join_cols(pieces).reshape(dm.d_in, dm.d)
    half = QK_ROPE // 2
    kr = w_t[dm.kv_end:dm.kr_end]
    zeros = jnp.zeros((half, dm.d), BF16)
    kr_p = jnp.concatenate([kr[:half], zeros, kr[half:], zeros, jnp.zeros((HEAD_PAD - LANES, dm.d), BF16)], axis=0)
    return dict(
        wg_t=w_t[dm.pool_end:],
        wr_t=jnp.concatenate([w_t[:dm.a_end], w_t[dm.kr_end:dm.pool_end], w_t[dm.a_end:dm.kv_end], kr_p], axis=0))


def _other_weights(dm, g):
    out = {}
    if "w_ukv" in g:
        w_ukv = _cols(g["w_ukv"]).reshape(KV_LORA, dm.heads, QK_NOPE + V_HEAD)
        out["wkn"] = w_ukv[:, :, :QK_NOPE].reshape(KV_LORA, dm.heads * QK_NOPE)
        out["wv"] = w_ukv[:, :, QK_NOPE:].reshape(KV_LORA, dm.heads * V_HEAD)
    if "w_uq" in g:
        out["wuq"] = _head_pad(_cols(g["w_uq"]), dm.heads)
    if "pool_w" in g:
        out["wp"] = jnp.transpose(g["pool_w"], (1, 0, 2, 3)).reshape(len(POOL_WINDOWS), dm.pg, dm.pg)
    for name, key in (("w_branch_a", "wba"), ("w_branch_c", "wbc"), ("w_up", "wup")):
        if name in g:
            out[key] = _cols(g[name])
    for name, key in (("w_branch_b", "wbb"), ("w_o", "wo"), ("w_down", "wdn")):
        if name in g:
            out[key] = g[name].reshape(-1, dm.d)
    return out


def _small_weights(small):
    return dict(
        conv_w=small["conv_w"],
        attn_norm=small["attn_norm"][None], mlp_norm=small["mlp_norm"][None],
        q_lat_norm=small["q_lat_norm"][None], kv_lat_norm=small["kv_lat_norm"][None],
        q_norm=_head_pad(small["q_norm"][None], 1), k_norm=_head_pad(small["k_norm"][None], 1),
        pool_scale=small["pool_scale"][None],
    )


def _grad_piece(dm, dw, name):
    half = QK_ROPE // 2
    rows = lambda a: a.reshape((4, a.shape[0] // 4) + a.shape[1:])
    if name == "w_in":
        dwr, dwg = dw["wr_t"], dw["wg_t"]
        d_t = jnp.concatenate([
            dwr[:dm.r_pool], dwr[dm.r_q:dm.r_kr], dwr[dm.r_kr:dm.r_kr + half],
            dwr[dm.r_kr + 2 * half:dm.r_kr + 3 * half], dwr[dm.r_pool:dm.r_q], dwg], axis=0)
        out = _split_cols(rows(d_t))
    elif name == "w_ukv":
        out = _uncols(jnp.concatenate([dw["wkn"].reshape(KV_LORA, dm.heads, QK_NOPE),
                                       dw["wv"].reshape(KV_LORA, dm.heads, V_HEAD)], axis=-1).reshape(KV_LORA, -1))
    elif name == "w_uq":
        out = _uncols(_head_unpad(dw["wuq"], dm.heads))
    elif name == "pool_w":
        out = jnp.transpose(dw["wp"].reshape(len(POOL_WINDOWS), 4, dm.pg // 4, dm.pg), (1, 0, 2, 3))
    elif name in ("w_branch_a", "w_branch_c", "w_up"):
        out = _uncols(dw[{"w_branch_a": "wba", "w_branch_c": "wbc", "w_up": "wup"}[name]])
    else:
        out = rows(dw[{"w_branch_b": "wbb", "w_o": "wo", "w_down": "wdn"}[name]])
    return out.astype(BF16)


def _layer_fwd(dm, W, x, cos_t, sin_t, tag, more=None, h=None):
    n = lambda s: f"{s}_{tag}"
    if h is None:
        h = _rms_fwd(x, W["attn_norm"], name=n("attn_norm"))
    gl = _mm(h, W["wg_t"], name=n("proj_gates"), tb=True)
    rest = _mm(h, W["wr_t"], name=n("proj_rest"), tb=True)
    if more is not None:
        W.update(more("after_proj", rest))
    y_a = _conv_fwd(rest, W["conv_w"], name=n("conv"), dc=dm.dc)
    y_c = _pool_fwd(rest, W["wp"], W["pool_scale"], name=n("pool"), seg0=dm.r_pool // dm.pg, pg=dm.pg)
    q_lat = _rms_fwd(rest, W["q_lat_norm"], name=n("q_lat_norm"), width=Q_LORA, seg=dm.r_q // Q_LORA)
    kv_lat = _rms_fwd(rest, W["kv_lat_norm"], name=n("kv_lat_norm"), width=KV_LORA, seg=dm.r_kv // KV_LORA)
    q_raw = _mm(q_lat, W["wuq"], name=n("up_q"))
    k_nope = _mm(kv_lat, W["wkn"], name=n("up_k"))
    v = _mm(kv_lat, W["wv"], name=n("up_v"), out_dtype=BF16)
    q, k = _qk_fwd(q_raw, k_nope, rest, cos_t, sin_t, W["q_norm"], W["k_norm"], name=n("qk_norm_rope"),
                   heads=dm.heads, kr_seg=dm.r_kr // HEAD_PAD)
    if more is not None:
        W.update(more("after_qk", q))
    y_b, lse = _flash_fwd(q, k, v, name=n("attention"), heads=dm.heads)
    pa = _mm(y_a, W["wba"], name=n("branch_a"))
    pb = _mm(y_b, W["wbb"], name=n("branch_b"))
    pc = _mm(y_c, W["wbc"], name=n("branch_c"))
    merged = _merge_fwd(gl, pa, pb, pc, name=n("merge"), d=dm.d)
    x1 = _mm(merged, W["wo"], name=n("out_proj"), add=x)
    h2 = _rms_fwd(x1, W["mlp_norm"], name=n("mlp_norm"))
    up, act = _mm(h2, W["wup"], name=n("mlp_up"), epi="relu2")
    x2 = _mm(act, W["wdn"], name=n("mlp_down"), add=x1, tk=2048)
    saved = dict(x=x, h=h, gl=gl, rest=rest, y_a=y_a, y_c=y_c, q_lat=q_lat, kv_lat=kv_lat, q_raw=q_raw, k_nope=k_nope,
                 v=v, q=q, k=k, y_b=y_b, lse=lse, pa=pa, pb=pb, pc=pc, merged=merged, x1=x1, h2=h2, up=up, act=act)
    return x2, saved


def _layer_bwd(dm, W, S, dx2, cos_t, sin_t, tag, hook=None):
    n = lambda s: f"{s}_{tag}"
    dw, ds = {}, {}
    if hook is None:
        hook = lambda point, t, dw_so_far: ()
    dup = _mm(dx2, W["wdn"], name=n("d_mlp_down"), tb=True, aux=S["up"], epi="drelu2", out_dtype=BF16,
              after=hook("start", dx2, dw))
    dw["wdn"] = _mm(S["act"], dx2, name=n("dw_mlp_down"), ta=True, tm=1024, tk=1408)
    dh2 = _mm(dup, W["wup"], name=n("d_mlp_up"), tb=True, tk=2048)
    dw["wup"] = _mm(S["h2"], dup, name=n("dw_mlp_up"), ta=True, tm=1024, tk=1408)
    dx1, ds["mlp_norm"] = _rms_bwd(dh2, S["x1"], W["mlp_norm"], name=n("d_mlp_norm"), res=dx2)
    dmerged = _mm(dx1, W["wo"], name=n("d_out_proj"), tb=True, after=hook("after_mlp", dx1, dw))
    dw["wo"] = _mm(S["merged"], dx1, name=n("dw_out_proj"), ta=True, tm=1024, tk=1408)
    dpa, dpb, dpc, dg0, dg1, dg2 = _merge_bwd(dmerged, S["gl"], S["pa"], S["pb"], S["pc"], name=n("d_merge"), d=dm.d)
    dgl = jnp.concatenate([dg0, dg1, dg2], axis=1)
    dy_a = _mm(dpa, W["wba"], name=n("d_branch_a"), tb=True)
    dw["wba"] = _mm(S["y_a"], dpa, name=n("dw_branch_a"), ta=True, tm=1024, tk=1408)
    dy_b = _mm(dpb, W["wbb"], name=n("d_branch_b"), tb=True, out_dtype=BF16)
    dw["wbb"] = _mm(S["y_b"], dpb, name=n("dw_branch_b"), ta=True, tm=1024, tk=1408)
    dy_c = _mm(dpc, W["wbc"], name=n("d_branch_c"), tb=True)
    dw["wbc"] = _mm(S["y_c"], dpc, name=n("dw_branch_c"), ta=True, tm=1024, tk=1408)
    dq, dk, dv = _flash_bwd(S["q"], S["k"], S["v"], S["y_b"], dy_b, S["lse"], name=n("d_attention"), heads=dm.heads)
    after_attention = hook("after_attention", dq, dw)
    dq_raw, dk_nope, dk_rope, dgq, dgk = _qk_bwd(
        dq, dk, S["q_raw"], S["k_nope"], S["rest"], cos_t, sin_t, W["q_norm"], W["k_norm"], name=n("d_qk_norm_rope"),
        heads=dm.heads, kr_seg=dm.r_kr // HEAD_PAD)
    ds["q_norm"] = _head_unpad(dgq, 1)
    ds["k_norm"] = _head_unpad(dgk, 1)
    dkv_v = _mm(dv, W["wv"], name=n("d_up_v"), tb=True, after=after_attention)
    dq_lat_n = _mm(dq_raw, W["wuq"], name=n("d_up_q"), tb=True, tk=2048, after=hook("after_qk", dq_raw, dw))
    dw["wuq"] = _mm(S["q_lat"], dq_raw, name=n("dw_up_q"), ta=True, tm=512, tk=1408)
    dkv_lat_n = _mm(dk_nope, W["wkn"], name=n("d_up_k"), tb=True, add=dkv_v)
    dw["wkn"] = _mm(S["kv_lat"], dk_nope, name=n("dw_up_k"), ta=True, tm=512, tk=1408)
    dw["wv"] = _mm(S["kv_lat"], dv, name=n("dw_up_v"), ta=True, tm=512, tk=1408)
    dq_lat, ds["q_lat_norm"] = _rms_bwd(dq_lat_n, S["rest"], W["q_lat_norm"], name=n("d_q_lat_norm"), width=Q_LORA,
                                        seg=dm.r_q // Q_LORA, out_dtype=BF16)
    dkv_lat, ds["kv_lat_norm"] = _rms_bwd(dkv_lat_n, S["rest"], W["kv_lat_norm"], name=n("d_kv_lat_norm"), width=KV_LORA,
                                          seg=dm.r_kv // KV_LORA, out_dtype=BF16)
    du, db, dc, ds["conv_w"] = _conv_bwd(S["rest"], W["conv_w"], dy_a, name=n("d_conv"), dc=dm.dc)
    dpool, dw["wp"], ds["pool_scale"] = _pool_bwd(S["rest"], W["wp"], W["pool_scale"], dy_c, name=n("d_pool"),
                                                  seg0=dm.r_pool // dm.pg, pg=dm.pg)
    drest = jnp.concatenate([du, db, dc, dpool, dq_lat, dkv_lat, dk_rope], axis=1)
    dw["wg_t"] = _mm(dgl, S["h"], name=n("dw_proj_gates"), ta=True, tm=1024, tk=1408)
    dw["wr_t"] = _mm(drest, S["h"], name=n("dw_proj_rest"), ta=True, tm=1024, tk=1408)
    dh_g = _mm(dgl, W["wg_t"], name=n("d_proj_gates"), tk=2048, after=hook("after_dw_in", dw["wr_t"], dw))
    dh = _mm(drest, W["wr_t"], name=n("d_proj_rest"), add=dh_g, tk=1792, after=hook("after_dh_gates", dh_g, dw))
    dx, ds["attn_norm"] = _rms_bwd(dh, S["x"], W["attn_norm"], name=n("d_attn_norm"), res=dx1)
    return dx, dw, ds


BIG = ("w_in", "w_uq", "w_ukv", "pool_w", "w_branch_a", "w_branch_b", "w_branch_c", "w_o", "w_up", "w_down")
REPLICATED = ("attn_norm", "q_lat_norm", "kv_lat_norm", "q_norm", "k_norm", "pool_scale", "mlp_norm")
WEIGHTS = ("meta_tokens", "attn_norm", "w_in", "conv_w", "q_lat_norm", "kv_lat_norm", "w_uq", "w_ukv", "q_norm",
           "k_norm", "pool_w", "pool_scale", "w_branch_a", "w_branch_b", "w_branch_c", "w_o", "mlp_norm", "w_up",
           "w_down")


def _pack(arrays):
    flat = jnp.concatenate([a.reshape(-1).astype(F32) for a in arrays])
    pad = (-flat.shape[0]) % (8 * LANES)
    return jnp.pad(flat, (0, pad)).reshape(-1, LANES)


def _unpack(flat, shapes):
    out, pos = [], 0
    flat = flat.reshape(-1)
    for shp in shapes:
        size = math.prod(shp)
        out.append(flat[pos:pos + size].reshape(shp))
        pos += size
    return out


def _update(w, g, m, v, name):
    shp = w.shape
    to2 = lambda a: a.reshape(-1, shp[-1])
    delta, nm, nv = _adamw(to2(w), to2(g), to2(m), to2(v), name=name)
    return delta.reshape(shp), nm.reshape(shp), nv.reshape(shp)


def _step(args):
    x = args["x"][0]
    seq, d = x.shape
    dm = _Dims(d, seq)
    xi, yi, ci = _coords()
    chip = 2 * xi + yi

    small_w = _gather_all(_pack([args["conv_w"], args["meta_tokens"]]), name="gather_small_weights")
    args = dict(args)
    for p in ("", "m_", "v_"):
        args[p + "w_in"] = jnp.swapaxes(args[p + "w_in"], 1, 2)
    order = [(k, l) for l in range(2) for k in BIG]
    shards = {n: args[n[0]][n[1]].astype(BF16) for n in order}
    for l in range(2):
        shards[("w_in", l)] = _split_cols(shards[("w_in", l)])
    small_w, shards[order[0]] = lax.optimization_barrier((small_w, shards[order[0]]))
    lands = {n: lax.empty((4,) + shards[n].shape, BF16) for n in order}
    last = ("w_up", "w_down")
    group_names = [[("w_in", 0)], [(k, 0) for k in BIG[1:] if k not in last], [(k, 0) for k in last],
                   [(k, 1) for k in BIG]]
    first, others = order[0], order[1:]
    sems, thru, token = _start_copies([shards[first], lands[first]], [_ici_gather_plan([(0, 1)])],
                                      name="start_gather_ici_first")
    shards[first], lands[first] = thru
    at = {n: i for i, n in enumerate(others)}
    sems_b, thru, token_b = _start_copies(
        [shards[n] for n in others] + [lands[n] for n in others] + [token],
        [_ici_gather_plan([(at[n], len(others) + at[n]) for n in g]) for g in group_names[1:]], name="start_gather_ici")
    sems = sems + sems_b
    for i, n in enumerate(others):
        shards[n], lands[n] = thru[i], thru[len(others) + i]

    def finish_gather(g, after, tag):
        names = group_names[g]
        k = len(names)
        plan, _ = _ici_gather_plan([(i, k + i) for i in range(k)])
        got = _wait_copies([shards[n] for n in names] + [lands[n] for n in names], sems[g], plan, after,
                           name=f"wait_gather_ici_{tag}")
        for i, n in enumerate(names):
            shards[n] = got[i]
        fwd = _d2d_forward_plan(list(range(k)))
        sems2, bufs2, tok2 = _start_copies(got[k:], [fwd], name=f"start_gather_d2d_{tag}")
        return names, bufs2, sems2[0], fwd[0], tok2

    def land_gather(pending, after, tag):
        names, bufs2, sems2, plan, tok2 = pending
        done = _wait_copies(bufs2, sems2, plan, tok2 if after is None else after, name=f"wait_gather_d2d_{tag}")
        return {n[0]: buf for n, buf in zip(names, done)}

    conv_shape, meta_shape = args["conv_w"].shape, args["meta_tokens"].shape
    per_chip = [_unpack(small_w[2 * j], [conv_shape, meta_shape]) for j in range(4)]
    conv_full = jnp.concatenate([p[0] for p in per_chip], axis=-1)
    meta_full = jnp.concatenate([p[1] for p in per_chip], axis=-1)

    layers = []
    for l in range(2):
        small = {k: args[k][l] for k in REPLICATED}
        small["conv_w"] = conv_full[l]
        layers.append(_small_weights(small))

    pos = jnp.arange(dm.t, dtype=F32)
    inv = ROPE_THETA ** (-jnp.arange(0, QK_ROPE, 2, dtype=F32) / QK_ROPE)
    ang = pos[:, None] * inv[None, :]
    cos_t = _rope_pad(jnp.cos(ang), jnp.cos(ang))
    sin_t = _rope_pad(-jnp.sin(ang), jnp.sin(ang))
    tail = jnp.zeros((dm.t - dm.t_real, d), F32)
    h0 = jnp.concatenate([meta_full, x, tail], axis=0)
    target = jnp.concatenate([jnp.zeros((N_META, d), F32), args["loss_target"][0], tail], axis=0)

    h_first = _rms_fwd(h0, layers[0]["attn_norm"], name="attn_norm_l0", after=(token, token_b))
    layers[0].update(_in_weights(dm, land_gather(finish_gather(0, h_first, "l0_in"), None, "l0_in")["w_in"]))
    def rest_of_layer0(point, after):
        g, tag = (1, "l0_mid") if point == "after_proj" else (2, "l0_mlp")
        return _other_weights(dm, land_gather(finish_gather(g, after, tag), None, tag))

    h1, saved0 = _layer_fwd(dm, layers[0], h0, cos_t, sin_t, "l0", more=rest_of_layer0, h=h_first)
    g1 = land_gather(finish_gather(3, saved0["y_b"], "l1"), h1, "l1")
    layers[1].update(_in_weights(dm, g1["w_in"]))
    layers[1].update(_other_weights(dm, g1))
    h2, saved1 = _layer_fwd(dm, layers[1], h1, cos_t, sin_t, "l1")
    sq, dy = _loss(h2, target, name="loss_head", first=N_META, last=dm.t_real)
    loss = lax.psum(0.5 / d * sq[0, 0], ("x", "y", "c"))
    core, chip_flags = _one_hot(ci, 2), _one_hot(chip, 4)

    class Reduce:
        def __init__(self, names, dw, tag):
            self.names, self.tag, self.nb = names, tag, len(names)
            self.idx = [(i, self.nb + i) for i in range(self.nb)]
            parts = [_as3d(_grad_piece(dm, dw, k)) for k in names]
            recv = [lax.empty((4, p.shape[1] // 2, p.shape[2]), BF16) for p in parts]
            self.plan = _swap_half_plan(self.idx)
            self.sems, self.bufs, self.token = _start_copies(parts + recv, [self.plan], name=f"start_swap_{tag}")

        def _land(self, after, what):
            return _wait_copies(self.bufs, self.sems[0], self.plan[0], self.token if after is None else after,
                                name=f"wait_{what}_{self.tag}")

        def scatter(self, after=None):
            got = self._land(after, "swap")
            pairs = [_pair_sum(got[i], got[j], core, name=f"pair_sum_{k}_{self.tag}")
                     for (i, j), k in zip(self.idx, self.names)]
            self.plan = _scatter_plan(self.idx)
            self.sems, self.bufs, self.token = _start_copies(pairs + [lax.empty(p.shape, BF16) for p in pairs],
                                                             [self.plan], name=f"start_scatter_{self.tag}")
            return self.token

        def totals(self, after=None):
            got = self._land(after, "scatter")
            sums = [_chip_sum(got[i], got[j], chip_flags, name=f"chip_sum_{k}_{self.tag}")
                    for (i, j), k in zip(self.idx, self.names)]
            self.plan = _swap_total_plan(self.idx)
            self.sems, self.bufs, self.token = _start_copies(sums + [lax.empty(t.shape, F32) for t in sums],
                                                             [self.plan], name=f"start_swap_total_{self.tag}")
            return self.token

        def finish(self, after=None):
            got = self._land(after, "swap_total")
            return {k: (got[i], got[j]) for (i, j), k in zip(self.idx, self.names)}

    dh1, dw1, ds1 = _layer_bwd(dm, layers[1], saved1, dy, cos_t, sin_t, "l1",
                               hook=lambda point, t, dw: (loss.reshape(1, 1),) if point == "start" else ())
    early = ("w_down", "w_up", "w_o", "w_branch_a", "w_branch_b", "w_branch_c")
    late = tuple(k for k in BIG if k not in early)
    stage = {}

    def during_layer0(point, t, dw):
        if point == "start":
            stage["l1"] = Reduce(BIG, dw1, "l1")
            return (stage["l1"].token,)
        if point == "after_mlp":
            return (stage["l1"].scatter(after=t),)
        if point == "after_attention":
            tok = stage["l1"].totals(after=t)
            stage["l0a"] = Reduce(early, dw, "l0a")
            return (tok, stage["l0a"].token)
        if point == "after_qk":
            stage["red1"] = stage["l1"].finish(after=t)
            return (stage["l0a"].scatter(after=t),)
        if point == "after_dw_in":
            tok = stage["l0a"].totals(after=t)
            stage["l0b"] = Reduce(late, dw, "l0b")
            return (tok, stage["l0b"].token)
        return (stage["l0b"].scatter(after=t),)

    dh0, dw0, ds0 = _layer_bwd(dm, layers[0], saved0, dh1, cos_t, sin_t, "l0", hook=during_layer0)
    grad_x = dh0[N_META:dm.t_real][None]
    stage["l0b"].totals(after=dh0)
    red1 = stage["red1"]
    red0 = {**stage["l0a"].finish(), **stage["l0b"].finish()}
    grads = {}

    small_names = REPLICATED + ("conv_w",)
    small_parts = [jnp.stack([ds0[k].reshape(ds0[k].shape[-2:] if k == "conv_w" else (-1,)),
                              ds1[k].reshape(ds1[k].shape[-2:] if k == "conv_w" else (-1,))]) for k in small_names]
    small_parts.append(dh0[:N_META])
    small_all = _gather_all(_pack(small_parts), name="gather_small_grads")
    small_sum = _sum_stack(small_all, name="sum_small_grads", out_dtype=F32)
    small_g = dict(zip(small_names + ("meta_tokens",), _unpack(small_sum, [p.shape for p in small_parts])))
    for k in REPLICATED:
        grads[k] = small_g[k]
    dcw = conv_shape[-1]
    grads["conv_w"] = lax.dynamic_slice_in_dim(small_g["conv_w"], chip * dcw, dcw, axis=2)
    dmeta = meta_shape[-1]
    grads["meta_tokens"] = lax.dynamic_slice_in_dim(small_g["meta_tokens"], chip * dmeta, dmeta, axis=1)

    delta, new_m, new_v = {}, {}, {}
    for k in WEIGHTS:
        shp = args[k].shape
        if k in BIG:
            wmv = [args[p + k].reshape(2, -1, shp[-1]) for p in ("", "m_", "v_")]
            by_cols = k == "w_in"
            out = _adamw_layer(*wmv, *red1[k], core, 1, None, name=f"adamw_{k}_l1", col_halves=by_cols)
            out = _adamw_layer(*wmv, *red0[k], core, 0, out, name=f"adamw_{k}_l0", col_halves=by_cols)
            out = [o.reshape(shp) for o in out]
            grads[k], delta[k], new_m[k], new_v[k] = [jnp.swapaxes(o, 1, 2) for o in out] if by_cols else out
        else:
            grads[k] = grads[k].reshape(shp)
            delta[k], new_m[k], new_v[k] = _update(args[k], grads[k], args["m_" + k], args["v_" + k], f"adamw_{k}")
    return (loss, grad_x, *[grads[k] for k in WEIGHTS], *[delta[k] for k in WEIGHTS],
            *[new_m[k] for k in WEIGHTS], *[new_v[k] for k in WEIGHTS])


def kernel(x, meta_tokens, attn_norm, w_in, conv_w, q_lat_norm, kv_lat_norm, w_uq, w_ukv, q_norm, k_norm, pool_w, pool_scale, w_branch_a, w_branch_b, w_branch_c, w_o, mlp_norm, w_up, w_down, loss_target, m_meta_tokens, m_attn_norm, m_w_in, m_conv_w, m_q_lat_norm, m_kv_lat_norm, m_w_uq, m_w_ukv, m_q_norm, m_k_norm, m_pool_w, m_pool_scale, m_w_branch_a, m_w_branch_b, m_w_branch_c, m_w_o, m_mlp_norm, m_w_up, m_w_down, v_meta_tokens, v_attn_norm, v_w_in, v_conv_w, v_q_lat_norm, v_kv_lat_norm, v_w_uq, v_w_ukv, v_q_norm, v_k_norm, v_pool_w, v_pool_scale, v_w_branch_a, v_w_branch_b, v_w_branch_c, v_w_o, v_mlp_norm, v_w_up, v_w_down):
    return _step(dict(locals()))
```

```python
import functools
import math

import jax
import jax.numpy as jnp
from jax import lax
from jax.experimental import pallas as pl
from jax.experimental.pallas import tpu as pltpu

F32 = jnp.float32
BF16 = jnp.bfloat16
MESH = pl.DeviceIdType.MESH

EPS = 1e-6
N_META = 16
QK_NOPE = 128
QK_ROPE = 64
QK_HEAD = QK_NOPE + QK_ROPE
V_HEAD = 128
HEAD_PAD = 256
Q_LORA = 512
KV_LORA = 512
ROPE_THETA = 10000.0
POOL_WINDOWS = (2, 4, 8, 16)
HALO = 16
LANES = 128
ADAM_LR = 0.001
ADAM_B1 = 0.9
ADAM_B2 = 0.999
ADAM_EPS = 1e-08
ADAM_WD = 0.01
ADAM_STEP = 10
VMEM_LIMIT = 52 * 1024 * 1024
NEG = -1e30
ATTN_SCALE = QK_HEAD ** -0.5
LOG2_E = 1.4426950408889634
Q_FOLD = ATTN_SCALE * LOG2_E


def _tile(n, target, mult=LANES):
    best = None
    for t in range(mult, min(n, target) + 1, mult):
        if n % t == 0:
            best = t
    return n if best is None else best


def _params(sem=None):
    return pltpu.CompilerParams(dimension_semantics=sem, vmem_limit_bytes=VMEM_LIMIT)


def _mm(a, b, *, name, ta=False, tb=False, add=None, aux=None, epi=None, out_dtype=F32,
        tm=1056, tn=1024, tk=None, after=()):
    if ta:
        K, M = a.shape
    else:
        M, K = a.shape
    if tb:
        N, kb = b.shape
    else:
        kb, N = b.shape
    assert K == kb, (a.shape, b.shape, ta, tb)
    tm = _tile(M, tm, LANES if ta else 16)
    tn = _tile(N, tn, LANES)
    tk = K if tk is None else _tile(K, tk, LANES if (not ta or tb) else 16)
    nk = K // tk
    a_bytes, b_bytes = a.size * a.dtype.itemsize, b.size * b.dtype.itemsize
    j_outer = nk == 1 and a_bytes * (N // tn) + b_bytes < a_bytes + b_bytes * (M // tm)
    grid = (N // tn, M // tm, nk) if j_outer else (M // tm, N // tn, nk)
    row = (lambda g0, g1: g1) if j_outer else (lambda g0, g1: g0)
    col = (lambda g0, g1: g0) if j_outer else (lambda g0, g1: g1)

    if ta:
        a_spec = pl.BlockSpec((tk, tm), lambda g0, g1, k: (k, row(g0, g1)))
    else:
        a_spec = pl.BlockSpec((tm, tk), lambda g0, g1, k: (row(g0, g1), k))
    if tb:
        b_spec = pl.BlockSpec((tn, tk), lambda g0, g1, k: (col(g0, g1), k))
    else:
        b_spec = pl.BlockSpec((tk, tn), lambda g0, g1, k: (k, col(g0, g1)))
    o_spec = pl.BlockSpec((tm, tn), lambda g0, g1, k: (row(g0, g1), col(g0, g1)))
    in_specs = [a_spec, b_spec]
    operands = [a, b]
    if add is not None:
        in_specs.append(o_spec)
        operands.append(add)
    if aux is not None:
        in_specs.append(o_spec)
        operands.append(aux)
    after = tuple(after)
    in_specs += [pl.BlockSpec(memory_space=pl.ANY)] * len(after)
    operands += list(after)
    if epi == "relu2":
        out_shape = (jax.ShapeDtypeStruct((M, N), BF16), jax.ShapeDtypeStruct((M, N), BF16))
        out_specs = (o_spec, o_spec)
    else:
        out_shape = jax.ShapeDtypeStruct((M, N), out_dtype)
        out_specs = o_spec
    dims = (((0 if ta else 1,), (1 if tb else 0,)), ((), ()))
    has_add, has_aux = add is not None, aux is not None

    def body(*refs):
        a_ref, b_ref = refs[0], refs[1]
        pos = 2
        add_ref = aux_ref = None
        if has_add:
            add_ref = refs[pos]
            pos += 1
        if has_aux:
            aux_ref = refs[pos]
            pos += 1
        pos += len(after)
        n_out = 2 if epi == "relu2" else 1
        out_refs = refs[pos:pos + n_out]
        acc_ref = refs[pos + n_out] if nk > 1 else None

        part = lax.dot_general(a_ref[...].astype(BF16), b_ref[...].astype(BF16), dims,
                               preferred_element_type=F32)

        def finish(acc):
            if has_add:
                acc = acc + add_ref[...].astype(F32)
            if epi == "relu2":
                r = jnp.maximum(acc, 0.0)
                out_refs[0][...] = acc.astype(BF16)
                out_refs[1][...] = (r * r).astype(BF16)
            elif epi == "drelu2":
                u = aux_ref[...].astype(F32)
                out_refs[0][...] = (acc * (2.0 * jnp.maximum(u, 0.0))).astype(out_dtype)
            else:
                out_refs[0][...] = acc.astype(out_dtype)

        if nk == 1:
            finish(part)
        else:
            k = pl.program_id(2)

            @pl.when(k == 0)
            def _():
                acc_ref[...] = part

            @pl.when(k > 0)
            def _():
                acc_ref[...] += part

            @pl.when(k == nk - 1)
            def _():
                finish(acc_ref[...])

    scratch = [pltpu.VMEM((tm, tn), F32)] if nk > 1 else []
    return pl.pallas_call(
        body, name=name, grid=grid, in_specs=in_specs, out_specs=out_specs, out_shape=out_shape,
        scratch_shapes=scratch, compiler_params=_params(("parallel", "parallel", "arbitrary")),
    )(*operands)


def _rms_fwd(x, g, *, name, width=None, seg=0, tm=384, after=()):
    T = x.shape[0]
    width = x.shape[1] if width is None else width
    tm = _tile(T, tm, 16)
    after = tuple(after)

    def body(x_ref, g_ref, *rest):
        xf = x_ref[...].astype(F32)
        r = lax.rsqrt(jnp.mean(xf * xf, axis=-1, keepdims=True) + EPS)
        rest[-1][...] = (xf * r * g_ref[...]).astype(BF16)

    return pl.pallas_call(
        body, name=name, grid=(T // tm,),
        in_specs=[pl.BlockSpec((tm, width), lambda i: (i, seg)), pl.BlockSpec((1, width), lambda i: (0, 0))]
        + [pl.BlockSpec(memory_space=pl.ANY)] * len(after),
        out_specs=pl.BlockSpec((tm, width), lambda i: (i, 0)),
        out_shape=jax.ShapeDtypeStruct((T, width), BF16),
        compiler_params=_params(("parallel",)),
    )(x, g, *after)


def _rms_bwd(dy, x, g, *, name, width=None, seg=0, res=None, out_dtype=F32, tm=384, bf16_copy=False):
    T = x.shape[0]
    width = x.shape[1] if width is None else width
    tm = _tile(T, tm, 16)
    has_res = res is not None

    def body(*refs):
        dy_ref, x_ref, g_ref = refs[:3]
        res_ref = refs[3] if has_res else None
        dx_ref, dg_ref = refs[4 if has_res else 3], refs[-1]
        xf = x_ref[...].astype(F32)
        dyf = dy_ref[...].astype(F32)
        r = lax.rsqrt(jnp.mean(xf * xf, axis=-1, keepdims=True) + EPS)
        xhat = xf * r
        dyh = dyf * g_ref[...]
        dx = r * (dyh - xhat * jnp.mean(dyh * xhat, axis=-1, keepdims=True))
        if has_res:
            dx = dx + res_ref[...].astype(F32)
        dx_ref[...] = dx.astype(out_dtype)
        if bf16_copy:
            refs[-2][...] = dx.astype(BF16)
        part = jnp.sum(dyf * xhat, axis=0, keepdims=True)

        @pl.when(pl.program_id(0) == 0)
        def _():
            dg_ref[...] = part

        @pl.when(pl.program_id(0) > 0)
        def _():
            dg_ref[...] += part

    row = pl.BlockSpec((tm, width), lambda i: (i, 0))
    in_specs = [row, pl.BlockSpec((tm, width), lambda i: (i, seg)), pl.BlockSpec((1, width), lambda i: (0, 0))]
    operands = [dy, x, g]
    if has_res:
        in_specs.append(row)
        operands.append(res)
    vec = pl.BlockSpec((1, width), lambda i: (0, 0))
    full = [jax.ShapeDtypeStruct((T, width), out_dtype)] + ([jax.ShapeDtypeStruct((T, width), BF16)] if bf16_copy else [])
    return pl.pallas_call(
        body, name=name, grid=(T // tm,), in_specs=in_specs,
        out_specs=tuple([row] * len(full) + [vec]),
        out_shape=tuple(full + [jax.ShapeDtypeStruct((1, width), F32)]),
        compiler_params=_params(("arbitrary",)),
    )(*operands)


def _down(ext, k):
    return pltpu.roll(ext, k, 0)


def _up(ext, k):
    return pltpu.roll(ext, ext.shape[0] - k, 0)


def _pre_halo(ref, r, R):
    start = pl.multiple_of(jnp.maximum(r * R - HALO, 0), 8)
    keep = (r > 0).astype(F32)
    return ref[pl.ds(start, HALO), :].astype(F32) * keep


def _post_halo(ref, r, R, n_chunks):
    start = pl.multiple_of(jnp.minimum(r * R + R, (n_chunks - 1) * R + R - HALO), 8)
    keep = (r < n_chunks - 1).astype(F32)
    return ref[pl.ds(start, HALO), :].astype(F32) * keep


def _chunk(ref, r, R):
    return ref[pl.ds(pl.multiple_of(r * R, 8), R), :].astype(F32)


def _conv_fwd(rest, conv_w, *, name, dc, tc=128, rows=1056):
    T = rest.shape[0]
    tc = _tile(dc, tc)
    nb = dc // tc
    R = _tile(T, rows, 16)
    n_chunks = T // R

    def body(u_ref, b_ref, c_ref, w_ref, y_ref):
        w0, w1, w2 = w_ref[0:1, :], w_ref[1:2, :], w_ref[2:3, :]

        def chunk(r, carry):
            cu = _chunk(c_ref, r, R) * _chunk(u_ref, r, R)
            ext = jnp.concatenate([_pre_halo(c_ref, r, R) * _pre_halo(u_ref, r, R), cu], axis=0)
            conv = w0 * _down(ext, 2)[HALO:] + w1 * _down(ext, 1)[HALO:] + w2 * cu
            y_ref[pl.ds(pl.multiple_of(r * R, 8), R), :] = (_chunk(b_ref, r, R) * conv).astype(BF16)
            return carry

        lax.fori_loop(0, n_chunks, chunk, 0)

    col = lambda off: pl.BlockSpec((T, tc), lambda j: (0, off * nb + j))
    return pl.pallas_call(
        body, name=name, grid=(nb,),
        in_specs=[col(0), col(1), col(2), pl.BlockSpec((3, tc), lambda j: (0, j))],
        out_specs=pl.BlockSpec((T, tc), lambda j: (0, j)),
        out_shape=jax.ShapeDtypeStruct((T, dc), BF16),
        compiler_params=_params(("parallel",)),
    )(rest, rest, rest, conv_w)


def _conv_bwd(rest, conv_w, dy, *, name, dc, tc=128, rows=1056):
    T = rest.shape[0]
    tc = _tile(dc, tc)
    nb = dc // tc
    R = _tile(T, rows, 16)
    n_chunks = T // R

    def body(u_ref, b_ref, c_ref, w_ref, dy_ref, du_ref, db_ref, dc_ref, dw_ref):
        w0, w1, w2 = w_ref[0:1, :], w_ref[1:2, :], w_ref[2:3, :]

        def chunk(r, carry):
            a0, a1, a2 = carry
            u, b, c = _chunk(u_ref, r, R), _chunk(b_ref, r, R), _chunk(c_ref, r, R)
            dy_c = _chunk(dy_ref, r, R)
            cu = c * u
            ext = jnp.concatenate([_pre_halo(c_ref, r, R) * _pre_halo(u_ref, r, R), cu], axis=0)
            cu1, cu2 = _down(ext, 1)[HALO:], _down(ext, 2)[HALO:]
            conv = w0 * cu2 + w1 * cu1 + w2 * cu
            dconv = dy_c * b
            dext = jnp.concatenate(
                [dconv, _post_halo(dy_ref, r, R, n_chunks) * _post_halo(b_ref, r, R, n_chunks)], axis=0)
            dcu = w2 * dconv + w1 * _up(dext, 1)[:R] + w0 * _up(dext, 2)[:R]
            rows_at = pl.ds(pl.multiple_of(r * R, 8), R)
            db_ref[rows_at, :] = (dy_c * conv).astype(BF16)
            du_ref[rows_at, :] = (dcu * c).astype(BF16)
            dc_ref[rows_at, :] = (dcu * u).astype(BF16)
            return (a0 + jnp.sum(dconv * cu2, axis=0, keepdims=True),
                    a1 + jnp.sum(dconv * cu1, axis=0, keepdims=True),
                    a2 + jnp.sum(dconv * cu, axis=0, keepdims=True))

        zero = jnp.zeros((1, tc), F32)
        a0, a1, a2 = lax.fori_loop(0, n_chunks, chunk, (zero, zero, zero))
        dw_ref[0:1, :] = a0
        dw_ref[1:2, :] = a1
        dw_ref[2:3, :] = a2

    col = lambda off: pl.BlockSpec((T, tc), lambda j: (0, off * nb + j))
    own = pl.BlockSpec((T, tc), lambda j: (0, j))
    return pl.pallas_call(
        body, name=name, grid=(nb,),
        in_specs=[col(0), col(1), col(2), pl.BlockSpec((3, tc), lambda j: (0, j)), own],
        out_specs=(own, own, own, pl.BlockSpec((3, tc), lambda j: (0, j))),
        out_shape=(jax.ShapeDtypeStruct((T, dc), BF16),) * 3 + (jax.ShapeDtypeStruct((3, dc), F32),),
        compiler_params=_params(("parallel",)),
    )(rest, rest, rest, conv_w, dy)


def _window_count(r, R, n_rows, w, first_row_offset):
    t = lax.broadcasted_iota(jnp.int32, (n_rows, 1), 0) + (r * R + first_row_offset)
    return jnp.minimum(t + 1, w).astype(F32)


def _pool_fwd(rest, pool_w, pool_scale, *, name, seg0, pg, rows=1056):
    T = rest.shape[0]
    R = _tile(T, rows, 16)
    n_chunks = T // R
    n_groups = len(POOL_WINDOWS)

    def body(x_ref, w_ref, s_ref, y_ref):
        def run(window):
            def chunk(r, carry):
                g = _chunk(x_ref, r, R)
                s = jnp.concatenate([_pre_halo(x_ref, r, R), g], axis=0)
                k = 1
                while k < window:
                    s = s + _down(s, k)
                    k *= 2
                pooled = s[HALO:] / _window_count(r, R, R, window, 0) - g
                mixed = jnp.dot(pooled.astype(BF16), w_ref[0], preferred_element_type=F32)
                y_ref[pl.ds(pl.multiple_of(r * R, 8), R), :] = (mixed * s_ref[...]).astype(BF16)
                return carry

            lax.fori_loop(0, n_chunks, chunk, 0)

        for gi, window in enumerate(POOL_WINDOWS):
            pl.when(pl.program_id(0) == gi)(functools.partial(run, window))

    return pl.pallas_call(
        body, name=name, grid=(n_groups,),
        in_specs=[pl.BlockSpec((T, pg), lambda g: (0, seg0 + g)),
                  pl.BlockSpec((1, pg, pg), lambda g: (g, 0, 0)),
                  pl.BlockSpec((1, pg), lambda g: (0, g))],
        out_specs=pl.BlockSpec((T, pg), lambda g: (0, g)),
        out_shape=jax.ShapeDtypeStruct((T, n_groups * pg), BF16),
        compiler_params=_params(("parallel",)),
    )(rest, pool_w, pool_scale)


def _pool_bwd(rest, pool_w, pool_scale, dy, *, name, seg0, pg, rows=1056):
    T = rest.shape[0]
    R = _tile(T, rows, 16)
    n_chunks = T // R
    n_groups = len(POOL_WINDOWS)

    def body(x_ref, w_ref, s_ref, dy_ref, dx_ref, dw_ref, ds_ref):
        def run(window):
            def chunk(r, carry):
                dw_acc, ds_acc = carry
                g = _chunk(x_ref, r, R)
                s = jnp.concatenate([_pre_halo(x_ref, r, R), g], axis=0)
                k = 1
                while k < window:
                    s = s + _down(s, k)
                    k *= 2
                pooled = (s[HALO:] / _window_count(r, R, R, window, 0) - g).astype(BF16)
                mixed = jnp.dot(pooled, w_ref[0], preferred_element_type=F32)
                dy_c = _chunk(dy_ref, r, R)
                dm_ext = (jnp.concatenate([dy_c, _post_halo(dy_ref, r, R, n_chunks)], axis=0)
                          * s_ref[...]).astype(BF16)
                dpool_ext = lax.dot_general(dm_ext, w_ref[0], (((1,), (1,)), ((), ())),
                                            preferred_element_type=F32)
                a = dpool_ext / _window_count(r, R, R + HALO, window, 0)
                k = 1
                while k < window:
                    a = a + _up(a, k)
                    k *= 2
                dx_ref[pl.ds(pl.multiple_of(r * R, 8), R), :] = (a[:R] - dpool_ext[:R]).astype(BF16)
                dw_acc = dw_acc + lax.dot_general(pooled, dm_ext[:R], (((0,), (0,)), ((), ())),
                                                  preferred_element_type=F32)
                ds_acc = ds_acc + jnp.sum(dy_c * mixed, axis=0, keepdims=True)
                return dw_acc, ds_acc

            dw_acc, ds_acc = lax.fori_loop(0, n_chunks, chunk,
                                           (jnp.zeros((pg, pg), F32), jnp.zeros((1, pg), F32)))
            dw_ref[0] = dw_acc
            ds_ref[...] = ds_acc

        for gi, window in enumerate(POOL_WINDOWS):
            pl.when(pl.program_id(0) == gi)(functools.partial(run, window))

    own = pl.BlockSpec((T, pg), lambda g: (0, g))
    return pl.pallas_call(
        body, name=name, grid=(n_groups,),
        in_specs=[pl.BlockSpec((T, pg), lambda g: (0, seg0 + g)),
                  pl.BlockSpec((1, pg, pg), lambda g: (g, 0, 0)),
                  pl.BlockSpec((1, pg), lambda g: (0, g)), own],
        out_specs=(own, pl.BlockSpec((1, pg, pg), lambda g: (g, 0, 0)), pl.BlockSpec((1, pg), lambda g: (0, g))),
        out_shape=(jax.ShapeDtypeStruct((T, n_groups * pg), BF16),
                   jax.ShapeDtypeStruct((n_groups, pg, pg), F32),
                   jax.ShapeDtypeStruct((1, n_groups * pg), F32)),
        compiler_params=_params(("parallel",)),
    )(rest, pool_w, pool_scale, dy)


def _rope(r, cos_t, sin_t):
    return r * cos_t + pltpu.roll(r, LANES // 2, 1) * sin_t


def _rope_t(d, cos_t, sin_t):
    return d * cos_t + pltpu.roll(d * sin_t, LANES // 2, 1)


def _qk_fwd(q_raw, k_nope, rest, cos_t, sin_t, q_norm, k_norm, *, name, heads, kr_seg, tm=192):
    T = q_raw.shape[0]
    tm = _tile(T, tm, 16)

    def body(q_ref, kn_ref, kr_ref, c_ref, s_ref, gq_ref, gk_ref, qo_ref, ko_ref):
        cos_b, sin_b = c_ref[...], s_ref[...]
        kr = kr_ref[:, 0:LANES]
        kr_ss = jnp.sum(kr * kr, axis=-1, keepdims=True)
        gq, gk = gq_ref[...], gk_ref[...]
        for h in range(heads):
            lo = h * HEAD_PAD
            q = q_ref[:, lo:lo + HEAD_PAD]
            rq = lax.rsqrt(jnp.sum(q * q, axis=-1, keepdims=True) / QK_HEAD + EPS)
            qn = q * (rq * Q_FOLD) * gq
            qo_ref[:, lo:lo + LANES] = qn[:, :LANES].astype(BF16)
            qo_ref[:, lo + LANES:lo + HEAD_PAD] = _rope(qn[:, LANES:], cos_b, sin_b).astype(BF16)
            kn = kn_ref[:, h * LANES:(h + 1) * LANES]
            rk = lax.rsqrt((jnp.sum(kn * kn, axis=-1, keepdims=True) + kr_ss) / QK_HEAD + EPS)
            ko_ref[:, lo:lo + LANES] = (kn * rk * gk[:, :LANES]).astype(BF16)
            ko_ref[:, lo + LANES:lo + HEAD_PAD] = _rope(kr * rk * gk[:, LANES:], cos_b, sin_b).astype(BF16)

    wq, wk = heads * HEAD_PAD, heads * LANES
    return pl.pallas_call(
        body, name=name, grid=(T // tm,),
        in_specs=[pl.BlockSpec((tm, wq), lambda i: (i, 0)), pl.BlockSpec((tm, wk), lambda i: (i, 0)),
                  pl.BlockSpec((tm, HEAD_PAD), lambda i: (i, kr_seg)),
                  pl.BlockSpec((tm, LANES), lambda i: (i, 0)), pl.BlockSpec((tm, LANES), lambda i: (i, 0)),
                  pl.BlockSpec((1, HEAD_PAD), lambda i: (0, 0)), pl.BlockSpec((1, HEAD_PAD), lambda i: (0, 0))],
        out_specs=(pl.BlockSpec((tm, wq), lambda i: (i, 0)), pl.BlockSpec((tm, wq), lambda i: (i, 0))),
        out_shape=(jax.ShapeDtypeStruct((T, wq), BF16), jax.ShapeDtypeStruct((T, wq), BF16)),
        compiler_params=_params(("parallel",)),
    )(q_raw, k_nope, rest, cos_t, sin_t, q_norm, k_norm)


def _qk_bwd(dq, dk, q_raw, k_nope, rest, cos_t, sin_t, q_norm, k_norm, *, name, heads, kr_seg, tm=128):
    T = q_raw.shape[0]
    tm = _tile(T, tm, 16)

    def body(dq_ref, dk_ref, q_ref, kn_ref, kr_ref, c_ref, s_ref, gq_ref, gk_ref,
             dqr_ref, dkn_ref, dkr_ref, dgq_ref, dgk_ref):
        cos_b, sin_b = c_ref[...], s_ref[...]
        kr = kr_ref[:, 0:LANES]
        kr_ss = jnp.sum(kr * kr, axis=-1, keepdims=True)
        gq, gk = gq_ref[...], gk_ref[...]
        dgq = jnp.zeros((1, HEAD_PAD), F32)
        dgk_n = jnp.zeros((1, LANES), F32)
        dgk_r = jnp.zeros((1, LANES), F32)
        dkr = jnp.zeros((tm, LANES), F32)
        for h in range(heads):
            lo = h * HEAD_PAD
            q = q_ref[:, lo:lo + HEAD_PAD]
            rq = lax.rsqrt(jnp.sum(q * q, axis=-1, keepdims=True) / QK_HEAD + EPS)
            qhat = q * rq
            dqn = jnp.concatenate([dq_ref[:, lo:lo + LANES],
                                   _rope_t(dq_ref[:, lo + LANES:lo + HEAD_PAD], cos_b, sin_b)], axis=1) * ATTN_SCALE
            dgq = dgq + jnp.sum(dqn * qhat, axis=0, keepdims=True)
            dqh = dqn * gq
            dqr_ref[:, lo:lo + HEAD_PAD] = (
                rq * (dqh - qhat * (jnp.sum(dqh * qhat, axis=-1, keepdims=True) / QK_HEAD))).astype(BF16)
            kn = kn_ref[:, h * LANES:(h + 1) * LANES]
            rk = lax.rsqrt((jnp.sum(kn * kn, axis=-1, keepdims=True) + kr_ss) / QK_HEAD + EPS)
            khat_n, khat_r = kn * rk, kr * rk
            dkn_n = dk_ref[:, lo:lo + LANES] * (1.0 / LOG2_E)
            dkn_r = _rope_t(dk_ref[:, lo + LANES:lo + HEAD_PAD], cos_b, sin_b) * (1.0 / LOG2_E)
            dgk_n = dgk_n + jnp.sum(dkn_n * khat_n, axis=0, keepdims=True)
            dgk_r = dgk_r + jnp.sum(dkn_r * khat_r, axis=0, keepdims=True)
            dkh_n, dkh_r = dkn_n * gk[:, :LANES], dkn_r * gk[:, LANES:]
            proj = (jnp.sum(dkh_n * khat_n, axis=-1, keepdims=True)
                    + jnp.sum(dkh_r * khat_r, axis=-1, keepdims=True)) / QK_HEAD
            dkn_ref[:, h * LANES:(h + 1) * LANES] = (rk * (dkh_n - khat_n * proj)).astype(BF16)
            dkr = dkr + rk * (dkh_r - khat_r * proj)
        dkr_ref[:, 0:LANES] = dkr.astype(BF16)
        dkr_ref[:, LANES:HEAD_PAD] = jnp.zeros((tm, HEAD_PAD - LANES), BF16)
        dgk = jnp.concatenate([dgk_n, dgk_r], axis=1)

        @pl.when(pl.program_id(0) == 0)
        def _():
            dgq_ref[...] = dgq
            dgk_ref[...] = dgk

        @pl.when(pl.program_id(0) > 0)
        def _():
            dgq_ref[...] += dgq
            dgk_ref[...] += dgk

    wq, wk = heads * HEAD_PAD, heads * LANES
    row = lambda w: pl.BlockSpec((tm, w), lambda i: (i, 0))
    vec = pl.BlockSpec((1, HEAD_PAD), lambda i: (0, 0))
    return pl.pallas_call(
        body, name=name, grid=(T // tm,),
        in_specs=[row(wq), row(wq), row(wq), row(wk), pl.BlockSpec((tm, HEAD_PAD), lambda i: (i, kr_seg)),
                  row(LANES), row(LANES), vec, vec],
        out_specs=(row(wq), row(wk), row(HEAD_PAD), vec, vec),
        out_shape=(jax.ShapeDtypeStruct((T, wq), BF16), jax.ShapeDtypeStruct((T, wk), BF16),
                   jax.ShapeDtypeStruct((T, HEAD_PAD), BF16),
                   jax.ShapeDtypeStruct((1, HEAD_PAD), F32), jax.ShapeDtypeStruct((1, HEAD_PAD), F32)),
        compiler_params=_params(("arbitrary",)),
    )(dq, dk, q_raw, k_nope, rest, cos_t, sin_t, q_norm, k_norm)


def _causal_mask(s):
    row = lax.broadcasted_iota(jnp.int32, s.shape, 0)
    col = lax.broadcasted_iota(jnp.int32, s.shape, 1)
    return jnp.where(row >= col, s, NEG)


def _flash_fwd(q, k, v, *, name, heads, tq=384, hp=2):
    T = q.shape[0]
    tq = _tile(T, tq, LANES)
    nq = T // tq
    nt = (((1,), (1,)), ((), ()))

    def body(q_ref, k_ref, v_ref, o_ref, lse_ref):
        def q_block(i, carry):
            q_at = pl.ds(pl.multiple_of(i * tq, tq), tq)
            qbs = [q_ref[q_at, h * HEAD_PAD:(h + 1) * HEAD_PAD] for h in range(hp)]

            def step(j, state, masked):
                k_at = pl.ds(pl.multiple_of(j * tq, tq), tq)
                new = []
                scores = [lax.dot_general(qbs[h], k_ref[k_at, h * HEAD_PAD:(h + 1) * HEAD_PAD], nt,
                                          preferred_element_type=F32) for h in range(hp)]
                for h in range(hp):
                    m, l, acc = state[h]
                    s = scores[h]
                    if masked:
                        s = _causal_mask(s)
                    m_new = jnp.maximum(m, jnp.max(s, axis=-1, keepdims=True))
                    p = jnp.exp2(s - m_new)
                    alpha = jnp.exp2(m - m_new)
                    l = alpha * l + jnp.sum(p, axis=-1, keepdims=True)
                    acc = alpha * acc + jnp.dot(p.astype(BF16), v_ref[k_at, h * V_HEAD:(h + 1) * V_HEAD],
                                                preferred_element_type=F32)
                    new.append((m_new, l, acc))
                return tuple(new)

            init = tuple((jnp.full((tq, 1), NEG, F32), jnp.zeros((tq, 1), F32), jnp.zeros((tq, V_HEAD), F32))
                         for _ in range(hp))
            state = lax.fori_loop(0, i, lambda j, st: step(j, st, False), init)
            state = step(i, state, True)
            for h in range(hp):
                m, l, acc = state[h]
                o_ref[q_at, h * V_HEAD:(h + 1) * V_HEAD] = (acc / l).astype(BF16)
                lse_ref[h, q_at, :] = jnp.broadcast_to(m + jnp.log2(l), (tq, LANES))
            return carry

        lax.fori_loop(0, nq, q_block, 0)

    qk_spec = pl.BlockSpec((T, hp * HEAD_PAD), lambda g: (0, g))
    v_spec = pl.BlockSpec((T, hp * V_HEAD), lambda g: (0, g))
    return pl.pallas_call(
        body, name=name, grid=(heads // hp,), in_specs=[qk_spec, qk_spec, v_spec],
        out_specs=(v_spec, pl.BlockSpec((hp, T, LANES), lambda g: (g, 0, 0))),
        out_shape=(jax.ShapeDtypeStruct((T, heads * V_HEAD), BF16), jax.ShapeDtypeStruct((heads, T, LANES), F32)),
        compiler_params=_params(("parallel",)),
    )(q, k, v)


def _flash_bwd(q, k, v, o, do, lse, *, name, heads, tq=384):
    T = q.shape[0]
    tq = _tile(T, tq, LANES)
    nq = T // tq
    nt = (((1,), (1,)), ((), ()))
    tn = (((0,), (0,)), ((), ()))

    def body(q_ref, k_ref, v_ref, o_ref, do_ref, lse_ref, dq_ref, dk_ref, dv_ref, delta_ref):
        def fill_delta(i, carry):
            at = pl.ds(pl.multiple_of(i * tq, tq), tq)
            d = jnp.sum(o_ref[at, :].astype(F32) * do_ref[at, :].astype(F32), axis=-1, keepdims=True)
            delta_ref[at, :] = jnp.broadcast_to(d, (tq, LANES))
            dq_ref[at, :] = jnp.zeros((tq, HEAD_PAD), F32)
            return carry

        lax.fori_loop(0, nq, fill_delta, 0)

        def kv_block(j, carry):
            k_at = pl.ds(pl.multiple_of(j * tq, tq), tq)
            kb, vb = k_ref[k_at, :], v_ref[k_at, :]

            def steps(blocks, state, masked):
                dk_acc, dv_acc = state
                at = [pl.ds(pl.multiple_of(i * tq, tq), tq) for i in blocks]
                qbs = [q_ref[a, :] for a in at]
                dobs = [do_ref[a, :] for a in at]
                scores = [lax.dot_general(qb, kb, nt, preferred_element_type=F32) for qb in qbs]
                dps = [lax.dot_general(dob, vb, nt, preferred_element_type=F32) for dob in dobs]
                for a, qb, dob, sc, dp in zip(at, qbs, dobs, scores, dps):
                    if masked:
                        sc = _causal_mask(sc)
                    p = jnp.exp2(sc - lse_ref[0, a, :][:, 0:1])
                    ds = (p * (dp - delta_ref[a, :][:, 0:1])).astype(BF16)
                    dv_acc = dv_acc + lax.dot_general(p.astype(BF16), dob, tn, preferred_element_type=F32)
                    dk_acc = dk_acc + lax.dot_general(ds, qb, tn, preferred_element_type=F32)
                    dq_ref[a, :] += jnp.dot(ds, kb, preferred_element_type=F32)
                return dk_acc, dv_acc

            state = steps([j], (jnp.zeros((tq, HEAD_PAD), F32), jnp.zeros((tq, V_HEAD), F32)), True)
            rest = nq - 1 - j
            state = lax.fori_loop(0, rest // 2, lambda t, st: steps([j + 1 + 2 * t, j + 2 + 2 * t], st, False), state)
            dk_acc, dv_acc = lax.cond(rest % 2 == 1, lambda st: steps([nq - 1], st, False), lambda st: st, state)
            dk_ref[k_at, :] = dk_acc
            dv_ref[k_at, :] = dv_acc.astype(BF16)
            return carry

        lax.fori_loop(0, nq, kv_block, 0)

    qk_spec = pl.BlockSpec((T, HEAD_PAD), lambda h: (0, h))
    v_spec = pl.BlockSpec((T, V_HEAD), lambda h: (0, h))
    return pl.pallas_call(
        body, name=name, grid=(heads,),
        in_specs=[qk_spec, qk_spec, v_spec, v_spec, v_spec, pl.BlockSpec((1, T, LANES), lambda h: (h, 0, 0))],
        out_specs=(qk_spec, qk_spec, v_spec),
        out_shape=(jax.ShapeDtypeStruct((T, heads * HEAD_PAD), F32), jax.ShapeDtypeStruct((T, heads * HEAD_PAD), F32),
                   jax.ShapeDtypeStruct((T, heads * V_HEAD), BF16)),
        scratch_shapes=[pltpu.VMEM((T, LANES), F32)],
        compiler_params=_params(("parallel",)),
    )(q, k, v, o, do, lse)


def _merge_fwd(gl, pa, pb, pc, *, name, d, tm=384, tn=1024):
    T = pa.shape[0]
    tm, tn = _tile(T, tm, 16), _tile(d, tn)
    nb = d // tn

    def body(g0, g1, g2, a, b, c, o_ref):
        o_ref[...] = (jax.nn.sigmoid(g0[...]) * a[...] + jax.nn.sigmoid(g1[...]) * b[...]
                      + jax.nn.sigmoid(g2[...]) * c[...]).astype(BF16)

    gate = lambda n: pl.BlockSpec((tm, tn), lambda i, j: (i, n * nb + j))
    blk = pl.BlockSpec((tm, tn), lambda i, j: (i, j))
    return pl.pallas_call(
        body, name=name, grid=(T // tm, nb), in_specs=[gate(0), gate(1), gate(2), blk, blk, blk],
        out_specs=blk, out_shape=jax.ShapeDtypeStruct((T, d), BF16),
        compiler_params=_params(("parallel", "parallel")),
    )(gl, gl, gl, pa, pb, pc)


def _merge_bwd(dm, gl, pa, pb, pc, *, name, d, tm=384, tn=1024):
    T = pa.shape[0]
    tm, tn = _tile(T, tm, 16), _tile(d, tn)
    nb = d // tn

    def body(dm_ref, g0, g1, g2, a, b, c, da, db, dc, dg0, dg1, dg2):
        dmv = dm_ref[...]
        for g_ref, p_ref, dp_ref, dg_ref in ((g0, a, da, dg0), (g1, b, db, dg1), (g2, c, dc, dg2)):
            sg = jax.nn.sigmoid(g_ref[...])
            dp_ref[...] = (dmv * sg).astype(BF16)
            dg_ref[...] = (dmv * p_ref[...] * sg * (1.0 - sg)).astype(BF16)

    gate = lambda n: pl.BlockSpec((tm, tn), lambda i, j: (i, n * nb + j))
    blk = pl.BlockSpec((tm, tn), lambda i, j: (i, j))
    return pl.pallas_call(
        body, name=name, grid=(T // tm, nb), in_specs=[blk, gate(0), gate(1), gate(2), blk, blk, blk],
        out_specs=(blk,) * 6, out_shape=(jax.ShapeDtypeStruct((T, d), BF16),) * 6,
        compiler_params=_params(("parallel", "parallel")),
    )(dm, gl, gl, gl, pa, pb, pc)


def _loss(y, target, *, name, first, last, tm=384):
    T, d = y.shape
    tm = _tile(T, tm, 16)

    def body(y_ref, t_ref, loss_ref, dy_ref, dyb_ref):
        i = pl.program_id(0)
        row = lax.broadcasted_iota(jnp.int32, (tm, 1), 0) + i * tm
        real = jnp.logical_and(row >= first, row < last)
        err = jnp.where(real, y_ref[...] - t_ref[...], 0.0)
        dy_ref[...] = err * (1.0 / d)
        dyb_ref[...] = (err * (1.0 / d)).astype(BF16)
        part = jnp.broadcast_to(jnp.sum(err * err, keepdims=True).reshape(1, 1), (1, LANES))

        @pl.when(i == 0)
        def _():
            loss_ref[...] = part

        @pl.when(i > 0)
        def _():
            loss_ref[...] += part

    blk = pl.BlockSpec((tm, d), lambda i: (i, 0))
    return pl.pallas_call(
        body, name=name, grid=(T // tm,), in_specs=[blk, blk],
        out_specs=(pl.BlockSpec((1, LANES), lambda i: (0, 0)), blk, blk),
        out_shape=(jax.ShapeDtypeStruct((1, LANES), F32), jax.ShapeDtypeStruct((T, d), F32),
                   jax.ShapeDtypeStruct((T, d), BF16)),
        compiler_params=_params(("arbitrary",)),
    )(y, target)


def _as3d(a):
    return a.reshape(a.shape[0], -1, a.shape[-1])


def _sum_stack(parts, *, name, out_dtype, rows=256):
    n, R, C = parts.shape
    tr = _tile(R, rows, 16)

    def body(p_ref, o_ref):
        acc = p_ref[0].astype(F32)
        for s in range(1, n):
            acc = acc + p_ref[s].astype(F32)
        o_ref[...] = acc.astype(out_dtype)

    return pl.pallas_call(
        body, name=name, grid=(R // tr,),
        in_specs=[pl.BlockSpec((n, tr, C), lambda i: (0, i, 0))],
        out_specs=pl.BlockSpec((tr, C), lambda i: (i, 0)),
        out_shape=jax.ShapeDtypeStruct((R, C), out_dtype),
        compiler_params=_params(("parallel",)),
    )(parts)


def _adamw(w, g, m, v, *, name, rows=128):
    R, C = w.shape
    tr = _tile(R, rows, 8)
    c1 = 1.0 - ADAM_B1 ** ADAM_STEP
    c2 = 1.0 - ADAM_B2 ** ADAM_STEP

    def body(w_ref, g_ref, m_ref, v_ref, d_ref, nm_ref, nv_ref):
        gv = g_ref[...]
        nm = ADAM_B1 * m_ref[...] + (1.0 - ADAM_B1) * gv
        nv = ADAM_B2 * v_ref[...] + (1.0 - ADAM_B2) * (gv * gv)
        nm_ref[...] = nm
        nv_ref[...] = nv
        d_ref[...] = -ADAM_LR * ((nm / c1) / (jnp.sqrt(nv / c2) + ADAM_EPS) + ADAM_WD * w_ref[...])

    blk = pl.BlockSpec((tr, C), lambda i: (i, 0))
    return pl.pallas_call(
        body, name=name, grid=(R // tr,), in_specs=[blk] * 4, out_specs=(blk,) * 3,
        out_shape=(jax.ShapeDtypeStruct((R, C), F32),) * 3,
        compiler_params=_params(("parallel",)),
    )(w, g, m, v)


def _one_hot(index, n):
    return jnp.broadcast_to((jnp.arange(n) == index).astype(F32)[:, None, None], (n, 8, LANES))


def _is_set(flags_ref, s):
    return flags_ref[s, 0:1, 0:1] > 0.5


def _rows_for(h, width, itemsize, n_stacked, budget, mult):
    return _tile(h, max(mult, budget // (n_stacked * width * itemsize)), mult)


def _pair_sum(pieces, recv, core, *, name):
    _, H, C = recv.shape
    tr = _rows_for(H, C, 2, 1, 2 << 20, 16)
    nh = H // tr

    def body(lo_ref, hi_ref, r_ref, core_ref, o_ref):
        mine = jnp.where(_is_set(core_ref, 0), lo_ref[0], hi_ref[0])
        o_ref[0] = (mine.astype(F32) + r_ref[0].astype(F32)).astype(BF16)

    blk = pl.BlockSpec((1, tr, C), lambda j, i: (j, i, 0))
    return pl.pallas_call(
        body, name=name, grid=(4, nh),
        in_specs=[blk, pl.BlockSpec((1, tr, C), lambda j, i: (j, nh + i, 0)), blk,
                  pl.BlockSpec((2, 8, LANES), lambda j, i: (0, 0, 0))],
        out_specs=blk, out_shape=jax.ShapeDtypeStruct((4, H, C), BF16),
        compiler_params=_params(("parallel", "parallel")),
    )(pieces, pieces, recv, core)


def _chip_sum(pair, landed, chip_flags, *, name):
    _, H, C = pair.shape
    tr = _rows_for(H, C, 2, 4, 8 << 20, 16)

    def body(p_ref, l_ref, chip_ref, o_ref):
        acc = None
        for s in range(4):
            part = jnp.where(_is_set(chip_ref, s), p_ref[s], l_ref[s]).astype(F32)
            acc = part if acc is None else acc + part
        o_ref[...] = acc

    blk = pl.BlockSpec((4, tr, C), lambda i: (0, i, 0))
    return pl.pallas_call(
        body, name=name, grid=(H // tr,),
        in_specs=[blk, blk, pl.BlockSpec((4, 8, LANES), lambda i: (0, 0, 0))],
        out_specs=pl.BlockSpec((tr, C), lambda i: (i, 0)), out_shape=jax.ShapeDtypeStruct((H, C), F32),
        compiler_params=_params(("parallel",)),
    )(pair, landed, chip_flags)


def _adamw_layer(w, m, v, total, recv, core, layer, prev, *, name, col_halves=False):
    _, R, C = w.shape
    H, wd = total.shape
    tr = _rows_for(H, wd, 4, 1, 2 << 20, 8)
    nh = H // tr
    c1 = 1.0 - ADAM_B1 ** ADAM_STEP
    c2 = 1.0 - ADAM_B2 ** ADAM_STEP
    n_prev = 0 if prev is None else 4

    def body(*refs):
        w_ref, m_ref, v_ref, t_ref, r_ref, core_ref = refs[:6]
        g_ref, d_ref, nm_ref, nv_ref = refs[6 + n_prev:]
        half_is_mine = jnp.where(pl.program_id(0) == 0, core_ref[0, 0:1, 0:1], core_ref[1, 0:1, 0:1]) > 0.5
        gv = jnp.where(half_is_mine, t_ref[...], r_ref[...])
        nm = ADAM_B1 * m_ref[0] + (1.0 - ADAM_B1) * gv
        nv = ADAM_B2 * v_ref[0] + (1.0 - ADAM_B2) * (gv * gv)
        g_ref[0] = gv
        nm_ref[0] = nm
        nv_ref[0] = nv
        d_ref[0] = -ADAM_LR * ((nm / c1) / (jnp.sqrt(nv / c2) + ADAM_EPS) + ADAM_WD * w_ref[0])

    if col_halves:
        lay = pl.BlockSpec((1, tr, wd), lambda hf, i: (layer, i, hf))
    else:
        lay = pl.BlockSpec((1, tr, wd), lambda hf, i: (layer, hf * nh + i, 0))
    one = pl.BlockSpec((tr, wd), lambda hf, i: (i, 0))
    operands = [w, m, v, total, recv, core] + ([] if prev is None else list(prev))
    return pl.pallas_call(
        body, name=name, grid=(2, nh),
        in_specs=[lay, lay, lay, one, one, pl.BlockSpec((2, 8, LANES), lambda hf, i: (0, 0, 0))] + [ANY] * n_prev,
        out_specs=(lay,) * 4, out_shape=(jax.ShapeDtypeStruct((2, R, C), F32),) * 4,
        input_output_aliases={6 + i: i for i in range(n_prev)},
        compiler_params=_params(("parallel", "parallel")),
    )(*operands)


ANY = pl.BlockSpec(memory_space=pl.ANY)


def _coords():
    return lax.axis_index("x"), lax.axis_index("y"), lax.axis_index("c")


HBM = pl.BlockSpec(memory_space=pltpu.HBM)
SEM = pl.BlockSpec(memory_space=pltpu.SEMAPHORE)
EFFECT = pltpu.SideEffectType.DATAFLOW_SIDE_EFFECTING


def _copies(plan, bufs, send_sems, recv_sems):
    return [pltpu.make_async_remote_copy(src_ref=s, dst_ref=d, send_sem=send_sems.at[i], recv_sem=recv_sems.at[i],
                                         device_id=to, device_id_type=MESH)
            for i, (s, d, to) in enumerate(plan(bufs))]


def _start_copies(bufs, groups, *, name):
    nb, ng = len(bufs), len(groups)

    def body(*refs):
        buf_refs = refs[:nb]
        sems = refs[nb:nb + 2 * ng]
        token = refs[-1]
        for g, (plan, _) in enumerate(groups):
            for cp in _copies(plan, buf_refs, sems[2 * g], sems[2 * g + 1]):
                cp.start()
        token[...] = jnp.zeros_like(token)

    sem_shapes = []
    for _, n in groups:
        sem_shapes += [pltpu.SemaphoreType.DMA((n,)), pltpu.SemaphoreType.DMA((n,))]
    out = pl.pallas_call(
        body, name=name, in_specs=[HBM] * nb,
        out_specs=tuple([SEM] * (2 * ng) + [HBM] * nb + [pl.BlockSpec(memory_space=pltpu.VMEM)]),
        out_shape=tuple(sem_shapes + [pltpu.HBM(b.shape, b.dtype) for b in bufs] + [jax.ShapeDtypeStruct((8, LANES), F32)]),
        input_output_aliases={i: 2 * ng + i for i in range(nb)},
        compiler_params=pltpu.CompilerParams(has_side_effects=EFFECT),
    )(*[pltpu.with_memory_space_constraint(b, pltpu.HBM) for b in bufs])
    sems = [(out[2 * g], out[2 * g + 1]) for g in range(ng)]
    return sems, list(out[2 * ng:2 * ng + nb]), out[-1]


def _wait_copies(bufs, sems, plan, after, *, name):
    nb = len(bufs)

    def body(*refs):
        buf_refs = refs[:nb]
        for cp in _copies(plan, buf_refs, refs[nb], refs[nb + 1]):
            cp.wait_send()
            cp.wait_recv()

    out = pl.pallas_call(
        body, name=name, in_specs=[HBM] * nb + [SEM, SEM, ANY], out_specs=tuple([HBM] * nb),
        out_shape=tuple(pltpu.HBM(b.shape, b.dtype) for b in bufs),
        input_output_aliases={i: i for i in range(nb)},
        compiler_params=pltpu.CompilerParams(has_side_effects=EFFECT),
    )(*bufs, sems[0], sems[1], after)
    return list(out)


def _half(ref, c):
    h = ref.shape[0] // 2
    return ref.at[pl.ds(c * h, h)]


def _ici_gather_plan(pairs):
    def plan(refs):
        x, y, c = _coords()
        me = 2 * x + y
        out = []
        for s, d in pairs:
            for cx, cy in [(1 - x, y), (x, 1 - y), (1 - x, 1 - y)]:
                out.append((_half(refs[s], c), _half(refs[d].at[me], c), (cx, cy, c)))
            out.append((refs[s], refs[d].at[me], (x, y, 1 - c)))
        return out
    return plan, 4 * len(pairs)


def _d2d_forward_plan(lands):
    def plan(refs):
        x, y, c = _coords()
        out = []
        for d in lands:
            for cx, cy in [(1 - x, y), (x, 1 - y), (1 - x, 1 - y)]:
                got = _half(refs[d].at[2 * cx + cy], c)
                out.append((got, got, (x, y, 1 - c)))
        return out
    return plan, 3 * len(lands)


def _swap_half_plan(pairs):
    def plan(refs):
        x, y, c = _coords()
        out = []
        for s, d in pairs:
            h = refs[d].shape[1]
            out.append((refs[s].at[:, pl.ds((1 - c) * h, h)], refs[d], (x, y, 1 - c)))
        return out
    return plan, len(pairs)


def _scatter_plan(pairs):
    def plan(refs):
        x, y, c = _coords()
        me = 2 * x + y
        out = []
        for s, d in pairs:
            for cx, cy in [(1 - x, y), (x, 1 - y), (1 - x, 1 - y)]:
                out.append((refs[s].at[2 * cx + cy], refs[d].at[me], (cx, cy, c)))
        return out
    return plan, 3 * len(pairs)


def _swap_total_plan(pairs):
    def plan(refs):
        x, y, c = _coords()
        return [(refs[s], refs[d], (x, y, 1 - c)) for s, d in pairs]
    return plan, len(pairs)


def _gather_all(block, *, name):
    def body(src, out, send_sems, recv_sems, local_sem):
        x, y, c = _coords()
        me = 4 * x + 2 * y + c
        flips = [(fx, fy, fc) for fx in (0, 1) for fy in (0, 1) for fc in (0, 1)][1:]
        mine = pltpu.make_async_copy(src, out.at[me], local_sem)
        mine.start()
        peers = [(x ^ fx, y ^ fy, c ^ fc) for fx, fy, fc in flips]
        cps = [pltpu.make_async_remote_copy(src_ref=src, dst_ref=out.at[me], send_sem=send_sems.at[k],
                                            recv_sem=recv_sems.at[k], device_id=peer, device_id_type=MESH)
               for k, peer in enumerate(peers)]
        for cp in cps:
            cp.start()
        for k, (px, py, pc) in enumerate(peers):
            slot = out.at[4 * px + 2 * py + pc]
            pltpu.make_async_remote_copy(src_ref=slot, dst_ref=slot, send_sem=send_sems.at[k], recv_sem=recv_sems.at[k],
                                         device_id=(px, py, pc), device_id_type=MESH).wait_recv()
        for cp in cps:
            cp.wait_send()
        mine.wait()

    return pl.pallas_call(
        body, name=name, in_specs=[ANY], out_specs=ANY,
        out_shape=jax.ShapeDtypeStruct((8,) + block.shape, block.dtype),
        scratch_shapes=[pltpu.SemaphoreType.DMA((7,)), pltpu.SemaphoreType.DMA((7,)), pltpu.SemaphoreType.DMA],
    )(block)


def _cols(o):
    return jnp.transpose(o, (1, 0, 2)).reshape(o.shape[1], -1)


def _uncols(full):
    return jnp.transpose(full.reshape(full.shape[0], 4, -1), (1, 0, 2))


def _rope_pad(x1, x2):
    z = jnp.zeros_like(x1)
    return jnp.concatenate([x1, z, x2, z], axis=-1)


def _head_pad(w, heads):
    r = w.reshape(w.shape[0], heads, QK_HEAD)
    half = QK_ROPE // 2
    out = jnp.concatenate([r[..., :QK_NOPE], _rope_pad(r[..., QK_NOPE:QK_NOPE + half], r[..., QK_NOPE + half:])], axis=-1)
    return out.reshape(w.shape[0], heads * HEAD_PAD)


def _head_unpad(w, heads):
    r = w.reshape(w.shape[0], heads, HEAD_PAD)
    half = QK_ROPE // 2
    out = jnp.concatenate([r[..., :QK_NOPE], r[..., QK_NOPE:QK_NOPE + half],
                           r[..., QK_NOPE + 2 * half:QK_NOPE + 3 * half]], axis=-1)
    return out.reshape(w.shape[0], heads * QK_HEAD)


class _Dims:
    def __init__(self, d, seq):
        self.d = d
        self.seq = seq
        self.t_real = N_META + seq
        self.t = -(-self.t_real // LANES) * LANES
        self.dc = d // 2
        self.dp = d // 2
        self.pg = self.dp // len(POOL_WINDOWS)
        self.heads = d // 128
        self.dff = 4 * d
        self.a_end = 3 * self.dc
        self.q_end = self.a_end + Q_LORA
        self.kv_end = self.q_end + KV_LORA
        self.kr_end = self.kv_end + QK_ROPE
        self.pool_end = self.kr_end + self.dp
        self.d_in = self.pool_end + 3 * d
        self.r_pool = 3 * self.dc
        self.r_q = self.r_pool + self.dp
        self.r_kv = self.r_q + Q_LORA
        self.r_kr = self.r_kv + KV_LORA
        self.r_width = self.r_kr + HEAD_PAD


def _split_cols(a):
    return jnp.moveaxis(a.reshape(a.shape[:-1] + (2, a.shape[-1] // 2)), -2, -3)


def _join_cols(a):
    a = jnp.moveaxis(a, -3, -2)
    return a.reshape(a.shape[:-2] + (a.shape[-2] * a.shape[-1],))


def _in_weights(dm, pieces):
    w_t = _join_cols(pieces).reshape(dm.d_in, dm.d)
    half = QK_ROPE // 2
    kr = w_t[dm.kv_end:dm.kr_end]
    zeros = jnp.zeros((half, dm.d), BF16)
    kr_p = jnp.concatenate([kr[:half], zeros, kr[half:], zeros, jnp.zeros((HEAD_PAD - LANES, dm.d), BF16)], axis=0)
    return dict(
        wg_t=w_t[dm.pool_end:],
        wr_t=jnp.concatenate([w_t[:dm.a_end], w_t[dm.kr_end:dm.pool_end], w_t[dm.a_end:dm.kv_end], kr_p], axis=0))


def _other_weights(dm, g):
    out = {}
    if "w_ukv" in g:
        w_ukv = _cols(g["w_ukv"]).reshape(KV_LORA, dm.heads, QK_NOPE + V_HEAD)
        out["wkn"] = w_ukv[:, :, :QK_NOPE].reshape(KV_LORA, dm.heads * QK_NOPE)
        out["wv"] = w_ukv[:, :, QK_NOPE:].reshape(KV_LORA, dm.heads * V_HEAD)
    if "w_uq" in g:
        out["wuq"] = _head_pad(_cols(g["w_uq"]), dm.heads)
    if "pool_w" in g:
        out["wp"] = jnp.transpose(g["pool_w"], (1, 0, 2, 3)).reshape(len(POOL_WINDOWS), dm.pg, dm.pg)
    for name, key in (("w_branch_a", "wba"), ("w_branch_c", "wbc"), ("w_up", "wup")):
        if name in g:
            out[key] = _cols(g[name])
    for name, key in (("w_branch_b", "wbb"), ("w_o", "wo"), ("w_down", "wdn")):
        if name in g:
            out[key] = g[name].reshape(-1, dm.d)
    return out


def _small_weights(small):
    return dict(
        conv_w=small["conv_w"],
        attn_norm=small["attn_norm"][None], mlp_norm=small["mlp_norm"][None],
        q_lat_norm=small["q_lat_norm"][None], kv_lat_norm=small["kv_lat_norm"][None],
        q_norm=_head_pad(small["q_norm"][None], 1), k_norm=_head_pad(small["k_norm"][None], 1),
        pool_scale=small["pool_scale"][None],
    )


def _grad_piece(dm, dw, name):
    half = QK_ROPE // 2
    rows = lambda a: a.reshape((4, a.shape[0] // 4) + a.shape[1:])
    if name == "w_in":
        dwr, dwg = dw["wr_t"], dw["wg_t"]
        d_t = jnp.concatenate([
            dwr[:dm.r_pool], dwr[dm.r_q:dm.r_kr], dwr[dm.r_kr:dm.r_kr + half],
            dwr[dm.r_kr + 2 * half:dm.r_kr + 3 * half], dwr[dm.r_pool:dm.r_q], dwg], axis=0)
        out = _split_cols(rows(d_t))
    elif name == "w_ukv":
        out = _uncols(jnp.concatenate([dw["wkn"].reshape(KV_LORA, dm.heads, QK_NOPE),
                                       dw["wv"].reshape(KV_LORA, dm.heads, V_HEAD)], axis=-1).reshape(KV_LORA, -1))
    elif name == "w_uq":
        out = _uncols(_head_unpad(dw["wuq"], dm.heads))
    elif name == "pool_w":
        out = jnp.transpose(dw["wp"].reshape(len(POOL_WINDOWS), 4, dm.pg // 4, dm.pg), (1, 0, 2, 3))
    elif name in ("w_branch_a", "w_branch_c", "w_up"):
        out = _uncols(dw[{"w_branch_a": "wba", "w_branch_c": "wbc", "w_up": "wup"}[name]])
    else:
        out = rows(dw[{"w_branch_b": "wbb", "w_o": "wo", "w_down": "wdn"}[name]])
    return out.astype(BF16)


def _layer_fwd(dm, W, x, cos_t, sin_t, tag, more=None, h=None):
    n = lambda s: f"{s}_{tag}"
    if h is None:
        h = _rms_fwd(x, W["attn_norm"], name=n("attn_norm"))
    gl = _mm(h, W["wg_t"], name=n("proj_gates"), tb=True)
    rest = _mm(h, W["wr_t"], name=n("proj_rest"), tb=True)
    if more is not None:
        W.update(more("after_proj", rest))
    y_a = _conv_fwd(rest, W["conv_w"], name=n("conv"), dc=dm.dc)
    y_c = _pool_fwd(rest, W["wp"], W["pool_scale"], name=n("pool"), seg0=dm.r_pool // dm.pg, pg=dm.pg)
    q_lat = _rms_fwd(rest, W["q_lat_norm"], name=n("q_lat_norm"), width=Q_LORA, seg=dm.r_q // Q_LORA)
    kv_lat = _rms_fwd(rest, W["kv_lat_norm"], name=n("kv_lat_norm"), width=KV_LORA, seg=dm.r_kv // KV_LORA)
    q_raw = _mm(q_lat, W["wuq"], name=n("up_q"))
    k_nope = _mm(kv_lat, W["wkn"], name=n("up_k"))
    v = _mm(kv_lat, W["wv"], name=n("up_v"), out_dtype=BF16)
    q, k = _qk_fwd(q_raw, k_nope, rest, cos_t, sin_t, W["q_norm"], W["k_norm"], name=n("qk_norm_rope"),
                   heads=dm.heads, kr_seg=dm.r_kr // HEAD_PAD)
    if more is not None:
        W.update(more("after_qk", q))
    y_b, lse = _flash_fwd(q, k, v, name=n("attention"), heads=dm.heads)
    pa = _mm(y_a, W["wba"], name=n("branch_a"))
    pb = _mm(y_b, W["wbb"], name=n("branch_b"))
    pc = _mm(y_c, W["wbc"], name=n("branch_c"))
    merged = _merge_fwd(gl, pa, pb, pc, name=n("merge"), d=dm.d)
    x1 = _mm(merged, W["wo"], name=n("out_proj"), add=x)
    h2 = _rms_fwd(x1, W["mlp_norm"], name=n("mlp_norm"))
    up, act = _mm(h2, W["wup"], name=n("mlp_up"), epi="relu2")
    x2 = _mm(act, W["wdn"], name=n("mlp_down"), add=x1, tm=704, tk=4096)
    saved = dict(x=x, h=h, gl=gl, rest=rest, y_a=y_a, y_c=y_c, q_lat=q_lat, kv_lat=kv_lat, q_raw=q_raw, k_nope=k_nope,
                 v=v, q=q, k=k, y_b=y_b, lse=lse, pa=pa, pb=pb, pc=pc, merged=merged, x1=x1, h2=h2, up=up, act=act)
    return x2, saved


def _layer_bwd(dm, W, S, dx2, dx2_b, cos_t, sin_t, tag, hook=None):
    n = lambda s: f"{s}_{tag}"
    dw, ds = {}, {}
    if hook is None:
        hook = lambda point, t, dw_so_far: ()
    dup = _mm(dx2_b, W["wdn"], name=n("d_mlp_down"), tb=True, aux=S["up"], epi="drelu2", out_dtype=BF16,
              after=hook("start", dx2, dw))
    dw["wdn"] = _mm(S["act"], dx2_b, name=n("dw_mlp_down"), ta=True, tm=512)
    dh2 = _mm(dup, W["wup"], name=n("d_mlp_up"), tb=True, tm=704, tk=4096)
    dw["wup"] = _mm(S["h2"], dup, name=n("dw_mlp_up"), ta=True, tm=512)
    dx1, dx1_b, ds["mlp_norm"] = _rms_bwd(dh2, S["x1"], W["mlp_norm"], name=n("d_mlp_norm"), res=dx2, bf16_copy=True)
    dmerged = _mm(dx1_b, W["wo"], name=n("d_out_proj"), tb=True, after=hook("after_mlp", dx1, dw))
    dw["wo"] = _mm(S["merged"], dx1_b, name=n("dw_out_proj"), ta=True, tm=512)
    dpa, dpb, dpc, dg0, dg1, dg2 = _merge_bwd(dmerged, S["gl"], S["pa"], S["pb"], S["pc"], name=n("d_merge"), d=dm.d)
    dgl = jnp.concatenate([dg0, dg1, dg2], axis=1)
    dy_a = _mm(dpa, W["wba"], name=n("d_branch_a"), tb=True)
    dw["wba"] = _mm(S["y_a"], dpa, name=n("dw_branch_a"), ta=True, tm=512)
    dy_b = _mm(dpb, W["wbb"], name=n("d_branch_b"), tb=True, out_dtype=BF16)
    dw["wbb"] = _mm(S["y_b"], dpb, name=n("dw_branch_b"), ta=True, tm=512)
    dy_c = _mm(dpc, W["wbc"], name=n("d_branch_c"), tb=True)
    dw["wbc"] = _mm(S["y_c"], dpc, name=n("dw_branch_c"), ta=True, tm=512)
    dq, dk, dv = _flash_bwd(S["q"], S["k"], S["v"], S["y_b"], dy_b, S["lse"], name=n("d_attention"), heads=dm.heads)
    after_attention = hook("after_attention", dq, dw)
    dq_raw, dk_nope, dk_rope, dgq, dgk = _qk_bwd(
        dq, dk, S["q_raw"], S["k_nope"], S["rest"], cos_t, sin_t, W["q_norm"], W["k_norm"], name=n("d_qk_norm_rope"),
        heads=dm.heads, kr_seg=dm.r_kr // HEAD_PAD)
    ds["q_norm"] = _head_unpad(dgq, 1)
    ds["k_norm"] = _head_unpad(dgk, 1)
    dkv_v = _mm(dv, W["wv"], name=n("d_up_v"), tb=True, after=after_attention)
    dq_lat_n = _mm(dq_raw, W["wuq"], name=n("d_up_q"), tb=True, after=hook("after_qk", dq_raw, dw))
    dw["wuq"] = _mm(S["q_lat"], dq_raw, name=n("dw_up_q"), ta=True, tm=512)
    dkv_lat_n = _mm(dk_nope, W["wkn"], name=n("d_up_k"), tb=True, add=dkv_v)
    dw["wkn"] = _mm(S["kv_lat"], dk_nope, name=n("dw_up_k"), ta=True, tm=512)
    dw["wv"] = _mm(S["kv_lat"], dv, name=n("dw_up_v"), ta=True, tm=512)
    dq_lat, ds["q_lat_norm"] = _rms_bwd(dq_lat_n, S["rest"], W["q_lat_norm"], name=n("d_q_lat_norm"), width=Q_LORA,
                                        seg=dm.r_q // Q_LORA, out_dtype=BF16)
    dkv_lat, ds["kv_lat_norm"] = _rms_bwd(dkv_lat_n, S["rest"], W["kv_lat_norm"], name=n("d_kv_lat_norm"), width=KV_LORA,
                                          seg=dm.r_kv // KV_LORA, out_dtype=BF16)
    du, db, dc, ds["conv_w"] = _conv_bwd(S["rest"], W["conv_w"], dy_a, name=n("d_conv"), dc=dm.dc)
    dpool, dw["wp"], ds["pool_scale"] = _pool_bwd(S["rest"], W["wp"], W["pool_scale"], dy_c, name=n("d_pool"),
                                                  seg0=dm.r_pool // dm.pg, pg=dm.pg)
    drest = jnp.concatenate([du, db, dc, dpool, dq_lat, dkv_lat, dk_rope], axis=1)
    dw["wg_t"] = _mm(dgl, S["h"], name=n("dw_proj_gates"), ta=True, tm=512)
    dw["wr_t"] = _mm(drest, S["h"], name=n("dw_proj_rest"), ta=True, tm=512)
    dh_g = _mm(dgl, W["wg_t"], name=n("d_proj_gates"), tm=704, tk=3072, after=hook("after_dw_in", dw["wr_t"], dw))
    dh = _mm(drest, W["wr_t"], name=n("d_proj_rest"), add=dh_g, tm=704, tk=2688, after=hook("after_dh_gates", dh_g, dw))
    dx, dx_b, ds["attn_norm"] = _rms_bwd(dh, S["x"], W["attn_norm"], name=n("d_attn_norm"), res=dx1, bf16_copy=True)
    return dx, dx_b, dw, ds


BIG = ("w_in", "w_uq", "w_ukv", "pool_w", "w_branch_a", "w_branch_b", "w_branch_c", "w_o", "w_up", "w_down")
REPLICATED = ("attn_norm", "q_lat_norm", "kv_lat_norm", "q_norm", "k_norm", "pool_scale", "mlp_norm")
WEIGHTS = ("meta_tokens", "attn_norm", "w_in", "conv_w", "q_lat_norm", "kv_lat_norm", "w_uq", "w_ukv", "q_norm",
           "k_norm", "pool_w", "pool_scale", "w_branch_a", "w_branch_b", "w_branch_c", "w_o", "mlp_norm", "w_up",
           "w_down")


def _pack(arrays):
    flat = jnp.concatenate([a.reshape(-1).astype(F32) for a in arrays])
    pad = (-flat.shape[0]) % (8 * LANES)
    return jnp.pad(flat, (0, pad)).reshape(-1, LANES)


def _unpack(flat, shapes):
    out, pos = [], 0
    flat = flat.reshape(-1)
    for shp in shapes:
        size = math.prod(shp)
        out.append(flat[pos:pos + size].reshape(shp))
        pos += size
    return out


def _update(w, g, m, v, name):
    shp = w.shape
    to2 = lambda a: a.reshape(-1, shp[-1])
    delta, nm, nv = _adamw(to2(w), to2(g), to2(m), to2(v), name=name)
    return delta.reshape(shp), nm.reshape(shp), nv.reshape(shp)


def _step(args):
    x = args["x"][0]
    seq, d = x.shape
    dm = _Dims(d, seq)
    xi, yi, ci = _coords()
    chip = 2 * xi + yi

    small_w = _gather_all(_pack([args["conv_w"], args["meta_tokens"]]), name="gather_small_weights")
    args = dict(args)
    for p in ("", "m_", "v_"):
        args[p + "w_in"] = jnp.swapaxes(args[p + "w_in"], 1, 2)
    order = [(k, l) for l in range(2) for k in BIG]
    shards = {n: args[n[0]][n[1]].astype(BF16) for n in order}
    for l in range(2):
        shards[("w_in", l)] = _split_cols(shards[("w_in", l)])
    small_w, shards[order[0]] = lax.optimization_barrier((small_w, shards[order[0]]))
    lands = {n: lax.empty((4,) + shards[n].shape, BF16) for n in order}
    last = ("w_up", "w_down")
    group_names = [[("w_in", 0)], [(k, 0) for k in BIG[1:] if k not in last], [(k, 0) for k in last],
                   [(k, 1) for k in BIG]]
    first, others = order[0], order[1:]
    sems, thru, token = _start_copies([shards[first], lands[first]], [_ici_gather_plan([(0, 1)])],
                                      name="start_gather_ici_first")
    shards[first], lands[first] = thru
    at = {n: i for i, n in enumerate(others)}
    sems_b, thru, token_b = _start_copies(
        [shards[n] for n in others] + [lands[n] for n in others] + [token],
        [_ici_gather_plan([(at[n], len(others) + at[n]) for n in g]) for g in group_names[1:]], name="start_gather_ici")
    sems = sems + sems_b
    for i, n in enumerate(others):
        shards[n], lands[n] = thru[i], thru[len(others) + i]

    def finish_gather(g, after, tag):
        names = group_names[g]
        k = len(names)
        plan, _ = _ici_gather_plan([(i, k + i) for i in range(k)])
        got = _wait_copies([shards[n] for n in names] + [lands[n] for n in names], sems[g], plan, after,
                           name=f"wait_gather_ici_{tag}")
        for i, n in enumerate(names):
            shards[n] = got[i]
        fwd = _d2d_forward_plan(list(range(k)))
        sems2, bufs2, tok2 = _start_copies(got[k:], [fwd], name=f"start_gather_d2d_{tag}")
        return names, bufs2, sems2[0], fwd[0], tok2

    def land_gather(pending, after, tag):
        names, bufs2, sems2, plan, tok2 = pending
        done = _wait_copies(bufs2, sems2, plan, tok2 if after is None else after, name=f"wait_gather_d2d_{tag}")
        return {n[0]: buf for n, buf in zip(names, done)}

    conv_shape, meta_shape = args["conv_w"].shape, args["meta_tokens"].shape
    per_chip = [_unpack(small_w[2 * j], [conv_shape, meta_shape]) for j in range(4)]
    conv_full = jnp.concatenate([p[0] for p in per_chip], axis=-1)
    meta_full = jnp.concatenate([p[1] for p in per_chip], axis=-1)

    layers = []
    for l in range(2):
        small = {k: args[k][l] for k in REPLICATED}
        small["conv_w"] = conv_full[l]
        layers.append(_small_weights(small))

    pos = jnp.arange(dm.t, dtype=F32)
    inv = ROPE_THETA ** (-jnp.arange(0, QK_ROPE, 2, dtype=F32) / QK_ROPE)
    ang = pos[:, None] * inv[None, :]
    cos_t = _rope_pad(jnp.cos(ang), jnp.cos(ang))
    sin_t = _rope_pad(-jnp.sin(ang), jnp.sin(ang))
    tail = jnp.zeros((dm.t - dm.t_real, d), F32)
    h0 = jnp.concatenate([meta_full, x, tail], axis=0)
    target = jnp.concatenate([jnp.zeros((N_META, d), F32), args["loss_target"][0], tail], axis=0)

    h_first = _rms_fwd(h0, layers[0]["attn_norm"], name="attn_norm_l0", after=(token, token_b))
    layers[0].update(_in_weights(dm, land_gather(finish_gather(0, h_first, "l0_in"), None, "l0_in")["w_in"]))
    def rest_of_layer0(point, after):
        g, tag = (1, "l0_mid") if point == "after_proj" else (2, "l0_mlp")
        return _other_weights(dm, land_gather(finish_gather(g, after, tag), None, tag))

    h1, saved0 = _layer_fwd(dm, layers[0], h0, cos_t, sin_t, "l0", more=rest_of_layer0, h=h_first)
    g1 = land_gather(finish_gather(3, saved0["y_b"], "l1"), h1, "l1")
    layers[1].update(_in_weights(dm, g1["w_in"]))
    layers[1].update(_other_weights(dm, g1))
    h2, saved1 = _layer_fwd(dm, layers[1], h1, cos_t, sin_t, "l1")
    sq, dy, dy_b = _loss(h2, target, name="loss_head", first=N_META, last=dm.t_real)
    loss = lax.psum(0.5 / d * sq[0, 0], ("x", "y", "c"))
    core, chip_flags = _one_hot(ci, 2), _one_hot(chip, 4)

    class Reduce:
        def __init__(self, names, dw, tag):
            self.names, self.tag, self.nb = names, tag, len(names)
            self.idx = [(i, self.nb + i) for i in range(self.nb)]
            parts = [_as3d(_grad_piece(dm, dw, k)) for k in names]
            recv = [lax.empty((4, p.shape[1] // 2, p.shape[2]), BF16) for p in parts]
            self.plan = _swap_half_plan(self.idx)
            self.sems, self.bufs, self.token = _start_copies(parts + recv, [self.plan], name=f"start_swap_{tag}")

        def _land(self, after, what):
            return _wait_copies(self.bufs, self.sems[0], self.plan[0], self.token if after is None else after,
                                name=f"wait_{what}_{self.tag}")

        def scatter(self, after=None):
            got = self._land(after, "swap")
            pairs = [_pair_sum(got[i], got[j], core, name=f"pair_sum_{k}_{self.tag}")
                     for (i, j), k in zip(self.idx, self.names)]
            self.plan = _scatter_plan(self.idx)
            self.sems, self.bufs, self.token = _start_copies(pairs + [lax.empty(p.shape, BF16) for p in pairs],
                                                             [self.plan], name=f"start_scatter_{self.tag}")
            return self.token

        def totals(self, after=None):
            got = self._land(after, "scatter")
            sums = [_chip_sum(got[i], got[j], chip_flags, name=f"chip_sum_{k}_{self.tag}")
                    for (i, j), k in zip(self.idx, self.names)]
            self.plan = _swap_total_plan(self.idx)
            self.sems, self.bufs, self.token = _start_copies(sums + [lax.empty(t.shape, F32) for t in sums],
                                                             [self.plan], name=f"start_swap_total_{self.tag}")
            return self.token

        def finish(self, after=None):
            got = self._land(after, "swap_total")
            return {k: (got[i], got[j]) for (i, j), k in zip(self.idx, self.names)}

    dh1, dh1_b, dw1, ds1 = _layer_bwd(dm, layers[1], saved1, dy, dy_b, cos_t, sin_t, "l1",
                               hook=lambda point, t, dw: (loss.reshape(1, 1),) if point == "start" else ())
    early = ("w_down", "w_up", "w_o", "w_branch_a", "w_branch_b", "w_branch_c")
    late = tuple(k for k in BIG if k not in early)
    stage = {}

    def during_layer0(point, t, dw):
        if point == "start":
            stage["l1"] = Reduce(BIG, dw1, "l1")
            return (stage["l1"].token,)
        if point == "after_mlp":
            return (stage["l1"].scatter(after=t),)
        if point == "after_attention":
            tok = stage["l1"].totals(after=t)
            stage["l0a"] = Reduce(early, dw, "l0a")
            return (tok, stage["l0a"].token)
        if point == "after_qk":
            stage["red1"] = stage["l1"].finish(after=t)
            return (stage["l0a"].scatter(after=t),)
        if point == "after_dw_in":
            tok = stage["l0a"].totals(after=t)
            stage["l0b"] = Reduce(late, dw, "l0b")
            return (tok, stage["l0b"].token)
        return (stage["l0b"].scatter(after=t),)

    dh0, _, dw0, ds0 = _layer_bwd(dm, layers[0], saved0, dh1, dh1_b, cos_t, sin_t, "l0", hook=during_layer0)
    grad_x = dh0[N_META:dm.t_real][None]
    stage["l0b"].totals(after=dh0)
    red1 = stage["red1"]
    red0 = {**stage["l0a"].finish(), **stage["l0b"].finish()}
    grads = {}

    small_names = REPLICATED + ("conv_w",)
    small_parts = [jnp.stack([ds0[k].reshape(ds0[k].shape[-2:] if k == "conv_w" else (-1,)),
                              ds1[k].reshape(ds1[k].shape[-2:] if k == "conv_w" else (-1,))]) for k in small_names]
    small_parts.append(dh0[:N_META])
    small_all = _gather_all(_pack(small_parts), name="gather_small_grads")
    small_sum = _sum_stack(small_all, name="sum_small_grads", out_dtype=F32)
    small_g = dict(zip(small_names + ("meta_tokens",), _unpack(small_sum, [p.shape for p in small_parts])))
    for k in REPLICATED:
        grads[k] = small_g[k]
    dcw = conv_shape[-1]
    grads["conv_w"] = lax.dynamic_slice_in_dim(small_g["conv_w"], chip * dcw, dcw, axis=2)
    dmeta = meta_shape[-1]
    grads["meta_tokens"] = lax.dynamic_slice_in_dim(small_g["meta_tokens"], chip * dmeta, dmeta, axis=1)

    delta, new_m, new_v = {}, {}, {}
    for k in WEIGHTS:
        shp = args[k].shape
        if k in BIG:
            wmv = [args[p + k].reshape(2, -1, shp[-1]) for p in ("", "m_", "v_")]
            by_cols = k == "w_in"
            out = _adamw_layer(*wmv, *red1[k], core, 1, None, name=f"adamw_{k}_l1", col_halves=by_cols)
            out = _adamw_layer(*wmv, *red0[k], core, 0, out, name=f"adamw_{k}_l0", col_halves=by_cols)
            out = [o.reshape(shp) for o in out]
            grads[k], delta[k], new_m[k], new_v[k] = [jnp.swapaxes(o, 1, 2) for o in out] if by_cols else out
        else:
            grads[k] = grads[k].reshape(shp)
            delta[k], new_m[k], new_v[k] = _update(args[k], grads[k], args["m_" + k], args["v_" + k], f"adamw_{k}")
    return (loss, grad_x, *[grads[k] for k in WEIGHTS], *[delta[k] for k in WEIGHTS],
            *[new_m[k] for k in WEIGHTS], *[new_v[k] for k in WEIGHTS])


def kernel(x, meta_tokens, attn_norm, w_in, conv_w, q_lat_norm, kv_lat_norm, w_uq, w_ukv, q_norm, k_norm, pool_w, pool_scale, w_branch_a, w_branch_b, w_branch_c, w_o, mlp_norm, w_up, w_down, loss_target, m_meta_tokens, m_attn_norm, m_w_in, m_conv_w, m_q_lat_norm, m_kv_lat_norm, m_w_uq, m_w_ukv, m_q_norm, m_k_norm, m_pool_w, m_pool_scale, m_w_branch_a, m_w_branch_b, m_w_branch_c, m_w_o, m_mlp_norm, m_w_up, m_w_down, v_meta_tokens, v_attn_norm, v_w_in, v_conv_w, v_q_lat_norm, v_kv_lat_norm, v_w_uq, v_w_ukv, v_q_norm, v_k_norm, v_pool_w, v_pool_scale, v_w_branch_a, v_w_branch_b, v_w_branch_c, v_w_o, v_mlp_norm, v_w_up, v_w_down):
    return _step(dict(locals()))
```

```python
import functools
import math

import jax
import jax.numpy as jnp
from jax import lax
from jax.experimental import pallas as pl
from jax.experimental.pallas import tpu as pltpu

F32 = jnp.float32
BF16 = jnp.bfloat16
MESH = pl.DeviceIdType.MESH

EPS = 1e-6
N_META = 16
QK_NOPE = 128
QK_ROPE = 64
QK_HEAD = QK_NOPE + QK_ROPE
V_HEAD = 128
HEAD_PAD = 256
Q_LORA = 512
KV_LORA = 512
ROPE_THETA = 10000.0
POOL_WINDOWS = (2, 4, 8, 16)
HALO = 16
LANES = 128
ADAM_LR = 0.001
ADAM_B1 = 0.9
ADAM_B2 = 0.999
ADAM_EPS = 1e-08
ADAM_WD = 0.01
ADAM_STEP = 10
VMEM_LIMIT = 52 * 1024 * 1024
NEG = -1e30
ATTN_SCALE = QK_HEAD ** -0.5
LOG2_E = 1.4426950408889634
Q_FOLD = ATTN_SCALE * LOG2_E


def _tile(n, target, mult=LANES):
    best = None
    for t in range(mult, min(n, target) + 1, mult):
        if n % t == 0:
            best = t
    return n if best is None else best


def _params(sem=None):
    return pltpu.CompilerParams(dimension_semantics=sem, vmem_limit_bytes=VMEM_LIMIT)


def _mm(a, b, *, name, ta=False, tb=False, add=None, aux=None, epi=None, out_dtype=F32,
        tm=1056, tn=1024, tk=None, after=(), pieces=None):
    if ta:
        K, M = a.shape
    else:
        M, K = a.shape
    if tb:
        N, kb = b.shape
    else:
        kb, N = b.shape
    assert K == kb, (a.shape, b.shape, ta, tb)
    tm = _tile(M, tm, LANES if ta else 16)
    tn = _tile(N if pieces is None else N // pieces, tn, LANES)
    tk = K if tk is None else _tile(K, tk, LANES if (not ta or tb) else 16)
    nk = K // tk
    a_bytes, b_bytes = a.size * a.dtype.itemsize, b.size * b.dtype.itemsize
    j_outer = nk == 1 and a_bytes * (N // tn) + b_bytes < a_bytes + b_bytes * (M // tm)
    grid = (N // tn, M // tm, nk) if j_outer else (M // tm, N // tn, nk)
    row = (lambda g0, g1: g1) if j_outer else (lambda g0, g1: g0)
    col = (lambda g0, g1: g0) if j_outer else (lambda g0, g1: g1)

    if ta:
        a_spec = pl.BlockSpec((tk, tm), lambda g0, g1, k: (k, row(g0, g1)))
    else:
        a_spec = pl.BlockSpec((tm, tk), lambda g0, g1, k: (row(g0, g1), k))
    if tb:
        b_spec = pl.BlockSpec((tn, tk), lambda g0, g1, k: (col(g0, g1), k))
    else:
        b_spec = pl.BlockSpec((tk, tn), lambda g0, g1, k: (k, col(g0, g1)))
    o_spec = pl.BlockSpec((tm, tn), lambda g0, g1, k: (row(g0, g1), col(g0, g1)))
    per = None if pieces is None else N // pieces // tn
    in_specs = [a_spec, b_spec]
    operands = [a, b]
    if add is not None:
        in_specs.append(o_spec)
        operands.append(add)
    if aux is not None:
        in_specs.append(o_spec)
        operands.append(aux)
    after = tuple(after)
    in_specs += [pl.BlockSpec(memory_space=pl.ANY)] * len(after)
    operands += list(after)
    if epi == "relu2":
        out_shape = (jax.ShapeDtypeStruct((M, N), BF16), jax.ShapeDtypeStruct((M, N), BF16))
        out_specs = (o_spec, o_spec)
    elif pieces is not None:
        out_shape = jax.ShapeDtypeStruct((pieces, M, N // pieces), out_dtype)
        out_specs = pl.BlockSpec((1, tm, tn), lambda g0, g1, k: (col(g0, g1) // per, row(g0, g1), col(g0, g1) % per))
    else:
        out_shape = jax.ShapeDtypeStruct((M, N), out_dtype)
        out_specs = o_spec
    dims =(((0 if ta else 1,), (1 if tb else 0,)), ((), ()))
    has_add, has_aux = add is not None, aux is not None

    def body(*refs):
        a_ref, b_ref = refs[0], refs[1]
        pos = 2
        add_ref = aux_ref = None
        if has_add:
            add_ref = refs[pos]
            pos += 1
        if has_aux:
            aux_ref = refs[pos]
            pos += 1
        pos += len(after)
        n_out = 2 if epi == "relu2" else 1
        out_refs = refs[pos:pos + n_out]
        acc_ref = refs[pos + n_out] if nk > 1 else None

        part = lax.dot_general(a_ref[...].astype(BF16), b_ref[...].astype(BF16), dims,
                               preferred_element_type=F32)

        def finish(acc):
            if has_add:
                acc = acc + add_ref[...].astype(F32)
            if epi == "relu2":
                r = jnp.maximum(acc, 0.0)
                out_refs[0][...] = acc.astype(BF16)
                out_refs[1][...] = (r * r).astype(BF16)
            elif epi == "drelu2":
                u = aux_ref[...].astype(F32)
                out_refs[0][...] = (acc * (2.0 * jnp.maximum(u, 0.0))).astype(out_dtype)
            else:
                out_refs[0][...] = acc.astype(out_dtype).reshape(out_refs[0].shape)

        if nk == 1:
            finish(part)
        else:
            k = pl.program_id(2)

            @pl.when(k == 0)
            def _():
                acc_ref[...] = part

            @pl.when(k > 0)
            def _():
                acc_ref[...] += part

            @pl.when(k == nk - 1)
            def _():
                finish(acc_ref[...])

    scratch = [pltpu.VMEM((tm, tn), F32)] if nk > 1 else []
    return pl.pallas_call(
        body, name=name, grid=grid, in_specs=in_specs, out_specs=out_specs, out_shape=out_shape,
        scratch_shapes=scratch, compiler_params=_params(("parallel", "parallel", "arbitrary")),
    )(*operands)


def _rms_fwd(x, g, *, name, width=None, seg=0, tm=384, after=()):
    T = x.shape[0]
    width = x.shape[1] if width is None else width
    tm = _tile(T, tm, 16)
    after = tuple(after)

    def body(x_ref, g_ref, *rest):
        xf = x_ref[...].astype(F32)
        r = lax.rsqrt(jnp.mean(xf * xf, axis=-1, keepdims=True) + EPS)
        rest[-1][...] = (xf * r * g_ref[...]).astype(BF16)

    return pl.pallas_call(
        body, name=name, grid=(T // tm,),
        in_specs=[pl.BlockSpec((tm, width), lambda i: (i, seg)), pl.BlockSpec((1, width), lambda i: (0, 0))]
        + [pl.BlockSpec(memory_space=pl.ANY)] * len(after),
        out_specs=pl.BlockSpec((tm, width), lambda i: (i, 0)),
        out_shape=jax.ShapeDtypeStruct((T, width), BF16),
        compiler_params=_params(("parallel",)),
    )(x, g, *after)


def _rms_bwd(dy, x, g, *, name, width=None, seg=0, res=None, out_dtype=F32, tm=384, bf16_copy=False):
    T = x.shape[0]
    width = x.shape[1] if width is None else width
    tm = _tile(T, tm, 16)
    has_res = res is not None

    def body(*refs):
        dy_ref, x_ref, g_ref = refs[:3]
        res_ref = refs[3] if has_res else None
        dx_ref, dg_ref = refs[4 if has_res else 3], refs[-1]
        xf = x_ref[...].astype(F32)
        dyf = dy_ref[...].astype(F32)
        r = lax.rsqrt(jnp.mean(xf * xf, axis=-1, keepdims=True) + EPS)
        xhat = xf * r
        dyh = dyf * g_ref[...]
        dx = r * (dyh - xhat * jnp.mean(dyh * xhat, axis=-1, keepdims=True))
        if has_res:
            dx = dx + res_ref[...].astype(F32)
        dx_ref[...] = dx.astype(out_dtype)
        if bf16_copy:
            refs[-2][...] = dx.astype(BF16)
        part = jnp.sum(dyf * xhat, axis=0, keepdims=True)

        @pl.when(pl.program_id(0) == 0)
        def _():
            dg_ref[...] = part

        @pl.when(pl.program_id(0) > 0)
        def _():
            dg_ref[...] += part

    row = pl.BlockSpec((tm, width), lambda i: (i, 0))
    in_specs = [row, pl.BlockSpec((tm, width), lambda i: (i, seg)), pl.BlockSpec((1, width), lambda i: (0, 0))]
    operands = [dy, x, g]
    if has_res:
        in_specs.append(row)
        operands.append(res)
    vec = pl.BlockSpec((1, width), lambda i: (0, 0))
    full = [jax.ShapeDtypeStruct((T, width), out_dtype)] + ([jax.ShapeDtypeStruct((T, width), BF16)] if bf16_copy else [])
    return pl.pallas_call(
        body, name=name, grid=(T // tm,), in_specs=in_specs,
        out_specs=tuple([row] * len(full) + [vec]),
        out_shape=tuple(full + [jax.ShapeDtypeStruct((1, width), F32)]),
        compiler_params=_params(("arbitrary",)),
    )(*operands)


def _down(ext, k):
    return pltpu.roll(ext, k, 0)


def _up(ext, k):
    return pltpu.roll(ext, ext.shape[0] - k, 0)


def _pre_halo(ref, r, R):
    start = pl.multiple_of(jnp.maximum(r * R - HALO, 0), 8)
    keep = (r > 0).astype(F32)
    return ref[pl.ds(start, HALO), :].astype(F32) * keep


def _post_halo(ref, r, R, n_chunks):
    start = pl.multiple_of(jnp.minimum(r * R + R, (n_chunks - 1) * R + R - HALO), 8)
    keep = (r < n_chunks - 1).astype(F32)
    return ref[pl.ds(start, HALO), :].astype(F32) * keep


def _chunk(ref, r, R):
    return ref[pl.ds(pl.multiple_of(r * R, 8), R), :].astype(F32)


def _conv_fwd(rest, conv_w, *, name, dc, tc=128, rows=1056):
    T = rest.shape[0]
    tc = _tile(dc, tc)
    nb = dc // tc
    R = _tile(T, rows, 16)
    n_chunks = T // R

    def body(u_ref, b_ref, c_ref, w_ref, y_ref):
        w0, w1, w2 = w_ref[0:1, :], w_ref[1:2, :], w_ref[2:3, :]

        def chunk(r, carry):
            cu = _chunk(c_ref, r, R) * _chunk(u_ref, r, R)
            ext = jnp.concatenate([_pre_halo(c_ref, r, R) * _pre_halo(u_ref, r, R), cu], axis=0)
            conv = w0 * _down(ext, 2)[HALO:] + w1 * _down(ext, 1)[HALO:] + w2 * cu
            y_ref[pl.ds(pl.multiple_of(r * R, 8), R), :] = (_chunk(b_ref, r, R) * conv).astype(BF16)
            return carry

        lax.fori_loop(0, n_chunks, chunk, 0)

    col = lambda off: pl.BlockSpec((T, tc), lambda j: (0, off * nb + j))
    return pl.pallas_call(
        body, name=name, grid=(nb,),
        in_specs=[col(0), col(1), col(2), pl.BlockSpec((3, tc), lambda j: (0, j))],
        out_specs=pl.BlockSpec((T, tc), lambda j: (0, j)),
        out_shape=jax.ShapeDtypeStruct((T, dc), BF16),
        compiler_params=_params(("parallel",)),
    )(rest, rest, rest, conv_w)


def _conv_bwd(rest, conv_w, dy, *, name, dc, tc=128, rows=1056):
    T = rest.shape[0]
    tc = _tile(dc, tc)
    nb = dc // tc
    R = _tile(T, rows, 16)
    n_chunks = T // R

    def body(u_ref, b_ref, c_ref, w_ref, dy_ref, du_ref, db_ref, dc_ref, dw_ref):
        w0, w1, w2 = w_ref[0:1, :], w_ref[1:2, :], w_ref[2:3, :]

        def chunk(r, carry):
            a0, a1, a2 = carry
            u, b, c = _chunk(u_ref, r, R), _chunk(b_ref, r, R), _chunk(c_ref, r, R)
            dy_c = _chunk(dy_ref, r, R)
            cu = c * u
            ext = jnp.concatenate([_pre_halo(c_ref, r, R) * _pre_halo(u_ref, r, R), cu], axis=0)
            cu1, cu2 = _down(ext, 1)[HALO:], _down(ext, 2)[HALO:]
            conv = w0 * cu2 + w1 * cu1 + w2 * cu
            dconv = dy_c * b
            dext = jnp.concatenate(
                [dconv, _post_halo(dy_ref, r, R, n_chunks) * _post_halo(b_ref, r, R, n_chunks)], axis=0)
            dcu = w2 * dconv + w1 * _up(dext, 1)[:R] + w0 * _up(dext, 2)[:R]
            rows_at = pl.ds(pl.multiple_of(r * R, 8), R)
            db_ref[rows_at, :] = (dy_c * conv).astype(BF16)
            du_ref[rows_at, :] = (dcu * c).astype(BF16)
            dc_ref[rows_at, :] = (dcu * u).astype(BF16)
            return (a0 + jnp.sum(dconv * cu2, axis=0, keepdims=True),
                    a1 + jnp.sum(dconv * cu1, axis=0, keepdims=True),
                    a2 + jnp.sum(dconv * cu, axis=0, keepdims=True))

        zero = jnp.zeros((1, tc), F32)
        a0, a1, a2 = lax.fori_loop(0, n_chunks, chunk, (zero, zero, zero))
        dw_ref[0:1, :] = a0
        dw_ref[1:2, :] = a1
        dw_ref[2:3, :] = a2

    col = lambda off: pl.BlockSpec((T, tc), lambda j: (0, off * nb + j))
    own = pl.BlockSpec((T, tc), lambda j: (0, j))
    return pl.pallas_call(
        body, name=name, grid=(nb,),
        in_specs=[col(0), col(1), col(2), pl.BlockSpec((3, tc), lambda j: (0, j)), own],
        out_specs=(own, own, own, pl.BlockSpec((3, tc), lambda j: (0, j))),
        out_shape=(jax.ShapeDtypeStruct((T, dc), BF16),) * 3 + (jax.ShapeDtypeStruct((3, dc), F32),),
        compiler_params=_params(("parallel",)),
    )(rest, rest, rest, conv_w, dy)


def _window_count(r, R, n_rows, w, first_row_offset):
    t = lax.broadcasted_iota(jnp.int32, (n_rows, 1), 0) + (r * R + first_row_offset)
    return jnp.minimum(t + 1, w).astype(F32)


def _pool_fwd(rest, pool_w, pool_scale, *, name, seg0, pg, rows=1056):
    T = rest.shape[0]
    R = _tile(T, rows, 16)
    n_chunks = T // R
    n_groups = len(POOL_WINDOWS)

    def body(x_ref, w_ref, s_ref, y_ref):
        def run(window):
            def chunk(r, carry):
                g = _chunk(x_ref, r, R)
                s = jnp.concatenate([_pre_halo(x_ref, r, R), g], axis=0)
                k = 1
                while k < window:
                    s = s + _down(s, k)
                    k *= 2
                pooled = s[HALO:] / _window_count(r, R, R, window, 0) - g
                mixed = jnp.dot(pooled.astype(BF16), w_ref[0], preferred_element_type=F32)
                y_ref[pl.ds(pl.multiple_of(r * R, 8), R), :] = (mixed * s_ref[...]).astype(BF16)
                return carry

            lax.fori_loop(0, n_chunks, chunk, 0)

        for gi, window in enumerate(POOL_WINDOWS):
            pl.when(pl.program_id(0) == gi)(functools.partial(run, window))

    return pl.pallas_call(
        body, name=name, grid=(n_groups,),
        in_specs=[pl.BlockSpec((T, pg), lambda g: (0, seg0 + g)),
                  pl.BlockSpec((1, pg, pg), lambda g: (g, 0, 0)),
                  pl.BlockSpec((1, pg), lambda g: (0, g))],
        out_specs=pl.BlockSpec((T, pg), lambda g: (0, g)),
        out_shape=jax.ShapeDtypeStruct((T, n_groups * pg), BF16),
        compiler_params=_params(("parallel",)),
    )(rest, pool_w, pool_scale)


def _pool_bwd(rest, pool_w, pool_scale, dy, *, name, seg0, pg, rows=1056):
    T = rest.shape[0]
    R = _tile(T, rows, 16)
    n_chunks = T // R
    n_groups = len(POOL_WINDOWS)

    def body(x_ref, w_ref, s_ref, dy_ref, dx_ref, dw_ref, ds_ref):
        def run(window):
            def chunk(r, carry):
                dw_acc, ds_acc = carry
                g = _chunk(x_ref, r, R)
                s = jnp.concatenate([_pre_halo(x_ref, r, R), g], axis=0)
                k = 1
                while k < window:
                    s = s + _down(s, k)
                    k *= 2
                pooled = (s[HALO:] / _window_count(r, R, R, window, 0) - g).astype(BF16)
                mixed = jnp.dot(pooled, w_ref[0], preferred_element_type=F32)
                dy_c = _chunk(dy_ref, r, R)
                dm_ext = (jnp.concatenate([dy_c, _post_halo(dy_ref, r, R, n_chunks)], axis=0)
                          * s_ref[...]).astype(BF16)
                dpool_ext = lax.dot_general(dm_ext, w_ref[0], (((1,), (1,)), ((), ())),
                                            preferred_element_type=F32)
                a = dpool_ext / _window_count(r, R, R + HALO, window, 0)
                k = 1
                while k < window:
                    a = a + _up(a, k)
                    k *= 2
                dx_ref[pl.ds(pl.multiple_of(r * R, 8), R), :] = (a[:R] - dpool_ext[:R]).astype(BF16)
                dw_acc = dw_acc + lax.dot_general(pooled, dm_ext[:R], (((0,), (0,)), ((), ())),
                                                  preferred_element_type=F32)
                ds_acc = ds_acc + jnp.sum(dy_c * mixed, axis=0, keepdims=True)
                return dw_acc, ds_acc

            dw_acc, ds_acc = lax.fori_loop(0, n_chunks, chunk,
                                           (jnp.zeros((pg, pg), F32), jnp.zeros((1, pg), F32)))
            dw_ref[0] = dw_acc
            ds_ref[...] = ds_acc

        for gi, window in enumerate(POOL_WINDOWS):
            pl.when(pl.program_id(0) == gi)(functools.partial(run, window))

    own = pl.BlockSpec((T, pg), lambda g: (0, g))
    return pl.pallas_call(
        body, name=name, grid=(n_groups,),
        in_specs=[pl.BlockSpec((T, pg), lambda g: (0, seg0 + g)),
                  pl.BlockSpec((1, pg, pg), lambda g: (g, 0, 0)),
                  pl.BlockSpec((1, pg), lambda g: (0, g)), own],
        out_specs=(own, pl.BlockSpec((1, pg, pg), lambda g: (g, 0, 0)), pl.BlockSpec((1, pg), lambda g: (0, g))),
        out_shape=(jax.ShapeDtypeStruct((T, n_groups * pg), BF16),
                   jax.ShapeDtypeStruct((n_groups, pg, pg), F32),
                   jax.ShapeDtypeStruct((1, n_groups * pg), F32)),
        compiler_params=_params(("parallel",)),
    )(rest, pool_w, pool_scale, dy)


def _rope(r, cos_t, sin_t):
    return r * cos_t + pltpu.roll(r, LANES // 2, 1) * sin_t


def _rope_t(d, cos_t, sin_t):
    return d * cos_t + pltpu.roll(d * sin_t, LANES // 2, 1)


def _qk_fwd(q_raw, k_nope, rest, cos_t, sin_t, q_norm, k_norm, *, name, heads, kr_seg, tm=192):
    T = q_raw.shape[0]
    tm = _tile(T, tm, 16)

    def body(q_ref, kn_ref, kr_ref, c_ref, s_ref, gq_ref, gk_ref, qo_ref, ko_ref):
        cos_b, sin_b = c_ref[...], s_ref[...]
        kr = kr_ref[:, 0:LANES]
        kr_ss = jnp.sum(kr * kr, axis=-1, keepdims=True)
        gq, gk = gq_ref[...], gk_ref[...]
        for h in range(heads):
            lo = h * HEAD_PAD
            q = q_ref[:, lo:lo + HEAD_PAD]
            rq = lax.rsqrt(jnp.sum(q * q, axis=-1, keepdims=True) / QK_HEAD + EPS)
            qn = q * (rq * Q_FOLD) * gq
            qo_ref[:, lo:lo + LANES] = qn[:, :LANES].astype(BF16)
            qo_ref[:, lo + LANES:lo + HEAD_PAD] = _rope(qn[:, LANES:], cos_b, sin_b).astype(BF16)
            kn = kn_ref[:, h * LANES:(h + 1) * LANES]
            rk = lax.rsqrt((jnp.sum(kn * kn, axis=-1, keepdims=True) + kr_ss) / QK_HEAD + EPS)
            ko_ref[:, lo:lo + LANES] = (kn * rk * gk[:, :LANES]).astype(BF16)
            ko_ref[:, lo + LANES:lo + HEAD_PAD] = _rope(kr * rk * gk[:, LANES:], cos_b, sin_b).astype(BF16)

    wq, wk = heads * HEAD_PAD, heads * LANES
    return pl.pallas_call(
        body, name=name, grid=(T // tm,),
        in_specs=[pl.BlockSpec((tm, wq), lambda i: (i, 0)), pl.BlockSpec((tm, wk), lambda i: (i, 0)),
                  pl.BlockSpec((tm, HEAD_PAD), lambda i: (i, kr_seg)),
                  pl.BlockSpec((tm, LANES), lambda i: (i, 0)), pl.BlockSpec((tm, LANES), lambda i: (i, 0)),
                  pl.BlockSpec((1, HEAD_PAD), lambda i: (0, 0)), pl.BlockSpec((1, HEAD_PAD), lambda i: (0, 0))],
        out_specs=(pl.BlockSpec((tm, wq), lambda i: (i, 0)), pl.BlockSpec((tm, wq), lambda i: (i, 0))),
        out_shape=(jax.ShapeDtypeStruct((T, wq), BF16), jax.ShapeDtypeStruct((T, wq), BF16)),
        compiler_params=_params(("parallel",)),
    )(q_raw, k_nope, rest, cos_t, sin_t, q_norm, k_norm)


def _qk_bwd(dq, dk, q_raw, k_nope, rest, cos_t, sin_t, q_norm, k_norm, *, name, heads, kr_seg, tm=128):
    T = q_raw.shape[0]
    tm = _tile(T, tm, 16)

    def body(dq_ref, dk_ref, q_ref, kn_ref, kr_ref, c_ref, s_ref, gq_ref, gk_ref,
             dqr_ref, dkn_ref, dkr_ref, dgq_ref, dgk_ref):
        cos_b, sin_b = c_ref[...], s_ref[...]
        kr = kr_ref[:, 0:LANES]
        kr_ss = jnp.sum(kr * kr, axis=-1, keepdims=True)
        gq, gk = gq_ref[...], gk_ref[...]
        dgq = jnp.zeros((1, HEAD_PAD), F32)
        dgk_n = jnp.zeros((1, LANES), F32)
        dgk_r = jnp.zeros((1, LANES), F32)
        dkr = jnp.zeros((tm, LANES), F32)
        for h in range(heads):
            lo = h * HEAD_PAD
            q = q_ref[:, lo:lo + HEAD_PAD]
            rq = lax.rsqrt(jnp.sum(q * q, axis=-1, keepdims=True) / QK_HEAD + EPS)
            qhat = q * rq
            dqn = jnp.concatenate([dq_ref[:, lo:lo + LANES],
                                   _rope_t(dq_ref[:, lo + LANES:lo + HEAD_PAD], cos_b, sin_b)], axis=1) * ATTN_SCALE
            dgq = dgq + jnp.sum(dqn * qhat, axis=0, keepdims=True)
            dqh = dqn * gq
            dqr_ref[:, lo:lo + HEAD_PAD] = (
                rq * (dqh - qhat * (jnp.sum(dqh * qhat, axis=-1, keepdims=True) / QK_HEAD))).astype(BF16)
            kn = kn_ref[:, h * LANES:(h + 1) * LANES]
            rk = lax.rsqrt((jnp.sum(kn * kn, axis=-1, keepdims=True) + kr_ss) / QK_HEAD + EPS)
            khat_n, khat_r = kn * rk, kr * rk
            dkn_n = dk_ref[:, lo:lo + LANES] * (1.0 / LOG2_E)
            dkn_r = _rope_t(dk_ref[:, lo + LANES:lo + HEAD_PAD], cos_b, sin_b) * (1.0 / LOG2_E)
            dgk_n = dgk_n + jnp.sum(dkn_n * khat_n, axis=0, keepdims=True)
            dgk_r = dgk_r + jnp.sum(dkn_r * khat_r, axis=0, keepdims=True)
            dkh_n, dkh_r = dkn_n * gk[:, :LANES], dkn_r * gk[:, LANES:]
            proj = (jnp.sum(dkh_n * khat_n, axis=-1, keepdims=True)
                    + jnp.sum(dkh_r * khat_r, axis=-1, keepdims=True)) / QK_HEAD
            dkn_ref[:, h * LANES:(h + 1) * LANES] = (rk * (dkh_n - khat_n * proj)).astype(BF16)
            dkr = dkr + rk * (dkh_r - khat_r * proj)
        dkr_ref[:, 0:LANES] = dkr.astype(BF16)
        dkr_ref[:, LANES:HEAD_PAD] = jnp.zeros((tm, HEAD_PAD - LANES), BF16)
        dgk = jnp.concatenate([dgk_n, dgk_r], axis=1)

        @pl.when(pl.program_id(0) == 0)
        def _():
            dgq_ref[...] = dgq
            dgk_ref[...] = dgk

        @pl.when(pl.program_id(0) > 0)
        def _():
            dgq_ref[...] += dgq
            dgk_ref[...] += dgk

    wq, wk = heads * HEAD_PAD, heads * LANES
    row = lambda w: pl.BlockSpec((tm, w), lambda i: (i, 0))
    vec = pl.BlockSpec((1, HEAD_PAD), lambda i: (0, 0))
    return pl.pallas_call(
        body, name=name, grid=(T // tm,),
        in_specs=[row(wq), row(wq), row(wq), row(wk), pl.BlockSpec((tm, HEAD_PAD), lambda i: (i, kr_seg)),
                  row(LANES), row(LANES), vec, vec],
        out_specs=(row(wq), row(wk), row(HEAD_PAD), vec, vec),
        out_shape=(jax.ShapeDtypeStruct((T, wq), BF16), jax.ShapeDtypeStruct((T, wk), BF16),
                   jax.ShapeDtypeStruct((T, HEAD_PAD), BF16),
                   jax.ShapeDtypeStruct((1, HEAD_PAD), F32), jax.ShapeDtypeStruct((1, HEAD_PAD), F32)),
        compiler_params=_params(("arbitrary",)),
    )(dq, dk, q_raw, k_nope, rest, cos_t, sin_t, q_norm, k_norm)


def _causal_mask(s):
    row = lax.broadcasted_iota(jnp.int32, s.shape, 0)
    col = lax.broadcasted_iota(jnp.int32, s.shape, 1)
    return jnp.where(row >= col, s, NEG)


def _flash_fwd(q, k, v, *, name, heads, tq=384, hp=2):
    T = q.shape[0]
    tq = _tile(T, tq, LANES)
    nq = T // tq
    nt = (((1,), (1,)), ((), ()))

    def body(q_ref, k_ref, v_ref, o_ref, lse_ref):
        def q_block(i, carry):
            q_at = pl.ds(pl.multiple_of(i * tq, tq), tq)
            qbs = [q_ref[q_at, h * HEAD_PAD:(h + 1) * HEAD_PAD] for h in range(hp)]

            def step(j, state, masked):
                k_at = pl.ds(pl.multiple_of(j * tq, tq), tq)
                new = []
                scores = [lax.dot_general(qbs[h], k_ref[k_at, h * HEAD_PAD:(h + 1) * HEAD_PAD], nt,
                                          preferred_element_type=F32) for h in range(hp)]
                for h in range(hp):
                    m, l, acc = state[h]
                    s = scores[h]
                    if masked:
                        s = _causal_mask(s)
                    m_new = jnp.maximum(m, jnp.max(s, axis=-1, keepdims=True))
                    p = jnp.exp2(s - m_new)
                    alpha = jnp.exp2(m - m_new)
                    l = alpha * l + jnp.sum(p, axis=-1, keepdims=True)
                    acc = alpha * acc + jnp.dot(p.astype(BF16), v_ref[k_at, h * V_HEAD:(h + 1) * V_HEAD],
                                                preferred_element_type=F32)
                    new.append((m_new, l, acc))
                return tuple(new)

            init = tuple((jnp.full((tq, 1), NEG, F32), jnp.zeros((tq, 1), F32), jnp.zeros((tq, V_HEAD), F32))
                         for _ in range(hp))
            state = lax.fori_loop(0, i, lambda j, st: step(j, st, False), init)
            state = step(i, state, True)
            for h in range(hp):
                m, l, acc = state[h]
                o_ref[q_at, h * V_HEAD:(h + 1) * V_HEAD] = (acc / l).astype(BF16)
                lse_ref[h, q_at, :] = jnp.broadcast_to(m + jnp.log2(l), (tq, LANES))
            return carry

        lax.fori_loop(0, nq, q_block, 0)

    qk_spec = pl.BlockSpec((T, hp * HEAD_PAD), lambda g: (0, g))
    v_spec = pl.BlockSpec((T, hp * V_HEAD), lambda g: (0, g))
    return pl.pallas_call(
        body, name=name, grid=(heads // hp,), in_specs=[qk_spec, qk_spec, v_spec],
        out_specs=(v_spec, pl.BlockSpec((hp, T, LANES), lambda g: (g, 0, 0))),
        out_shape=(jax.ShapeDtypeStruct((T, heads * V_HEAD), BF16), jax.ShapeDtypeStruct((heads, T, LANES), F32)),
        compiler_params=_params(("parallel",)),
    )(q, k, v)


def _flash_bwd(q, k, v, o, do, lse, *, name, heads, tq=384):
    T = q.shape[0]
    tq = _tile(T, tq, LANES)
    nq = T // tq
    nt = (((1,), (1,)), ((), ()))
    tn = (((0,), (0,)), ((), ()))

    def body(q_ref, k_ref, v_ref, o_ref, do_ref, lse_ref, dq_ref, dk_ref, dv_ref, delta_ref):
        def fill_delta(i, carry):
            at = pl.ds(pl.multiple_of(i * tq, tq), tq)
            d = jnp.sum(o_ref[at, :].astype(F32) * do_ref[at, :].astype(F32), axis=-1, keepdims=True)
            delta_ref[at, :] = jnp.broadcast_to(d, (tq, LANES))
            dq_ref[at, :] = jnp.zeros((tq, HEAD_PAD), F32)
            return carry

        lax.fori_loop(0, nq, fill_delta, 0)

        def kv_block(j, carry):
            k_at = pl.ds(pl.multiple_of(j * tq, tq), tq)
            kb, vb = k_ref[k_at, :], v_ref[k_at, :]

            def steps(blocks, state, masked):
                dk_acc, dv_acc = state
                at = [pl.ds(pl.multiple_of(i * tq, tq), tq) for i in blocks]
                qbs = [q_ref[a, :] for a in at]
                dobs = [do_ref[a, :] for a in at]
                scores = [lax.dot_general(qb, kb, nt, preferred_element_type=F32) for qb in qbs]
                dps = [lax.dot_general(dob, vb, nt, preferred_element_type=F32) for dob in dobs]
                for a, qb, dob, sc, dp in zip(at, qbs, dobs, scores, dps):
                    if masked:
                        sc = _causal_mask(sc)
                    p = jnp.exp2(sc - lse_ref[0, a, :][:, 0:1])
                    ds = (p * (dp - delta_ref[a, :][:, 0:1])).astype(BF16)
                    dv_acc = dv_acc + lax.dot_general(p.astype(BF16), dob, tn, preferred_element_type=F32)
                    dk_acc = dk_acc + lax.dot_general(ds, qb, tn, preferred_element_type=F32)
                    dq_ref[a, :] += jnp.dot(ds, kb, preferred_element_type=F32)
                return dk_acc, dv_acc

            state = steps([j], (jnp.zeros((tq, HEAD_PAD), F32), jnp.zeros((tq, V_HEAD), F32)), True)
            rest = nq - 1 - j
            state = lax.fori_loop(0, rest // 2, lambda t, st: steps([j + 1 + 2 * t, j + 2 + 2 * t], st, False), state)
            dk_acc, dv_acc = lax.cond(rest % 2 == 1, lambda st: steps([nq - 1], st, False), lambda st: st, state)
            dk_ref[k_at, :] = dk_acc
            dv_ref[k_at, :] = dv_acc.astype(BF16)
            return carry

        lax.fori_loop(0, nq, kv_block, 0)

    qk_spec = pl.BlockSpec((T, HEAD_PAD), lambda h: (0, h))
    v_spec = pl.BlockSpec((T, V_HEAD), lambda h: (0, h))
    return pl.pallas_call(
        body, name=name, grid=(heads,),
        in_specs=[qk_spec, qk_spec, v_spec, v_spec, v_spec, pl.BlockSpec((1, T, LANES), lambda h: (h, 0, 0))],
        out_specs=(qk_spec, qk_spec, v_spec),
        out_shape=(jax.ShapeDtypeStruct((T, heads * HEAD_PAD), F32), jax.ShapeDtypeStruct((T, heads * HEAD_PAD), F32),
                   jax.ShapeDtypeStruct((T, heads * V_HEAD), BF16)),
        scratch_shapes=[pltpu.VMEM((T, LANES), F32)],
        compiler_params=_params(("parallel",)),
    )(q, k, v, o, do, lse)


def _merge_fwd(gl, pa, pb, pc, *, name, d, tm=384, tn=1024):
    T = pa.shape[0]
    tm, tn = _tile(T, tm, 16), _tile(d, tn)
    nb = d // tn

    def body(g0, g1, g2, a, b, c, o_ref):
        o_ref[...] = (jax.nn.sigmoid(g0[...]) * a[...] + jax.nn.sigmoid(g1[...]) * b[...]
                      + jax.nn.sigmoid(g2[...]) * c[...]).astype(BF16)

    gate = lambda n: pl.BlockSpec((tm, tn), lambda i, j: (i, n * nb + j))
    blk = pl.BlockSpec((tm, tn), lambda i, j: (i, j))
    return pl.pallas_call(
        body, name=name, grid=(T // tm, nb), in_specs=[gate(0), gate(1), gate(2), blk, blk, blk],
        out_specs=blk, out_shape=jax.ShapeDtypeStruct((T, d), BF16),
        compiler_params=_params(("parallel", "parallel")),
    )(gl, gl, gl, pa, pb, pc)


def _merge_bwd(dm, gl, pa, pb, pc, *, name, d, tm=384, tn=1024):
    T = pa.shape[0]
    tm, tn = _tile(T, tm, 16), _tile(d, tn)
    nb = d // tn

    def body(dm_ref, g0, g1, g2, a, b, c, da, db, dc, dg0, dg1, dg2):
        dmv = dm_ref[...]
        for g_ref, p_ref, dp_ref, dg_ref in ((g0, a, da, dg0), (g1, b, db, dg1), (g2, c, dc, dg2)):
            sg = jax.nn.sigmoid(g_ref[...])
            dp_ref[...] = (dmv * sg).astype(BF16)
            dg_ref[...] = (dmv * p_ref[...] * sg * (1.0 - sg)).astype(BF16)

    gate = lambda n: pl.BlockSpec((tm, tn), lambda i, j: (i, n * nb + j))
    blk = pl.BlockSpec((tm, tn), lambda i, j: (i, j))
    return pl.pallas_call(
        body, name=name, grid=(T // tm, nb), in_specs=[blk, gate(0), gate(1), gate(2), blk, blk, blk],
        out_specs=(blk,) * 6, out_shape=(jax.ShapeDtypeStruct((T, d), BF16),) * 6,
        compiler_params=_params(("parallel", "parallel")),
    )(dm, gl, gl, gl, pa, pb, pc)


def _loss(y, target, *, name, first, last, tm=384):
    T, d = y.shape
    tm = _tile(T, tm, 16)

    def body(y_ref, t_ref, loss_ref, dy_ref, dyb_ref):
        i = pl.program_id(0)
        row = lax.broadcasted_iota(jnp.int32, (tm, 1), 0) + i * tm
        real = jnp.logical_and(row >= first, row < last)
        err = jnp.where(real, y_ref[...] - t_ref[...], 0.0)
        dy_ref[...] = err * (1.0 / d)
        dyb_ref[...] = (err * (1.0 / d)).astype(BF16)
        part = jnp.broadcast_to(jnp.sum(err * err, keepdims=True).reshape(1, 1), (1, LANES))

        @pl.when(i == 0)
        def _():
            loss_ref[...] = part

        @pl.when(i > 0)
        def _():
            loss_ref[...] += part

    blk = pl.BlockSpec((tm, d), lambda i: (i, 0))
    return pl.pallas_call(
        body, name=name, grid=(T // tm,), in_specs=[blk, blk],
        out_specs=(pl.BlockSpec((1, LANES), lambda i: (0, 0)), blk, blk),
        out_shape=(jax.ShapeDtypeStruct((1, LANES), F32), jax.ShapeDtypeStruct((T, d), F32),
                   jax.ShapeDtypeStruct((T, d), BF16)),
        compiler_params=_params(("arbitrary",)),
    )(y, target)


def _as3d(a):
    return a.reshape(a.shape[0], -1, a.shape[-1])


def _sum_stack(parts, *, name, out_dtype, rows=256):
    n, R, C = parts.shape
    tr = _tile(R, rows, 16)

    def body(p_ref, o_ref):
        acc = p_ref[0].astype(F32)
        for s in range(1, n):
            acc = acc + p_ref[s].astype(F32)
        o_ref[...] = acc.astype(out_dtype)

    return pl.pallas_call(
        body, name=name, grid=(R // tr,),
        in_specs=[pl.BlockSpec((n, tr, C), lambda i: (0, i, 0))],
        out_specs=pl.BlockSpec((tr, C), lambda i: (i, 0)),
        out_shape=jax.ShapeDtypeStruct((R, C), out_dtype),
        compiler_params=_params(("parallel",)),
    )(parts)


def _adamw(w, g, m, v, *, name, rows=128):
    R, C = w.shape
    tr = _tile(R, rows, 8)
    c1 = 1.0 - ADAM_B1 ** ADAM_STEP
    c2 = 1.0 - ADAM_B2 ** ADAM_STEP

    def body(w_ref, g_ref, m_ref, v_ref, d_ref, nm_ref, nv_ref):
        gv = g_ref[...]
        nm = ADAM_B1 * m_ref[...] + (1.0 - ADAM_B1) * gv
        nv = ADAM_B2 * v_ref[...] + (1.0 - ADAM_B2) * (gv * gv)
        nm_ref[...] = nm
        nv_ref[...] = nv
        d_ref[...] = -ADAM_LR * ((nm / c1) / (jnp.sqrt(nv / c2) + ADAM_EPS) + ADAM_WD * w_ref[...])

    blk = pl.BlockSpec((tr, C), lambda i: (i, 0))
    return pl.pallas_call(
        body, name=name, grid=(R // tr,), in_specs=[blk] * 4, out_specs=(blk,) * 3,
        out_shape=(jax.ShapeDtypeStruct((R, C), F32),) * 3,
        compiler_params=_params(("parallel",)),
    )(w, g, m, v)


def _one_hot(index, n):
    return jnp.broadcast_to((jnp.arange(n) == index).astype(F32)[:, None, None], (n, 8, LANES))


def _is_set(flags_ref, s):
    return flags_ref[s, 0:1, 0:1] > 0.5


def _rows_for(h, width, itemsize, n_stacked, budget, mult):
    return _tile(h, max(mult, budget // (n_stacked * width * itemsize)), mult)


def _pair_sum(pieces, recv, core, *, name):
    _, H, C = recv.shape
    tr = _rows_for(H, C, 2, 1, 2 << 20, 16)
    nh = H // tr

    def body(lo_ref, hi_ref, r_ref, core_ref, o_ref):
        mine = jnp.where(_is_set(core_ref, 0), lo_ref[0], hi_ref[0])
        o_ref[0] = (mine.astype(F32) + r_ref[0].astype(F32)).astype(BF16)

    blk = pl.BlockSpec((1, tr, C), lambda j, i: (j, i, 0))
    return pl.pallas_call(
        body, name=name, grid=(4, nh),
        in_specs=[blk, pl.BlockSpec((1, tr, C), lambda j, i: (j, nh + i, 0)), blk,
                  pl.BlockSpec((2, 8, LANES), lambda j, i: (0, 0, 0))],
        out_specs=blk, out_shape=jax.ShapeDtypeStruct((4, H, C), BF16),
        compiler_params=_params(("parallel", "parallel")),
    )(pieces, pieces, recv, core)


def _chip_sum(pair, landed, chip_flags, *, name):
    _, H, C = pair.shape
    tr = _rows_for(H, C, 2, 4, 8 << 20, 16)

    def body(p_ref, l_ref, chip_ref, o_ref):
        acc = None
        for s in range(4):
            part = jnp.where(_is_set(chip_ref, s), p_ref[s], l_ref[s]).astype(F32)
            acc = part if acc is None else acc + part
        o_ref[...] = acc

    blk = pl.BlockSpec((4, tr, C), lambda i: (0, i, 0))
    return pl.pallas_call(
        body, name=name, grid=(H // tr,),
        in_specs=[blk, blk, pl.BlockSpec((4, 8, LANES), lambda i: (0, 0, 0))],
        out_specs=pl.BlockSpec((tr, C), lambda i: (i, 0)), out_shape=jax.ShapeDtypeStruct((H, C), F32),
        compiler_params=_params(("parallel",)),
    )(pair, landed, chip_flags)


def _adamw_layer(w, m, v, total, recv, core, layer, prev, *, name, col_halves=False, after=()):
    _, R, C = w.shape
    H, wd = total.shape
    tr = _rows_for(H, wd, 4, 1, 2 << 20, 8)
    nh = H // tr
    c1 = 1.0 - ADAM_B1 ** ADAM_STEP
    c2 = 1.0 - ADAM_B2 ** ADAM_STEP
    n_prev = 0 if prev is None else 4
    after = tuple(after)

    def body(*refs):
        w_ref, m_ref, v_ref, t_ref, r_ref, core_ref = refs[:6]
        g_ref, d_ref, nm_ref, nv_ref = refs[6 + n_prev + len(after):]
        half_is_mine = jnp.where(pl.program_id(0) == 0, core_ref[0, 0:1, 0:1], core_ref[1, 0:1, 0:1]) > 0.5
        gv = jnp.where(half_is_mine, t_ref[...], r_ref[...])
        nm = ADAM_B1 * m_ref[0] + (1.0 - ADAM_B1) * gv
        nv = ADAM_B2 * v_ref[0] + (1.0 - ADAM_B2) * (gv * gv)
        g_ref[0] = gv
        nm_ref[0] = nm
        nv_ref[0] = nv
        d_ref[0] = -ADAM_LR * ((nm / c1) / (jnp.sqrt(nv / c2) + ADAM_EPS) + ADAM_WD * w_ref[0])

    if col_halves:
        lay = pl.BlockSpec((1, tr, wd), lambda hf, i: (layer, i, hf))
    else:
        lay = pl.BlockSpec((1, tr, wd), lambda hf, i: (layer, hf * nh + i, 0))
    one = pl.BlockSpec((tr, wd), lambda hf, i: (i, 0))
    operands = [w, m, v, total, recv, core] + ([] if prev is None else list(prev)) + list(after)
    return pl.pallas_call(
        body, name=name, grid=(2, nh),
        in_specs=[lay, lay, lay, one, one, pl.BlockSpec((2, 8, LANES), lambda hf, i: (0, 0, 0))]
        + [ANY] * (n_prev + len(after)),
        out_specs=(lay,) * 4, out_shape=(jax.ShapeDtypeStruct((2, R, C), F32),) * 4,
        input_output_aliases={6 + i: i for i in range(n_prev)},
        compiler_params=_params(("parallel", "parallel")),
    )(*operands)


ANY = pl.BlockSpec(memory_space=pl.ANY)


def _coords():
    return lax.axis_index("x"), lax.axis_index("y"), lax.axis_index("c")


HBM = pl.BlockSpec(memory_space=pltpu.HBM)
SEM = pl.BlockSpec(memory_space=pltpu.SEMAPHORE)
EFFECT = pltpu.SideEffectType.DATAFLOW_SIDE_EFFECTING


def _copies(plan, bufs, send_sems, recv_sems):
    return [pltpu.make_async_remote_copy(src_ref=s, dst_ref=d, send_sem=send_sems.at[i], recv_sem=recv_sems.at[i],
                                         device_id=to, device_id_type=MESH)
            for i, (s, d, to) in enumerate(plan(bufs))]


def _start_copies(bufs, groups, *, name):
    nb, ng = len(bufs), len(groups)

    def body(*refs):
        buf_refs = refs[:nb]
        sems = refs[nb:nb + 2 * ng]
        token = refs[-1]
        for g, (plan, _) in enumerate(groups):
            for cp in _copies(plan, buf_refs, sems[2 * g], sems[2 * g + 1]):
                cp.start()
        token[...] = jnp.zeros_like(token)

    sem_shapes = []
    for _, n in groups:
        sem_shapes += [pltpu.SemaphoreType.DMA((n,)), pltpu.SemaphoreType.DMA((n,))]
    out = pl.pallas_call(
        body, name=name, in_specs=[HBM] * nb,
        out_specs=tuple([SEM] * (2 * ng) + [HBM] * nb + [pl.BlockSpec(memory_space=pltpu.VMEM)]),
        out_shape=tuple(sem_shapes + [pltpu.HBM(b.shape, b.dtype) for b in bufs] + [jax.ShapeDtypeStruct((8, LANES), F32)]),
        input_output_aliases={i: 2 * ng + i for i in range(nb)},
        compiler_params=pltpu.CompilerParams(has_side_effects=EFFECT),
    )(*[pltpu.with_memory_space_constraint(b, pltpu.HBM) for b in bufs])
    sems = [(out[2 * g], out[2 * g + 1]) for g in range(ng)]
    return sems, list(out[2 * ng:2 * ng + nb]), out[-1]


def _wait_copies(bufs, sems, plan, after, *, name):
    nb = len(bufs)

    def body(*refs):
        buf_refs = refs[:nb]
        for cp in _copies(plan, buf_refs, refs[nb], refs[nb + 1]):
            cp.wait_send()
            cp.wait_recv()

    out = pl.pallas_call(
        body, name=name, in_specs=[HBM] * nb + [SEM, SEM, ANY], out_specs=tuple([HBM] * nb),
        out_shape=tuple(pltpu.HBM(b.shape, b.dtype) for b in bufs),
        input_output_aliases={i: i for i in range(nb)},
        compiler_params=pltpu.CompilerParams(has_side_effects=EFFECT),
    )(*bufs, sems[0], sems[1], after)
    return list(out)


def _half(ref, c):
    h = ref.shape[0] // 2
    return ref.at[pl.ds(c * h, h)]


def _ici_gather_plan(pairs):
    def plan(refs):
        x, y, c = _coords()
        me = 2 * x + y
        out = []
        for s, d in pairs:
            for cx, cy in [(1 - x, y), (x, 1 - y), (1 - x, 1 - y)]:
                out.append((_half(refs[s], c), _half(refs[d].at[me], c), (cx, cy, c)))
            out.append((refs[s], refs[d].at[me], (x, y, 1 - c)))
        return out
    return plan, 4 * len(pairs)


def _d2d_forward_plan(lands):
    def plan(refs):
        x, y, c = _coords()
        out = []
        for d in lands:
            for cx, cy in [(1 - x, y), (x, 1 - y), (1 - x, 1 - y)]:
                got = _half(refs[d].at[2 * cx + cy], c)
                out.append((got, got, (x, y, 1 - c)))
        return out
    return plan, 3 * len(lands)


def _swap_half_plan(pairs):
    def plan(refs):
        x, y, c = _coords()
        out = []
        for s, d in pairs:
            h = refs[d].shape[1]
            out.append((refs[s].at[:, pl.ds((1 - c) * h, h)], refs[d], (x, y, 1 - c)))
        return out
    return plan, len(pairs)


def _scatter_plan(pairs):
    def plan(refs):
        x, y, c = _coords()
        me = 2 * x + y
        out = []
        for s, d in pairs:
            for cx, cy in [(1 - x, y), (x, 1 - y), (1 - x, 1 - y)]:
                out.append((refs[s].at[2 * cx + cy], refs[d].at[me], (cx, cy, c)))
        return out
    return plan, 3 * len(pairs)


def _swap_total_plan(pairs):
    def plan(refs):
        x, y, c = _coords()
        return [(refs[s], refs[d], (x, y, 1 - c)) for s, d in pairs]
    return plan, len(pairs)


def _gather_all(block, *, name, after=()):
    after = tuple(after)

    def body(src, *rest):
        out, send_sems, recv_sems, local_sem = rest[len(after):]
        x, y, c = _coords()
        me = 4 * x + 2 * y + c
        flips = [(fx, fy, fc) for fx in (0, 1) for fy in (0, 1) for fc in (0, 1)][1:]
        mine = pltpu.make_async_copy(src, out.at[me], local_sem)
        mine.start()
        peers = [(x ^ fx, y ^ fy, c ^ fc) for fx, fy, fc in flips]
        cps = [pltpu.make_async_remote_copy(src_ref=src, dst_ref=out.at[me], send_sem=send_sems.at[k],
                                            recv_sem=recv_sems.at[k], device_id=peer, device_id_type=MESH)
               for k, peer in enumerate(peers)]
        for cp in cps:
            cp.start()
        for k, (px, py, pc) in enumerate(peers):
            slot = out.at[4 * px + 2 * py + pc]
            pltpu.make_async_remote_copy(src_ref=slot, dst_ref=slot, send_sem=send_sems.at[k], recv_sem=recv_sems.at[k],
                                         device_id=(px, py, pc), device_id_type=MESH).wait_recv()
        for cp in cps:
            cp.wait_send()
        mine.wait()

    return pl.pallas_call(
        body, name=name, in_specs=[ANY] * (1 + len(after)), out_specs=ANY,
        out_shape=jax.ShapeDtypeStruct((8,) + block.shape, block.dtype),
        scratch_shapes=[pltpu.SemaphoreType.DMA((7,)), pltpu.SemaphoreType.DMA((7,)), pltpu.SemaphoreType.DMA],
    )(block, *after)


def _cols(o):
    return jnp.transpose(o, (1, 0, 2)).reshape(o.shape[1], -1)


def _uncols(full):
    return jnp.transpose(full.reshape(full.shape[0], 4, -1), (1, 0, 2))


def _rope_pad(x1, x2):
    z = jnp.zeros_like(x1)
    return jnp.concatenate([x1, z, x2, z], axis=-1)


def _head_pad(w, heads):
    r = w.reshape(w.shape[0], heads, QK_HEAD)
    half = QK_ROPE // 2
    out = jnp.concatenate([r[..., :QK_NOPE], _rope_pad(r[..., QK_NOPE:QK_NOPE + half], r[..., QK_NOPE + half:])], axis=-1)
    return out.reshape(w.shape[0], heads * HEAD_PAD)


def _head_unpad(w, heads):
    r = w.reshape(w.shape[0], heads, HEAD_PAD)
    half = QK_ROPE // 2
    out = jnp.concatenate([r[..., :QK_NOPE], r[..., QK_NOPE:QK_NOPE + half],
                           r[..., QK_NOPE + 2 * half:QK_NOPE + 3 * half]], axis=-1)
    return out.reshape(w.shape[0], heads * QK_HEAD)


class _Dims:
    def __init__(self, d, seq):
        self.d = d
        self.seq = seq
        self.t_real = N_META + seq
        self.t = -(-self.t_real // LANES) * LANES
        self.dc = d // 2
        self.dp = d // 2
        self.pg = self.dp // len(POOL_WINDOWS)
        self.heads = d // 128
        self.dff = 4 * d
        self.a_end = 3 * self.dc
        self.q_end = self.a_end + Q_LORA
        self.kv_end = self.q_end + KV_LORA
        self.kr_end = self.kv_end + QK_ROPE
        self.pool_end = self.kr_end + self.dp
        self.d_in = self.pool_end + 3 * d
        self.r_pool = 3 * self.dc
        self.r_q = self.r_pool + self.dp
        self.r_kv = self.r_q + Q_LORA
        self.r_kr = self.r_kv + KV_LORA
        self.r_width = self.r_kr + HEAD_PAD


def _split_cols(a):
    return jnp.moveaxis(a.reshape(a.shape[:-1] + (2, a.shape[-1] // 2)), -2, -3)


def _join_cols(a):
    a = jnp.moveaxis(a, -3, -2)
    return a.reshape(a.shape[:-2] + (a.shape[-2] * a.shape[-1],))


def _in_weights(dm, pieces):
    w_t = _join_cols(pieces).reshape(dm.d_in, dm.d)
    half = QK_ROPE // 2
    kr = w_t[dm.kv_end:dm.kr_end]
    zeros = jnp.zeros((half, dm.d), BF16)
    kr_p = jnp.concatenate([kr[:half], zeros, kr[half:], zeros, jnp.zeros((HEAD_PAD - LANES, dm.d), BF16)], axis=0)
    return dict(
        wg_t=w_t[dm.pool_end:],
        wr_t=jnp.concatenate([w_t[:dm.a_end], w_t[dm.kr_end:dm.pool_end], w_t[dm.a_end:dm.kv_end], kr_p], axis=0))


def _other_weights(dm, g):
    out = {}
    if "w_ukv" in g:
        w_ukv = _cols(g["w_ukv"]).reshape(KV_LORA, dm.heads, QK_NOPE + V_HEAD)
        out["wkn"] = w_ukv[:, :, :QK_NOPE].reshape(KV_LORA, dm.heads * QK_NOPE)
        out["wv"] = w_ukv[:, :, QK_NOPE:].reshape(KV_LORA, dm.heads * V_HEAD)
    if "w_uq" in g:
        out["wuq"] = _head_pad(_cols(g["w_uq"]), dm.heads)
    if "pool_w" in g:
        out["wp"] = jnp.transpose(g["pool_w"], (1, 0, 2, 3)).reshape(len(POOL_WINDOWS), dm.pg, dm.pg)
    for name, key in (("w_branch_a", "wba"), ("w_branch_c", "wbc"), ("w_up", "wup")):
        if name in g:
            out[key] = _cols(g[name])
    for name, key in (("w_branch_b", "wbb"), ("w_o", "wo"), ("w_down", "wdn")):
        if name in g:
            out[key] = g[name].reshape(-1, dm.d)
    return out


def _small_weights(small):
    return dict(
        conv_w=small["conv_w"],
        attn_norm=small["attn_norm"][None], mlp_norm=small["mlp_norm"][None],
        q_lat_norm=small["q_lat_norm"][None], kv_lat_norm=small["kv_lat_norm"][None],
        q_norm=_head_pad(small["q_norm"][None], 1), k_norm=_head_pad(small["k_norm"][None], 1),
        pool_scale=small["pool_scale"][None],
    )


def _grad_piece(dm, dw, name):
    half = QK_ROPE // 2
    rows = lambda a: a.reshape((4, a.shape[0] // 4) + a.shape[1:])
    if name == "w_in":
        dwr, dwg = dw["wr_t"], dw["wg_t"]
        d_t = jnp.concatenate([
            dwr[:dm.r_pool], dwr[dm.r_q:dm.r_kr], dwr[dm.r_kr:dm.r_kr + half],
            dwr[dm.r_kr + 2 * half:dm.r_kr + 3 * half], dwr[dm.r_pool:dm.r_q], dwg], axis=0)
        out = _split_cols(rows(d_t))
    elif name == "w_ukv":
        out = _uncols(jnp.concatenate([dw["wkn"].reshape(KV_LORA, dm.heads, QK_NOPE),
                                       dw["wv"].reshape(KV_LORA, dm.heads, V_HEAD)], axis=-1).reshape(KV_LORA, -1))
    elif name == "w_uq":
        out = _uncols(_head_unpad(dw["wuq"], dm.heads))
    elif name == "pool_w":
        out = jnp.transpose(dw["wp"].reshape(len(POOL_WINDOWS), 4, dm.pg // 4, dm.pg), (1, 0, 2, 3))
    elif name in ("w_branch_a", "w_branch_c", "w_up"):
        out = dw[{"w_branch_a": "wba", "w_branch_c": "wbc", "w_up": "wup"}[name]]
    else:
        out = rows(dw[{"w_branch_b": "wbb", "w_o": "wo", "w_down": "wdn"}[name]])
    return out.astype(BF16)


def _layer_fwd(dm, W, x, cos_t, sin_t, tag, more=None, h=None):
    n = lambda s: f"{s}_{tag}"
    if h is None:
        h = _rms_fwd(x, W["attn_norm"], name=n("attn_norm"))
    gl = _mm(h, W["wg_t"], name=n("proj_gates"), tb=True)
    rest = _mm(h, W["wr_t"], name=n("proj_rest"), tb=True)
    if more is not None:
        W.update(more("after_proj", rest))
    y_a = _conv_fwd(rest, W["conv_w"], name=n("conv"), dc=dm.dc)
    y_c = _pool_fwd(rest, W["wp"], W["pool_scale"], name=n("pool"), seg0=dm.r_pool // dm.pg, pg=dm.pg)
    q_lat = _rms_fwd(rest, W["q_lat_norm"], name=n("q_lat_norm"), width=Q_LORA, seg=dm.r_q // Q_LORA)
    kv_lat = _rms_fwd(rest, W["kv_lat_norm"], name=n("kv_lat_norm"), width=KV_LORA, seg=dm.r_kv // KV_LORA)
    q_raw = _mm(q_lat, W["wuq"], name=n("up_q"))
    k_nope = _mm(kv_lat, W["wkn"], name=n("up_k"))
    v = _mm(kv_lat, W["wv"], name=n("up_v"), out_dtype=BF16)
    q, k = _qk_fwd(q_raw, k_nope, rest, cos_t, sin_t, W["q_norm"], W["k_norm"], name=n("qk_norm_rope"),
                   heads=dm.heads, kr_seg=dm.r_kr // HEAD_PAD)
    if more is not None:
        W.update(more("after_qk", q))
    y_b, lse = _flash_fwd(q, k, v, name=n("attention"), heads=dm.heads)
    pa = _mm(y_a, W["wba"], name=n("branch_a"))
    pb = _mm(y_b, W["wbb"], name=n("branch_b"))
    pc = _mm(y_c, W["wbc"], name=n("branch_c"))
    merged = _merge_fwd(gl, pa, pb, pc, name=n("merge"), d=dm.d)
    x1 = _mm(merged, W["wo"], name=n("out_proj"), add=x)
    h2 = _rms_fwd(x1, W["mlp_norm"], name=n("mlp_norm"))
    up, act = _mm(h2, W["wup"], name=n("mlp_up"), epi="relu2")
    x2 = _mm(act, W["wdn"], name=n("mlp_down"), add=x1, tm=704, tk=4096)
    saved = dict(x=x, h=h, gl=gl, rest=rest, y_a=y_a, y_c=y_c, q_lat=q_lat, kv_lat=kv_lat, q_raw=q_raw, k_nope=k_nope,
                 v=v, q=q, k=k, y_b=y_b, lse=lse, pa=pa, pb=pb, pc=pc, merged=merged, x1=x1, h2=h2, up=up, act=act)
    return x2, saved


def _layer_bwd(dm, W, S, dx2, dx2_b, cos_t, sin_t, tag, hook=None):
    n = lambda s: f"{s}_{tag}"
    dw, ds = {}, {}
    if hook is None:
        hook = lambda point, t, dw_so_far: ()
    dup = _mm(dx2_b, W["wdn"], name=n("d_mlp_down"), tb=True, aux=S["up"], epi="drelu2", out_dtype=BF16,
              after=hook("start", dx2, dw))
    dw["wdn"] = _mm(S["act"], dx2_b, name=n("dw_mlp_down"), ta=True, tm=512, out_dtype=BF16)
    dh2 = _mm(dup, W["wup"], name=n("d_mlp_up"), tb=True, tm=704, tk=4096)
    dw["wup"] = _mm(S["h2"], dup, name=n("dw_mlp_up"), ta=True, tm=512, out_dtype=BF16, pieces=4)
    dx1, dx1_b, ds["mlp_norm"] = _rms_bwd(dh2, S["x1"], W["mlp_norm"], name=n("d_mlp_norm"), res=dx2, bf16_copy=True)
    dmerged = _mm(dx1_b, W["wo"], name=n("d_out_proj"), tb=True, after=hook("after_mlp", dx1, dw))
    dw["wo"] = _mm(S["merged"], dx1_b, name=n("dw_out_proj"), ta=True, tm=512, out_dtype=BF16)
    dpa, dpb, dpc, dg0, dg1, dg2 = _merge_bwd(dmerged, S["gl"], S["pa"], S["pb"], S["pc"], name=n("d_merge"), d=dm.d)
    dgl = jnp.concatenate([dg0, dg1, dg2], axis=1)
    dy_a = _mm(dpa, W["wba"], name=n("d_branch_a"), tb=True)
    dw["wba"] = _mm(S["y_a"], dpa, name=n("dw_branch_a"), ta=True, tm=512, out_dtype=BF16, pieces=4)
    dy_b = _mm(dpb, W["wbb"], name=n("d_branch_b"), tb=True, out_dtype=BF16)
    dw["wbb"] = _mm(S["y_b"], dpb, name=n("dw_branch_b"), ta=True, tm=512, out_dtype=BF16)
    dy_c = _mm(dpc, W["wbc"], name=n("d_branch_c"), tb=True)
    dw["wbc"] = _mm(S["y_c"], dpc, name=n("dw_branch_c"), ta=True, tm=512, out_dtype=BF16, pieces=4)
    dq, dk, dv = _flash_bwd(S["q"], S["k"], S["v"], S["y_b"], dy_b, S["lse"], name=n("d_attention"), heads=dm.heads)
    after_attention = hook("after_attention", dq, dw)
    dq_raw, dk_nope, dk_rope, dgq, dgk = _qk_bwd(
        dq, dk, S["q_raw"], S["k_nope"], S["rest"], cos_t, sin_t, W["q_norm"], W["k_norm"], name=n("d_qk_norm_rope"),
        heads=dm.heads, kr_seg=dm.r_kr // HEAD_PAD)
    ds["q_norm"] = _head_unpad(dgq, 1)
    ds["k_norm"] = _head_unpad(dgk, 1)
    dkv_v = _mm(dv, W["wv"], name=n("d_up_v"), tb=True, after=after_attention)
    dq_lat_n = _mm(dq_raw, W["wuq"], name=n("d_up_q"), tb=True, after=hook("after_qk", dq_raw, dw))
    dw["wuq"] = _mm(S["q_lat"], dq_raw, name=n("dw_up_q"), ta=True, tm=512)
    dkv_lat_n = _mm(dk_nope, W["wkn"], name=n("d_up_k"), tb=True, add=dkv_v)
    dw["wkn"] = _mm(S["kv_lat"], dk_nope, name=n("dw_up_k"), ta=True, tm=512)
    dw["wv"] = _mm(S["kv_lat"], dv, name=n("dw_up_v"), ta=True, tm=512)
    dq_lat, ds["q_lat_norm"] = _rms_bwd(dq_lat_n, S["rest"], W["q_lat_norm"], name=n("d_q_lat_norm"), width=Q_LORA,
                                        seg=dm.r_q // Q_LORA, out_dtype=BF16)
    dkv_lat, ds["kv_lat_norm"] = _rms_bwd(dkv_lat_n, S["rest"], W["kv_lat_norm"], name=n("d_kv_lat_norm"), width=KV_LORA,
                                          seg=dm.r_kv // KV_LORA, out_dtype=BF16)
    du, db, dc, ds["conv_w"] = _conv_bwd(S["rest"], W["conv_w"], dy_a, name=n("d_conv"), dc=dm.dc)
    dpool, dw["wp"], ds["pool_scale"] = _pool_bwd(S["rest"], W["wp"], W["pool_scale"], dy_c, name=n("d_pool"),
                                                  seg0=dm.r_pool // dm.pg, pg=dm.pg)
    drest = jnp.concatenate([du, db, dc, dpool, dq_lat, dkv_lat, dk_rope], axis=1)
    dw["wg_t"] = _mm(dgl, S["h"], name=n("dw_proj_gates"), ta=True, tm=512, out_dtype=BF16)
    dw["wr_t"] = _mm(drest, S["h"], name=n("dw_proj_rest"), ta=True, tm=512, out_dtype=BF16)
    dh_g = _mm(dgl, W["wg_t"], name=n("d_proj_gates"), tm=704, tk=3072, after=hook("after_dw_in", dw["wr_t"], dw))
    dh = _mm(drest, W["wr_t"], name=n("d_proj_rest"), add=dh_g, tm=704, tk=2688, after=hook("after_dh_gates", dh_g, dw))
    dx, dx_b, ds["attn_norm"] = _rms_bwd(dh, S["x"], W["attn_norm"], name=n("d_attn_norm"), res=dx1, bf16_copy=True)
    return dx, dx_b, dw, ds


BIG = ("w_in", "w_uq", "w_ukv", "pool_w", "w_branch_a", "w_branch_b", "w_branch_c", "w_o", "w_up", "w_down")
REPLICATED = ("attn_norm", "q_lat_norm", "kv_lat_norm", "q_norm", "k_norm", "pool_scale", "mlp_norm")
WEIGHTS = ("meta_tokens", "attn_norm", "w_in", "conv_w", "q_lat_norm", "kv_lat_norm", "w_uq", "w_ukv", "q_norm",
           "k_norm", "pool_w", "pool_scale", "w_branch_a", "w_branch_b", "w_branch_c", "w_o", "mlp_norm", "w_up",
           "w_down")


def _pack(arrays):
    flat = jnp.concatenate([a.reshape(-1).astype(F32) for a in arrays])
    pad = (-flat.shape[0]) % (8 * LANES)
    return jnp.pad(flat, (0, pad)).reshape(-1, LANES)


def _unpack(flat, shapes):
    out, pos = [], 0
    flat = flat.reshape(-1)
    for shp in shapes:
        size = math.prod(shp)
        out.append(flat[pos:pos + size].reshape(shp))
        pos += size
    return out


def _update(w, g, m, v, name):
    shp = w.shape
    to2 = lambda a: a.reshape(-1, shp[-1])
    delta, nm, nv = _adamw(to2(w), to2(g), to2(m), to2(v), name=name)
    return delta.reshape(shp), nm.reshape(shp), nv.reshape(shp)


def _step(args):
    x = args["x"][0]
    seq, d = x.shape
    dm = _Dims(d, seq)
    xi, yi, ci = _coords()
    chip = 2 * xi + yi

    small_w = _gather_all(_pack([args["conv_w"], args["meta_tokens"]]), name="gather_small_weights")
    args = dict(args)
    for p in ("", "m_", "v_"):
        args[p + "w_in"] = jnp.swapaxes(args[p + "w_in"], 1, 2)
    order = [(k, l) for l in range(2) for k in BIG]
    shards = {n: args[n[0]][n[1]].astype(BF16) for n in order}
    for l in range(2):
        shards[("w_in", l)] = _split_cols(shards[("w_in", l)])
    small_w, shards[order[0]] = lax.optimization_barrier((small_w, shards[order[0]]))
    lands = {n: lax.empty((4,) + shards[n].shape, BF16) for n in order}
    last = ("w_up", "w_down")
    group_names = [[("w_in", 0)], [(k, 0) for k in BIG[1:] if k not in last], [(k, 0) for k in last],
                   [(k, 1) for k in BIG]]
    first, others = order[0], order[1:]
    sems, thru, token = _start_copies([shards[first], lands[first]], [_ici_gather_plan([(0, 1)])],
                                      name="start_gather_ici_first")
    shards[first], lands[first] = thru
    at = {n: i for i, n in enumerate(others)}
    sems_b, thru, token_b = _start_copies(
        [shards[n] for n in others] + [lands[n] for n in others] + [token],
        [_ici_gather_plan([(at[n], len(others) + at[n]) for n in g]) for g in group_names[1:]], name="start_gather_ici")
    sems = sems + sems_b
    for i, n in enumerate(others):
        shards[n], lands[n] = thru[i], thru[len(others) + i]

    def finish_gather(g, after, tag):
        names = group_names[g]
        k = len(names)
        plan, _ = _ici_gather_plan([(i, k + i) for i in range(k)])
        got = _wait_copies([shards[n] for n in names] + [lands[n] for n in names], sems[g], plan, after,
                           name=f"wait_gather_ici_{tag}")
        for i, n in enumerate(names):
            shards[n] = got[i]
        fwd = _d2d_forward_plan(list(range(k)))
        sems2, bufs2, tok2 = _start_copies(got[k:], [fwd], name=f"start_gather_d2d_{tag}")
        return names, bufs2, sems2[0], fwd[0], tok2

    def land_gather(pending, after, tag):
        names, bufs2, sems2, plan, tok2 = pending
        done = _wait_copies(bufs2, sems2, plan, tok2 if after is None else after, name=f"wait_gather_d2d_{tag}")
        return {n[0]: buf for n, buf in zip(names, done)}

    conv_shape, meta_shape = args["conv_w"].shape, args["meta_tokens"].shape
    per_chip = [_unpack(small_w[2 * j], [conv_shape, meta_shape]) for j in range(4)]
    conv_full = jnp.concatenate([p[0] for p in per_chip], axis=-1)
    meta_full = jnp.concatenate([p[1] for p in per_chip], axis=-1)

    layers = []
    for l in range(2):
        small = {k: args[k][l] for k in REPLICATED}
        small["conv_w"] = conv_full[l]
        layers.append(_small_weights(small))

    pos = jnp.arange(dm.t, dtype=F32)
    inv = ROPE_THETA ** (-jnp.arange(0, QK_ROPE, 2, dtype=F32) / QK_ROPE)
    ang = pos[:, None] * inv[None, :]
    cos_t = _rope_pad(jnp.cos(ang), jnp.cos(ang))
    sin_t = _rope_pad(-jnp.sin(ang), jnp.sin(ang))
    tail = jnp.zeros((dm.t - dm.t_real, d), F32)
    h0 = jnp.concatenate([meta_full, x, tail], axis=0)
    target = jnp.concatenate([jnp.zeros((N_META, d), F32), args["loss_target"][0], tail], axis=0)

    h_first = _rms_fwd(h0, layers[0]["attn_norm"], name="attn_norm_l0", after=(token, token_b))
    layers[0].update(_in_weights(dm, land_gather(finish_gather(0, h_first, "l0_in"), None, "l0_in")["w_in"]))
    def rest_of_layer0(point, after):
        g, tag = (1, "l0_mid") if point == "after_proj" else (2, "l0_mlp")
        return _other_weights(dm, land_gather(finish_gather(g, after, tag), None, tag))

    h1, saved0 = _layer_fwd(dm, layers[0], h0, cos_t, sin_t, "l0", more=rest_of_layer0, h=h_first)
    g1 = land_gather(finish_gather(3, saved0["y_b"], "l1"), h1, "l1")
    layers[1].update(_in_weights(dm, g1["w_in"]))
    layers[1].update(_other_weights(dm, g1))
    h2, saved1 = _layer_fwd(dm, layers[1], h1, cos_t, sin_t, "l1")
    sq, dy, dy_b = _loss(h2, target, name="loss_head", first=N_META, last=dm.t_real)
    loss = lax.psum(0.5 / d * sq[0, 0], ("x", "y", "c"))
    core, chip_flags = _one_hot(ci, 2), _one_hot(chip, 4)

    class Reduce:
        def __init__(self, names, dw, tag):
            self.names, self.tag, self.nb = names, tag, len(names)
            self.idx = [(i, self.nb + i) for i in range(self.nb)]
            parts = [_as3d(_grad_piece(dm, dw, k)) for k in names]
            recv = [lax.empty((4, p.shape[1] // 2, p.shape[2]), BF16) for p in parts]
            self.plan = _swap_half_plan(self.idx)
            self.sems, self.bufs, self.token = _start_copies(parts + recv, [self.plan], name=f"start_swap_{tag}")

        def _land(self, after, what):
            return _wait_copies(self.bufs, self.sems[0], self.plan[0], self.token if after is None else after,
                                name=f"wait_{what}_{self.tag}")

        def scatter(self, after=None):
            got = self._land(after, "swap")
            pairs = [_pair_sum(got[i], got[j], core, name=f"pair_sum_{k}_{self.tag}")
                     for (i, j), k in zip(self.idx, self.names)]
            self.plan = _scatter_plan(self.idx)
            self.sems, self.bufs, self.token = _start_copies(pairs + [lax.empty(p.shape, BF16) for p in pairs],
                                                             [self.plan], name=f"start_scatter_{self.tag}")
            return self.token

        def totals(self, after=None):
            got = self._land(after, "scatter")
            sums = [_chip_sum(got[i], got[j], chip_flags, name=f"chip_sum_{k}_{self.tag}")
                    for (i, j), k in zip(self.idx, self.names)]
            self.plan = _swap_total_plan(self.idx)
            self.sems, self.bufs, self.token = _start_copies(sums + [lax.empty(t.shape, F32) for t in sums],
                                                             [self.plan], name=f"start_swap_total_{self.tag}")
            return self.token

        def finish(self, after=None):
            got = self._land(after, "swap_total")
            return {k: (got[i], got[j]) for (i, j), k in zip(self.idx, self.names)}

    dh1, dh1_b, dw1, ds1 = _layer_bwd(dm, layers[1], saved1, dy, dy_b, cos_t, sin_t, "l1",
                               hook=lambda point, t, dw: (loss.reshape(1, 1),) if point == "start" else ())
    early = ("w_down", "w_up", "w_o", "w_branch_a", "w_branch_b", "w_branch_c")
    late = tuple(k for k in BIG if k not in early)
    stage = {}

    def during_layer0(point, t, dw):
        if point == "start":
            stage["l1"] = Reduce(BIG, dw1, "l1")
            return (stage["l1"].token,)
        if point == "after_mlp":
            return (stage["l1"].scatter(after=t),)
        if point == "after_attention":
            tok = stage["l1"].totals(after=t)
            stage["l0a"] = Reduce(early, dw, "l0a")
            return (tok, stage["l0a"].token)
        if point == "after_qk":
            stage["red1"] = stage["l1"].finish(after=t)
            return (stage["l0a"].scatter(after=t),)
        if point == "after_dw_in":
            tok = stage["l0a"].totals(after=t)
            stage["l0b"] = Reduce(late, dw, "l0b")
            return (tok, stage["l0b"].token)
        return (stage["l0b"].scatter(after=t),)

    dh0, _, dw0, ds0 = _layer_bwd(dm, layers[0], saved0, dh1, dh1_b, cos_t, sin_t, "l0", hook=during_layer0)
    grad_x = dh0[N_META:dm.t_real][None]
    red1 = stage["red1"]
    grads, delta, new_m, new_v = {}, {}, {}, {}

    def adamw_big(k, layer, red, prev, after):
        shp = args[k].shape
        wmv = [args[p + k].reshape(2, -1, shp[-1]) for p in ("", "m_", "v_")]
        return _adamw_layer(*wmv, *red[k], core, layer, prev, name=f"adamw_{k}_l{layer}", col_halves=k == "w_in",
                            after=after)

    def keep(k, out):
        shp = args[k].shape
        out = [o.reshape(shp) for o in out]
        grads[k], delta[k], new_m[k], new_v[k] = [jnp.swapaxes(o, 1, 2) for o in out] if k == "w_in" else out

    half_done = {}
    pin = dh0
    for k in BIG:
        half_done[k] = adamw_big(k, 1, red1, None, (pin,))
        pin = half_done[k][0]
    red0a = stage["l0a"].finish(after=pin)
    for k in early:
        out = adamw_big(k, 0, red0a, half_done[k], ())
        keep(k, out)
        pin = out[0]

    small_names = REPLICATED + ("conv_w",)
    small_parts = [jnp.stack([ds0[k].reshape(ds0[k].shape[-2:] if k == "conv_w" else (-1,)),
                              ds1[k].reshape(ds1[k].shape[-2:] if k == "conv_w" else (-1,))]) for k in small_names]
    small_parts.append(dh0[:N_META])
    small_all = _gather_all(_pack(small_parts), name="gather_small_grads", after=(pin,))
    small_sum = _sum_stack(small_all, name="sum_small_grads", out_dtype=F32)
    small_g = dict(zip(small_names + ("meta_tokens",), _unpack(small_sum, [p.shape for p in small_parts])))
    for k in REPLICATED:
        grads[k] = small_g[k]
    dcw = conv_shape[-1]
    grads["conv_w"] = lax.dynamic_slice_in_dim(small_g["conv_w"], chip * dcw, dcw, axis=2)
    dmeta = meta_shape[-1]
    grads["meta_tokens"] = lax.dynamic_slice_in_dim(small_g["meta_tokens"], chip * dmeta, dmeta, axis=1)

    stage["l0b"].totals(after=small_sum)
    red0b = stage["l0b"].finish()
    for k in late:
        keep(k, adamw_big(k, 0, red0b, half_done[k], ()))
    for k in WEIGHTS:
        if k not in BIG:
            grads[k] = grads[k].reshape(args[k].shape)
            delta[k], new_m[k], new_v[k] = _update(args[k], grads[k], args["m_" + k], args["v_" + k], f"adamw_{k}")
    return (loss, grad_x, *[grads[k] for k in WEIGHTS], *[delta[k] for k in WEIGHTS],
            *[new_m[k] for k in WEIGHTS], *[new_v[k] for k in WEIGHTS])


def kernel(x, meta_tokens, attn_norm, w_in, conv_w, q_lat_norm, kv_lat_norm, w_uq, w_ukv, q_norm, k_norm, pool_w, pool_scale, w_branch_a, w_branch_b, w_branch_c, w_o, mlp_norm, w_up, w_down, loss_target, m_meta_tokens, m_attn_norm, m_w_in, m_conv_w, m_q_lat_norm, m_kv_lat_norm, m_w_uq, m_w_ukv, m_q_norm, m_k_norm, m_pool_w, m_pool_scale, m_w_branch_a, m_w_branch_b, m_w_branch_c, m_w_o, m_mlp_norm, m_w_up, m_w_down, v_meta_tokens, v_attn_norm, v_w_in, v_conv_w, v_q_lat_norm, v_kv_lat_norm, v_w_uq, v_w_ukv, v_q_norm, v_k_norm, v_pool_w, v_pool_scale, v_w_branch_a, v_w_branch_b, v_w_branch_c, v_w_o, v_mlp_norm, v_w_up, v_w_down):
    return _step(dict(locals()))
```

```python
import functools
import math

import jax
import jax.numpy as jnp
from jax import lax
from jax.experimental import pallas as pl
from jax.experimental.pallas import tpu as pltpu

F32 = jnp.float32
BF16 = jnp.bfloat16
MESH = pl.DeviceIdType.MESH

EPS = 1e-6
N_META = 16
QK_NOPE = 128
QK_ROPE = 64
QK_HEAD = QK_NOPE + QK_ROPE
V_HEAD = 128
HEAD_PAD = 256
Q_LORA = 512
KV_LORA = 512
ROPE_THETA = 10000.0
POOL_WINDOWS = (2, 4, 8, 16)
HALO = 16
LANES = 128
ADAM_LR = 0.001
ADAM_B1 = 0.9
ADAM_B2 = 0.999
ADAM_EPS = 1e-08
ADAM_WD = 0.01
ADAM_STEP = 10
VMEM_LIMIT = 52 * 1024 * 1024
NEG = -1e30
ATTN_SCALE = QK_HEAD ** -0.5
LOG2_E = 1.4426950408889634
Q_FOLD = ATTN_SCALE * LOG2_E


def _tile(n, target, mult=LANES):
    best = None
    for t in range(mult, min(n, target) + 1, mult):
        if n % t == 0:
            best = t
    return n if best is None else best


def _params(sem=None):
    return pltpu.CompilerParams(dimension_semantics=sem, vmem_limit_bytes=VMEM_LIMIT)


def _mm(a, b, *, name, ta=False, tb=False, add=None, aux=None, epi=None, out_dtype=F32,
        tm=1056, tn=1024, tk=None, after=(), pieces=None):
    if ta:
        K, M = a.shape
    else:
        M, K = a.shape
    if tb:
        N, kb = b.shape
    else:
        kb, N = b.shape
    assert K == kb, (a.shape, b.shape, ta, tb)
    tm = _tile(M, tm, LANES if ta else 16)
    tn = _tile(N if pieces is None else N // pieces, tn, LANES)
    tk = K if tk is None else _tile(K, tk, LANES if (not ta or tb) else 16)
    nk = K // tk
    a_bytes, b_bytes = a.size * a.dtype.itemsize, b.size * b.dtype.itemsize
    j_outer = nk == 1 and a_bytes * (N // tn) + b_bytes < a_bytes + b_bytes * (M // tm)
    grid = (N // tn, M // tm, nk) if j_outer else (M // tm, N // tn, nk)
    row = (lambda g0, g1: g1) if j_outer else (lambda g0, g1: g0)
    col = (lambda g0, g1: g0) if j_outer else (lambda g0, g1: g1)

    if ta:
        a_spec = pl.BlockSpec((tk, tm), lambda g0, g1, k: (k, row(g0, g1)))
    else:
        a_spec = pl.BlockSpec((tm, tk), lambda g0, g1, k: (row(g0, g1), k))
    if tb:
        b_spec = pl.BlockSpec((tn, tk), lambda g0, g1, k: (col(g0, g1), k))
    else:
        b_spec = pl.BlockSpec((tk, tn), lambda g0, g1, k: (k, col(g0, g1)))
    o_spec = pl.BlockSpec((tm, tn), lambda g0, g1, k: (row(g0, g1), col(g0, g1)))
    per = None if pieces is None else N // pieces // tn
    in_specs = [a_spec, b_spec]
    operands = [a, b]
    if add is not None:
        in_specs.append(o_spec)
        operands.append(add)
    if aux is not None:
        in_specs.append(o_spec)
        operands.append(aux)
    after = tuple(after)
    in_specs += [pl.BlockSpec(memory_space=pl.ANY)] * len(after)
    operands += list(after)
    if epi == "relu2":
        out_shape = (jax.ShapeDtypeStruct((M, N), BF16), jax.ShapeDtypeStruct((M, N), BF16))
        out_specs = (o_spec, o_spec)
    elif pieces is not None:
        out_shape = jax.ShapeDtypeStruct((pieces, M, N // pieces), out_dtype)
        out_specs = pl.BlockSpec((1, tm, tn), lambda g0, g1, k: (col(g0, g1) // per, row(g0, g1), col(g0, g1) % per))
    else:
        out_shape = jax.ShapeDtypeStruct((M, N), out_dtype)
        out_specs = o_spec
    dims =(((0 if ta else 1,), (1 if tb else 0,)), ((), ()))
    has_add, has_aux = add is not None, aux is not None

    def body(*refs):
        a_ref, b_ref = refs[0], refs[1]
        pos = 2
        add_ref = aux_ref = None
        if has_add:
            add_ref = refs[pos]
            pos += 1
        if has_aux:
            aux_ref = refs[pos]
            pos += 1
        pos += len(after)
        n_out = 2 if epi == "relu2" else 1
        out_refs = refs[pos:pos + n_out]
        acc_ref = refs[pos + n_out] if nk > 1 else None

        part = lax.dot_general(a_ref[...].astype(BF16), b_ref[...].astype(BF16), dims,
                               preferred_element_type=F32)

        def finish(acc):
            if has_add:
                acc = acc + add_ref[...].astype(F32)
            if epi == "relu2":
                r = jnp.maximum(acc, 0.0)
                out_refs[0][...] = acc.astype(BF16)
                out_refs[1][...] = (r * r).astype(BF16)
            elif epi == "drelu2":
                u = aux_ref[...].astype(F32)
                out_refs[0][...] = (acc * (2.0 * jnp.maximum(u, 0.0))).astype(out_dtype)
            else:
                out_refs[0][...] = acc.astype(out_dtype).reshape(out_refs[0].shape)

        if nk == 1:
            finish(part)
        else:
            k = pl.program_id(2)

            @pl.when(k == 0)
            def _():
                acc_ref[...] = part

            @pl.when(k > 0)
            def _():
                acc_ref[...] += part

            @pl.when(k == nk - 1)
            def _():
                finish(acc_ref[...])

    scratch = [pltpu.VMEM((tm, tn), F32)] if nk > 1 else []
    return pl.pallas_call(
        body, name=name, grid=grid, in_specs=in_specs, out_specs=out_specs, out_shape=out_shape,
        scratch_shapes=scratch, compiler_params=_params(("parallel", "parallel", "arbitrary")),
    )(*operands)


def _rms_fwd(x, g, *, name, width=None, seg=0, tm=384, after=()):
    T = x.shape[0]
    width = x.shape[1] if width is None else width
    tm = _tile(T, tm, 16)
    after = tuple(after)

    def body(x_ref, g_ref, *rest):
        xf = x_ref[...].astype(F32)
        r = lax.rsqrt(jnp.mean(xf * xf, axis=-1, keepdims=True) + EPS)
        rest[-1][...] = (xf * r * g_ref[...]).astype(BF16)

    return pl.pallas_call(
        body, name=name, grid=(T // tm,),
        in_specs=[pl.BlockSpec((tm, width), lambda i: (i, seg)), pl.BlockSpec((1, width), lambda i: (0, 0))]
        + [pl.BlockSpec(memory_space=pl.ANY)] * len(after),
        out_specs=pl.BlockSpec((tm, width), lambda i: (i, 0)),
        out_shape=jax.ShapeDtypeStruct((T, width), BF16),
        compiler_params=_params(("parallel",)),
    )(x, g, *after)


def _rms_bwd(dy, x, g, *, name, width=None, seg=0, res=None, out_dtype=F32, tm=384, bf16_copy=False):
    T = x.shape[0]
    width = x.shape[1] if width is None else width
    tm = _tile(T, tm, 16)
    has_res = res is not None

    def body(*refs):
        dy_ref, x_ref, g_ref = refs[:3]
        res_ref = refs[3] if has_res else None
        dx_ref, dg_ref = refs[4 if has_res else 3], refs[-1]
        xf = x_ref[...].astype(F32)
        dyf = dy_ref[...].astype(F32)
        r = lax.rsqrt(jnp.mean(xf * xf, axis=-1, keepdims=True) + EPS)
        xhat = xf * r
        dyh = dyf * g_ref[...]
        dx = r * (dyh - xhat * jnp.mean(dyh * xhat, axis=-1, keepdims=True))
        if has_res:
            dx = dx + res_ref[...].astype(F32)
        dx_ref[...] = dx.astype(out_dtype)
        if bf16_copy:
            refs[-2][...] = dx.astype(BF16)
        part = jnp.sum(dyf * xhat, axis=0, keepdims=True)

        @pl.when(pl.program_id(0) == 0)
        def _():
            dg_ref[...] = part

        @pl.when(pl.program_id(0) > 0)
        def _():
            dg_ref[...] += part

    row = pl.BlockSpec((tm, width), lambda i: (i, 0))
    in_specs = [row, pl.BlockSpec((tm, width), lambda i: (i, seg)), pl.BlockSpec((1, width), lambda i: (0, 0))]
    operands = [dy, x, g]
    if has_res:
        in_specs.append(row)
        operands.append(res)
    vec = pl.BlockSpec((1, width), lambda i: (0, 0))
    full = [jax.ShapeDtypeStruct((T, width), out_dtype)] + ([jax.ShapeDtypeStruct((T, width), BF16)] if bf16_copy else [])
    return pl.pallas_call(
        body, name=name, grid=(T // tm,), in_specs=in_specs,
        out_specs=tuple([row] * len(full) + [vec]),
        out_shape=tuple(full + [jax.ShapeDtypeStruct((1, width), F32)]),
        compiler_params=_params(("arbitrary",)),
    )(*operands)


def _down(ext, k):
    return pltpu.roll(ext, k, 0)


def _up(ext, k):
    return pltpu.roll(ext, ext.shape[0] - k, 0)


def _pre_halo(ref, r, R):
    start = pl.multiple_of(jnp.maximum(r * R - HALO, 0), 8)
    keep = (r > 0).astype(F32)
    return ref[pl.ds(start, HALO), :].astype(F32) * keep


def _post_halo(ref, r, R, n_chunks):
    start = pl.multiple_of(jnp.minimum(r * R + R, (n_chunks - 1) * R + R - HALO), 8)
    keep = (r < n_chunks - 1).astype(F32)
    return ref[pl.ds(start, HALO), :].astype(F32) * keep


def _chunk(ref, r, R):
    return ref[pl.ds(pl.multiple_of(r * R, 8), R), :].astype(F32)


def _conv_fwd(rest, conv_w, *, name, dc, tc=128, rows=1056):
    T = rest.shape[0]
    tc = _tile(dc, tc)
    nb = dc // tc
    R = _tile(T, rows, 16)
    n_chunks = T // R

    def body(u_ref, b_ref, c_ref, w_ref, y_ref):
        w0, w1, w2 = w_ref[0:1, :], w_ref[1:2, :], w_ref[2:3, :]

        def chunk(r, carry):
            cu = _chunk(c_ref, r, R) * _chunk(u_ref, r, R)
            ext = jnp.concatenate([_pre_halo(c_ref, r, R) * _pre_halo(u_ref, r, R), cu], axis=0)
            conv = w0 * _down(ext, 2)[HALO:] + w1 * _down(ext, 1)[HALO:] + w2 * cu
            y_ref[pl.ds(pl.multiple_of(r * R, 8), R), :] = (_chunk(b_ref, r, R) * conv).astype(BF16)
            return carry

        lax.fori_loop(0, n_chunks, chunk, 0)

    col = lambda off: pl.BlockSpec((T, tc), lambda j: (0, off * nb + j))
    return pl.pallas_call(
        body, name=name, grid=(nb,),
        in_specs=[col(0), col(1), col(2), pl.BlockSpec((3, tc), lambda j: (0, j))],
        out_specs=pl.BlockSpec((T, tc), lambda j: (0, j)),
        out_shape=jax.ShapeDtypeStruct((T, dc), BF16),
        compiler_params=_params(("parallel",)),
    )(rest, rest, rest, conv_w)


def _conv_bwd(rest, conv_w, dy, *, name, dc, tc=128, rows=1056):
    T = rest.shape[0]
    tc = _tile(dc, tc)
    nb = dc // tc
    R = _tile(T, rows, 16)
    n_chunks = T // R

    def body(u_ref, b_ref, c_ref, w_ref, dy_ref, du_ref, db_ref, dc_ref, dw_ref):
        w0, w1, w2 = w_ref[0:1, :], w_ref[1:2, :], w_ref[2:3, :]

        def chunk(r, carry):
            a0, a1, a2 = carry
            u, b, c = _chunk(u_ref, r, R), _chunk(b_ref, r, R), _chunk(c_ref, r, R)
            dy_c = _chunk(dy_ref, r, R)
            cu = c * u
            ext = jnp.concatenate([_pre_halo(c_ref, r, R) * _pre_halo(u_ref, r, R), cu], axis=0)
            cu1, cu2 = _down(ext, 1)[HALO:], _down(ext, 2)[HALO:]
            conv = w0 * cu2 + w1 * cu1 + w2 * cu
            dconv = dy_c * b
            dext = jnp.concatenate(
                [dconv, _post_halo(dy_ref, r, R, n_chunks) * _post_halo(b_ref, r, R, n_chunks)], axis=0)
            dcu = w2 * dconv + w1 * _up(dext, 1)[:R] + w0 * _up(dext, 2)[:R]
            rows_at = pl.ds(pl.multiple_of(r * R, 8), R)
            db_ref[rows_at, :] = (dy_c * conv).astype(BF16)
            du_ref[rows_at, :] = (dcu * c).astype(BF16)
            dc_ref[rows_at, :] = (dcu * u).astype(BF16)
            return (a0 + jnp.sum(dconv * cu2, axis=0, keepdims=True),
                    a1 + jnp.sum(dconv * cu1, axis=0, keepdims=True),
                    a2 + jnp.sum(dconv * cu, axis=0, keepdims=True))

        zero = jnp.zeros((1, tc), F32)
        a0, a1, a2 = lax.fori_loop(0, n_chunks, chunk, (zero, zero, zero))
        dw_ref[0:1, :] = a0
        dw_ref[1:2, :] = a1
        dw_ref[2:3, :] = a2

    col = lambda off: pl.BlockSpec((T, tc), lambda j: (0, off * nb + j))
    own = pl.BlockSpec((T, tc), lambda j: (0, j))
    return pl.pallas_call(
        body, name=name, grid=(nb,),
        in_specs=[col(0), col(1), col(2), pl.BlockSpec((3, tc), lambda j: (0, j)), own],
        out_specs=(own, own, own, pl.BlockSpec((3, tc), lambda j: (0, j))),
        out_shape=(jax.ShapeDtypeStruct((T, dc), BF16),) * 3 + (jax.ShapeDtypeStruct((3, dc), F32),),
        compiler_params=_params(("parallel",)),
    )(rest, rest, rest, conv_w, dy)


def _window_count(r, R, n_rows, w, first_row_offset):
    t = lax.broadcasted_iota(jnp.int32, (n_rows, 1), 0) + (r * R + first_row_offset)
    return jnp.minimum(t + 1, w).astype(F32)


def _pool_fwd(rest, pool_w, pool_scale, *, name, seg0, pg, rows=1056):
    T = rest.shape[0]
    R = _tile(T, rows, 16)
    n_chunks = T // R
    n_groups = len(POOL_WINDOWS)

    def body(x_ref, w_ref, s_ref, y_ref):
        def run(window):
            def chunk(r, carry):
                g = _chunk(x_ref, r, R)
                s = jnp.concatenate([_pre_halo(x_ref, r, R), g], axis=0)
                k = 1
                while k < window:
                    s = s + _down(s, k)
                    k *= 2
                pooled = s[HALO:] / _window_count(r, R, R, window, 0) - g
                mixed = jnp.dot(pooled.astype(BF16), w_ref[0], preferred_element_type=F32)
                y_ref[pl.ds(pl.multiple_of(r * R, 8), R), :] = (mixed * s_ref[...]).astype(BF16)
                return carry

            lax.fori_loop(0, n_chunks, chunk, 0)

        for gi, window in enumerate(POOL_WINDOWS):
            pl.when(pl.program_id(0) == gi)(functools.partial(run, window))

    return pl.pallas_call(
        body, name=name, grid=(n_groups,),
        in_specs=[pl.BlockSpec((T, pg), lambda g: (0, seg0 + g)),
                  pl.BlockSpec((1, pg, pg), lambda g: (g, 0, 0)),
                  pl.BlockSpec((1, pg), lambda g: (0, g))],
        out_specs=pl.BlockSpec((T, pg), lambda g: (0, g)),
        out_shape=jax.ShapeDtypeStruct((T, n_groups * pg), BF16),
        compiler_params=_params(("parallel",)),
    )(rest, pool_w, pool_scale)


def _pool_bwd(rest, pool_w, pool_scale, dy, *, name, seg0, pg, rows=1056):
    T = rest.shape[0]
    R = _tile(T, rows, 16)
    n_chunks = T // R
    n_groups = len(POOL_WINDOWS)

    def body(x_ref, w_ref, s_ref, dy_ref, dx_ref, dw_ref, ds_ref):
        def run(window):
            def chunk(r, carry):
                dw_acc, ds_acc = carry
                g = _chunk(x_ref, r, R)
                s = jnp.concatenate([_pre_halo(x_ref, r, R), g], axis=0)
                k = 1
                while k < window:
                    s = s + _down(s, k)
                    k *= 2
                pooled = (s[HALO:] / _window_count(r, R, R, window, 0) - g).astype(BF16)
                mixed = jnp.dot(pooled, w_ref[0], preferred_element_type=F32)
                dy_c = _chunk(dy_ref, r, R)
                dm_ext = (jnp.concatenate([dy_c, _post_halo(dy_ref, r, R, n_chunks)], axis=0)
                          * s_ref[...]).astype(BF16)
                dpool_ext = lax.dot_general(dm_ext, w_ref[0], (((1,), (1,)), ((), ())),
                                            preferred_element_type=F32)
                a = dpool_ext / _window_count(r, R, R + HALO, window, 0)
                k = 1
                while k < window:
                    a = a + _up(a, k)
                    k *= 2
                dx_ref[pl.ds(pl.multiple_of(r * R, 8), R), :] = (a[:R] - dpool_ext[:R]).astype(BF16)
                dw_acc = dw_acc + lax.dot_general(pooled, dm_ext[:R], (((0,), (0,)), ((), ())),
                                                  preferred_element_type=F32)
                ds_acc = ds_acc + jnp.sum(dy_c * mixed, axis=0, keepdims=True)
                return dw_acc, ds_acc

            dw_acc, ds_acc = lax.fori_loop(0, n_chunks, chunk,
                                           (jnp.zeros((pg, pg), F32), jnp.zeros((1, pg), F32)))
            dw_ref[0] = dw_acc
            ds_ref[...] = ds_acc

        for gi, window in enumerate(POOL_WINDOWS):
            pl.when(pl.program_id(0) == gi)(functools.partial(run, window))

    own = pl.BlockSpec((T, pg), lambda g: (0, g))
    return pl.pallas_call(
        body, name=name, grid=(n_groups,),
        in_specs=[pl.BlockSpec((T, pg), lambda g: (0, seg0 + g)),
                  pl.BlockSpec((1, pg, pg), lambda g: (g, 0, 0)),
                  pl.BlockSpec((1, pg), lambda g: (0, g)), own],
        out_specs=(own, pl.BlockSpec((1, pg, pg), lambda g: (g, 0, 0)), pl.BlockSpec((1, pg), lambda g: (0, g))),
        out_shape=(jax.ShapeDtypeStruct((T, n_groups * pg), BF16),
                   jax.ShapeDtypeStruct((n_groups, pg, pg), F32),
                   jax.ShapeDtypeStruct((1, n_groups * pg), F32)),
        compiler_params=_params(("parallel",)),
    )(rest, pool_w, pool_scale, dy)


def _rope(r, cos_t, sin_t):
    return r * cos_t + pltpu.roll(r, LANES // 2, 1) * sin_t


def _rope_t(d, cos_t, sin_t):
    return d * cos_t + pltpu.roll(d * sin_t, LANES // 2, 1)


def _qk_fwd(q_raw, k_nope, rest, cos_t, sin_t, q_norm, k_norm, *, name, heads, kr_seg, tm=192):
    T = q_raw.shape[0]
    tm = _tile(T, tm, 16)

    def body(q_ref, kn_ref, kr_ref, c_ref, s_ref, gq_ref, gk_ref, qo_ref, ko_ref):
        cos_b, sin_b = c_ref[...], s_ref[...]
        kr = kr_ref[:, 0:LANES]
        kr_ss = jnp.sum(kr * kr, axis=-1, keepdims=True)
        gq, gk = gq_ref[...], gk_ref[...]
        for h in range(heads):
            lo = h * HEAD_PAD
            q = q_ref[:, lo:lo + HEAD_PAD]
            rq = lax.rsqrt(jnp.sum(q * q, axis=-1, keepdims=True) / QK_HEAD + EPS)
            qn = q * (rq * Q_FOLD) * gq
            qo_ref[:, lo:lo + LANES] = qn[:, :LANES].astype(BF16)
            qo_ref[:, lo + LANES:lo + HEAD_PAD] = _rope(qn[:, LANES:], cos_b, sin_b).astype(BF16)
            kn = kn_ref[:, h * LANES:(h + 1) * LANES]
            rk = lax.rsqrt((jnp.sum(kn * kn, axis=-1, keepdims=True) + kr_ss) / QK_HEAD + EPS)
            ko_ref[:, lo:lo + LANES] = (kn * rk * gk[:, :LANES]).astype(BF16)
            ko_ref[:, lo + LANES:lo + HEAD_PAD] = _rope(kr * rk * gk[:, LANES:], cos_b, sin_b).astype(BF16)

    wq, wk = heads * HEAD_PAD, heads * LANES
    return pl.pallas_call(
        body, name=name, grid=(T // tm,),
        in_specs=[pl.BlockSpec((tm, wq), lambda i: (i, 0)), pl.BlockSpec((tm, wk), lambda i: (i, 0)),
                  pl.BlockSpec((tm, HEAD_PAD), lambda i: (i, kr_seg)),
                  pl.BlockSpec((tm, LANES), lambda i: (i, 0)), pl.BlockSpec((tm, LANES), lambda i: (i, 0)),
                  pl.BlockSpec((1, HEAD_PAD), lambda i: (0, 0)), pl.BlockSpec((1, HEAD_PAD), lambda i: (0, 0))],
        out_specs=(pl.BlockSpec((tm, wq), lambda i: (i, 0)), pl.BlockSpec((tm, wq), lambda i: (i, 0))),
        out_shape=(jax.ShapeDtypeStruct((T, wq), BF16), jax.ShapeDtypeStruct((T, wq), BF16)),
        compiler_params=_params(("parallel",)),
    )(q_raw, k_nope, rest, cos_t, sin_t, q_norm, k_norm)


def _qk_bwd(dq, dk, q_raw, k_nope, rest, cos_t, sin_t, q_norm, k_norm, *, name, heads, kr_seg, tm=128):
    T = q_raw.shape[0]
    tm = _tile(T, tm, 16)

    def body(dq_ref, dk_ref, q_ref, kn_ref, kr_ref, c_ref, s_ref, gq_ref, gk_ref,
             dqr_ref, dkn_ref, dkr_ref, dgq_ref, dgk_ref):
        cos_b, sin_b = c_ref[...], s_ref[...]
        kr = kr_ref[:, 0:LANES]
        kr_ss = jnp.sum(kr * kr, axis=-1, keepdims=True)
        gq, gk = gq_ref[...], gk_ref[...]
        dgq = jnp.zeros((1, HEAD_PAD), F32)
        dgk_n = jnp.zeros((1, LANES), F32)
        dgk_r = jnp.zeros((1, LANES), F32)
        dkr = jnp.zeros((tm, LANES), F32)
        for h in range(heads):
            lo = h * HEAD_PAD
            q = q_ref[:, lo:lo + HEAD_PAD]
            rq = lax.rsqrt(jnp.sum(q * q, axis=-1, keepdims=True) / QK_HEAD + EPS)
            qhat = q * rq
            dqn = jnp.concatenate([dq_ref[:, lo:lo + LANES],
                                   _rope_t(dq_ref[:, lo + LANES:lo + HEAD_PAD], cos_b, sin_b)], axis=1) * ATTN_SCALE
            dgq = dgq + jnp.sum(dqn * qhat, axis=0, keepdims=True)
            dqh = dqn * gq
            dqr_ref[:, lo:lo + HEAD_PAD] = (
                rq * (dqh - qhat * (jnp.sum(dqh * qhat, axis=-1, keepdims=True) / QK_HEAD))).astype(BF16)
            kn = kn_ref[:, h * LANES:(h + 1) * LANES]
            rk = lax.rsqrt((jnp.sum(kn * kn, axis=-1, keepdims=True) + kr_ss) / QK_HEAD + EPS)
            khat_n, khat_r = kn * rk, kr * rk
            dkn_n = dk_ref[:, lo:lo + LANES] * (1.0 / LOG2_E)
            dkn_r = _rope_t(dk_ref[:, lo + LANES:lo + HEAD_PAD], cos_b, sin_b) * (1.0 / LOG2_E)
            dgk_n = dgk_n + jnp.sum(dkn_n * khat_n, axis=0, keepdims=True)
            dgk_r = dgk_r + jnp.sum(dkn_r * khat_r, axis=0, keepdims=True)
            dkh_n, dkh_r = dkn_n * gk[:, :LANES], dkn_r * gk[:, LANES:]
            proj = (jnp.sum(dkh_n * khat_n, axis=-1, keepdims=True)
                    + jnp.sum(dkh_r * khat_r, axis=-1, keepdims=True)) / QK_HEAD
            dkn_ref[:, h * LANES:(h + 1) * LANES] = (rk * (dkh_n - khat_n * proj)).astype(BF16)
            dkr = dkr + rk * (dkh_r - khat_r * proj)
        dkr_ref[:, 0:LANES] = dkr.astype(BF16)
        dkr_ref[:, LANES:HEAD_PAD] = jnp.zeros((tm, HEAD_PAD - LANES), BF16)
        dgk = jnp.concatenate([dgk_n, dgk_r], axis=1)

        @pl.when(pl.program_id(0) == 0)
        def _():
            dgq_ref[...] = dgq
            dgk_ref[...] = dgk

        @pl.when(pl.program_id(0) > 0)
        def _():
            dgq_ref[...] += dgq
            dgk_ref[...] += dgk

    wq, wk = heads * HEAD_PAD, heads * LANES
    row = lambda w: pl.BlockSpec((tm, w), lambda i: (i, 0))
    vec = pl.BlockSpec((1, HEAD_PAD), lambda i: (0, 0))
    return pl.pallas_call(
        body, name=name, grid=(T // tm,),
        in_specs=[row(wq), row(wq), row(wq), row(wk), pl.BlockSpec((tm, HEAD_PAD), lambda i: (i, kr_seg)),
                  row(LANES), row(LANES), vec, vec],
        out_specs=(row(wq), row(wk), row(HEAD_PAD), vec, vec),
        out_shape=(jax.ShapeDtypeStruct((T, wq), BF16), jax.ShapeDtypeStruct((T, wk), BF16),
                   jax.ShapeDtypeStruct((T, HEAD_PAD), BF16),
                   jax.ShapeDtypeStruct((1, HEAD_PAD), F32), jax.ShapeDtypeStruct((1, HEAD_PAD), F32)),
        compiler_params=_params(("arbitrary",)),
    )(dq, dk, q_raw, k_nope, rest, cos_t, sin_t, q_norm, k_norm)


def _causal_mask(s):
    row = lax.broadcasted_iota(jnp.int32, s.shape, 0)
    col = lax.broadcasted_iota(jnp.int32, s.shape, 1)
    return jnp.where(row >= col, s, NEG)


def _flash_fwd(q, k, v, *, name, heads, tq=384, hp=2, parts=2):
    T = q.shape[0]
    tq = _tile(T, tq, LANES)
    nq = T // tq
    tr = tq // parts
    nt = (((1,), (1,)), ((), ()))
    chains = [(h, r) for h in range(hp) for r in range(parts)]

    def body(q_ref, k_ref, v_ref, o_ref, lse_ref, acc_ref):
        def q_block(i, carry):
            rows_at = [pl.ds(pl.multiple_of(i * tq + r * tr, tr), tr) for r in range(parts)]
            qbs = [q_ref[rows_at[r], h * HEAD_PAD:(h + 1) * HEAD_PAD] for h, r in chains]
            for c in range(len(chains)):
                acc_ref[c] = jnp.zeros((tr, V_HEAD), F32)

            def step(j, state, masked):
                k_at = pl.ds(pl.multiple_of(j * tq, tq), tq)
                new = []
                scores = [lax.dot_general(qb, k_ref[k_at, h * HEAD_PAD:(h + 1) * HEAD_PAD], nt,
                                          preferred_element_type=F32) for qb, (h, r) in zip(qbs, chains)]
                for c, (s, (h, r)) in enumerate(zip(scores, chains)):
                    m, l = state[c]
                    if masked:
                        row = lax.broadcasted_iota(jnp.int32, s.shape, 0) + r * tr
                        s = jnp.where(row >= lax.broadcasted_iota(jnp.int32, s.shape, 1), s, NEG)
                    m_new = jnp.maximum(m, jnp.max(s, axis=-1, keepdims=True))
                    p = jnp.exp2(s - m_new)
                    alpha = jnp.exp2(m - m_new)
                    new.append((m_new, alpha * l + jnp.sum(p, axis=-1, keepdims=True)))
                    acc_ref[c] = alpha * acc_ref[c] + jnp.dot(p.astype(BF16), v_ref[k_at, h * V_HEAD:(h + 1) * V_HEAD],
                                                              preferred_element_type=F32)
                return tuple(new)

            init = tuple((jnp.full((tr, 1), NEG, F32), jnp.zeros((tr, 1), F32)) for _ in chains)
            state = lax.fori_loop(0, i, lambda j, st: step(j, st, False), init)
            state = step(i, state, True)
            for c, ((m, l), (h, r)) in enumerate(zip(state, chains)):
                o_ref[rows_at[r], h * V_HEAD:(h + 1) * V_HEAD] = (acc_ref[c] / l).astype(BF16)
                lse_ref[h, rows_at[r], :] = jnp.broadcast_to(m + jnp.log2(l), (tr, LANES))
            return carry

        lax.fori_loop(0, nq, q_block, 0)

    qk_spec = pl.BlockSpec((T, hp * HEAD_PAD), lambda g: (0, g))
    v_spec = pl.BlockSpec((T, hp * V_HEAD), lambda g: (0, g))
    return pl.pallas_call(
        body, name=name, grid=(heads // hp,), in_specs=[qk_spec, qk_spec, v_spec],
        out_specs=(v_spec, pl.BlockSpec((hp, T, LANES), lambda g: (g, 0, 0))),
        out_shape=(jax.ShapeDtypeStruct((T, heads * V_HEAD), BF16), jax.ShapeDtypeStruct((heads, T, LANES), F32)),
        scratch_shapes=[pltpu.VMEM((len(chains), tr, V_HEAD), F32)],
        compiler_params=_params(("parallel",)),
    )(q, k, v)


def _flash_bwd(q, k, v, o, do, lse, *, name, heads, tq=384):
    T = q.shape[0]
    tq = _tile(T, tq, LANES)
    nq = T // tq
    nt = (((1,), (1,)), ((), ()))
    tn = (((0,), (0,)), ((), ()))

    def body(q_ref, k_ref, v_ref, o_ref, do_ref, lse_ref, dq_ref, dk_ref, dv_ref, delta_ref, dv_acc_ref):
        def fill_delta(i, carry):
            at = pl.ds(pl.multiple_of(i * tq, tq), tq)
            d = jnp.sum(o_ref[at, :].astype(F32) * do_ref[at, :].astype(F32), axis=-1, keepdims=True)
            delta_ref[at, :] = jnp.broadcast_to(d, (tq, LANES))
            dq_ref[at, :] = jnp.zeros((tq, HEAD_PAD), F32)
            return carry

        lax.fori_loop(0, nq, fill_delta, 0)

        def kv_block(j, carry):
            k_at = pl.ds(pl.multiple_of(j * tq, tq), tq)
            kb, vb = k_ref[k_at, :], v_ref[k_at, :]

            def steps(blocks, masked):
                at = [pl.ds(pl.multiple_of(i * tq, tq), tq) for i in blocks]
                qbs = [q_ref[a, :] for a in at]
                dobs = [do_ref[a, :] for a in at]
                scores = [lax.dot_general(qb, kb, nt, preferred_element_type=F32) for qb in qbs]
                dps = [lax.dot_general(dob, vb, nt, preferred_element_type=F32) for dob in dobs]
                for a, qb, dob, sc, dp in zip(at, qbs, dobs, scores, dps):
                    if masked:
                        sc = _causal_mask(sc)
                    p = jnp.exp2(sc - lse_ref[0, a, :][:, 0:1])
                    ds = (p * (dp - delta_ref[a, :][:, 0:1])).astype(BF16)
                    dv_part = lax.dot_general(p.astype(BF16), dob, tn, preferred_element_type=F32)
                    dk_part = lax.dot_general(ds, qb, tn, preferred_element_type=F32)
                    if masked:
                        dv_acc_ref[...] = dv_part
                        dk_ref[k_at, :] = dk_part
                    else:
                        dv_acc_ref[...] += dv_part
                        dk_ref[k_at, :] += dk_part
                    dq_ref[a, :] += jnp.dot(ds, kb, preferred_element_type=F32)

            def two_blocks(t, carry):
                steps([j + 1 + 2 * t, j + 2 + 2 * t], False)
                return carry

            steps([j], True)
            rest = nq - 1 - j
            lax.fori_loop(0, rest // 2, two_blocks, 0)

            @pl.when(rest % 2 == 1)
            def _():
                steps([nq - 1], False)

            dv_ref[k_at, :] = dv_acc_ref[...].astype(BF16)
            return carry

        lax.fori_loop(0, nq, kv_block, 0)

    qk_spec = pl.BlockSpec((T, HEAD_PAD), lambda h: (0, h))
    v_spec = pl.BlockSpec((T, V_HEAD), lambda h: (0, h))
    return pl.pallas_call(
        body, name=name, grid=(heads,),
        in_specs=[qk_spec, qk_spec, v_spec, v_spec, v_spec, pl.BlockSpec((1, T, LANES), lambda h: (h, 0, 0))],
        out_specs=(qk_spec, qk_spec, v_spec),
        out_shape=(jax.ShapeDtypeStruct((T, heads * HEAD_PAD), F32), jax.ShapeDtypeStruct((T, heads * HEAD_PAD), F32),
                   jax.ShapeDtypeStruct((T, heads * V_HEAD), BF16)),
        scratch_shapes=[pltpu.VMEM((T, LANES), F32), pltpu.VMEM((tq, V_HEAD), F32)],
        compiler_params=_params(("parallel",)),
    )(q, k, v, o, do, lse)


def _merge_fwd(gl, pa, pb, pc, *, name, d, tm=384, tn=1024):
    T = pa.shape[0]
    tm, tn = _tile(T, tm, 16), _tile(d, tn)
    nb = d // tn

    def body(g0, g1, g2, a, b, c, o_ref):
        o_ref[...] = (jax.nn.sigmoid(g0[...]) * a[...] + jax.nn.sigmoid(g1[...]) * b[...]
                      + jax.nn.sigmoid(g2[...]) * c[...]).astype(BF16)

    gate = lambda n: pl.BlockSpec((tm, tn), lambda i, j: (i, n * nb + j))
    blk = pl.BlockSpec((tm, tn), lambda i, j: (i, j))
    return pl.pallas_call(
        body, name=name, grid=(T // tm, nb), in_specs=[gate(0), gate(1), gate(2), blk, blk, blk],
        out_specs=blk, out_shape=jax.ShapeDtypeStruct((T, d), BF16),
        compiler_params=_params(("parallel", "parallel")),
    )(gl, gl, gl, pa, pb, pc)


def _merge_bwd(dm, gl, pa, pb, pc, *, name, d, tm=384, tn=1024):
    T = pa.shape[0]
    tm, tn = _tile(T, tm, 16), _tile(d, tn)
    nb = d // tn

    def body(dm_ref, g0, g1, g2, a, b, c, da, db, dc, dg0, dg1, dg2):
        dmv = dm_ref[...]
        for g_ref, p_ref, dp_ref, dg_ref in ((g0, a, da, dg0), (g1, b, db, dg1), (g2, c, dc, dg2)):
            sg = jax.nn.sigmoid(g_ref[...])
            dp_ref[...] = (dmv * sg).astype(BF16)
            dg_ref[...] = (dmv * p_ref[...] * sg * (1.0 - sg)).astype(BF16)

    gate = lambda n: pl.BlockSpec((tm, tn), lambda i, j: (i, n * nb + j))
    blk = pl.BlockSpec((tm, tn), lambda i, j: (i, j))
    return pl.pallas_call(
        body, name=name, grid=(T // tm, nb), in_specs=[blk, gate(0), gate(1), gate(2), blk, blk, blk],
        out_specs=(blk,) * 6, out_shape=(jax.ShapeDtypeStruct((T, d), BF16),) * 6,
        compiler_params=_params(("parallel", "parallel")),
    )(dm, gl, gl, gl, pa, pb, pc)


def _loss(y, target, *, name, first, last, tm=384):
    T, d = y.shape
    tm = _tile(T, tm, 16)

    def body(y_ref, t_ref, loss_ref, dy_ref, dyb_ref):
        i = pl.program_id(0)
        row = lax.broadcasted_iota(jnp.int32, (tm, 1), 0) + i * tm
        real = jnp.logical_and(row >= first, row < last)
        err = jnp.where(real, y_ref[...] - t_ref[...], 0.0)
        dy_ref[...] = err * (1.0 / d)
        dyb_ref[...] = (err * (1.0 / d)).astype(BF16)
        part = jnp.broadcast_to(jnp.sum(err * err, keepdims=True).reshape(1, 1), (1, LANES))

        @pl.when(i == 0)
        def _():
            loss_ref[...] = part

        @pl.when(i > 0)
        def _():
            loss_ref[...] += part

    blk = pl.BlockSpec((tm, d), lambda i: (i, 0))
    return pl.pallas_call(
        body, name=name, grid=(T // tm,), in_specs=[blk, blk],
        out_specs=(pl.BlockSpec((1, LANES), lambda i: (0, 0)), blk, blk),
        out_shape=(jax.ShapeDtypeStruct((1, LANES), F32), jax.ShapeDtypeStruct((T, d), F32),
                   jax.ShapeDtypeStruct((T, d), BF16)),
        compiler_params=_params(("arbitrary",)),
    )(y, target)


def _as3d(a):
    return a.reshape(a.shape[0], -1, a.shape[-1])


def _sum_stack(parts, *, name, out_dtype, rows=256):
    n, R, C = parts.shape
    tr = _tile(R, rows, 16)

    def body(p_ref, o_ref):
        acc = p_ref[0].astype(F32)
        for s in range(1, n):
            acc = acc + p_ref[s].astype(F32)
        o_ref[...] = acc.astype(out_dtype)

    return pl.pallas_call(
        body, name=name, grid=(R // tr,),
        in_specs=[pl.BlockSpec((n, tr, C), lambda i: (0, i, 0))],
        out_specs=pl.BlockSpec((tr, C), lambda i: (i, 0)),
        out_shape=jax.ShapeDtypeStruct((R, C), out_dtype),
        compiler_params=_params(("parallel",)),
    )(parts)


def _adamw(w, g, m, v, *, name, rows=128):
    R, C = w.shape
    tr = _tile(R, rows, 8)
    c1 = 1.0 - ADAM_B1 ** ADAM_STEP
    c2 = 1.0 - ADAM_B2 ** ADAM_STEP

    def body(w_ref, g_ref, m_ref, v_ref, d_ref, nm_ref, nv_ref):
        gv = g_ref[...]
        nm = ADAM_B1 * m_ref[...] + (1.0 - ADAM_B1) * gv
        nv = ADAM_B2 * v_ref[...] + (1.0 - ADAM_B2) * (gv * gv)
        nm_ref[...] = nm
        nv_ref[...] = nv
        d_ref[...] = -ADAM_LR * ((nm / c1) / (jnp.sqrt(nv / c2) + ADAM_EPS) + ADAM_WD * w_ref[...])

    blk = pl.BlockSpec((tr, C), lambda i: (i, 0))
    return pl.pallas_call(
        body, name=name, grid=(R // tr,), in_specs=[blk] * 4, out_specs=(blk,) * 3,
        out_shape=(jax.ShapeDtypeStruct((R, C), F32),) * 3,
        compiler_params=_params(("parallel",)),
    )(w, g, m, v)


def _one_hot(index, n):
    return jnp.broadcast_to((jnp.arange(n) == index).astype(F32)[:, None, None], (n, 8, LANES))


def _is_set(flags_ref, s):
    return flags_ref[s, 0:1, 0:1] > 0.5


def _rows_for(h, width, itemsize, n_stacked, budget, mult):
    return _tile(h, max(mult, budget // (n_stacked * width * itemsize)), mult)


def _pair_sum(pieces, recv, core, *, name):
    _, H, C = recv.shape
    tr = _rows_for(H, C, 2, 1, 2 << 20, 16)
    nh = H // tr

    def body(lo_ref, hi_ref, r_ref, core_ref, o_ref):
        mine = jnp.where(_is_set(core_ref, 0), lo_ref[0], hi_ref[0])
        o_ref[0] = (mine.astype(F32) + r_ref[0].astype(F32)).astype(BF16)

    blk = pl.BlockSpec((1, tr, C), lambda j, i: (j, i, 0))
    return pl.pallas_call(
        body, name=name, grid=(4, nh),
        in_specs=[blk, pl.BlockSpec((1, tr, C), lambda j, i: (j, nh + i, 0)), blk,
                  pl.BlockSpec((2, 8, LANES), lambda j, i: (0, 0, 0))],
        out_specs=blk, out_shape=jax.ShapeDtypeStruct((4, H, C), BF16),
        compiler_params=_params(("parallel", "parallel")),
    )(pieces, pieces, recv, core)


def _chip_sum(pair, landed, chip_flags, *, name):
    _, H, C = pair.shape
    tr = _rows_for(H, C, 2, 4, 8 << 20, 16)

    def body(p_ref, l_ref, chip_ref, o_ref):
        acc = None
        for s in range(4):
            part = jnp.where(_is_set(chip_ref, s), p_ref[s], l_ref[s]).astype(F32)
            acc = part if acc is None else acc + part
        o_ref[...] = acc

    blk = pl.BlockSpec((4, tr, C), lambda i: (0, i, 0))
    return pl.pallas_call(
        body, name=name, grid=(H // tr,),
        in_specs=[blk, blk, pl.BlockSpec((4, 8, LANES), lambda i: (0, 0, 0))],
        out_specs=pl.BlockSpec((tr, C), lambda i: (i, 0)), out_shape=jax.ShapeDtypeStruct((H, C), F32),
        compiler_params=_params(("parallel",)),
    )(pair, landed, chip_flags)


def _adamw_layer(w, m, v, total, recv, core, layer, prev, *, name, col_halves=False, after=()):
    _, R, C = w.shape
    H, wd = total.shape
    tr = _rows_for(H, wd, 4, 1, 2 << 20, 8)
    nh = H // tr
    c1 = 1.0 - ADAM_B1 ** ADAM_STEP
    c2 = 1.0 - ADAM_B2 ** ADAM_STEP
    n_prev = 0 if prev is None else 4
    after = tuple(after)

    def body(*refs):
        w_ref, m_ref, v_ref, t_ref, r_ref, core_ref = refs[:6]
        g_ref, d_ref, nm_ref, nv_ref = refs[6 + n_prev + len(after):]
        half_is_mine = jnp.where(pl.program_id(0) == 0, core_ref[0, 0:1, 0:1], core_ref[1, 0:1, 0:1]) > 0.5
        gv = jnp.where(half_is_mine, t_ref[...], r_ref[...])
        nm = ADAM_B1 * m_ref[0] + (1.0 - ADAM_B1) * gv
        nv = ADAM_B2 * v_ref[0] + (1.0 - ADAM_B2) * (gv * gv)
        g_ref[0] = gv
        nm_ref[0] = nm
        nv_ref[0] = nv
        d_ref[0] = -ADAM_LR * ((nm / c1) / (jnp.sqrt(nv / c2) + ADAM_EPS) + ADAM_WD * w_ref[0])

    if col_halves:
        lay = pl.BlockSpec((1, tr, wd), lambda hf, i: (layer, i, hf))
    else:
        lay = pl.BlockSpec((1, tr, wd), lambda hf, i: (layer, hf * nh + i, 0))
    one = pl.BlockSpec((tr, wd), lambda hf, i: (i, 0))
    operands = [w, m, v, total, recv, core] + ([] if prev is None else list(prev)) + list(after)
    return pl.pallas_call(
        body, name=name, grid=(2, nh),
        in_specs=[lay, lay, lay, one, one, pl.BlockSpec((2, 8, LANES), lambda hf, i: (0, 0, 0))]
        + [ANY] * (n_prev + len(after)),
        out_specs=(lay,) * 4, out_shape=(jax.ShapeDtypeStruct((2, R, C), F32),) * 4,
        input_output_aliases={6 + i: i for i in range(n_prev)},
        compiler_params=_params(("parallel", "parallel")),
    )(*operands)


ANY = pl.BlockSpec(memory_space=pl.ANY)


def _coords():
    return lax.axis_index("x"), lax.axis_index("y"), lax.axis_index("c")


HBM = pl.BlockSpec(memory_space=pltpu.HBM)
SEM = pl.BlockSpec(memory_space=pltpu.SEMAPHORE)
EFFECT = pltpu.SideEffectType.DATAFLOW_SIDE_EFFECTING


def _copies(plan, bufs, send_sems, recv_sems):
    return [pltpu.make_async_remote_copy(src_ref=s, dst_ref=d, send_sem=send_sems.at[i], recv_sem=recv_sems.at[i],
                                         device_id=to, device_id_type=MESH)
            for i, (s, d, to) in enumerate(plan(bufs))]


def _start_copies(bufs, groups, *, name):
    nb, ng = len(bufs), len(groups)

    def body(*refs):
        buf_refs = refs[:nb]
        sems = refs[nb:nb + 2 * ng]
        token = refs[-1]
        for g, (plan, _) in enumerate(groups):
            for cp in _copies(plan, buf_refs, sems[2 * g], sems[2 * g + 1]):
                cp.start()
        token[...] = jnp.zeros_like(token)

    sem_shapes = []
    for _, n in groups:
        sem_shapes += [pltpu.SemaphoreType.DMA((n,)), pltpu.SemaphoreType.DMA((n,))]
    out = pl.pallas_call(
        body, name=name, in_specs=[HBM] * nb,
        out_specs=tuple([SEM] * (2 * ng) + [HBM] * nb + [pl.BlockSpec(memory_space=pltpu.VMEM)]),
        out_shape=tuple(sem_shapes + [pltpu.HBM(b.shape, b.dtype) for b in bufs] + [jax.ShapeDtypeStruct((8, LANES), F32)]),
        input_output_aliases={i: 2 * ng + i for i in range(nb)},
        compiler_params=pltpu.CompilerParams(has_side_effects=EFFECT),
    )(*[pltpu.with_memory_space_constraint(b, pltpu.HBM) for b in bufs])
    sems = [(out[2 * g], out[2 * g + 1]) for g in range(ng)]
    return sems, list(out[2 * ng:2 * ng + nb]), out[-1]


def _wait_copies(bufs, sems, plan, after, *, name):
    nb = len(bufs)

    def body(*refs):
        buf_refs = refs[:nb]
        for cp in _copies(plan, buf_refs, refs[nb], refs[nb + 1]):
            cp.wait_send()
            cp.wait_recv()

    out = pl.pallas_call(
        body, name=name, in_specs=[HBM] * nb + [SEM, SEM, ANY], out_specs=tuple([HBM] * nb),
        out_shape=tuple(pltpu.HBM(b.shape, b.dtype) for b in bufs),
        input_output_aliases={i: i for i in range(nb)},
        compiler_params=pltpu.CompilerParams(has_side_effects=EFFECT),
    )(*bufs, sems[0], sems[1], after)
    return list(out)


def _half(ref, c):
    h = ref.shape[0] // 2
    return ref.at[pl.ds(c * h, h)]


def _ici_gather_plan(pairs):
    def plan(refs):
        x, y, c = _coords()
        me = 2 * x + y
        out = []
        for s, d in pairs:
            for cx, cy in [(1 - x, y), (x, 1 - y), (1 - x, 1 - y)]:
                out.append((_half(refs[s], c), _half(refs[d].at[me], c), (cx, cy, c)))
            out.append((refs[s], refs[d].at[me], (x, y, 1 - c)))
        return out
    return plan, 4 * len(pairs)


def _d2d_forward_plan(lands):
    def plan(refs):
        x, y, c = _coords()
        out = []
        for d in lands:
            for cx, cy in [(1 - x, y), (x, 1 - y), (1 - x, 1 - y)]:
                got = _half(refs[d].at[2 * cx + cy], c)
                out.append((got, got, (x, y, 1 - c)))
        return out
    return plan, 3 * len(lands)


def _swap_half_plan(pairs):
    def plan(refs):
        x, y, c = _coords()
        out = []
        for s, d in pairs:
            h = refs[d].shape[1]
            out.append((refs[s].at[:, pl.ds((1 - c) * h, h)], refs[d], (x, y, 1 - c)))
        return out
    return plan, len(pairs)


def _scatter_plan(pairs):
    def plan(refs):
        x, y, c = _coords()
        me = 2 * x + y
        out = []
        for s, d in pairs:
            for cx, cy in [(1 - x, y), (x, 1 - y), (1 - x, 1 - y)]:
                out.append((refs[s].at[2 * cx + cy], refs[d].at[me], (cx, cy, c)))
        return out
    return plan, 3 * len(pairs)


def _swap_total_plan(pairs):
    def plan(refs):
        x, y, c = _coords()
        return [(refs[s], refs[d], (x, y, 1 - c)) for s, d in pairs]
    return plan, len(pairs)


def _gather_all(block, *, name, after=()):
    after = tuple(after)

    def body(src, *rest):
        out, send_sems, recv_sems, local_sem = rest[len(after):]
        x, y, c = _coords()
        me = 4 * x + 2 * y + c
        flips = [(fx, fy, fc) for fx in (0, 1) for fy in (0, 1) for fc in (0, 1)][1:]
        mine = pltpu.make_async_copy(src, out.at[me], local_sem)
        mine.start()
        peers = [(x ^ fx, y ^ fy, c ^ fc) for fx, fy, fc in flips]
        cps = [pltpu.make_async_remote_copy(src_ref=src, dst_ref=out.at[me], send_sem=send_sems.at[k],
                                            recv_sem=recv_sems.at[k], device_id=peer, device_id_type=MESH)
               for k, peer in enumerate(peers)]
        for cp in cps:
            cp.start()
        for k, (px, py, pc) in enumerate(peers):
            slot = out.at[4 * px + 2 * py + pc]
            pltpu.make_async_remote_copy(src_ref=slot, dst_ref=slot, send_sem=send_sems.at[k], recv_sem=recv_sems.at[k],
                                         device_id=(px, py, pc), device_id_type=MESH).wait_recv()
        for cp in cps:
            cp.wait_send()
        mine.wait()

    return pl.pallas_call(
        body, name=name, in_specs=[ANY] * (1 + len(after)), out_specs=ANY,
        out_shape=jax.ShapeDtypeStruct((8,) + block.shape, block.dtype),
        scratch_shapes=[pltpu.SemaphoreType.DMA((7,)), pltpu.SemaphoreType.DMA((7,)), pltpu.SemaphoreType.DMA],
    )(block, *after)


def _cols(o):
    return jnp.transpose(o, (1, 0, 2)).reshape(o.shape[1], -1)


def _uncols(full):
    return jnp.transpose(full.reshape(full.shape[0], 4, -1), (1, 0, 2))


def _rope_pad(x1, x2):
    z = jnp.zeros_like(x1)
    return jnp.concatenate([x1, z, x2, z], axis=-1)


def _head_pad(w, heads):
    r = w.reshape(w.shape[0], heads, QK_HEAD)
    half = QK_ROPE // 2
    out = jnp.concatenate([r[..., :QK_NOPE], _rope_pad(r[..., QK_NOPE:QK_NOPE + half], r[..., QK_NOPE + half:])], axis=-1)
    return out.reshape(w.shape[0], heads * HEAD_PAD)


def _head_unpad(w, heads):
    r = w.reshape(w.shape[0], heads, HEAD_PAD)
    half = QK_ROPE // 2
    out = jnp.concatenate([r[..., :QK_NOPE], r[..., QK_NOPE:QK_NOPE + half],
                           r[..., QK_NOPE + 2 * half:QK_NOPE + 3 * half]], axis=-1)
    return out.reshape(w.shape[0], heads * QK_HEAD)


class _Dims:
    def __init__(self, d, seq):
        self.d = d
        self.seq = seq
        self.t_real = N_META + seq
        self.t = -(-self.t_real // LANES) * LANES
        self.dc = d // 2
        self.dp = d // 2
        self.pg = self.dp // len(POOL_WINDOWS)
        self.heads = d // 128
        self.dff = 4 * d
        self.a_end = 3 * self.dc
        self.q_end = self.a_end + Q_LORA
        self.kv_end = self.q_end + KV_LORA
        self.kr_end = self.kv_end + QK_ROPE
        self.pool_end = self.kr_end + self.dp
        self.d_in = self.pool_end + 3 * d
        self.r_pool = 3 * self.dc
        self.r_q = self.r_pool + self.dp
        self.r_kv = self.r_q + Q_LORA
        self.r_kr = self.r_kv + KV_LORA
        self.r_width = self.r_kr + HEAD_PAD


def _split_cols(a):
    return jnp.moveaxis(a.reshape(a.shape[:-1] + (2, a.shape[-1] // 2)), -2, -3)


def _join_cols(a):
    a = jnp.moveaxis(a, -3, -2)
    return a.reshape(a.shape[:-2] + (a.shape[-2] * a.shape[-1],))


def _in_weights(dm, pieces):
    w_t = _join_cols(pieces).reshape(dm.d_in, dm.d)
    half = QK_ROPE // 2
    kr = w_t[dm.kv_end:dm.kr_end]
    zeros = jnp.zeros((half, dm.d), BF16)
    kr_p = jnp.concatenate([kr[:half], zeros, kr[half:], zeros, jnp.zeros((HEAD_PAD - LANES, dm.d), BF16)], axis=0)
    return dict(
        wg_t=w_t[dm.pool_end:],
        wr_t=jnp.concatenate([w_t[:dm.a_end], w_t[dm.kr_end:dm.pool_end], w_t[dm.a_end:dm.kv_end], kr_p], axis=0))


def _other_weights(dm, g):
    out = {}
    if "w_ukv" in g:
        w_ukv = _cols(g["w_ukv"]).reshape(KV_LORA, dm.heads, QK_NOPE + V_HEAD)
        out["wkn"] = w_ukv[:, :, :QK_NOPE].reshape(KV_LORA, dm.heads * QK_NOPE)
        out["wv"] = w_ukv[:, :, QK_NOPE:].reshape(KV_LORA, dm.heads * V_HEAD)
    if "w_uq" in g:
        out["wuq"] = _head_pad(_cols(g["w_uq"]), dm.heads)
    if "pool_w" in g:
        out["wp"] = jnp.transpose(g["pool_w"], (1, 0, 2, 3)).reshape(len(POOL_WINDOWS), dm.pg, dm.pg)
    for name, key in (("w_branch_a", "wba"), ("w_branch_c", "wbc"), ("w_up", "wup")):
        if name in g:
            out[key] = _cols(g[name])
    for name, key in (("w_branch_b", "wbb"), ("w_o", "wo"), ("w_down", "wdn")):
        if name in g:
            out[key] = g[name].reshape(-1, dm.d)
    return out


def _small_weights(small):
    return dict(
        conv_w=small["conv_w"],
        attn_norm=small["attn_norm"][None], mlp_norm=small["mlp_norm"][None],
        q_lat_norm=small["q_lat_norm"][None], kv_lat_norm=small["kv_lat_norm"][None],
        q_norm=_head_pad(small["q_norm"][None], 1), k_norm=_head_pad(small["k_norm"][None], 1),
        pool_scale=small["pool_scale"][None],
    )


def _grad_piece(dm, dw, name):
    half = QK_ROPE // 2
    rows = lambda a: a.reshape((4, a.shape[0] // 4) + a.shape[1:])
    if name == "w_in":
        dwr, dwg = dw["wr_t"], dw["wg_t"]
        d_t = jnp.concatenate([
            dwr[:dm.r_pool], dwr[dm.r_q:dm.r_kr], dwr[dm.r_kr:dm.r_kr + half],
            dwr[dm.r_kr + 2 * half:dm.r_kr + 3 * half], dwr[dm.r_pool:dm.r_q], dwg], axis=0)
        out = _split_cols(rows(d_t))
    elif name == "w_ukv":
        out = _uncols(jnp.concatenate([dw["wkn"].reshape(KV_LORA, dm.heads, QK_NOPE),
                                       dw["wv"].reshape(KV_LORA, dm.heads, V_HEAD)], axis=-1).reshape(KV_LORA, -1))
    elif name == "w_uq":
        out = _uncols(_head_unpad(dw["wuq"], dm.heads))
    elif name == "pool_w":
        out = jnp.transpose(dw["wp"].reshape(len(POOL_WINDOWS), 4, dm.pg // 4, dm.pg), (1, 0, 2, 3))
    elif name in ("w_branch_a", "w_branch_c", "w_up"):
        out = dw[{"w_branch_a": "wba", "w_branch_c": "wbc", "w_up": "wup"}[name]]
    else:
        out = rows(dw[{"w_branch_b": "wbb", "w_o": "wo", "w_down": "wdn"}[name]])
    return out.astype(BF16)


def _layer_fwd(dm, W, x, cos_t, sin_t, tag, more=None, h=None):
    n = lambda s: f"{s}_{tag}"
    if h is None:
        h = _rms_fwd(x, W["attn_norm"], name=n("attn_norm"))
    gl = _mm(h, W["wg_t"], name=n("proj_gates"), tb=True)
    rest = _mm(h, W["wr_t"], name=n("proj_rest"), tb=True)
    if more is not None:
        W.update(more("after_proj", rest))
    y_a = _conv_fwd(rest, W["conv_w"], name=n("conv"), dc=dm.dc)
    y_c = _pool_fwd(rest, W["wp"], W["pool_scale"], name=n("pool"), seg0=dm.r_pool // dm.pg, pg=dm.pg)
    q_lat = _rms_fwd(rest, W["q_lat_norm"], name=n("q_lat_norm"), width=Q_LORA, seg=dm.r_q // Q_LORA)
    kv_lat = _rms_fwd(rest, W["kv_lat_norm"], name=n("kv_lat_norm"), width=KV_LORA, seg=dm.r_kv // KV_LORA)
    q_raw = _mm(q_lat, W["wuq"], name=n("up_q"))
    k_nope = _mm(kv_lat, W["wkn"], name=n("up_k"))
    v = _mm(kv_lat, W["wv"], name=n("up_v"), out_dtype=BF16)
    q, k = _qk_fwd(q_raw, k_nope, rest, cos_t, sin_t, W["q_norm"], W["k_norm"], name=n("qk_norm_rope"),
                   heads=dm.heads, kr_seg=dm.r_kr // HEAD_PAD)
    if more is not None:
        W.update(more("after_qk", q))
    y_b, lse = _flash_fwd(q, k, v, name=n("attention"), heads=dm.heads)
    pa = _mm(y_a, W["wba"], name=n("branch_a"))
    pb = _mm(y_b, W["wbb"], name=n("branch_b"))
    pc = _mm(y_c, W["wbc"], name=n("branch_c"))
    merged = _merge_fwd(gl, pa, pb, pc, name=n("merge"), d=dm.d)
    x1 = _mm(merged, W["wo"], name=n("out_proj"), add=x)
    h2 = _rms_fwd(x1, W["mlp_norm"], name=n("mlp_norm"))
    up, act = _mm(h2, W["wup"], name=n("mlp_up"), epi="relu2")
    x2 = _mm(act, W["wdn"], name=n("mlp_down"), add=x1, tm=704, tk=4096)
    saved = dict(x=x, h=h, gl=gl, rest=rest, y_a=y_a, y_c=y_c, q_lat=q_lat, kv_lat=kv_lat, q_raw=q_raw, k_nope=k_nope,
                 v=v, q=q, k=k, y_b=y_b, lse=lse, pa=pa, pb=pb, pc=pc, merged=merged, x1=x1, h2=h2, up=up, act=act)
    return x2, saved


def _layer_bwd(dm, W, S, dx2, dx2_b, cos_t, sin_t, tag, hook=None):
    n = lambda s: f"{s}_{tag}"
    dw, ds = {}, {}
    if hook is None:
        hook = lambda point, t, dw_so_far: ()
    dup = _mm(dx2_b, W["wdn"], name=n("d_mlp_down"), tb=True, aux=S["up"], epi="drelu2", out_dtype=BF16,
              after=hook("start", dx2, dw))
    dw["wdn"] = _mm(S["act"], dx2_b, name=n("dw_mlp_down"), ta=True, tm=512, out_dtype=BF16)
    dh2 = _mm(dup, W["wup"], name=n("d_mlp_up"), tb=True, tm=704, tk=4096)
    dw["wup"] = _mm(S["h2"], dup, name=n("dw_mlp_up"), ta=True, tm=512, out_dtype=BF16, pieces=4)
    dx1, dx1_b, ds["mlp_norm"] = _rms_bwd(dh2, S["x1"], W["mlp_norm"], name=n("d_mlp_norm"), res=dx2, bf16_copy=True)
    dmerged = _mm(dx1_b, W["wo"], name=n("d_out_proj"), tb=True, after=hook("after_mlp", dx1, dw))
    dw["wo"] = _mm(S["merged"], dx1_b, name=n("dw_out_proj"), ta=True, tm=512, out_dtype=BF16)
    dpa, dpb, dpc, dg0, dg1, dg2 = _merge_bwd(dmerged, S["gl"], S["pa"], S["pb"], S["pc"], name=n("d_merge"), d=dm.d)
    dgl = jnp.concatenate([dg0, dg1, dg2], axis=1)
    dy_a = _mm(dpa, W["wba"], name=n("d_branch_a"), tb=True)
    dw["wba"] = _mm(S["y_a"], dpa, name=n("dw_branch_a"), ta=True, tm=512, out_dtype=BF16, pieces=4)
    dy_b = _mm(dpb, W["wbb"], name=n("d_branch_b"), tb=True, out_dtype=BF16)
    dw["wbb"] = _mm(S["y_b"], dpb, name=n("dw_branch_b"), ta=True, tm=512, out_dtype=BF16)
    dy_c = _mm(dpc, W["wbc"], name=n("d_branch_c"), tb=True)
    dw["wbc"] = _mm(S["y_c"], dpc, name=n("dw_branch_c"), ta=True, tm=512, out_dtype=BF16, pieces=4)
    dq, dk, dv = _flash_bwd(S["q"], S["k"], S["v"], S["y_b"], dy_b, S["lse"], name=n("d_attention"), heads=dm.heads)
    after_attention = hook("after_attention", dq, dw)
    dq_raw, dk_nope, dk_rope, dgq, dgk = _qk_bwd(
        dq, dk, S["q_raw"], S["k_nope"], S["rest"], cos_t, sin_t, W["q_norm"], W["k_norm"], name=n("d_qk_norm_rope"),
        heads=dm.heads, kr_seg=dm.r_kr // HEAD_PAD)
    ds["q_norm"] = _head_unpad(dgq, 1)
    ds["k_norm"] = _head_unpad(dgk, 1)
    dkv_v = _mm(dv, W["wv"], name=n("d_up_v"), tb=True, after=after_attention)
    dq_lat_n = _mm(dq_raw, W["wuq"], name=n("d_up_q"), tb=True, after=hook("after_qk", dq_raw, dw))
    dw["wuq"] = _mm(S["q_lat"], dq_raw, name=n("dw_up_q"), ta=True, tm=512)
    dkv_lat_n = _mm(dk_nope, W["wkn"], name=n("d_up_k"), tb=True, add=dkv_v)
    dw["wkn"] = _mm(S["kv_lat"], dk_nope, name=n("dw_up_k"), ta=True, tm=512)
    dw["wv"] = _mm(S["kv_lat"], dv, name=n("dw_up_v"), ta=True, tm=512)
    dq_lat, ds["q_lat_norm"] = _rms_bwd(dq_lat_n, S["rest"], W["q_lat_norm"], name=n("d_q_lat_norm"), width=Q_LORA,
                                        seg=dm.r_q // Q_LORA, out_dtype=BF16)
    dkv_lat, ds["kv_lat_norm"] = _rms_bwd(dkv_lat_n, S["rest"], W["kv_lat_norm"], name=n("d_kv_lat_norm"), width=KV_LORA,
                                          seg=dm.r_kv // KV_LORA, out_dtype=BF16)
    du, db, dc, ds["conv_w"] = _conv_bwd(S["rest"], W["conv_w"], dy_a, name=n("d_conv"), dc=dm.dc)
    dpool, dw["wp"], ds["pool_scale"] = _pool_bwd(S["rest"], W["wp"], W["pool_scale"], dy_c, name=n("d_pool"),
                                                  seg0=dm.r_pool // dm.pg, pg=dm.pg)
    drest = jnp.concatenate([du, db, dc, dpool, dq_lat, dkv_lat, dk_rope], axis=1)
    dw["wg_t"] = _mm(dgl, S["h"], name=n("dw_proj_gates"), ta=True, tm=512, out_dtype=BF16)
    dw["wr_t"] = _mm(drest, S["h"], name=n("dw_proj_rest"), ta=True, tm=512, out_dtype=BF16)
    dh_g = _mm(dgl, W["wg_t"], name=n("d_proj_gates"), tm=704, tk=3072, after=hook("after_dw_in", dw["wr_t"], dw))
    dh = _mm(drest, W["wr_t"], name=n("d_proj_rest"), add=dh_g, tm=704, tk=2688, after=hook("after_dh_gates", dh_g, dw))
    dx, dx_b, ds["attn_norm"] = _rms_bwd(dh, S["x"], W["attn_norm"], name=n("d_attn_norm"), res=dx1, bf16_copy=True)
    return dx, dx_b, dw, ds


BIG = ("w_in", "w_uq", "w_ukv", "pool_w", "w_branch_a", "w_branch_b", "w_branch_c", "w_o", "w_up", "w_down")
REPLICATED = ("attn_norm", "q_lat_norm", "kv_lat_norm", "q_norm", "k_norm", "pool_scale", "mlp_norm")
WEIGHTS = ("meta_tokens", "attn_norm", "w_in", "conv_w", "q_lat_norm", "kv_lat_norm", "w_uq", "w_ukv", "q_norm",
           "k_norm", "pool_w", "pool_scale", "w_branch_a", "w_branch_b", "w_branch_c", "w_o", "mlp_norm", "w_up",
           "w_down")


def _pack(arrays):
    flat = jnp.concatenate([a.reshape(-1).astype(F32) for a in arrays])
    pad = (-flat.shape[0]) % (8 * LANES)
    return jnp.pad(flat, (0, pad)).reshape(-1, LANES)


def _unpack(flat, shapes):
    out, pos = [], 0
    flat = flat.reshape(-1)
    for shp in shapes:
        size = math.prod(shp)
        out.append(flat[pos:pos + size].reshape(shp))
        pos += size
    return out


def _update(w, g, m, v, name):
    shp = w.shape
    to2 = lambda a: a.reshape(-1, shp[-1])
    delta, nm, nv = _adamw(to2(w), to2(g), to2(m), to2(v), name=name)
    return delta.reshape(shp), nm.reshape(shp), nv.reshape(shp)


def _step(args):
    x = args["x"][0]
    seq, d = x.shape
    dm = _Dims(d, seq)
    xi, yi, ci = _coords()
    chip = 2 * xi + yi

    small_w = _gather_all(_pack([args["conv_w"], args["meta_tokens"]]), name="gather_small_weights")
    args = dict(args)
    for p in ("", "m_", "v_"):
        args[p + "w_in"] = jnp.swapaxes(args[p + "w_in"], 1, 2)
    order = [(k, l) for l in range(2) for k in BIG]
    shards = {n: args[n[0]][n[1]].astype(BF16) for n in order}
    for l in range(2):
        shards[("w_in", l)] = _split_cols(shards[("w_in", l)])
    small_w, shards[order[0]] = lax.optimization_barrier((small_w, shards[order[0]]))
    lands = {n: lax.empty((4,) + shards[n].shape, BF16) for n in order}
    last = ("w_up", "w_down")
    group_names = [[("w_in", 0)], [(k, 0) for k in BIG[1:] if k not in last], [(k, 0) for k in last],
                   [(k, 1) for k in BIG]]
    first, others = order[0], order[1:]
    sems, thru, token = _start_copies([shards[first], lands[first]], [_ici_gather_plan([(0, 1)])],
                                      name="start_gather_ici_first")
    shards[first], lands[first] = thru
    at = {n: i for i, n in enumerate(others)}
    sems_b, thru, token_b = _start_copies(
        [shards[n] for n in others] + [lands[n] for n in others] + [token],
        [_ici_gather_plan([(at[n], len(others) + at[n]) for n in g]) for g in group_names[1:]], name="start_gather_ici")
    sems = sems + sems_b
    for i, n in enumerate(others):
        shards[n], lands[n] = thru[i], thru[len(others) + i]

    def finish_gather(g, after, tag):
        names = group_names[g]
        k = len(names)
        plan, _ = _ici_gather_plan([(i, k + i) for i in range(k)])
        got = _wait_copies([shards[n] for n in names] + [lands[n] for n in names], sems[g], plan, after,
                           name=f"wait_gather_ici_{tag}")
        for i, n in enumerate(names):
            shards[n] = got[i]
        fwd = _d2d_forward_plan(list(range(k)))
        sems2, bufs2, tok2 = _start_copies(got[k:], [fwd], name=f"start_gather_d2d_{tag}")
        return names, bufs2, sems2[0], fwd[0], tok2

    def land_gather(pending, after, tag):
        names, bufs2, sems2, plan, tok2 = pending
        done = _wait_copies(bufs2, sems2, plan, tok2 if after is None else after, name=f"wait_gather_d2d_{tag}")
        return {n[0]: buf for n, buf in zip(names, done)}

    conv_shape, meta_shape = args["conv_w"].shape, args["meta_tokens"].shape
    per_chip = [_unpack(small_w[2 * j], [conv_shape, meta_shape]) for j in range(4)]
    conv_full = jnp.concatenate([p[0] for p in per_chip], axis=-1)
    meta_full = jnp.concatenate([p[1] for p in per_chip], axis=-1)

    layers = []
    for l in range(2):
        small = {k: args[k][l] for k in REPLICATED}
        small["conv_w"] = conv_full[l]
        layers.append(_small_weights(small))

    pos = jnp.arange(dm.t, dtype=F32)
    inv = ROPE_THETA ** (-jnp.arange(0, QK_ROPE, 2, dtype=F32) / QK_ROPE)
    ang = pos[:, None] * inv[None, :]
    cos_t = _rope_pad(jnp.cos(ang), jnp.cos(ang))
    sin_t = _rope_pad(-jnp.sin(ang), jnp.sin(ang))
    tail = jnp.zeros((dm.t - dm.t_real, d), F32)
    h0 = jnp.concatenate([meta_full, x, tail], axis=0)
    target = jnp.concatenate([jnp.zeros((N_META, d), F32), args["loss_target"][0], tail], axis=0)

    h_first = _rms_fwd(h0, layers[0]["attn_norm"], name="attn_norm_l0", after=(token, token_b))
    layers[0].update(_in_weights(dm, land_gather(finish_gather(0, h_first, "l0_in"), None, "l0_in")["w_in"]))
    def rest_of_layer0(point, after):
        g, tag = (1, "l0_mid") if point == "after_proj" else (2, "l0_mlp")
        return _other_weights(dm, land_gather(finish_gather(g, after, tag), None, tag))

    h1, saved0 = _layer_fwd(dm, layers[0], h0, cos_t, sin_t, "l0", more=rest_of_layer0, h=h_first)
    g1 = land_gather(finish_gather(3, saved0["y_b"], "l1"), h1, "l1")
    layers[1].update(_in_weights(dm, g1["w_in"]))
    layers[1].update(_other_weights(dm, g1))
    h2, saved1 = _layer_fwd(dm, layers[1], h1, cos_t, sin_t, "l1")
    sq, dy, dy_b = _loss(h2, target, name="loss_head", first=N_META, last=dm.t_real)
    loss = lax.psum(0.5 / d * sq[0, 0], ("x", "y", "c"))
    core, chip_flags = _one_hot(ci, 2), _one_hot(chip, 4)

    class Reduce:
        def __init__(self, names, dw, tag):
            self.names, self.tag, self.nb = names, tag, len(names)
            self.idx = [(i, self.nb + i) for i in range(self.nb)]
            parts = [_as3d(_grad_piece(dm, dw, k)) for k in names]
            recv = [lax.empty((4, p.shape[1] // 2, p.shape[2]), BF16) for p in parts]
            self.plan = _swap_half_plan(self.idx)
            self.sems, self.bufs, self.token = _start_copies(parts + recv, [self.plan], name=f"start_swap_{tag}")

        def _land(self, after, what):
            return _wait_copies(self.bufs, self.sems[0], self.plan[0], self.token if after is None else after,
                                name=f"wait_{what}_{self.tag}")

        def scatter(self, after=None):
            got = self._land(after, "swap")
            pairs = [_pair_sum(got[i], got[j], core, name=f"pair_sum_{k}_{self.tag}")
                     for (i, j), k in zip(self.idx, self.names)]
            self.plan = _scatter_plan(self.idx)
            self.sems, self.bufs, self.token = _start_copies(pairs + [lax.empty(p.shape, BF16) for p in pairs],
                                                             [self.plan], name=f"start_scatter_{self.tag}")
            return self.token

        def totals(self, after=None):
            got = self._land(after, "scatter")
            sums = [_chip_sum(got[i], got[j], chip_flags, name=f"chip_sum_{k}_{self.tag}")
                    for (i, j), k in zip(self.idx, self.names)]
            self.plan = _swap_total_plan(self.idx)
            self.sems, self.bufs, self.token = _start_copies(sums + [lax.empty(t.shape, F32) for t in sums],
                                                             [self.plan], name=f"start_swap_total_{self.tag}")
            return self.token

        def finish(self, after=None):
            got = self._land(after, "swap_total")
            return {k: (got[i], got[j]) for (i, j), k in zip(self.idx, self.names)}

    dh1, dh1_b, dw1, ds1 = _layer_bwd(dm, layers[1], saved1, dy, dy_b, cos_t, sin_t, "l1",
                               hook=lambda point, t, dw: (loss.reshape(1, 1),) if point == "start" else ())
    early = ("w_down", "w_up", "w_o", "w_branch_a", "w_branch_b", "w_branch_c")
    late = tuple(k for k in BIG if k not in early)
    stage = {}

    def during_layer0(point, t, dw):
        if point == "start":
            stage["l1"] = Reduce(BIG, dw1, "l1")
            return (stage["l1"].token,)
        if point == "after_mlp":
            return (stage["l1"].scatter(after=t),)
        if point == "after_attention":
            tok = stage["l1"].totals(after=t)
            stage["l0a"] = Reduce(early, dw, "l0a")
            return (tok, stage["l0a"].token)
        if point == "after_qk":
            stage["red1"] = stage["l1"].finish(after=t)
            return (stage["l0a"].scatter(after=t),)
        if point == "after_dw_in":
            tok = stage["l0a"].totals(after=t)
            stage["l0b"] = Reduce(late, dw, "l0b")
            return (tok, stage["l0b"].token)
        return (stage["l0b"].scatter(after=t),)

    dh0, _, dw0, ds0 = _layer_bwd(dm, layers[0], saved0, dh1, dh1_b, cos_t, sin_t, "l0", hook=during_layer0)
    grad_x = dh0[N_META:dm.t_real][None]
    red1 = stage["red1"]
    grads, delta, new_m, new_v = {}, {}, {}, {}

    def adamw_big(k, layer, red, prev, after):
        shp = args[k].shape
        wmv = [args[p + k].reshape(2, -1, shp[-1]) for p in ("", "m_", "v_")]
        return _adamw_layer(*wmv, *red[k], core, layer, prev, name=f"adamw_{k}_l{layer}", col_halves=k == "w_in",
                            after=after)

    def keep(k, out):
        shp = args[k].shape
        out = [o.reshape(shp) for o in out]
        grads[k], delta[k], new_m[k], new_v[k] = [jnp.swapaxes(o, 1, 2) for o in out] if k == "w_in" else out

    half_done = {}
    pin = dh0
    for k in BIG:
        half_done[k] = adamw_big(k, 1, red1, None, (pin,))
        pin = half_done[k][0]
    red0a = stage["l0a"].finish(after=pin)
    for k in early:
        out = adamw_big(k, 0, red0a, half_done[k], ())
        keep(k, out)
        pin = out[0]

    small_names = REPLICATED + ("conv_w",)
    small_parts = [jnp.stack([ds0[k].reshape(ds0[k].shape[-2:] if k == "conv_w" else (-1,)),
                              ds1[k].reshape(ds1[k].shape[-2:] if k == "conv_w" else (-1,))]) for k in small_names]
    small_parts.append(dh0[:N_META])
    small_all = _gather_all(_pack(small_parts), name="gather_small_grads", after=(pin,))
    small_sum = _sum_stack(small_all, name="sum_small_grads", out_dtype=F32)
    small_g = dict(zip(small_names + ("meta_tokens",), _unpack(small_sum, [p.shape for p in small_parts])))
    for k in REPLICATED:
        grads[k] = small_g[k]
    dcw = conv_shape[-1]
    grads["conv_w"] = lax.dynamic_slice_in_dim(small_g["conv_w"], chip * dcw, dcw, axis=2)
    dmeta = meta_shape[-1]
    grads["meta_tokens"] = lax.dynamic_slice_in_dim(small_g["meta_tokens"], chip * dmeta, dmeta, axis=1)

    stage["l0b"].totals(after=small_sum)
    red0b = stage["l0b"].finish()
    for k in late:
        keep(k, adamw_big(k, 0, red0b, half_done[k], ()))
    for k in WEIGHTS:
        if k not in BIG:
            grads[k] = grads[k].reshape(args[k].shape)
            delta[k], new_m[k], new_v[k] = _update(args[k], grads[k], args["m_" + k], args["v_" + k], f"adamw_{k}")
    return (loss, grad_x, *[grads[k] for k in WEIGHTS], *[delta[k] for k in WEIGHTS],
            *[new_m[k] for k in WEIGHTS], *[new_v[k] for k in WEIGHTS])


def kernel(x, meta_tokens, attn_norm, w_in, conv_w, q_lat_norm, kv_lat_norm, w_uq, w_ukv, q_norm, k_norm, pool_w, pool_scale, w_branch_a, w_branch_b, w_branch_c, w_o, mlp_norm, w_up, w_down, loss_target, m_meta_tokens, m_attn_norm, m_w_in, m_conv_w, m_q_lat_norm, m_kv_lat_norm, m_w_uq, m_w_ukv, m_q_norm, m_k_norm, m_pool_w, m_pool_scale, m_w_branch_a, m_w_branch_b, m_w_branch_c, m_w_o, m_mlp_norm, m_w_up, m_w_down, v_meta_tokens, v_attn_norm, v_w_in, v_conv_w, v_q_lat_norm, v_kv_lat_norm, v_w_uq, v_w_ukv, v_q_norm, v_k_norm, v_pool_w, v_pool_scale, v_w_branch_a, v_w_branch_b, v_w_branch_c, v_w_o, v_mlp_norm, v_w_up, v_w_down):
    return _step(dict(locals()))
```

```python
import functools
import math

import jax
import jax.numpy as jnp
from jax import lax
from jax.experimental import pallas as pl
from jax.experimental.pallas import tpu as pltpu

F32 = jnp.float32
BF16 = jnp.bfloat16
MESH = pl.DeviceIdType.MESH

EPS = 1e-6
N_META = 16
QK_NOPE = 128
QK_ROPE = 64
QK_HEAD = QK_NOPE + QK_ROPE
V_HEAD = 128
HEAD_PAD = 256
Q_LORA = 512
KV_LORA = 512
ROPE_THETA = 10000.0
POOL_WINDOWS = (2, 4, 8, 16)
HALO = 16
LANES = 128
ADAM_LR = 0.001
ADAM_B1 = 0.9
ADAM_B2 = 0.999
ADAM_EPS = 1e-08
ADAM_WD = 0.01
ADAM_STEP = 10
VMEM_LIMIT = 52 * 1024 * 1024
NEG = -1e30
ATTN_SCALE = QK_HEAD ** -0.5
LOG2_E = 1.4426950408889634
Q_FOLD = ATTN_SCALE * LOG2_E


def _tile(n, target, mult=LANES):
    best = None
    for t in range(mult, min(n, target) + 1, mult):
        if n % t == 0:
            best = t
    return n if best is None else best


def _params(sem=None):
    return pltpu.CompilerParams(dimension_semantics=sem, vmem_limit_bytes=VMEM_LIMIT)


def _mm(a, b, *, name, ta=False, tb=False, add=None, aux=None, epi=None, out_dtype=F32,
        tm=1056, tn=1024, tk=None, after=(), pieces=None):
    if ta:
        K, M = a.shape
    else:
        M, K = a.shape
    if tb:
        N, kb = b.shape
    else:
        kb, N = b.shape
    assert K == kb, (a.shape, b.shape, ta, tb)
    tm = _tile(M, tm, LANES if ta else 16)
    tn = _tile(N if pieces is None else N // pieces, tn, LANES)
    tk = K if tk is None else _tile(K, tk, LANES if (not ta or tb) else 16)
    nk = K // tk
    a_bytes, b_bytes = a.size * a.dtype.itemsize, b.size * b.dtype.itemsize
    j_outer = nk == 1 and a_bytes * (N // tn) + b_bytes < a_bytes + b_bytes * (M // tm)
    grid = (N // tn, M // tm, nk) if j_outer else (M // tm, N // tn, nk)
    row = (lambda g0, g1: g1) if j_outer else (lambda g0, g1: g0)
    col = (lambda g0, g1: g0) if j_outer else (lambda g0, g1: g1)

    if ta:
        a_spec = pl.BlockSpec((tk, tm), lambda g0, g1, k: (k, row(g0, g1)))
    else:
        a_spec = pl.BlockSpec((tm, tk), lambda g0, g1, k: (row(g0, g1), k))
    if tb:
        b_spec = pl.BlockSpec((tn, tk), lambda g0, g1, k: (col(g0, g1), k))
    else:
        b_spec = pl.BlockSpec((tk, tn), lambda g0, g1, k: (k, col(g0, g1)))
    o_spec = pl.BlockSpec((tm, tn), lambda g0, g1, k: (row(g0, g1), col(g0, g1)))
    per = None if pieces is None else N // pieces // tn
    in_specs = [a_spec, b_spec]
    operands = [a, b]
    if add is not None:
        in_specs.append(o_spec)
        operands.append(add)
    if aux is not None:
        in_specs.append(o_spec)
        operands.append(aux)
    after = tuple(after)
    in_specs += [pl.BlockSpec(memory_space=pl.ANY)] * len(after)
    operands += list(after)
    if epi == "relu2":
        out_shape = (jax.ShapeDtypeStruct((M, N), BF16), jax.ShapeDtypeStruct((M, N), BF16))
        out_specs = (o_spec, o_spec)
    elif pieces is not None:
        out_shape = jax.ShapeDtypeStruct((pieces, M, N // pieces), out_dtype)
        out_specs = pl.BlockSpec((1, tm, tn), lambda g0, g1, k: (col(g0, g1) // per, row(g0, g1), col(g0, g1) % per))
    else:
        out_shape = jax.ShapeDtypeStruct((M, N), out_dtype)
        out_specs = o_spec
    dims =(((0 if ta else 1,), (1 if tb else 0,)), ((), ()))
    has_add, has_aux = add is not None, aux is not None

    def body(*refs):
        a_ref, b_ref = refs[0], refs[1]
        pos = 2
        add_ref = aux_ref = None
        if has_add:
            add_ref = refs[pos]
            pos += 1
        if has_aux:
            aux_ref = refs[pos]
            pos += 1
        pos += len(after)
        n_out = 2 if epi == "relu2" else 1
        out_refs = refs[pos:pos + n_out]
        acc_ref = refs[pos + n_out] if nk > 1 else None

        part = lax.dot_general(a_ref[...].astype(BF16), b_ref[...].astype(BF16), dims,
                               preferred_element_type=F32)

        def finish(acc):
            if has_add:
                acc = acc + add_ref[...].astype(F32)
            if epi == "relu2":
                r = jnp.maximum(acc, 0.0)
                out_refs[0][...] = acc.astype(BF16)
                out_refs[1][...] = (r * r).astype(BF16)
            elif epi == "drelu2":
                u = aux_ref[...].astype(F32)
                out_refs[0][...] = (acc * (2.0 * jnp.maximum(u, 0.0))).astype(out_dtype)
            else:
                out_refs[0][...] = acc.astype(out_dtype).reshape(out_refs[0].shape)

        if nk == 1:
            finish(part)
        else:
            k = pl.program_id(2)

            @pl.when(k == 0)
            def _():
                acc_ref[...] = part

            @pl.when(k > 0)
            def _():
                acc_ref[...] += part

            @pl.when(k == nk - 1)
            def _():
                finish(acc_ref[...])

    scratch = [pltpu.VMEM((tm, tn), F32)] if nk > 1 else []
    return pl.pallas_call(
        body, name=name, grid=grid, in_specs=in_specs, out_specs=out_specs, out_shape=out_shape,
        scratch_shapes=scratch, compiler_params=_params(("parallel", "parallel", "arbitrary")),
    )(*operands)


def _rms_fwd(x, g, *, name, width=None, seg=0, tm=384, after=()):
    T = x.shape[0]
    width = x.shape[1] if width is None else width
    tm = _tile(T, tm, 16)
    after = tuple(after)

    def body(x_ref, g_ref, *rest):
        xf = x_ref[...].astype(F32)
        r = lax.rsqrt(jnp.mean(xf * xf, axis=-1, keepdims=True) + EPS)
        rest[-1][...] = (xf * r * g_ref[...]).astype(BF16)

    return pl.pallas_call(
        body, name=name, grid=(T // tm,),
        in_specs=[pl.BlockSpec((tm, width), lambda i: (i, seg)), pl.BlockSpec((1, width), lambda i: (0, 0))]
        + [pl.BlockSpec(memory_space=pl.ANY)] * len(after),
        out_specs=pl.BlockSpec((tm, width), lambda i: (i, 0)),
        out_shape=jax.ShapeDtypeStruct((T, width), BF16),
        compiler_params=_params(("parallel",)),
    )(x, g, *after)


def _rms_bwd(dy, x, g, *, name, width=None, seg=0, res=None, out_dtype=F32, tm=384, bf16_copy=False):
    T = x.shape[0]
    width = x.shape[1] if width is None else width
    tm = _tile(T, tm, 16)
    has_res = res is not None

    def body(*refs):
        dy_ref, x_ref, g_ref = refs[:3]
        res_ref = refs[3] if has_res else None
        dx_ref, dg_ref = refs[4 if has_res else 3], refs[-1]
        xf = x_ref[...].astype(F32)
        dyf = dy_ref[...].astype(F32)
        r = lax.rsqrt(jnp.mean(xf * xf, axis=-1, keepdims=True) + EPS)
        xhat = xf * r
        dyh = dyf * g_ref[...]
        dx = r * (dyh - xhat * jnp.mean(dyh * xhat, axis=-1, keepdims=True))
        if has_res:
            dx = dx + res_ref[...].astype(F32)
        dx_ref[...] = dx.astype(out_dtype)
        if bf16_copy:
            refs[-2][...] = dx.astype(BF16)
        part = jnp.sum(dyf * xhat, axis=0, keepdims=True)

        @pl.when(pl.program_id(0) == 0)
        def _():
            dg_ref[...] = part

        @pl.when(pl.program_id(0) > 0)
        def _():
            dg_ref[...] += part

    row = pl.BlockSpec((tm, width), lambda i: (i, 0))
    in_specs = [row, pl.BlockSpec((tm, width), lambda i: (i, seg)), pl.BlockSpec((1, width), lambda i: (0, 0))]
    operands = [dy, x, g]
    if has_res:
        in_specs.append(row)
        operands.append(res)
    vec = pl.BlockSpec((1, width), lambda i: (0, 0))
    full = [jax.ShapeDtypeStruct((T, width), out_dtype)] + ([jax.ShapeDtypeStruct((T, width), BF16)] if bf16_copy else [])
    return pl.pallas_call(
        body, name=name, grid=(T // tm,), in_specs=in_specs,
        out_specs=tuple([row] * len(full) + [vec]),
        out_shape=tuple(full + [jax.ShapeDtypeStruct((1, width), F32)]),
        compiler_params=_params(("arbitrary",)),
    )(*operands)


def _down(ext, k):
    return pltpu.roll(ext, k, 0)


def _up(ext, k):
    return pltpu.roll(ext, ext.shape[0] - k, 0)


def _pre_halo(ref, r, R):
    start = pl.multiple_of(jnp.maximum(r * R - HALO, 0), 8)
    keep = (r > 0).astype(F32)
    return ref[pl.ds(start, HALO), :].astype(F32) * keep


def _post_halo(ref, r, R, n_chunks):
    start = pl.multiple_of(jnp.minimum(r * R + R, (n_chunks - 1) * R + R - HALO), 8)
    keep = (r < n_chunks - 1).astype(F32)
    return ref[pl.ds(start, HALO), :].astype(F32) * keep


def _chunk(ref, r, R):
    return ref[pl.ds(pl.multiple_of(r * R, 8), R), :].astype(F32)


def _conv_fwd(rest, conv_w, *, name, dc, tc=128, rows=1056):
    T = rest.shape[0]
    tc = _tile(dc, tc)
    nb = dc // tc
    R = _tile(T, rows, 16)
    n_chunks = T // R

    def body(u_ref, b_ref, c_ref, w_ref, y_ref):
        w0, w1, w2 = w_ref[0:1, :], w_ref[1:2, :], w_ref[2:3, :]

        def chunk(r, carry):
            cu = _chunk(c_ref, r, R) * _chunk(u_ref, r, R)
            ext = jnp.concatenate([_pre_halo(c_ref, r, R) * _pre_halo(u_ref, r, R), cu], axis=0)
            conv = w0 * _down(ext, 2)[HALO:] + w1 * _down(ext, 1)[HALO:] + w2 * cu
            y_ref[pl.ds(pl.multiple_of(r * R, 8), R), :] = (_chunk(b_ref, r, R) * conv).astype(BF16)
            return carry

        lax.fori_loop(0, n_chunks, chunk, 0)

    col = lambda off: pl.BlockSpec((T, tc), lambda j: (0, off * nb + j))
    return pl.pallas_call(
        body, name=name, grid=(nb,),
        in_specs=[col(0), col(1), col(2), pl.BlockSpec((3, tc), lambda j: (0, j))],
        out_specs=pl.BlockSpec((T, tc), lambda j: (0, j)),
        out_shape=jax.ShapeDtypeStruct((T, dc), BF16),
        compiler_params=_params(("parallel",)),
    )(rest, rest, rest, conv_w)


def _conv_bwd(rest, conv_w, dy, *, name, dc, tc=128, rows=1056):
    T = rest.shape[0]
    tc = _tile(dc, tc)
    nb = dc // tc
    R = _tile(T, rows, 16)
    n_chunks = T // R

    def body(u_ref, b_ref, c_ref, w_ref, dy_ref, du_ref, db_ref, dc_ref, dw_ref):
        w0, w1, w2 = w_ref[0:1, :], w_ref[1:2, :], w_ref[2:3, :]

        def chunk(r, carry):
            a0, a1, a2 = carry
            u, b, c = _chunk(u_ref, r, R), _chunk(b_ref, r, R), _chunk(c_ref, r, R)
            dy_c = _chunk(dy_ref, r, R)
            cu = c * u
            ext = jnp.concatenate([_pre_halo(c_ref, r, R) * _pre_halo(u_ref, r, R), cu], axis=0)
            cu1, cu2 = _down(ext, 1)[HALO:], _down(ext, 2)[HALO:]
            conv = w0 * cu2 + w1 * cu1 + w2 * cu
            dconv = dy_c * b
            dext = jnp.concatenate(
                [dconv, _post_halo(dy_ref, r, R, n_chunks) * _post_halo(b_ref, r, R, n_chunks)], axis=0)
            dcu = w2 * dconv + w1 * _up(dext, 1)[:R] + w0 * _up(dext, 2)[:R]
            rows_at = pl.ds(pl.multiple_of(r * R, 8), R)
            db_ref[rows_at, :] = (dy_c * conv).astype(BF16)
            du_ref[rows_at, :] = (dcu * c).astype(BF16)
            dc_ref[rows_at, :] = (dcu * u).astype(BF16)
            return (a0 + jnp.sum(dconv * cu2, axis=0, keepdims=True),
                    a1 + jnp.sum(dconv * cu1, axis=0, keepdims=True),
                    a2 + jnp.sum(dconv * cu, axis=0, keepdims=True))

        zero = jnp.zeros((1, tc), F32)
        a0, a1, a2 = lax.fori_loop(0, n_chunks, chunk, (zero, zero, zero))
        dw_ref[0:1, :] = a0
        dw_ref[1:2, :] = a1
        dw_ref[2:3, :] = a2

    col = lambda off: pl.BlockSpec((T, tc), lambda j: (0, off * nb + j))
    own = pl.BlockSpec((T, tc), lambda j: (0, j))
    return pl.pallas_call(
        body, name=name, grid=(nb,),
        in_specs=[col(0), col(1), col(2), pl.BlockSpec((3, tc), lambda j: (0, j)), own],
        out_specs=(own, own, own, pl.BlockSpec((3, tc), lambda j: (0, j))),
        out_shape=(jax.ShapeDtypeStruct((T, dc), BF16),) * 3 + (jax.ShapeDtypeStruct((3, dc), F32),),
        compiler_params=_params(("parallel",)),
    )(rest, rest, rest, conv_w, dy)


def _window_count(r, R, n_rows, w, first_row_offset):
    t = lax.broadcasted_iota(jnp.int32, (n_rows, 1), 0) + (r * R + first_row_offset)
    return jnp.minimum(t + 1, w).astype(F32)


def _pool_fwd(rest, pool_w, pool_scale, *, name, seg0, pg, rows=1056):
    T = rest.shape[0]
    R = _tile(T, rows, 16)
    n_chunks = T // R
    n_groups = len(POOL_WINDOWS)

    def body(x_ref, w_ref, s_ref, y_ref):
        def run(window):
            def chunk(r, carry):
                g = _chunk(x_ref, r, R)
                s = jnp.concatenate([_pre_halo(x_ref, r, R), g], axis=0)
                k = 1
                while k < window:
                    s = s + _down(s, k)
                    k *= 2
                pooled = s[HALO:] / _window_count(r, R, R, window, 0) - g
                mixed = jnp.dot(pooled.astype(BF16), w_ref[0], preferred_element_type=F32)
                y_ref[pl.ds(pl.multiple_of(r * R, 8), R), :] = (mixed * s_ref[...]).astype(BF16)
                return carry

            lax.fori_loop(0, n_chunks, chunk, 0)

        for gi, window in enumerate(POOL_WINDOWS):
            pl.when(pl.program_id(0) == gi)(functools.partial(run, window))

    return pl.pallas_call(
        body, name=name, grid=(n_groups,),
        in_specs=[pl.BlockSpec((T, pg), lambda g: (0, seg0 + g)),
                  pl.BlockSpec((1, pg, pg), lambda g: (g, 0, 0)),
                  pl.BlockSpec((1, pg), lambda g: (0, g))],
        out_specs=pl.BlockSpec((T, pg), lambda g: (0, g)),
        out_shape=jax.ShapeDtypeStruct((T, n_groups * pg), BF16),
        compiler_params=_params(("parallel",)),
    )(rest, pool_w, pool_scale)


def _pool_bwd(rest, pool_w, pool_scale, dy, *, name, seg0, pg, rows=1056):
    T = rest.shape[0]
    R = _tile(T, rows, 16)
    n_chunks = T // R
    n_groups = len(POOL_WINDOWS)

    def body(x_ref, w_ref, s_ref, dy_ref, dx_ref, dw_ref, ds_ref):
        def run(window):
            def chunk(r, carry):
                dw_acc, ds_acc = carry
                g = _chunk(x_ref, r, R)
                s = jnp.concatenate([_pre_halo(x_ref, r, R), g], axis=0)
                k = 1
                while k < window:
                    s = s + _down(s, k)
                    k *= 2
                pooled = (s[HALO:] / _window_count(r, R, R, window, 0) - g).astype(BF16)
                mixed = jnp.dot(pooled, w_ref[0], preferred_element_type=F32)
                dy_c = _chunk(dy_ref, r, R)
                dm_ext = (jnp.concatenate([dy_c, _post_halo(dy_ref, r, R, n_chunks)], axis=0)
                          * s_ref[...]).astype(BF16)
                dpool_ext = lax.dot_general(dm_ext, w_ref[0], (((1,), (1,)), ((), ())),
                                            preferred_element_type=F32)
                a = dpool_ext / _window_count(r, R, R + HALO, window, 0)
                k = 1
                while k < window:
                    a = a + _up(a, k)
                    k *= 2
                dx_ref[pl.ds(pl.multiple_of(r * R, 8), R), :] = (a[:R] - dpool_ext[:R]).astype(BF16)
                dw_acc = dw_acc + lax.dot_general(pooled, dm_ext[:R], (((0,), (0,)), ((), ())),
                                                  preferred_element_type=F32)
                ds_acc = ds_acc + jnp.sum(dy_c * mixed, axis=0, keepdims=True)
                return dw_acc, ds_acc

            dw_acc, ds_acc = lax.fori_loop(0, n_chunks, chunk,
                                           (jnp.zeros((pg, pg), F32), jnp.zeros((1, pg), F32)))
            dw_ref[0] = dw_acc
            ds_ref[...] = ds_acc

        for gi, window in enumerate(POOL_WINDOWS):
            pl.when(pl.program_id(0) == gi)(functools.partial(run, window))

    own = pl.BlockSpec((T, pg), lambda g: (0, g))
    return pl.pallas_call(
        body, name=name, grid=(n_groups,),
        in_specs=[pl.BlockSpec((T, pg), lambda g: (0, seg0 + g)),
                  pl.BlockSpec((1, pg, pg), lambda g: (g, 0, 0)),
                  pl.BlockSpec((1, pg), lambda g: (0, g)), own],
        out_specs=(own, pl.BlockSpec((1, pg, pg), lambda g: (g, 0, 0)), pl.BlockSpec((1, pg), lambda g: (0, g))),
        out_shape=(jax.ShapeDtypeStruct((T, n_groups * pg), BF16),
                   jax.ShapeDtypeStruct((n_groups, pg, pg), F32),
                   jax.ShapeDtypeStruct((1, n_groups * pg), F32)),
        compiler_params=_params(("parallel",)),
    )(rest, pool_w, pool_scale, dy)


def _rope(r, cos_t, sin_t):
    return r * cos_t + pltpu.roll(r, LANES // 2, 1) * sin_t


def _rope_t(d, cos_t, sin_t):
    return d * cos_t + pltpu.roll(d * sin_t, LANES // 2, 1)


def _qk_fwd(q_raw, k_nope, rest, cos_t, sin_t, q_norm, k_norm, *, name, heads, kr_seg, tm=192):
    T = q_raw.shape[0]
    tm = _tile(T, tm, 16)

    def body(q_ref, kn_ref, kr_ref, c_ref, s_ref, gq_ref, gk_ref, qo_ref, ko_ref):
        cos_b, sin_b = c_ref[...], s_ref[...]
        kr = kr_ref[:, 0:LANES]
        kr_ss = jnp.sum(kr * kr, axis=-1, keepdims=True)
        gq, gk = gq_ref[...], gk_ref[...]
        for h in range(heads):
            lo = h * HEAD_PAD
            q = q_ref[:, lo:lo + HEAD_PAD]
            rq = lax.rsqrt(jnp.sum(q * q, axis=-1, keepdims=True) / QK_HEAD + EPS)
            qn = q * (rq * Q_FOLD) * gq
            qo_ref[:, lo:lo + LANES] = qn[:, :LANES].astype(BF16)
            qo_ref[:, lo + LANES:lo + HEAD_PAD] = _rope(qn[:, LANES:], cos_b, sin_b).astype(BF16)
            kn = kn_ref[:, h * LANES:(h + 1) * LANES]
            rk = lax.rsqrt((jnp.sum(kn * kn, axis=-1, keepdims=True) + kr_ss) / QK_HEAD + EPS)
            ko_ref[:, lo:lo + LANES] = (kn * rk * gk[:, :LANES]).astype(BF16)
            ko_ref[:, lo + LANES:lo + HEAD_PAD] = _rope(kr * rk * gk[:, LANES:], cos_b, sin_b).astype(BF16)

    wq, wk = heads * HEAD_PAD, heads * LANES
    return pl.pallas_call(
        body, name=name, grid=(T // tm,),
        in_specs=[pl.BlockSpec((tm, wq), lambda i: (i, 0)), pl.BlockSpec((tm, wk), lambda i: (i, 0)),
                  pl.BlockSpec((tm, HEAD_PAD), lambda i: (i, kr_seg)),
                  pl.BlockSpec((tm, LANES), lambda i: (i, 0)), pl.BlockSpec((tm, LANES), lambda i: (i, 0)),
                  pl.BlockSpec((1, HEAD_PAD), lambda i: (0, 0)), pl.BlockSpec((1, HEAD_PAD), lambda i: (0, 0))],
        out_specs=(pl.BlockSpec((tm, wq), lambda i: (i, 0)), pl.BlockSpec((tm, wq), lambda i: (i, 0))),
        out_shape=(jax.ShapeDtypeStruct((T, wq), BF16), jax.ShapeDtypeStruct((T, wq), BF16)),
        compiler_params=_params(("parallel",)),
    )(q_raw, k_nope, rest, cos_t, sin_t, q_norm, k_norm)


def _qk_bwd(dq, dk, q_raw, k_nope, rest, cos_t, sin_t, q_norm, k_norm, *, name, heads, kr_seg, tm=128):
    T = q_raw.shape[0]
    tm = _tile(T, tm, 16)

    def body(dq_ref, dk_ref, q_ref, kn_ref, kr_ref, c_ref, s_ref, gq_ref, gk_ref,
             dqr_ref, dkn_ref, dkr_ref, dgq_ref, dgk_ref):
        cos_b, sin_b = c_ref[...], s_ref[...]
        kr = kr_ref[:, 0:LANES]
        kr_ss = jnp.sum(kr * kr, axis=-1, keepdims=True)
        gq, gk = gq_ref[...], gk_ref[...]
        dgq = jnp.zeros((1, HEAD_PAD), F32)
        dgk_n = jnp.zeros((1, LANES), F32)
        dgk_r = jnp.zeros((1, LANES), F32)
        dkr = jnp.zeros((tm, LANES), F32)
        for h in range(heads):
            lo = h * HEAD_PAD
            q = q_ref[:, lo:lo + HEAD_PAD]
            rq = lax.rsqrt(jnp.sum(q * q, axis=-1, keepdims=True) / QK_HEAD + EPS)
            qhat = q * rq
            dqn = jnp.concatenate([dq_ref[:, lo:lo + LANES],
                                   _rope_t(dq_ref[:, lo + LANES:lo + HEAD_PAD], cos_b, sin_b)], axis=1) * ATTN_SCALE
            dgq = dgq + jnp.sum(dqn * qhat, axis=0, keepdims=True)
            dqh = dqn * gq
            dqr_ref[:, lo:lo + HEAD_PAD] = (
                rq * (dqh - qhat * (jnp.sum(dqh * qhat, axis=-1, keepdims=True) / QK_HEAD))).astype(BF16)
            kn = kn_ref[:, h * LANES:(h + 1) * LANES]
            rk = lax.rsqrt((jnp.sum(kn * kn, axis=-1, keepdims=True) + kr_ss) / QK_HEAD + EPS)
            khat_n, khat_r = kn * rk, kr * rk
            dkn_n = dk_ref[:, lo:lo + LANES] * (1.0 / LOG2_E)
            dkn_r = _rope_t(dk_ref[:, lo + LANES:lo + HEAD_PAD], cos_b, sin_b) * (1.0 / LOG2_E)
            dgk_n = dgk_n + jnp.sum(dkn_n * khat_n, axis=0, keepdims=True)
            dgk_r = dgk_r + jnp.sum(dkn_r * khat_r, axis=0, keepdims=True)
            dkh_n, dkh_r = dkn_n * gk[:, :LANES], dkn_r * gk[:, LANES:]
            proj = (jnp.sum(dkh_n * khat_n, axis=-1, keepdims=True)
                    + jnp.sum(dkh_r * khat_r, axis=-1, keepdims=True)) / QK_HEAD
            dkn_ref[:, h * LANES:(h + 1) * LANES] = (rk * (dkh_n - khat_n * proj)).astype(BF16)
            dkr = dkr + rk * (dkh_r - khat_r * proj)
        dkr_ref[:, 0:LANES] = dkr.astype(BF16)
        dkr_ref[:, LANES:HEAD_PAD] = jnp.zeros((tm, HEAD_PAD - LANES), BF16)
        dgk = jnp.concatenate([dgk_n, dgk_r], axis=1)

        @pl.when(pl.program_id(0) == 0)
        def _():
            dgq_ref[...] = dgq
            dgk_ref[...] = dgk

        @pl.when(pl.program_id(0) > 0)
        def _():
            dgq_ref[...] += dgq
            dgk_ref[...] += dgk

    wq, wk = heads * HEAD_PAD, heads * LANES
    row = lambda w: pl.BlockSpec((tm, w), lambda i: (i, 0))
    vec = pl.BlockSpec((1, HEAD_PAD), lambda i: (0, 0))
    return pl.pallas_call(
        body, name=name, grid=(T // tm,),
        in_specs=[row(wq), row(wq), row(wq), row(wk), pl.BlockSpec((tm, HEAD_PAD), lambda i: (i, kr_seg)),
                  row(LANES), row(LANES), vec, vec],
        out_specs=(row(wq), row(wk), row(HEAD_PAD), vec, vec),
        out_shape=(jax.ShapeDtypeStruct((T, wq), BF16), jax.ShapeDtypeStruct((T, wk), BF16),
                   jax.ShapeDtypeStruct((T, HEAD_PAD), BF16),
                   jax.ShapeDtypeStruct((1, HEAD_PAD), F32), jax.ShapeDtypeStruct((1, HEAD_PAD), F32)),
        compiler_params=_params(("arbitrary",)),
    )(dq, dk, q_raw, k_nope, rest, cos_t, sin_t, q_norm, k_norm)


def _causal_mask(s):
    row = lax.broadcasted_iota(jnp.int32, s.shape, 0)
    col = lax.broadcasted_iota(jnp.int32, s.shape, 1)
    return jnp.where(row >= col, s, NEG)


def _flash_fwd(q, k, v, *, name, heads, tq=384, hp=2, parts=2):
    T = q.shape[0]
    tq = _tile(T, tq, LANES)
    nq = T // tq
    tr = tq // parts
    nt = (((1,), (1,)), ((), ()))
    chains = [(h, r) for h in range(hp) for r in range(parts)]

    def body(q_ref, k_ref, v_ref, o_ref, lse_ref, acc_ref):
        def q_block(i, carry):
            rows_at = [pl.ds(pl.multiple_of(i * tq + r * tr, tr), tr) for r in range(parts)]
            qbs = [q_ref[rows_at[r], h * HEAD_PAD:(h + 1) * HEAD_PAD] for h, r in chains]
            for c in range(len(chains)):
                acc_ref[c] = jnp.zeros((tr, V_HEAD), F32)

            def step(j, state, masked):
                k_at = pl.ds(pl.multiple_of(j * tq, tq), tq)
                new = []
                scores = [lax.dot_general(qb, k_ref[k_at, h * HEAD_PAD:(h + 1) * HEAD_PAD], nt,
                                          preferred_element_type=F32) for qb, (h, r) in zip(qbs, chains)]
                for c, (s, (h, r)) in enumerate(zip(scores, chains)):
                    m, l = state[c]
                    if masked:
                        row = lax.broadcasted_iota(jnp.int32, s.shape, 0) + r * tr
                        s = jnp.where(row >= lax.broadcasted_iota(jnp.int32, s.shape, 1), s, NEG)
                    m_new = jnp.maximum(m, jnp.max(s, axis=-1, keepdims=True))
                    p = jnp.exp2(s - m_new)
                    alpha = jnp.exp2(m - m_new)
                    new.append((m_new, alpha * l + jnp.sum(p, axis=-1, keepdims=True)))
                    acc_ref[c] = alpha * acc_ref[c] + jnp.dot(p.astype(BF16), v_ref[k_at, h * V_HEAD:(h + 1) * V_HEAD],
                                                              preferred_element_type=F32)
                return tuple(new)

            init = tuple((jnp.full((tr, 1), NEG, F32), jnp.zeros((tr, 1), F32)) for _ in chains)
            state = lax.fori_loop(0, i, lambda j, st: step(j, st, False), init)
            state = step(i, state, True)
            for c, ((m, l), (h, r)) in enumerate(zip(state, chains)):
                o_ref[rows_at[r], h * V_HEAD:(h + 1) * V_HEAD] = (acc_ref[c] / l).astype(BF16)
                lse_ref[h, rows_at[r], :] = jnp.broadcast_to(m + jnp.log2(l), (tr, LANES))
            return carry

        lax.fori_loop(0, nq, q_block, 0)

    qk_spec = pl.BlockSpec((T, hp * HEAD_PAD), lambda g: (0, g))
    v_spec = pl.BlockSpec((T, hp * V_HEAD), lambda g: (0, g))
    return pl.pallas_call(
        body, name=name, grid=(heads // hp,), in_specs=[qk_spec, qk_spec, v_spec],
        out_specs=(v_spec, pl.BlockSpec((hp, T, LANES), lambda g: (g, 0, 0))),
        out_shape=(jax.ShapeDtypeStruct((T, heads * V_HEAD), BF16), jax.ShapeDtypeStruct((heads, T, LANES), F32)),
        scratch_shapes=[pltpu.VMEM((len(chains), tr, V_HEAD), F32)],
        compiler_params=_params(("parallel",)),
    )(q, k, v)


def _flash_bwd(q, k, v, o, do, lse, *, name, heads, tq=384):
    T = q.shape[0]
    tq = _tile(T, tq, LANES)
    nq = T // tq
    nt = (((1,), (1,)), ((), ()))
    tn = (((0,), (0,)), ((), ()))

    def body(q_ref, k_ref, v_ref, o_ref, do_ref, lse_ref, dq_ref, dk_ref, dv_ref, delta_ref, dv_acc_ref):
        def fill_delta(i, carry):
            at = pl.ds(pl.multiple_of(i * tq, tq), tq)
            d = jnp.sum(o_ref[at, :].astype(F32) * do_ref[at, :].astype(F32), axis=-1, keepdims=True)
            delta_ref[at, :] = jnp.broadcast_to(d, (tq, LANES))
            dq_ref[at, :] = jnp.zeros((tq, HEAD_PAD), F32)
            return carry

        lax.fori_loop(0, nq, fill_delta, 0)

        def kv_block(j, carry):
            k_at = pl.ds(pl.multiple_of(j * tq, tq), tq)
            kb, vb = k_ref[k_at, :], v_ref[k_at, :]

            def steps(blocks, masked):
                at = [pl.ds(pl.multiple_of(i * tq, tq), tq) for i in blocks]
                qbs = [q_ref[a, :] for a in at]
                dobs = [do_ref[a, :] for a in at]
                scores = [lax.dot_general(qb, kb, nt, preferred_element_type=F32) for qb in qbs]
                dps = [lax.dot_general(dob, vb, nt, preferred_element_type=F32) for dob in dobs]
                for a, qb, dob, sc, dp in zip(at, qbs, dobs, scores, dps):
                    if masked:
                        sc = _causal_mask(sc)
                    p = jnp.exp2(sc - lse_ref[0, a, :][:, 0:1])
                    ds = (p * (dp - delta_ref[a, :][:, 0:1])).astype(BF16)
                    dv_part = lax.dot_general(p.astype(BF16), dob, tn, preferred_element_type=F32)
                    dk_part = lax.dot_general(ds, qb, tn, preferred_element_type=F32)
                    if masked:
                        dv_acc_ref[...] = dv_part
                        dk_ref[k_at, :] = dk_part
                    else:
                        dv_acc_ref[...] += dv_part
                        dk_ref[k_at, :] += dk_part
                    dq_ref[a, :] += jnp.dot(ds, kb, preferred_element_type=F32)

            def two_blocks(t, carry):
                steps([j + 1 + 2 * t, j + 2 + 2 * t], False)
                return carry

            steps([j], True)
            rest = nq - 1 - j
            lax.fori_loop(0, rest // 2, two_blocks, 0)

            @pl.when(rest % 2 == 1)
            def _():
                steps([nq - 1], False)

            dv_ref[k_at, :] = dv_acc_ref[...].astype(BF16)
            return carry

        lax.fori_loop(0, nq, kv_block, 0)

    qk_spec = pl.BlockSpec((T, HEAD_PAD), lambda h: (0, h))
    v_spec = pl.BlockSpec((T, V_HEAD), lambda h: (0, h))
    return pl.pallas_call(
        body, name=name, grid=(heads,),
        in_specs=[qk_spec, qk_spec, v_spec, v_spec, v_spec, pl.BlockSpec((1, T, LANES), lambda h: (h, 0, 0))],
        out_specs=(qk_spec, qk_spec, v_spec),
        out_shape=(jax.ShapeDtypeStruct((T, heads * HEAD_PAD), F32), jax.ShapeDtypeStruct((T, heads * HEAD_PAD), F32),
                   jax.ShapeDtypeStruct((T, heads * V_HEAD), BF16)),
        scratch_shapes=[pltpu.VMEM((T, LANES), F32), pltpu.VMEM((tq, V_HEAD), F32)],
        compiler_params=_params(("parallel",)),
    )(q, k, v, o, do, lse)


def _merge_fwd(gl, pa, pb, pc, *, name, d, tm=384, tn=1024):
    T = pa.shape[0]
    tm, tn = _tile(T, tm, 16), _tile(d, tn)
    nb = d // tn

    def body(g0, g1, g2, a, b, c, o_ref):
        o_ref[...] = (jax.nn.sigmoid(g0[...]) * a[...] + jax.nn.sigmoid(g1[...]) * b[...]
                      + jax.nn.sigmoid(g2[...]) * c[...]).astype(BF16)

    gate = lambda n: pl.BlockSpec((tm, tn), lambda i, j: (i, n * nb + j))
    blk = pl.BlockSpec((tm, tn), lambda i, j: (i, j))
    return pl.pallas_call(
        body, name=name, grid=(T // tm, nb), in_specs=[gate(0), gate(1), gate(2), blk, blk, blk],
        out_specs=blk, out_shape=jax.ShapeDtypeStruct((T, d), BF16),
        compiler_params=_params(("parallel", "parallel")),
    )(gl, gl, gl, pa, pb, pc)


def _merge_bwd(dm, gl, pa, pb, pc, *, name, d, tm=384, tn=1024):
    T = pa.shape[0]
    tm, tn = _tile(T, tm, 16), _tile(d, tn)
    nb = d // tn

    def body(dm_ref, g0, g1, g2, a, b, c, da, db, dc, dg0, dg1, dg2):
        dmv = dm_ref[...]
        for g_ref, p_ref, dp_ref, dg_ref in ((g0, a, da, dg0), (g1, b, db, dg1), (g2, c, dc, dg2)):
            sg = jax.nn.sigmoid(g_ref[...])
            dp_ref[...] = (dmv * sg).astype(BF16)
            dg_ref[...] = (dmv * p_ref[...] * sg * (1.0 - sg)).astype(BF16)

    gate = lambda n: pl.BlockSpec((tm, tn), lambda i, j: (i, n * nb + j))
    blk = pl.BlockSpec((tm, tn), lambda i, j: (i, j))
    return pl.pallas_call(
        body, name=name, grid=(T // tm, nb), in_specs=[blk, gate(0), gate(1), gate(2), blk, blk, blk],
        out_specs=(blk,) * 6, out_shape=(jax.ShapeDtypeStruct((T, d), BF16),) * 6,
        compiler_params=_params(("parallel", "parallel")),
    )(dm, gl, gl, gl, pa, pb, pc)


def _loss(y, target, *, name, first, last, tm=384):
    T, d = y.shape
    tm = _tile(T, tm, 16)

    def body(y_ref, t_ref, loss_ref, dy_ref, dyb_ref):
        i = pl.program_id(0)
        row = lax.broadcasted_iota(jnp.int32, (tm, 1), 0) + i * tm
        real = jnp.logical_and(row >= first, row < last)
        err = jnp.where(real, y_ref[...] - t_ref[...], 0.0)
        dy_ref[...] = err * (1.0 / d)
        dyb_ref[...] = (err * (1.0 / d)).astype(BF16)
        part = jnp.broadcast_to(jnp.sum(err * err, keepdims=True).reshape(1, 1), (1, LANES))

        @pl.when(i == 0)
        def _():
            loss_ref[...] = part

        @pl.when(i > 0)
        def _():
            loss_ref[...] += part

    blk = pl.BlockSpec((tm, d), lambda i: (i, 0))
    return pl.pallas_call(
        body, name=name, grid=(T // tm,), in_specs=[blk, blk],
        out_specs=(pl.BlockSpec((1, LANES), lambda i: (0, 0)), blk, blk),
        out_shape=(jax.ShapeDtypeStruct((1, LANES), F32), jax.ShapeDtypeStruct((T, d), F32),
                   jax.ShapeDtypeStruct((T, d), BF16)),
        compiler_params=_params(("arbitrary",)),
    )(y, target)


def _as3d(a):
    return a.reshape(a.shape[0], -1, a.shape[-1])


def _sum_stack(parts, *, name, out_dtype, rows=256):
    n, R, C = parts.shape
    tr = _tile(R, rows, 16)

    def body(p_ref, o_ref):
        acc = p_ref[0].astype(F32)
        for s in range(1, n):
            acc = acc + p_ref[s].astype(F32)
        o_ref[...] = acc.astype(out_dtype)

    return pl.pallas_call(
        body, name=name, grid=(R // tr,),
        in_specs=[pl.BlockSpec((n, tr, C), lambda i: (0, i, 0))],
        out_specs=pl.BlockSpec((tr, C), lambda i: (i, 0)),
        out_shape=jax.ShapeDtypeStruct((R, C), out_dtype),
        compiler_params=_params(("parallel",)),
    )(parts)


def _adamw(w, g, m, v, *, name, rows=128):
    R, C = w.shape
    tr = _tile(R, rows, 8)
    c1 = 1.0 - ADAM_B1 ** ADAM_STEP
    c2 = 1.0 - ADAM_B2 ** ADAM_STEP

    def body(w_ref, g_ref, m_ref, v_ref, d_ref, nm_ref, nv_ref):
        gv = g_ref[...]
        nm = ADAM_B1 * m_ref[...] + (1.0 - ADAM_B1) * gv
        nv = ADAM_B2 * v_ref[...] + (1.0 - ADAM_B2) * (gv * gv)
        nm_ref[...] = nm
        nv_ref[...] = nv
        d_ref[...] = -ADAM_LR * ((nm / c1) / (jnp.sqrt(nv / c2) + ADAM_EPS) + ADAM_WD * w_ref[...])

    blk = pl.BlockSpec((tr, C), lambda i: (i, 0))
    return pl.pallas_call(
        body, name=name, grid=(R // tr,), in_specs=[blk] * 4, out_specs=(blk,) * 3,
        out_shape=(jax.ShapeDtypeStruct((R, C), F32),) * 3,
        compiler_params=_params(("parallel",)),
    )(w, g, m, v)


def _one_hot(index, n):
    return jnp.broadcast_to((jnp.arange(n) == index).astype(F32)[:, None, None], (n, 8, LANES))


def _is_set(flags_ref, s):
    return flags_ref[s, 0:1, 0:1] > 0.5


def _rows_for(h, width, itemsize, n_stacked, budget, mult):
    return _tile(h, max(mult, budget // (n_stacked * width * itemsize)), mult)


def _pair_sum(pieces, recv, core, *, name):
    _, H, C = recv.shape
    tr = _rows_for(H, C, 2, 1, 2 << 20, 16)
    nh = H // tr
    halves_lead = pieces.ndim == 4

    def body(lo_ref, hi_ref, r_ref, core_ref, o_ref):
        lo, hi = (lo_ref[0, 0], hi_ref[0, 0]) if halves_lead else (lo_ref[0], hi_ref[0])
        mine = jnp.where(_is_set(core_ref, 0), lo, hi)
        o_ref[0] = (mine.astype(F32) + r_ref[0].astype(F32)).astype(BF16)

    blk = pl.BlockSpec((1, tr, C), lambda j, i: (j, i, 0))
    if halves_lead:
        lo_spec = pl.BlockSpec((1, 1, tr, C), lambda j, i: (0, j, i, 0))
        hi_spec = pl.BlockSpec((1, 1, tr, C), lambda j, i: (1, j, i, 0))
    else:
        lo_spec, hi_spec = blk, pl.BlockSpec((1, tr, C), lambda j, i: (j, nh + i, 0))
    return pl.pallas_call(
        body, name=name, grid=(4, nh),
        in_specs=[lo_spec, hi_spec, blk, pl.BlockSpec((2, 8, LANES), lambda j, i: (0, 0, 0))],
        out_specs=blk, out_shape=jax.ShapeDtypeStruct((4, H, C), BF16),
        compiler_params=_params(("parallel", "parallel")),
    )(pieces, pieces, recv, core)


def _chip_sum(pair, landed, chip_flags, *, name):
    _, H, C = pair.shape
    tr = _rows_for(H, C, 2, 4, 8 << 20, 16)

    def body(p_ref, l_ref, chip_ref, o_ref):
        acc = None
        for s in range(4):
            part = jnp.where(_is_set(chip_ref, s), p_ref[s], l_ref[s]).astype(F32)
            acc = part if acc is None else acc + part
        o_ref[...] = acc

    blk = pl.BlockSpec((4, tr, C), lambda i: (0, i, 0))
    return pl.pallas_call(
        body, name=name, grid=(H // tr,),
        in_specs=[blk, blk, pl.BlockSpec((4, 8, LANES), lambda i: (0, 0, 0))],
        out_specs=pl.BlockSpec((tr, C), lambda i: (i, 0)), out_shape=jax.ShapeDtypeStruct((H, C), F32),
        compiler_params=_params(("parallel",)),
    )(pair, landed, chip_flags)


def _adamw_layer(w, m, v, total, recv, core, layer, prev, *, name, col_halves=False, after=()):
    _, R, C = w.shape
    H, wd = total.shape
    tr = _rows_for(H, wd, 4, 1, 2 << 20, 8)
    nh = H // tr
    c1 = 1.0 - ADAM_B1 ** ADAM_STEP
    c2 = 1.0 - ADAM_B2 ** ADAM_STEP
    n_prev = 0 if prev is None else 4
    after = tuple(after)

    def body(*refs):
        w_ref, m_ref, v_ref, t_ref, r_ref, core_ref = refs[:6]
        g_ref, d_ref, nm_ref, nv_ref = refs[6 + n_prev + len(after):]
        half_is_mine = jnp.where(pl.program_id(0) == 0, core_ref[0, 0:1, 0:1], core_ref[1, 0:1, 0:1]) > 0.5
        gv = jnp.where(half_is_mine, t_ref[...], r_ref[...])
        nm = ADAM_B1 * m_ref[0] + (1.0 - ADAM_B1) * gv
        nv = ADAM_B2 * v_ref[0] + (1.0 - ADAM_B2) * (gv * gv)
        g_ref[0] = gv
        nm_ref[0] = nm
        nv_ref[0] = nv
        d_ref[0] = -ADAM_LR * ((nm / c1) / (jnp.sqrt(nv / c2) + ADAM_EPS) + ADAM_WD * w_ref[0])

    if col_halves:
        lay = pl.BlockSpec((1, tr, wd), lambda hf, i: (layer, i, hf))
    else:
        lay = pl.BlockSpec((1, tr, wd), lambda hf, i: (layer, hf * nh + i, 0))
    one = pl.BlockSpec((tr, wd), lambda hf, i: (i, 0))
    operands = [w, m, v, total, recv, core] + ([] if prev is None else list(prev)) + list(after)
    return pl.pallas_call(
        body, name=name, grid=(2, nh),
        in_specs=[lay, lay, lay, one, one, pl.BlockSpec((2, 8, LANES), lambda hf, i: (0, 0, 0))]
        + [ANY] * (n_prev + len(after)),
        out_specs=(lay,) * 4, out_shape=(jax.ShapeDtypeStruct((2, R, C), F32),) * 4,
        input_output_aliases={6 + i: i for i in range(n_prev)},
        compiler_params=_params(("parallel", "parallel")),
    )(*operands)


ANY = pl.BlockSpec(memory_space=pl.ANY)


def _coords():
    return lax.axis_index("x"), lax.axis_index("y"), lax.axis_index("c")


HBM = pl.BlockSpec(memory_space=pltpu.HBM)
SEM = pl.BlockSpec(memory_space=pltpu.SEMAPHORE)
EFFECT = pltpu.SideEffectType.DATAFLOW_SIDE_EFFECTING


def _copies(plan, bufs, send_sems, recv_sems):
    return [pltpu.make_async_remote_copy(src_ref=s, dst_ref=d, send_sem=send_sems.at[i], recv_sem=recv_sems.at[i],
                                         device_id=to, device_id_type=MESH)
            for i, (s, d, to) in enumerate(plan(bufs))]


def _start_copies(bufs, groups, *, name):
    nb, ng = len(bufs), len(groups)

    def body(*refs):
        buf_refs = refs[:nb]
        sems = refs[nb:nb + 2 * ng]
        token = refs[-1]
        for g, (plan, _) in enumerate(groups):
            for cp in _copies(plan, buf_refs, sems[2 * g], sems[2 * g + 1]):
                cp.start()
        token[...] = jnp.zeros_like(token)

    sem_shapes = []
    for _, n in groups:
        sem_shapes += [pltpu.SemaphoreType.DMA((n,)), pltpu.SemaphoreType.DMA((n,))]
    out = pl.pallas_call(
        body, name=name, in_specs=[HBM] * nb,
        out_specs=tuple([SEM] * (2 * ng) + [HBM] * nb + [pl.BlockSpec(memory_space=pltpu.VMEM)]),
        out_shape=tuple(sem_shapes + [pltpu.HBM(b.shape, b.dtype) for b in bufs] + [jax.ShapeDtypeStruct((8, LANES), F32)]),
        input_output_aliases={i: 2 * ng + i for i in range(nb)},
        compiler_params=pltpu.CompilerParams(has_side_effects=EFFECT),
    )(*[pltpu.with_memory_space_constraint(b, pltpu.HBM) for b in bufs])
    sems = [(out[2 * g], out[2 * g + 1]) for g in range(ng)]
    return sems, list(out[2 * ng:2 * ng + nb]), out[-1]


def _wait_copies(bufs, sems, plan, after, *, name):
    nb = len(bufs)

    def body(*refs):
        buf_refs = refs[:nb]
        for cp in _copies(plan, buf_refs, refs[nb], refs[nb + 1]):
            cp.wait_send()
            cp.wait_recv()

    out = pl.pallas_call(
        body, name=name, in_specs=[HBM] * nb + [SEM, SEM, ANY], out_specs=tuple([HBM] * nb),
        out_shape=tuple(pltpu.HBM(b.shape, b.dtype) for b in bufs),
        input_output_aliases={i: i for i in range(nb)},
        compiler_params=pltpu.CompilerParams(has_side_effects=EFFECT),
    )(*bufs, sems[0], sems[1], after)
    return list(out)


def _half(ref, c):
    h = ref.shape[0] // 2
    return ref.at[pl.ds(c * h, h)]


def _ici_gather_plan(pairs):
    def plan(refs):
        x, y, c = _coords()
        me = 2 * x + y
        out = []
        for s, d in pairs:
            for cx, cy in [(1 - x, y), (x, 1 - y), (1 - x, 1 - y)]:
                out.append((_half(refs[s], c), _half(refs[d].at[me], c), (cx, cy, c)))
            out.append((refs[s], refs[d].at[me], (x, y, 1 - c)))
        return out
    return plan, 4 * len(pairs)


def _d2d_forward_plan(lands):
    def plan(refs):
        x, y, c = _coords()
        out = []
        for d in lands:
            for cx, cy in [(1 - x, y), (x, 1 - y), (1 - x, 1 - y)]:
                got = _half(refs[d].at[2 * cx + cy], c)
                out.append((got, got, (x, y, 1 - c)))
        return out
    return plan, 3 * len(lands)


def _swap_half_plan(pairs):
    def plan(refs):
        x, y, c = _coords()
        out = []
        for s, d in pairs:
            h = refs[d].shape[1]
            other = refs[s].at[1 - c] if len(refs[s].shape) == 4 else refs[s].at[:, pl.ds((1 - c) * h, h)]
            out.append((other, refs[d], (x, y, 1 - c)))
        return out
    return plan, len(pairs)


def _scatter_plan(pairs):
    def plan(refs):
        x, y, c = _coords()
        me = 2 * x + y
        out = []
        for s, d in pairs:
            for cx, cy in [(1 - x, y), (x, 1 - y), (1 - x, 1 - y)]:
                out.append((refs[s].at[2 * cx + cy], refs[d].at[me], (cx, cy, c)))
        return out
    return plan, 3 * len(pairs)


def _swap_total_plan(pairs):
    def plan(refs):
        x, y, c = _coords()
        return [(refs[s], refs[d], (x, y, 1 - c)) for s, d in pairs]
    return plan, len(pairs)


def _gather_all(block, *, name, after=()):
    after = tuple(after)

    def body(src, *rest):
        out, send_sems, recv_sems, local_sem = rest[len(after):]
        x, y, c = _coords()
        me = 4 * x + 2 * y + c
        flips = [(fx, fy, fc) for fx in (0, 1) for fy in (0, 1) for fc in (0, 1)][1:]
        mine = pltpu.make_async_copy(src, out.at[me], local_sem)
        mine.start()
        peers = [(x ^ fx, y ^ fy, c ^ fc) for fx, fy, fc in flips]
        cps = [pltpu.make_async_remote_copy(src_ref=src, dst_ref=out.at[me], send_sem=send_sems.at[k],
                                            recv_sem=recv_sems.at[k], device_id=peer, device_id_type=MESH)
               for k, peer in enumerate(peers)]
        for cp in cps:
            cp.start()
        for k, (px, py, pc) in enumerate(peers):
            slot = out.at[4 * px + 2 * py + pc]
            pltpu.make_async_remote_copy(src_ref=slot, dst_ref=slot, send_sem=send_sems.at[k], recv_sem=recv_sems.at[k],
                                         device_id=(px, py, pc), device_id_type=MESH).wait_recv()
        for cp in cps:
            cp.wait_send()
        mine.wait()

    return pl.pallas_call(
        body, name=name, in_specs=[ANY] * (1 + len(after)), out_specs=ANY,
        out_shape=jax.ShapeDtypeStruct((8,) + block.shape, block.dtype),
        scratch_shapes=[pltpu.SemaphoreType.DMA((7,)), pltpu.SemaphoreType.DMA((7,)), pltpu.SemaphoreType.DMA],
    )(block, *after)


def _cols(o):
    return jnp.transpose(o, (1, 0, 2)).reshape(o.shape[1], -1)


def _uncols(full):
    return jnp.transpose(full.reshape(full.shape[0], 4, -1), (1, 0, 2))


def _rope_pad(x1, x2):
    z = jnp.zeros_like(x1)
    return jnp.concatenate([x1, z, x2, z], axis=-1)


def _head_pad(w, heads):
    r = w.reshape(w.shape[0], heads, QK_HEAD)
    half = QK_ROPE // 2
    out = jnp.concatenate([r[..., :QK_NOPE], _rope_pad(r[..., QK_NOPE:QK_NOPE + half], r[..., QK_NOPE + half:])], axis=-1)
    return out.reshape(w.shape[0], heads * HEAD_PAD)


def _head_unpad(w, heads):
    r = w.reshape(w.shape[0], heads, HEAD_PAD)
    half = QK_ROPE // 2
    out = jnp.concatenate([r[..., :QK_NOPE], r[..., QK_NOPE:QK_NOPE + half],
                           r[..., QK_NOPE + 2 * half:QK_NOPE + 3 * half]], axis=-1)
    return out.reshape(w.shape[0], heads * QK_HEAD)


class _Dims:
    def __init__(self, d, seq):
        self.d = d
        self.seq = seq
        self.t_real = N_META + seq
        self.t = -(-self.t_real // LANES) * LANES
        self.dc = d // 2
        self.dp = d // 2
        self.pg = self.dp // len(POOL_WINDOWS)
        self.heads = d // 128
        self.dff = 4 * d
        self.a_end = 3 * self.dc
        self.q_end = self.a_end + Q_LORA
        self.kv_end = self.q_end + KV_LORA
        self.kr_end = self.kv_end + QK_ROPE
        self.pool_end = self.kr_end + self.dp
        self.d_in = self.pool_end + 3 * d
        self.r_pool = 3 * self.dc
        self.r_q = self.r_pool + self.dp
        self.r_kv = self.r_q + Q_LORA
        self.r_kr = self.r_kv + KV_LORA
        self.r_width = self.r_kr + HEAD_PAD


def _split_cols(a):
    return jnp.moveaxis(a.reshape(a.shape[:-1] + (2, a.shape[-1] // 2)), -2, -3)


def _join_cols(a):
    a = jnp.moveaxis(a, -3, -2)
    return a.reshape(a.shape[:-2] + (a.shape[-2] * a.shape[-1],))


def _in_weights(dm, pieces):
    w_t = _join_cols(pieces).reshape(dm.d_in, dm.d)
    half = QK_ROPE // 2
    kr = w_t[dm.kv_end:dm.kr_end]
    zeros = jnp.zeros((half, dm.d), BF16)
    kr_p = jnp.concatenate([kr[:half], zeros, kr[half:], zeros, jnp.zeros((HEAD_PAD - LANES, dm.d), BF16)], axis=0)
    return dict(
        wg_t=w_t[dm.pool_end:],
        wr_t=jnp.concatenate([w_t[:dm.a_end], w_t[dm.kr_end:dm.pool_end], w_t[dm.a_end:dm.kv_end], kr_p], axis=0))


def _other_weights(dm, g):
    out = {}
    if "w_ukv" in g:
        w_ukv = _cols(g["w_ukv"]).reshape(KV_LORA, dm.heads, QK_NOPE + V_HEAD)
        out["wkn"] = w_ukv[:, :, :QK_NOPE].reshape(KV_LORA, dm.heads * QK_NOPE)
        out["wv"] = w_ukv[:, :, QK_NOPE:].reshape(KV_LORA, dm.heads * V_HEAD)
    if "w_uq" in g:
        out["wuq"] = _head_pad(_cols(g["w_uq"]), dm.heads)
    if "pool_w" in g:
        out["wp"] = jnp.transpose(g["pool_w"], (1, 0, 2, 3)).reshape(len(POOL_WINDOWS), dm.pg, dm.pg)
    for name, key in (("w_branch_a", "wba"), ("w_branch_c", "wbc"), ("w_up", "wup")):
        if name in g:
            out[key] = _cols(g[name])
    for name, key in (("w_branch_b", "wbb"), ("w_o", "wo"), ("w_down", "wdn")):
        if name in g:
            out[key] = g[name].reshape(-1, dm.d)
    return out


def _small_weights(small):
    return dict(
        conv_w=small["conv_w"],
        attn_norm=small["attn_norm"][None], mlp_norm=small["mlp_norm"][None],
        q_lat_norm=small["q_lat_norm"][None], kv_lat_norm=small["kv_lat_norm"][None],
        q_norm=_head_pad(small["q_norm"][None], 1), k_norm=_head_pad(small["k_norm"][None], 1),
        pool_scale=small["pool_scale"][None],
    )


def _grad_piece(dm, dw, name):
    half = QK_ROPE // 2
    rows = lambda a: a.reshape((4, a.shape[0] // 4) + a.shape[1:])
    if name == "w_in":
        dwr, dwg = dw["wr_t"], dw["wg_t"]
        d_t = jnp.concatenate([
            dwr[:, :dm.r_pool], dwr[:, dm.r_q:dm.r_kr], dwr[:, dm.r_kr:dm.r_kr + half],
            dwr[:, dm.r_kr + 2 * half:dm.r_kr + 3 * half], dwr[:, dm.r_pool:dm.r_q], dwg], axis=1)
        out = d_t.reshape(2, 4, d_t.shape[1] // 4, d_t.shape[2])
    elif name == "w_ukv":
        out = _uncols(jnp.concatenate([dw["wkn"].reshape(KV_LORA, dm.heads, QK_NOPE),
                                       dw["wv"].reshape(KV_LORA, dm.heads, V_HEAD)], axis=-1).reshape(KV_LORA, -1))
    elif name == "w_uq":
        out = _uncols(_head_unpad(dw["wuq"], dm.heads))
    elif name == "pool_w":
        out = jnp.transpose(dw["wp"].reshape(len(POOL_WINDOWS), 4, dm.pg // 4, dm.pg), (1, 0, 2, 3))
    elif name in ("w_branch_a", "w_branch_c", "w_up"):
        out = dw[{"w_branch_a": "wba", "w_branch_c": "wbc", "w_up": "wup"}[name]]
    else:
        out = rows(dw[{"w_branch_b": "wbb", "w_o": "wo", "w_down": "wdn"}[name]])
    return out.astype(BF16)


def _layer_fwd(dm, W, x, cos_t, sin_t, tag, more=None, h=None):
    n = lambda s: f"{s}_{tag}"
    if h is None:
        h = _rms_fwd(x, W["attn_norm"], name=n("attn_norm"))
    gl = _mm(h, W["wg_t"], name=n("proj_gates"), tb=True)
    rest = _mm(h, W["wr_t"], name=n("proj_rest"), tb=True)
    if more is not None:
        W.update(more("after_proj", rest))
    y_a = _conv_fwd(rest, W["conv_w"], name=n("conv"), dc=dm.dc)
    y_c = _pool_fwd(rest, W["wp"], W["pool_scale"], name=n("pool"), seg0=dm.r_pool // dm.pg, pg=dm.pg)
    q_lat = _rms_fwd(rest, W["q_lat_norm"], name=n("q_lat_norm"), width=Q_LORA, seg=dm.r_q // Q_LORA)
    kv_lat = _rms_fwd(rest, W["kv_lat_norm"], name=n("kv_lat_norm"), width=KV_LORA, seg=dm.r_kv // KV_LORA)
    q_raw = _mm(q_lat, W["wuq"], name=n("up_q"))
    k_nope = _mm(kv_lat, W["wkn"], name=n("up_k"))
    v = _mm(kv_lat, W["wv"], name=n("up_v"), out_dtype=BF16)
    q, k = _qk_fwd(q_raw, k_nope, rest, cos_t, sin_t, W["q_norm"], W["k_norm"], name=n("qk_norm_rope"),
                   heads=dm.heads, kr_seg=dm.r_kr // HEAD_PAD)
    if more is not None:
        W.update(more("after_qk", q))
    y_b, lse = _flash_fwd(q, k, v, name=n("attention"), heads=dm.heads)
    pa = _mm(y_a, W["wba"], name=n("branch_a"))
    pb = _mm(y_b, W["wbb"], name=n("branch_b"))
    pc = _mm(y_c, W["wbc"], name=n("branch_c"))
    merged = _merge_fwd(gl, pa, pb, pc, name=n("merge"), d=dm.d)
    x1 = _mm(merged, W["wo"], name=n("out_proj"), add=x)
    h2 = _rms_fwd(x1, W["mlp_norm"], name=n("mlp_norm"))
    up, act = _mm(h2, W["wup"], name=n("mlp_up"), epi="relu2")
    x2 = _mm(act, W["wdn"], name=n("mlp_down"), add=x1, tm=704, tk=4096)
    saved = dict(x=x, h=h, gl=gl, rest=rest, y_a=y_a, y_c=y_c, q_lat=q_lat, kv_lat=kv_lat, q_raw=q_raw, k_nope=k_nope,
                 v=v, q=q, k=k, y_b=y_b, lse=lse, pa=pa, pb=pb, pc=pc, merged=merged, x1=x1, h2=h2, up=up, act=act)
    return x2, saved


def _layer_bwd(dm, W, S, dx2, dx2_b, cos_t, sin_t, tag, hook=None):
    n = lambda s: f"{s}_{tag}"
    dw, ds = {}, {}
    if hook is None:
        hook = lambda point, t, dw_so_far: ()
    dup = _mm(dx2_b, W["wdn"], name=n("d_mlp_down"), tb=True, aux=S["up"], epi="drelu2", out_dtype=BF16,
              after=hook("start", dx2, dw))
    dw["wdn"] = _mm(S["act"], dx2_b, name=n("dw_mlp_down"), ta=True, tm=512, out_dtype=BF16)
    dh2 = _mm(dup, W["wup"], name=n("d_mlp_up"), tb=True, tm=704, tk=4096)
    dw["wup"] = _mm(S["h2"], dup, name=n("dw_mlp_up"), ta=True, tm=512, out_dtype=BF16, pieces=4)
    dx1, dx1_b, ds["mlp_norm"] = _rms_bwd(dh2, S["x1"], W["mlp_norm"], name=n("d_mlp_norm"), res=dx2, bf16_copy=True)
    dmerged = _mm(dx1_b, W["wo"], name=n("d_out_proj"), tb=True, after=hook("after_mlp", dx1, dw))
    dw["wo"] = _mm(S["merged"], dx1_b, name=n("dw_out_proj"), ta=True, tm=512, out_dtype=BF16)
    dpa, dpb, dpc, dg0, dg1, dg2 = _merge_bwd(dmerged, S["gl"], S["pa"], S["pb"], S["pc"], name=n("d_merge"), d=dm.d)
    dgl = jnp.concatenate([dg0, dg1, dg2], axis=1)
    dy_a = _mm(dpa, W["wba"], name=n("d_branch_a"), tb=True)
    dw["wba"] = _mm(S["y_a"], dpa, name=n("dw_branch_a"), ta=True, tm=512, out_dtype=BF16, pieces=4)
    dy_b = _mm(dpb, W["wbb"], name=n("d_branch_b"), tb=True, out_dtype=BF16)
    dw["wbb"] = _mm(S["y_b"], dpb, name=n("dw_branch_b"), ta=True, tm=512, out_dtype=BF16)
    dy_c = _mm(dpc, W["wbc"], name=n("d_branch_c"), tb=True)
    dw["wbc"] = _mm(S["y_c"], dpc, name=n("dw_branch_c"), ta=True, tm=512, out_dtype=BF16, pieces=4)
    dq, dk, dv = _flash_bwd(S["q"], S["k"], S["v"], S["y_b"], dy_b, S["lse"], name=n("d_attention"), heads=dm.heads)
    after_attention = hook("after_attention", dq, dw)
    dq_raw, dk_nope, dk_rope, dgq, dgk = _qk_bwd(
        dq, dk, S["q_raw"], S["k_nope"], S["rest"], cos_t, sin_t, W["q_norm"], W["k_norm"], name=n("d_qk_norm_rope"),
        heads=dm.heads, kr_seg=dm.r_kr // HEAD_PAD)
    ds["q_norm"] = _head_unpad(dgq, 1)
    ds["k_norm"] = _head_unpad(dgk, 1)
    dkv_v = _mm(dv, W["wv"], name=n("d_up_v"), tb=True, after=after_attention)
    dq_lat_n = _mm(dq_raw, W["wuq"], name=n("d_up_q"), tb=True, after=hook("after_qk", dq_raw, dw))
    dw["wuq"] = _mm(S["q_lat"], dq_raw, name=n("dw_up_q"), ta=True, tm=512)
    dkv_lat_n = _mm(dk_nope, W["wkn"], name=n("d_up_k"), tb=True, add=dkv_v)
    dw["wkn"] = _mm(S["kv_lat"], dk_nope, name=n("dw_up_k"), ta=True, tm=512)
    dw["wv"] = _mm(S["kv_lat"], dv, name=n("dw_up_v"), ta=True, tm=512)
    dq_lat, ds["q_lat_norm"] = _rms_bwd(dq_lat_n, S["rest"], W["q_lat_norm"], name=n("d_q_lat_norm"), width=Q_LORA,
                                        seg=dm.r_q // Q_LORA, out_dtype=BF16)
    dkv_lat, ds["kv_lat_norm"] = _rms_bwd(dkv_lat_n, S["rest"], W["kv_lat_norm"], name=n("d_kv_lat_norm"), width=KV_LORA,
                                          seg=dm.r_kv // KV_LORA, out_dtype=BF16)
    du, db, dc, ds["conv_w"] = _conv_bwd(S["rest"], W["conv_w"], dy_a, name=n("d_conv"), dc=dm.dc)
    dpool, dw["wp"], ds["pool_scale"] = _pool_bwd(S["rest"], W["wp"], W["pool_scale"], dy_c, name=n("d_pool"),
                                                  seg0=dm.r_pool // dm.pg, pg=dm.pg)
    drest = jnp.concatenate([du, db, dc, dpool, dq_lat, dkv_lat, dk_rope], axis=1)
    dw["wg_t"] = _mm(dgl, S["h"], name=n("dw_proj_gates"), ta=True, tm=512, out_dtype=BF16, pieces=2)
    dw["wr_t"] = _mm(drest, S["h"], name=n("dw_proj_rest"), ta=True, tm=512, out_dtype=BF16, pieces=2)
    dh_g = _mm(dgl, W["wg_t"], name=n("d_proj_gates"), tm=704, tk=3072, after=hook("after_dw_in", dw["wr_t"], dw))
    dh = _mm(drest, W["wr_t"], name=n("d_proj_rest"), add=dh_g, tm=704, tk=2688, after=hook("after_dh_gates", dh_g, dw))
    dx, dx_b, ds["attn_norm"] = _rms_bwd(dh, S["x"], W["attn_norm"], name=n("d_attn_norm"), res=dx1, bf16_copy=True)
    return dx, dx_b, dw, ds


BIG = ("w_in", "w_uq", "w_ukv", "pool_w", "w_branch_a", "w_branch_b", "w_branch_c", "w_o", "w_up", "w_down")
REPLICATED = ("attn_norm", "q_lat_norm", "kv_lat_norm", "q_norm", "k_norm", "pool_scale", "mlp_norm")
WEIGHTS = ("meta_tokens", "attn_norm", "w_in", "conv_w", "q_lat_norm", "kv_lat_norm", "w_uq", "w_ukv", "q_norm",
           "k_norm", "pool_w", "pool_scale", "w_branch_a", "w_branch_b", "w_branch_c", "w_o", "mlp_norm", "w_up",
           "w_down")


def _pack(arrays):
    flat = jnp.concatenate([a.reshape(-1).astype(F32) for a in arrays])
    pad = (-flat.shape[0]) % (8 * LANES)
    return jnp.pad(flat, (0, pad)).reshape(-1, LANES)


def _unpack(flat, shapes):
    out, pos = [], 0
    flat = flat.reshape(-1)
    for shp in shapes:
        size = math.prod(shp)
        out.append(flat[pos:pos + size].reshape(shp))
        pos += size
    return out


def _update(w, g, m, v, name):
    shp = w.shape
    to2 = lambda a: a.reshape(-1, shp[-1])
    delta, nm, nv = _adamw(to2(w), to2(g), to2(m), to2(v), name=name)
    return delta.reshape(shp), nm.reshape(shp), nv.reshape(shp)


def _step(args):
    x = args["x"][0]
    seq, d = x.shape
    dm = _Dims(d, seq)
    xi, yi, ci = _coords()
    chip = 2 * xi + yi

    small_w = _gather_all(_pack([args["conv_w"], args["meta_tokens"]]), name="gather_small_weights")
    args = dict(args)
    for p in ("", "m_", "v_"):
        args[p + "w_in"] = jnp.swapaxes(args[p + "w_in"], 1, 2)
    order = [(k, l) for l in range(2) for k in BIG]
    shards = {n: args[n[0]][n[1]].astype(BF16) for n in order}
    for l in range(2):
        shards[("w_in", l)] = _split_cols(shards[("w_in", l)])
    small_w, shards[order[0]] = lax.optimization_barrier((small_w, shards[order[0]]))
    lands = {n: lax.empty((4,) + shards[n].shape, BF16) for n in order}
    last = ("w_up", "w_down")
    group_names = [[("w_in", 0)], [(k, 0) for k in BIG[1:] if k not in last], [(k, 0) for k in last],
                   [(k, 1) for k in BIG]]
    first, others = order[0], order[1:]
    sems, thru, token = _start_copies([shards[first], lands[first]], [_ici_gather_plan([(0, 1)])],
                                      name="start_gather_ici_first")
    shards[first], lands[first] = thru
    at = {n: i for i, n in enumerate(others)}
    sems_b, thru, token_b = _start_copies(
        [shards[n] for n in others] + [lands[n] for n in others] + [token],
        [_ici_gather_plan([(at[n], len(others) + at[n]) for n in g]) for g in group_names[1:]], name="start_gather_ici")
    sems = sems + sems_b
    for i, n in enumerate(others):
        shards[n], lands[n] = thru[i], thru[len(others) + i]

    def finish_gather(g, after, tag):
        names = group_names[g]
        k = len(names)
        plan, _ = _ici_gather_plan([(i, k + i) for i in range(k)])
        got = _wait_copies([shards[n] for n in names] + [lands[n] for n in names], sems[g], plan, after,
                           name=f"wait_gather_ici_{tag}")
        for i, n in enumerate(names):
            shards[n] = got[i]
        fwd = _d2d_forward_plan(list(range(k)))
        sems2, bufs2, tok2 = _start_copies(got[k:], [fwd], name=f"start_gather_d2d_{tag}")
        return names, bufs2, sems2[0], fwd[0], tok2

    def land_gather(pending, after, tag):
        names, bufs2, sems2, plan, tok2 = pending
        done = _wait_copies(bufs2, sems2, plan, tok2 if after is None else after, name=f"wait_gather_d2d_{tag}")
        return {n[0]: buf for n, buf in zip(names, done)}

    conv_shape, meta_shape = args["conv_w"].shape, args["meta_tokens"].shape
    per_chip = [_unpack(small_w[2 * j], [conv_shape, meta_shape]) for j in range(4)]
    conv_full = jnp.concatenate([p[0] for p in per_chip], axis=-1)
    meta_full = jnp.concatenate([p[1] for p in per_chip], axis=-1)

    layers = []
    for l in range(2):
        small = {k: args[k][l] for k in REPLICATED}
        small["conv_w"] = conv_full[l]
        layers.append(_small_weights(small))

    pos = jnp.arange(dm.t, dtype=F32)
    inv = ROPE_THETA ** (-jnp.arange(0, QK_ROPE, 2, dtype=F32) / QK_ROPE)
    ang = pos[:, None] * inv[None, :]
    cos_t = _rope_pad(jnp.cos(ang), jnp.cos(ang))
    sin_t = _rope_pad(-jnp.sin(ang), jnp.sin(ang))
    tail = jnp.zeros((dm.t - dm.t_real, d), F32)
    h0 = jnp.concatenate([meta_full, x, tail], axis=0)
    target = jnp.concatenate([jnp.zeros((N_META, d), F32), args["loss_target"][0], tail], axis=0)

    h_first = _rms_fwd(h0, layers[0]["attn_norm"], name="attn_norm_l0", after=(token, token_b))
    layers[0].update(_in_weights(dm, land_gather(finish_gather(0, h_first, "l0_in"), None, "l0_in")["w_in"]))
    def rest_of_layer0(point, after):
        g, tag = (1, "l0_mid") if point == "after_proj" else (2, "l0_mlp")
        return _other_weights(dm, land_gather(finish_gather(g, after, tag), None, tag))

    h1, saved0 = _layer_fwd(dm, layers[0], h0, cos_t, sin_t, "l0", more=rest_of_layer0, h=h_first)
    g1 = land_gather(finish_gather(3, saved0["y_b"], "l1"), h1, "l1")
    layers[1].update(_in_weights(dm, g1["w_in"]))
    layers[1].update(_other_weights(dm, g1))
    h2, saved1 = _layer_fwd(dm, layers[1], h1, cos_t, sin_t, "l1")
    sq, dy, dy_b = _loss(h2, target, name="loss_head", first=N_META, last=dm.t_real)
    loss = lax.psum(0.5 / d * sq[0, 0], ("x", "y", "c"))
    core, chip_flags = _one_hot(ci, 2), _one_hot(chip, 4)

    class Reduce:
        def __init__(self, names, dw, tag):
            self.names, self.tag, self.nb = names, tag, len(names)
            self.idx = [(i, self.nb + i) for i in range(self.nb)]
            parts = [_grad_piece(dm, dw, k) for k in names]
            parts = [p if k == "w_in" else _as3d(p) for p, k in zip(parts, names)]
            recv = [lax.empty((4,) + p.shape[2:] if k == "w_in" else (4, p.shape[1] // 2, p.shape[2]), BF16)
                    for p, k in zip(parts, names)]
            self.plan = _swap_half_plan(self.idx)
            self.sems, self.bufs, self.token = _start_copies(parts + recv, [self.plan], name=f"start_swap_{tag}")

        def _land(self, after, what):
            return _wait_copies(self.bufs, self.sems[0], self.plan[0], self.token if after is None else after,
                                name=f"wait_{what}_{self.tag}")

        def scatter(self, after=None):
            got = self._land(after, "swap")
            pairs = [_pair_sum(got[i], got[j], core, name=f"pair_sum_{k}_{self.tag}")
                     for (i, j), k in zip(self.idx, self.names)]
            self.plan = _scatter_plan(self.idx)
            self.sems, self.bufs, self.token = _start_copies(pairs + [lax.empty(p.shape, BF16) for p in pairs],
                                                             [self.plan], name=f"start_scatter_{self.tag}")
            return self.token

        def totals(self, after=None):
            got = self._land(after, "scatter")
            sums = [_chip_sum(got[i], got[j], chip_flags, name=f"chip_sum_{k}_{self.tag}")
                    for (i, j), k in zip(self.idx, self.names)]
            self.plan = _swap_total_plan(self.idx)
            self.sems, self.bufs, self.token = _start_copies(sums + [lax.empty(t.shape, F32) for t in sums],
                                                             [self.plan], name=f"start_swap_total_{self.tag}")
            return self.token

        def finish(self, after=None):
            got = self._land(after, "swap_total")
            return {k: (got[i], got[j]) for (i, j), k in zip(self.idx, self.names)}

    dh1, dh1_b, dw1, ds1 = _layer_bwd(dm, layers[1], saved1, dy, dy_b, cos_t, sin_t, "l1",
                               hook=lambda point, t, dw: (loss.reshape(1, 1),) if point == "start" else ())
    early = ("w_down", "w_up", "w_o", "w_branch_a", "w_branch_b", "w_branch_c")
    late = tuple(k for k in BIG if k not in early)
    stage = {}

    def during_layer0(point, t, dw):
        if point == "start":
            stage["l1"] = Reduce(BIG, dw1, "l1")
            return (stage["l1"].token,)
        if point == "after_mlp":
            return (stage["l1"].scatter(after=t),)
        if point == "after_attention":
            tok = stage["l1"].totals(after=t)
            stage["l0a"] = Reduce(early, dw, "l0a")
            return (tok, stage["l0a"].token)
        if point == "after_qk":
            stage["red1"] = stage["l1"].finish(after=t)
            return (stage["l0a"].scatter(after=t),)
        if point == "after_dw_in":
            tok = stage["l0a"].totals(after=t)
            stage["l0b"] = Reduce(late, dw, "l0b")
            return (tok, stage["l0b"].token)
        return (stage["l0b"].scatter(after=t),)

    dh0, _, dw0, ds0 = _layer_bwd(dm, layers[0], saved0, dh1, dh1_b, cos_t, sin_t, "l0", hook=during_layer0)
    grad_x = dh0[N_META:dm.t_real][None]
    red1 = stage["red1"]
    grads, delta, new_m, new_v = {}, {}, {}, {}

    def adamw_big(k, layer, red, prev, after):
        shp = args[k].shape
        wmv = [args[p + k].reshape(2, -1, shp[-1]) for p in ("", "m_", "v_")]
        return _adamw_layer(*wmv, *red[k], core, layer, prev, name=f"adamw_{k}_l{layer}", col_halves=k == "w_in",
                            after=after)

    def keep(k, out):
        shp = args[k].shape
        out = [o.reshape(shp) for o in out]
        grads[k], delta[k], new_m[k], new_v[k] = [jnp.swapaxes(o, 1, 2) for o in out] if k == "w_in" else out

    half_done = {}
    pin = dh0
    for k in BIG:
        half_done[k] = adamw_big(k, 1, red1, None, (pin,))
        pin = half_done[k][0]
    red0a = stage["l0a"].finish(after=pin)
    for k in early:
        out = adamw_big(k, 0, red0a, half_done[k], ())
        keep(k, out)
        pin = out[0]

    small_names = REPLICATED + ("conv_w",)
    small_parts = [jnp.stack([ds0[k].reshape(ds0[k].shape[-2:] if k == "conv_w" else (-1,)),
                              ds1[k].reshape(ds1[k].shape[-2:] if k == "conv_w" else (-1,))]) for k in small_names]
    small_parts.append(dh0[:N_META])
    small_all = _gather_all(_pack(small_parts), name="gather_small_grads", after=(pin,))
    small_sum = _sum_stack(small_all, name="sum_small_grads", out_dtype=F32)
    small_g = dict(zip(small_names + ("meta_tokens",), _unpack(small_sum, [p.shape for p in small_parts])))
    for k in REPLICATED:
        grads[k] = small_g[k]
    dcw = conv_shape[-1]
    grads["conv_w"] = lax.dynamic_slice_in_dim(small_g["conv_w"], chip * dcw, dcw, axis=2)
    dmeta = meta_shape[-1]
    grads["meta_tokens"] = lax.dynamic_slice_in_dim(small_g["meta_tokens"], chip * dmeta, dmeta, axis=1)

    stage["l0b"].totals(after=small_sum)
    red0b = stage["l0b"].finish()
    for k in late:
        keep(k, adamw_big(k, 0, red0b, half_done[k], ()))
    for k in WEIGHTS:
        if k not in BIG:
            grads[k] = grads[k].reshape(args[k].shape)
            delta[k], new_m[k], new_v[k] = _update(args[k], grads[k], args["m_" + k], args["v_" + k], f"adamw_{k}")
    return (loss, grad_x, *[grads[k] for k in WEIGHTS], *[delta[k] for k in WEIGHTS],
            *[new_m[k] for k in WEIGHTS], *[new_v[k] for k in WEIGHTS])


def kernel(x, meta_tokens, attn_norm, w_in, conv_w, q_lat_norm, kv_lat_norm, w_uq, w_ukv, q_norm, k_norm, pool_w, pool_scale, w_branch_a, w_branch_b, w_branch_c, w_o, mlp_norm, w_up, w_down, loss_target, m_meta_tokens, m_attn_norm, m_w_in, m_conv_w, m_q_lat_norm, m_kv_lat_norm, m_w_uq, m_w_ukv, m_q_norm, m_k_norm, m_pool_w, m_pool_scale, m_w_branch_a, m_w_branch_b, m_w_branch_c, m_w_o, m_mlp_norm, m_w_up, m_w_down, v_meta_tokens, v_attn_norm, v_w_in, v_conv_w, v_q_lat_norm, v_kv_lat_norm, v_w_uq, v_w_ukv, v_q_norm, v_k_norm, v_pool_w, v_pool_scale, v_w_branch_a, v_w_branch_b, v_w_branch_c, v_w_o, v_mlp_norm, v_w_up, v_w_down):
    return _step(dict(locals()))
```

```python
import functools
import math

import jax
import jax.numpy as jnp
from jax import lax
from jax.experimental import pallas as pl
from jax.experimental.pallas import tpu as pltpu

F32 = jnp.float32
BF16 = jnp.bfloat16
MESH = pl.DeviceIdType.MESH

EPS = 1e-6
N_META = 16
QK_NOPE = 128
QK_ROPE = 64
QK_HEAD = QK_NOPE + QK_ROPE
V_HEAD = 128
HEAD_PAD = 256
Q_LORA = 512
KV_LORA = 512
ROPE_THETA = 10000.0
POOL_WINDOWS = (2, 4, 8, 16)
HALO = 16
LANES = 128
ADAM_LR = 0.001
ADAM_B1 = 0.9
ADAM_B2 = 0.999
ADAM_EPS = 1e-08
ADAM_WD = 0.01
ADAM_STEP = 10
VMEM_LIMIT = 52 * 1024 * 1024
NEG = -1e30
ATTN_SCALE = QK_HEAD ** -0.5
LOG2_E = 1.4426950408889634
Q_FOLD = ATTN_SCALE * LOG2_E


def _tile(n, target, mult=LANES):
    best = None
    for t in range(mult, min(n, target) + 1, mult):
        if n % t == 0:
            best = t
    return n if best is None else best


def _params(sem=None):
    return pltpu.CompilerParams(dimension_semantics=sem, vmem_limit_bytes=VMEM_LIMIT)


def _mm(a, b, *, name, ta=False, tb=False, add=None, aux=None, epi=None, out_dtype=F32,
        tm=1056, tn=1024, tk=None, after=(), pieces=None):
    if ta:
        K, M = a.shape
    else:
        M, K = a.shape
    if tb:
        N, kb = b.shape
    else:
        kb, N = b.shape
    assert K == kb, (a.shape, b.shape, ta, tb)
    tm = _tile(M, tm, LANES if ta else 16)
    tn = _tile(N if pieces is None else N // pieces, tn, LANES)
    tk = K if tk is None else _tile(K, tk, LANES if (not ta or tb) else 16)
    nk = K // tk
    a_bytes, b_bytes = a.size * a.dtype.itemsize, b.size * b.dtype.itemsize
    j_outer = nk == 1 and a_bytes * (N // tn) + b_bytes < a_bytes + b_bytes * (M // tm)
    grid = (N // tn, M // tm, nk) if j_outer else (M // tm, N // tn, nk)
    row = (lambda g0, g1: g1) if j_outer else (lambda g0, g1: g0)
    col = (lambda g0, g1: g0) if j_outer else (lambda g0, g1: g1)

    if ta:
        a_spec = pl.BlockSpec((tk, tm), lambda g0, g1, k: (k, row(g0, g1)))
    else:
        a_spec = pl.BlockSpec((tm, tk), lambda g0, g1, k: (row(g0, g1), k))
    if tb:
        b_spec = pl.BlockSpec((tn, tk), lambda g0, g1, k: (col(g0, g1), k))
    else:
        b_spec = pl.BlockSpec((tk, tn), lambda g0, g1, k: (k, col(g0, g1)))
    o_spec = pl.BlockSpec((tm, tn), lambda g0, g1, k: (row(g0, g1), col(g0, g1)))
    per = None if pieces is None else N // pieces // tn
    in_specs = [a_spec, b_spec]
    operands = [a, b]
    if add is not None:
        in_specs.append(o_spec)
        operands.append(add)
    if aux is not None:
        in_specs.append(o_spec)
        operands.append(aux)
    after = tuple(after)
    in_specs += [pl.BlockSpec(memory_space=pl.ANY)] * len(after)
    operands += list(after)
    if epi == "relu2":
        out_shape = (jax.ShapeDtypeStruct((M, N), BF16), jax.ShapeDtypeStruct((M, N), BF16))
        out_specs = (o_spec, o_spec)
    elif pieces is not None:
        out_shape = jax.ShapeDtypeStruct((pieces, M, N // pieces), out_dtype)
        out_specs = pl.BlockSpec((1, tm, tn), lambda g0, g1, k: (col(g0, g1) // per, row(g0, g1), col(g0, g1) % per))
    else:
        out_shape = jax.ShapeDtypeStruct((M, N), out_dtype)
        out_specs = o_spec
    dims =(((0 if ta else 1,), (1 if tb else 0,)), ((), ()))
    has_add, has_aux = add is not None, aux is not None

    def body(*refs):
        a_ref, b_ref = refs[0], refs[1]
        pos = 2
        add_ref = aux_ref = None
        if has_add:
            add_ref = refs[pos]
            pos += 1
        if has_aux:
            aux_ref = refs[pos]
            pos += 1
        pos += len(after)
        n_out = 2 if epi == "relu2" else 1
        out_refs = refs[pos:pos + n_out]
        acc_ref = refs[pos + n_out] if nk > 1 else None

        part = lax.dot_general(a_ref[...].astype(BF16), b_ref[...].astype(BF16), dims,
                               preferred_element_type=F32)

        def finish(acc):
            if has_add:
                acc = acc + add_ref[...].astype(F32)
            if epi == "relu2":
                r = jnp.maximum(acc, 0.0)
                out_refs[0][...] = acc.astype(BF16)
                out_refs[1][...] = (r * r).astype(BF16)
            elif epi == "drelu2":
                u = aux_ref[...].astype(F32)
                out_refs[0][...] = (acc * (2.0 * jnp.maximum(u, 0.0))).astype(out_dtype)
            else:
                out_refs[0][...] = acc.astype(out_dtype).reshape(out_refs[0].shape)

        if nk == 1:
            finish(part)
        else:
            k = pl.program_id(2)

            @pl.when(k == 0)
            def _():
                acc_ref[...] = part

            @pl.when(k > 0)
            def _():
                acc_ref[...] += part

            @pl.when(k == nk - 1)
            def _():
                finish(acc_ref[...])

    scratch = [pltpu.VMEM((tm, tn), F32)] if nk > 1 else []
    return pl.pallas_call(
        body, name=name, grid=grid, in_specs=in_specs, out_specs=out_specs, out_shape=out_shape,
        scratch_shapes=scratch, compiler_params=_params(("parallel", "parallel", "arbitrary")),
    )(*operands)


def _rms_fwd(x, g, *, name, width=None, seg=0, tm=384, after=()):
    T = x.shape[0]
    width = x.shape[1] if width is None else width
    tm = _tile(T, tm, 16)
    after = tuple(after)

    def body(x_ref, g_ref, *rest):
        xf = x_ref[...].astype(F32)
        r = lax.rsqrt(jnp.mean(xf * xf, axis=-1, keepdims=True) + EPS)
        rest[-1][...] = (xf * r * g_ref[...]).astype(BF16)

    return pl.pallas_call(
        body, name=name, grid=(T // tm,),
        in_specs=[pl.BlockSpec((tm, width), lambda i: (i, seg)), pl.BlockSpec((1, width), lambda i: (0, 0))]
        + [pl.BlockSpec(memory_space=pl.ANY)] * len(after),
        out_specs=pl.BlockSpec((tm, width), lambda i: (i, 0)),
        out_shape=jax.ShapeDtypeStruct((T, width), BF16),
        compiler_params=_params(("parallel",)),
    )(x, g, *after)


def _rms_bwd(dy, x, g, *, name, width=None, seg=0, res=None, out_dtype=F32, tm=384, bf16_copy=False):
    T = x.shape[0]
    width = x.shape[1] if width is None else width
    tm = _tile(T, tm, 16)
    has_res = res is not None

    def body(*refs):
        dy_ref, x_ref, g_ref = refs[:3]
        res_ref = refs[3] if has_res else None
        dx_ref, dg_ref = refs[4 if has_res else 3], refs[-1]
        xf = x_ref[...].astype(F32)
        dyf = dy_ref[...].astype(F32)
        r = lax.rsqrt(jnp.mean(xf * xf, axis=-1, keepdims=True) + EPS)
        xhat = xf * r
        dyh = dyf * g_ref[...]
        dx = r * (dyh - xhat * jnp.mean(dyh * xhat, axis=-1, keepdims=True))
        if has_res:
            dx = dx + res_ref[...].astype(F32)
        dx_ref[...] = dx.astype(out_dtype)
        if bf16_copy:
            refs[-2][...] = dx.astype(BF16)
        part = jnp.sum(dyf * xhat, axis=0, keepdims=True)

        @pl.when(pl.program_id(0) == 0)
        def _():
            dg_ref[...] = part

        @pl.when(pl.program_id(0) > 0)
        def _():
            dg_ref[...] += part

    row = pl.BlockSpec((tm, width), lambda i: (i, 0))
    in_specs = [row, pl.BlockSpec((tm, width), lambda i: (i, seg)), pl.BlockSpec((1, width), lambda i: (0, 0))]
    operands = [dy, x, g]
    if has_res:
        in_specs.append(row)
        operands.append(res)
    vec = pl.BlockSpec((1, width), lambda i: (0, 0))
    full = [jax.ShapeDtypeStruct((T, width), out_dtype)] + ([jax.ShapeDtypeStruct((T, width), BF16)] if bf16_copy else [])
    return pl.pallas_call(
        body, name=name, grid=(T // tm,), in_specs=in_specs,
        out_specs=tuple([row] * len(full) + [vec]),
        out_shape=tuple(full + [jax.ShapeDtypeStruct((1, width), F32)]),
        compiler_params=_params(("arbitrary",)),
    )(*operands)


def _down(ext, k):
    return pltpu.roll(ext, k, 0)


def _up(ext, k):
    return pltpu.roll(ext, ext.shape[0] - k, 0)


def _pre_halo(ref, r, R):
    start = pl.multiple_of(jnp.maximum(r * R - HALO, 0), 8)
    keep = (r > 0).astype(F32)
    return ref[pl.ds(start, HALO), :].astype(F32) * keep


def _post_halo(ref, r, R, n_chunks):
    start = pl.multiple_of(jnp.minimum(r * R + R, (n_chunks - 1) * R + R - HALO), 8)
    keep = (r < n_chunks - 1).astype(F32)
    return ref[pl.ds(start, HALO), :].astype(F32) * keep


def _chunk(ref, r, R):
    return ref[pl.ds(pl.multiple_of(r * R, 8), R), :].astype(F32)


def _conv_fwd(rest, conv_w, *, name, dc, tc=128, rows=1056):
    T = rest.shape[0]
    tc = _tile(dc, tc)
    nb = dc // tc
    R = _tile(T, rows, 16)
    n_chunks = T // R

    def body(u_ref, b_ref, c_ref, w_ref, y_ref):
        w0, w1, w2 = w_ref[0:1, :], w_ref[1:2, :], w_ref[2:3, :]

        def chunk(r, carry):
            cu = _chunk(c_ref, r, R) * _chunk(u_ref, r, R)
            ext = jnp.concatenate([_pre_halo(c_ref, r, R) * _pre_halo(u_ref, r, R), cu], axis=0)
            conv = w0 * _down(ext, 2)[HALO:] + w1 * _down(ext, 1)[HALO:] + w2 * cu
            y_ref[pl.ds(pl.multiple_of(r * R, 8), R), :] = (_chunk(b_ref, r, R) * conv).astype(BF16)
            return carry

        lax.fori_loop(0, n_chunks, chunk, 0)

    col = lambda off: pl.BlockSpec((T, tc), lambda j: (0, off * nb + j))
    return pl.pallas_call(
        body, name=name, grid=(nb,),
        in_specs=[col(0), col(1), col(2), pl.BlockSpec((3, tc), lambda j: (0, j))],
        out_specs=pl.BlockSpec((T, tc), lambda j: (0, j)),
        out_shape=jax.ShapeDtypeStruct((T, dc), BF16),
        compiler_params=_params(("parallel",)),
    )(rest, rest, rest, conv_w)


def _conv_bwd(rest, conv_w, dy, *, name, dc, tc=128, rows=1056):
    T = rest.shape[0]
    tc = _tile(dc, tc)
    nb = dc // tc
    R = _tile(T, rows, 16)
    n_chunks = T // R

    def body(u_ref, b_ref, c_ref, w_ref, dy_ref, du_ref, db_ref, dc_ref, dw_ref):
        w0, w1, w2 = w_ref[0:1, :], w_ref[1:2, :], w_ref[2:3, :]

        def chunk(r, carry):
            a0, a1, a2 = carry
            u, b, c = _chunk(u_ref, r, R), _chunk(b_ref, r, R), _chunk(c_ref, r, R)
            dy_c = _chunk(dy_ref, r, R)
            cu = c * u
            ext = jnp.concatenate([_pre_halo(c_ref, r, R) * _pre_halo(u_ref, r, R), cu], axis=0)
            cu1, cu2 = _down(ext, 1)[HALO:], _down(ext, 2)[HALO:]
            conv = w0 * cu2 + w1 * cu1 + w2 * cu
            dconv = dy_c * b
            dext = jnp.concatenate(
                [dconv, _post_halo(dy_ref, r, R, n_chunks) * _post_halo(b_ref, r, R, n_chunks)], axis=0)
            dcu = w2 * dconv + w1 * _up(dext, 1)[:R] + w0 * _up(dext, 2)[:R]
            rows_at = pl.ds(pl.multiple_of(r * R, 8), R)
            db_ref[rows_at, :] = (dy_c * conv).astype(BF16)
            du_ref[rows_at, :] = (dcu * c).astype(BF16)
            dc_ref[rows_at, :] = (dcu * u).astype(BF16)
            return (a0 + jnp.sum(dconv * cu2, axis=0, keepdims=True),
                    a1 + jnp.sum(dconv * cu1, axis=0, keepdims=True),
                    a2 + jnp.sum(dconv * cu, axis=0, keepdims=True))

        zero = jnp.zeros((1, tc), F32)
        a0, a1, a2 = lax.fori_loop(0, n_chunks, chunk, (zero, zero, zero))
        dw_ref[0:1, :] = a0
        dw_ref[1:2, :] = a1
        dw_ref[2:3, :] = a2

    col = lambda off: pl.BlockSpec((T, tc), lambda j: (0, off * nb + j))
    own = pl.BlockSpec((T, tc), lambda j: (0, j))
    return pl.pallas_call(
        body, name=name, grid=(nb,),
        in_specs=[col(0), col(1), col(2), pl.BlockSpec((3, tc), lambda j: (0, j)), own],
        out_specs=(own, own, own, pl.BlockSpec((3, tc), lambda j: (0, j))),
        out_shape=(jax.ShapeDtypeStruct((T, dc), BF16),) * 3 + (jax.ShapeDtypeStruct((3, dc), F32),),
        compiler_params=_params(("parallel",)),
    )(rest, rest, rest, conv_w, dy)


def _window_count(r, R, n_rows, w, first_row_offset):
    t = lax.broadcasted_iota(jnp.int32, (n_rows, 1), 0) + (r * R + first_row_offset)
    return jnp.minimum(t + 1, w).astype(F32)


def _pool_fwd(rest, pool_w, pool_scale, *, name, seg0, pg, rows=1056):
    T = rest.shape[0]
    R = _tile(T, rows, 16)
    n_chunks = T // R
    n_groups = len(POOL_WINDOWS)

    def body(x_ref, w_ref, s_ref, y_ref):
        def run(window):
            def chunk(r, carry):
                g = _chunk(x_ref, r, R)
                s = jnp.concatenate([_pre_halo(x_ref, r, R), g], axis=0)
                k = 1
                while k < window:
                    s = s + _down(s, k)
                    k *= 2
                pooled = s[HALO:] / _window_count(r, R, R, window, 0) - g
                mixed = jnp.dot(pooled.astype(BF16), w_ref[0], preferred_element_type=F32)
                y_ref[pl.ds(pl.multiple_of(r * R, 8), R), :] = (mixed * s_ref[...]).astype(BF16)
                return carry

            lax.fori_loop(0, n_chunks, chunk, 0)

        for gi, window in enumerate(POOL_WINDOWS):
            pl.when(pl.program_id(0) == gi)(functools.partial(run, window))

    return pl.pallas_call(
        body, name=name, grid=(n_groups,),
        in_specs=[pl.BlockSpec((T, pg), lambda g: (0, seg0 + g)),
                  pl.BlockSpec((1, pg, pg), lambda g: (g, 0, 0)),
                  pl.BlockSpec((1, pg), lambda g: (0, g))],
        out_specs=pl.BlockSpec((T, pg), lambda g: (0, g)),
        out_shape=jax.ShapeDtypeStruct((T, n_groups * pg), BF16),
        compiler_params=_params(("parallel",)),
    )(rest, pool_w, pool_scale)


def _pool_bwd(rest, pool_w, pool_scale, dy, *, name, seg0, pg, rows=1056):
    T = rest.shape[0]
    R = _tile(T, rows, 16)
    n_chunks = T // R
    n_groups = len(POOL_WINDOWS)

    def body(x_ref, w_ref, s_ref, dy_ref, dx_ref, dw_ref, ds_ref):
        def run(window):
            def chunk(r, carry):
                dw_acc, ds_acc = carry
                g = _chunk(x_ref, r, R)
                s = jnp.concatenate([_pre_halo(x_ref, r, R), g], axis=0)
                k = 1
                while k < window:
                    s = s + _down(s, k)
                    k *= 2
                pooled = (s[HALO:] / _window_count(r, R, R, window, 0) - g).astype(BF16)
                mixed = jnp.dot(pooled, w_ref[0], preferred_element_type=F32)
                dy_c = _chunk(dy_ref, r, R)
                dm_ext = (jnp.concatenate([dy_c, _post_halo(dy_ref, r, R, n_chunks)], axis=0)
                          * s_ref[...]).astype(BF16)
                dpool_ext = lax.dot_general(dm_ext, w_ref[0], (((1,), (1,)), ((), ())),
                                            preferred_element_type=F32)
                a = dpool_ext / _window_count(r, R, R + HALO, window, 0)
                k = 1
                while k < window:
                    a = a + _up(a, k)
                    k *= 2
                dx_ref[pl.ds(pl.multiple_of(r * R, 8), R), :] = (a[:R] - dpool_ext[:R]).astype(BF16)
                dw_acc = dw_acc + lax.dot_general(pooled, dm_ext[:R], (((0,), (0,)), ((), ())),
                                                  preferred_element_type=F32)
                ds_acc = ds_acc + jnp.sum(dy_c * mixed, axis=0, keepdims=True)
                return dw_acc, ds_acc

            dw_acc, ds_acc = lax.fori_loop(0, n_chunks, chunk,
                                           (jnp.zeros((pg, pg), F32), jnp.zeros((1, pg), F32)))
            dw_ref[0] = dw_acc
            ds_ref[...] = ds_acc

        for gi, window in enumerate(POOL_WINDOWS):
            pl.when(pl.program_id(0) == gi)(functools.partial(run, window))

    own = pl.BlockSpec((T, pg), lambda g: (0, g))
    return pl.pallas_call(
        body, name=name, grid=(n_groups,),
        in_specs=[pl.BlockSpec((T, pg), lambda g: (0, seg0 + g)),
                  pl.BlockSpec((1, pg, pg), lambda g: (g, 0, 0)),
                  pl.BlockSpec((1, pg), lambda g: (0, g)), own],
        out_specs=(own, pl.BlockSpec((1, pg, pg), lambda g: (g, 0, 0)), pl.BlockSpec((1, pg), lambda g: (0, g))),
        out_shape=(jax.ShapeDtypeStruct((T, n_groups * pg), BF16),
                   jax.ShapeDtypeStruct((n_groups, pg, pg), F32),
                   jax.ShapeDtypeStruct((1, n_groups * pg), F32)),
        compiler_params=_params(("parallel",)),
    )(rest, pool_w, pool_scale, dy)


def _rope(r, cos_t, sin_t):
    return r * cos_t + pltpu.roll(r, LANES // 2, 1) * sin_t


def _rope_t(d, cos_t, sin_t):
    return d * cos_t + pltpu.roll(d * sin_t, LANES // 2, 1)


def _qk_fwd(q_raw, k_nope, rest, cos_t, sin_t, q_norm, k_norm, *, name, heads, kr_seg, tm=192):
    T = q_raw.shape[0]
    tm = _tile(T, tm, 16)

    def body(q_ref, kn_ref, kr_ref, c_ref, s_ref, gq_ref, gk_ref, qo_ref, ko_ref):
        cos_b, sin_b = c_ref[...], s_ref[...]
        kr = kr_ref[:, 0:LANES]
        kr_ss = jnp.sum(kr * kr, axis=-1, keepdims=True)
        gq, gk = gq_ref[...], gk_ref[...]
        for h in range(heads):
            lo = h * HEAD_PAD
            q = q_ref[:, lo:lo + HEAD_PAD].astype(F32)
            rq = lax.rsqrt(jnp.sum(q * q, axis=-1, keepdims=True) / QK_HEAD + EPS)
            qn = q * (rq * Q_FOLD) * gq
            qo_ref[:, lo:lo + LANES] = qn[:, :LANES].astype(BF16)
            qo_ref[:, lo + LANES:lo + HEAD_PAD] = _rope(qn[:, LANES:], cos_b, sin_b).astype(BF16)
            kn = kn_ref[:, h * LANES:(h + 1) * LANES].astype(F32)
            rk = lax.rsqrt((jnp.sum(kn * kn, axis=-1, keepdims=True) + kr_ss) / QK_HEAD + EPS)
            ko_ref[:, lo:lo + LANES] = (kn * rk * gk[:, :LANES]).astype(BF16)
            ko_ref[:, lo + LANES:lo + HEAD_PAD] = _rope(kr * rk * gk[:, LANES:], cos_b, sin_b).astype(BF16)

    wq, wk = heads * HEAD_PAD, heads * LANES
    return pl.pallas_call(
        body, name=name, grid=(T // tm,),
        in_specs=[pl.BlockSpec((tm, wq), lambda i: (i, 0)), pl.BlockSpec((tm, wk), lambda i: (i, 0)),
                  pl.BlockSpec((tm, HEAD_PAD), lambda i: (i, kr_seg)),
                  pl.BlockSpec((tm, LANES), lambda i: (i, 0)), pl.BlockSpec((tm, LANES), lambda i: (i, 0)),
                  pl.BlockSpec((1, HEAD_PAD), lambda i: (0, 0)), pl.BlockSpec((1, HEAD_PAD), lambda i: (0, 0))],
        out_specs=(pl.BlockSpec((tm, wq), lambda i: (i, 0)), pl.BlockSpec((tm, wq), lambda i: (i, 0))),
        out_shape=(jax.ShapeDtypeStruct((T, wq), BF16), jax.ShapeDtypeStruct((T, wq), BF16)),
        compiler_params=_params(("parallel",)),
    )(q_raw, k_nope, rest, cos_t, sin_t, q_norm, k_norm)


def _qk_bwd(dq, dk, q_raw, k_nope, rest, cos_t, sin_t, q_norm, k_norm, *, name, heads, kr_seg, tm=128):
    T = q_raw.shape[0]
    tm = _tile(T, tm, 16)

    def body(dq_ref, dk_ref, q_ref, kn_ref, kr_ref, c_ref, s_ref, gq_ref, gk_ref,
             dqr_ref, dkn_ref, dkr_ref, dgq_ref, dgk_ref):
        cos_b, sin_b = c_ref[...], s_ref[...]
        kr = kr_ref[:, 0:LANES]
        kr_ss = jnp.sum(kr * kr, axis=-1, keepdims=True)
        gq, gk = gq_ref[...], gk_ref[...]
        dgq = jnp.zeros((1, HEAD_PAD), F32)
        dgk_n = jnp.zeros((1, LANES), F32)
        dgk_r = jnp.zeros((1, LANES), F32)
        dkr = jnp.zeros((tm, LANES), F32)
        for h in range(heads):
            lo = h * HEAD_PAD
            q = q_ref[:, lo:lo + HEAD_PAD].astype(F32)
            rq = lax.rsqrt(jnp.sum(q * q, axis=-1, keepdims=True) / QK_HEAD + EPS)
            qhat = q * rq
            dqn = jnp.concatenate([dq_ref[:, lo:lo + LANES],
                                   _rope_t(dq_ref[:, lo + LANES:lo + HEAD_PAD], cos_b, sin_b)], axis=1) * ATTN_SCALE
            dgq = dgq + jnp.sum(dqn * qhat, axis=0, keepdims=True)
            dqh = dqn * gq
            dqr_ref[:, lo:lo + HEAD_PAD] = (
                rq * (dqh - qhat * (jnp.sum(dqh * qhat, axis=-1, keepdims=True) / QK_HEAD))).astype(BF16)
            kn = kn_ref[:, h * LANES:(h + 1) * LANES].astype(F32)
            rk = lax.rsqrt((jnp.sum(kn * kn, axis=-1, keepdims=True) + kr_ss) / QK_HEAD + EPS)
            khat_n, khat_r = kn * rk, kr * rk
            dkn_n = dk_ref[:, lo:lo + LANES] * (1.0 / LOG2_E)
            dkn_r = _rope_t(dk_ref[:, lo + LANES:lo + HEAD_PAD], cos_b, sin_b) * (1.0 / LOG2_E)
            dgk_n = dgk_n + jnp.sum(dkn_n * khat_n, axis=0, keepdims=True)
            dgk_r = dgk_r + jnp.sum(dkn_r * khat_r, axis=0, keepdims=True)
            dkh_n, dkh_r = dkn_n * gk[:, :LANES], dkn_r * gk[:, LANES:]
            proj = (jnp.sum(dkh_n * khat_n, axis=-1, keepdims=True)
                    + jnp.sum(dkh_r * khat_r, axis=-1, keepdims=True)) / QK_HEAD
            dkn_ref[:, h * LANES:(h + 1) * LANES] = (rk * (dkh_n - khat_n * proj)).astype(BF16)
            dkr = dkr + rk * (dkh_r - khat_r * proj)
        dkr_ref[:, 0:LANES] = dkr.astype(BF16)
        dkr_ref[:, LANES:HEAD_PAD] = jnp.zeros((tm, HEAD_PAD - LANES), BF16)
        dgk = jnp.concatenate([dgk_n, dgk_r], axis=1)

        @pl.when(pl.program_id(0) == 0)
        def _():
            dgq_ref[...] = dgq
            dgk_ref[...] = dgk

        @pl.when(pl.program_id(0) > 0)
        def _():
            dgq_ref[...] += dgq
            dgk_ref[...] += dgk

    wq, wk = heads * HEAD_PAD, heads * LANES
    row = lambda w: pl.BlockSpec((tm, w), lambda i: (i, 0))
    vec = pl.BlockSpec((1, HEAD_PAD), lambda i: (0, 0))
    return pl.pallas_call(
        body, name=name, grid=(T // tm,),
        in_specs=[row(wq), row(wq), row(wq), row(wk), pl.BlockSpec((tm, HEAD_PAD), lambda i: (i, kr_seg)),
                  row(LANES), row(LANES), vec, vec],
        out_specs=(row(wq), row(wk), row(HEAD_PAD), vec, vec),
        out_shape=(jax.ShapeDtypeStruct((T, wq), BF16), jax.ShapeDtypeStruct((T, wk), BF16),
                   jax.ShapeDtypeStruct((T, HEAD_PAD), BF16),
                   jax.ShapeDtypeStruct((1, HEAD_PAD), F32), jax.ShapeDtypeStruct((1, HEAD_PAD), F32)),
        compiler_params=_params(("arbitrary",)),
    )(dq, dk, q_raw, k_nope, rest, cos_t, sin_t, q_norm, k_norm)


def _causal_mask(s):
    row = lax.broadcasted_iota(jnp.int32, s.shape, 0)
    col = lax.broadcasted_iota(jnp.int32, s.shape, 1)
    return jnp.where(row >= col, s, NEG)


def _flash_fwd(q, k, v, *, name, heads, tq=384, hp=2, parts=2):
    T = q.shape[0]
    tq = _tile(T, tq, LANES)
    nq = T // tq
    tr = tq // parts
    nt = (((1,), (1,)), ((), ()))
    chains = [(h, r) for h in range(hp) for r in range(parts)]

    def body(q_ref, k_ref, v_ref, o_ref, lse_ref, acc_ref):
        def q_block(i, carry):
            rows_at = [pl.ds(pl.multiple_of(i * tq + r * tr, tr), tr) for r in range(parts)]
            qbs = [q_ref[rows_at[r], h * HEAD_PAD:(h + 1) * HEAD_PAD] for h, r in chains]
            for c in range(len(chains)):
                acc_ref[c] = jnp.zeros((tr, V_HEAD), F32)

            def step(j, state, masked):
                k_at = pl.ds(pl.multiple_of(j * tq, tq), tq)
                new = []
                scores = [lax.dot_general(qb, k_ref[k_at, h * HEAD_PAD:(h + 1) * HEAD_PAD], nt,
                                          preferred_element_type=F32) for qb, (h, r) in zip(qbs, chains)]
                for c, (s, (h, r)) in enumerate(zip(scores, chains)):
                    m, l = state[c]
                    if masked:
                        row = lax.broadcasted_iota(jnp.int32, s.shape, 0) + r * tr
                        s = jnp.where(row >= lax.broadcasted_iota(jnp.int32, s.shape, 1), s, NEG)
                    m_new = jnp.maximum(m, jnp.max(s, axis=-1, keepdims=True))
                    p = jnp.exp2(s - m_new)
                    alpha = jnp.exp2(m - m_new)
                    new.append((m_new, alpha * l + jnp.sum(p, axis=-1, keepdims=True)))
                    acc_ref[c] = alpha * acc_ref[c] + jnp.dot(p.astype(BF16), v_ref[k_at, h * V_HEAD:(h + 1) * V_HEAD],
                                                              preferred_element_type=F32)
                return tuple(new)

            init = tuple((jnp.full((tr, 1), NEG, F32), jnp.zeros((tr, 1), F32)) for _ in chains)
            state = lax.fori_loop(0, i, lambda j, st: step(j, st, False), init)
            state = step(i, state, True)
            for c, ((m, l), (h, r)) in enumerate(zip(state, chains)):
                o_ref[rows_at[r], h * V_HEAD:(h + 1) * V_HEAD] = (acc_ref[c] / l).astype(BF16)
                lse_ref[h, rows_at[r], :] = jnp.broadcast_to(m + jnp.log2(l), (tr, LANES))
            return carry

        lax.fori_loop(0, nq, q_block, 0)

    qk_spec = pl.BlockSpec((T, hp * HEAD_PAD), lambda g: (0, g))
    v_spec = pl.BlockSpec((T, hp * V_HEAD), lambda g: (0, g))
    return pl.pallas_call(
        body, name=name, grid=(heads // hp,), in_specs=[qk_spec, qk_spec, v_spec],
        out_specs=(v_spec, pl.BlockSpec((hp, T, LANES), lambda g: (g, 0, 0))),
        out_shape=(jax.ShapeDtypeStruct((T, heads * V_HEAD), BF16), jax.ShapeDtypeStruct((heads, T, LANES), F32)),
        scratch_shapes=[pltpu.VMEM((len(chains), tr, V_HEAD), F32)],
        compiler_params=_params(("parallel",)),
    )(q, k, v)


def _flash_bwd(q, k, v, o, do, lse, *, name, heads, tq=384):
    T = q.shape[0]
    tq = _tile(T, tq, LANES)
    nq = T // tq
    nt = (((1,), (1,)), ((), ()))
    tn = (((0,), (0,)), ((), ()))

    def body(q_ref, k_ref, v_ref, o_ref, do_ref, lse_ref, dq_ref, dk_ref, dv_ref, delta_ref, dv_acc_ref):
        def fill_delta(i, carry):
            at = pl.ds(pl.multiple_of(i * tq, tq), tq)
            d = jnp.sum(o_ref[at, :].astype(F32) * do_ref[at, :].astype(F32), axis=-1, keepdims=True)
            delta_ref[at, :] = jnp.broadcast_to(d, (tq, LANES))
            dq_ref[at, :] = jnp.zeros((tq, HEAD_PAD), F32)
            return carry

        lax.fori_loop(0, nq, fill_delta, 0)

        def kv_block(j, carry):
            k_at = pl.ds(pl.multiple_of(j * tq, tq), tq)
            kb, vb = k_ref[k_at, :], v_ref[k_at, :]

            def steps(blocks, masked):
                at = [pl.ds(pl.multiple_of(i * tq, tq), tq) for i in blocks]
                qbs = [q_ref[a, :] for a in at]
                dobs = [do_ref[a, :] for a in at]
                scores = [lax.dot_general(qb, kb, nt, preferred_element_type=F32) for qb in qbs]
                dps = [lax.dot_general(dob, vb, nt, preferred_element_type=F32) for dob in dobs]
                for a, qb, dob, sc, dp in zip(at, qbs, dobs, scores, dps):
                    if masked:
                        sc = _causal_mask(sc)
                    p = jnp.exp2(sc - lse_ref[0, a, :][:, 0:1])
                    ds = (p * (dp - delta_ref[a, :][:, 0:1])).astype(BF16)
                    dv_part = lax.dot_general(p.astype(BF16), dob, tn, preferred_element_type=F32)
                    dk_part = lax.dot_general(ds, qb, tn, preferred_element_type=F32)
                    if masked:
                        dv_acc_ref[...] = dv_part
                        dk_ref[k_at, :] = dk_part
                    else:
                        dv_acc_ref[...] += dv_part
                        dk_ref[k_at, :] += dk_part
                    dq_ref[a, :] += jnp.dot(ds, kb, preferred_element_type=F32)

            def two_blocks(t, carry):
                steps([j + 1 + 2 * t, j + 2 + 2 * t], False)
                return carry

            steps([j], True)
            rest = nq - 1 - j
            lax.fori_loop(0, rest // 2, two_blocks, 0)

            @pl.when(rest % 2 == 1)
            def _():
                steps([nq - 1], False)

            dv_ref[k_at, :] = dv_acc_ref[...].astype(BF16)
            return carry

        lax.fori_loop(0, nq, kv_block, 0)

    qk_spec = pl.BlockSpec((T, HEAD_PAD), lambda h: (0, h))
    v_spec = pl.BlockSpec((T, V_HEAD), lambda h: (0, h))
    return pl.pallas_call(
        body, name=name, grid=(heads,),
        in_specs=[qk_spec, qk_spec, v_spec, v_spec, v_spec, pl.BlockSpec((1, T, LANES), lambda h: (h, 0, 0))],
        out_specs=(qk_spec, qk_spec, v_spec),
        out_shape=(jax.ShapeDtypeStruct((T, heads * HEAD_PAD), F32), jax.ShapeDtypeStruct((T, heads * HEAD_PAD), F32),
                   jax.ShapeDtypeStruct((T, heads * V_HEAD), BF16)),
        scratch_shapes=[pltpu.VMEM((T, LANES), F32), pltpu.VMEM((tq, V_HEAD), F32)],
        compiler_params=_params(("parallel",)),
    )(q, k, v, o, do, lse)


def _merge_fwd(gl, pa, pb, pc, *, name, d, tm=384, tn=1024):
    T = pa.shape[0]
    tm, tn = _tile(T, tm, 16), _tile(d, tn)
    nb = d // tn

    def body(g0, g1, g2, a, b, c, o_ref):
        f = lambda ref: ref[...].astype(F32)
        o_ref[...] = (jax.nn.sigmoid(f(g0)) * f(a) + jax.nn.sigmoid(f(g1)) * f(b)
                      + jax.nn.sigmoid(f(g2)) * f(c)).astype(BF16)

    gate = lambda n: pl.BlockSpec((tm, tn), lambda i, j: (i, n * nb + j))
    blk = pl.BlockSpec((tm, tn), lambda i, j: (i, j))
    return pl.pallas_call(
        body, name=name, grid=(T // tm, nb), in_specs=[gate(0), gate(1), gate(2), blk, blk, blk],
        out_specs=blk, out_shape=jax.ShapeDtypeStruct((T, d), BF16),
        compiler_params=_params(("parallel", "parallel")),
    )(gl, gl, gl, pa, pb, pc)


def _merge_bwd(dm, gl, pa, pb, pc, *, name, d, tm=384, tn=1024):
    T = pa.shape[0]
    tm, tn = _tile(T, tm, 16), _tile(d, tn)
    nb = d // tn

    def body(dm_ref, g0, g1, g2, a, b, c, da, db, dc, dg0, dg1, dg2):
        dmv = dm_ref[...].astype(F32)
        for g_ref, p_ref, dp_ref, dg_ref in ((g0, a, da, dg0), (g1, b, db, dg1), (g2, c, dc, dg2)):
            sg = jax.nn.sigmoid(g_ref[...].astype(F32))
            dp_ref[...] = (dmv * sg).astype(BF16)
            dg_ref[...] = (dmv * p_ref[...].astype(F32) * sg * (1.0 - sg)).astype(BF16)

    gate = lambda n: pl.BlockSpec((tm, tn), lambda i, j: (i, n * nb + j))
    blk = pl.BlockSpec((tm, tn), lambda i, j: (i, j))
    return pl.pallas_call(
        body, name=name, grid=(T // tm, nb), in_specs=[blk, gate(0), gate(1), gate(2), blk, blk, blk],
        out_specs=(blk,) * 6, out_shape=(jax.ShapeDtypeStruct((T, d), BF16),) * 6,
        compiler_params=_params(("parallel", "parallel")),
    )(dm, gl, gl, gl, pa, pb, pc)


def _loss(y, target, *, name, first, last, tm=384):
    T, d = y.shape
    tm = _tile(T, tm, 16)

    def body(y_ref, t_ref, loss_ref, dy_ref, dyb_ref):
        i = pl.program_id(0)
        row = lax.broadcasted_iota(jnp.int32, (tm, 1), 0) + i * tm
        real = jnp.logical_and(row >= first, row < last)
        err = jnp.where(real, y_ref[...] - t_ref[...], 0.0)
        dy_ref[...] = err * (1.0 / d)
        dyb_ref[...] = (err * (1.0 / d)).astype(BF16)
        part = jnp.broadcast_to(jnp.sum(err * err, keepdims=True).reshape(1, 1), (1, LANES))

        @pl.when(i == 0)
        def _():
            loss_ref[...] = part

        @pl.when(i > 0)
        def _():
            loss_ref[...] += part

    blk = pl.BlockSpec((tm, d), lambda i: (i, 0))
    return pl.pallas_call(
        body, name=name, grid=(T // tm,), in_specs=[blk, blk],
        out_specs=(pl.BlockSpec((1, LANES), lambda i: (0, 0)), blk, blk),
        out_shape=(jax.ShapeDtypeStruct((1, LANES), F32), jax.ShapeDtypeStruct((T, d), F32),
                   jax.ShapeDtypeStruct((T, d), BF16)),
        compiler_params=_params(("arbitrary",)),
    )(y, target)


def _as3d(a):
    return a.reshape(a.shape[0], -1, a.shape[-1])


def _sum_stack(parts, *, name, out_dtype, rows=256):
    n, R, C = parts.shape
    tr = _tile(R, rows, 16)

    def body(p_ref, o_ref):
        acc = p_ref[0].astype(F32)
        for s in range(1, n):
            acc = acc + p_ref[s].astype(F32)
        o_ref[...] = acc.astype(out_dtype)

    return pl.pallas_call(
        body, name=name, grid=(R // tr,),
        in_specs=[pl.BlockSpec((n, tr, C), lambda i: (0, i, 0))],
        out_specs=pl.BlockSpec((tr, C), lambda i: (i, 0)),
        out_shape=jax.ShapeDtypeStruct((R, C), out_dtype),
        compiler_params=_params(("parallel",)),
    )(parts)


def _adamw(w, g, m, v, *, name, rows=128):
    R, C = w.shape
    tr = _tile(R, rows, 8)
    c1 = 1.0 - ADAM_B1 ** ADAM_STEP
    c2 = 1.0 - ADAM_B2 ** ADAM_STEP

    def body(w_ref, g_ref, m_ref, v_ref, d_ref, nm_ref, nv_ref):
        gv = g_ref[...]
        nm = ADAM_B1 * m_ref[...] + (1.0 - ADAM_B1) * gv
        nv = ADAM_B2 * v_ref[...] + (1.0 - ADAM_B2) * (gv * gv)
        nm_ref[...] = nm
        nv_ref[...] = nv
        d_ref[...] = -ADAM_LR * ((nm / c1) / (jnp.sqrt(nv / c2) + ADAM_EPS) + ADAM_WD * w_ref[...])

    blk = pl.BlockSpec((tr, C), lambda i: (i, 0))
    return pl.pallas_call(
        body, name=name, grid=(R // tr,), in_specs=[blk] * 4, out_specs=(blk,) * 3,
        out_shape=(jax.ShapeDtypeStruct((R, C), F32),) * 3,
        compiler_params=_params(("parallel",)),
    )(w, g, m, v)


def _one_hot(index, n):
    return jnp.broadcast_to((jnp.arange(n) == index).astype(F32)[:, None, None], (n, 8, LANES))


def _is_set(flags_ref, s):
    return flags_ref[s, 0:1, 0:1] > 0.5


def _rows_for(h, width, itemsize, n_stacked, budget, mult):
    return _tile(h, max(mult, budget // (n_stacked * width * itemsize)), mult)


def _pair_sum(pieces, recv, core, *, name):
    _, H, C = recv.shape
    tr = _rows_for(H, C, 2, 1, 2 << 20, 16)
    nh = H // tr
    halves_lead = pieces.ndim == 4

    def body(lo_ref, hi_ref, r_ref, core_ref, o_ref):
        lo, hi = (lo_ref[0, 0], hi_ref[0, 0]) if halves_lead else (lo_ref[0], hi_ref[0])
        mine = jnp.where(_is_set(core_ref, 0), lo, hi)
        o_ref[0] = (mine.astype(F32) + r_ref[0].astype(F32)).astype(BF16)

    blk = pl.BlockSpec((1, tr, C), lambda j, i: (j, i, 0))
    if halves_lead:
        lo_spec = pl.BlockSpec((1, 1, tr, C), lambda j, i: (0, j, i, 0))
        hi_spec = pl.BlockSpec((1, 1, tr, C), lambda j, i: (1, j, i, 0))
    else:
        lo_spec, hi_spec = blk, pl.BlockSpec((1, tr, C), lambda j, i: (j, nh + i, 0))
    return pl.pallas_call(
        body, name=name, grid=(4, nh),
        in_specs=[lo_spec, hi_spec, blk, pl.BlockSpec((2, 8, LANES), lambda j, i: (0, 0, 0))],
        out_specs=blk, out_shape=jax.ShapeDtypeStruct((4, H, C), BF16),
        compiler_params=_params(("parallel", "parallel")),
    )(pieces, pieces, recv, core)


def _chip_sum(pair, landed, chip_flags, *, name):
    _, H, C = pair.shape
    tr = _rows_for(H, C, 2, 4, 8 << 20, 16)

    def body(p_ref, l_ref, chip_ref, o_ref):
        acc = None
        for s in range(4):
            part = jnp.where(_is_set(chip_ref, s), p_ref[s], l_ref[s]).astype(F32)
            acc = part if acc is None else acc + part
        o_ref[...] = acc

    blk = pl.BlockSpec((4, tr, C), lambda i: (0, i, 0))
    return pl.pallas_call(
        body, name=name, grid=(H // tr,),
        in_specs=[blk, blk, pl.BlockSpec((4, 8, LANES), lambda i: (0, 0, 0))],
        out_specs=pl.BlockSpec((tr, C), lambda i: (i, 0)), out_shape=jax.ShapeDtypeStruct((H, C), F32),
        compiler_params=_params(("parallel",)),
    )(pair, landed, chip_flags)


def _adamw_layer(w, m, v, total, recv, core, layer, prev, *, name, col_halves=False, after=()):
    _, R, C = w.shape
    H, wd = total.shape
    tr = _rows_for(H, wd, 4, 1, 2 << 20, 8)
    nh = H // tr
    c1 = 1.0 - ADAM_B1 ** ADAM_STEP
    c2 = 1.0 - ADAM_B2 ** ADAM_STEP
    n_prev = 0 if prev is None else 4
    after = tuple(after)

    def body(*refs):
        w_ref, m_ref, v_ref, t_ref, r_ref, core_ref = refs[:6]
        g_ref, d_ref, nm_ref, nv_ref = refs[6 + n_prev + len(after):]
        half_is_mine = jnp.where(pl.program_id(0) == 0, core_ref[0, 0:1, 0:1], core_ref[1, 0:1, 0:1]) > 0.5
        gv = jnp.where(half_is_mine, t_ref[...], r_ref[...])
        nm = ADAM_B1 * m_ref[0] + (1.0 - ADAM_B1) * gv
        nv = ADAM_B2 * v_ref[0] + (1.0 - ADAM_B2) * (gv * gv)
        g_ref[0] = gv
        nm_ref[0] = nm
        nv_ref[0] = nv
        d_ref[0] = -ADAM_LR * ((nm / c1) / (jnp.sqrt(nv / c2) + ADAM_EPS) + ADAM_WD * w_ref[0])

    if col_halves:
        lay = pl.BlockSpec((1, tr, wd), lambda hf, i: (layer, i, hf))
    else:
        lay = pl.BlockSpec((1, tr, wd), lambda hf, i: (layer, hf * nh + i, 0))
    one = pl.BlockSpec((tr, wd), lambda hf, i: (i, 0))
    operands = [w, m, v, total, recv, core] + ([] if prev is None else list(prev)) + list(after)
    return pl.pallas_call(
        body, name=name, grid=(2, nh),
        in_specs=[lay, lay, lay, one, one, pl.BlockSpec((2, 8, LANES), lambda hf, i: (0, 0, 0))]
        + [ANY] * (n_prev + len(after)),
        out_specs=(lay,) * 4, out_shape=(jax.ShapeDtypeStruct((2, R, C), F32),) * 4,
        input_output_aliases={6 + i: i for i in range(n_prev)},
        compiler_params=_params(("parallel", "parallel")),
    )(*operands)


ANY = pl.BlockSpec(memory_space=pl.ANY)


def _coords():
    return lax.axis_index("x"), lax.axis_index("y"), lax.axis_index("c")


HBM = pl.BlockSpec(memory_space=pltpu.HBM)
SEM = pl.BlockSpec(memory_space=pltpu.SEMAPHORE)
EFFECT = pltpu.SideEffectType.DATAFLOW_SIDE_EFFECTING


def _copies(plan, bufs, send_sems, recv_sems):
    return [pltpu.make_async_remote_copy(src_ref=s, dst_ref=d, send_sem=send_sems.at[i], recv_sem=recv_sems.at[i],
                                         device_id=to, device_id_type=MESH)
            for i, (s, d, to) in enumerate(plan(bufs))]


def _start_copies(bufs, groups, *, name):
    nb, ng = len(bufs), len(groups)

    def body(*refs):
        buf_refs = refs[:nb]
        sems = refs[nb:nb + 2 * ng]
        token = refs[-1]
        for g, (plan, _) in enumerate(groups):
            for cp in _copies(plan, buf_refs, sems[2 * g], sems[2 * g + 1]):
                cp.start()
        token[...] = jnp.zeros_like(token)

    sem_shapes = []
    for _, n in groups:
        sem_shapes += [pltpu.SemaphoreType.DMA((n,)), pltpu.SemaphoreType.DMA((n,))]
    out = pl.pallas_call(
        body, name=name, in_specs=[HBM] * nb,
        out_specs=tuple([SEM] * (2 * ng) + [HBM] * nb + [pl.BlockSpec(memory_space=pltpu.VMEM)]),
        out_shape=tuple(sem_shapes + [pltpu.HBM(b.shape, b.dtype) for b in bufs] + [jax.ShapeDtypeStruct((8, LANES), F32)]),
        input_output_aliases={i: 2 * ng + i for i in range(nb)},
        compiler_params=pltpu.CompilerParams(has_side_effects=EFFECT),
    )(*[pltpu.with_memory_space_constraint(b, pltpu.HBM) for b in bufs])
    sems = [(out[2 * g], out[2 * g + 1]) for g in range(ng)]
    return sems, list(out[2 * ng:2 * ng + nb]), out[-1]


def _wait_copies(bufs, sems, plan, after, *, name):
    nb = len(bufs)

    def body(*refs):
        buf_refs = refs[:nb]
        for cp in _copies(plan, buf_refs, refs[nb], refs[nb + 1]):
            cp.wait_send()
            cp.wait_recv()

    out = pl.pallas_call(
        body, name=name, in_specs=[HBM] * nb + [SEM, SEM, ANY], out_specs=tuple([HBM] * nb),
        out_shape=tuple(pltpu.HBM(b.shape, b.dtype) for b in bufs),
        input_output_aliases={i: i for i in range(nb)},
        compiler_params=pltpu.CompilerParams(has_side_effects=EFFECT),
    )(*bufs, sems[0], sems[1], after)
    return list(out)


def _half(ref, c):
    h = ref.shape[0] // 2
    return ref.at[pl.ds(c * h, h)]


def _ici_gather_plan(pairs):
    def plan(refs):
        x, y, c = _coords()
        me = 2 * x + y
        out = []
        for s, d in pairs:
            for cx, cy in [(1 - x, y), (x, 1 - y), (1 - x, 1 - y)]:
                out.append((_half(refs[s], c), _half(refs[d].at[me], c), (cx, cy, c)))
            out.append((refs[s], refs[d].at[me], (x, y, 1 - c)))
        return out
    return plan, 4 * len(pairs)


def _d2d_forward_plan(lands):
    def plan(refs):
        x, y, c = _coords()
        out = []
        for d in lands:
            for cx, cy in [(1 - x, y), (x, 1 - y), (1 - x, 1 - y)]:
                got = _half(refs[d].at[2 * cx + cy], c)
                out.append((got, got, (x, y, 1 - c)))
        return out
    return plan, 3 * len(lands)


def _swap_half_plan(pairs):
    def plan(refs):
        x, y, c = _coords()
        out = []
        for s, d in pairs:
            h = refs[d].shape[1]
            other = refs[s].at[1 - c] if len(refs[s].shape) == 4 else refs[s].at[:, pl.ds((1 - c) * h, h)]
            out.append((other, refs[d], (x, y, 1 - c)))
        return out
    return plan, len(pairs)


def _scatter_plan(pairs):
    def plan(refs):
        x, y, c = _coords()
        me = 2 * x + y
        out = []
        for s, d in pairs:
            for cx, cy in [(1 - x, y), (x, 1 - y), (1 - x, 1 - y)]:
                out.append((refs[s].at[2 * cx + cy], refs[d].at[me], (cx, cy, c)))
        return out
    return plan, 3 * len(pairs)


def _swap_total_plan(pairs):
    def plan(refs):
        x, y, c = _coords()
        return [(refs[s], refs[d], (x, y, 1 - c)) for s, d in pairs]
    return plan, len(pairs)


def _gather_all(block, *, name, after=()):
    after = tuple(after)

    def body(src, *rest):
        out, send_sems, recv_sems, local_sem = rest[len(after):]
        x, y, c = _coords()
        me = 4 * x + 2 * y + c
        flips = [(fx, fy, fc) for fx in (0, 1) for fy in (0, 1) for fc in (0, 1)][1:]
        mine = pltpu.make_async_copy(src, out.at[me], local_sem)
        mine.start()
        peers = [(x ^ fx, y ^ fy, c ^ fc) for fx, fy, fc in flips]
        cps = [pltpu.make_async_remote_copy(src_ref=src, dst_ref=out.at[me], send_sem=send_sems.at[k],
                                            recv_sem=recv_sems.at[k], device_id=peer, device_id_type=MESH)
               for k, peer in enumerate(peers)]
        for cp in cps:
            cp.start()
        for k, (px, py, pc) in enumerate(peers):
            slot = out.at[4 * px + 2 * py + pc]
            pltpu.make_async_remote_copy(src_ref=slot, dst_ref=slot, send_sem=send_sems.at[k], recv_sem=recv_sems.at[k],
                                         device_id=(px, py, pc), device_id_type=MESH).wait_recv()
        for cp in cps:
            cp.wait_send()
        mine.wait()

    return pl.pallas_call(
        body, name=name, in_specs=[ANY] * (1 + len(after)), out_specs=ANY,
        out_shape=jax.ShapeDtypeStruct((8,) + block.shape, block.dtype),
        scratch_shapes=[pltpu.SemaphoreType.DMA((7,)), pltpu.SemaphoreType.DMA((7,)), pltpu.SemaphoreType.DMA],
    )(block, *after)


def _cols(o):
    return jnp.transpose(o, (1, 0, 2)).reshape(o.shape[1], -1)


def _uncols(full):
    return jnp.transpose(full.reshape(full.shape[0], 4, -1), (1, 0, 2))


def _rope_pad(x1, x2):
    z = jnp.zeros_like(x1)
    return jnp.concatenate([x1, z, x2, z], axis=-1)


def _head_pad(w, heads):
    r = w.reshape(w.shape[0], heads, QK_HEAD)
    half = QK_ROPE // 2
    out = jnp.concatenate([r[..., :QK_NOPE], _rope_pad(r[..., QK_NOPE:QK_NOPE + half], r[..., QK_NOPE + half:])], axis=-1)
    return out.reshape(w.shape[0], heads * HEAD_PAD)


def _head_unpad(w, heads):
    r = w.reshape(w.shape[0], heads, HEAD_PAD)
    half = QK_ROPE // 2
    out = jnp.concatenate([r[..., :QK_NOPE], r[..., QK_NOPE:QK_NOPE + half],
                           r[..., QK_NOPE + 2 * half:QK_NOPE + 3 * half]], axis=-1)
    return out.reshape(w.shape[0], heads * QK_HEAD)


class _Dims:
    def __init__(self, d, seq):
        self.d = d
        self.seq = seq
        self.t_real = N_META + seq
        self.t = -(-self.t_real // LANES) * LANES
        self.dc = d // 2
        self.dp = d // 2
        self.pg = self.dp // len(POOL_WINDOWS)
        self.heads = d // 128
        self.dff = 4 * d
        self.a_end = 3 * self.dc
        self.q_end = self.a_end + Q_LORA
        self.kv_end = self.q_end + KV_LORA
        self.kr_end = self.kv_end + QK_ROPE
        self.pool_end = self.kr_end + self.dp
        self.d_in = self.pool_end + 3 * d
        self.r_pool = 3 * self.dc
        self.r_q = self.r_pool + self.dp
        self.r_kv = self.r_q + Q_LORA
        self.r_kr = self.r_kv + KV_LORA
        self.r_width = self.r_kr + HEAD_PAD


def _split_cols(a):
    return jnp.moveaxis(a.reshape(a.shape[:-1] + (2, a.shape[-1] // 2)), -2, -3)


def _join_cols(a):
    a = jnp.moveaxis(a, -3, -2)
    return a.reshape(a.shape[:-2] + (a.shape[-2] * a.shape[-1],))


def _in_weights(dm, pieces):
    w_t = _join_cols(pieces).reshape(dm.d_in, dm.d)
    half = QK_ROPE // 2
    kr = w_t[dm.kv_end:dm.kr_end]
    zeros = jnp.zeros((half, dm.d), BF16)
    kr_p = jnp.concatenate([kr[:half], zeros, kr[half:], zeros, jnp.zeros((HEAD_PAD - LANES, dm.d), BF16)], axis=0)
    return dict(
        wg_t=w_t[dm.pool_end:],
        wr_t=jnp.concatenate([w_t[:dm.a_end], w_t[dm.kr_end:dm.pool_end], w_t[dm.a_end:dm.kv_end], kr_p], axis=0))


def _other_weights(dm, g):
    out = {}
    if "w_ukv" in g:
        w_ukv = _cols(g["w_ukv"]).reshape(KV_LORA, dm.heads, QK_NOPE + V_HEAD)
        out["wkn"] = w_ukv[:, :, :QK_NOPE].reshape(KV_LORA, dm.heads * QK_NOPE)
        out["wv"] = w_ukv[:, :, QK_NOPE:].reshape(KV_LORA, dm.heads * V_HEAD)
    if "w_uq" in g:
        out["wuq"] = _head_pad(_cols(g["w_uq"]), dm.heads)
    if "pool_w" in g:
        out["wp"] = jnp.transpose(g["pool_w"], (1, 0, 2, 3)).reshape(len(POOL_WINDOWS), dm.pg, dm.pg)
    for name, key in (("w_branch_a", "wba"), ("w_branch_c", "wbc"), ("w_up", "wup")):
        if name in g:
            out[key] = _cols(g[name])
    for name, key in (("w_branch_b", "wbb"), ("w_o", "wo"), ("w_down", "wdn")):
        if name in g:
            out[key] = g[name].reshape(-1, dm.d)
    return out


def _small_weights(small):
    return dict(
        conv_w=small["conv_w"],
        attn_norm=small["attn_norm"][None], mlp_norm=small["mlp_norm"][None],
        q_lat_norm=small["q_lat_norm"][None], kv_lat_norm=small["kv_lat_norm"][None],
        q_norm=_head_pad(small["q_norm"][None], 1), k_norm=_head_pad(small["k_norm"][None], 1),
        pool_scale=small["pool_scale"][None],
    )


def _grad_piece(dm, dw, name):
    half = QK_ROPE // 2
    rows = lambda a: a.reshape((4, a.shape[0] // 4) + a.shape[1:])
    if name == "w_in":
        dwr, dwg = dw["wr_t"], dw["wg_t"]
        d_t = jnp.concatenate([
            dwr[:, :dm.r_pool], dwr[:, dm.r_q:dm.r_kr], dwr[:, dm.r_kr:dm.r_kr + half],
            dwr[:, dm.r_kr + 2 * half:dm.r_kr + 3 * half], dwr[:, dm.r_pool:dm.r_q], dwg], axis=1)
        out = d_t.reshape(2, 4, d_t.shape[1] // 4, d_t.shape[2])
    elif name == "w_ukv":
        out = _uncols(jnp.concatenate([dw["wkn"].reshape(KV_LORA, dm.heads, QK_NOPE),
                                       dw["wv"].reshape(KV_LORA, dm.heads, V_HEAD)], axis=-1).reshape(KV_LORA, -1))
    elif name == "w_uq":
        out = _uncols(_head_unpad(dw["wuq"], dm.heads))
    elif name == "pool_w":
        out = jnp.transpose(dw["wp"].reshape(len(POOL_WINDOWS), 4, dm.pg // 4, dm.pg), (1, 0, 2, 3))
    elif name in ("w_branch_a", "w_branch_c", "w_up"):
        out = dw[{"w_branch_a": "wba", "w_branch_c": "wbc", "w_up": "wup"}[name]]
    else:
        out = rows(dw[{"w_branch_b": "wbb", "w_o": "wo", "w_down": "wdn"}[name]])
    return out.astype(BF16)


def _layer_fwd(dm, W, x, cos_t, sin_t, tag, more=None, h=None):
    n = lambda s: f"{s}_{tag}"
    if h is None:
        h = _rms_fwd(x, W["attn_norm"], name=n("attn_norm"))
    gl = _mm(h, W["wg_t"], name=n("proj_gates"), tb=True, out_dtype=BF16)
    rest = _mm(h, W["wr_t"], name=n("proj_rest"), tb=True)
    if more is not None:
        W.update(more("after_proj", rest))
    y_a = _conv_fwd(rest, W["conv_w"], name=n("conv"), dc=dm.dc)
    y_c = _pool_fwd(rest, W["wp"], W["pool_scale"], name=n("pool"), seg0=dm.r_pool // dm.pg, pg=dm.pg)
    q_lat = _rms_fwd(rest, W["q_lat_norm"], name=n("q_lat_norm"), width=Q_LORA, seg=dm.r_q // Q_LORA)
    kv_lat = _rms_fwd(rest, W["kv_lat_norm"], name=n("kv_lat_norm"), width=KV_LORA, seg=dm.r_kv // KV_LORA)
    q_raw = _mm(q_lat, W["wuq"], name=n("up_q"), out_dtype=BF16)
    k_nope = _mm(kv_lat, W["wkn"], name=n("up_k"), out_dtype=BF16)
    v = _mm(kv_lat, W["wv"], name=n("up_v"), out_dtype=BF16)
    q, k = _qk_fwd(q_raw, k_nope, rest, cos_t, sin_t, W["q_norm"], W["k_norm"], name=n("qk_norm_rope"),
                   heads=dm.heads, kr_seg=dm.r_kr // HEAD_PAD)
    if more is not None:
        W.update(more("after_qk", q))
    y_b, lse = _flash_fwd(q, k, v, name=n("attention"), heads=dm.heads)
    pa = _mm(y_a, W["wba"], name=n("branch_a"), out_dtype=BF16)
    pb = _mm(y_b, W["wbb"], name=n("branch_b"), out_dtype=BF16)
    pc = _mm(y_c, W["wbc"], name=n("branch_c"), out_dtype=BF16)
    merged = _merge_fwd(gl, pa, pb, pc, name=n("merge"), d=dm.d)
    x1 = _mm(merged, W["wo"], name=n("out_proj"), add=x)
    h2 = _rms_fwd(x1, W["mlp_norm"], name=n("mlp_norm"))
    up, act = _mm(h2, W["wup"], name=n("mlp_up"), epi="relu2")
    x2 = _mm(act, W["wdn"], name=n("mlp_down"), add=x1, tm=704, tk=4096)
    saved = dict(x=x, h=h, gl=gl, rest=rest, y_a=y_a, y_c=y_c, q_lat=q_lat, kv_lat=kv_lat, q_raw=q_raw, k_nope=k_nope,
                 v=v, q=q, k=k, y_b=y_b, lse=lse, pa=pa, pb=pb, pc=pc, merged=merged, x1=x1, h2=h2, up=up, act=act)
    return x2, saved


def _layer_bwd(dm, W, S, dx2, dx2_b, cos_t, sin_t, tag, hook=None):
    n = lambda s: f"{s}_{tag}"
    dw, ds = {}, {}
    if hook is None:
        hook = lambda point, t, dw_so_far: ()
    dup = _mm(dx2_b, W["wdn"], name=n("d_mlp_down"), tb=True, aux=S["up"], epi="drelu2", out_dtype=BF16,
              after=hook("start", dx2, dw))
    dw["wdn"] = _mm(S["act"], dx2_b, name=n("dw_mlp_down"), ta=True, tm=512, out_dtype=BF16)
    dh2 = _mm(dup, W["wup"], name=n("d_mlp_up"), tb=True, tm=704, tk=4096)
    dw["wup"] = _mm(S["h2"], dup, name=n("dw_mlp_up"), ta=True, tm=512, out_dtype=BF16, pieces=4)
    dx1, dx1_b, ds["mlp_norm"] = _rms_bwd(dh2, S["x1"], W["mlp_norm"], name=n("d_mlp_norm"), res=dx2, bf16_copy=True)
    dmerged = _mm(dx1_b, W["wo"], name=n("d_out_proj"), tb=True, after=hook("after_mlp", dx1, dw))
    dw["wo"] = _mm(S["merged"], dx1_b, name=n("dw_out_proj"), ta=True, tm=512, out_dtype=BF16)
    dpa, dpb, dpc, dg0, dg1, dg2 = _merge_bwd(dmerged, S["gl"], S["pa"], S["pb"], S["pc"], name=n("d_merge"), d=dm.d)
    dgl = jnp.concatenate([dg0, dg1, dg2], axis=1)
    dy_a = _mm(dpa, W["wba"], name=n("d_branch_a"), tb=True)
    dw["wba"] = _mm(S["y_a"], dpa, name=n("dw_branch_a"), ta=True, tm=512, out_dtype=BF16, pieces=4)
    dy_b = _mm(dpb, W["wbb"], name=n("d_branch_b"), tb=True, out_dtype=BF16)
    dw["wbb"] = _mm(S["y_b"], dpb, name=n("dw_branch_b"), ta=True, tm=512, out_dtype=BF16)
    dy_c = _mm(dpc, W["wbc"], name=n("d_branch_c"), tb=True)
    dw["wbc"] = _mm(S["y_c"], dpc, name=n("dw_branch_c"), ta=True, tm=512, out_dtype=BF16, pieces=4)
    dq, dk, dv = _flash_bwd(S["q"], S["k"], S["v"], S["y_b"], dy_b, S["lse"], name=n("d_attention"), heads=dm.heads)
    after_attention = hook("after_attention", dq, dw)
    dq_raw, dk_nope, dk_rope, dgq, dgk = _qk_bwd(
        dq, dk, S["q_raw"], S["k_nope"], S["rest"], cos_t, sin_t, W["q_norm"], W["k_norm"], name=n("d_qk_norm_rope"),
        heads=dm.heads, kr_seg=dm.r_kr // HEAD_PAD)
    ds["q_norm"] = _head_unpad(dgq, 1)
    ds["k_norm"] = _head_unpad(dgk, 1)
    dkv_v = _mm(dv, W["wv"], name=n("d_up_v"), tb=True, after=after_attention)
    dq_lat_n = _mm(dq_raw, W["wuq"], name=n("d_up_q"), tb=True, after=hook("after_qk", dq_raw, dw))
    dw["wuq"] = _mm(S["q_lat"], dq_raw, name=n("dw_up_q"), ta=True, tm=512)
    dkv_lat_n = _mm(dk_nope, W["wkn"], name=n("d_up_k"), tb=True, add=dkv_v)
    dw["wkn"] = _mm(S["kv_lat"], dk_nope, name=n("dw_up_k"), ta=True, tm=512)
    dw["wv"] = _mm(S["kv_lat"], dv, name=n("dw_up_v"), ta=True, tm=512)
    dq_lat, ds["q_lat_norm"] = _rms_bwd(dq_lat_n, S["rest"], W["q_lat_norm"], name=n("d_q_lat_norm"), width=Q_LORA,
                                        seg=dm.r_q // Q_LORA, out_dtype=BF16)
    dkv_lat, ds["kv_lat_norm"] = _rms_bwd(dkv_lat_n, S["rest"], W["kv_lat_norm"], name=n("d_kv_lat_norm"), width=KV_LORA,
                                          seg=dm.r_kv // KV_LORA, out_dtype=BF16)
    du, db, dc, ds["conv_w"] = _conv_bwd(S["rest"], W["conv_w"], dy_a, name=n("d_conv"), dc=dm.dc)
    dpool, dw["wp"], ds["pool_scale"] = _pool_bwd(S["rest"], W["wp"], W["pool_scale"], dy_c, name=n("d_pool"),
                                                  seg0=dm.r_pool // dm.pg, pg=dm.pg)
    drest = jnp.concatenate([du, db, dc, dpool, dq_lat, dkv_lat, dk_rope], axis=1)
    dw["wg_t"] = _mm(dgl, S["h"], name=n("dw_proj_gates"), ta=True, tm=512, out_dtype=BF16, pieces=2)
    dw["wr_t"] = _mm(drest, S["h"], name=n("dw_proj_rest"), ta=True, tm=512, out_dtype=BF16, pieces=2)
    dh_g = _mm(dgl, W["wg_t"], name=n("d_proj_gates"), tm=704, tk=3072, after=hook("after_dw_in", dw["wr_t"], dw))
    dh = _mm(drest, W["wr_t"], name=n("d_proj_rest"), add=dh_g, tm=704, tk=2688, after=hook("after_dh_gates", dh_g, dw))
    dx, dx_b, ds["attn_norm"] = _rms_bwd(dh, S["x"], W["attn_norm"], name=n("d_attn_norm"), res=dx1, bf16_copy=True)
    return dx, dx_b, dw, ds


BIG = ("w_in", "w_uq", "w_ukv", "pool_w", "w_branch_a", "w_branch_b", "w_branch_c", "w_o", "w_up", "w_down")
REPLICATED = ("attn_norm", "q_lat_norm", "kv_lat_norm", "q_norm", "k_norm", "pool_scale", "mlp_norm")
WEIGHTS = ("meta_tokens", "attn_norm", "w_in", "conv_w", "q_lat_norm", "kv_lat_norm", "w_uq", "w_ukv", "q_norm",
           "k_norm", "pool_w", "pool_scale", "w_branch_a", "w_branch_b", "w_branch_c", "w_o", "mlp_norm", "w_up",
           "w_down")


def _pack(arrays):
    flat = jnp.concatenate([a.reshape(-1).astype(F32) for a in arrays])
    pad = (-flat.shape[0]) % (8 * LANES)
    return jnp.pad(flat, (0, pad)).reshape(-1, LANES)


def _unpack(flat, shapes):
    out, pos = [], 0
    flat = flat.reshape(-1)
    for shp in shapes:
        size = math.prod(shp)
        out.append(flat[pos:pos + size].reshape(shp))
        pos += size
    return out


def _update(w, g, m, v, name):
    shp = w.shape
    to2 = lambda a: a.reshape(-1, shp[-1])
    delta, nm, nv = _adamw(to2(w), to2(g), to2(m), to2(v), name=name)
    return delta.reshape(shp), nm.reshape(shp), nv.reshape(shp)


def _step(args):
    x = args["x"][0]
    seq, d = x.shape
    dm = _Dims(d, seq)
    xi, yi, ci = _coords()
    chip = 2 * xi + yi

    small_w = _gather_all(_pack([args["conv_w"], args["meta_tokens"]]), name="gather_small_weights")
    args = dict(args)
    for p in ("", "m_", "v_"):
        args[p + "w_in"] = jnp.swapaxes(args[p + "w_in"], 1, 2)
    order = [(k, l) for l in range(2) for k in BIG]
    shards = {n: args[n[0]][n[1]].astype(BF16) for n in order}
    for l in range(2):
        shards[("w_in", l)] = _split_cols(shards[("w_in", l)])
    small_w, shards[order[0]] = lax.optimization_barrier((small_w, shards[order[0]]))
    lands = {n: lax.empty((4,) + shards[n].shape, BF16) for n in order}
    last = ("w_up", "w_down")
    group_names = [[("w_in", 0)], [(k, 0) for k in BIG[1:] if k not in last], [(k, 0) for k in last],
                   [(k, 1) for k in BIG]]
    first, others = order[0], order[1:]
    sems, thru, token = _start_copies([shards[first], lands[first]], [_ici_gather_plan([(0, 1)])],
                                      name="start_gather_ici_first")
    shards[first], lands[first] = thru
    at = {n: i for i, n in enumerate(others)}
    sems_b, thru, token_b = _start_copies(
        [shards[n] for n in others] + [lands[n] for n in others] + [token],
        [_ici_gather_plan([(at[n], len(others) + at[n]) for n in g]) for g in group_names[1:]], name="start_gather_ici")
    sems = sems + sems_b
    for i, n in enumerate(others):
        shards[n], lands[n] = thru[i], thru[len(others) + i]

    def finish_gather(g, after, tag):
        names = group_names[g]
        k = len(names)
        plan, _ = _ici_gather_plan([(i, k + i) for i in range(k)])
        got = _wait_copies([shards[n] for n in names] + [lands[n] for n in names], sems[g], plan, after,
                           name=f"wait_gather_ici_{tag}")
        for i, n in enumerate(names):
            shards[n] = got[i]
        fwd = _d2d_forward_plan(list(range(k)))
        sems2, bufs2, tok2 = _start_copies(got[k:], [fwd], name=f"start_gather_d2d_{tag}")
        return names, bufs2, sems2[0], fwd[0], tok2

    def land_gather(pending, after, tag):
        names, bufs2, sems2, plan, tok2 = pending
        done = _wait_copies(bufs2, sems2, plan, tok2 if after is None else after, name=f"wait_gather_d2d_{tag}")
        return {n[0]: buf for n, buf in zip(names, done)}

    conv_shape, meta_shape = args["conv_w"].shape, args["meta_tokens"].shape
    per_chip = [_unpack(small_w[2 * j], [conv_shape, meta_shape]) for j in range(4)]
    conv_full = jnp.concatenate([p[0] for p in per_chip], axis=-1)
    meta_full = jnp.concatenate([p[1] for p in per_chip], axis=-1)

    layers = []
    for l in range(2):
        small = {k: args[k][l] for k in REPLICATED}
        small["conv_w"] = conv_full[l]
        layers.append(_small_weights(small))

    pos = jnp.arange(dm.t, dtype=F32)
    inv = ROPE_THETA ** (-jnp.arange(0, QK_ROPE, 2, dtype=F32) / QK_ROPE)
    ang = pos[:, None] * inv[None, :]
    cos_t = _rope_pad(jnp.cos(ang), jnp.cos(ang))
    sin_t = _rope_pad(-jnp.sin(ang), jnp.sin(ang))
    tail = jnp.zeros((dm.t - dm.t_real, d), F32)
    h0 = jnp.concatenate([meta_full, x, tail], axis=0)
    target = jnp.concatenate([jnp.zeros((N_META, d), F32), args["loss_target"][0], tail], axis=0)

    h_first = _rms_fwd(h0, layers[0]["attn_norm"], name="attn_norm_l0", after=(token, token_b))
    layers[0].update(_in_weights(dm, land_gather(finish_gather(0, h_first, "l0_in"), None, "l0_in")["w_in"]))
    def rest_of_layer0(point, after):
        g, tag = (1, "l0_mid") if point == "after_proj" else (2, "l0_mlp")
        return _other_weights(dm, land_gather(finish_gather(g, after, tag), None, tag))

    h1, saved0 = _layer_fwd(dm, layers[0], h0, cos_t, sin_t, "l0", more=rest_of_layer0, h=h_first)
    g1 = land_gather(finish_gather(3, saved0["y_b"], "l1"), h1, "l1")
    layers[1].update(_in_weights(dm, g1["w_in"]))
    layers[1].update(_other_weights(dm, g1))
    h2, saved1 = _layer_fwd(dm, layers[1], h1, cos_t, sin_t, "l1")
    sq, dy, dy_b = _loss(h2, target, name="loss_head", first=N_META, last=dm.t_real)
    loss = lax.psum(0.5 / d * sq[0, 0], ("x", "y", "c"))
    core, chip_flags = _one_hot(ci, 2), _one_hot(chip, 4)

    class Reduce:
        def __init__(self, names, dw, tag):
            self.names, self.tag, self.nb = names, tag, len(names)
            self.idx = [(i, self.nb + i) for i in range(self.nb)]
            parts = [_grad_piece(dm, dw, k) for k in names]
            parts = [p if k == "w_in" else _as3d(p) for p, k in zip(parts, names)]
            recv = [lax.empty((4,) + p.shape[2:] if k == "w_in" else (4, p.shape[1] // 2, p.shape[2]), BF16)
                    for p, k in zip(parts, names)]
            self.plan = _swap_half_plan(self.idx)
            self.sems, self.bufs, self.token = _start_copies(parts + recv, [self.plan], name=f"start_swap_{tag}")

        def _land(self, after, what):
            return _wait_copies(self.bufs, self.sems[0], self.plan[0], self.token if after is None else after,
                                name=f"wait_{what}_{self.tag}")

        def scatter(self, after=None):
            got = self._land(after, "swap")
            pairs = [_pair_sum(got[i], got[j], core, name=f"pair_sum_{k}_{self.tag}")
                     for (i, j), k in zip(self.idx, self.names)]
            self.plan = _scatter_plan(self.idx)
            self.sems, self.bufs, self.token = _start_copies(pairs + [lax.empty(p.shape, BF16) for p in pairs],
                                                             [self.plan], name=f"start_scatter_{self.tag}")
            return self.token

        def totals(self, after=None):
            got = self._land(after, "scatter")
            sums = [_chip_sum(got[i], got[j], chip_flags, name=f"chip_sum_{k}_{self.tag}")
                    for (i, j), k in zip(self.idx, self.names)]
            self.plan = _swap_total_plan(self.idx)
            self.sems, self.bufs, self.token = _start_copies(sums + [lax.empty(t.shape, F32) for t in sums],
                                                             [self.plan], name=f"start_swap_total_{self.tag}")
            return self.token

        def finish(self, after=None):
            got = self._land(after, "swap_total")
            return {k: (got[i], got[j]) for (i, j), k in zip(self.idx, self.names)}

    dh1, dh1_b, dw1, ds1 = _layer_bwd(dm, layers[1], saved1, dy, dy_b, cos_t, sin_t, "l1",
                               hook=lambda point, t, dw: (loss.reshape(1, 1),) if point == "start" else ())
    early = ("w_down", "w_up", "w_o", "w_branch_a", "w_branch_b", "w_branch_c")
    late = tuple(k for k in BIG if k not in early)
    stage = {}

    def during_layer0(point, t, dw):
        if point == "start":
            stage["l1"] = Reduce(BIG, dw1, "l1")
            return (stage["l1"].token,)
        if point == "after_mlp":
            return (stage["l1"].scatter(after=t),)
        if point == "after_attention":
            tok = stage["l1"].totals(after=t)
            stage["l0a"] = Reduce(early, dw, "l0a")
            return (tok, stage["l0a"].token)
        if point == "after_qk":
            stage["red1"] = stage["l1"].finish(after=t)
            return (stage["l0a"].scatter(after=t),)
        if point == "after_dw_in":
            tok = stage["l0a"].totals(after=t)
            stage["l0b"] = Reduce(late, dw, "l0b")
            return (tok, stage["l0b"].token)
        return (stage["l0b"].scatter(after=t),)

    dh0, _, dw0, ds0 = _layer_bwd(dm, layers[0], saved0, dh1, dh1_b, cos_t, sin_t, "l0", hook=during_layer0)
    grad_x = dh0[N_META:dm.t_real][None]
    red1 = stage["red1"]
    grads, delta, new_m, new_v = {}, {}, {}, {}

    def adamw_big(k, layer, red, prev, after):
        shp = args[k].shape
        wmv = [args[p + k].reshape(2, -1, shp[-1]) for p in ("", "m_", "v_")]
        return _adamw_layer(*wmv, *red[k], core, layer, prev, name=f"adamw_{k}_l{layer}", col_halves=k == "w_in",
                            after=after)

    def keep(k, out):
        shp = args[k].shape
        out = [o.reshape(shp) for o in out]
        grads[k], delta[k], new_m[k], new_v[k] = [jnp.swapaxes(o, 1, 2) for o in out] if k == "w_in" else out

    half_done = {}
    pin = dh0
    for k in BIG:
        half_done[k] = adamw_big(k, 1, red1, None, (pin,))
        pin = half_done[k][0]
    red0a = stage["l0a"].finish(after=pin)
    for k in early:
        out = adamw_big(k, 0, red0a, half_done[k], ())
        keep(k, out)
        pin = out[0]

    small_names = REPLICATED + ("conv_w",)
    small_parts = [jnp.stack([ds0[k].reshape(ds0[k].shape[-2:] if k == "conv_w" else (-1,)),
                              ds1[k].reshape(ds1[k].shape[-2:] if k == "conv_w" else (-1,))]) for k in small_names]
    small_parts.append(dh0[:N_META])
    small_all = _gather_all(_pack(small_parts), name="gather_small_grads", after=(pin,))
    small_sum = _sum_stack(small_all, name="sum_small_grads", out_dtype=F32)
    small_g = dict(zip(small_names + ("meta_tokens",), _unpack(small_sum, [p.shape for p in small_parts])))
    for k in REPLICATED:
        grads[k] = small_g[k]
    dcw = conv_shape[-1]
    grads["conv_w"] = lax.dynamic_slice_in_dim(small_g["conv_w"], chip * dcw, dcw, axis=2)
    dmeta = meta_shape[-1]
    grads["meta_tokens"] = lax.dynamic_slice_in_dim(small_g["meta_tokens"], chip * dmeta, dmeta, axis=1)

    stage["l0b"].totals(after=small_sum)
    red0b = stage["l0b"].finish()
    for k in late:
        keep(k, adamw_big(k, 0, red0b, half_done[k], ()))
    for k in WEIGHTS:
        if k not in BIG:
            grads[k] = grads[k].reshape(args[k].shape)
            delta[k], new_m[k], new_v[k] = _update(args[k], grads[k], args["m_" + k], args["v_" + k], f"adamw_{k}")
    return (loss, grad_x, *[grads[k] for k in WEIGHTS], *[delta[k] for k in WEIGHTS],
            *[new_m[k] for k in WEIGHTS], *[new_v[k] for k in WEIGHTS])


def kernel(x, meta_tokens, attn_norm, w_in, conv_w, q_lat_norm, kv_lat_norm, w_uq, w_ukv, q_norm, k_norm, pool_w, pool_scale, w_branch_a, w_branch_b, w_branch_c, w_o, mlp_norm, w_up, w_down, loss_target, m_meta_tokens, m_attn_norm, m_w_in, m_conv_w, m_q_lat_norm, m_kv_lat_norm, m_w_uq, m_w_ukv, m_q_norm, m_k_norm, m_pool_w, m_pool_scale, m_w_branch_a, m_w_branch_b, m_w_branch_c, m_w_o, m_mlp_norm, m_w_up, m_w_down, v_meta_tokens, v_attn_norm, v_w_in, v_conv_w, v_q_lat_norm, v_kv_lat_norm, v_w_uq, v_w_ukv, v_q_norm, v_k_norm, v_pool_w, v_pool_scale, v_w_branch_a, v_w_branch_b, v_w_branch_c, v_w_o, v_mlp_norm, v_w_up, v_w_down):
    return _step(dict(locals()))
```

```python
import functools
import math

import jax
import jax.numpy as jnp
from jax import lax
from jax.experimental import pallas as pl
from jax.experimental.pallas import tpu as pltpu

F32 = jnp.float32
BF16 = jnp.bfloat16
MESH = pl.DeviceIdType.MESH

EPS = 1e-6
N_META = 16
QK_NOPE = 128
QK_ROPE = 64
QK_HEAD = QK_NOPE + QK_ROPE
V_HEAD = 128
HEAD_PAD = 256
Q_LORA = 512
KV_LORA = 512
ROPE_THETA = 10000.0
POOL_WINDOWS = (2, 4, 8, 16)
HALO = 16
LANES = 128
ADAM_LR = 0.001
ADAM_B1 = 0.9
ADAM_B2 = 0.999
ADAM_EPS = 1e-08
ADAM_WD = 0.01
ADAM_STEP = 10
VMEM_LIMIT = 52 * 1024 * 1024
NEG = -1e30
ATTN_SCALE = QK_HEAD ** -0.5
LOG2_E = 1.4426950408889634
Q_FOLD = ATTN_SCALE * LOG2_E


def _tile(n, target, mult=LANES):
    best = None
    for t in range(mult, min(n, target) + 1, mult):
        if n % t == 0:
            best = t
    return n if best is None else best


def _params(sem=None):
    return pltpu.CompilerParams(dimension_semantics=sem, vmem_limit_bytes=VMEM_LIMIT)


def _mm(a, b, *, name, ta=False, tb=False, add=None, aux=None, epi=None, out_dtype=F32,
        tm=1056, tn=1024, tk=None, after=(), pieces=None):
    if ta:
        K, M = a.shape
    else:
        M, K = a.shape
    if tb:
        N, kb = b.shape
    else:
        kb, N = b.shape
    assert K == kb, (a.shape, b.shape, ta, tb)
    tm = _tile(M, tm, LANES if ta else 16)
    tn = _tile(N if pieces is None else N // pieces, tn, LANES)
    tk = K if tk is None else _tile(K, tk, LANES if (not ta or tb) else 16)
    nk = K // tk
    a_bytes, b_bytes = a.size * a.dtype.itemsize, b.size * b.dtype.itemsize
    j_outer = nk == 1 and a_bytes * (N // tn) + b_bytes < a_bytes + b_bytes * (M // tm)
    grid = (N // tn, M // tm, nk) if j_outer else (M // tm, N // tn, nk)
    row = (lambda g0, g1: g1) if j_outer else (lambda g0, g1: g0)
    col = (lambda g0, g1: g0) if j_outer else (lambda g0, g1: g1)

    if ta:
        a_spec = pl.BlockSpec((tk, tm), lambda g0, g1, k: (k, row(g0, g1)))
    else:
        a_spec = pl.BlockSpec((tm, tk), lambda g0, g1, k: (row(g0, g1), k))
    if tb:
        b_spec = pl.BlockSpec((tn, tk), lambda g0, g1, k: (col(g0, g1), k))
    else:
        b_spec = pl.BlockSpec((tk, tn), lambda g0, g1, k: (k, col(g0, g1)))
    o_spec = pl.BlockSpec((tm, tn), lambda g0, g1, k: (row(g0, g1), col(g0, g1)))
    per = None if pieces is None else N // pieces // tn
    in_specs = [a_spec, b_spec]
    operands = [a, b]
    if add is not None:
        in_specs.append(o_spec)
        operands.append(add)
    if aux is not None:
        in_specs.append(o_spec)
        operands.append(aux)
    after = tuple(after)
    in_specs += [pl.BlockSpec(memory_space=pl.ANY)] * len(after)
    operands += list(after)
    if epi == "relu2":
        out_shape = (jax.ShapeDtypeStruct((M, N), BF16), jax.ShapeDtypeStruct((M, N), BF16))
        out_specs = (o_spec, o_spec)
    elif pieces is not None:
        out_shape = jax.ShapeDtypeStruct((pieces, M, N // pieces), out_dtype)
        out_specs = pl.BlockSpec((1, tm, tn), lambda g0, g1, k: (col(g0, g1) // per, row(g0, g1), col(g0, g1) % per))
    else:
        out_shape = jax.ShapeDtypeStruct((M, N), out_dtype)
        out_specs = o_spec
    dims =(((0 if ta else 1,), (1 if tb else 0,)), ((), ()))
    has_add, has_aux = add is not None, aux is not None

    def body(*refs):
        a_ref, b_ref = refs[0], refs[1]
        pos = 2
        add_ref = aux_ref = None
        if has_add:
            add_ref = refs[pos]
            pos += 1
        if has_aux:
            aux_ref = refs[pos]
            pos += 1
        pos += len(after)
        n_out = 2 if epi == "relu2" else 1
        out_refs = refs[pos:pos + n_out]
        acc_ref = refs[pos + n_out] if nk > 1 else None

        part = lax.dot_general(a_ref[...].astype(BF16), b_ref[...].astype(BF16), dims,
                               preferred_element_type=F32)

        def finish(acc):
            if has_add:
                acc = acc + add_ref[...].astype(F32)
            if epi == "relu2":
                r = jnp.maximum(acc, 0.0)
                out_refs[0][...] = acc.astype(BF16)
                out_refs[1][...] = (r * r).astype(BF16)
            elif epi == "drelu2":
                u = aux_ref[...].astype(F32)
                out_refs[0][...] = (acc * (2.0 * jnp.maximum(u, 0.0))).astype(out_dtype)
            else:
                out_refs[0][...] = acc.astype(out_dtype).reshape(out_refs[0].shape)

        if nk == 1:
            finish(part)
        else:
            k = pl.program_id(2)

            @pl.when(k == 0)
            def _():
                acc_ref[...] = part

            @pl.when(k > 0)
            def _():
                acc_ref[...] += part

            @pl.when(k == nk - 1)
            def _():
                finish(acc_ref[...])

    scratch = [pltpu.VMEM((tm, tn), F32)] if nk > 1 else []
    return pl.pallas_call(
        body, name=name, grid=grid, in_specs=in_specs, out_specs=out_specs, out_shape=out_shape,
        scratch_shapes=scratch, compiler_params=_params(("parallel", "parallel", "arbitrary")),
    )(*operands)


def _rms_fwd(x, g, *, name, width=None, seg=0, tm=384, after=()):
    T = x.shape[0]
    width = x.shape[1] if width is None else width
    tm = _tile(T, tm, 16)
    after = tuple(after)

    def body(x_ref, g_ref, *rest):
        xf = x_ref[...].astype(F32)
        r = lax.rsqrt(jnp.mean(xf * xf, axis=-1, keepdims=True) + EPS)
        rest[-1][...] = (xf * r * g_ref[...]).astype(BF16)

    return pl.pallas_call(
        body, name=name, grid=(T // tm,),
        in_specs=[pl.BlockSpec((tm, width), lambda i: (i, seg)), pl.BlockSpec((1, width), lambda i: (0, 0))]
        + [pl.BlockSpec(memory_space=pl.ANY)] * len(after),
        out_specs=pl.BlockSpec((tm, width), lambda i: (i, 0)),
        out_shape=jax.ShapeDtypeStruct((T, width), BF16),
        compiler_params=_params(("parallel",)),
    )(x, g, *after)


def _rms_bwd(dy, x, g, *, name, width=None, seg=0, res=None, out_dtype=F32, tm=384, bf16_copy=False):
    T = x.shape[0]
    width = x.shape[1] if width is None else width
    tm = _tile(T, tm, 16)
    has_res = res is not None

    def body(*refs):
        dy_ref, x_ref, g_ref = refs[:3]
        res_ref = refs[3] if has_res else None
        dx_ref, dg_ref = refs[4 if has_res else 3], refs[-1]
        xf = x_ref[...].astype(F32)
        dyf = dy_ref[...].astype(F32)
        r = lax.rsqrt(jnp.mean(xf * xf, axis=-1, keepdims=True) + EPS)
        xhat = xf * r
        dyh = dyf * g_ref[...]
        dx = r * (dyh - xhat * jnp.mean(dyh * xhat, axis=-1, keepdims=True))
        if has_res:
            dx = dx + res_ref[...].astype(F32)
        dx_ref[...] = dx.astype(out_dtype)
        if bf16_copy:
            refs[-2][...] = dx.astype(BF16)
        part = jnp.sum(dyf * xhat, axis=0, keepdims=True)

        @pl.when(pl.program_id(0) == 0)
        def _():
            dg_ref[...] = part

        @pl.when(pl.program_id(0) > 0)
        def _():
            dg_ref[...] += part

    row = pl.BlockSpec((tm, width), lambda i: (i, 0))
    in_specs = [row, pl.BlockSpec((tm, width), lambda i: (i, seg)), pl.BlockSpec((1, width), lambda i: (0, 0))]
    operands = [dy, x, g]
    if has_res:
        in_specs.append(row)
        operands.append(res)
    vec = pl.BlockSpec((1, width), lambda i: (0, 0))
    full = [jax.ShapeDtypeStruct((T, width), out_dtype)] + ([jax.ShapeDtypeStruct((T, width), BF16)] if bf16_copy else [])
    return pl.pallas_call(
        body, name=name, grid=(T // tm,), in_specs=in_specs,
        out_specs=tuple([row] * len(full) + [vec]),
        out_shape=tuple(full + [jax.ShapeDtypeStruct((1, width), F32)]),
        compiler_params=_params(("arbitrary",)),
    )(*operands)


def _down(ext, k):
    return pltpu.roll(ext, k, 0)


def _up(ext, k):
    return pltpu.roll(ext, ext.shape[0] - k, 0)


def _pre_halo(ref, r, R):
    start = pl.multiple_of(jnp.maximum(r * R - HALO, 0), 8)
    keep = (r > 0).astype(F32)
    return ref[pl.ds(start, HALO), :].astype(F32) * keep


def _post_halo(ref, r, R, n_chunks):
    start = pl.multiple_of(jnp.minimum(r * R + R, (n_chunks - 1) * R + R - HALO), 8)
    keep = (r < n_chunks - 1).astype(F32)
    return ref[pl.ds(start, HALO), :].astype(F32) * keep


def _chunk(ref, r, R):
    return ref[pl.ds(pl.multiple_of(r * R, 8), R), :].astype(F32)


def _conv_fwd(rest, conv_w, *, name, dc, tc=128, rows=1056):
    T = rest.shape[0]
    tc = _tile(dc, tc)
    nb = dc // tc
    R = _tile(T, rows, 16)
    n_chunks = T // R

    def body(u_ref, b_ref, c_ref, w_ref, y_ref):
        w0, w1, w2 = w_ref[0:1, :], w_ref[1:2, :], w_ref[2:3, :]

        def chunk(r, carry):
            cu = _chunk(c_ref, r, R) * _chunk(u_ref, r, R)
            ext = jnp.concatenate([_pre_halo(c_ref, r, R) * _pre_halo(u_ref, r, R), cu], axis=0)
            conv = w0 * _down(ext, 2)[HALO:] + w1 * _down(ext, 1)[HALO:] + w2 * cu
            y_ref[pl.ds(pl.multiple_of(r * R, 8), R), :] = (_chunk(b_ref, r, R) * conv).astype(BF16)
            return carry

        lax.fori_loop(0, n_chunks, chunk, 0)

    col = lambda off: pl.BlockSpec((T, tc), lambda j: (0, off * nb + j))
    return pl.pallas_call(
        body, name=name, grid=(nb,),
        in_specs=[col(0), col(1), col(2), pl.BlockSpec((3, tc), lambda j: (0, j))],
        out_specs=pl.BlockSpec((T, tc), lambda j: (0, j)),
        out_shape=jax.ShapeDtypeStruct((T, dc), BF16),
        compiler_params=_params(("parallel",)),
    )(rest, rest, rest, conv_w)


def _conv_bwd(rest, conv_w, dy, *, name, dc, tc=128, rows=1056):
    T = rest.shape[0]
    tc = _tile(dc, tc)
    nb = dc // tc
    R = _tile(T, rows, 16)
    n_chunks = T // R

    def body(u_ref, b_ref, c_ref, w_ref, dy_ref, du_ref, db_ref, dc_ref, dw_ref):
        w0, w1, w2 = w_ref[0:1, :], w_ref[1:2, :], w_ref[2:3, :]

        def chunk(r, carry):
            a0, a1, a2 = carry
            u, b, c = _chunk(u_ref, r, R), _chunk(b_ref, r, R), _chunk(c_ref, r, R)
            dy_c = _chunk(dy_ref, r, R)
            cu = c * u
            ext = jnp.concatenate([_pre_halo(c_ref, r, R) * _pre_halo(u_ref, r, R), cu], axis=0)
            cu1, cu2 = _down(ext, 1)[HALO:], _down(ext, 2)[HALO:]
            conv = w0 * cu2 + w1 * cu1 + w2 * cu
            dconv = dy_c * b
            dext = jnp.concatenate(
                [dconv, _post_halo(dy_ref, r, R, n_chunks) * _post_halo(b_ref, r, R, n_chunks)], axis=0)
            dcu = w2 * dconv + w1 * _up(dext, 1)[:R] + w0 * _up(dext, 2)[:R]
            rows_at = pl.ds(pl.multiple_of(r * R, 8), R)
            db_ref[rows_at, :] = (dy_c * conv).astype(BF16)
            du_ref[rows_at, :] = (dcu * c).astype(BF16)
            dc_ref[rows_at, :] = (dcu * u).astype(BF16)
            return (a0 + jnp.sum(dconv * cu2, axis=0, keepdims=True),
                    a1 + jnp.sum(dconv * cu1, axis=0, keepdims=True),
                    a2 + jnp.sum(dconv * cu, axis=0, keepdims=True))

        zero = jnp.zeros((1, tc), F32)
        a0, a1, a2 = lax.fori_loop(0, n_chunks, chunk, (zero, zero, zero))
        dw_ref[0:1, :] = a0
        dw_ref[1:2, :] = a1
        dw_ref[2:3, :] = a2

    col = lambda off: pl.BlockSpec((T, tc), lambda j: (0, off * nb + j))
    own = pl.BlockSpec((T, tc), lambda j: (0, j))
    return pl.pallas_call(
        body, name=name, grid=(nb,),
        in_specs=[col(0), col(1), col(2), pl.BlockSpec((3, tc), lambda j: (0, j)), own],
        out_specs=(own, own, own, pl.BlockSpec((3, tc), lambda j: (0, j))),
        out_shape=(jax.ShapeDtypeStruct((T, dc), BF16),) * 3 + (jax.ShapeDtypeStruct((3, dc), F32),),
        compiler_params=_params(("parallel",)),
    )(rest, rest, rest, conv_w, dy)


def _window_count(r, R, n_rows, w, first_row_offset):
    t = lax.broadcasted_iota(jnp.int32, (n_rows, 1), 0) + (r * R + first_row_offset)
    return jnp.minimum(t + 1, w).astype(F32)


def _pool_fwd(rest, pool_w, pool_scale, *, name, seg0, pg, rows=1056):
    T = rest.shape[0]
    R = _tile(T, rows, 16)
    n_chunks = T // R
    n_groups = len(POOL_WINDOWS)

    def body(x_ref, w_ref, s_ref, y_ref):
        def run(window):
            def chunk(r, carry):
                g = _chunk(x_ref, r, R)
                s = jnp.concatenate([_pre_halo(x_ref, r, R), g], axis=0)
                k = 1
                while k < window:
                    s = s + _down(s, k)
                    k *= 2
                pooled = s[HALO:] / _window_count(r, R, R, window, 0) - g
                mixed = jnp.dot(pooled.astype(BF16), w_ref[0], preferred_element_type=F32)
                y_ref[pl.ds(pl.multiple_of(r * R, 8), R), :] = (mixed * s_ref[...]).astype(BF16)
                return carry

            lax.fori_loop(0, n_chunks, chunk, 0)

        for gi, window in enumerate(POOL_WINDOWS):
            pl.when(pl.program_id(0) == gi)(functools.partial(run, window))

    return pl.pallas_call(
        body, name=name, grid=(n_groups,),
        in_specs=[pl.BlockSpec((T, pg), lambda g: (0, seg0 + g)),
                  pl.BlockSpec((1, pg, pg), lambda g: (g, 0, 0)),
                  pl.BlockSpec((1, pg), lambda g: (0, g))],
        out_specs=pl.BlockSpec((T, pg), lambda g: (0, g)),
        out_shape=jax.ShapeDtypeStruct((T, n_groups * pg), BF16),
        compiler_params=_params(("parallel",)),
    )(rest, pool_w, pool_scale)


def _pool_bwd(rest, pool_w, pool_scale, dy, *, name, seg0, pg, rows=1056):
    T = rest.shape[0]
    R = _tile(T, rows, 16)
    n_chunks = T // R
    n_groups = len(POOL_WINDOWS)

    def body(x_ref, w_ref, s_ref, dy_ref, dx_ref, dw_ref, ds_ref):
        def run(window):
            def chunk(r, carry):
                dw_acc, ds_acc = carry
                g = _chunk(x_ref, r, R)
                s = jnp.concatenate([_pre_halo(x_ref, r, R), g], axis=0)
                k = 1
                while k < window:
                    s = s + _down(s, k)
                    k *= 2
                pooled = (s[HALO:] / _window_count(r, R, R, window, 0) - g).astype(BF16)
                mixed = jnp.dot(pooled, w_ref[0], preferred_element_type=F32)
                dy_c = _chunk(dy_ref, r, R)
                dm_ext = (jnp.concatenate([dy_c, _post_halo(dy_ref, r, R, n_chunks)], axis=0)
                          * s_ref[...]).astype(BF16)
                dpool_ext = lax.dot_general(dm_ext, w_ref[0], (((1,), (1,)), ((), ())),
                                            preferred_element_type=F32)
                a = dpool_ext / _window_count(r, R, R + HALO, window, 0)
                k = 1
                while k < window:
                    a = a + _up(a, k)
                    k *= 2
                dx_ref[pl.ds(pl.multiple_of(r * R, 8), R), :] = (a[:R] - dpool_ext[:R]).astype(BF16)
                dw_acc = dw_acc + lax.dot_general(pooled, dm_ext[:R], (((0,), (0,)), ((), ())),
                                                  preferred_element_type=F32)
                ds_acc = ds_acc + jnp.sum(dy_c * mixed, axis=0, keepdims=True)
                return dw_acc, ds_acc

            dw_acc, ds_acc = lax.fori_loop(0, n_chunks, chunk,
                                           (jnp.zeros((pg, pg), F32), jnp.zeros((1, pg), F32)))
            dw_ref[0] = dw_acc
            ds_ref[...] = ds_acc

        for gi, window in enumerate(POOL_WINDOWS):
            pl.when(pl.program_id(0) == gi)(functools.partial(run, window))

    own = pl.BlockSpec((T, pg), lambda g: (0, g))
    return pl.pallas_call(
        body, name=name, grid=(n_groups,),
        in_specs=[pl.BlockSpec((T, pg), lambda g: (0, seg0 + g)),
                  pl.BlockSpec((1, pg, pg), lambda g: (g, 0, 0)),
                  pl.BlockSpec((1, pg), lambda g: (0, g)), own],
        out_specs=(own, pl.BlockSpec((1, pg, pg), lambda g: (g, 0, 0)), pl.BlockSpec((1, pg), lambda g: (0, g))),
        out_shape=(jax.ShapeDtypeStruct((T, n_groups * pg), BF16),
                   jax.ShapeDtypeStruct((n_groups, pg, pg), F32),
                   jax.ShapeDtypeStruct((1, n_groups * pg), F32)),
        compiler_params=_params(("parallel",)),
    )(rest, pool_w, pool_scale, dy)


def _rope(r, cos_t, sin_t):
    return r * cos_t + pltpu.roll(r, LANES // 2, 1) * sin_t


def _rope_t(d, cos_t, sin_t):
    return d * cos_t + pltpu.roll(d * sin_t, LANES // 2, 1)


def _qk_fwd(q_raw, k_nope, rest, cos_t, sin_t, q_norm, k_norm, *, name, heads, kr_seg, tm=192):
    T = q_raw.shape[0]
    tm = _tile(T, tm, 16)

    def body(q_ref, kn_ref, kr_ref, c_ref, s_ref, gq_ref, gk_ref, qo_ref, ko_ref):
        cos_b, sin_b = c_ref[...], s_ref[...]
        kr = kr_ref[:, 0:LANES]
        kr_ss = jnp.sum(kr * kr, axis=-1, keepdims=True)
        gq, gk = gq_ref[...], gk_ref[...]
        for h in range(heads):
            lo = h * HEAD_PAD
            q = q_ref[:, lo:lo + HEAD_PAD].astype(F32)
            rq = lax.rsqrt(jnp.sum(q * q, axis=-1, keepdims=True) / QK_HEAD + EPS)
            qn = q * (rq * Q_FOLD) * gq
            qo_ref[:, lo:lo + LANES] = qn[:, :LANES].astype(BF16)
            qo_ref[:, lo + LANES:lo + HEAD_PAD] = _rope(qn[:, LANES:], cos_b, sin_b).astype(BF16)
            kn = kn_ref[:, h * LANES:(h + 1) * LANES].astype(F32)
            rk = lax.rsqrt((jnp.sum(kn * kn, axis=-1, keepdims=True) + kr_ss) / QK_HEAD + EPS)
            ko_ref[:, lo:lo + LANES] = (kn * rk * gk[:, :LANES]).astype(BF16)
            ko_ref[:, lo + LANES:lo + HEAD_PAD] = _rope(kr * rk * gk[:, LANES:], cos_b, sin_b).astype(BF16)

    wq, wk = heads * HEAD_PAD, heads * LANES
    return pl.pallas_call(
        body, name=name, grid=(T // tm,),
        in_specs=[pl.BlockSpec((tm, wq), lambda i: (i, 0)), pl.BlockSpec((tm, wk), lambda i: (i, 0)),
                  pl.BlockSpec((tm, HEAD_PAD), lambda i: (i, kr_seg)),
                  pl.BlockSpec((tm, LANES), lambda i: (i, 0)), pl.BlockSpec((tm, LANES), lambda i: (i, 0)),
                  pl.BlockSpec((1, HEAD_PAD), lambda i: (0, 0)), pl.BlockSpec((1, HEAD_PAD), lambda i: (0, 0))],
        out_specs=(pl.BlockSpec((tm, wq), lambda i: (i, 0)), pl.BlockSpec((tm, wq), lambda i: (i, 0))),
        out_shape=(jax.ShapeDtypeStruct((T, wq), BF16), jax.ShapeDtypeStruct((T, wq), BF16)),
        compiler_params=_params(("parallel",)),
    )(q_raw, k_nope, rest, cos_t, sin_t, q_norm, k_norm)


def _qk_bwd(dq, dk, q_raw, k_nope, rest, cos_t, sin_t, q_norm, k_norm, *, name, heads, kr_seg, tm=128):
    T = q_raw.shape[0]
    tm = _tile(T, tm, 16)

    def body(dq_ref, dk_ref, q_ref, kn_ref, kr_ref, c_ref, s_ref, gq_ref, gk_ref,
             dqr_ref, dkn_ref, dkr_ref, dgq_ref, dgk_ref):
        cos_b, sin_b = c_ref[...], s_ref[...]
        kr = kr_ref[:, 0:LANES]
        kr_ss = jnp.sum(kr * kr, axis=-1, keepdims=True)
        gq, gk = gq_ref[...], gk_ref[...]
        dgq = jnp.zeros((1, HEAD_PAD), F32)
        dgk_n = jnp.zeros((1, LANES), F32)
        dgk_r = jnp.zeros((1, LANES), F32)
        dkr = jnp.zeros((tm, LANES), F32)
        for h in range(heads):
            lo = h * HEAD_PAD
            q = q_ref[:, lo:lo + HEAD_PAD].astype(F32)
            rq = lax.rsqrt(jnp.sum(q * q, axis=-1, keepdims=True) / QK_HEAD + EPS)
            qhat = q * rq
            dqn = jnp.concatenate([dq_ref[:, lo:lo + LANES],
                                   _rope_t(dq_ref[:, lo + LANES:lo + HEAD_PAD], cos_b, sin_b)], axis=1) * ATTN_SCALE
            dgq = dgq + jnp.sum(dqn * qhat, axis=0, keepdims=True)
            dqh = dqn * gq
            dqr_ref[:, lo:lo + HEAD_PAD] = (
                rq * (dqh - qhat * (jnp.sum(dqh * qhat, axis=-1, keepdims=True) / QK_HEAD))).astype(BF16)
            kn = kn_ref[:, h * LANES:(h + 1) * LANES].astype(F32)
            rk = lax.rsqrt((jnp.sum(kn * kn, axis=-1, keepdims=True) + kr_ss) / QK_HEAD + EPS)
            khat_n, khat_r = kn * rk, kr * rk
            dkn_n = dk_ref[:, lo:lo + LANES] * (1.0 / LOG2_E)
            dkn_r = _rope_t(dk_ref[:, lo + LANES:lo + HEAD_PAD], cos_b, sin_b) * (1.0 / LOG2_E)
            dgk_n = dgk_n + jnp.sum(dkn_n * khat_n, axis=0, keepdims=True)
            dgk_r = dgk_r + jnp.sum(dkn_r * khat_r, axis=0, keepdims=True)
            dkh_n, dkh_r = dkn_n * gk[:, :LANES], dkn_r * gk[:, LANES:]
            proj = (jnp.sum(dkh_n * khat_n, axis=-1, keepdims=True)
                    + jnp.sum(dkh_r * khat_r, axis=-1, keepdims=True)) / QK_HEAD
            dkn_ref[:, h * LANES:(h + 1) * LANES] = (rk * (dkh_n - khat_n * proj)).astype(BF16)
            dkr = dkr + rk * (dkh_r - khat_r * proj)
        dkr_ref[:, 0:LANES] = dkr.astype(BF16)
        dkr_ref[:, LANES:HEAD_PAD] = jnp.zeros((tm, HEAD_PAD - LANES), BF16)
        dgk = jnp.concatenate([dgk_n, dgk_r], axis=1)

        @pl.when(pl.program_id(0) == 0)
        def _():
            dgq_ref[...] = dgq
            dgk_ref[...] = dgk

        @pl.when(pl.program_id(0) > 0)
        def _():
            dgq_ref[...] += dgq
            dgk_ref[...] += dgk

    wq, wk = heads * HEAD_PAD, heads * LANES
    row = lambda w: pl.BlockSpec((tm, w), lambda i: (i, 0))
    vec = pl.BlockSpec((1, HEAD_PAD), lambda i: (0, 0))
    return pl.pallas_call(
        body, name=name, grid=(T // tm,),
        in_specs=[row(wq), row(wq), row(wq), row(wk), pl.BlockSpec((tm, HEAD_PAD), lambda i: (i, kr_seg)),
                  row(LANES), row(LANES), vec, vec],
        out_specs=(row(wq), row(wk), row(HEAD_PAD), vec, vec),
        out_shape=(jax.ShapeDtypeStruct((T, wq), BF16), jax.ShapeDtypeStruct((T, wk), BF16),
                   jax.ShapeDtypeStruct((T, HEAD_PAD), BF16),
                   jax.ShapeDtypeStruct((1, HEAD_PAD), F32), jax.ShapeDtypeStruct((1, HEAD_PAD), F32)),
        compiler_params=_params(("arbitrary",)),
    )(dq, dk, q_raw, k_nope, rest, cos_t, sin_t, q_norm, k_norm)


def _causal_mask(s):
    row = lax.broadcasted_iota(jnp.int32, s.shape, 0)
    col = lax.broadcasted_iota(jnp.int32, s.shape, 1)
    return jnp.where(row >= col, s, NEG)


def _flash_fwd(q, k, v, *, name, heads, tq=384, hp=2, parts=2):
    T = q.shape[0]
    tq = _tile(T, tq, LANES)
    nq = T // tq
    tr = tq // parts
    nt = (((1,), (1,)), ((), ()))
    chains = [(h, r) for h in range(hp) for r in range(parts)]

    def body(q_ref, k_ref, v_ref, o_ref, lse_ref, acc_ref):
        def q_block(i, carry):
            rows_at = [pl.ds(pl.multiple_of(i * tq + r * tr, tr), tr) for r in range(parts)]
            qbs = [q_ref[rows_at[r], h * HEAD_PAD:(h + 1) * HEAD_PAD] for h, r in chains]
            for c in range(len(chains)):
                acc_ref[c] = jnp.zeros((tr, V_HEAD), F32)

            def step(j, state, masked):
                k_at = pl.ds(pl.multiple_of(j * tq, tq), tq)
                new = []
                scores = [lax.dot_general(qb, k_ref[k_at, h * HEAD_PAD:(h + 1) * HEAD_PAD], nt,
                                          preferred_element_type=F32) for qb, (h, r) in zip(qbs, chains)]
                for c, (s, (h, r)) in enumerate(zip(scores, chains)):
                    m, l = state[c]
                    if masked:
                        row = lax.broadcasted_iota(jnp.int32, s.shape, 0) + r * tr
                        s = jnp.where(row >= lax.broadcasted_iota(jnp.int32, s.shape, 1), s, NEG)
                    m_new = jnp.maximum(m, jnp.max(s, axis=-1, keepdims=True))
                    p = jnp.exp2(s - m_new)
                    alpha = jnp.exp2(m - m_new)
                    new.append((m_new, alpha * l + jnp.sum(p, axis=-1, keepdims=True)))
                    acc_ref[c] = alpha * acc_ref[c] + jnp.dot(p.astype(BF16), v_ref[k_at, h * V_HEAD:(h + 1) * V_HEAD],
                                                              preferred_element_type=F32)
                return tuple(new)

            init = tuple((jnp.full((tr, 1), NEG, F32), jnp.zeros((tr, 1), F32)) for _ in chains)
            state = lax.fori_loop(0, i, lambda j, st: step(j, st, False), init)
            state = step(i, state, True)
            for c, ((m, l), (h, r)) in enumerate(zip(state, chains)):
                o_ref[rows_at[r], h * V_HEAD:(h + 1) * V_HEAD] = (acc_ref[c] / l).astype(BF16)
                lse_ref[h, rows_at[r], :] = jnp.broadcast_to(m + jnp.log2(l), (tr, LANES))
            return carry

        lax.fori_loop(0, nq, q_block, 0)

    qk_spec = pl.BlockSpec((T, hp * HEAD_PAD), lambda g: (0, g))
    v_spec = pl.BlockSpec((T, hp * V_HEAD), lambda g: (0, g))
    return pl.pallas_call(
        body, name=name, grid=(heads // hp,), in_specs=[qk_spec, qk_spec, v_spec],
        out_specs=(v_spec, pl.BlockSpec((hp, T, LANES), lambda g: (g, 0, 0))),
        out_shape=(jax.ShapeDtypeStruct((T, heads * V_HEAD), BF16), jax.ShapeDtypeStruct((heads, T, LANES), F32)),
        scratch_shapes=[pltpu.VMEM((len(chains), tr, V_HEAD), F32)],
        compiler_params=_params(("parallel",)),
    )(q, k, v)


def _flash_bwd(q, k, v, o, do, lse, *, name, heads, tq=384, after=()):
    T = q.shape[0]
    tq = _tile(T, tq, LANES)
    nq = T // tq
    nt = (((1,), (1,)), ((), ()))
    tn = (((0,), (0,)), ((), ()))

    after = tuple(after)

    def body(q_ref, k_ref, v_ref, o_ref, do_ref, lse_ref, *rest):
        dq_ref, dk_ref, dv_ref, delta_ref, dv_acc_ref = rest[len(after):]
        def fill_delta(i, carry):
            at = pl.ds(pl.multiple_of(i * tq, tq), tq)
            d = jnp.sum(o_ref[at, :].astype(F32) * do_ref[at, :].astype(F32), axis=-1, keepdims=True)
            delta_ref[at, :] = jnp.broadcast_to(d, (tq, LANES))
            dq_ref[at, :] = jnp.zeros((tq, HEAD_PAD), F32)
            return carry

        lax.fori_loop(0, nq, fill_delta, 0)

        def kv_block(j, carry):
            k_at = pl.ds(pl.multiple_of(j * tq, tq), tq)
            kb, vb = k_ref[k_at, :], v_ref[k_at, :]

            def steps(blocks, masked):
                at = [pl.ds(pl.multiple_of(i * tq, tq), tq) for i in blocks]
                qbs = [q_ref[a, :] for a in at]
                dobs = [do_ref[a, :] for a in at]
                scores = [lax.dot_general(qb, kb, nt, preferred_element_type=F32) for qb in qbs]
                dps = [lax.dot_general(dob, vb, nt, preferred_element_type=F32) for dob in dobs]
                for a, qb, dob, sc, dp in zip(at, qbs, dobs, scores, dps):
                    if masked:
                        sc = _causal_mask(sc)
                    p = jnp.exp2(sc - lse_ref[0, a, :][:, 0:1])
                    ds = (p * (dp - delta_ref[a, :][:, 0:1])).astype(BF16)
                    dv_part = lax.dot_general(p.astype(BF16), dob, tn, preferred_element_type=F32)
                    dk_part = lax.dot_general(ds, qb, tn, preferred_element_type=F32)
                    if masked:
                        dv_acc_ref[...] = dv_part
                        dk_ref[k_at, :] = dk_part
                    else:
                        dv_acc_ref[...] += dv_part
                        dk_ref[k_at, :] += dk_part
                    dq_ref[a, :] += jnp.dot(ds, kb, preferred_element_type=F32)

            def two_blocks(t, carry):
                steps([j + 1 + 2 * t, j + 2 + 2 * t], False)
                return carry

            steps([j], True)
            rest = nq - 1 - j
            lax.fori_loop(0, rest // 2, two_blocks, 0)

            @pl.when(rest % 2 == 1)
            def _():
                steps([nq - 1], False)

            dv_ref[k_at, :] = dv_acc_ref[...].astype(BF16)
            return carry

        lax.fori_loop(0, nq, kv_block, 0)

    qk_spec = pl.BlockSpec((T, HEAD_PAD), lambda h: (0, h))
    v_spec = pl.BlockSpec((T, V_HEAD), lambda h: (0, h))
    return pl.pallas_call(
        body, name=name, grid=(heads,),
        in_specs=[qk_spec, qk_spec, v_spec, v_spec, v_spec, pl.BlockSpec((1, T, LANES), lambda h: (h, 0, 0))]
        + [pl.BlockSpec(memory_space=pl.ANY)] * len(after),
        out_specs=(qk_spec, qk_spec, v_spec),
        out_shape=(jax.ShapeDtypeStruct((T, heads * HEAD_PAD), F32), jax.ShapeDtypeStruct((T, heads * HEAD_PAD), F32),
                   jax.ShapeDtypeStruct((T, heads * V_HEAD), BF16)),
        scratch_shapes=[pltpu.VMEM((T, LANES), F32), pltpu.VMEM((tq, V_HEAD), F32)],
        compiler_params=_params(("parallel",)),
    )(q, k, v, o, do, lse, *after)


def _merge_fwd(gl, pa, pb, pc, *, name, d, tm=384, tn=1024):
    T = pa.shape[0]
    tm, tn = _tile(T, tm, 16), _tile(d, tn)
    nb = d // tn

    def body(g0, g1, g2, a, b, c, o_ref):
        f = lambda ref: ref[...].astype(F32)
        o_ref[...] = (jax.nn.sigmoid(f(g0)) * f(a) + jax.nn.sigmoid(f(g1)) * f(b)
                      + jax.nn.sigmoid(f(g2)) * f(c)).astype(BF16)

    gate = lambda n: pl.BlockSpec((tm, tn), lambda i, j: (i, n * nb + j))
    blk = pl.BlockSpec((tm, tn), lambda i, j: (i, j))
    return pl.pallas_call(
        body, name=name, grid=(T // tm, nb), in_specs=[gate(0), gate(1), gate(2), blk, blk, blk],
        out_specs=blk, out_shape=jax.ShapeDtypeStruct((T, d), BF16),
        compiler_params=_params(("parallel", "parallel")),
    )(gl, gl, gl, pa, pb, pc)


def _merge_bwd(dm, gl, pa, pb, pc, *, name, d, tm=384, tn=1024):
    T = pa.shape[0]
    tm, tn = _tile(T, tm, 16), _tile(d, tn)
    nb = d // tn

    def body(dm_ref, g0, g1, g2, a, b, c, da, db, dc, dg0, dg1, dg2):
        dmv = dm_ref[...].astype(F32)
        for g_ref, p_ref, dp_ref, dg_ref in ((g0, a, da, dg0), (g1, b, db, dg1), (g2, c, dc, dg2)):
            sg = jax.nn.sigmoid(g_ref[...].astype(F32))
            dp_ref[...] = (dmv * sg).astype(BF16)
            dg_ref[...] = (dmv * p_ref[...].astype(F32) * sg * (1.0 - sg)).astype(BF16)

    gate = lambda n: pl.BlockSpec((tm, tn), lambda i, j: (i, n * nb + j))
    blk = pl.BlockSpec((tm, tn), lambda i, j: (i, j))
    return pl.pallas_call(
        body, name=name, grid=(T // tm, nb), in_specs=[blk, gate(0), gate(1), gate(2), blk, blk, blk],
        out_specs=(blk,) * 6, out_shape=(jax.ShapeDtypeStruct((T, d), BF16),) * 6,
        compiler_params=_params(("parallel", "parallel")),
    )(dm, gl, gl, gl, pa, pb, pc)


def _loss(y, target, *, name, first, last, tm=384):
    T, d = y.shape
    tm = _tile(T, tm, 16)

    def body(y_ref, t_ref, loss_ref, dy_ref, dyb_ref):
        i = pl.program_id(0)
        row = lax.broadcasted_iota(jnp.int32, (tm, 1), 0) + i * tm
        real = jnp.logical_and(row >= first, row < last)
        err = jnp.where(real, y_ref[...] - t_ref[...], 0.0)
        dy_ref[...] = err * (1.0 / d)
        dyb_ref[...] = (err * (1.0 / d)).astype(BF16)
        part = jnp.broadcast_to(jnp.sum(err * err, keepdims=True).reshape(1, 1), (1, LANES))

        @pl.when(i == 0)
        def _():
            loss_ref[...] = part

        @pl.when(i > 0)
        def _():
            loss_ref[...] += part

    blk = pl.BlockSpec((tm, d), lambda i: (i, 0))
    return pl.pallas_call(
        body, name=name, grid=(T // tm,), in_specs=[blk, blk],
        out_specs=(pl.BlockSpec((1, LANES), lambda i: (0, 0)), blk, blk),
        out_shape=(jax.ShapeDtypeStruct((1, LANES), F32), jax.ShapeDtypeStruct((T, d), F32),
                   jax.ShapeDtypeStruct((T, d), BF16)),
        compiler_params=_params(("arbitrary",)),
    )(y, target)


def _as3d(a):
    return a.reshape(a.shape[0], -1, a.shape[-1])


def _sum_stack(parts, *, name, out_dtype, rows=256):
    n, R, C = parts.shape
    tr = _tile(R, rows, 16)

    def body(p_ref, o_ref):
        acc = p_ref[0].astype(F32)
        for s in range(1, n):
            acc = acc + p_ref[s].astype(F32)
        o_ref[...] = acc.astype(out_dtype)

    return pl.pallas_call(
        body, name=name, grid=(R // tr,),
        in_specs=[pl.BlockSpec((n, tr, C), lambda i: (0, i, 0))],
        out_specs=pl.BlockSpec((tr, C), lambda i: (i, 0)),
        out_shape=jax.ShapeDtypeStruct((R, C), out_dtype),
        compiler_params=_params(("parallel",)),
    )(parts)


def _adamw(w, g, m, v, *, name, rows=128):
    R, C = w.shape
    tr = _tile(R, rows, 8)
    c1 = 1.0 - ADAM_B1 ** ADAM_STEP
    c2 = 1.0 - ADAM_B2 ** ADAM_STEP

    def body(w_ref, g_ref, m_ref, v_ref, d_ref, nm_ref, nv_ref):
        gv = g_ref[...]
        nm = ADAM_B1 * m_ref[...] + (1.0 - ADAM_B1) * gv
        nv = ADAM_B2 * v_ref[...] + (1.0 - ADAM_B2) * (gv * gv)
        nm_ref[...] = nm
        nv_ref[...] = nv
        d_ref[...] = -ADAM_LR * ((nm / c1) / (jnp.sqrt(nv / c2) + ADAM_EPS) + ADAM_WD * w_ref[...])

    blk = pl.BlockSpec((tr, C), lambda i: (i, 0))
    return pl.pallas_call(
        body, name=name, grid=(R // tr,), in_specs=[blk] * 4, out_specs=(blk,) * 3,
        out_shape=(jax.ShapeDtypeStruct((R, C), F32),) * 3,
        compiler_params=_params(("parallel",)),
    )(w, g, m, v)


def _one_hot(index, n):
    return jnp.broadcast_to((jnp.arange(n) == index).astype(F32)[:, None, None], (n, 8, LANES))


def _is_set(flags_ref, s):
    return flags_ref[s, 0:1, 0:1] > 0.5


def _rows_for(h, width, itemsize, n_stacked, budget, mult):
    return _tile(h, max(mult, budget // (n_stacked * width * itemsize)), mult)


def _pair_sum(pieces, recv, core, *, name):
    _, H, C = recv.shape
    tr = _rows_for(H, C, 2, 1, 2 << 20, 16)
    nh = H // tr
    halves_lead = pieces.ndim == 4

    def body(lo_ref, hi_ref, r_ref, core_ref, o_ref):
        lo, hi = (lo_ref[0, 0], hi_ref[0, 0]) if halves_lead else (lo_ref[0], hi_ref[0])
        mine = jnp.where(_is_set(core_ref, 0), lo, hi)
        o_ref[0] = (mine.astype(F32) + r_ref[0].astype(F32)).astype(BF16)

    blk = pl.BlockSpec((1, tr, C), lambda j, i: (j, i, 0))
    if halves_lead:
        lo_spec = pl.BlockSpec((1, 1, tr, C), lambda j, i: (0, j, i, 0))
        hi_spec = pl.BlockSpec((1, 1, tr, C), lambda j, i: (1, j, i, 0))
    else:
        lo_spec, hi_spec = blk, pl.BlockSpec((1, tr, C), lambda j, i: (j, nh + i, 0))
    return pl.pallas_call(
        body, name=name, grid=(4, nh),
        in_specs=[lo_spec, hi_spec, blk, pl.BlockSpec((2, 8, LANES), lambda j, i: (0, 0, 0))],
        out_specs=blk, out_shape=jax.ShapeDtypeStruct((4, H, C), BF16),
        compiler_params=_params(("parallel", "parallel")),
    )(pieces, pieces, recv, core)


def _chip_sum(pair, landed, chip_flags, *, name):
    _, H, C = pair.shape
    tr = _rows_for(H, C, 2, 4, 8 << 20, 16)

    def body(p_ref, l_ref, chip_ref, o_ref):
        acc = None
        for s in range(4):
            part = jnp.where(_is_set(chip_ref, s), p_ref[s], l_ref[s]).astype(F32)
            acc = part if acc is None else acc + part
        o_ref[...] = acc

    blk = pl.BlockSpec((4, tr, C), lambda i: (0, i, 0))
    return pl.pallas_call(
        body, name=name, grid=(H // tr,),
        in_specs=[blk, blk, pl.BlockSpec((4, 8, LANES), lambda i: (0, 0, 0))],
        out_specs=pl.BlockSpec((tr, C), lambda i: (i, 0)), out_shape=jax.ShapeDtypeStruct((H, C), F32),
        compiler_params=_params(("parallel",)),
    )(pair, landed, chip_flags)


def _adamw_layer(w, m, v, total, recv, core, layer, prev, *, name, col_halves=False, after=()):
    _, R, C = w.shape
    H, wd = total.shape
    tr = _rows_for(H, wd, 4, 1, 2 << 20, 8)
    nh = H // tr
    c1 = 1.0 - ADAM_B1 ** ADAM_STEP
    c2 = 1.0 - ADAM_B2 ** ADAM_STEP
    n_prev = 0 if prev is None else 4
    after = tuple(after)

    def body(*refs):
        w_ref, m_ref, v_ref, t_ref, r_ref, core_ref = refs[:6]
        g_ref, d_ref, nm_ref, nv_ref = refs[6 + n_prev + len(after):]
        half_is_mine = jnp.where(pl.program_id(0) == 0, core_ref[0, 0:1, 0:1], core_ref[1, 0:1, 0:1]) > 0.5
        gv = jnp.where(half_is_mine, t_ref[...], r_ref[...])
        nm = ADAM_B1 * m_ref[0] + (1.0 - ADAM_B1) * gv
        nv = ADAM_B2 * v_ref[0] + (1.0 - ADAM_B2) * (gv * gv)
        g_ref[0] = gv
        nm_ref[0] = nm
        nv_ref[0] = nv
        d_ref[0] = -ADAM_LR * ((nm / c1) / (jnp.sqrt(nv / c2) + ADAM_EPS) + ADAM_WD * w_ref[0])

    if col_halves:
        lay = pl.BlockSpec((1, tr, wd), lambda hf, i: (layer, i, hf))
    else:
        lay = pl.BlockSpec((1, tr, wd), lambda hf, i: (layer, hf * nh + i, 0))
    one = pl.BlockSpec((tr, wd), lambda hf, i: (i, 0))
    operands = [w, m, v, total, recv, core] + ([] if prev is None else list(prev)) + list(after)
    return pl.pallas_call(
        body, name=name, grid=(2, nh),
        in_specs=[lay, lay, lay, one, one, pl.BlockSpec((2, 8, LANES), lambda hf, i: (0, 0, 0))]
        + [ANY] * (n_prev + len(after)),
        out_specs=(lay,) * 4, out_shape=(jax.ShapeDtypeStruct((2, R, C), F32),) * 4,
        input_output_aliases={6 + i: i for i in range(n_prev)},
        compiler_params=_params(("parallel", "parallel")),
    )(*operands)


ANY = pl.BlockSpec(memory_space=pl.ANY)


def _coords():
    return lax.axis_index("x"), lax.axis_index("y"), lax.axis_index("c")


HBM = pl.BlockSpec(memory_space=pltpu.HBM)
SEM = pl.BlockSpec(memory_space=pltpu.SEMAPHORE)
EFFECT = pltpu.SideEffectType.DATAFLOW_SIDE_EFFECTING


def _copies(plan, bufs, send_sems, recv_sems):
    return [pltpu.make_async_remote_copy(src_ref=s, dst_ref=d, send_sem=send_sems.at[i], recv_sem=recv_sems.at[i],
                                         device_id=to, device_id_type=MESH)
            for i, (s, d, to) in enumerate(plan(bufs))]


def _start_copies(bufs, groups, *, name):
    nb, ng = len(bufs), len(groups)

    def body(*refs):
        buf_refs = refs[:nb]
        sems = refs[nb:nb + 2 * ng]
        token = refs[-1]
        for g, (plan, _) in enumerate(groups):
            for cp in _copies(plan, buf_refs, sems[2 * g], sems[2 * g + 1]):
                cp.start()
        token[...] = jnp.zeros_like(token)

    sem_shapes = []
    for _, n in groups:
        sem_shapes += [pltpu.SemaphoreType.DMA((n,)), pltpu.SemaphoreType.DMA((n,))]
    out = pl.pallas_call(
        body, name=name, in_specs=[HBM] * nb,
        out_specs=tuple([SEM] * (2 * ng) + [HBM] * nb + [pl.BlockSpec(memory_space=pltpu.VMEM)]),
        out_shape=tuple(sem_shapes + [pltpu.HBM(b.shape, b.dtype) for b in bufs] + [jax.ShapeDtypeStruct((8, LANES), F32)]),
        input_output_aliases={i: 2 * ng + i for i in range(nb)},
        compiler_params=pltpu.CompilerParams(has_side_effects=EFFECT),
    )(*[pltpu.with_memory_space_constraint(b, pltpu.HBM) for b in bufs])
    sems = [(out[2 * g], out[2 * g + 1]) for g in range(ng)]
    return sems, list(out[2 * ng:2 * ng + nb]), out[-1]


def _wait_copies(bufs, sems, plan, after, *, name):
    nb = len(bufs)

    def body(*refs):
        buf_refs = refs[:nb]
        for cp in _copies(plan, buf_refs, refs[nb], refs[nb + 1]):
            cp.wait_send()
            cp.wait_recv()

    out = pl.pallas_call(
        body, name=name, in_specs=[HBM] * nb + [SEM, SEM, ANY], out_specs=tuple([HBM] * nb),
        out_shape=tuple(pltpu.HBM(b.shape, b.dtype) for b in bufs),
        input_output_aliases={i: i for i in range(nb)},
        compiler_params=pltpu.CompilerParams(has_side_effects=EFFECT),
    )(*bufs, sems[0], sems[1], after)
    return list(out)


def _half(ref, c):
    h = ref.shape[0] // 2
    return ref.at[pl.ds(c * h, h)]


def _ici_gather_plan(pairs):
    def plan(refs):
        x, y, c = _coords()
        me = 2 * x + y
        out = []
        for s, d in pairs:
            for cx, cy in [(1 - x, y), (x, 1 - y), (1 - x, 1 - y)]:
                out.append((_half(refs[s], c), _half(refs[d].at[me], c), (cx, cy, c)))
            out.append((refs[s], refs[d].at[me], (x, y, 1 - c)))
        return out
    return plan, 4 * len(pairs)


def _d2d_forward_plan(lands):
    def plan(refs):
        x, y, c = _coords()
        out = []
        for d in lands:
            for cx, cy in [(1 - x, y), (x, 1 - y), (1 - x, 1 - y)]:
                got = _half(refs[d].at[2 * cx + cy], c)
                out.append((got, got, (x, y, 1 - c)))
        return out
    return plan, 3 * len(lands)


def _swap_half_plan(pairs):
    def plan(refs):
        x, y, c = _coords()
        out = []
        for s, d in pairs:
            h = refs[d].shape[1]
            other = refs[s].at[1 - c] if len(refs[s].shape) == 4 else refs[s].at[:, pl.ds((1 - c) * h, h)]
            out.append((other, refs[d], (x, y, 1 - c)))
        return out
    return plan, len(pairs)


def _scatter_plan(pairs):
    def plan(refs):
        x, y, c = _coords()
        me = 2 * x + y
        out = []
        for s, d in pairs:
            for cx, cy in [(1 - x, y), (x, 1 - y), (1 - x, 1 - y)]:
                out.append((refs[s].at[2 * cx + cy], refs[d].at[me], (cx, cy, c)))
        return out
    return plan, 3 * len(pairs)


def _swap_total_plan(pairs):
    def plan(refs):
        x, y, c = _coords()
        return [(refs[s], refs[d], (x, y, 1 - c)) for s, d in pairs]
    return plan, len(pairs)


def _gather_all(block, *, name, after=()):
    after = tuple(after)

    def body(src, *rest):
        out, send_sems, recv_sems, local_sem = rest[len(after):]
        x, y, c = _coords()
        me = 4 * x + 2 * y + c
        flips = [(fx, fy, fc) for fx in (0, 1) for fy in (0, 1) for fc in (0, 1)][1:]
        mine = pltpu.make_async_copy(src, out.at[me], local_sem)
        mine.start()
        peers = [(x ^ fx, y ^ fy, c ^ fc) for fx, fy, fc in flips]
        cps = [pltpu.make_async_remote_copy(src_ref=src, dst_ref=out.at[me], send_sem=send_sems.at[k],
                                            recv_sem=recv_sems.at[k], device_id=peer, device_id_type=MESH)
               for k, peer in enumerate(peers)]
        for cp in cps:
            cp.start()
        for k, (px, py, pc) in enumerate(peers):
            slot = out.at[4 * px + 2 * py + pc]
            pltpu.make_async_remote_copy(src_ref=slot, dst_ref=slot, send_sem=send_sems.at[k], recv_sem=recv_sems.at[k],
                                         device_id=(px, py, pc), device_id_type=MESH).wait_recv()
        for cp in cps:
            cp.wait_send()
        mine.wait()

    return pl.pallas_call(
        body, name=name, in_specs=[ANY] * (1 + len(after)), out_specs=ANY,
        out_shape=jax.ShapeDtypeStruct((8,) + block.shape, block.dtype),
        scratch_shapes=[pltpu.SemaphoreType.DMA((7,)), pltpu.SemaphoreType.DMA((7,)), pltpu.SemaphoreType.DMA],
    )(block, *after)


def _cols(o):
    return jnp.transpose(o, (1, 0, 2)).reshape(o.shape[1], -1)


def _uncols(full):
    return jnp.transpose(full.reshape(full.shape[0], 4, -1), (1, 0, 2))


def _rope_pad(x1, x2):
    z = jnp.zeros_like(x1)
    return jnp.concatenate([x1, z, x2, z], axis=-1)


def _head_pad(w, heads):
    r = w.reshape(w.shape[0], heads, QK_HEAD)
    half = QK_ROPE // 2
    out = jnp.concatenate([r[..., :QK_NOPE], _rope_pad(r[..., QK_NOPE:QK_NOPE + half], r[..., QK_NOPE + half:])], axis=-1)
    return out.reshape(w.shape[0], heads * HEAD_PAD)


def _head_unpad(w, heads):
    r = w.reshape(w.shape[0], heads, HEAD_PAD)
    half = QK_ROPE // 2
    out = jnp.concatenate([r[..., :QK_NOPE], r[..., QK_NOPE:QK_NOPE + half],
                           r[..., QK_NOPE + 2 * half:QK_NOPE + 3 * half]], axis=-1)
    return out.reshape(w.shape[0], heads * QK_HEAD)


class _Dims:
    def __init__(self, d, seq):
        self.d = d
        self.seq = seq
        self.t_real = N_META + seq
        self.t = -(-self.t_real // LANES) * LANES
        self.dc = d // 2
        self.dp = d // 2
        self.pg = self.dp // len(POOL_WINDOWS)
        self.heads = d // 128
        self.dff = 4 * d
        self.a_end = 3 * self.dc
        self.q_end = self.a_end + Q_LORA
        self.kv_end = self.q_end + KV_LORA
        self.kr_end = self.kv_end + QK_ROPE
        self.pool_end = self.kr_end + self.dp
        self.d_in = self.pool_end + 3 * d
        self.r_pool = 3 * self.dc
        self.r_q = self.r_pool + self.dp
        self.r_kv = self.r_q + Q_LORA
        self.r_kr = self.r_kv + KV_LORA
        self.r_width = self.r_kr + HEAD_PAD


def _split_cols(a):
    return jnp.moveaxis(a.reshape(a.shape[:-1] + (2, a.shape[-1] // 2)), -2, -3)


def _join_cols(a):
    a = jnp.moveaxis(a, -3, -2)
    return a.reshape(a.shape[:-2] + (a.shape[-2] * a.shape[-1],))


def _in_weights(dm, pieces):
    w_t = _join_cols(pieces).reshape(dm.d_in, dm.d)
    half = QK_ROPE // 2
    kr = w_t[dm.kv_end:dm.kr_end]
    zeros = jnp.zeros((half, dm.d), BF16)
    kr_p = jnp.concatenate([kr[:half], zeros, kr[half:], zeros, jnp.zeros((HEAD_PAD - LANES, dm.d), BF16)], axis=0)
    return dict(
        wg_t=w_t[dm.pool_end:],
        wr_t=jnp.concatenate([w_t[:dm.a_end], w_t[dm.kr_end:dm.pool_end], w_t[dm.a_end:dm.kv_end], kr_p], axis=0))


def _other_weights(dm, g):
    out = {}
    if "w_ukv" in g:
        w_ukv = _cols(g["w_ukv"]).reshape(KV_LORA, dm.heads, QK_NOPE + V_HEAD)
        out["wkn"] = w_ukv[:, :, :QK_NOPE].reshape(KV_LORA, dm.heads * QK_NOPE)
        out["wv"] = w_ukv[:, :, QK_NOPE:].reshape(KV_LORA, dm.heads * V_HEAD)
    if "w_uq" in g:
        out["wuq"] = _head_pad(_cols(g["w_uq"]), dm.heads)
    if "pool_w" in g:
        out["wp"] = jnp.transpose(g["pool_w"], (1, 0, 2, 3)).reshape(len(POOL_WINDOWS), dm.pg, dm.pg)
    for name, key in (("w_branch_a", "wba"), ("w_branch_c", "wbc"), ("w_up", "wup")):
        if name in g:
            out[key] = _cols(g[name])
    for name, key in (("w_branch_b", "wbb"), ("w_o", "wo"), ("w_down", "wdn")):
        if name in g:
            out[key] = g[name].reshape(-1, dm.d)
    return out


def _small_weights(small):
    return dict(
        conv_w=small["conv_w"],
        attn_norm=small["attn_norm"][None], mlp_norm=small["mlp_norm"][None],
        q_lat_norm=small["q_lat_norm"][None], kv_lat_norm=small["kv_lat_norm"][None],
        q_norm=_head_pad(small["q_norm"][None], 1), k_norm=_head_pad(small["k_norm"][None], 1),
        pool_scale=small["pool_scale"][None],
    )


def _grad_piece(dm, dw, name):
    half = QK_ROPE // 2
    rows = lambda a: a.reshape((4, a.shape[0] // 4) + a.shape[1:])
    if name == "w_in":
        dwr, dwg = dw["wr_t"], dw["wg_t"]
        d_t = jnp.concatenate([
            dwr[:, :dm.r_pool], dwr[:, dm.r_q:dm.r_kr], dwr[:, dm.r_kr:dm.r_kr + half],
            dwr[:, dm.r_kr + 2 * half:dm.r_kr + 3 * half], dwr[:, dm.r_pool:dm.r_q], dwg], axis=1)
        out = d_t.reshape(2, 4, d_t.shape[1] // 4, d_t.shape[2])
    elif name == "w_ukv":
        out = _uncols(jnp.concatenate([dw["wkn"].reshape(KV_LORA, dm.heads, QK_NOPE),
                                       dw["wv"].reshape(KV_LORA, dm.heads, V_HEAD)], axis=-1).reshape(KV_LORA, -1))
    elif name == "w_uq":
        out = _uncols(_head_unpad(dw["wuq"], dm.heads))
    elif name == "pool_w":
        out = jnp.transpose(dw["wp"].reshape(len(POOL_WINDOWS), 4, dm.pg // 4, dm.pg), (1, 0, 2, 3))
    elif name in ("w_branch_a", "w_branch_c", "w_up"):
        out = dw[{"w_branch_a": "wba", "w_branch_c": "wbc", "w_up": "wup"}[name]]
    else:
        out = rows(dw[{"w_branch_b": "wbb", "w_o": "wo", "w_down": "wdn"}[name]])
    return out.astype(BF16)


def _layer_fwd(dm, W, x, cos_t, sin_t, tag, more=None, h=None):
    n = lambda s: f"{s}_{tag}"
    if h is None:
        h = _rms_fwd(x, W["attn_norm"], name=n("attn_norm"))
    gl = _mm(h, W["wg_t"], name=n("proj_gates"), tb=True, out_dtype=BF16)
    rest = _mm(h, W["wr_t"], name=n("proj_rest"), tb=True)
    if more is not None:
        W.update(more("after_proj", rest))
    y_a = _conv_fwd(rest, W["conv_w"], name=n("conv"), dc=dm.dc)
    y_c = _pool_fwd(rest, W["wp"], W["pool_scale"], name=n("pool"), seg0=dm.r_pool // dm.pg, pg=dm.pg)
    q_lat = _rms_fwd(rest, W["q_lat_norm"], name=n("q_lat_norm"), width=Q_LORA, seg=dm.r_q // Q_LORA)
    kv_lat = _rms_fwd(rest, W["kv_lat_norm"], name=n("kv_lat_norm"), width=KV_LORA, seg=dm.r_kv // KV_LORA)
    q_raw = _mm(q_lat, W["wuq"], name=n("up_q"), out_dtype=BF16)
    k_nope = _mm(kv_lat, W["wkn"], name=n("up_k"), out_dtype=BF16)
    v = _mm(kv_lat, W["wv"], name=n("up_v"), out_dtype=BF16)
    q, k = _qk_fwd(q_raw, k_nope, rest, cos_t, sin_t, W["q_norm"], W["k_norm"], name=n("qk_norm_rope"),
                   heads=dm.heads, kr_seg=dm.r_kr // HEAD_PAD)
    y_b, lse = _flash_fwd(q, k, v, name=n("attention"), heads=dm.heads)
    pa = _mm(y_a, W["wba"], name=n("branch_a"), out_dtype=BF16)
    pb = _mm(y_b, W["wbb"], name=n("branch_b"), out_dtype=BF16)
    pc = _mm(y_c, W["wbc"], name=n("branch_c"), out_dtype=BF16)
    merged = _merge_fwd(gl, pa, pb, pc, name=n("merge"), d=dm.d)
    x1 = _mm(merged, W["wo"], name=n("out_proj"), add=x)
    h2 = _rms_fwd(x1, W["mlp_norm"], name=n("mlp_norm"))
    if more is not None:
        W.update(more("before_mlp", h2))
    up, act = _mm(h2, W["wup"], name=n("mlp_up"), epi="relu2")
    x2 = _mm(act, W["wdn"], name=n("mlp_down"), add=x1, tm=704, tk=4096)
    saved = dict(x=x, h=h, gl=gl, rest=rest, y_a=y_a, y_c=y_c, q_lat=q_lat, kv_lat=kv_lat, q_raw=q_raw, k_nope=k_nope,
                 v=v, q=q, k=k, y_b=y_b, lse=lse, pa=pa, pb=pb, pc=pc, merged=merged, x1=x1, h2=h2, up=up, act=act)
    return x2, saved


def _layer_bwd(dm, W, S, dx2, dx2_b, cos_t, sin_t, tag, hook=None):
    n = lambda s: f"{s}_{tag}"
    dw, ds = {}, {}
    if hook is None:
        hook = lambda point, t, dw_so_far: ()
    dup = _mm(dx2_b, W["wdn"], name=n("d_mlp_down"), tb=True, aux=S["up"], epi="drelu2", out_dtype=BF16,
              after=hook("start", dx2, dw))
    dw["wdn"] = _mm(S["act"], dx2_b, name=n("dw_mlp_down"), ta=True, tm=512, out_dtype=BF16)
    dh2 = _mm(dup, W["wup"], name=n("d_mlp_up"), tb=True, tm=704, tk=4096)
    dw["wup"] = _mm(S["h2"], dup, name=n("dw_mlp_up"), ta=True, tm=512, out_dtype=BF16, pieces=4)
    dx1, dx1_b, ds["mlp_norm"] = _rms_bwd(dh2, S["x1"], W["mlp_norm"], name=n("d_mlp_norm"), res=dx2, bf16_copy=True)
    dmerged = _mm(dx1_b, W["wo"], name=n("d_out_proj"), tb=True, after=hook("after_mlp", dx1, dw))
    dw["wo"] = _mm(S["merged"], dx1_b, name=n("dw_out_proj"), ta=True, tm=512, out_dtype=BF16)
    dpa, dpb, dpc, dg0, dg1, dg2 = _merge_bwd(dmerged, S["gl"], S["pa"], S["pb"], S["pc"], name=n("d_merge"), d=dm.d)
    dgl = jnp.concatenate([dg0, dg1, dg2], axis=1)
    dy_a = _mm(dpa, W["wba"], name=n("d_branch_a"), tb=True)
    dw["wba"] = _mm(S["y_a"], dpa, name=n("dw_branch_a"), ta=True, tm=512, out_dtype=BF16, pieces=4)
    dy_b = _mm(dpb, W["wbb"], name=n("d_branch_b"), tb=True, out_dtype=BF16)
    dw["wbb"] = _mm(S["y_b"], dpb, name=n("dw_branch_b"), ta=True, tm=512, out_dtype=BF16)
    dy_c = _mm(dpc, W["wbc"], name=n("d_branch_c"), tb=True)
    dw["wbc"] = _mm(S["y_c"], dpc, name=n("dw_branch_c"), ta=True, tm=512, out_dtype=BF16, pieces=4)
    dq, dk, dv = _flash_bwd(S["q"], S["k"], S["v"], S["y_b"], dy_b, S["lse"], name=n("d_attention"), heads=dm.heads,
                            after=hook("before_attention", dw["wbc"], dw))
    after_attention = hook("after_attention", dq, dw)
    dq_raw, dk_nope, dk_rope, dgq, dgk = _qk_bwd(
        dq, dk, S["q_raw"], S["k_nope"], S["rest"], cos_t, sin_t, W["q_norm"], W["k_norm"], name=n("d_qk_norm_rope"),
        heads=dm.heads, kr_seg=dm.r_kr // HEAD_PAD)
    ds["q_norm"] = _head_unpad(dgq, 1)
    ds["k_norm"] = _head_unpad(dgk, 1)
    dkv_v = _mm(dv, W["wv"], name=n("d_up_v"), tb=True, after=after_attention)
    dq_lat_n = _mm(dq_raw, W["wuq"], name=n("d_up_q"), tb=True, after=hook("after_qk", dq_raw, dw))
    dw["wuq"] = _mm(S["q_lat"], dq_raw, name=n("dw_up_q"), ta=True, tm=512)
    dkv_lat_n = _mm(dk_nope, W["wkn"], name=n("d_up_k"), tb=True, add=dkv_v)
    dw["wkn"] = _mm(S["kv_lat"], dk_nope, name=n("dw_up_k"), ta=True, tm=512)
    dw["wv"] = _mm(S["kv_lat"], dv, name=n("dw_up_v"), ta=True, tm=512)
    dq_lat, ds["q_lat_norm"] = _rms_bwd(dq_lat_n, S["rest"], W["q_lat_norm"], name=n("d_q_lat_norm"), width=Q_LORA,
                                        seg=dm.r_q // Q_LORA, out_dtype=BF16)
    dkv_lat, ds["kv_lat_norm"] = _rms_bwd(dkv_lat_n, S["rest"], W["kv_lat_norm"], name=n("d_kv_lat_norm"), width=KV_LORA,
                                          seg=dm.r_kv // KV_LORA, out_dtype=BF16)
    du, db, dc, ds["conv_w"] = _conv_bwd(S["rest"], W["conv_w"], dy_a, name=n("d_conv"), dc=dm.dc)
    dpool, dw["wp"], ds["pool_scale"] = _pool_bwd(S["rest"], W["wp"], W["pool_scale"], dy_c, name=n("d_pool"),
                                                  seg0=dm.r_pool // dm.pg, pg=dm.pg)
    drest = jnp.concatenate([du, db, dc, dpool, dq_lat, dkv_lat, dk_rope], axis=1)
    dw["wg_t"] = _mm(dgl, S["h"], name=n("dw_proj_gates"), ta=True, tm=512, out_dtype=BF16, pieces=2)
    dw["wr_t"] = _mm(drest, S["h"], name=n("dw_proj_rest"), ta=True, tm=512, out_dtype=BF16, pieces=2)
    dh_g = _mm(dgl, W["wg_t"], name=n("d_proj_gates"), tm=704, tk=3072, after=hook("after_dw_in", dw["wr_t"], dw))
    dh = _mm(drest, W["wr_t"], name=n("d_proj_rest"), add=dh_g, tm=704, tk=2688, after=hook("after_dh_gates", dh_g, dw))
    dx, dx_b, ds["attn_norm"] = _rms_bwd(dh, S["x"], W["attn_norm"], name=n("d_attn_norm"), res=dx1, bf16_copy=True)
    return dx, dx_b, dw, ds


BIG = ("w_in", "w_uq", "w_ukv", "pool_w", "w_branch_a", "w_branch_b", "w_branch_c", "w_o", "w_up", "w_down")
REPLICATED = ("attn_norm", "q_lat_norm", "kv_lat_norm", "q_norm", "k_norm", "pool_scale", "mlp_norm")
WEIGHTS = ("meta_tokens", "attn_norm", "w_in", "conv_w", "q_lat_norm", "kv_lat_norm", "w_uq", "w_ukv", "q_norm",
           "k_norm", "pool_w", "pool_scale", "w_branch_a", "w_branch_b", "w_branch_c", "w_o", "mlp_norm", "w_up",
           "w_down")


def _pack(arrays):
    flat = jnp.concatenate([a.reshape(-1).astype(F32) for a in arrays])
    pad = (-flat.shape[0]) % (8 * LANES)
    return jnp.pad(flat, (0, pad)).reshape(-1, LANES)


def _unpack(flat, shapes):
    out, pos = [], 0
    flat = flat.reshape(-1)
    for shp in shapes:
        size = math.prod(shp)
        out.append(flat[pos:pos + size].reshape(shp))
        pos += size
    return out


def _update(w, g, m, v, name):
    shp = w.shape
    to2 = lambda a: a.reshape(-1, shp[-1])
    delta, nm, nv = _adamw(to2(w), to2(g), to2(m), to2(v), name=name)
    return delta.reshape(shp), nm.reshape(shp), nv.reshape(shp)


def _step(args):
    x = args["x"][0]
    seq, d = x.shape
    dm = _Dims(d, seq)
    xi, yi, ci = _coords()
    chip = 2 * xi + yi

    small_w = _gather_all(_pack([args["conv_w"], args["meta_tokens"]]), name="gather_small_weights")
    args = dict(args)
    for p in ("", "m_", "v_"):
        args[p + "w_in"] = jnp.swapaxes(args[p + "w_in"], 1, 2)
    order = [(k, l) for l in range(2) for k in BIG]
    shards = {n: args[n[0]][n[1]].astype(BF16) for n in order}
    for l in range(2):
        shards[("w_in", l)] = _split_cols(shards[("w_in", l)])
    small_w, shards[order[0]] = lax.optimization_barrier((small_w, shards[order[0]]))
    lands = {n: lax.empty((4,) + shards[n].shape, BF16) for n in order}
    last = ("w_up", "w_down")
    group_names = [[("w_in", 0)], [(k, 0) for k in BIG[1:] if k not in last], [(k, 0) for k in last],
                   [(k, 1) for k in BIG]]
    first, others = order[0], order[1:]
    sems, thru, token = _start_copies([shards[first], lands[first]], [_ici_gather_plan([(0, 1)])],
                                      name="start_gather_ici_first")
    shards[first], lands[first] = thru
    at = {n: i for i, n in enumerate(others)}
    sems_b, thru, token_b = _start_copies(
        [shards[n] for n in others] + [lands[n] for n in others] + [token],
        [_ici_gather_plan([(at[n], len(others) + at[n]) for n in g]) for g in group_names[1:]], name="start_gather_ici")
    sems = sems + sems_b
    for i, n in enumerate(others):
        shards[n], lands[n] = thru[i], thru[len(others) + i]

    def finish_gather(g, after, tag):
        names = group_names[g]
        k = len(names)
        plan, _ = _ici_gather_plan([(i, k + i) for i in range(k)])
        got = _wait_copies([shards[n] for n in names] + [lands[n] for n in names], sems[g], plan, after,
                           name=f"wait_gather_ici_{tag}")
        for i, n in enumerate(names):
            shards[n] = got[i]
        fwd = _d2d_forward_plan(list(range(k)))
        sems2, bufs2, tok2 = _start_copies(got[k:], [fwd], name=f"start_gather_d2d_{tag}")
        return names, bufs2, sems2[0], fwd[0], tok2

    def land_gather(pending, after, tag):
        names, bufs2, sems2, plan, tok2 = pending
        done = _wait_copies(bufs2, sems2, plan, tok2 if after is None else after, name=f"wait_gather_d2d_{tag}")
        return {n[0]: buf for n, buf in zip(names, done)}

    conv_shape, meta_shape = args["conv_w"].shape, args["meta_tokens"].shape
    per_chip = [_unpack(small_w[2 * j], [conv_shape, meta_shape]) for j in range(4)]
    conv_full = jnp.concatenate([p[0] for p in per_chip], axis=-1)
    meta_full = jnp.concatenate([p[1] for p in per_chip], axis=-1)

    layers = []
    for l in range(2):
        small = {k: args[k][l] for k in REPLICATED}
        small["conv_w"] = conv_full[l]
        layers.append(_small_weights(small))

    pos = jnp.arange(dm.t, dtype=F32)
    inv = ROPE_THETA ** (-jnp.arange(0, QK_ROPE, 2, dtype=F32) / QK_ROPE)
    ang = pos[:, None] * inv[None, :]
    cos_t = _rope_pad(jnp.cos(ang), jnp.cos(ang))
    sin_t = _rope_pad(-jnp.sin(ang), jnp.sin(ang))
    tail = jnp.zeros((dm.t - dm.t_real, d), F32)
    h0 = jnp.concatenate([meta_full, x, tail], axis=0)
    target = jnp.concatenate([jnp.zeros((N_META, d), F32), args["loss_target"][0], tail], axis=0)

    h_first = _rms_fwd(h0, layers[0]["attn_norm"], name="attn_norm_l0", after=(token, token_b))
    layers[0].update(_in_weights(dm, land_gather(finish_gather(0, h_first, "l0_in"), None, "l0_in")["w_in"]))
    def rest_of_layer0(point, after):
        g, tag = (1, "l0_mid") if point == "after_proj" else (2, "l0_mlp")
        return _other_weights(dm, land_gather(finish_gather(g, after, tag), None, tag))

    h1, saved0 = _layer_fwd(dm, layers[0], h0, cos_t, sin_t, "l0", more=rest_of_layer0, h=h_first)
    g1 = land_gather(finish_gather(3, saved0["y_b"], "l1"), h1, "l1")
    layers[1].update(_in_weights(dm, g1["w_in"]))
    layers[1].update(_other_weights(dm, g1))
    h2, saved1 = _layer_fwd(dm, layers[1], h1, cos_t, sin_t, "l1")
    sq, dy, dy_b = _loss(h2, target, name="loss_head", first=N_META, last=dm.t_real)
    loss = lax.psum(0.5 / d * sq[0, 0], ("x", "y", "c"))
    core, chip_flags = _one_hot(ci, 2), _one_hot(chip, 4)

    class Reduce:
        def __init__(self, names, dw, tag):
            self.names, self.tag, self.nb = names, tag, len(names)
            self.idx = [(i, self.nb + i) for i in range(self.nb)]
            parts = [_grad_piece(dm, dw, k) for k in names]
            parts = [p if k == "w_in" else _as3d(p) for p, k in zip(parts, names)]
            recv = [lax.empty((4,) + p.shape[2:] if k == "w_in" else (4, p.shape[1] // 2, p.shape[2]), BF16)
                    for p, k in zip(parts, names)]
            self.plan = _swap_half_plan(self.idx)
            self.sems, self.bufs, self.token = _start_copies(parts + recv, [self.plan], name=f"start_swap_{tag}")

        def _land(self, after, what):
            return _wait_copies(self.bufs, self.sems[0], self.plan[0], self.token if after is None else after,
                                name=f"wait_{what}_{self.tag}")

        def scatter(self, after=None):
            got = self._land(after, "swap")
            pairs = [_pair_sum(got[i], got[j], core, name=f"pair_sum_{k}_{self.tag}")
                     for (i, j), k in zip(self.idx, self.names)]
            self.plan = _scatter_plan(self.idx)
            self.sems, self.bufs, self.token = _start_copies(pairs + [lax.empty(p.shape, BF16) for p in pairs],
                                                             [self.plan], name=f"start_scatter_{self.tag}")
            return self.token

        def totals(self, after=None):
            got = self._land(after, "scatter")
            sums = [_chip_sum(got[i], got[j], chip_flags, name=f"chip_sum_{k}_{self.tag}")
                    for (i, j), k in zip(self.idx, self.names)]
            self.plan = _swap_total_plan(self.idx)
            self.sems, self.bufs, self.token = _start_copies(sums + [lax.empty(t.shape, F32) for t in sums],
                                                             [self.plan], name=f"start_swap_total_{self.tag}")
            return self.token

        def finish(self, after=None):
            got = self._land(after, "swap_total")
            return {k: (got[i], got[j]) for (i, j), k in zip(self.idx, self.names)}

    dh1, dh1_b, dw1, ds1 = _layer_bwd(dm, layers[1], saved1, dy, dy_b, cos_t, sin_t, "l1",
                               hook=lambda point, t, dw: (loss.reshape(1, 1),) if point == "start" else ())
    early = ("w_down", "w_up", "w_o", "w_branch_a", "w_branch_b", "w_branch_c")
    late = tuple(k for k in BIG if k not in early)
    stage = {}

    def during_layer0(point, t, dw):
        if point == "start":
            stage["l1"] = Reduce(BIG, dw1, "l1")
            return (stage["l1"].token,)
        if point == "after_mlp":
            return (stage["l1"].scatter(after=t),)
        if point == "before_attention":
            stage["l0a"] = Reduce(early, dw, "l0a")
            return (stage["l0a"].token,)
        if point == "after_attention":
            return (stage["l1"].totals(after=t), stage["l0a"].scatter(after=t))
        if point == "after_qk":
            stage["red1"] = stage["l1"].finish(after=t)
            return ()
        if point == "after_dw_in":
            tok = stage["l0a"].totals(after=t)
            stage["l0b"] = Reduce(late, dw, "l0b")
            return (tok, stage["l0b"].token)
        return (stage["l0b"].scatter(after=t),)

    dh0, _, dw0, ds0 = _layer_bwd(dm, layers[0], saved0, dh1, dh1_b, cos_t, sin_t, "l0", hook=during_layer0)
    grad_x = dh0[N_META:dm.t_real][None]
    red1 = stage["red1"]
    grads, delta, new_m, new_v = {}, {}, {}, {}

    def adamw_big(k, layer, red, prev, after):
        shp = args[k].shape
        wmv = [args[p + k].reshape(2, -1, shp[-1]) for p in ("", "m_", "v_")]
        return _adamw_layer(*wmv, *red[k], core, layer, prev, name=f"adamw_{k}_l{layer}", col_halves=k == "w_in",
                            after=after)

    def keep(k, out):
        shp = args[k].shape
        out = [o.reshape(shp) for o in out]
        grads[k], delta[k], new_m[k], new_v[k] = [jnp.swapaxes(o, 1, 2) for o in out] if k == "w_in" else out

    half_done = {}
    pin = dh0
    for k in BIG:
        half_done[k] = adamw_big(k, 1, red1, None, (pin,))
        pin = half_done[k][0]
    red0a = stage["l0a"].finish(after=pin)
    for k in early:
        out = adamw_big(k, 0, red0a, half_done[k], ())
        keep(k, out)
        pin = out[0]

    small_names = REPLICATED + ("conv_w",)
    small_parts = [jnp.stack([ds0[k].reshape(ds0[k].shape[-2:] if k == "conv_w" else (-1,)),
                              ds1[k].reshape(ds1[k].shape[-2:] if k == "conv_w" else (-1,))]) for k in small_names]
    small_parts.append(dh0[:N_META])
    small_all = _gather_all(_pack(small_parts), name="gather_small_grads", after=(pin,))
    small_sum = _sum_stack(small_all, name="sum_small_grads", out_dtype=F32)
    small_g = dict(zip(small_names + ("meta_tokens",), _unpack(small_sum, [p.shape for p in small_parts])))
    for k in REPLICATED:
        grads[k] = small_g[k]
    dcw = conv_shape[-1]
    grads["conv_w"] = lax.dynamic_slice_in_dim(small_g["conv_w"], chip * dcw, dcw, axis=2)
    dmeta = meta_shape[-1]
    grads["meta_tokens"] = lax.dynamic_slice_in_dim(small_g["meta_tokens"], chip * dmeta, dmeta, axis=1)

    stage["l0b"].totals(after=small_sum)
    red0b = stage["l0b"].finish()
    for k in late:
        keep(k, adamw_big(k, 0, red0b, half_done[k], ()))
    for k in WEIGHTS:
        if k not in BIG:
            grads[k] = grads[k].reshape(args[k].shape)
            delta[k], new_m[k], new_v[k] = _update(args[k], grads[k], args["m_" + k], args["v_" + k], f"adamw_{k}")
    return (loss, grad_x, *[grads[k] for k in WEIGHTS], *[delta[k] for k in WEIGHTS],
            *[new_m[k] for k in WEIGHTS], *[new_v[k] for k in WEIGHTS])


def kernel(x, meta_tokens, attn_norm, w_in, conv_w, q_lat_norm, kv_lat_norm, w_uq, w_ukv, q_norm, k_norm, pool_w, pool_scale, w_branch_a, w_branch_b, w_branch_c, w_o, mlp_norm, w_up, w_down, loss_target, m_meta_tokens, m_attn_norm, m_w_in, m_conv_w, m_q_lat_norm, m_kv_lat_norm, m_w_uq, m_w_ukv, m_q_norm, m_k_norm, m_pool_w, m_pool_scale, m_w_branch_a, m_w_branch_b, m_w_branch_c, m_w_o, m_mlp_norm, m_w_up, m_w_down, v_meta_tokens, v_attn_norm, v_w_in, v_conv_w, v_q_lat_norm, v_kv_lat_norm, v_w_uq, v_w_ukv, v_q_norm, v_k_norm, v_pool_w, v_pool_scale, v_w_branch_a, v_w_branch_b, v_w_branch_c, v_w_o, v_mlp_norm, v_w_up, v_w_down):
    return _step(dict(locals()))
```

```python
import functools
import math

import jax
import jax.numpy as jnp
from jax import lax
from jax.experimental import pallas as pl
from jax.experimental.pallas import tpu as pltpu

F32 = jnp.float32
BF16 = jnp.bfloat16
MESH = pl.DeviceIdType.MESH

EPS = 1e-6
N_META = 16
QK_NOPE = 128
QK_ROPE = 64
QK_HEAD = QK_NOPE + QK_ROPE
V_HEAD = 128
HEAD_PAD = 256
Q_LORA = 512
KV_LORA = 512
ROPE_THETA = 10000.0
POOL_WINDOWS = (2, 4, 8, 16)
HALO = 16
LANES = 128
ADAM_LR = 0.001
ADAM_B1 = 0.9
ADAM_B2 = 0.999
ADAM_EPS = 1e-08
ADAM_WD = 0.01
ADAM_STEP = 10
VMEM_LIMIT = 52 * 1024 * 1024
NEG = -1e30
ATTN_SCALE = QK_HEAD ** -0.5
LOG2_E = 1.4426950408889634
Q_FOLD = ATTN_SCALE * LOG2_E


def _tile(n, target, mult=LANES):
    best = None
    for t in range(mult, min(n, target) + 1, mult):
        if n % t == 0:
            best = t
    return n if best is None else best


def _params(sem=None):
    return pltpu.CompilerParams(dimension_semantics=sem, vmem_limit_bytes=VMEM_LIMIT)


def _mm(a, b, *, name, ta=False, tb=False, add=None, aux=None, epi=None, out_dtype=F32,
        tm=1056, tn=1024, tk=None, after=(), pieces=None):
    if ta:
        K, M = a.shape
    else:
        M, K = a.shape
    if tb:
        N, kb = b.shape
    else:
        kb, N = b.shape
    assert K == kb, (a.shape, b.shape, ta, tb)
    tm = _tile(M, tm, LANES if ta else 16)
    tn = _tile(N if pieces is None else N // pieces, tn, LANES)
    tk = K if tk is None else _tile(K, tk, LANES if (not ta or tb) else 16)
    nk = K // tk
    a_bytes, b_bytes = a.size * a.dtype.itemsize, b.size * b.dtype.itemsize
    j_outer = nk == 1 and a_bytes * (N // tn) + b_bytes < a_bytes + b_bytes * (M // tm)
    grid = (N // tn, M // tm, nk) if j_outer else (M // tm, N // tn, nk)
    row = (lambda g0, g1: g1) if j_outer else (lambda g0, g1: g0)
    col = (lambda g0, g1: g0) if j_outer else (lambda g0, g1: g1)

    if ta:
        a_spec = pl.BlockSpec((tk, tm), lambda g0, g1, k: (k, row(g0, g1)))
    else:
        a_spec = pl.BlockSpec((tm, tk), lambda g0, g1, k: (row(g0, g1), k))
    if tb:
        b_spec = pl.BlockSpec((tn, tk), lambda g0, g1, k: (col(g0, g1), k))
    else:
        b_spec = pl.BlockSpec((tk, tn), lambda g0, g1, k: (k, col(g0, g1)))
    o_spec = pl.BlockSpec((tm, tn), lambda g0, g1, k: (row(g0, g1), col(g0, g1)))
    per = None if pieces is None else N // pieces // tn
    in_specs = [a_spec, b_spec]
    operands = [a, b]
    if add is not None:
        in_specs.append(o_spec)
        operands.append(add)
    if aux is not None:
        in_specs.append(o_spec)
        operands.append(aux)
    after = tuple(after)
    in_specs += [pl.BlockSpec(memory_space=pl.ANY)] * len(after)
    operands += list(after)
    if epi == "relu2":
        out_shape = (jax.ShapeDtypeStruct((M, N), BF16), jax.ShapeDtypeStruct((M, N), BF16))
        out_specs = (o_spec, o_spec)
    elif pieces is not None:
        out_shape = jax.ShapeDtypeStruct((pieces, M, N // pieces), out_dtype)
        out_specs = pl.BlockSpec((1, tm, tn), lambda g0, g1, k: (col(g0, g1) // per, row(g0, g1), col(g0, g1) % per))
    else:
        out_shape = jax.ShapeDtypeStruct((M, N), out_dtype)
        out_specs = o_spec
    dims =(((0 if ta else 1,), (1 if tb else 0,)), ((), ()))
    has_add, has_aux = add is not None, aux is not None

    def body(*refs):
        a_ref, b_ref = refs[0], refs[1]
        pos = 2
        add_ref = aux_ref = None
        if has_add:
            add_ref = refs[pos]
            pos += 1
        if has_aux:
            aux_ref = refs[pos]
            pos += 1
        pos += len(after)
        n_out = 2 if epi == "relu2" else 1
        out_refs = refs[pos:pos + n_out]
        acc_ref = refs[pos + n_out] if nk > 1 else None

        part = lax.dot_general(a_ref[...].astype(BF16), b_ref[...].astype(BF16), dims,
                               preferred_element_type=F32)

        def finish(acc):
            if has_add:
                acc = acc + add_ref[...].astype(F32)
            if epi == "relu2":
                r = jnp.maximum(acc, 0.0)
                out_refs[0][...] = acc.astype(BF16)
                out_refs[1][...] = (r * r).astype(BF16)
            elif epi == "drelu2":
                u = aux_ref[...].astype(F32)
                out_refs[0][...] = (acc * (2.0 * jnp.maximum(u, 0.0))).astype(out_dtype)
            else:
                out_refs[0][...] = acc.astype(out_dtype).reshape(out_refs[0].shape)

        if nk == 1:
            finish(part)
        else:
            k = pl.program_id(2)

            @pl.when(k == 0)
            def _():
                acc_ref[...] = part

            @pl.when(k > 0)
            def _():
                acc_ref[...] += part

            @pl.when(k == nk - 1)
            def _():
                finish(acc_ref[...])

    scratch = [pltpu.VMEM((tm, tn), F32)] if nk > 1 else []
    return pl.pallas_call(
        body, name=name, grid=grid, in_specs=in_specs, out_specs=out_specs, out_shape=out_shape,
        scratch_shapes=scratch, compiler_params=_params(("parallel", "parallel", "arbitrary")),
    )(*operands)


def _rms_fwd(x, g, *, name, width=None, seg=0, tm=384, after=()):
    T = x.shape[0]
    width = x.shape[1] if width is None else width
    tm = _tile(T, tm, 16)
    after = tuple(after)

    def body(x_ref, g_ref, *rest):
        xf = x_ref[...].astype(F32)
        r = lax.rsqrt(jnp.mean(xf * xf, axis=-1, keepdims=True) + EPS)
        rest[-1][...] = (xf * r * g_ref[...]).astype(BF16)

    return pl.pallas_call(
        body, name=name, grid=(T // tm,),
        in_specs=[pl.BlockSpec((tm, width), lambda i: (i, seg)), pl.BlockSpec((1, width), lambda i: (0, 0))]
        + [pl.BlockSpec(memory_space=pl.ANY)] * len(after),
        out_specs=pl.BlockSpec((tm, width), lambda i: (i, 0)),
        out_shape=jax.ShapeDtypeStruct((T, width), BF16),
        compiler_params=_params(("parallel",)),
    )(x, g, *after)


def _rms_bwd(dy, x, g, *, name, width=None, seg=0, res=None, out_dtype=F32, tm=384, bf16_copy=False):
    T = x.shape[0]
    width = x.shape[1] if width is None else width
    tm = _tile(T, tm, 16)
    has_res = res is not None

    def body(*refs):
        dy_ref, x_ref, g_ref = refs[:3]
        res_ref = refs[3] if has_res else None
        dx_ref, dg_ref = refs[4 if has_res else 3], refs[-1]
        xf = x_ref[...].astype(F32)
        dyf = dy_ref[...].astype(F32)
        r = lax.rsqrt(jnp.mean(xf * xf, axis=-1, keepdims=True) + EPS)
        xhat = xf * r
        dyh = dyf * g_ref[...]
        dx = r * (dyh - xhat * jnp.mean(dyh * xhat, axis=-1, keepdims=True))
        if has_res:
            dx = dx + res_ref[...].astype(F32)
        dx_ref[...] = dx.astype(out_dtype)
        if bf16_copy:
            refs[-2][...] = dx.astype(BF16)
        part = jnp.sum(dyf * xhat, axis=0, keepdims=True)

        @pl.when(pl.program_id(0) == 0)
        def _():
            dg_ref[...] = part

        @pl.when(pl.program_id(0) > 0)
        def _():
            dg_ref[...] += part

    row = pl.BlockSpec((tm, width), lambda i: (i, 0))
    in_specs = [row, pl.BlockSpec((tm, width), lambda i: (i, seg)), pl.BlockSpec((1, width), lambda i: (0, 0))]
    operands = [dy, x, g]
    if has_res:
        in_specs.append(row)
        operands.append(res)
    vec = pl.BlockSpec((1, width), lambda i: (0, 0))
    full = [jax.ShapeDtypeStruct((T, width), out_dtype)] + ([jax.ShapeDtypeStruct((T, width), BF16)] if bf16_copy else [])
    return pl.pallas_call(
        body, name=name, grid=(T // tm,), in_specs=in_specs,
        out_specs=tuple([row] * len(full) + [vec]),
        out_shape=tuple(full + [jax.ShapeDtypeStruct((1, width), F32)]),
        compiler_params=_params(("arbitrary",)),
    )(*operands)


def _down(ext, k):
    return pltpu.roll(ext, k, 0)


def _up(ext, k):
    return pltpu.roll(ext, ext.shape[0] - k, 0)


def _pre_halo(ref, r, R):
    start = pl.multiple_of(jnp.maximum(r * R - HALO, 0), 8)
    keep = (r > 0).astype(F32)
    return ref[pl.ds(start, HALO), :].astype(F32) * keep


def _post_halo(ref, r, R, n_chunks):
    start = pl.multiple_of(jnp.minimum(r * R + R, (n_chunks - 1) * R + R - HALO), 8)
    keep = (r < n_chunks - 1).astype(F32)
    return ref[pl.ds(start, HALO), :].astype(F32) * keep


def _chunk(ref, r, R):
    return ref[pl.ds(pl.multiple_of(r * R, 8), R), :].astype(F32)


def _conv_fwd(rest, conv_w, *, name, dc, tc=128, rows=1056):
    T = rest.shape[0]
    tc = _tile(dc, tc)
    nb = dc // tc
    R = _tile(T, rows, 16)
    n_chunks = T // R

    def body(u_ref, b_ref, c_ref, w_ref, y_ref):
        w0, w1, w2 = w_ref[0:1, :], w_ref[1:2, :], w_ref[2:3, :]

        def chunk(r, carry):
            cu = _chunk(c_ref, r, R) * _chunk(u_ref, r, R)
            ext = jnp.concatenate([_pre_halo(c_ref, r, R) * _pre_halo(u_ref, r, R), cu], axis=0)
            conv = w0 * _down(ext, 2)[HALO:] + w1 * _down(ext, 1)[HALO:] + w2 * cu
            y_ref[pl.ds(pl.multiple_of(r * R, 8), R), :] = (_chunk(b_ref, r, R) * conv).astype(BF16)
            return carry

        lax.fori_loop(0, n_chunks, chunk, 0)

    col = lambda off: pl.BlockSpec((T, tc), lambda j: (0, off * nb + j))
    return pl.pallas_call(
        body, name=name, grid=(nb,),
        in_specs=[col(0), col(1), col(2), pl.BlockSpec((3, tc), lambda j: (0, j))],
        out_specs=pl.BlockSpec((T, tc), lambda j: (0, j)),
        out_shape=jax.ShapeDtypeStruct((T, dc), BF16),
        compiler_params=_params(("parallel",)),
    )(rest, rest, rest, conv_w)


def _conv_bwd(rest, conv_w, dy, *, name, dc, tc=128, rows=1056):
    T = rest.shape[0]
    tc = _tile(dc, tc)
    nb = dc // tc
    R = _tile(T, rows, 16)
    n_chunks = T // R

    def body(u_ref, b_ref, c_ref, w_ref, dy_ref, du_ref, db_ref, dc_ref, dw_ref):
        w0, w1, w2 = w_ref[0:1, :], w_ref[1:2, :], w_ref[2:3, :]

        def chunk(r, carry):
            a0, a1, a2 = carry
            u, b, c = _chunk(u_ref, r, R), _chunk(b_ref, r, R), _chunk(c_ref, r, R)
            dy_c = _chunk(dy_ref, r, R)
            cu = c * u
            ext = jnp.concatenate([_pre_halo(c_ref, r, R) * _pre_halo(u_ref, r, R), cu], axis=0)
            cu1, cu2 = _down(ext, 1)[HALO:], _down(ext, 2)[HALO:]
            conv = w0 * cu2 + w1 * cu1 + w2 * cu
            dconv = dy_c * b
            dext = jnp.concatenate(
                [dconv, _post_halo(dy_ref, r, R, n_chunks) * _post_halo(b_ref, r, R, n_chunks)], axis=0)
            dcu = w2 * dconv + w1 * _up(dext, 1)[:R] + w0 * _up(dext, 2)[:R]
            rows_at = pl.ds(pl.multiple_of(r * R, 8), R)
            db_ref[rows_at, :] = (dy_c * conv).astype(BF16)
            du_ref[rows_at, :] = (dcu * c).astype(BF16)
            dc_ref[rows_at, :] = (dcu * u).astype(BF16)
            return (a0 + jnp.sum(dconv * cu2, axis=0, keepdims=True),
                    a1 + jnp.sum(dconv * cu1, axis=0, keepdims=True),
                    a2 + jnp.sum(dconv * cu, axis=0, keepdims=True))

        zero = jnp.zeros((1, tc), F32)
        a0, a1, a2 = lax.fori_loop(0, n_chunks, chunk, (zero, zero, zero))
        dw_ref[0:1, :] = a0
        dw_ref[1:2, :] = a1
        dw_ref[2:3, :] = a2

    col = lambda off: pl.BlockSpec((T, tc), lambda j: (0, off * nb + j))
    own = pl.BlockSpec((T, tc), lambda j: (0, j))
    return pl.pallas_call(
        body, name=name, grid=(nb,),
        in_specs=[col(0), col(1), col(2), pl.BlockSpec((3, tc), lambda j: (0, j)), own],
        out_specs=(own, own, own, pl.BlockSpec((3, tc), lambda j: (0, j))),
        out_shape=(jax.ShapeDtypeStruct((T, dc), BF16),) * 3 + (jax.ShapeDtypeStruct((3, dc), F32),),
        compiler_params=_params(("parallel",)),
    )(rest, rest, rest, conv_w, dy)


def _window_count(r, R, n_rows, w, first_row_offset):
    t = lax.broadcasted_iota(jnp.int32, (n_rows, 1), 0) + (r * R + first_row_offset)
    return jnp.minimum(t + 1, w).astype(F32)


def _pool_fwd(rest, pool_w, pool_scale, *, name, seg0, pg, rows=1056):
    T = rest.shape[0]
    R = _tile(T, rows, 16)
    n_chunks = T // R
    n_groups = len(POOL_WINDOWS)

    def body(x_ref, w_ref, s_ref, y_ref):
        def run(window):
            def chunk(r, carry):
                g = _chunk(x_ref, r, R)
                s = jnp.concatenate([_pre_halo(x_ref, r, R), g], axis=0)
                k = 1
                while k < window:
                    s = s + _down(s, k)
                    k *= 2
                pooled = s[HALO:] / _window_count(r, R, R, window, 0) - g
                mixed = jnp.dot(pooled.astype(BF16), w_ref[0], preferred_element_type=F32)
                y_ref[pl.ds(pl.multiple_of(r * R, 8), R), :] = (mixed * s_ref[...]).astype(BF16)
                return carry

            lax.fori_loop(0, n_chunks, chunk, 0)

        for gi, window in enumerate(POOL_WINDOWS):
            pl.when(pl.program_id(0) == gi)(functools.partial(run, window))

    return pl.pallas_call(
        body, name=name, grid=(n_groups,),
        in_specs=[pl.BlockSpec((T, pg), lambda g: (0, seg0 + g)),
                  pl.BlockSpec((1, pg, pg), lambda g: (g, 0, 0)),
                  pl.BlockSpec((1, pg), lambda g: (0, g))],
        out_specs=pl.BlockSpec((T, pg), lambda g: (0, g)),
        out_shape=jax.ShapeDtypeStruct((T, n_groups * pg), BF16),
        compiler_params=_params(("parallel",)),
    )(rest, pool_w, pool_scale)


def _pool_bwd(rest, pool_w, pool_scale, dy, *, name, seg0, pg, rows=1056):
    T = rest.shape[0]
    R = _tile(T, rows, 16)
    n_chunks = T // R
    n_groups = len(POOL_WINDOWS)

    def body(x_ref, w_ref, s_ref, dy_ref, dx_ref, dw_ref, ds_ref):
        def run(window):
            def chunk(r, carry):
                dw_acc, ds_acc = carry
                g = _chunk(x_ref, r, R)
                s = jnp.concatenate([_pre_halo(x_ref, r, R), g], axis=0)
                k = 1
                while k < window:
                    s = s + _down(s, k)
                    k *= 2
                pooled = (s[HALO:] / _window_count(r, R, R, window, 0) - g).astype(BF16)
                mixed = jnp.dot(pooled, w_ref[0], preferred_element_type=F32)
                dy_c = _chunk(dy_ref, r, R)
                dm_ext = (jnp.concatenate([dy_c, _post_halo(dy_ref, r, R, n_chunks)], axis=0)
                          * s_ref[...]).astype(BF16)
                dpool_ext = lax.dot_general(dm_ext, w_ref[0], (((1,), (1,)), ((), ())),
                                            preferred_element_type=F32)
                a = dpool_ext / _window_count(r, R, R + HALO, window, 0)
                k = 1
                while k < window:
                    a = a + _up(a, k)
                    k *= 2
                dx_ref[pl.ds(pl.multiple_of(r * R, 8), R), :] = (a[:R] - dpool_ext[:R]).astype(BF16)
                dw_acc = dw_acc + lax.dot_general(pooled, dm_ext[:R], (((0,), (0,)), ((), ())),
                                                  preferred_element_type=F32)
                ds_acc = ds_acc + jnp.sum(dy_c * mixed, axis=0, keepdims=True)
                return dw_acc, ds_acc

            dw_acc, ds_acc = lax.fori_loop(0, n_chunks, chunk,
                                           (jnp.zeros((pg, pg), F32), jnp.zeros((1, pg), F32)))
            dw_ref[0] = dw_acc
            ds_ref[...] = ds_acc

        for gi, window in enumerate(POOL_WINDOWS):
            pl.when(pl.program_id(0) == gi)(functools.partial(run, window))

    own = pl.BlockSpec((T, pg), lambda g: (0, g))
    return pl.pallas_call(
        body, name=name, grid=(n_groups,),
        in_specs=[pl.BlockSpec((T, pg), lambda g: (0, seg0 + g)),
                  pl.BlockSpec((1, pg, pg), lambda g: (g, 0, 0)),
                  pl.BlockSpec((1, pg), lambda g: (0, g)), own],
        out_specs=(own, pl.BlockSpec((1, pg, pg), lambda g: (g, 0, 0)), pl.BlockSpec((1, pg), lambda g: (0, g))),
        out_shape=(jax.ShapeDtypeStruct((T, n_groups * pg), BF16),
                   jax.ShapeDtypeStruct((n_groups, pg, pg), F32),
                   jax.ShapeDtypeStruct((1, n_groups * pg), F32)),
        compiler_params=_params(("parallel",)),
    )(rest, pool_w, pool_scale, dy)


def _rope(r, cos_t, sin_t):
    return r * cos_t + pltpu.roll(r, LANES // 2, 1) * sin_t


def _rope_t(d, cos_t, sin_t):
    return d * cos_t + pltpu.roll(d * sin_t, LANES // 2, 1)


def _qk_fwd(q_raw, k_nope, rest, cos_t, sin_t, q_norm, k_norm, *, name, heads, kr_seg, tm=192):
    T = q_raw.shape[0]
    tm = _tile(T, tm, 16)

    def body(q_ref, kn_ref, kr_ref, c_ref, s_ref, gq_ref, gk_ref, qo_ref, ko_ref):
        cos_b, sin_b = c_ref[...], s_ref[...]
        kr = kr_ref[:, 0:LANES]
        kr_ss = jnp.sum(kr * kr, axis=-1, keepdims=True)
        gq, gk = gq_ref[...], gk_ref[...]
        for h in range(heads):
            lo = h * HEAD_PAD
            q = q_ref[:, lo:lo + HEAD_PAD].astype(F32)
            rq = lax.rsqrt(jnp.sum(q * q, axis=-1, keepdims=True) / QK_HEAD + EPS)
            qn = q * (rq * Q_FOLD) * gq
            qo_ref[:, lo:lo + LANES] = qn[:, :LANES].astype(BF16)
            qo_ref[:, lo + LANES:lo + HEAD_PAD] = _rope(qn[:, LANES:], cos_b, sin_b).astype(BF16)
            kn = kn_ref[:, h * LANES:(h + 1) * LANES].astype(F32)
            rk = lax.rsqrt((jnp.sum(kn * kn, axis=-1, keepdims=True) + kr_ss) / QK_HEAD + EPS)
            ko_ref[:, lo:lo + LANES] = (kn * rk * gk[:, :LANES]).astype(BF16)
            ko_ref[:, lo + LANES:lo + HEAD_PAD] = _rope(kr * rk * gk[:, LANES:], cos_b, sin_b).astype(BF16)

    wq, wk = heads * HEAD_PAD, heads * LANES
    return pl.pallas_call(
        body, name=name, grid=(T // tm,),
        in_specs=[pl.BlockSpec((tm, wq), lambda i: (i, 0)), pl.BlockSpec((tm, wk), lambda i: (i, 0)),
                  pl.BlockSpec((tm, HEAD_PAD), lambda i: (i, kr_seg)),
                  pl.BlockSpec((tm, LANES), lambda i: (i, 0)), pl.BlockSpec((tm, LANES), lambda i: (i, 0)),
                  pl.BlockSpec((1, HEAD_PAD), lambda i: (0, 0)), pl.BlockSpec((1, HEAD_PAD), lambda i: (0, 0))],
        out_specs=(pl.BlockSpec((tm, wq), lambda i: (i, 0)), pl.BlockSpec((tm, wq), lambda i: (i, 0))),
        out_shape=(jax.ShapeDtypeStruct((T, wq), BF16), jax.ShapeDtypeStruct((T, wq), BF16)),
        compiler_params=_params(("parallel",)),
    )(q_raw, k_nope, rest, cos_t, sin_t, q_norm, k_norm)


def _qk_bwd(dq, dk, q_raw, k_nope, rest, cos_t, sin_t, q_norm, k_norm, *, name, heads, kr_seg, tm=128, after=()):
    T = q_raw.shape[0]
    tm = _tile(T, tm, 16)
    after = tuple(after)

    def body(dq_ref, dk_ref, q_ref, kn_ref, kr_ref, c_ref, s_ref, gq_ref, gk_ref, *rest_refs):
        dqr_ref, dkn_ref, dkr_ref, dgq_ref, dgk_ref = rest_refs[len(after):]
        cos_b, sin_b = c_ref[...], s_ref[...]
        kr = kr_ref[:, 0:LANES]
        kr_ss = jnp.sum(kr * kr, axis=-1, keepdims=True)
        gq, gk = gq_ref[...], gk_ref[...]
        dgq = jnp.zeros((1, HEAD_PAD), F32)
        dgk_n = jnp.zeros((1, LANES), F32)
        dgk_r = jnp.zeros((1, LANES), F32)
        dkr = jnp.zeros((tm, LANES), F32)
        for h in range(heads):
            lo = h * HEAD_PAD
            q = q_ref[:, lo:lo + HEAD_PAD].astype(F32)
            rq = lax.rsqrt(jnp.sum(q * q, axis=-1, keepdims=True) / QK_HEAD + EPS)
            qhat = q * rq
            dqn = jnp.concatenate([dq_ref[:, lo:lo + LANES],
                                   _rope_t(dq_ref[:, lo + LANES:lo + HEAD_PAD], cos_b, sin_b)], axis=1) * ATTN_SCALE
            dgq = dgq + jnp.sum(dqn * qhat, axis=0, keepdims=True)
            dqh = dqn * gq
            dqr_ref[:, lo:lo + HEAD_PAD] = (
                rq * (dqh - qhat * (jnp.sum(dqh * qhat, axis=-1, keepdims=True) / QK_HEAD))).astype(BF16)
            kn = kn_ref[:, h * LANES:(h + 1) * LANES].astype(F32)
            rk = lax.rsqrt((jnp.sum(kn * kn, axis=-1, keepdims=True) + kr_ss) / QK_HEAD + EPS)
            khat_n, khat_r = kn * rk, kr * rk
            dkn_n = dk_ref[:, lo:lo + LANES] * (1.0 / LOG2_E)
            dkn_r = _rope_t(dk_ref[:, lo + LANES:lo + HEAD_PAD], cos_b, sin_b) * (1.0 / LOG2_E)
            dgk_n = dgk_n + jnp.sum(dkn_n * khat_n, axis=0, keepdims=True)
            dgk_r = dgk_r + jnp.sum(dkn_r * khat_r, axis=0, keepdims=True)
            dkh_n, dkh_r = dkn_n * gk[:, :LANES], dkn_r * gk[:, LANES:]
            proj = (jnp.sum(dkh_n * khat_n, axis=-1, keepdims=True)
                    + jnp.sum(dkh_r * khat_r, axis=-1, keepdims=True)) / QK_HEAD
            dkn_ref[:, h * LANES:(h + 1) * LANES] = (rk * (dkh_n - khat_n * proj)).astype(BF16)
            dkr = dkr + rk * (dkh_r - khat_r * proj)
        dkr_ref[:, 0:LANES] = dkr.astype(BF16)
        dkr_ref[:, LANES:HEAD_PAD] = jnp.zeros((tm, HEAD_PAD - LANES), BF16)
        dgk = jnp.concatenate([dgk_n, dgk_r], axis=1)

        @pl.when(pl.program_id(0) == 0)
        def _():
            dgq_ref[...] = dgq
            dgk_ref[...] = dgk

        @pl.when(pl.program_id(0) > 0)
        def _():
            dgq_ref[...] += dgq
            dgk_ref[...] += dgk

    wq, wk = heads * HEAD_PAD, heads * LANES
    row = lambda w: pl.BlockSpec((tm, w), lambda i: (i, 0))
    vec = pl.BlockSpec((1, HEAD_PAD), lambda i: (0, 0))
    return pl.pallas_call(
        body, name=name, grid=(T // tm,),
        in_specs=[row(wq), row(wq), row(wq), row(wk), pl.BlockSpec((tm, HEAD_PAD), lambda i: (i, kr_seg)),
                  row(LANES), row(LANES), vec, vec] + [pl.BlockSpec(memory_space=pl.ANY)] * len(after),
        out_specs=(row(wq), row(wk), row(HEAD_PAD), vec, vec),
        out_shape=(jax.ShapeDtypeStruct((T, wq), BF16), jax.ShapeDtypeStruct((T, wk), BF16),
                   jax.ShapeDtypeStruct((T, HEAD_PAD), BF16),
                   jax.ShapeDtypeStruct((1, HEAD_PAD), F32), jax.ShapeDtypeStruct((1, HEAD_PAD), F32)),
        compiler_params=_params(("arbitrary",)),
    )(dq, dk, q_raw, k_nope, rest, cos_t, sin_t, q_norm, k_norm, *after)


def _causal_mask(s):
    row = lax.broadcasted_iota(jnp.int32, s.shape, 0)
    col = lax.broadcasted_iota(jnp.int32, s.shape, 1)
    return jnp.where(row >= col, s, NEG)


def _flash_fwd(q, k, v, *, name, heads, tq=384, hp=2, parts=2):
    T = q.shape[0]
    tq = _tile(T, tq, LANES)
    nq = T // tq
    tr = tq // parts
    nt = (((1,), (1,)), ((), ()))
    chains = [(h, r) for h in range(hp) for r in range(parts)]

    def body(q_ref, k_ref, v_ref, o_ref, lse_ref, acc_ref):
        def q_block(i, carry):
            rows_at = [pl.ds(pl.multiple_of(i * tq + r * tr, tr), tr) for r in range(parts)]
            qbs = [q_ref[rows_at[r], h * HEAD_PAD:(h + 1) * HEAD_PAD] for h, r in chains]
            for c in range(len(chains)):
                acc_ref[c] = jnp.zeros((tr, V_HEAD), F32)

            def step(j, state, masked):
                k_at = pl.ds(pl.multiple_of(j * tq, tq), tq)
                new = []
                scores = [lax.dot_general(qb, k_ref[k_at, h * HEAD_PAD:(h + 1) * HEAD_PAD], nt,
                                          preferred_element_type=F32) for qb, (h, r) in zip(qbs, chains)]
                for c, (s, (h, r)) in enumerate(zip(scores, chains)):
                    m, l = state[c]
                    if masked:
                        row = lax.broadcasted_iota(jnp.int32, s.shape, 0) + r * tr
                        s = jnp.where(row >= lax.broadcasted_iota(jnp.int32, s.shape, 1), s, NEG)
                    m_new = jnp.maximum(m, jnp.max(s, axis=-1, keepdims=True))
                    p = jnp.exp2(s - m_new)
                    alpha = jnp.exp2(m - m_new)
                    new.append((m_new, alpha * l + jnp.sum(p, axis=-1, keepdims=True)))
                    acc_ref[c] = alpha * acc_ref[c] + jnp.dot(p.astype(BF16), v_ref[k_at, h * V_HEAD:(h + 1) * V_HEAD],
                                                              preferred_element_type=F32)
                return tuple(new)

            init = tuple((jnp.full((tr, 1), NEG, F32), jnp.zeros((tr, 1), F32)) for _ in chains)
            state = lax.fori_loop(0, i, lambda j, st: step(j, st, False), init)
            state = step(i, state, True)
            for c, ((m, l), (h, r)) in enumerate(zip(state, chains)):
                o_ref[rows_at[r], h * V_HEAD:(h + 1) * V_HEAD] = (acc_ref[c] / l).astype(BF16)
                lse_ref[h, rows_at[r], :] = jnp.broadcast_to(m + jnp.log2(l), (tr, LANES))
            return carry

        lax.fori_loop(0, nq, q_block, 0)

    qk_spec = pl.BlockSpec((T, hp * HEAD_PAD), lambda g: (0, g))
    v_spec = pl.BlockSpec((T, hp * V_HEAD), lambda g: (0, g))
    return pl.pallas_call(
        body, name=name, grid=(heads // hp,), in_specs=[qk_spec, qk_spec, v_spec],
        out_specs=(v_spec, pl.BlockSpec((hp, T, LANES), lambda g: (g, 0, 0))),
        out_shape=(jax.ShapeDtypeStruct((T, heads * V_HEAD), BF16), jax.ShapeDtypeStruct((heads, T, LANES), F32)),
        scratch_shapes=[pltpu.VMEM((len(chains), tr, V_HEAD), F32)],
        compiler_params=_params(("parallel",)),
    )(q, k, v)


def _flash_bwd(q, k, v, o, do, lse, *, name, heads, tq=384, after=()):
    T = q.shape[0]
    tq = _tile(T, tq, LANES)
    nq = T // tq
    nt = (((1,), (1,)), ((), ()))
    tn = (((0,), (0,)), ((), ()))

    after = tuple(after)

    def body(q_ref, k_ref, v_ref, o_ref, do_ref, lse_ref, *rest):
        dq_ref, dk_ref, dv_ref, delta_ref, dv_acc_ref = rest[len(after):]
        def fill_delta(i, carry):
            at = pl.ds(pl.multiple_of(i * tq, tq), tq)
            d = jnp.sum(o_ref[at, :].astype(F32) * do_ref[at, :].astype(F32), axis=-1, keepdims=True)
            delta_ref[at, :] = jnp.broadcast_to(d, (tq, LANES))
            dq_ref[at, :] = jnp.zeros((tq, HEAD_PAD), F32)
            return carry

        lax.fori_loop(0, nq, fill_delta, 0)

        def kv_block(j, carry):
            k_at = pl.ds(pl.multiple_of(j * tq, tq), tq)
            kb, vb = k_ref[k_at, :], v_ref[k_at, :]

            def steps(blocks, masked):
                at = [pl.ds(pl.multiple_of(i * tq, tq), tq) for i in blocks]
                qbs = [q_ref[a, :] for a in at]
                dobs = [do_ref[a, :] for a in at]
                scores = [lax.dot_general(qb, kb, nt, preferred_element_type=F32) for qb in qbs]
                dps = [lax.dot_general(dob, vb, nt, preferred_element_type=F32) for dob in dobs]
                for a, qb, dob, sc, dp in zip(at, qbs, dobs, scores, dps):
                    if masked:
                        sc = _causal_mask(sc)
                    p = jnp.exp2(sc - lse_ref[0, a, :][:, 0:1])
                    ds = (p * (dp - delta_ref[a, :][:, 0:1])).astype(BF16)
                    dv_part = lax.dot_general(p.astype(BF16), dob, tn, preferred_element_type=F32)
                    dk_part = lax.dot_general(ds, qb, tn, preferred_element_type=F32)
                    if masked:
                        dv_acc_ref[...] = dv_part
                        dk_ref[k_at, :] = dk_part
                    else:
                        dv_acc_ref[...] += dv_part
                        dk_ref[k_at, :] += dk_part
                    dq_ref[a, :] += jnp.dot(ds, kb, preferred_element_type=F32)

            def two_blocks(t, carry):
                steps([j + 1 + 2 * t, j + 2 + 2 * t], False)
                return carry

            steps([j], True)
            rest = nq - 1 - j
            lax.fori_loop(0, rest // 2, two_blocks, 0)

            @pl.when(rest % 2 == 1)
            def _():
                steps([nq - 1], False)

            dv_ref[k_at, :] = dv_acc_ref[...].astype(BF16)
            return carry

        lax.fori_loop(0, nq, kv_block, 0)

    qk_spec = pl.BlockSpec((T, HEAD_PAD), lambda h: (0, h))
    v_spec = pl.BlockSpec((T, V_HEAD), lambda h: (0, h))
    return pl.pallas_call(
        body, name=name, grid=(heads,),
        in_specs=[qk_spec, qk_spec, v_spec, v_spec, v_spec, pl.BlockSpec((1, T, LANES), lambda h: (h, 0, 0))]
        + [pl.BlockSpec(memory_space=pl.ANY)] * len(after),
        out_specs=(qk_spec, qk_spec, v_spec),
        out_shape=(jax.ShapeDtypeStruct((T, heads * HEAD_PAD), F32), jax.ShapeDtypeStruct((T, heads * HEAD_PAD), F32),
                   jax.ShapeDtypeStruct((T, heads * V_HEAD), BF16)),
        scratch_shapes=[pltpu.VMEM((T, LANES), F32), pltpu.VMEM((tq, V_HEAD), F32)],
        compiler_params=_params(("parallel",)),
    )(q, k, v, o, do, lse, *after)


def _merge_fwd(gl, pa, pb, pc, *, name, d, tm=384, tn=1024):
    T = pa.shape[0]
    tm, tn = _tile(T, tm, 16), _tile(d, tn)
    nb = d // tn

    def body(g0, g1, g2, a, b, c, o_ref):
        f = lambda ref: ref[...].astype(F32)
        o_ref[...] = (jax.nn.sigmoid(f(g0)) * f(a) + jax.nn.sigmoid(f(g1)) * f(b)
                      + jax.nn.sigmoid(f(g2)) * f(c)).astype(BF16)

    gate = lambda n: pl.BlockSpec((tm, tn), lambda i, j: (i, n * nb + j))
    blk = pl.BlockSpec((tm, tn), lambda i, j: (i, j))
    return pl.pallas_call(
        body, name=name, grid=(T // tm, nb), in_specs=[gate(0), gate(1), gate(2), blk, blk, blk],
        out_specs=blk, out_shape=jax.ShapeDtypeStruct((T, d), BF16),
        compiler_params=_params(("parallel", "parallel")),
    )(gl, gl, gl, pa, pb, pc)


def _merge_bwd(dm, gl, pa, pb, pc, *, name, d, tm=384, tn=1024):
    T = pa.shape[0]
    tm, tn = _tile(T, tm, 16), _tile(d, tn)
    nb = d // tn

    def body(dm_ref, g0, g1, g2, a, b, c, da, db, dc, dg0, dg1, dg2):
        dmv = dm_ref[...].astype(F32)
        for g_ref, p_ref, dp_ref, dg_ref in ((g0, a, da, dg0), (g1, b, db, dg1), (g2, c, dc, dg2)):
            sg = jax.nn.sigmoid(g_ref[...].astype(F32))
            dp_ref[...] = (dmv * sg).astype(BF16)
            dg_ref[...] = (dmv * p_ref[...].astype(F32) * sg * (1.0 - sg)).astype(BF16)

    gate = lambda n: pl.BlockSpec((tm, tn), lambda i, j: (i, n * nb + j))
    blk = pl.BlockSpec((tm, tn), lambda i, j: (i, j))
    return pl.pallas_call(
        body, name=name, grid=(T // tm, nb), in_specs=[blk, gate(0), gate(1), gate(2), blk, blk, blk],
        out_specs=(blk,) * 6, out_shape=(jax.ShapeDtypeStruct((T, d), BF16),) * 6,
        compiler_params=_params(("parallel", "parallel")),
    )(dm, gl, gl, gl, pa, pb, pc)


def _loss(y, target, *, name, first, last, tm=384):
    T, d = y.shape
    tm = _tile(T, tm, 16)

    def body(y_ref, t_ref, loss_ref, dy_ref, dyb_ref):
        i = pl.program_id(0)
        row = lax.broadcasted_iota(jnp.int32, (tm, 1), 0) + i * tm
        real = jnp.logical_and(row >= first, row < last)
        err = jnp.where(real, y_ref[...] - t_ref[...], 0.0)
        dy_ref[...] = err * (1.0 / d)
        dyb_ref[...] = (err * (1.0 / d)).astype(BF16)
        part = jnp.broadcast_to(jnp.sum(err * err, keepdims=True).reshape(1, 1), (1, LANES))

        @pl.when(i == 0)
        def _():
            loss_ref[...] = part

        @pl.when(i > 0)
        def _():
            loss_ref[...] += part

    blk = pl.BlockSpec((tm, d), lambda i: (i, 0))
    return pl.pallas_call(
        body, name=name, grid=(T // tm,), in_specs=[blk, blk],
        out_specs=(pl.BlockSpec((1, LANES), lambda i: (0, 0)), blk, blk),
        out_shape=(jax.ShapeDtypeStruct((1, LANES), F32), jax.ShapeDtypeStruct((T, d), F32),
                   jax.ShapeDtypeStruct((T, d), BF16)),
        compiler_params=_params(("arbitrary",)),
    )(y, target)


def _as3d(a):
    return a.reshape(a.shape[0], -1, a.shape[-1])


def _sum_stack(parts, *, name, out_dtype, rows=256):
    n, R, C = parts.shape
    tr = _tile(R, rows, 16)

    def body(p_ref, o_ref):
        acc = p_ref[0].astype(F32)
        for s in range(1, n):
            acc = acc + p_ref[s].astype(F32)
        o_ref[...] = acc.astype(out_dtype)

    return pl.pallas_call(
        body, name=name, grid=(R // tr,),
        in_specs=[pl.BlockSpec((n, tr, C), lambda i: (0, i, 0))],
        out_specs=pl.BlockSpec((tr, C), lambda i: (i, 0)),
        out_shape=jax.ShapeDtypeStruct((R, C), out_dtype),
        compiler_params=_params(("parallel",)),
    )(parts)


def _adamw(w, g, m, v, *, name, rows=128):
    R, C = w.shape
    tr = _tile(R, rows, 8)
    c1 = 1.0 - ADAM_B1 ** ADAM_STEP
    c2 = 1.0 - ADAM_B2 ** ADAM_STEP

    def body(w_ref, g_ref, m_ref, v_ref, d_ref, nm_ref, nv_ref):
        gv = g_ref[...]
        nm = ADAM_B1 * m_ref[...] + (1.0 - ADAM_B1) * gv
        nv = ADAM_B2 * v_ref[...] + (1.0 - ADAM_B2) * (gv * gv)
        nm_ref[...] = nm
        nv_ref[...] = nv
        d_ref[...] = -ADAM_LR * ((nm / c1) / (jnp.sqrt(nv / c2) + ADAM_EPS) + ADAM_WD * w_ref[...])

    blk = pl.BlockSpec((tr, C), lambda i: (i, 0))
    return pl.pallas_call(
        body, name=name, grid=(R // tr,), in_specs=[blk] * 4, out_specs=(blk,) * 3,
        out_shape=(jax.ShapeDtypeStruct((R, C), F32),) * 3,
        compiler_params=_params(("parallel",)),
    )(w, g, m, v)


def _one_hot(index, n):
    return jnp.broadcast_to((jnp.arange(n) == index).astype(F32)[:, None, None], (n, 8, LANES))


def _is_set(flags_ref, s):
    return flags_ref[s, 0:1, 0:1] > 0.5


def _rows_for(h, width, itemsize, n_stacked, budget, mult):
    return _tile(h, max(mult, budget // (n_stacked * width * itemsize)), mult)


def _pair_sum(pieces, recv, core, *, name):
    _, H, C = recv.shape
    tr = _rows_for(H, C, 2, 1, 2 << 20, 16)
    nh = H // tr
    halves_lead = pieces.ndim == 4

    def body(lo_ref, hi_ref, r_ref, core_ref, o_ref):
        lo, hi = (lo_ref[0, 0], hi_ref[0, 0]) if halves_lead else (lo_ref[0], hi_ref[0])
        mine = jnp.where(_is_set(core_ref, 0), lo, hi)
        o_ref[0] = (mine.astype(F32) + r_ref[0].astype(F32)).astype(BF16)

    blk = pl.BlockSpec((1, tr, C), lambda j, i: (j, i, 0))
    if halves_lead:
        lo_spec = pl.BlockSpec((1, 1, tr, C), lambda j, i: (0, j, i, 0))
        hi_spec = pl.BlockSpec((1, 1, tr, C), lambda j, i: (1, j, i, 0))
    else:
        lo_spec, hi_spec = blk, pl.BlockSpec((1, tr, C), lambda j, i: (j, nh + i, 0))
    return pl.pallas_call(
        body, name=name, grid=(4, nh),
        in_specs=[lo_spec, hi_spec, blk, pl.BlockSpec((2, 8, LANES), lambda j, i: (0, 0, 0))],
        out_specs=blk, out_shape=jax.ShapeDtypeStruct((4, H, C), BF16),
        compiler_params=_params(("parallel", "parallel")),
    )(pieces, pieces, recv, core)


def _chip_sum(pair, landed, chip_flags, *, name):
    _, H, C = pair.shape
    tr = _rows_for(H, C, 2, 4, 8 << 20, 16)

    def body(p_ref, l_ref, chip_ref, o_ref):
        acc = None
        for s in range(4):
            part = jnp.where(_is_set(chip_ref, s), p_ref[s], l_ref[s]).astype(F32)
            acc = part if acc is None else acc + part
        o_ref[...] = acc

    blk = pl.BlockSpec((4, tr, C), lambda i: (0, i, 0))
    return pl.pallas_call(
        body, name=name, grid=(H // tr,),
        in_specs=[blk, blk, pl.BlockSpec((4, 8, LANES), lambda i: (0, 0, 0))],
        out_specs=pl.BlockSpec((tr, C), lambda i: (i, 0)), out_shape=jax.ShapeDtypeStruct((H, C), F32),
        compiler_params=_params(("parallel",)),
    )(pair, landed, chip_flags)


def _adamw_layer(w, m, v, total, recv, core, layer, prev, *, name, col_halves=False, after=()):
    _, R, C = w.shape
    H, wd = total.shape
    tr = _rows_for(H, wd, 4, 1, 2 << 20, 8)
    nh = H // tr
    c1 = 1.0 - ADAM_B1 ** ADAM_STEP
    c2 = 1.0 - ADAM_B2 ** ADAM_STEP
    n_prev = 0 if prev is None else 4
    after = tuple(after)

    def body(*refs):
        w_ref, m_ref, v_ref, t_ref, r_ref, core_ref = refs[:6]
        g_ref, d_ref, nm_ref, nv_ref = refs[6 + n_prev + len(after):]
        half_is_mine = jnp.where(pl.program_id(0) == 0, core_ref[0, 0:1, 0:1], core_ref[1, 0:1, 0:1]) > 0.5
        gv = jnp.where(half_is_mine, t_ref[...], r_ref[...])
        nm = ADAM_B1 * m_ref[0] + (1.0 - ADAM_B1) * gv
        nv = ADAM_B2 * v_ref[0] + (1.0 - ADAM_B2) * (gv * gv)
        g_ref[0] = gv
        nm_ref[0] = nm
        nv_ref[0] = nv
        d_ref[0] = -ADAM_LR * ((nm / c1) / (jnp.sqrt(nv / c2) + ADAM_EPS) + ADAM_WD * w_ref[0])

    if col_halves:
        lay = pl.BlockSpec((1, tr, wd), lambda hf, i: (layer, i, hf))
    else:
        lay = pl.BlockSpec((1, tr, wd), lambda hf, i: (layer, hf * nh + i, 0))
    one = pl.BlockSpec((tr, wd), lambda hf, i: (i, 0))
    operands = [w, m, v, total, recv, core] + ([] if prev is None else list(prev)) + list(after)
    return pl.pallas_call(
        body, name=name, grid=(2, nh),
        in_specs=[lay, lay, lay, one, one, pl.BlockSpec((2, 8, LANES), lambda hf, i: (0, 0, 0))]
        + [ANY] * (n_prev + len(after)),
        out_specs=(lay,) * 4, out_shape=(jax.ShapeDtypeStruct((2, R, C), F32),) * 4,
        input_output_aliases={6 + i: i for i in range(n_prev)},
        compiler_params=_params(("parallel", "parallel")),
    )(*operands)


ANY = pl.BlockSpec(memory_space=pl.ANY)


def _coords():
    return lax.axis_index("x"), lax.axis_index("y"), lax.axis_index("c")


HBM = pl.BlockSpec(memory_space=pltpu.HBM)
SEM = pl.BlockSpec(memory_space=pltpu.SEMAPHORE)
EFFECT = pltpu.SideEffectType.DATAFLOW_SIDE_EFFECTING


def _copies(plan, bufs, send_sems, recv_sems):
    return [pltpu.make_async_remote_copy(src_ref=s, dst_ref=d, send_sem=send_sems.at[i], recv_sem=recv_sems.at[i],
                                         device_id=to, device_id_type=MESH)
            for i, (s, d, to) in enumerate(plan(bufs))]


def _start_copies(bufs, groups, *, name):
    nb, ng = len(bufs), len(groups)

    def body(*refs):
        buf_refs = refs[:nb]
        sems = refs[nb:nb + 2 * ng]
        token = refs[-1]
        for g, (plan, _) in enumerate(groups):
            for cp in _copies(plan, buf_refs, sems[2 * g], sems[2 * g + 1]):
                cp.start()
        token[...] = jnp.zeros_like(token)

    sem_shapes = []
    for _, n in groups:
        sem_shapes += [pltpu.SemaphoreType.DMA((n,)), pltpu.SemaphoreType.DMA((n,))]
    out = pl.pallas_call(
        body, name=name, in_specs=[HBM] * nb,
        out_specs=tuple([SEM] * (2 * ng) + [HBM] * nb + [pl.BlockSpec(memory_space=pltpu.VMEM)]),
        out_shape=tuple(sem_shapes + [pltpu.HBM(b.shape, b.dtype) for b in bufs] + [jax.ShapeDtypeStruct((8, LANES), F32)]),
        input_output_aliases={i: 2 * ng + i for i in range(nb)},
        compiler_params=pltpu.CompilerParams(has_side_effects=EFFECT),
    )(*[pltpu.with_memory_space_constraint(b, pltpu.HBM) for b in bufs])
    sems = [(out[2 * g], out[2 * g + 1]) for g in range(ng)]
    return sems, list(out[2 * ng:2 * ng + nb]), out[-1]


def _wait_copies(bufs, sems, plan, after, *, name):
    nb = len(bufs)

    def body(*refs):
        buf_refs = refs[:nb]
        for cp in _copies(plan, buf_refs, refs[nb], refs[nb + 1]):
            cp.wait_send()
            cp.wait_recv()

    out = pl.pallas_call(
        body, name=name, in_specs=[HBM] * nb + [SEM, SEM, ANY], out_specs=tuple([HBM] * nb),
        out_shape=tuple(pltpu.HBM(b.shape, b.dtype) for b in bufs),
        input_output_aliases={i: i for i in range(nb)},
        compiler_params=pltpu.CompilerParams(has_side_effects=EFFECT),
    )(*bufs, sems[0], sems[1], after)
    return list(out)


def _half(ref, c):
    h = ref.shape[0] // 2
    return ref.at[pl.ds(c * h, h)]


def _ici_gather_plan(pairs):
    def plan(refs):
        x, y, c = _coords()
        me = 2 * x + y
        out = []
        for s, d in pairs:
            for cx, cy in [(1 - x, y), (x, 1 - y), (1 - x, 1 - y)]:
                out.append((_half(refs[s], c), _half(refs[d].at[me], c), (cx, cy, c)))
            out.append((refs[s], refs[d].at[me], (x, y, 1 - c)))
        return out
    return plan, 4 * len(pairs)


def _d2d_forward_plan(lands):
    def plan(refs):
        x, y, c = _coords()
        out = []
        for d in lands:
            for cx, cy in [(1 - x, y), (x, 1 - y), (1 - x, 1 - y)]:
                got = _half(refs[d].at[2 * cx + cy], c)
                out.append((got, got, (x, y, 1 - c)))
        return out
    return plan, 3 * len(lands)


def _swap_half_plan(pairs):
    def plan(refs):
        x, y, c = _coords()
        out = []
        for s, d in pairs:
            h = refs[d].shape[1]
            other = refs[s].at[1 - c] if len(refs[s].shape) == 4 else refs[s].at[:, pl.ds((1 - c) * h, h)]
            out.append((other, refs[d], (x, y, 1 - c)))
        return out
    return plan, len(pairs)


def _scatter_plan(pairs):
    def plan(refs):
        x, y, c = _coords()
        me = 2 * x + y
        out = []
        for s, d in pairs:
            for cx, cy in [(1 - x, y), (x, 1 - y), (1 - x, 1 - y)]:
                out.append((refs[s].at[2 * cx + cy], refs[d].at[me], (cx, cy, c)))
        return out
    return plan, 3 * len(pairs)


def _swap_total_plan(pairs):
    def plan(refs):
        x, y, c = _coords()
        return [(refs[s], refs[d], (x, y, 1 - c)) for s, d in pairs]
    return plan, len(pairs)


def _gather_all(block, *, name, after=()):
    after = tuple(after)

    def body(src, *rest):
        out, send_sems, recv_sems, local_sem = rest[len(after):]
        x, y, c = _coords()
        me = 4 * x + 2 * y + c
        flips = [(fx, fy, fc) for fx in (0, 1) for fy in (0, 1) for fc in (0, 1)][1:]
        mine = pltpu.make_async_copy(src, out.at[me], local_sem)
        mine.start()
        peers = [(x ^ fx, y ^ fy, c ^ fc) for fx, fy, fc in flips]
        cps = [pltpu.make_async_remote_copy(src_ref=src, dst_ref=out.at[me], send_sem=send_sems.at[k],
                                            recv_sem=recv_sems.at[k], device_id=peer, device_id_type=MESH)
               for k, peer in enumerate(peers)]
        for cp in cps:
            cp.start()
        for k, (px, py, pc) in enumerate(peers):
            slot = out.at[4 * px + 2 * py + pc]
            pltpu.make_async_remote_copy(src_ref=slot, dst_ref=slot, send_sem=send_sems.at[k], recv_sem=recv_sems.at[k],
                                         device_id=(px, py, pc), device_id_type=MESH).wait_recv()
        for cp in cps:
            cp.wait_send()
        mine.wait()

    return pl.pallas_call(
        body, name=name, in_specs=[ANY] * (1 + len(after)), out_specs=ANY,
        out_shape=jax.ShapeDtypeStruct((8,) + block.shape, block.dtype),
        scratch_shapes=[pltpu.SemaphoreType.DMA((7,)), pltpu.SemaphoreType.DMA((7,)), pltpu.SemaphoreType.DMA],
    )(block, *after)


def _cols(o):
    return jnp.transpose(o, (1, 0, 2)).reshape(o.shape[1], -1)


def _uncols(full):
    return jnp.transpose(full.reshape(full.shape[0], 4, -1), (1, 0, 2))


def _rope_pad(x1, x2):
    z = jnp.zeros_like(x1)
    return jnp.concatenate([x1, z, x2, z], axis=-1)


def _head_pad(w, heads):
    r = w.reshape(w.shape[0], heads, QK_HEAD)
    half = QK_ROPE // 2
    out = jnp.concatenate([r[..., :QK_NOPE], _rope_pad(r[..., QK_NOPE:QK_NOPE + half], r[..., QK_NOPE + half:])], axis=-1)
    return out.reshape(w.shape[0], heads * HEAD_PAD)


def _head_unpad(w, heads):
    r = w.reshape(w.shape[0], heads, HEAD_PAD)
    half = QK_ROPE // 2
    out = jnp.concatenate([r[..., :QK_NOPE], r[..., QK_NOPE:QK_NOPE + half],
                           r[..., QK_NOPE + 2 * half:QK_NOPE + 3 * half]], axis=-1)
    return out.reshape(w.shape[0], heads * QK_HEAD)


class _Dims:
    def __init__(self, d, seq):
        self.d = d
        self.seq = seq
        self.t_real = N_META + seq
        self.t = -(-self.t_real // LANES) * LANES
        self.dc = d // 2
        self.dp = d // 2
        self.pg = self.dp // len(POOL_WINDOWS)
        self.heads = d // 128
        self.dff = 4 * d
        self.a_end = 3 * self.dc
        self.q_end = self.a_end + Q_LORA
        self.kv_end = self.q_end + KV_LORA
        self.kr_end = self.kv_end + QK_ROPE
        self.pool_end = self.kr_end + self.dp
        self.d_in = self.pool_end + 3 * d
        self.r_pool = 3 * self.dc
        self.r_q = self.r_pool + self.dp
        self.r_kv = self.r_q + Q_LORA
        self.r_kr = self.r_kv + KV_LORA
        self.r_width = self.r_kr + HEAD_PAD


def _split_cols(a):
    return jnp.moveaxis(a.reshape(a.shape[:-1] + (2, a.shape[-1] // 2)), -2, -3)


def _join_cols(a):
    a = jnp.moveaxis(a, -3, -2)
    return a.reshape(a.shape[:-2] + (a.shape[-2] * a.shape[-1],))


def _in_weights(dm, pieces):
    w_t = _join_cols(pieces).reshape(dm.d_in, dm.d)
    half = QK_ROPE // 2
    kr = w_t[dm.kv_end:dm.kr_end]
    zeros = jnp.zeros((half, dm.d), BF16)
    kr_p = jnp.concatenate([kr[:half], zeros, kr[half:], zeros, jnp.zeros((HEAD_PAD - LANES, dm.d), BF16)], axis=0)
    return dict(
        wg_t=w_t[dm.pool_end:],
        wr_t=jnp.concatenate([w_t[:dm.a_end], w_t[dm.kr_end:dm.pool_end], w_t[dm.a_end:dm.kv_end], kr_p], axis=0))


def _other_weights(dm, g):
    out = {}
    if "w_ukv" in g:
        w_ukv = _cols(g["w_ukv"]).reshape(KV_LORA, dm.heads, QK_NOPE + V_HEAD)
        out["wkn"] = w_ukv[:, :, :QK_NOPE].reshape(KV_LORA, dm.heads * QK_NOPE)
        out["wv"] = w_ukv[:, :, QK_NOPE:].reshape(KV_LORA, dm.heads * V_HEAD)
    if "w_uq" in g:
        out["wuq"] = _head_pad(_cols(g["w_uq"]), dm.heads)
    if "pool_w" in g:
        out["wp"] = jnp.transpose(g["pool_w"], (1, 0, 2, 3)).reshape(len(POOL_WINDOWS), dm.pg, dm.pg)
    for name, key in (("w_branch_a", "wba"), ("w_branch_c", "wbc"), ("w_up", "wup")):
        if name in g:
            out[key] = _cols(g[name])
    for name, key in (("w_branch_b", "wbb"), ("w_o", "wo"), ("w_down", "wdn")):
        if name in g:
            out[key] = g[name].reshape(-1, dm.d)
    return out


def _small_weights(small):
    return dict(
        conv_w=small["conv_w"],
        attn_norm=small["attn_norm"][None], mlp_norm=small["mlp_norm"][None],
        q_lat_norm=small["q_lat_norm"][None], kv_lat_norm=small["kv_lat_norm"][None],
        q_norm=_head_pad(small["q_norm"][None], 1), k_norm=_head_pad(small["k_norm"][None], 1),
        pool_scale=small["pool_scale"][None],
    )


def _grad_piece(dm, dw, name):
    half = QK_ROPE // 2
    rows = lambda a: a.reshape((4, a.shape[0] // 4) + a.shape[1:])
    if name == "w_in":
        dwr, dwg = dw["wr_t"], dw["wg_t"]
        d_t = jnp.concatenate([
            dwr[:, :dm.r_pool], dwr[:, dm.r_q:dm.r_kr], dwr[:, dm.r_kr:dm.r_kr + half],
            dwr[:, dm.r_kr + 2 * half:dm.r_kr + 3 * half], dwr[:, dm.r_pool:dm.r_q], dwg], axis=1)
        out = d_t.reshape(2, 4, d_t.shape[1] // 4, d_t.shape[2])
    elif name == "w_ukv":
        out = _uncols(jnp.concatenate([dw["wkn"].reshape(KV_LORA, dm.heads, QK_NOPE),
                                       dw["wv"].reshape(KV_LORA, dm.heads, V_HEAD)], axis=-1).reshape(KV_LORA, -1))
    elif name == "w_uq":
        out = _uncols(_head_unpad(dw["wuq"], dm.heads))
    elif name == "pool_w":
        out = jnp.transpose(dw["wp"].reshape(len(POOL_WINDOWS), 4, dm.pg // 4, dm.pg), (1, 0, 2, 3))
    elif name in ("w_branch_a", "w_branch_c", "w_up"):
        out = dw[{"w_branch_a": "wba", "w_branch_c": "wbc", "w_up": "wup"}[name]]
    else:
        out = rows(dw[{"w_branch_b": "wbb", "w_o": "wo", "w_down": "wdn"}[name]])
    return out.astype(BF16)


def _layer_fwd(dm, W, x, cos_t, sin_t, tag, more=None, h=None):
    n = lambda s: f"{s}_{tag}"
    if h is None:
        h = _rms_fwd(x, W["attn_norm"], name=n("attn_norm"))
    gl = _mm(h, W["wg_t"], name=n("proj_gates"), tb=True, out_dtype=BF16)
    rest = _mm(h, W["wr_t"], name=n("proj_rest"), tb=True)
    if more is not None:
        W.update(more("after_proj", rest))
    y_a = _conv_fwd(rest, W["conv_w"], name=n("conv"), dc=dm.dc)
    y_c = _pool_fwd(rest, W["wp"], W["pool_scale"], name=n("pool"), seg0=dm.r_pool // dm.pg, pg=dm.pg)
    q_lat = _rms_fwd(rest, W["q_lat_norm"], name=n("q_lat_norm"), width=Q_LORA, seg=dm.r_q // Q_LORA)
    kv_lat = _rms_fwd(rest, W["kv_lat_norm"], name=n("kv_lat_norm"), width=KV_LORA, seg=dm.r_kv // KV_LORA)
    q_raw = _mm(q_lat, W["wuq"], name=n("up_q"), out_dtype=BF16)
    k_nope = _mm(kv_lat, W["wkn"], name=n("up_k"), out_dtype=BF16)
    v = _mm(kv_lat, W["wv"], name=n("up_v"), out_dtype=BF16)
    q, k = _qk_fwd(q_raw, k_nope, rest, cos_t, sin_t, W["q_norm"], W["k_norm"], name=n("qk_norm_rope"),
                   heads=dm.heads, kr_seg=dm.r_kr // HEAD_PAD)
    y_b, lse = _flash_fwd(q, k, v, name=n("attention"), heads=dm.heads)
    if more is not None:
        W.update(more("after_attention", y_b))
    pa = _mm(y_a, W["wba"], name=n("branch_a"), out_dtype=BF16)
    pb = _mm(y_b, W["wbb"], name=n("branch_b"), out_dtype=BF16, after=W.pop("pin", ()))
    pc = _mm(y_c, W["wbc"], name=n("branch_c"), out_dtype=BF16)
    merged = _merge_fwd(gl, pa, pb, pc, name=n("merge"), d=dm.d)
    x1 = _mm(merged, W["wo"], name=n("out_proj"), add=x)
    h2 = _rms_fwd(x1, W["mlp_norm"], name=n("mlp_norm"))
    if more is not None:
        W.update(more("before_mlp", h2))
    up, act = _mm(h2, W["wup"], name=n("mlp_up"), epi="relu2")
    x2 = _mm(act, W["wdn"], name=n("mlp_down"), add=x1, tm=704, tk=4096)
    saved = dict(x=x, h=h, gl=gl, rest=rest, y_a=y_a, y_c=y_c, q_lat=q_lat, kv_lat=kv_lat, q_raw=q_raw, k_nope=k_nope,
                 v=v, q=q, k=k, y_b=y_b, lse=lse, pa=pa, pb=pb, pc=pc, merged=merged, x1=x1, h2=h2, up=up, act=act)
    return x2, saved


def _layer_bwd(dm, W, S, dx2, dx2_b, cos_t, sin_t, tag, hook=None):
    n = lambda s: f"{s}_{tag}"
    dw, ds = {}, {}
    if hook is None:
        hook = lambda point, t, dw_so_far: ()
    dup = _mm(dx2_b, W["wdn"], name=n("d_mlp_down"), tb=True, aux=S["up"], epi="drelu2", out_dtype=BF16,
              after=hook("start", dx2, dw))
    dw["wdn"] = _mm(S["act"], dx2_b, name=n("dw_mlp_down"), ta=True, tm=512, out_dtype=BF16)
    dh2 = _mm(dup, W["wup"], name=n("d_mlp_up"), tb=True, tm=704, tk=4096)
    dw["wup"] = _mm(S["h2"], dup, name=n("dw_mlp_up"), ta=True, tm=512, out_dtype=BF16, pieces=4)
    dx1, dx1_b, ds["mlp_norm"] = _rms_bwd(dh2, S["x1"], W["mlp_norm"], name=n("d_mlp_norm"), res=dx2, bf16_copy=True)
    dmerged = _mm(dx1_b, W["wo"], name=n("d_out_proj"), tb=True, after=hook("after_mlp", dx1, dw))
    dw["wo"] = _mm(S["merged"], dx1_b, name=n("dw_out_proj"), ta=True, tm=512, out_dtype=BF16)
    dpa, dpb, dpc, dg0, dg1, dg2 = _merge_bwd(dmerged, S["gl"], S["pa"], S["pb"], S["pc"], name=n("d_merge"), d=dm.d)
    dgl = jnp.concatenate([dg0, dg1, dg2], axis=1)
    dy_a = _mm(dpa, W["wba"], name=n("d_branch_a"), tb=True)
    dw["wba"] = _mm(S["y_a"], dpa, name=n("dw_branch_a"), ta=True, tm=512, out_dtype=BF16, pieces=4)
    dy_b = _mm(dpb, W["wbb"], name=n("d_branch_b"), tb=True, out_dtype=BF16)
    dw["wbb"] = _mm(S["y_b"], dpb, name=n("dw_branch_b"), ta=True, tm=512, out_dtype=BF16)
    dy_c = _mm(dpc, W["wbc"], name=n("d_branch_c"), tb=True)
    dw["wbc"] = _mm(S["y_c"], dpc, name=n("dw_branch_c"), ta=True, tm=512, out_dtype=BF16, pieces=4)
    dq, dk, dv = _flash_bwd(S["q"], S["k"], S["v"], S["y_b"], dy_b, S["lse"], name=n("d_attention"), heads=dm.heads,
                            after=hook("before_attention", dw["wbc"], dw))
    after_attention = hook("after_attention", dq, dw)
    dq_raw, dk_nope, dk_rope, dgq, dgk = _qk_bwd(
        dq, dk, S["q_raw"], S["k_nope"], S["rest"], cos_t, sin_t, W["q_norm"], W["k_norm"], name=n("d_qk_norm_rope"),
        heads=dm.heads, kr_seg=dm.r_kr // HEAD_PAD, after=after_attention)
    ds["q_norm"] = _head_unpad(dgq, 1)
    ds["k_norm"] = _head_unpad(dgk, 1)
    dkv_v = _mm(dv, W["wv"], name=n("d_up_v"), tb=True)
    dq_lat_n = _mm(dq_raw, W["wuq"], name=n("d_up_q"), tb=True, after=hook("after_qk", dq_raw, dw))
    dw["wuq"] = _mm(S["q_lat"], dq_raw, name=n("dw_up_q"), ta=True, tm=512)
    dkv_lat_n = _mm(dk_nope, W["wkn"], name=n("d_up_k"), tb=True, add=dkv_v)
    dw["wkn"] = _mm(S["kv_lat"], dk_nope, name=n("dw_up_k"), ta=True, tm=512)
    dw["wv"] = _mm(S["kv_lat"], dv, name=n("dw_up_v"), ta=True, tm=512)
    dq_lat, ds["q_lat_norm"] = _rms_bwd(dq_lat_n, S["rest"], W["q_lat_norm"], name=n("d_q_lat_norm"), width=Q_LORA,
                                        seg=dm.r_q // Q_LORA, out_dtype=BF16)
    dkv_lat, ds["kv_lat_norm"] = _rms_bwd(dkv_lat_n, S["rest"], W["kv_lat_norm"], name=n("d_kv_lat_norm"), width=KV_LORA,
                                          seg=dm.r_kv // KV_LORA, out_dtype=BF16)
    du, db, dc, ds["conv_w"] = _conv_bwd(S["rest"], W["conv_w"], dy_a, name=n("d_conv"), dc=dm.dc)
    dpool, dw["wp"], ds["pool_scale"] = _pool_bwd(S["rest"], W["wp"], W["pool_scale"], dy_c, name=n("d_pool"),
                                                  seg0=dm.r_pool // dm.pg, pg=dm.pg)
    drest = jnp.concatenate([du, db, dc, dpool, dq_lat, dkv_lat, dk_rope], axis=1)
    dw["wg_t"] = _mm(dgl, S["h"], name=n("dw_proj_gates"), ta=True, tm=512, out_dtype=BF16, pieces=2)
    dw["wr_t"] = _mm(drest, S["h"], name=n("dw_proj_rest"), ta=True, tm=512, out_dtype=BF16, pieces=2)
    dh_g = _mm(dgl, W["wg_t"], name=n("d_proj_gates"), tm=704, tk=3072, after=hook("after_dw_in", dw["wr_t"], dw))
    dh = _mm(drest, W["wr_t"], name=n("d_proj_rest"), add=dh_g, tm=704, tk=2688, after=hook("after_dh_gates", dh_g, dw))
    dx, dx_b, ds["attn_norm"] = _rms_bwd(dh, S["x"], W["attn_norm"], name=n("d_attn_norm"), res=dx1, bf16_copy=True)
    return dx, dx_b, dw, ds


BIG = ("w_in", "w_uq", "w_ukv", "pool_w", "w_branch_a", "w_branch_b", "w_branch_c", "w_o", "w_up", "w_down")
REPLICATED = ("attn_norm", "q_lat_norm", "kv_lat_norm", "q_norm", "k_norm", "pool_scale", "mlp_norm")
WEIGHTS = ("meta_tokens", "attn_norm", "w_in", "conv_w", "q_lat_norm", "kv_lat_norm", "w_uq", "w_ukv", "q_norm",
           "k_norm", "pool_w", "pool_scale", "w_branch_a", "w_branch_b", "w_branch_c", "w_o", "mlp_norm", "w_up",
           "w_down")


def _pack(arrays):
    flat = jnp.concatenate([a.reshape(-1).astype(F32) for a in arrays])
    pad = (-flat.shape[0]) % (8 * LANES)
    return jnp.pad(flat, (0, pad)).reshape(-1, LANES)


def _unpack(flat, shapes):
    out, pos = [], 0
    flat = flat.reshape(-1)
    for shp in shapes:
        size = math.prod(shp)
        out.append(flat[pos:pos + size].reshape(shp))
        pos += size
    return out


def _update(w, g, m, v, name):
    shp = w.shape
    to2 = lambda a: a.reshape(-1, shp[-1])
    delta, nm, nv = _adamw(to2(w), to2(g), to2(m), to2(v), name=name)
    return delta.reshape(shp), nm.reshape(shp), nv.reshape(shp)


def _step(args):
    x = args["x"][0]
    seq, d = x.shape
    dm = _Dims(d, seq)
    xi, yi, ci = _coords()
    chip = 2 * xi + yi

    small_w = _gather_all(_pack([args["conv_w"], args["meta_tokens"]]), name="gather_small_weights")
    args = dict(args)
    for p in ("", "m_", "v_"):
        args[p + "w_in"] = jnp.swapaxes(args[p + "w_in"], 1, 2)
    order = [(k, l) for l in range(2) for k in BIG]
    shards = {n: args[n[0]][n[1]].astype(BF16) for n in order}
    for l in range(2):
        shards[("w_in", l)] = _split_cols(shards[("w_in", l)])
    small_w, shards[order[0]] = lax.optimization_barrier((small_w, shards[order[0]]))
    lands = {n: lax.empty((4,) + shards[n].shape, BF16) for n in order}
    last = ("w_up", "w_down")
    group_names = [[("w_in", 0)], [(k, 0) for k in BIG[1:] if k not in last], [(k, 0) for k in last],
                   [(k, 1) for k in BIG]]
    first, others = order[0], order[1:]
    sems, thru, token = _start_copies([shards[first], lands[first]], [_ici_gather_plan([(0, 1)])],
                                      name="start_gather_ici_first")
    shards[first], lands[first] = thru
    at = {n: i for i, n in enumerate(others)}
    sems_b, thru, token_b = _start_copies(
        [shards[n] for n in others] + [lands[n] for n in others] + [token],
        [_ici_gather_plan([(at[n], len(others) + at[n]) for n in g]) for g in group_names[1:]], name="start_gather_ici")
    sems = sems + sems_b
    for i, n in enumerate(others):
        shards[n], lands[n] = thru[i], thru[len(others) + i]

    def finish_gather(g, after, tag):
        names = group_names[g]
        k = len(names)
        plan, _ = _ici_gather_plan([(i, k + i) for i in range(k)])
        got = _wait_copies([shards[n] for n in names] + [lands[n] for n in names], sems[g], plan, after,
                           name=f"wait_gather_ici_{tag}")
        for i, n in enumerate(names):
            shards[n] = got[i]
        fwd = _d2d_forward_plan(list(range(k)))
        sems2, bufs2, tok2 = _start_copies(got[k:], [fwd], name=f"start_gather_d2d_{tag}")
        return names, bufs2, sems2[0], fwd[0], tok2

    def land_gather(pending, after, tag):
        names, bufs2, sems2, plan, tok2 = pending
        done = _wait_copies(bufs2, sems2, plan, tok2 if after is None else after, name=f"wait_gather_d2d_{tag}")
        return {n[0]: buf for n, buf in zip(names, done)}

    conv_shape, meta_shape = args["conv_w"].shape, args["meta_tokens"].shape
    per_chip = [_unpack(small_w[2 * j], [conv_shape, meta_shape]) for j in range(4)]
    conv_full = jnp.concatenate([p[0] for p in per_chip], axis=-1)
    meta_full = jnp.concatenate([p[1] for p in per_chip], axis=-1)

    layers = []
    for l in range(2):
        small = {k: args[k][l] for k in REPLICATED}
        small["conv_w"] = conv_full[l]
        layers.append(_small_weights(small))

    pos = jnp.arange(dm.t, dtype=F32)
    inv = ROPE_THETA ** (-jnp.arange(0, QK_ROPE, 2, dtype=F32) / QK_ROPE)
    ang = pos[:, None] * inv[None, :]
    cos_t = _rope_pad(jnp.cos(ang), jnp.cos(ang))
    sin_t = _rope_pad(-jnp.sin(ang), jnp.sin(ang))
    tail = jnp.zeros((dm.t - dm.t_real, d), F32)
    h0 = jnp.concatenate([meta_full, x, tail], axis=0)
    target = jnp.concatenate([jnp.zeros((N_META, d), F32), args["loss_target"][0], tail], axis=0)

    h_first = _rms_fwd(h0, layers[0]["attn_norm"], name="attn_norm_l0", after=(token, token_b))
    layers[0].update(_in_weights(dm, land_gather(finish_gather(0, h_first, "l0_in"), None, "l0_in")["w_in"]))
    pending = {}

    def rest_of_layer0(point, after):
        if point == "after_proj":
            return _other_weights(dm, land_gather(finish_gather(1, after, "l0_mid"), None, "l0_mid"))
        if point == "after_attention":
            pending["mlp"] = finish_gather(2, after, "l0_mlp")
            return {"pin": (pending["mlp"][4],)}
        return _other_weights(dm, land_gather(pending["mlp"], after, "l0_mlp"))

    h1, saved0 = _layer_fwd(dm, layers[0], h0, cos_t, sin_t, "l0", more=rest_of_layer0, h=h_first)
    g1 = land_gather(finish_gather(3, saved0["y_b"], "l1"), h1, "l1")
    layers[1].update(_in_weights(dm, g1["w_in"]))
    layers[1].update(_other_weights(dm, g1))
    h2, saved1 = _layer_fwd(dm, layers[1], h1, cos_t, sin_t, "l1")
    sq, dy, dy_b = _loss(h2, target, name="loss_head", first=N_META, last=dm.t_real)
    loss = lax.psum(0.5 / d * sq[0, 0], ("x", "y", "c"))
    core, chip_flags = _one_hot(ci, 2), _one_hot(chip, 4)

    class Reduce:
        def __init__(self, names, dw, tag):
            self.names, self.tag, self.nb = names, tag, len(names)
            self.idx = [(i, self.nb + i) for i in range(self.nb)]
            parts = [_grad_piece(dm, dw, k) for k in names]
            parts = [p if k == "w_in" else _as3d(p) for p, k in zip(parts, names)]
            recv = [lax.empty((4,) + p.shape[2:] if k == "w_in" else (4, p.shape[1] // 2, p.shape[2]), BF16)
                    for p, k in zip(parts, names)]
            self.plan = _swap_half_plan(self.idx)
            self.sems, self.bufs, self.token = _start_copies(parts + recv, [self.plan], name=f"start_swap_{tag}")

        def _land(self, after, what):
            return _wait_copies(self.bufs, self.sems[0], self.plan[0], self.token if after is None else after,
                                name=f"wait_{what}_{self.tag}")

        def scatter(self, after=None):
            got = self._land(after, "swap")
            pairs = [_pair_sum(got[i], got[j], core, name=f"pair_sum_{k}_{self.tag}")
                     for (i, j), k in zip(self.idx, self.names)]
            self.plan = _scatter_plan(self.idx)
            self.sems, self.bufs, self.token = _start_copies(pairs + [lax.empty(p.shape, BF16) for p in pairs],
                                                             [self.plan], name=f"start_scatter_{self.tag}")
            return self.token

        def totals(self, after=None):
            got = self._land(after, "scatter")
            sums = [_chip_sum(got[i], got[j], chip_flags, name=f"chip_sum_{k}_{self.tag}")
                    for (i, j), k in zip(self.idx, self.names)]
            self.plan = _swap_total_plan(self.idx)
            self.sems, self.bufs, self.token = _start_copies(sums + [lax.empty(t.shape, F32) for t in sums],
                                                             [self.plan], name=f"start_swap_total_{self.tag}")
            return self.token

        def finish(self, after=None):
            got = self._land(after, "swap_total")
            return {k: (got[i], got[j]) for (i, j), k in zip(self.idx, self.names)}

    dh1, dh1_b, dw1, ds1 = _layer_bwd(dm, layers[1], saved1, dy, dy_b, cos_t, sin_t, "l1",
                               hook=lambda point, t, dw: (loss.reshape(1, 1),) if point == "start" else ())
    early = ("w_down", "w_up", "w_o", "w_branch_a", "w_branch_b", "w_branch_c")
    late = tuple(k for k in BIG if k not in early)
    stage = {}

    def during_layer0(point, t, dw):
        if point == "start":
            stage["l1"] = Reduce(BIG, dw1, "l1")
            return (stage["l1"].token,)
        if point == "after_mlp":
            return (stage["l1"].scatter(after=t),)
        if point == "before_attention":
            stage["l0a"] = Reduce(early, dw, "l0a")
            return (stage["l0a"].token,)
        if point == "after_attention":
            return (stage["l1"].totals(after=t), stage["l0a"].scatter(after=t))
        if point == "after_qk":
            stage["red1"] = stage["l1"].finish(after=t)
            return ()
        if point == "after_dw_in":
            tok = stage["l0a"].totals(after=t)
            stage["l0b"] = Reduce(late, dw, "l0b")
            return (tok, stage["l0b"].token)
        return (stage["l0b"].scatter(after=t),)

    dh0, _, dw0, ds0 = _layer_bwd(dm, layers[0], saved0, dh1, dh1_b, cos_t, sin_t, "l0", hook=during_layer0)
    grad_x = dh0[N_META:dm.t_real][None]
    red1 = stage["red1"]
    grads, delta, new_m, new_v = {}, {}, {}, {}

    def adamw_big(k, layer, red, prev, after):
        shp = args[k].shape
        wmv = [args[p + k].reshape(2, -1, shp[-1]) for p in ("", "m_", "v_")]
        return _adamw_layer(*wmv, *red[k], core, layer, prev, name=f"adamw_{k}_l{layer}", col_halves=k == "w_in",
                            after=after)

    def keep(k, out):
        shp = args[k].shape
        out = [o.reshape(shp) for o in out]
        grads[k], delta[k], new_m[k], new_v[k] = [jnp.swapaxes(o, 1, 2) for o in out] if k == "w_in" else out

    half_done = {}
    pin = dh0
    for k in BIG:
        half_done[k] = adamw_big(k, 1, red1, None, (pin,))
        pin = half_done[k][0]
    red0a = stage["l0a"].finish(after=pin)
    for k in early:
        out = adamw_big(k, 0, red0a, half_done[k], ())
        keep(k, out)
        pin = out[0]

    small_names = REPLICATED + ("conv_w",)
    small_parts = [jnp.stack([ds0[k].reshape(ds0[k].shape[-2:] if k == "conv_w" else (-1,)),
                              ds1[k].reshape(ds1[k].shape[-2:] if k == "conv_w" else (-1,))]) for k in small_names]
    small_parts.append(dh0[:N_META])
    small_all = _gather_all(_pack(small_parts), name="gather_small_grads", after=(pin,))
    small_sum = _sum_stack(small_all, name="sum_small_grads", out_dtype=F32)
    small_g = dict(zip(small_names + ("meta_tokens",), _unpack(small_sum, [p.shape for p in small_parts])))
    for k in REPLICATED:
        grads[k] = small_g[k]
    dcw = conv_shape[-1]
    grads["conv_w"] = lax.dynamic_slice_in_dim(small_g["conv_w"], chip * dcw, dcw, axis=2)
    dmeta = meta_shape[-1]
    grads["meta_tokens"] = lax.dynamic_slice_in_dim(small_g["meta_tokens"], chip * dmeta, dmeta, axis=1)

    stage["l0b"].totals(after=small_sum)
    red0b = stage["l0b"].finish()
    for k in late:
        keep(k, adamw_big(k, 0, red0b, half_done[k], ()))
    for k in WEIGHTS:
        if k not in BIG:
            grads[k] = grads[k].reshape(args[k].shape)
            delta[k], new_m[k], new_v[k] = _update(args[k], grads[k], args["m_" + k], args["v_" + k], f"adamw_{k}")
    return (loss, grad_x, *[grads[k] for k in WEIGHTS], *[delta[k] for k in WEIGHTS],
            *[new_m[k] for k in WEIGHTS], *[new_v[k] for k in WEIGHTS])


def kernel(x, meta_tokens, attn_norm, w_in, conv_w, q_lat_norm, kv_lat_norm, w_uq, w_ukv, q_norm, k_norm, pool_w, pool_scale, w_branch_a, w_branch_b, w_branch_c, w_o, mlp_norm, w_up, w_down, loss_target, m_meta_tokens, m_attn_norm, m_w_in, m_conv_w, m_q_lat_norm, m_kv_lat_norm, m_w_uq, m_w_ukv, m_q_norm, m_k_norm, m_pool_w, m_pool_scale, m_w_branch_a, m_w_branch_b, m_w_branch_c, m_w_o, m_mlp_norm, m_w_up, m_w_down, v_meta_tokens, v_attn_norm, v_w_in, v_conv_w, v_q_lat_norm, v_kv_lat_norm, v_w_uq, v_w_ukv, v_q_norm, v_k_norm, v_pool_w, v_pool_scale, v_w_branch_a, v_w_branch_b, v_w_branch_c, v_w_o, v_mlp_norm, v_w_up, v_w_down):
    return _step(dict(locals()))
```

```python
import functools
import math

import jax
import jax.numpy as jnp
from jax import lax
from jax.experimental import pallas as pl
from jax.experimental.pallas import tpu as pltpu

F32 = jnp.float32
BF16 = jnp.bfloat16
MESH = pl.DeviceIdType.MESH

EPS = 1e-6
N_META = 16
QK_NOPE = 128
QK_ROPE = 64
QK_HEAD = QK_NOPE + QK_ROPE
V_HEAD = 128
HEAD_PAD = 256
Q_LORA = 512
KV_LORA = 512
ROPE_THETA = 10000.0
POOL_WINDOWS = (2, 4, 8, 16)
HALO = 16
LANES = 128
ADAM_LR = 0.001
ADAM_B1 = 0.9
ADAM_B2 = 0.999
ADAM_EPS = 1e-08
ADAM_WD = 0.01
ADAM_STEP = 10
VMEM_LIMIT = 52 * 1024 * 1024
NEG = -1e30
ATTN_SCALE = QK_HEAD ** -0.5
LOG2_E = 1.4426950408889634
Q_FOLD = ATTN_SCALE * LOG2_E


def _tile(n, target, mult=LANES):
    best = None
    for t in range(mult, min(n, target) + 1, mult):
        if n % t == 0:
            best = t
    return n if best is None else best


def _params(sem=None):
    return pltpu.CompilerParams(dimension_semantics=sem, vmem_limit_bytes=VMEM_LIMIT)


def _mm(a, b, *, name, ta=False, tb=False, add=None, aux=None, epi=None, out_dtype=F32,
        tm=1056, tn=1024, tk=None, after=(), pieces=None):
    if ta:
        K, M = a.shape
    else:
        M, K = a.shape
    if tb:
        N, kb = b.shape
    else:
        kb, N = b.shape
    assert K == kb, (a.shape, b.shape, ta, tb)
    tm = _tile(M, tm, LANES if ta else 16)
    tn = _tile(N if pieces is None else N // pieces, tn, LANES)
    tk = K if tk is None else _tile(K, tk, LANES if (not ta or tb) else 16)
    nk = K // tk
    a_bytes, b_bytes = a.size * a.dtype.itemsize, b.size * b.dtype.itemsize
    j_outer = nk == 1 and a_bytes * (N // tn) + b_bytes < a_bytes + b_bytes * (M // tm)
    grid = (N // tn, M // tm, nk) if j_outer else (M // tm, N // tn, nk)
    row = (lambda g0, g1: g1) if j_outer else (lambda g0, g1: g0)
    col = (lambda g0, g1: g0) if j_outer else (lambda g0, g1: g1)

    if ta:
        a_spec = pl.BlockSpec((tk, tm), lambda g0, g1, k: (k, row(g0, g1)))
    else:
        a_spec = pl.BlockSpec((tm, tk), lambda g0, g1, k: (row(g0, g1), k))
    if tb:
        b_spec = pl.BlockSpec((tn, tk), lambda g0, g1, k: (col(g0, g1), k))
    else:
        b_spec = pl.BlockSpec((tk, tn), lambda g0, g1, k: (k, col(g0, g1)))
    o_spec = pl.BlockSpec((tm, tn), lambda g0, g1, k: (row(g0, g1), col(g0, g1)))
    per = None if pieces is None else N // pieces // tn
    in_specs = [a_spec, b_spec]
    operands = [a, b]
    if add is not None:
        in_specs.append(o_spec)
        operands.append(add)
    if aux is not None:
        in_specs.append(o_spec)
        operands.append(aux)
    after = tuple(after)
    in_specs += [pl.BlockSpec(memory_space=pl.ANY)] * len(after)
    operands += list(after)
    if epi == "relu2":
        out_shape = (jax.ShapeDtypeStruct((M, N), BF16), jax.ShapeDtypeStruct((M, N), BF16))
        out_specs = (o_spec, o_spec)
    elif pieces is not None:
        out_shape = jax.ShapeDtypeStruct((pieces, M, N // pieces), out_dtype)
        out_specs = pl.BlockSpec((1, tm, tn), lambda g0, g1, k: (col(g0, g1) // per, row(g0, g1), col(g0, g1) % per))
    else:
        out_shape = jax.ShapeDtypeStruct((M, N), out_dtype)
        out_specs = o_spec
    dims =(((0 if ta else 1,), (1 if tb else 0,)), ((), ()))
    has_add, has_aux = add is not None, aux is not None

    def body(*refs):
        a_ref, b_ref = refs[0], refs[1]
        pos = 2
        add_ref = aux_ref = None
        if has_add:
            add_ref = refs[pos]
            pos += 1
        if has_aux:
            aux_ref = refs[pos]
            pos += 1
        pos += len(after)
        n_out = 2 if epi == "relu2" else 1
        out_refs = refs[pos:pos + n_out]
        acc_ref = refs[pos + n_out] if nk > 1 else None

        part = lax.dot_general(a_ref[...].astype(BF16), b_ref[...].astype(BF16), dims,
                               preferred_element_type=F32)

        def finish(acc):
            if has_add:
                acc = acc + add_ref[...].astype(F32)
            if epi == "relu2":
                r = jnp.maximum(acc, 0.0)
                out_refs[0][...] = acc.astype(BF16)
                out_refs[1][...] = (r * r).astype(BF16)
            elif epi == "drelu2":
                u = aux_ref[...].astype(F32)
                out_refs[0][...] = (acc * (2.0 * jnp.maximum(u, 0.0))).astype(out_dtype)
            else:
                out_refs[0][...] = acc.astype(out_dtype).reshape(out_refs[0].shape)

        if nk == 1:
            finish(part)
        else:
            k = pl.program_id(2)

            @pl.when(k == 0)
            def _():
                acc_ref[...] = part

            @pl.when(k > 0)
            def _():
                acc_ref[...] += part

            @pl.when(k == nk - 1)
            def _():
                finish(acc_ref[...])

    scratch = [pltpu.VMEM((tm, tn), F32)] if nk > 1 else []
    return pl.pallas_call(
        body, name=name, grid=grid, in_specs=in_specs, out_specs=out_specs, out_shape=out_shape,
        scratch_shapes=scratch, compiler_params=_params(("parallel", "parallel", "arbitrary")),
    )(*operands)


def _rms_fwd(x, g, *, name, width=None, seg=0, tm=384, after=()):
    T = x.shape[0]
    width = x.shape[1] if width is None else width
    tm = _tile(T, tm, 16)
    after = tuple(after)

    def body(x_ref, g_ref, *rest):
        xf = x_ref[...].astype(F32)
        r = lax.rsqrt(jnp.mean(xf * xf, axis=-1, keepdims=True) + EPS)
        rest[-1][...] = (xf * r * g_ref[...]).astype(BF16)

    return pl.pallas_call(
        body, name=name, grid=(T // tm,),
        in_specs=[pl.BlockSpec((tm, width), lambda i: (i, seg)), pl.BlockSpec((1, width), lambda i: (0, 0))]
        + [pl.BlockSpec(memory_space=pl.ANY)] * len(after),
        out_specs=pl.BlockSpec((tm, width), lambda i: (i, 0)),
        out_shape=jax.ShapeDtypeStruct((T, width), BF16),
        compiler_params=_params(("parallel",)),
    )(x, g, *after)


def _rms_bwd(dy, x, g, *, name, width=None, seg=0, res=None, out_dtype=F32, tm=384, bf16_copy=False):
    T = x.shape[0]
    width = x.shape[1] if width is None else width
    tm = _tile(T, tm, 16)
    has_res = res is not None

    def body(*refs):
        dy_ref, x_ref, g_ref = refs[:3]
        res_ref = refs[3] if has_res else None
        dx_ref, dg_ref = refs[4 if has_res else 3], refs[-1]
        xf = x_ref[...].astype(F32)
        dyf = dy_ref[...].astype(F32)
        r = lax.rsqrt(jnp.mean(xf * xf, axis=-1, keepdims=True) + EPS)
        xhat = xf * r
        dyh = dyf * g_ref[...]
        dx = r * (dyh - xhat * jnp.mean(dyh * xhat, axis=-1, keepdims=True))
        if has_res:
            dx = dx + res_ref[...].astype(F32)
        dx_ref[...] = dx.astype(out_dtype)
        if bf16_copy:
            refs[-2][...] = dx.astype(BF16)
        part = jnp.sum(dyf * xhat, axis=0, keepdims=True)

        @pl.when(pl.program_id(0) == 0)
        def _():
            dg_ref[...] = part

        @pl.when(pl.program_id(0) > 0)
        def _():
            dg_ref[...] += part

    row = pl.BlockSpec((tm, width), lambda i: (i, 0))
    in_specs = [row, pl.BlockSpec((tm, width), lambda i: (i, seg)), pl.BlockSpec((1, width), lambda i: (0, 0))]
    operands = [dy, x, g]
    if has_res:
        in_specs.append(row)
        operands.append(res)
    vec = pl.BlockSpec((1, width), lambda i: (0, 0))
    full = [jax.ShapeDtypeStruct((T, width), out_dtype)] + ([jax.ShapeDtypeStruct((T, width), BF16)] if bf16_copy else [])
    return pl.pallas_call(
        body, name=name, grid=(T // tm,), in_specs=in_specs,
        out_specs=tuple([row] * len(full) + [vec]),
        out_shape=tuple(full + [jax.ShapeDtypeStruct((1, width), F32)]),
        compiler_params=_params(("arbitrary",)),
    )(*operands)


def _down(ext, k):
    return pltpu.roll(ext, k, 0)


def _up(ext, k):
    return pltpu.roll(ext, ext.shape[0] - k, 0)


def _pre_halo(ref, r, R):
    start = pl.multiple_of(jnp.maximum(r * R - HALO, 0), 8)
    keep = (r > 0).astype(F32)
    return ref[pl.ds(start, HALO), :].astype(F32) * keep


def _post_halo(ref, r, R, n_chunks):
    start = pl.multiple_of(jnp.minimum(r * R + R, (n_chunks - 1) * R + R - HALO), 8)
    keep = (r < n_chunks - 1).astype(F32)
    return ref[pl.ds(start, HALO), :].astype(F32) * keep


def _chunk(ref, r, R):
    return ref[pl.ds(pl.multiple_of(r * R, 8), R), :].astype(F32)


def _conv_fwd(rest, conv_w, *, name, dc, tc=128, rows=1056):
    T = rest.shape[0]
    tc = _tile(dc, tc)
    nb = dc // tc
    R = _tile(T, rows, 16)
    n_chunks = T // R

    def body(u_ref, b_ref, c_ref, w_ref, y_ref):
        w0, w1, w2 = w_ref[0:1, :], w_ref[1:2, :], w_ref[2:3, :]

        def chunk(r, carry):
            cu = _chunk(c_ref, r, R) * _chunk(u_ref, r, R)
            ext = jnp.concatenate([_pre_halo(c_ref, r, R) * _pre_halo(u_ref, r, R), cu], axis=0)
            conv = w0 * _down(ext, 2)[HALO:] + w1 * _down(ext, 1)[HALO:] + w2 * cu
            y_ref[pl.ds(pl.multiple_of(r * R, 8), R), :] = (_chunk(b_ref, r, R) * conv).astype(BF16)
            return carry

        lax.fori_loop(0, n_chunks, chunk, 0)

    col = lambda off: pl.BlockSpec((T, tc), lambda j: (0, off * nb + j))
    return pl.pallas_call(
        body, name=name, grid=(nb,),
        in_specs=[col(0), col(1), col(2), pl.BlockSpec((3, tc), lambda j: (0, j))],
        out_specs=pl.BlockSpec((T, tc), lambda j: (0, j)),
        out_shape=jax.ShapeDtypeStruct((T, dc), BF16),
        compiler_params=_params(("parallel",)),
    )(rest, rest, rest, conv_w)


def _conv_bwd(rest, conv_w, dy, *, name, dc, tc=128, rows=1056):
    T = rest.shape[0]
    tc = _tile(dc, tc)
    nb = dc // tc
    R = _tile(T, rows, 16)
    n_chunks = T // R

    def body(u_ref, b_ref, c_ref, w_ref, dy_ref, du_ref, db_ref, dc_ref, dw_ref):
        w0, w1, w2 = w_ref[0:1, :], w_ref[1:2, :], w_ref[2:3, :]

        def chunk(r, carry):
            a0, a1, a2 = carry
            u, b, c = _chunk(u_ref, r, R), _chunk(b_ref, r, R), _chunk(c_ref, r, R)
            dy_c = _chunk(dy_ref, r, R)
            cu = c * u
            ext = jnp.concatenate([_pre_halo(c_ref, r, R) * _pre_halo(u_ref, r, R), cu], axis=0)
            cu1, cu2 = _down(ext, 1)[HALO:], _down(ext, 2)[HALO:]
            conv = w0 * cu2 + w1 * cu1 + w2 * cu
            dconv = dy_c * b
            dext = jnp.concatenate(
                [dconv, _post_halo(dy_ref, r, R, n_chunks) * _post_halo(b_ref, r, R, n_chunks)], axis=0)
            dcu = w2 * dconv + w1 * _up(dext, 1)[:R] + w0 * _up(dext, 2)[:R]
            rows_at = pl.ds(pl.multiple_of(r * R, 8), R)
            db_ref[rows_at, :] = (dy_c * conv).astype(BF16)
            du_ref[rows_at, :] = (dcu * c).astype(BF16)
            dc_ref[rows_at, :] = (dcu * u).astype(BF16)
            return (a0 + jnp.sum(dconv * cu2, axis=0, keepdims=True),
                    a1 + jnp.sum(dconv * cu1, axis=0, keepdims=True),
                    a2 + jnp.sum(dconv * cu, axis=0, keepdims=True))

        zero = jnp.zeros((1, tc), F32)
        a0, a1, a2 = lax.fori_loop(0, n_chunks, chunk, (zero, zero, zero))
        dw_ref[0:1, :] = a0
        dw_ref[1:2, :] = a1
        dw_ref[2:3, :] = a2

    col = lambda off: pl.BlockSpec((T, tc), lambda j: (0, off * nb + j))
    own = pl.BlockSpec((T, tc), lambda j: (0, j))
    return pl.pallas_call(
        body, name=name, grid=(nb,),
        in_specs=[col(0), col(1), col(2), pl.BlockSpec((3, tc), lambda j: (0, j)), own],
        out_specs=(own, own, own, pl.BlockSpec((3, tc), lambda j: (0, j))),
        out_shape=(jax.ShapeDtypeStruct((T, dc), BF16),) * 3 + (jax.ShapeDtypeStruct((3, dc), F32),),
        compiler_params=_params(("parallel",)),
    )(rest, rest, rest, conv_w, dy)


def _window_count(r, R, n_rows, w, first_row_offset):
    t = lax.broadcasted_iota(jnp.int32, (n_rows, 1), 0) + (r * R + first_row_offset)
    return jnp.minimum(t + 1, w).astype(F32)


def _pool_fwd(rest, pool_w, pool_scale, *, name, seg0, pg, rows=1056):
    T = rest.shape[0]
    R = _tile(T, rows, 16)
    n_chunks = T // R
    n_groups = len(POOL_WINDOWS)

    def body(x_ref, w_ref, s_ref, y_ref):
        def run(window):
            def chunk(r, carry):
                g = _chunk(x_ref, r, R)
                s = jnp.concatenate([_pre_halo(x_ref, r, R), g], axis=0)
                k = 1
                while k < window:
                    s = s + _down(s, k)
                    k *= 2
                pooled = s[HALO:] / _window_count(r, R, R, window, 0) - g
                mixed = jnp.dot(pooled.astype(BF16), w_ref[0], preferred_element_type=F32)
                y_ref[pl.ds(pl.multiple_of(r * R, 8), R), :] = (mixed * s_ref[...]).astype(BF16)
                return carry

            lax.fori_loop(0, n_chunks, chunk, 0)

        for gi, window in enumerate(POOL_WINDOWS):
            pl.when(pl.program_id(0) == gi)(functools.partial(run, window))

    return pl.pallas_call(
        body, name=name, grid=(n_groups,),
        in_specs=[pl.BlockSpec((T, pg), lambda g: (0, seg0 + g)),
                  pl.BlockSpec((1, pg, pg), lambda g: (g, 0, 0)),
                  pl.BlockSpec((1, pg), lambda g: (0, g))],
        out_specs=pl.BlockSpec((T, pg), lambda g: (0, g)),
        out_shape=jax.ShapeDtypeStruct((T, n_groups * pg), BF16),
        compiler_params=_params(("parallel",)),
    )(rest, pool_w, pool_scale)


def _pool_bwd(rest, pool_w, pool_scale, dy, *, name, seg0, pg, rows=1056):
    T = rest.shape[0]
    R = _tile(T, rows, 16)
    n_chunks = T // R
    n_groups = len(POOL_WINDOWS)

    def body(x_ref, w_ref, s_ref, dy_ref, dx_ref, dw_ref, ds_ref):
        def run(window):
            def chunk(r, carry):
                dw_acc, ds_acc = carry
                g = _chunk(x_ref, r, R)
                s = jnp.concatenate([_pre_halo(x_ref, r, R), g], axis=0)
                k = 1
                while k < window:
                    s = s + _down(s, k)
                    k *= 2
                pooled = (s[HALO:] / _window_count(r, R, R, window, 0) - g).astype(BF16)
                mixed = jnp.dot(pooled, w_ref[0], preferred_element_type=F32)
                dy_c = _chunk(dy_ref, r, R)
                dm_ext = (jnp.concatenate([dy_c, _post_halo(dy_ref, r, R, n_chunks)], axis=0)
                          * s_ref[...]).astype(BF16)
                dpool_ext = lax.dot_general(dm_ext, w_ref[0], (((1,), (1,)), ((), ())),
                                            preferred_element_type=F32)
                a = dpool_ext / _window_count(r, R, R + HALO, window, 0)
                k = 1
                while k < window:
                    a = a + _up(a, k)
                    k *= 2
                dx_ref[pl.ds(pl.multiple_of(r * R, 8), R), :] = (a[:R] - dpool_ext[:R]).astype(BF16)
                dw_acc = dw_acc + lax.dot_general(pooled, dm_ext[:R], (((0,), (0,)), ((), ())),
                                                  preferred_element_type=F32)
                ds_acc = ds_acc + jnp.sum(dy_c * mixed, axis=0, keepdims=True)
                return dw_acc, ds_acc

            dw_acc, ds_acc = lax.fori_loop(0, n_chunks, chunk,
                                           (jnp.zeros((pg, pg), F32), jnp.zeros((1, pg), F32)))
            dw_ref[0] = dw_acc
            ds_ref[...] = ds_acc

        for gi, window in enumerate(POOL_WINDOWS):
            pl.when(pl.program_id(0) == gi)(functools.partial(run, window))

    own = pl.BlockSpec((T, pg), lambda g: (0, g))
    return pl.pallas_call(
        body, name=name, grid=(n_groups,),
        in_specs=[pl.BlockSpec((T, pg), lambda g: (0, seg0 + g)),
                  pl.BlockSpec((1, pg, pg), lambda g: (g, 0, 0)),
                  pl.BlockSpec((1, pg), lambda g: (0, g)), own],
        out_specs=(own, pl.BlockSpec((1, pg, pg), lambda g: (g, 0, 0)), pl.BlockSpec((1, pg), lambda g: (0, g))),
        out_shape=(jax.ShapeDtypeStruct((T, n_groups * pg), BF16),
                   jax.ShapeDtypeStruct((n_groups, pg, pg), F32),
                   jax.ShapeDtypeStruct((1, n_groups * pg), F32)),
        compiler_params=_params(("parallel",)),
    )(rest, pool_w, pool_scale, dy)


def _rope(r, cos_t, sin_t):
    return r * cos_t + pltpu.roll(r, LANES // 2, 1) * sin_t


def _rope_t(d, cos_t, sin_t):
    return d * cos_t + pltpu.roll(d * sin_t, LANES // 2, 1)


def _qk_fwd(q_raw, k_nope, rest, cos_t, sin_t, q_norm, k_norm, *, name, heads, kr_seg, tm=192):
    T = q_raw.shape[0]
    tm = _tile(T, tm, 16)

    def body(q_ref, kn_ref, kr_ref, c_ref, s_ref, gq_ref, gk_ref, qo_ref, ko_ref):
        cos_b, sin_b = c_ref[...], s_ref[...]
        kr = kr_ref[:, 0:LANES]
        kr_ss = jnp.sum(kr * kr, axis=-1, keepdims=True)
        gq, gk = gq_ref[...], gk_ref[...]
        for h in range(heads):
            lo = h * HEAD_PAD
            q = q_ref[:, lo:lo + HEAD_PAD].astype(F32)
            rq = lax.rsqrt(jnp.sum(q * q, axis=-1, keepdims=True) / QK_HEAD + EPS)
            qn = q * (rq * Q_FOLD) * gq
            qo_ref[:, lo:lo + LANES] = qn[:, :LANES].astype(BF16)
            qo_ref[:, lo + LANES:lo + HEAD_PAD] = _rope(qn[:, LANES:], cos_b, sin_b).astype(BF16)
            kn = kn_ref[:, h * LANES:(h + 1) * LANES].astype(F32)
            rk = lax.rsqrt((jnp.sum(kn * kn, axis=-1, keepdims=True) + kr_ss) / QK_HEAD + EPS)
            ko_ref[:, lo:lo + LANES] = (kn * rk * gk[:, :LANES]).astype(BF16)
            ko_ref[:, lo + LANES:lo + HEAD_PAD] = _rope(kr * rk * gk[:, LANES:], cos_b, sin_b).astype(BF16)

    wq, wk = heads * HEAD_PAD, heads * LANES
    return pl.pallas_call(
        body, name=name, grid=(T // tm,),
        in_specs=[pl.BlockSpec((tm, wq), lambda i: (i, 0)), pl.BlockSpec((tm, wk), lambda i: (i, 0)),
                  pl.BlockSpec((tm, HEAD_PAD), lambda i: (i, kr_seg)),
                  pl.BlockSpec((tm, LANES), lambda i: (i, 0)), pl.BlockSpec((tm, LANES), lambda i: (i, 0)),
                  pl.BlockSpec((1, HEAD_PAD), lambda i: (0, 0)), pl.BlockSpec((1, HEAD_PAD), lambda i: (0, 0))],
        out_specs=(pl.BlockSpec((tm, wq), lambda i: (i, 0)), pl.BlockSpec((tm, wq), lambda i: (i, 0))),
        out_shape=(jax.ShapeDtypeStruct((T, wq), BF16), jax.ShapeDtypeStruct((T, wq), BF16)),
        compiler_params=_params(("parallel",)),
    )(q_raw, k_nope, rest, cos_t, sin_t, q_norm, k_norm)


def _qk_bwd(dq, dk, q_raw, k_nope, rest, cos_t, sin_t, q_norm, k_norm, *, name, heads, kr_seg, tm=128, after=()):
    T = q_raw.shape[0]
    tm = _tile(T, tm, 16)
    after = tuple(after)

    def body(dq_ref, dk_ref, q_ref, kn_ref, kr_ref, c_ref, s_ref, gq_ref, gk_ref, *rest_refs):
        dqr_ref, dkn_ref, dkr_ref, dgq_ref, dgk_ref = rest_refs[len(after):]
        cos_b, sin_b = c_ref[...], s_ref[...]
        kr = kr_ref[:, 0:LANES]
        kr_ss = jnp.sum(kr * kr, axis=-1, keepdims=True)
        gq, gk = gq_ref[...], gk_ref[...]
        dgq = jnp.zeros((1, HEAD_PAD), F32)
        dgk_n = jnp.zeros((1, LANES), F32)
        dgk_r = jnp.zeros((1, LANES), F32)
        dkr = jnp.zeros((tm, LANES), F32)
        for h in range(heads):
            lo = h * HEAD_PAD
            q = q_ref[:, lo:lo + HEAD_PAD].astype(F32)
            rq = lax.rsqrt(jnp.sum(q * q, axis=-1, keepdims=True) / QK_HEAD + EPS)
            qhat = q * rq
            dqn = jnp.concatenate([dq_ref[:, lo:lo + LANES],
                                   _rope_t(dq_ref[:, lo + LANES:lo + HEAD_PAD], cos_b, sin_b)], axis=1) * ATTN_SCALE
            dgq = dgq + jnp.sum(dqn * qhat, axis=0, keepdims=True)
            dqh = dqn * gq
            dqr_ref[:, lo:lo + HEAD_PAD] = (
                rq * (dqh - qhat * (jnp.sum(dqh * qhat, axis=-1, keepdims=True) / QK_HEAD))).astype(BF16)
            kn = kn_ref[:, h * LANES:(h + 1) * LANES].astype(F32)
            rk = lax.rsqrt((jnp.sum(kn * kn, axis=-1, keepdims=True) + kr_ss) / QK_HEAD + EPS)
            khat_n, khat_r = kn * rk, kr * rk
            dkn_n = dk_ref[:, lo:lo + LANES] * (1.0 / LOG2_E)
            dkn_r = _rope_t(dk_ref[:, lo + LANES:lo + HEAD_PAD], cos_b, sin_b) * (1.0 / LOG2_E)
            dgk_n = dgk_n + jnp.sum(dkn_n * khat_n, axis=0, keepdims=True)
            dgk_r = dgk_r + jnp.sum(dkn_r * khat_r, axis=0, keepdims=True)
            dkh_n, dkh_r = dkn_n * gk[:, :LANES], dkn_r * gk[:, LANES:]
            proj = (jnp.sum(dkh_n * khat_n, axis=-1, keepdims=True)
                    + jnp.sum(dkh_r * khat_r, axis=-1, keepdims=True)) / QK_HEAD
            dkn_ref[:, h * LANES:(h + 1) * LANES] = (rk * (dkh_n - khat_n * proj)).astype(BF16)
            dkr = dkr + rk * (dkh_r - khat_r * proj)
        dkr_ref[:, 0:LANES] = dkr.astype(BF16)
        dkr_ref[:, LANES:HEAD_PAD] = jnp.zeros((tm, HEAD_PAD - LANES), BF16)
        dgk = jnp.concatenate([dgk_n, dgk_r], axis=1)

        @pl.when(pl.program_id(0) == 0)
        def _():
            dgq_ref[...] = dgq
            dgk_ref[...] = dgk

        @pl.when(pl.program_id(0) > 0)
        def _():
            dgq_ref[...] += dgq
            dgk_ref[...] += dgk

    wq, wk = heads * HEAD_PAD, heads * LANES
    row = lambda w: pl.BlockSpec((tm, w), lambda i: (i, 0))
    vec = pl.BlockSpec((1, HEAD_PAD), lambda i: (0, 0))
    return pl.pallas_call(
        body, name=name, grid=(T // tm,),
        in_specs=[row(wq), row(wq), row(wq), row(wk), pl.BlockSpec((tm, HEAD_PAD), lambda i: (i, kr_seg)),
                  row(LANES), row(LANES), vec, vec] + [pl.BlockSpec(memory_space=pl.ANY)] * len(after),
        out_specs=(row(wq), row(wk), row(HEAD_PAD), vec, vec),
        out_shape=(jax.ShapeDtypeStruct((T, wq), BF16), jax.ShapeDtypeStruct((T, wk), BF16),
                   jax.ShapeDtypeStruct((T, HEAD_PAD), BF16),
                   jax.ShapeDtypeStruct((1, HEAD_PAD), F32), jax.ShapeDtypeStruct((1, HEAD_PAD), F32)),
        compiler_params=_params(("arbitrary",)),
    )(dq, dk, q_raw, k_nope, rest, cos_t, sin_t, q_norm, k_norm, *after)


def _causal_mask(s):
    row = lax.broadcasted_iota(jnp.int32, s.shape, 0)
    col = lax.broadcasted_iota(jnp.int32, s.shape, 1)
    return jnp.where(row >= col, s, NEG)


def _flash_fwd(q, k, v, *, name, heads, tq=384, hp=2, parts=2):
    T = q.shape[0]
    tq = _tile(T, tq, LANES)
    nq = T // tq
    tr = tq // parts
    nt = (((1,), (1,)), ((), ()))
    chains = [(h, r) for h in range(hp) for r in range(parts)]

    def body(q_ref, k_ref, v_ref, o_ref, lse_ref, acc_ref):
        def q_block(i, carry):
            rows_at = [pl.ds(pl.multiple_of(i * tq + r * tr, tr), tr) for r in range(parts)]
            qbs = [q_ref[rows_at[r], h * HEAD_PAD:(h + 1) * HEAD_PAD] for h, r in chains]
            for c in range(len(chains)):
                acc_ref[c] = jnp.zeros((tr, V_HEAD), F32)

            def step(j, state, masked):
                k_at = pl.ds(pl.multiple_of(j * tq, tq), tq)
                new = []
                scores = [lax.dot_general(qb, k_ref[k_at, h * HEAD_PAD:(h + 1) * HEAD_PAD], nt,
                                          preferred_element_type=F32) for qb, (h, r) in zip(qbs, chains)]
                for c, (s, (h, r)) in enumerate(zip(scores, chains)):
                    m, l = state[c]
                    if masked:
                        row = lax.broadcasted_iota(jnp.int32, s.shape, 0) + r * tr
                        s = jnp.where(row >= lax.broadcasted_iota(jnp.int32, s.shape, 1), s, NEG)
                    m_new = jnp.maximum(m, jnp.max(s, axis=-1, keepdims=True))
                    p = jnp.exp2(s - m_new)
                    alpha = jnp.exp2(m - m_new)
                    new.append((m_new, alpha * l + jnp.sum(p, axis=-1, keepdims=True)))
                    acc_ref[c] = alpha * acc_ref[c] + jnp.dot(p.astype(BF16), v_ref[k_at, h * V_HEAD:(h + 1) * V_HEAD],
                                                              preferred_element_type=F32)
                return tuple(new)

            init = tuple((jnp.full((tr, 1), NEG, F32), jnp.zeros((tr, 1), F32)) for _ in chains)
            state = lax.fori_loop(0, i, lambda j, st: step(j, st, False), init)
            state = step(i, state, True)
            for c, ((m, l), (h, r)) in enumerate(zip(state, chains)):
                o_ref[rows_at[r], h * V_HEAD:(h + 1) * V_HEAD] = (acc_ref[c] / l).astype(BF16)
                lse_ref[h, rows_at[r], :] = jnp.broadcast_to(m + jnp.log2(l), (tr, LANES))
            return carry

        lax.fori_loop(0, nq, q_block, 0)

    qk_spec = pl.BlockSpec((T, hp * HEAD_PAD), lambda g: (0, g))
    v_spec = pl.BlockSpec((T, hp * V_HEAD), lambda g: (0, g))
    return pl.pallas_call(
        body, name=name, grid=(heads // hp,), in_specs=[qk_spec, qk_spec, v_spec],
        out_specs=(v_spec, pl.BlockSpec((hp, T, LANES), lambda g: (g, 0, 0))),
        out_shape=(jax.ShapeDtypeStruct((T, heads * V_HEAD), BF16), jax.ShapeDtypeStruct((heads, T, LANES), F32)),
        scratch_shapes=[pltpu.VMEM((len(chains), tr, V_HEAD), F32)],
        compiler_params=_params(("parallel",)),
    )(q, k, v)


def _flash_bwd(q, k, v, o, do, lse, *, name, heads, tq=384, after=()):
    T = q.shape[0]
    tq = _tile(T, tq, LANES)
    nq = T // tq
    nt = (((1,), (1,)), ((), ()))
    tn = (((0,), (0,)), ((), ()))

    after = tuple(after)

    def body(q_ref, k_ref, v_ref, o_ref, do_ref, lse_ref, *rest):
        dq_ref, dk_ref, dv_ref, delta_ref, dv_acc_ref = rest[len(after):]
        def fill_delta(i, carry):
            at = pl.ds(pl.multiple_of(i * tq, tq), tq)
            d = jnp.sum(o_ref[at, :].astype(F32) * do_ref[at, :].astype(F32), axis=-1, keepdims=True)
            delta_ref[at, :] = jnp.broadcast_to(d, (tq, LANES))
            dq_ref[at, :] = jnp.zeros((tq, HEAD_PAD), F32)
            return carry

        lax.fori_loop(0, nq, fill_delta, 0)

        def kv_block(j, carry):
            k_at = pl.ds(pl.multiple_of(j * tq, tq), tq)
            kb, vb = k_ref[k_at, :], v_ref[k_at, :]

            def steps(blocks, masked):
                at = [pl.ds(pl.multiple_of(i * tq, tq), tq) for i in blocks]
                qbs = [q_ref[a, :] for a in at]
                dobs = [do_ref[a, :] for a in at]
                scores = [lax.dot_general(qb, kb, nt, preferred_element_type=F32) for qb in qbs]
                dps = [lax.dot_general(dob, vb, nt, preferred_element_type=F32) for dob in dobs]
                for a, qb, dob, sc, dp in zip(at, qbs, dobs, scores, dps):
                    if masked:
                        sc = _causal_mask(sc)
                    p = jnp.exp2(sc - lse_ref[0, a, :][:, 0:1])
                    ds = (p * (dp - delta_ref[a, :][:, 0:1])).astype(BF16)
                    dv_part = lax.dot_general(p.astype(BF16), dob, tn, preferred_element_type=F32)
                    dk_part = lax.dot_general(ds, qb, tn, preferred_element_type=F32)
                    if masked:
                        dv_acc_ref[...] = dv_part
                        dk_ref[k_at, :] = dk_part
                    else:
                        dv_acc_ref[...] += dv_part
                        dk_ref[k_at, :] += dk_part
                    dq_ref[a, :] += jnp.dot(ds, kb, preferred_element_type=F32)

            def two_blocks(t, carry):
                steps([j + 1 + 2 * t, j + 2 + 2 * t], False)
                return carry

            steps([j], True)
            rest = nq - 1 - j
            lax.fori_loop(0, rest // 2, two_blocks, 0)

            @pl.when(rest % 2 == 1)
            def _():
                steps([nq - 1], False)

            dv_ref[k_at, :] = dv_acc_ref[...].astype(BF16)
            return carry

        lax.fori_loop(0, nq, kv_block, 0)

    qk_spec = pl.BlockSpec((T, HEAD_PAD), lambda h: (0, h))
    v_spec = pl.BlockSpec((T, V_HEAD), lambda h: (0, h))
    return pl.pallas_call(
        body, name=name, grid=(heads,),
        in_specs=[qk_spec, qk_spec, v_spec, v_spec, v_spec, pl.BlockSpec((1, T, LANES), lambda h: (h, 0, 0))]
        + [pl.BlockSpec(memory_space=pl.ANY)] * len(after),
        out_specs=(qk_spec, qk_spec, v_spec),
        out_shape=(jax.ShapeDtypeStruct((T, heads * HEAD_PAD), F32), jax.ShapeDtypeStruct((T, heads * HEAD_PAD), F32),
                   jax.ShapeDtypeStruct((T, heads * V_HEAD), BF16)),
        scratch_shapes=[pltpu.VMEM((T, LANES), F32), pltpu.VMEM((tq, V_HEAD), F32)],
        compiler_params=_params(("parallel",)),
    )(q, k, v, o, do, lse, *after)


def _merge_fwd(gl, pa, pb, pc, *, name, d, tm=384, tn=1024):
    T = pa.shape[0]
    tm, tn = _tile(T, tm, 16), _tile(d, tn)
    nb = d // tn

    def body(g0, g1, g2, a, b, c, o_ref):
        f = lambda ref: ref[...].astype(F32)
        o_ref[...] = (jax.nn.sigmoid(f(g0)) * f(a) + jax.nn.sigmoid(f(g1)) * f(b)
                      + jax.nn.sigmoid(f(g2)) * f(c)).astype(BF16)

    gate = lambda n: pl.BlockSpec((tm, tn), lambda i, j: (i, n * nb + j))
    blk = pl.BlockSpec((tm, tn), lambda i, j: (i, j))
    return pl.pallas_call(
        body, name=name, grid=(T // tm, nb), in_specs=[gate(0), gate(1), gate(2), blk, blk, blk],
        out_specs=blk, out_shape=jax.ShapeDtypeStruct((T, d), BF16),
        compiler_params=_params(("parallel", "parallel")),
    )(gl, gl, gl, pa, pb, pc)


def _merge_bwd(dm, gl, pa, pb, pc, *, name, d, tm=384, tn=1024):
    T = pa.shape[0]
    tm, tn = _tile(T, tm, 16), _tile(d, tn)
    nb = d // tn

    def body(dm_ref, g0, g1, g2, a, b, c, da, db, dc, dg0, dg1, dg2):
        dmv = dm_ref[...].astype(F32)
        for g_ref, p_ref, dp_ref, dg_ref in ((g0, a, da, dg0), (g1, b, db, dg1), (g2, c, dc, dg2)):
            sg = jax.nn.sigmoid(g_ref[...].astype(F32))
            dp_ref[...] = (dmv * sg).astype(BF16)
            dg_ref[...] = (dmv * p_ref[...].astype(F32) * sg * (1.0 - sg)).astype(BF16)

    gate = lambda n: pl.BlockSpec((tm, tn), lambda i, j: (i, n * nb + j))
    blk = pl.BlockSpec((tm, tn), lambda i, j: (i, j))
    return pl.pallas_call(
        body, name=name, grid=(T // tm, nb), in_specs=[blk, gate(0), gate(1), gate(2), blk, blk, blk],
        out_specs=(blk,) * 6, out_shape=(jax.ShapeDtypeStruct((T, d), BF16),) * 6,
        compiler_params=_params(("parallel", "parallel")),
    )(dm, gl, gl, gl, pa, pb, pc)


def _loss(y, target, *, name, first, last, tm=384):
    T, d = y.shape
    tm = _tile(T, tm, 16)

    def body(y_ref, t_ref, loss_ref, dy_ref, dyb_ref):
        i = pl.program_id(0)
        row = lax.broadcasted_iota(jnp.int32, (tm, 1), 0) + i * tm
        real = jnp.logical_and(row >= first, row < last)
        err = jnp.where(real, y_ref[...] - t_ref[...], 0.0)
        dy_ref[...] = err * (1.0 / d)
        dyb_ref[...] = (err * (1.0 / d)).astype(BF16)
        part = jnp.broadcast_to(jnp.sum(err * err, keepdims=True).reshape(1, 1), (1, LANES))

        @pl.when(i == 0)
        def _():
            loss_ref[...] = part

        @pl.when(i > 0)
        def _():
            loss_ref[...] += part

    blk = pl.BlockSpec((tm, d), lambda i: (i, 0))
    return pl.pallas_call(
        body, name=name, grid=(T // tm,), in_specs=[blk, blk],
        out_specs=(pl.BlockSpec((1, LANES), lambda i: (0, 0)), blk, blk),
        out_shape=(jax.ShapeDtypeStruct((1, LANES), F32), jax.ShapeDtypeStruct((T, d), F32),
                   jax.ShapeDtypeStruct((T, d), BF16)),
        compiler_params=_params(("arbitrary",)),
    )(y, target)


def _as3d(a):
    return a.reshape(a.shape[0], -1, a.shape[-1])


def _sum_stack(parts, *, name, out_dtype, rows=256):
    n, R, C = parts.shape
    tr = _tile(R, rows, 16)

    def body(p_ref, o_ref):
        acc = p_ref[0].astype(F32)
        for s in range(1, n):
            acc = acc + p_ref[s].astype(F32)
        o_ref[...] = acc.astype(out_dtype)

    return pl.pallas_call(
        body, name=name, grid=(R // tr,),
        in_specs=[pl.BlockSpec((n, tr, C), lambda i: (0, i, 0))],
        out_specs=pl.BlockSpec((tr, C), lambda i: (i, 0)),
        out_shape=jax.ShapeDtypeStruct((R, C), out_dtype),
        compiler_params=_params(("parallel",)),
    )(parts)


def _adamw(w, g, m, v, *, name, rows=128):
    R, C = w.shape
    tr = _tile(R, rows, 8)
    c1 = 1.0 - ADAM_B1 ** ADAM_STEP
    c2 = 1.0 - ADAM_B2 ** ADAM_STEP

    def body(w_ref, g_ref, m_ref, v_ref, d_ref, nm_ref, nv_ref):
        gv = g_ref[...]
        nm = ADAM_B1 * m_ref[...] + (1.0 - ADAM_B1) * gv
        nv = ADAM_B2 * v_ref[...] + (1.0 - ADAM_B2) * (gv * gv)
        nm_ref[...] = nm
        nv_ref[...] = nv
        d_ref[...] = -ADAM_LR * ((nm / c1) / (jnp.sqrt(nv / c2) + ADAM_EPS) + ADAM_WD * w_ref[...])

    blk = pl.BlockSpec((tr, C), lambda i: (i, 0))
    return pl.pallas_call(
        body, name=name, grid=(R // tr,), in_specs=[blk] * 4, out_specs=(blk,) * 3,
        out_shape=(jax.ShapeDtypeStruct((R, C), F32),) * 3,
        compiler_params=_params(("parallel",)),
    )(w, g, m, v)


def _one_hot(index, n):
    return jnp.broadcast_to((jnp.arange(n) == index).astype(F32)[:, None, None], (n, 8, LANES))


def _is_set(flags_ref, s):
    return flags_ref[s, 0:1, 0:1] > 0.5


def _rows_for(h, width, itemsize, n_stacked, budget, mult):
    return _tile(h, max(mult, budget // (n_stacked * width * itemsize)), mult)


def _pair_sum(pieces, recv, core, *, name):
    _, H, C = recv.shape
    tr = _rows_for(H, C, 2, 1, 2 << 20, 16)
    nh = H // tr
    halves_lead = pieces.ndim == 4

    def body(lo_ref, hi_ref, r_ref, core_ref, o_ref):
        lo, hi = (lo_ref[0, 0], hi_ref[0, 0]) if halves_lead else (lo_ref[0], hi_ref[0])
        mine = jnp.where(_is_set(core_ref, 0), lo, hi)
        o_ref[0] = (mine.astype(F32) + r_ref[0].astype(F32)).astype(BF16)

    blk = pl.BlockSpec((1, tr, C), lambda j, i: (j, i, 0))
    if halves_lead:
        lo_spec = pl.BlockSpec((1, 1, tr, C), lambda j, i: (0, j, i, 0))
        hi_spec = pl.BlockSpec((1, 1, tr, C), lambda j, i: (1, j, i, 0))
    else:
        lo_spec, hi_spec = blk, pl.BlockSpec((1, tr, C), lambda j, i: (j, nh + i, 0))
    return pl.pallas_call(
        body, name=name, grid=(4, nh),
        in_specs=[lo_spec, hi_spec, blk, pl.BlockSpec((2, 8, LANES), lambda j, i: (0, 0, 0))],
        out_specs=blk, out_shape=jax.ShapeDtypeStruct((4, H, C), BF16),
        compiler_params=_params(("parallel", "parallel")),
    )(pieces, pieces, recv, core)


def _chip_sum(pair, landed, chip_flags, *, name):
    _, H, C = pair.shape
    tr = _rows_for(H, C, 2, 4, 8 << 20, 16)

    def body(p_ref, l_ref, chip_ref, o_ref):
        acc = None
        for s in range(4):
            part = jnp.where(_is_set(chip_ref, s), p_ref[s], l_ref[s]).astype(F32)
            acc = part if acc is None else acc + part
        o_ref[...] = acc

    blk = pl.BlockSpec((4, tr, C), lambda i: (0, i, 0))
    return pl.pallas_call(
        body, name=name, grid=(H // tr,),
        in_specs=[blk, blk, pl.BlockSpec((4, 8, LANES), lambda i: (0, 0, 0))],
        out_specs=pl.BlockSpec((tr, C), lambda i: (i, 0)), out_shape=jax.ShapeDtypeStruct((H, C), F32),
        compiler_params=_params(("parallel",)),
    )(pair, landed, chip_flags)


def _adamw_layer(w, m, v, total, recv, core, layer, prev, *, name, col_halves=False, after=()):
    _, R, C = w.shape
    H, wd = total.shape
    tr = _rows_for(H, wd, 4, 1, 2 << 20, 8)
    nh = H // tr
    c1 = 1.0 - ADAM_B1 ** ADAM_STEP
    c2 = 1.0 - ADAM_B2 ** ADAM_STEP
    n_prev = 0 if prev is None else 4
    after = tuple(after)

    def body(*refs):
        w_ref, m_ref, v_ref, t_ref, r_ref, core_ref = refs[:6]
        g_ref, d_ref, nm_ref, nv_ref = refs[6 + n_prev + len(after):]
        half_is_mine = jnp.where(pl.program_id(0) == 0, core_ref[0, 0:1, 0:1], core_ref[1, 0:1, 0:1]) > 0.5
        gv = jnp.where(half_is_mine, t_ref[...], r_ref[...])
        nm = ADAM_B1 * m_ref[0] + (1.0 - ADAM_B1) * gv
        nv = ADAM_B2 * v_ref[0] + (1.0 - ADAM_B2) * (gv * gv)
        g_ref[0] = gv
        nm_ref[0] = nm
        nv_ref[0] = nv
        d_ref[0] = -ADAM_LR * ((nm / c1) / (jnp.sqrt(nv / c2) + ADAM_EPS) + ADAM_WD * w_ref[0])

    if col_halves:
        lay = pl.BlockSpec((1, tr, wd), lambda hf, i: (layer, i, hf))
    else:
        lay = pl.BlockSpec((1, tr, wd), lambda hf, i: (layer, hf * nh + i, 0))
    one = pl.BlockSpec((tr, wd), lambda hf, i: (i, 0))
    operands = [w, m, v, total, recv, core] + ([] if prev is None else list(prev)) + list(after)
    return pl.pallas_call(
        body, name=name, grid=(2, nh),
        in_specs=[lay, lay, lay, one, one, pl.BlockSpec((2, 8, LANES), lambda hf, i: (0, 0, 0))]
        + [ANY] * (n_prev + len(after)),
        out_specs=(lay,) * 4, out_shape=(jax.ShapeDtypeStruct((2, R, C), F32),) * 4,
        input_output_aliases={6 + i: i for i in range(n_prev)},
        compiler_params=_params(("parallel", "parallel")),
    )(*operands)


ANY = pl.BlockSpec(memory_space=pl.ANY)


def _coords():
    return lax.axis_index("x"), lax.axis_index("y"), lax.axis_index("c")


HBM = pl.BlockSpec(memory_space=pltpu.HBM)
SEM = pl.BlockSpec(memory_space=pltpu.SEMAPHORE)
EFFECT = pltpu.SideEffectType.DATAFLOW_SIDE_EFFECTING


def _copies(plan, bufs, send_sems, recv_sems):
    return [pltpu.make_async_remote_copy(src_ref=s, dst_ref=d, send_sem=send_sems.at[i], recv_sem=recv_sems.at[i],
                                         device_id=to, device_id_type=MESH)
            for i, (s, d, to) in enumerate(plan(bufs))]


def _start_copies(bufs, groups, *, name):
    nb, ng = len(bufs), len(groups)

    def body(*refs):
        buf_refs = refs[:nb]
        sems = refs[nb:nb + 2 * ng]
        token = refs[-1]
        for g, (plan, _) in enumerate(groups):
            for cp in _copies(plan, buf_refs, sems[2 * g], sems[2 * g + 1]):
                cp.start()
        token[...] = jnp.zeros_like(token)

    sem_shapes = []
    for _, n in groups:
        sem_shapes += [pltpu.SemaphoreType.DMA((n,)), pltpu.SemaphoreType.DMA((n,))]
    out = pl.pallas_call(
        body, name=name, in_specs=[HBM] * nb,
        out_specs=tuple([SEM] * (2 * ng) + [HBM] * nb + [pl.BlockSpec(memory_space=pltpu.VMEM)]),
        out_shape=tuple(sem_shapes + [pltpu.HBM(b.shape, b.dtype) for b in bufs] + [jax.ShapeDtypeStruct((8, LANES), F32)]),
        input_output_aliases={i: 2 * ng + i for i in range(nb)},
        compiler_params=pltpu.CompilerParams(has_side_effects=EFFECT),
    )(*[pltpu.with_memory_space_constraint(b, pltpu.HBM) for b in bufs])
    sems = [(out[2 * g], out[2 * g + 1]) for g in range(ng)]
    return sems, list(out[2 * ng:2 * ng + nb]), out[-1]


def _wait_copies(bufs, sems, plan, after, *, name):
    nb = len(bufs)

    def body(*refs):
        buf_refs = refs[:nb]
        for cp in _copies(plan, buf_refs, refs[nb], refs[nb + 1]):
            cp.wait_send()
            cp.wait_recv()

    out = pl.pallas_call(
        body, name=name, in_specs=[HBM] * nb + [SEM, SEM, ANY], out_specs=tuple([HBM] * nb),
        out_shape=tuple(pltpu.HBM(b.shape, b.dtype) for b in bufs),
        input_output_aliases={i: i for i in range(nb)},
        compiler_params=pltpu.CompilerParams(has_side_effects=EFFECT),
    )(*bufs, sems[0], sems[1], after)
    return list(out)


def _half(ref, c):
    h = ref.shape[0] // 2
    return ref.at[pl.ds(c * h, h)]


def _ici_gather_plan(pairs):
    def plan(refs):
        x, y, c = _coords()
        me = 2 * x + y
        out = []
        for s, d in pairs:
            for cx, cy in [(1 - x, y), (x, 1 - y), (1 - x, 1 - y)]:
                out.append((_half(refs[s], c), _half(refs[d].at[me], c), (cx, cy, c)))
            out.append((refs[s], refs[d].at[me], (x, y, 1 - c)))
        return out
    return plan, 4 * len(pairs)


def _d2d_forward_plan(lands):
    def plan(refs):
        x, y, c = _coords()
        out = []
        for d in lands:
            for cx, cy in [(1 - x, y), (x, 1 - y), (1 - x, 1 - y)]:
                got = _half(refs[d].at[2 * cx + cy], c)
                out.append((got, got, (x, y, 1 - c)))
        return out
    return plan, 3 * len(lands)


def _swap_half_plan(pairs):
    def plan(refs):
        x, y, c = _coords()
        out = []
        for s, d in pairs:
            h = refs[d].shape[1]
            other = refs[s].at[1 - c] if len(refs[s].shape) == 4 else refs[s].at[:, pl.ds((1 - c) * h, h)]
            out.append((other, refs[d], (x, y, 1 - c)))
        return out
    return plan, len(pairs)


def _scatter_plan(pairs):
    def plan(refs):
        x, y, c = _coords()
        me = 2 * x + y
        out = []
        for s, d in pairs:
            for cx, cy in [(1 - x, y), (x, 1 - y), (1 - x, 1 - y)]:
                out.append((refs[s].at[2 * cx + cy], refs[d].at[me], (cx, cy, c)))
        return out
    return plan, 3 * len(pairs)


def _swap_total_plan(pairs):
    def plan(refs):
        x, y, c = _coords()
        return [(refs[s], refs[d], (x, y, 1 - c)) for s, d in pairs]
    return plan, len(pairs)


def _gather_all(block, *, name, after=()):
    after = tuple(after)

    def body(src, *rest):
        out, send_sems, recv_sems, local_sem = rest[len(after):]
        x, y, c = _coords()
        me = 4 * x + 2 * y + c
        flips = [(fx, fy, fc) for fx in (0, 1) for fy in (0, 1) for fc in (0, 1)][1:]
        mine = pltpu.make_async_copy(src, out.at[me], local_sem)
        mine.start()
        peers = [(x ^ fx, y ^ fy, c ^ fc) for fx, fy, fc in flips]
        cps = [pltpu.make_async_remote_copy(src_ref=src, dst_ref=out.at[me], send_sem=send_sems.at[k],
                                            recv_sem=recv_sems.at[k], device_id=peer, device_id_type=MESH)
               for k, peer in enumerate(peers)]
        for cp in cps:
            cp.start()
        for k, (px, py, pc) in enumerate(peers):
            slot = out.at[4 * px + 2 * py + pc]
            pltpu.make_async_remote_copy(src_ref=slot, dst_ref=slot, send_sem=send_sems.at[k], recv_sem=recv_sems.at[k],
                                         device_id=(px, py, pc), device_id_type=MESH).wait_recv()
        for cp in cps:
            cp.wait_send()
        mine.wait()

    return pl.pallas_call(
        body, name=name, in_specs=[ANY] * (1 + len(after)), out_specs=ANY,
        out_shape=jax.ShapeDtypeStruct((8,) + block.shape, block.dtype),
        scratch_shapes=[pltpu.SemaphoreType.DMA((7,)), pltpu.SemaphoreType.DMA((7,)), pltpu.SemaphoreType.DMA],
    )(block, *after)


def _cols(o):
    return jnp.transpose(o, (1, 0, 2)).reshape(o.shape[1], -1)


def _uncols(full):
    return jnp.transpose(full.reshape(full.shape[0], 4, -1), (1, 0, 2))


def _rope_pad(x1, x2):
    z = jnp.zeros_like(x1)
    return jnp.concatenate([x1, z, x2, z], axis=-1)


def _head_pad(w, heads):
    r = w.reshape(w.shape[0], heads, QK_HEAD)
    half = QK_ROPE // 2
    out = jnp.concatenate([r[..., :QK_NOPE], _rope_pad(r[..., QK_NOPE:QK_NOPE + half], r[..., QK_NOPE + half:])], axis=-1)
    return out.reshape(w.shape[0], heads * HEAD_PAD)


def _head_unpad(w, heads):
    r = w.reshape(w.shape[0], heads, HEAD_PAD)
    half = QK_ROPE // 2
    out = jnp.concatenate([r[..., :QK_NOPE], r[..., QK_NOPE:QK_NOPE + half],
                           r[..., QK_NOPE + 2 * half:QK_NOPE + 3 * half]], axis=-1)
    return out.reshape(w.shape[0], heads * QK_HEAD)


class _Dims:
    def __init__(self, d, seq):
        self.d = d
        self.seq = seq
        self.t_real = N_META + seq
        self.t = -(-self.t_real // LANES) * LANES
        self.dc = d // 2
        self.dp = d // 2
        self.pg = self.dp // len(POOL_WINDOWS)
        self.heads = d // 128
        self.dff = 4 * d
        self.a_end = 3 * self.dc
        self.q_end = self.a_end + Q_LORA
        self.kv_end = self.q_end + KV_LORA
        self.kr_end = self.kv_end + QK_ROPE
        self.pool_end = self.kr_end + self.dp
        self.d_in = self.pool_end + 3 * d
        self.r_pool = 3 * self.dc
        self.r_q = self.r_pool + self.dp
        self.r_kv = self.r_q + Q_LORA
        self.r_kr = self.r_kv + KV_LORA
        self.r_width = self.r_kr + HEAD_PAD


def _split_cols(a):
    return jnp.moveaxis(a.reshape(a.shape[:-1] + (2, a.shape[-1] // 2)), -2, -3)


def _join_cols(a):
    a = jnp.moveaxis(a, -3, -2)
    return a.reshape(a.shape[:-2] + (a.shape[-2] * a.shape[-1],))


def _in_weights(dm, pieces):
    w_t = _join_cols(pieces).reshape(dm.d_in, dm.d)
    half = QK_ROPE // 2
    kr = w_t[dm.kv_end:dm.kr_end]
    zeros = jnp.zeros((half, dm.d), BF16)
    kr_p = jnp.concatenate([kr[:half], zeros, kr[half:], zeros, jnp.zeros((HEAD_PAD - LANES, dm.d), BF16)], axis=0)
    return dict(
        wg_t=w_t[dm.pool_end:],
        wr_t=jnp.concatenate([w_t[:dm.a_end], w_t[dm.kr_end:dm.pool_end], w_t[dm.a_end:dm.kv_end], kr_p], axis=0))


def _other_weights(dm, g):
    out = {}
    if "w_ukv" in g:
        w_ukv = _cols(g["w_ukv"]).reshape(KV_LORA, dm.heads, QK_NOPE + V_HEAD)
        out["wkn"] = w_ukv[:, :, :QK_NOPE].reshape(KV_LORA, dm.heads * QK_NOPE)
        out["wv"] = w_ukv[:, :, QK_NOPE:].reshape(KV_LORA, dm.heads * V_HEAD)
    if "w_uq" in g:
        out["wuq"] = _head_pad(_cols(g["w_uq"]), dm.heads)
    if "pool_w" in g:
        out["wp"] = jnp.transpose(g["pool_w"], (1, 0, 2, 3)).reshape(len(POOL_WINDOWS), dm.pg, dm.pg)
    for name, key in (("w_branch_a", "wba"), ("w_branch_c", "wbc"), ("w_up", "wup")):
        if name in g:
            out[key] = _cols(g[name])
    for name, key in (("w_branch_b", "wbb"), ("w_o", "wo"), ("w_down", "wdn")):
        if name in g:
            out[key] = g[name].reshape(-1, dm.d)
    return out


def _small_weights(small):
    return dict(
        conv_w=small["conv_w"],
        attn_norm=small["attn_norm"][None], mlp_norm=small["mlp_norm"][None],
        q_lat_norm=small["q_lat_norm"][None], kv_lat_norm=small["kv_lat_norm"][None],
        q_norm=_head_pad(small["q_norm"][None], 1), k_norm=_head_pad(small["k_norm"][None], 1),
        pool_scale=small["pool_scale"][None],
    )


def _grad_piece(dm, dw, name):
    half = QK_ROPE // 2
    rows = lambda a: a.reshape((4, a.shape[0] // 4) + a.shape[1:])
    if name == "w_in":
        dwr, dwg = dw["wr_t"], dw["wg_t"]
        d_t = jnp.concatenate([
            dwr[:, :dm.r_pool], dwr[:, dm.r_q:dm.r_kr], dwr[:, dm.r_kr:dm.r_kr + half],
            dwr[:, dm.r_kr + 2 * half:dm.r_kr + 3 * half], dwr[:, dm.r_pool:dm.r_q], dwg], axis=1)
        out = d_t.reshape(2, 4, d_t.shape[1] // 4, d_t.shape[2])
    elif name == "w_ukv":
        out = _uncols(jnp.concatenate([dw["wkn"].reshape(KV_LORA, dm.heads, QK_NOPE),
                                       dw["wv"].reshape(KV_LORA, dm.heads, V_HEAD)], axis=-1).reshape(KV_LORA, -1))
    elif name == "w_uq":
        out = _uncols(_head_unpad(dw["wuq"], dm.heads))
    elif name == "pool_w":
        out = jnp.transpose(dw["wp"].reshape(len(POOL_WINDOWS), 4, dm.pg // 4, dm.pg), (1, 0, 2, 3))
    elif name in ("w_branch_a", "w_branch_c", "w_up"):
        out = dw[{"w_branch_a": "wba", "w_branch_c": "wbc", "w_up": "wup"}[name]]
    else:
        out = rows(dw[{"w_branch_b": "wbb", "w_o": "wo", "w_down": "wdn"}[name]])
    return out.astype(BF16)


def _layer_fwd(dm, W, x, cos_t, sin_t, tag, more=None, h=None):
    n = lambda s: f"{s}_{tag}"
    if h is None:
        h = _rms_fwd(x, W["attn_norm"], name=n("attn_norm"))
    gl = _mm(h, W["wg_t"], name=n("proj_gates"), tb=True, out_dtype=BF16)
    rest = _mm(h, W["wr_t"], name=n("proj_rest"), tb=True)
    if more is not None:
        W.update(more("after_proj", rest))
    y_a = _conv_fwd(rest, W["conv_w"], name=n("conv"), dc=dm.dc)
    y_c = _pool_fwd(rest, W["wp"], W["pool_scale"], name=n("pool"), seg0=dm.r_pool // dm.pg, pg=dm.pg)
    q_lat = _rms_fwd(rest, W["q_lat_norm"], name=n("q_lat_norm"), width=Q_LORA, seg=dm.r_q // Q_LORA)
    kv_lat = _rms_fwd(rest, W["kv_lat_norm"], name=n("kv_lat_norm"), width=KV_LORA, seg=dm.r_kv // KV_LORA)
    q_raw = _mm(q_lat, W["wuq"], name=n("up_q"), out_dtype=BF16)
    k_nope = _mm(kv_lat, W["wkn"], name=n("up_k"), out_dtype=BF16)
    v = _mm(kv_lat, W["wv"], name=n("up_v"), out_dtype=BF16)
    q, k = _qk_fwd(q_raw, k_nope, rest, cos_t, sin_t, W["q_norm"], W["k_norm"], name=n("qk_norm_rope"),
                   heads=dm.heads, kr_seg=dm.r_kr // HEAD_PAD)
    y_b, lse = _flash_fwd(q, k, v, name=n("attention"), heads=dm.heads)
    if more is not None:
        W.update(more("after_attention", y_b))
    pa = _mm(y_a, W["wba"], name=n("branch_a"), out_dtype=BF16)
    pb = _mm(y_b, W["wbb"], name=n("branch_b"), out_dtype=BF16, after=W.pop("pin", ()))
    pc = _mm(y_c, W["wbc"], name=n("branch_c"), out_dtype=BF16)
    merged = _merge_fwd(gl, pa, pb, pc, name=n("merge"), d=dm.d)
    x1 = _mm(merged, W["wo"], name=n("out_proj"), add=x)
    h2 = _rms_fwd(x1, W["mlp_norm"], name=n("mlp_norm"))
    if more is not None:
        W.update(more("before_mlp", h2))
    up, act = _mm(h2, W["wup"], name=n("mlp_up"), epi="relu2")
    x2 = _mm(act, W["wdn"], name=n("mlp_down"), add=x1, tm=704, tk=4096)
    saved = dict(x=x, h=h, gl=gl, rest=rest, y_a=y_a, y_c=y_c, q_lat=q_lat, kv_lat=kv_lat, q_raw=q_raw, k_nope=k_nope,
                 v=v, q=q, k=k, y_b=y_b, lse=lse, pa=pa, pb=pb, pc=pc, merged=merged, x1=x1, h2=h2, up=up, act=act)
    return x2, saved


def _layer_bwd(dm, W, S, dx2, dx2_b, cos_t, sin_t, tag, hook=None):
    n = lambda s: f"{s}_{tag}"
    dw, ds = {}, {}
    if hook is None:
        hook = lambda point, t, dw_so_far: ()
    dup = _mm(dx2_b, W["wdn"], name=n("d_mlp_down"), tb=True, aux=S["up"], epi="drelu2", out_dtype=BF16,
              after=hook("start", dx2, dw))
    dw["wdn"] = _mm(S["act"], dx2_b, name=n("dw_mlp_down"), ta=True, tm=512, out_dtype=BF16)
    dh2 = _mm(dup, W["wup"], name=n("d_mlp_up"), tb=True, tm=704, tk=4096)
    dw["wup"] = _mm(S["h2"], dup, name=n("dw_mlp_up"), ta=True, tm=512, out_dtype=BF16, pieces=4)
    dx1, dx1_b, ds["mlp_norm"] = _rms_bwd(dh2, S["x1"], W["mlp_norm"], name=n("d_mlp_norm"), res=dx2, bf16_copy=True)
    dmerged = _mm(dx1_b, W["wo"], name=n("d_out_proj"), tb=True, after=hook("after_mlp", dx1, dw))
    dw["wo"] = _mm(S["merged"], dx1_b, name=n("dw_out_proj"), ta=True, tm=512, out_dtype=BF16)
    dpa, dpb, dpc, dg0, dg1, dg2 = _merge_bwd(dmerged, S["gl"], S["pa"], S["pb"], S["pc"], name=n("d_merge"), d=dm.d)
    dgl = jnp.concatenate([dg0, dg1, dg2], axis=1)
    dy_a = _mm(dpa, W["wba"], name=n("d_branch_a"), tb=True)
    dw["wba"] = _mm(S["y_a"], dpa, name=n("dw_branch_a"), ta=True, tm=512, out_dtype=BF16, pieces=4)
    dy_b = _mm(dpb, W["wbb"], name=n("d_branch_b"), tb=True, out_dtype=BF16)
    dw["wbb"] = _mm(S["y_b"], dpb, name=n("dw_branch_b"), ta=True, tm=512, out_dtype=BF16)
    dy_c = _mm(dpc, W["wbc"], name=n("d_branch_c"), tb=True)
    dw["wbc"] = _mm(S["y_c"], dpc, name=n("dw_branch_c"), ta=True, tm=512, out_dtype=BF16, pieces=4)
    dq, dk, dv = _flash_bwd(S["q"], S["k"], S["v"], S["y_b"], dy_b, S["lse"], name=n("d_attention"), heads=dm.heads,
                            after=hook("before_attention", dw["wbc"], dw))
    after_attention = hook("after_attention", dq, dw)
    dq_raw, dk_nope, dk_rope, dgq, dgk = _qk_bwd(
        dq, dk, S["q_raw"], S["k_nope"], S["rest"], cos_t, sin_t, W["q_norm"], W["k_norm"], name=n("d_qk_norm_rope"),
        heads=dm.heads, kr_seg=dm.r_kr // HEAD_PAD, after=after_attention)
    ds["q_norm"] = _head_unpad(dgq, 1)
    ds["k_norm"] = _head_unpad(dgk, 1)
    dkv_v = _mm(dv, W["wv"], name=n("d_up_v"), tb=True)
    dq_lat_n = _mm(dq_raw, W["wuq"], name=n("d_up_q"), tb=True, after=hook("after_qk", dq_raw, dw))
    dw["wuq"] = _mm(S["q_lat"], dq_raw, name=n("dw_up_q"), ta=True, tm=512)
    dkv_lat_n = _mm(dk_nope, W["wkn"], name=n("d_up_k"), tb=True, add=dkv_v)
    dw["wkn"] = _mm(S["kv_lat"], dk_nope, name=n("dw_up_k"), ta=True, tm=512)
    dw["wv"] = _mm(S["kv_lat"], dv, name=n("dw_up_v"), ta=True, tm=512)
    dq_lat, ds["q_lat_norm"] = _rms_bwd(dq_lat_n, S["rest"], W["q_lat_norm"], name=n("d_q_lat_norm"), width=Q_LORA,
                                        seg=dm.r_q // Q_LORA, out_dtype=BF16)
    dkv_lat, ds["kv_lat_norm"] = _rms_bwd(dkv_lat_n, S["rest"], W["kv_lat_norm"], name=n("d_kv_lat_norm"), width=KV_LORA,
                                          seg=dm.r_kv // KV_LORA, out_dtype=BF16)
    du, db, dc, ds["conv_w"] = _conv_bwd(S["rest"], W["conv_w"], dy_a, name=n("d_conv"), dc=dm.dc)
    dpool, dw["wp"], ds["pool_scale"] = _pool_bwd(S["rest"], W["wp"], W["pool_scale"], dy_c, name=n("d_pool"),
                                                  seg0=dm.r_pool // dm.pg, pg=dm.pg)
    drest = jnp.concatenate([du, db, dc, dpool, dq_lat, dkv_lat, dk_rope], axis=1)
    dw["wg_t"] = _mm(dgl, S["h"], name=n("dw_proj_gates"), ta=True, tm=512, out_dtype=BF16, pieces=2)
    dw["wr_t"] = _mm(drest, S["h"], name=n("dw_proj_rest"), ta=True, tm=512, out_dtype=BF16, pieces=2)
    dh_g = _mm(dgl, W["wg_t"], name=n("d_proj_gates"), tm=704, tk=3072, after=hook("after_dw_in", dw["wr_t"], dw))
    dh = _mm(drest, W["wr_t"], name=n("d_proj_rest"), add=dh_g, tm=704, tk=2688, after=hook("after_dh_gates", dh_g, dw))
    dx, dx_b, ds["attn_norm"] = _rms_bwd(dh, S["x"], W["attn_norm"], name=n("d_attn_norm"), res=dx1, bf16_copy=True)
    return dx, dx_b, dw, ds


BIG = ("w_in", "w_uq", "w_ukv", "pool_w", "w_branch_a", "w_branch_b", "w_branch_c", "w_o", "w_up", "w_down")
REPLICATED = ("attn_norm", "q_lat_norm", "kv_lat_norm", "q_norm", "k_norm", "pool_scale", "mlp_norm")
WEIGHTS = ("meta_tokens", "attn_norm", "w_in", "conv_w", "q_lat_norm", "kv_lat_norm", "w_uq", "w_ukv", "q_norm",
           "k_norm", "pool_w", "pool_scale", "w_branch_a", "w_branch_b", "w_branch_c", "w_o", "mlp_norm", "w_up",
           "w_down")


def _pack(arrays):
    flat = jnp.concatenate([a.reshape(-1).astype(F32) for a in arrays])
    pad = (-flat.shape[0]) % (8 * LANES)
    return jnp.pad(flat, (0, pad)).reshape(-1, LANES)


def _unpack(flat, shapes):
    out, pos = [], 0
    flat = flat.reshape(-1)
    for shp in shapes:
        size = math.prod(shp)
        out.append(flat[pos:pos + size].reshape(shp))
        pos += size
    return out


def _update(w, g, m, v, name):
    shp = w.shape
    to2 = lambda a: a.reshape(-1, shp[-1])
    delta, nm, nv = _adamw(to2(w), to2(g), to2(m), to2(v), name=name)
    return delta.reshape(shp), nm.reshape(shp), nv.reshape(shp)


def _step(args):
    x = args["x"][0]
    seq, d = x.shape
    dm = _Dims(d, seq)
    xi, yi, ci = _coords()
    chip = 2 * xi + yi

    small_w = _gather_all(_pack([args["conv_w"], args["meta_tokens"]]), name="gather_small_weights")
    args = dict(args)
    for p in ("", "m_", "v_"):
        args[p + "w_in"] = jnp.swapaxes(args[p + "w_in"], 1, 2)
    order = [(k, l) for l in range(2) for k in BIG]
    last = ("w_up", "w_down")
    group_names = [[("w_in", 0)], [(k, 0) for k in BIG[1:] if k not in last], [(k, 0) for k in last],
                   [(k, 1) for k in BIG]]
    first, others = order[0], order[1:]
    shards = {first: _split_cols(args["w_in"][0].astype(BF16))}
    lands = {first: lax.empty((4,) + shards[first].shape, BF16)}
    sems, thru, token = _start_copies([shards[first], lands[first], small_w], [_ici_gather_plan([(0, 1)])],
                                      name="start_gather_ici_first")
    shards[first], lands[first], small_w = thru
    zero = token[0, 0]
    for n in others:
        shards[n] = (args[n[0]][n[1]] + zero).astype(BF16)
        if n[0] == "w_in":
            shards[n] = _split_cols(shards[n])
        lands[n] = lax.empty((4,) + shards[n].shape, BF16)
    at = {n: i for i, n in enumerate(others)}
    sems_b, thru, token_b = _start_copies(
        [shards[n] for n in others] + [lands[n] for n in others] + [token],
        [_ici_gather_plan([(at[n], len(others) + at[n]) for n in g]) for g in group_names[1:]], name="start_gather_ici")
    sems = sems + sems_b
    for i, n in enumerate(others):
        shards[n], lands[n] = thru[i], thru[len(others) + i]

    def finish_gather(g, after, tag):
        names = group_names[g]
        k = len(names)
        plan, _ = _ici_gather_plan([(i, k + i) for i in range(k)])
        got = _wait_copies([shards[n] for n in names] + [lands[n] for n in names], sems[g], plan, after,
                           name=f"wait_gather_ici_{tag}")
        for i, n in enumerate(names):
            shards[n] = got[i]
        fwd = _d2d_forward_plan(list(range(k)))
        sems2, bufs2, tok2 = _start_copies(got[k:], [fwd], name=f"start_gather_d2d_{tag}")
        return names, bufs2, sems2[0], fwd[0], tok2

    def land_gather(pending, after, tag):
        names, bufs2, sems2, plan, tok2 = pending
        done = _wait_copies(bufs2, sems2, plan, tok2 if after is None else after, name=f"wait_gather_d2d_{tag}")
        return {n[0]: buf for n, buf in zip(names, done)}

    conv_shape, meta_shape = args["conv_w"].shape, args["meta_tokens"].shape
    per_chip = [_unpack(small_w[2 * j], [conv_shape, meta_shape]) for j in range(4)]
    conv_full = jnp.concatenate([p[0] for p in per_chip], axis=-1)
    meta_full = jnp.concatenate([p[1] for p in per_chip], axis=-1)

    layers = []
    for l in range(2):
        small = {k: args[k][l] for k in REPLICATED}
        small["conv_w"] = conv_full[l]
        layers.append(_small_weights(small))

    pos = jnp.arange(dm.t, dtype=F32)
    inv = ROPE_THETA ** (-jnp.arange(0, QK_ROPE, 2, dtype=F32) / QK_ROPE)
    ang = pos[:, None] * inv[None, :]
    cos_t = _rope_pad(jnp.cos(ang), jnp.cos(ang))
    sin_t = _rope_pad(-jnp.sin(ang), jnp.sin(ang))
    tail = jnp.zeros((dm.t - dm.t_real, d), F32) + zero
    h0 = jnp.concatenate([meta_full, x, tail], axis=0)
    target = jnp.concatenate([jnp.zeros((N_META, d), F32), args["loss_target"][0], tail], axis=0)

    h_first = _rms_fwd(h0, layers[0]["attn_norm"], name="attn_norm_l0", after=(token, token_b))
    layers[0].update(_in_weights(dm, land_gather(finish_gather(0, h_first, "l0_in"), None, "l0_in")["w_in"]))
    pending = {}

    def rest_of_layer0(point, after):
        if point == "after_proj":
            return _other_weights(dm, land_gather(finish_gather(1, after, "l0_mid"), None, "l0_mid"))
        if point == "after_attention":
            pending["mlp"] = finish_gather(2, after, "l0_mlp")
            return {"pin": (pending["mlp"][4],)}
        return _other_weights(dm, land_gather(pending["mlp"], after, "l0_mlp"))

    h1, saved0 = _layer_fwd(dm, layers[0], h0, cos_t, sin_t, "l0", more=rest_of_layer0, h=h_first)
    g1 = land_gather(finish_gather(3, saved0["y_b"], "l1"), h1, "l1")
    layers[1].update(_in_weights(dm, g1["w_in"]))
    layers[1].update(_other_weights(dm, g1))
    h2, saved1 = _layer_fwd(dm, layers[1], h1, cos_t, sin_t, "l1")
    sq, dy, dy_b = _loss(h2, target, name="loss_head", first=N_META, last=dm.t_real)
    loss = lax.psum(0.5 / d * sq[0, 0], ("x", "y", "c"))
    core, chip_flags = _one_hot(ci, 2), _one_hot(chip, 4)

    class Reduce:
        def __init__(self, names, dw, tag):
            self.names, self.tag, self.nb = names, tag, len(names)
            self.idx = [(i, self.nb + i) for i in range(self.nb)]
            parts = [_grad_piece(dm, dw, k) for k in names]
            parts = [p if k == "w_in" else _as3d(p) for p, k in zip(parts, names)]
            recv = [lax.empty((4,) + p.shape[2:] if k == "w_in" else (4, p.shape[1] // 2, p.shape[2]), BF16)
                    for p, k in zip(parts, names)]
            self.plan = _swap_half_plan(self.idx)
            self.sems, self.bufs, self.token = _start_copies(parts + recv, [self.plan], name=f"start_swap_{tag}")

        def _land(self, after, what):
            return _wait_copies(self.bufs, self.sems[0], self.plan[0], self.token if after is None else after,
                                name=f"wait_{what}_{self.tag}")

        def scatter(self, after=None):
            got = self._land(after, "swap")
            pairs = [_pair_sum(got[i], got[j], core, name=f"pair_sum_{k}_{self.tag}")
                     for (i, j), k in zip(self.idx, self.names)]
            self.plan = _scatter_plan(self.idx)
            self.sems, self.bufs, self.token = _start_copies(pairs + [lax.empty(p.shape, BF16) for p in pairs],
                                                             [self.plan], name=f"start_scatter_{self.tag}")
            return self.token

        def totals(self, after=None):
            got = self._land(after, "scatter")
            sums = [_chip_sum(got[i], got[j], chip_flags, name=f"chip_sum_{k}_{self.tag}")
                    for (i, j), k in zip(self.idx, self.names)]
            self.plan = _swap_total_plan(self.idx)
            self.sems, self.bufs, self.token = _start_copies(sums + [lax.empty(t.shape, F32) for t in sums],
                                                             [self.plan], name=f"start_swap_total_{self.tag}")
            return self.token

        def finish(self, after=None):
            got = self._land(after, "swap_total")
            return {k: (got[i], got[j]) for (i, j), k in zip(self.idx, self.names)}

    dh1, dh1_b, dw1, ds1 = _layer_bwd(dm, layers[1], saved1, dy, dy_b, cos_t, sin_t, "l1",
                               hook=lambda point, t, dw: (loss.reshape(1, 1),) if point == "start" else ())
    early = ("w_down", "w_up", "w_o", "w_branch_a", "w_branch_b", "w_branch_c")
    late = tuple(k for k in BIG if k not in early)
    stage = {}

    def during_layer0(point, t, dw):
        if point == "start":
            stage["l1"] = Reduce(BIG, dw1, "l1")
            return (stage["l1"].token,)
        if point == "after_mlp":
            return (stage["l1"].scatter(after=t),)
        if point == "before_attention":
            stage["l0a"] = Reduce(early, dw, "l0a")
            return (stage["l0a"].token,)
        if point == "after_attention":
            return (stage["l1"].totals(after=t), stage["l0a"].scatter(after=t))
        if point == "after_qk":
            stage["red1"] = stage["l1"].finish(after=t)
            return ()
        if point == "after_dw_in":
            tok = stage["l0a"].totals(after=t)
            stage["l0b"] = Reduce(late, dw, "l0b")
            return (tok, stage["l0b"].token)
        return (stage["l0b"].scatter(after=t),)

    dh0, _, dw0, ds0 = _layer_bwd(dm, layers[0], saved0, dh1, dh1_b, cos_t, sin_t, "l0", hook=during_layer0)
    grad_x = dh0[N_META:dm.t_real][None]
    red1 = stage["red1"]
    grads, delta, new_m, new_v = {}, {}, {}, {}

    def adamw_big(k, layer, red, prev, after):
        shp = args[k].shape
        wmv = [args[p + k].reshape(2, -1, shp[-1]) for p in ("", "m_", "v_")]
        return _adamw_layer(*wmv, *red[k], core, layer, prev, name=f"adamw_{k}_l{layer}", col_halves=k == "w_in",
                            after=after)

    def keep(k, out):
        shp = args[k].shape
        out = [o.reshape(shp) for o in out]
        grads[k], delta[k], new_m[k], new_v[k] = [jnp.swapaxes(o, 1, 2) for o in out] if k == "w_in" else out

    half_done = {}
    pin = dh0
    for k in BIG:
        half_done[k] = adamw_big(k, 1, red1, None, (pin,))
        pin = half_done[k][0]
    red0a = stage["l0a"].finish(after=pin)
    for k in early:
        out = adamw_big(k, 0, red0a, half_done[k], ())
        keep(k, out)
        pin = out[0]

    small_names = REPLICATED + ("conv_w",)
    small_parts = [jnp.stack([ds0[k].reshape(ds0[k].shape[-2:] if k == "conv_w" else (-1,)),
                              ds1[k].reshape(ds1[k].shape[-2:] if k == "conv_w" else (-1,))]) for k in small_names]
    small_parts.append(dh0[:N_META])
    small_all = _gather_all(_pack(small_parts), name="gather_small_grads", after=(pin,))
    small_sum = _sum_stack(small_all, name="sum_small_grads", out_dtype=F32)
    small_g = dict(zip(small_names + ("meta_tokens",), _unpack(small_sum, [p.shape for p in small_parts])))
    for k in REPLICATED:
        grads[k] = small_g[k]
    dcw = conv_shape[-1]
    grads["conv_w"] = lax.dynamic_slice_in_dim(small_g["conv_w"], chip * dcw, dcw, axis=2)
    dmeta = meta_shape[-1]
    grads["meta_tokens"] = lax.dynamic_slice_in_dim(small_g["meta_tokens"], chip * dmeta, dmeta, axis=1)

    stage["l0b"].totals(after=small_sum)
    red0b = stage["l0b"].finish()
    for k in late:
        keep(k, adamw_big(k, 0, red0b, half_done[k], ()))
    for k in WEIGHTS:
        if k not in BIG:
            grads[k] = grads[k].reshape(args[k].shape)
            delta[k], new_m[k], new_v[k] = _update(args[k], grads[k], args["m_" + k], args["v_" + k], f"adamw_{k}")
    return (loss, grad_x, *[grads[k] for k in WEIGHTS], *[delta[k] for k in WEIGHTS],
            *[new_m[k] for k in WEIGHTS], *[new_v[k] for k in WEIGHTS])


def kernel(x, meta_tokens, attn_norm, w_in, conv_w, q_lat_norm, kv_lat_norm, w_uq, w_ukv, q_norm, k_norm, pool_w, pool_scale, w_branch_a, w_branch_b, w_branch_c, w_o, mlp_norm, w_up, w_down, loss_target, m_meta_tokens, m_attn_norm, m_w_in, m_conv_w, m_q_lat_norm, m_kv_lat_norm, m_w_uq, m_w_ukv, m_q_norm, m_k_norm, m_pool_w, m_pool_scale, m_w_branch_a, m_w_branch_b, m_w_branch_c, m_w_o, m_mlp_norm, m_w_up, m_w_down, v_meta_tokens, v_attn_norm, v_w_in, v_conv_w, v_q_lat_norm, v_kv_lat_norm, v_w_uq, v_w_ukv, v_q_norm, v_k_norm, v_pool_w, v_pool_scale, v_w_branch_a, v_w_branch_b, v_w_branch_c, v_w_o, v_mlp_norm, v_w_up, v_w_down):
    return _step(dict(locals()))
```

```python
import functools
import math

import jax
import jax.numpy as jnp
from jax import lax
from jax.experimental import pallas as pl
from jax.experimental.pallas import tpu as pltpu

F32 = jnp.float32
BF16 = jnp.bfloat16
MESH = pl.DeviceIdType.MESH

EPS = 1e-6
N_META = 16
QK_NOPE = 128
QK_ROPE = 64
QK_HEAD = QK_NOPE + QK_ROPE
V_HEAD = 128
HEAD_PAD = 256
Q_LORA = 512
KV_LORA = 512
ROPE_THETA = 10000.0
POOL_WINDOWS = (2, 4, 8, 16)
HALO = 16
LANES = 128
ADAM_LR = 0.001
ADAM_B1 = 0.9
ADAM_B2 = 0.999
ADAM_EPS = 1e-08
ADAM_WD = 0.01
ADAM_STEP = 10
VMEM_LIMIT = 52 * 1024 * 1024
NEG = -1e30
ATTN_SCALE = QK_HEAD ** -0.5
LOG2_E = 1.4426950408889634
Q_FOLD = ATTN_SCALE * LOG2_E


def _tile(n, target, mult=LANES):
    best = None
    for t in range(mult, min(n, target) + 1, mult):
        if n % t == 0:
            best = t
    return n if best is None else best


def _params(sem=None):
    return pltpu.CompilerParams(dimension_semantics=sem, vmem_limit_bytes=VMEM_LIMIT)


def _mm(a, b, *, name, ta=False, tb=False, add=None, aux=None, epi=None, out_dtype=F32,
        tm=1056, tn=1024, tk=None, after=(), pieces=None):
    if ta:
        K, M = a.shape
    else:
        M, K = a.shape
    if tb:
        N, kb = b.shape
    else:
        kb, N = b.shape
    assert K == kb, (a.shape, b.shape, ta, tb)
    tm = _tile(M, tm, LANES if ta else 16)
    tn = _tile(N if pieces is None else N // pieces, tn, LANES)
    tk = K if tk is None else _tile(K, tk, LANES if (not ta or tb) else 16)
    nk = K // tk
    a_bytes, b_bytes = a.size * a.dtype.itemsize, b.size * b.dtype.itemsize
    j_outer = nk == 1 and a_bytes * (N // tn) + b_bytes < a_bytes + b_bytes * (M // tm)
    grid = (N // tn, M // tm, nk) if j_outer else (M // tm, N // tn, nk)
    row = (lambda g0, g1: g1) if j_outer else (lambda g0, g1: g0)
    col = (lambda g0, g1: g0) if j_outer else (lambda g0, g1: g1)

    if ta:
        a_spec = pl.BlockSpec((tk, tm), lambda g0, g1, k: (k, row(g0, g1)))
    else:
        a_spec = pl.BlockSpec((tm, tk), lambda g0, g1, k: (row(g0, g1), k))
    if tb:
        b_spec = pl.BlockSpec((tn, tk), lambda g0, g1, k: (col(g0, g1), k))
    else:
        b_spec = pl.BlockSpec((tk, tn), lambda g0, g1, k: (k, col(g0, g1)))
    o_spec = pl.BlockSpec((tm, tn), lambda g0, g1, k: (row(g0, g1), col(g0, g1)))
    per = None if pieces is None else N // pieces // tn
    in_specs = [a_spec, b_spec]
    operands = [a, b]
    if add is not None:
        in_specs.append(o_spec)
        operands.append(add)
    if aux is not None:
        in_specs.append(o_spec)
        operands.append(aux)
    after = tuple(after)
    in_specs += [pl.BlockSpec(memory_space=pl.ANY)] * len(after)
    operands += list(after)
    if epi == "relu2":
        out_shape = (jax.ShapeDtypeStruct((M, N), BF16), jax.ShapeDtypeStruct((M, N), BF16))
        out_specs = (o_spec, o_spec)
    elif pieces is not None:
        out_shape = jax.ShapeDtypeStruct((pieces, M, N // pieces), out_dtype)
        out_specs = pl.BlockSpec((1, tm, tn), lambda g0, g1, k: (col(g0, g1) // per, row(g0, g1), col(g0, g1) % per))
    else:
        out_shape = jax.ShapeDtypeStruct((M, N), out_dtype)
        out_specs = o_spec
    dims =(((0 if ta else 1,), (1 if tb else 0,)), ((), ()))
    has_add, has_aux = add is not None, aux is not None

    def body(*refs):
        a_ref, b_ref = refs[0], refs[1]
        pos = 2
        add_ref = aux_ref = None
        if has_add:
            add_ref = refs[pos]
            pos += 1
        if has_aux:
            aux_ref = refs[pos]
            pos += 1
        pos += len(after)
        n_out = 2 if epi == "relu2" else 1
        out_refs = refs[pos:pos + n_out]
        acc_ref = refs[pos + n_out] if nk > 1 else None

        part = lax.dot_general(a_ref[...].astype(BF16), b_ref[...].astype(BF16), dims,
                               preferred_element_type=F32)

        def finish(acc):
            if has_add:
                acc = acc + add_ref[...].astype(F32)
            if epi == "relu2":
                r = jnp.maximum(acc, 0.0)
                out_refs[0][...] = acc.astype(BF16)
                out_refs[1][...] = (r * r).astype(BF16)
            elif epi == "drelu2":
                u = aux_ref[...].astype(F32)
                out_refs[0][...] = (acc * (2.0 * jnp.maximum(u, 0.0))).astype(out_dtype)
            else:
                out_refs[0][...] = acc.astype(out_dtype).reshape(out_refs[0].shape)

        if nk == 1:
            finish(part)
        else:
            k = pl.program_id(2)

            @pl.when(k == 0)
            def _():
                acc_ref[...] = part

            @pl.when(k > 0)
            def _():
                acc_ref[...] += part

            @pl.when(k == nk - 1)
            def _():
                finish(acc_ref[...])

    scratch = [pltpu.VMEM((tm, tn), F32)] if nk > 1 else []
    return pl.pallas_call(
        body, name=name, grid=grid, in_specs=in_specs, out_specs=out_specs, out_shape=out_shape,
        scratch_shapes=scratch, compiler_params=_params(("parallel", "parallel", "arbitrary")),
    )(*operands)


def _rms_fwd(x, g, *, name, width=None, seg=0, tm=384, after=()):
    T = x.shape[0]
    width = x.shape[1] if width is None else width
    tm = _tile(T, tm, 16)
    after = tuple(after)

    def body(x_ref, g_ref, *rest):
        xf = x_ref[...].astype(F32)
        r = lax.rsqrt(jnp.mean(xf * xf, axis=-1, keepdims=True) + EPS)
        rest[-1][...] = (xf * r * g_ref[...]).astype(BF16)

    return pl.pallas_call(
        body, name=name, grid=(T // tm,),
        in_specs=[pl.BlockSpec((tm, width), lambda i: (i, seg)), pl.BlockSpec((1, width), lambda i: (0, 0))]
        + [pl.BlockSpec(memory_space=pl.ANY)] * len(after),
        out_specs=pl.BlockSpec((tm, width), lambda i: (i, 0)),
        out_shape=jax.ShapeDtypeStruct((T, width), BF16),
        compiler_params=_params(("parallel",)),
    )(x, g, *after)


def _rms_bwd(dy, x, g, *, name, width=None, seg=0, res=None, out_dtype=F32, tm=384, bf16_copy=False):
    T = x.shape[0]
    width = x.shape[1] if width is None else width
    tm = _tile(T, tm, 16)
    has_res = res is not None

    def body(*refs):
        dy_ref, x_ref, g_ref = refs[:3]
        res_ref = refs[3] if has_res else None
        dx_ref, dg_ref = refs[4 if has_res else 3], refs[-1]
        xf = x_ref[...].astype(F32)
        dyf = dy_ref[...].astype(F32)
        r = lax.rsqrt(jnp.mean(xf * xf, axis=-1, keepdims=True) + EPS)
        xhat = xf * r
        dyh = dyf * g_ref[...]
        dx = r * (dyh - xhat * jnp.mean(dyh * xhat, axis=-1, keepdims=True))
        if has_res:
            dx = dx + res_ref[...].astype(F32)
        dx_ref[...] = dx.astype(out_dtype)
        if bf16_copy:
            refs[-2][...] = dx.astype(BF16)
        part = jnp.sum(dyf * xhat, axis=0, keepdims=True)

        @pl.when(pl.program_id(0) == 0)
        def _():
            dg_ref[...] = part

        @pl.when(pl.program_id(0) > 0)
        def _():
            dg_ref[...] += part

    row = pl.BlockSpec((tm, width), lambda i: (i, 0))
    in_specs = [row, pl.BlockSpec((tm, width), lambda i: (i, seg)), pl.BlockSpec((1, width), lambda i: (0, 0))]
    operands = [dy, x, g]
    if has_res:
        in_specs.append(row)
        operands.append(res)
    vec = pl.BlockSpec((1, width), lambda i: (0, 0))
    full = [jax.ShapeDtypeStruct((T, width), out_dtype)] + ([jax.ShapeDtypeStruct((T, width), BF16)] if bf16_copy else [])
    return pl.pallas_call(
        body, name=name, grid=(T // tm,), in_specs=in_specs,
        out_specs=tuple([row] * len(full) + [vec]),
        out_shape=tuple(full + [jax.ShapeDtypeStruct((1, width), F32)]),
        compiler_params=_params(("arbitrary",)),
    )(*operands)


def _down(ext, k):
    return pltpu.roll(ext, k, 0)


def _up(ext, k):
    return pltpu.roll(ext, ext.shape[0] - k, 0)


def _pre_halo(ref, r, R):
    start = pl.multiple_of(jnp.maximum(r * R - HALO, 0), 8)
    keep = (r > 0).astype(F32)
    return ref[pl.ds(start, HALO), :].astype(F32) * keep


def _post_halo(ref, r, R, n_chunks):
    start = pl.multiple_of(jnp.minimum(r * R + R, (n_chunks - 1) * R + R - HALO), 8)
    keep = (r < n_chunks - 1).astype(F32)
    return ref[pl.ds(start, HALO), :].astype(F32) * keep


def _chunk(ref, r, R):
    return ref[pl.ds(pl.multiple_of(r * R, 8), R), :].astype(F32)


def _conv_fwd(rest, conv_w, *, name, dc, tc=128, rows=1056):
    T = rest.shape[0]
    tc = _tile(dc, tc)
    nb = dc // tc
    R = _tile(T, rows, 16)
    n_chunks = T // R

    def body(u_ref, b_ref, c_ref, w_ref, y_ref):
        w0, w1, w2 = w_ref[0:1, :], w_ref[1:2, :], w_ref[2:3, :]

        def chunk(r, carry):
            cu = _chunk(c_ref, r, R) * _chunk(u_ref, r, R)
            ext = jnp.concatenate([_pre_halo(c_ref, r, R) * _pre_halo(u_ref, r, R), cu], axis=0)
            conv = w0 * _down(ext, 2)[HALO:] + w1 * _down(ext, 1)[HALO:] + w2 * cu
            y_ref[pl.ds(pl.multiple_of(r * R, 8), R), :] = (_chunk(b_ref, r, R) * conv).astype(BF16)
            return carry

        lax.fori_loop(0, n_chunks, chunk, 0)

    col = lambda off: pl.BlockSpec((T, tc), lambda j: (0, off * nb + j))
    return pl.pallas_call(
        body, name=name, grid=(nb,),
        in_specs=[col(0), col(1), col(2), pl.BlockSpec((3, tc), lambda j: (0, j))],
        out_specs=pl.BlockSpec((T, tc), lambda j: (0, j)),
        out_shape=jax.ShapeDtypeStruct((T, dc), BF16),
        compiler_params=_params(("parallel",)),
    )(rest, rest, rest, conv_w)


def _conv_bwd(rest, conv_w, dy, *, name, dc, tc=128, rows=1056):
    T = rest.shape[0]
    tc = _tile(dc, tc)
    nb = dc // tc
    R = _tile(T, rows, 16)
    n_chunks = T // R

    def body(u_ref, b_ref, c_ref, w_ref, dy_ref, du_ref, db_ref, dc_ref, dw_ref):
        w0, w1, w2 = w_ref[0:1, :], w_ref[1:2, :], w_ref[2:3, :]

        def chunk(r, carry):
            a0, a1, a2 = carry
            u, b, c = _chunk(u_ref, r, R), _chunk(b_ref, r, R), _chunk(c_ref, r, R)
            dy_c = _chunk(dy_ref, r, R)
            cu = c * u
            ext = jnp.concatenate([_pre_halo(c_ref, r, R) * _pre_halo(u_ref, r, R), cu], axis=0)
            cu1, cu2 = _down(ext, 1)[HALO:], _down(ext, 2)[HALO:]
            conv = w0 * cu2 + w1 * cu1 + w2 * cu
            dconv = dy_c * b
            dext = jnp.concatenate(
                [dconv, _post_halo(dy_ref, r, R, n_chunks) * _post_halo(b_ref, r, R, n_chunks)], axis=0)
            dcu = w2 * dconv + w1 * _up(dext, 1)[:R] + w0 * _up(dext, 2)[:R]
            rows_at = pl.ds(pl.multiple_of(r * R, 8), R)
            db_ref[rows_at, :] = (dy_c * conv).astype(BF16)
            du_ref[rows_at, :] = (dcu * c).astype(BF16)
            dc_ref[rows_at, :] = (dcu * u).astype(BF16)
            return (a0 + jnp.sum(dconv * cu2, axis=0, keepdims=True),
                    a1 + jnp.sum(dconv * cu1, axis=0, keepdims=True),
                    a2 + jnp.sum(dconv * cu, axis=0, keepdims=True))

        zero = jnp.zeros((1, tc), F32)
        a0, a1, a2 = lax.fori_loop(0, n_chunks, chunk, (zero, zero, zero))
        dw_ref[0:1, :] = a0
        dw_ref[1:2, :] = a1
        dw_ref[2:3, :] = a2

    col = lambda off: pl.BlockSpec((T, tc), lambda j: (0, off * nb + j))
    own = pl.BlockSpec((T, tc), lambda j: (0, j))
    return pl.pallas_call(
        body, name=name, grid=(nb,),
        in_specs=[col(0), col(1), col(2), pl.BlockSpec((3, tc), lambda j: (0, j)), own],
        out_specs=(own, own, own, pl.BlockSpec((3, tc), lambda j: (0, j))),
        out_shape=(jax.ShapeDtypeStruct((T, dc), BF16),) * 3 + (jax.ShapeDtypeStruct((3, dc), F32),),
        compiler_params=_params(("parallel",)),
    )(rest, rest, rest, conv_w, dy)


def _window_count(r, R, n_rows, w, first_row_offset):
    t = lax.broadcasted_iota(jnp.int32, (n_rows, 1), 0) + (r * R + first_row_offset)
    return jnp.minimum(t + 1, w).astype(F32)


def _pool_fwd(rest, pool_w, pool_scale, *, name, seg0, pg, rows=1056):
    T = rest.shape[0]
    R = _tile(T, rows, 16)
    n_chunks = T // R
    n_groups = len(POOL_WINDOWS)

    def body(x_ref, w_ref, s_ref, y_ref):
        def run(window):
            def chunk(r, carry):
                g = _chunk(x_ref, r, R)
                s = jnp.concatenate([_pre_halo(x_ref, r, R), g], axis=0)
                k = 1
                while k < window:
                    s = s + _down(s, k)
                    k *= 2
                pooled = s[HALO:] / _window_count(r, R, R, window, 0) - g
                mixed = jnp.dot(pooled.astype(BF16), w_ref[0], preferred_element_type=F32)
                y_ref[pl.ds(pl.multiple_of(r * R, 8), R), :] = (mixed * s_ref[...]).astype(BF16)
                return carry

            lax.fori_loop(0, n_chunks, chunk, 0)

        for gi, window in enumerate(POOL_WINDOWS):
            pl.when(pl.program_id(0) == gi)(functools.partial(run, window))

    return pl.pallas_call(
        body, name=name, grid=(n_groups,),
        in_specs=[pl.BlockSpec((T, pg), lambda g: (0, seg0 + g)),
                  pl.BlockSpec((1, pg, pg), lambda g: (g, 0, 0)),
                  pl.BlockSpec((1, pg), lambda g: (0, g))],
        out_specs=pl.BlockSpec((T, pg), lambda g: (0, g)),
        out_shape=jax.ShapeDtypeStruct((T, n_groups * pg), BF16),
        compiler_params=_params(("parallel",)),
    )(rest, pool_w, pool_scale)


def _pool_bwd(rest, pool_w, pool_scale, dy, *, name, seg0, pg, rows=1056):
    T = rest.shape[0]
    R = _tile(T, rows, 16)
    n_chunks = T // R
    n_groups = len(POOL_WINDOWS)

    def body(x_ref, w_ref, s_ref, dy_ref, dx_ref, dw_ref, ds_ref):
        def run(window):
            def chunk(r, carry):
                dw_acc, ds_acc = carry
                g = _chunk(x_ref, r, R)
                s = jnp.concatenate([_pre_halo(x_ref, r, R), g], axis=0)
                k = 1
                while k < window:
                    s = s + _down(s, k)
                    k *= 2
                pooled = (s[HALO:] / _window_count(r, R, R, window, 0) - g).astype(BF16)
                mixed = jnp.dot(pooled, w_ref[0], preferred_element_type=F32)
                dy_c = _chunk(dy_ref, r, R)
                dm_ext = (jnp.concatenate([dy_c, _post_halo(dy_ref, r, R, n_chunks)], axis=0)
                          * s_ref[...]).astype(BF16)
                dpool_ext = lax.dot_general(dm_ext, w_ref[0], (((1,), (1,)), ((), ())),
                                            preferred_element_type=F32)
                a = dpool_ext / _window_count(r, R, R + HALO, window, 0)
                k = 1
                while k < window:
                    a = a + _up(a, k)
                    k *= 2
                dx_ref[pl.ds(pl.multiple_of(r * R, 8), R), :] = (a[:R] - dpool_ext[:R]).astype(BF16)
                dw_acc = dw_acc + lax.dot_general(pooled, dm_ext[:R], (((0,), (0,)), ((), ())),
                                                  preferred_element_type=F32)
                ds_acc = ds_acc + jnp.sum(dy_c * mixed, axis=0, keepdims=True)
                return dw_acc, ds_acc

            dw_acc, ds_acc = lax.fori_loop(0, n_chunks, chunk,
                                           (jnp.zeros((pg, pg), F32), jnp.zeros((1, pg), F32)))
            dw_ref[0] = dw_acc
            ds_ref[...] = ds_acc

        for gi, window in enumerate(POOL_WINDOWS):
            pl.when(pl.program_id(0) == gi)(functools.partial(run, window))

    own = pl.BlockSpec((T, pg), lambda g: (0, g))
    return pl.pallas_call(
        body, name=name, grid=(n_groups,),
        in_specs=[pl.BlockSpec((T, pg), lambda g: (0, seg0 + g)),
                  pl.BlockSpec((1, pg, pg), lambda g: (g, 0, 0)),
                  pl.BlockSpec((1, pg), lambda g: (0, g)), own],
        out_specs=(own, pl.BlockSpec((1, pg, pg), lambda g: (g, 0, 0)), pl.BlockSpec((1, pg), lambda g: (0, g))),
        out_shape=(jax.ShapeDtypeStruct((T, n_groups * pg), BF16),
                   jax.ShapeDtypeStruct((n_groups, pg, pg), F32),
                   jax.ShapeDtypeStruct((1, n_groups * pg), F32)),
        compiler_params=_params(("parallel",)),
    )(rest, pool_w, pool_scale, dy)


def _rope(r, cos_t, sin_t):
    return r * cos_t + pltpu.roll(r, LANES // 2, 1) * sin_t


def _rope_t(d, cos_t, sin_t):
    return d * cos_t + pltpu.roll(d * sin_t, LANES // 2, 1)


def _qk_fwd(q_raw, k_nope, rest, cos_t, sin_t, q_norm, k_norm, *, name, heads, kr_seg, tm=192):
    T = q_raw.shape[0]
    tm = _tile(T, tm, 16)

    def body(q_ref, kn_ref, kr_ref, c_ref, s_ref, gq_ref, gk_ref, qo_ref, ko_ref):
        cos_b, sin_b = c_ref[...], s_ref[...]
        kr = kr_ref[:, 0:LANES]
        kr_ss = jnp.sum(kr * kr, axis=-1, keepdims=True)
        gq, gk = gq_ref[...], gk_ref[...]
        for h in range(heads):
            lo = h * HEAD_PAD
            q = q_ref[:, lo:lo + HEAD_PAD].astype(F32)
            rq = lax.rsqrt(jnp.sum(q * q, axis=-1, keepdims=True) / QK_HEAD + EPS)
            qn = q * (rq * Q_FOLD) * gq
            qo_ref[:, lo:lo + LANES] = qn[:, :LANES].astype(BF16)
            qo_ref[:, lo + LANES:lo + HEAD_PAD] = _rope(qn[:, LANES:], cos_b, sin_b).astype(BF16)
            kn = kn_ref[:, h * LANES:(h + 1) * LANES].astype(F32)
            rk = lax.rsqrt((jnp.sum(kn * kn, axis=-1, keepdims=True) + kr_ss) / QK_HEAD + EPS)
            ko_ref[:, lo:lo + LANES] = (kn * rk * gk[:, :LANES]).astype(BF16)
            ko_ref[:, lo + LANES:lo + HEAD_PAD] = _rope(kr * rk * gk[:, LANES:], cos_b, sin_b).astype(BF16)

    wq, wk = heads * HEAD_PAD, heads * LANES
    return pl.pallas_call(
        body, name=name, grid=(T // tm,),
        in_specs=[pl.BlockSpec((tm, wq), lambda i: (i, 0)), pl.BlockSpec((tm, wk), lambda i: (i, 0)),
                  pl.BlockSpec((tm, HEAD_PAD), lambda i: (i, kr_seg)),
                  pl.BlockSpec((tm, LANES), lambda i: (i, 0)), pl.BlockSpec((tm, LANES), lambda i: (i, 0)),
                  pl.BlockSpec((1, HEAD_PAD), lambda i: (0, 0)), pl.BlockSpec((1, HEAD_PAD), lambda i: (0, 0))],
        out_specs=(pl.BlockSpec((tm, wq), lambda i: (i, 0)), pl.BlockSpec((tm, wq), lambda i: (i, 0))),
        out_shape=(jax.ShapeDtypeStruct((T, wq), BF16), jax.ShapeDtypeStruct((T, wq), BF16)),
        compiler_params=_params(("parallel",)),
    )(q_raw, k_nope, rest, cos_t, sin_t, q_norm, k_norm)


def _qk_bwd(dq, dk, q_raw, k_nope, rest, cos_t, sin_t, q_norm, k_norm, *, name, heads, kr_seg, tm=128, after=()):
    T = q_raw.shape[0]
    tm = _tile(T, tm, 16)
    after = tuple(after)

    def body(dq_ref, dk_ref, q_ref, kn_ref, kr_ref, c_ref, s_ref, gq_ref, gk_ref, *rest_refs):
        dqr_ref, dkn_ref, dkr_ref, dgq_ref, dgk_ref = rest_refs[len(after):]
        cos_b, sin_b = c_ref[...], s_ref[...]
        kr = kr_ref[:, 0:LANES]
        kr_ss = jnp.sum(kr * kr, axis=-1, keepdims=True)
        gq, gk = gq_ref[...], gk_ref[...]
        dgq = jnp.zeros((1, HEAD_PAD), F32)
        dgk_n = jnp.zeros((1, LANES), F32)
        dgk_r = jnp.zeros((1, LANES), F32)
        dkr = jnp.zeros((tm, LANES), F32)
        for h in range(heads):
            lo = h * HEAD_PAD
            q = q_ref[:, lo:lo + HEAD_PAD].astype(F32)
            rq = lax.rsqrt(jnp.sum(q * q, axis=-1, keepdims=True) / QK_HEAD + EPS)
            qhat = q * rq
            dqn = jnp.concatenate([dq_ref[:, lo:lo + LANES],
                                   _rope_t(dq_ref[:, lo + LANES:lo + HEAD_PAD], cos_b, sin_b)], axis=1) * ATTN_SCALE
            dgq = dgq + jnp.sum(dqn * qhat, axis=0, keepdims=True)
            dqh = dqn * gq
            dqr_ref[:, lo:lo + HEAD_PAD] = (
                rq * (dqh - qhat * (jnp.sum(dqh * qhat, axis=-1, keepdims=True) / QK_HEAD))).astype(BF16)
            kn = kn_ref[:, h * LANES:(h + 1) * LANES].astype(F32)
            rk = lax.rsqrt((jnp.sum(kn * kn, axis=-1, keepdims=True) + kr_ss) / QK_HEAD + EPS)
            khat_n, khat_r = kn * rk, kr * rk
            dkn_n = dk_ref[:, lo:lo + LANES] * (1.0 / LOG2_E)
            dkn_r = _rope_t(dk_ref[:, lo + LANES:lo + HEAD_PAD], cos_b, sin_b) * (1.0 / LOG2_E)
            dgk_n = dgk_n + jnp.sum(dkn_n * khat_n, axis=0, keepdims=True)
            dgk_r = dgk_r + jnp.sum(dkn_r * khat_r, axis=0, keepdims=True)
            dkh_n, dkh_r = dkn_n * gk[:, :LANES], dkn_r * gk[:, LANES:]
            proj = (jnp.sum(dkh_n * khat_n, axis=-1, keepdims=True)
                    + jnp.sum(dkh_r * khat_r, axis=-1, keepdims=True)) / QK_HEAD
            dkn_ref[:, h * LANES:(h + 1) * LANES] = (rk * (dkh_n - khat_n * proj)).astype(BF16)
            dkr = dkr + rk * (dkh_r - khat_r * proj)
        dkr_ref[:, 0:LANES] = dkr.astype(BF16)
        dkr_ref[:, LANES:HEAD_PAD] = jnp.zeros((tm, HEAD_PAD - LANES), BF16)
        dgk = jnp.concatenate([dgk_n, dgk_r], axis=1)

        @pl.when(pl.program_id(0) == 0)
        def _():
            dgq_ref[...] = dgq
            dgk_ref[...] = dgk

        @pl.when(pl.program_id(0) > 0)
        def _():
            dgq_ref[...] += dgq
            dgk_ref[...] += dgk

    wq, wk = heads * HEAD_PAD, heads * LANES
    row = lambda w: pl.BlockSpec((tm, w), lambda i: (i, 0))
    vec = pl.BlockSpec((1, HEAD_PAD), lambda i: (0, 0))
    return pl.pallas_call(
        body, name=name, grid=(T // tm,),
        in_specs=[row(wq), row(wq), row(wq), row(wk), pl.BlockSpec((tm, HEAD_PAD), lambda i: (i, kr_seg)),
                  row(LANES), row(LANES), vec, vec] + [pl.BlockSpec(memory_space=pl.ANY)] * len(after),
        out_specs=(row(wq), row(wk), row(HEAD_PAD), vec, vec),
        out_shape=(jax.ShapeDtypeStruct((T, wq), BF16), jax.ShapeDtypeStruct((T, wk), BF16),
                   jax.ShapeDtypeStruct((T, HEAD_PAD), BF16),
                   jax.ShapeDtypeStruct((1, HEAD_PAD), F32), jax.ShapeDtypeStruct((1, HEAD_PAD), F32)),
        compiler_params=_params(("arbitrary",)),
    )(dq, dk, q_raw, k_nope, rest, cos_t, sin_t, q_norm, k_norm, *after)


def _causal_mask(s):
    row = lax.broadcasted_iota(jnp.int32, s.shape, 0)
    col = lax.broadcasted_iota(jnp.int32, s.shape, 1)
    return jnp.where(row >= col, s, NEG)


def _flash_fwd(q, k, v, *, name, heads, tq=384, hp=2, parts=2):
    T = q.shape[0]
    tq = _tile(T, tq, LANES)
    nq = T // tq
    tr = tq // parts
    nt = (((1,), (1,)), ((), ()))
    chains = [(h, r) for h in range(hp) for r in range(parts)]

    def body(q_ref, k_ref, v_ref, o_ref, lse_ref, acc_ref):
        def q_block(i, carry):
            rows_at = [pl.ds(pl.multiple_of(i * tq + r * tr, tr), tr) for r in range(parts)]
            qbs = [q_ref[rows_at[r], h * HEAD_PAD:(h + 1) * HEAD_PAD] for h, r in chains]
            for c in range(len(chains)):
                acc_ref[c] = jnp.zeros((tr, V_HEAD), F32)

            def step(j, state, masked):
                k_at = pl.ds(pl.multiple_of(j * tq, tq), tq)
                new = []
                scores = [lax.dot_general(qb, k_ref[k_at, h * HEAD_PAD:(h + 1) * HEAD_PAD], nt,
                                          preferred_element_type=F32) for qb, (h, r) in zip(qbs, chains)]
                for c, (s, (h, r)) in enumerate(zip(scores, chains)):
                    m, l = state[c]
                    if masked:
                        row = lax.broadcasted_iota(jnp.int32, s.shape, 0) + r * tr
                        s = jnp.where(row >= lax.broadcasted_iota(jnp.int32, s.shape, 1), s, NEG)
                    m_new = jnp.maximum(m, jnp.max(s, axis=-1, keepdims=True))
                    p = jnp.exp2(s - m_new)
                    alpha = jnp.exp2(m - m_new)
                    new.append((m_new, alpha * l + jnp.sum(p, axis=-1, keepdims=True)))
                    acc_ref[c] = alpha * acc_ref[c] + jnp.dot(p.astype(BF16), v_ref[k_at, h * V_HEAD:(h + 1) * V_HEAD],
                                                              preferred_element_type=F32)
                return tuple(new)

            init = tuple((jnp.full((tr, 1), NEG, F32), jnp.zeros((tr, 1), F32)) for _ in chains)
            state = lax.fori_loop(0, i, lambda j, st: step(j, st, False), init)
            state = step(i, state, True)
            for c, ((m, l), (h, r)) in enumerate(zip(state, chains)):
                o_ref[rows_at[r], h * V_HEAD:(h + 1) * V_HEAD] = (acc_ref[c] / l).astype(BF16)
                lse_ref[h, rows_at[r], :] = jnp.broadcast_to(m + jnp.log2(l), (tr, LANES))
            return carry

        lax.fori_loop(0, nq, q_block, 0)

    qk_spec = pl.BlockSpec((T, hp * HEAD_PAD), lambda g: (0, g))
    v_spec = pl.BlockSpec((T, hp * V_HEAD), lambda g: (0, g))
    return pl.pallas_call(
        body, name=name, grid=(heads // hp,), in_specs=[qk_spec, qk_spec, v_spec],
        out_specs=(v_spec, pl.BlockSpec((hp, T, LANES), lambda g: (g, 0, 0))),
        out_shape=(jax.ShapeDtypeStruct((T, heads * V_HEAD), BF16), jax.ShapeDtypeStruct((heads, T, LANES), F32)),
        scratch_shapes=[pltpu.VMEM((len(chains), tr, V_HEAD), F32)],
        compiler_params=_params(("parallel",)),
    )(q, k, v)


def _flash_bwd(q, k, v, o, do, lse, *, name, heads, tq=384, after=()):
    T = q.shape[0]
    tq = _tile(T, tq, LANES)
    nq = T // tq
    nt = (((1,), (1,)), ((), ()))
    tn = (((0,), (0,)), ((), ()))

    after = tuple(after)

    def body(q_ref, k_ref, v_ref, o_ref, do_ref, lse_ref, *rest):
        dq_ref, dk_ref, dv_ref, delta_ref, dv_acc_ref = rest[len(after):]
        def fill_delta(i, carry):
            at = pl.ds(pl.multiple_of(i * tq, tq), tq)
            d = jnp.sum(o_ref[at, :].astype(F32) * do_ref[at, :].astype(F32), axis=-1, keepdims=True)
            delta_ref[at, :] = jnp.broadcast_to(d, (tq, LANES))
            dq_ref[at, :] = jnp.zeros((tq, HEAD_PAD), F32)
            return carry

        lax.fori_loop(0, nq, fill_delta, 0)

        def kv_block(j, carry):
            k_at = pl.ds(pl.multiple_of(j * tq, tq), tq)
            kb, vb = k_ref[k_at, :], v_ref[k_at, :]

            def steps(blocks, masked):
                at = [pl.ds(pl.multiple_of(i * tq, tq), tq) for i in blocks]
                qbs = [q_ref[a, :] for a in at]
                dobs = [do_ref[a, :] for a in at]
                scores = [lax.dot_general(qb, kb, nt, preferred_element_type=F32) for qb in qbs]
                dps = [lax.dot_general(dob, vb, nt, preferred_element_type=F32) for dob in dobs]
                for a, qb, dob, sc, dp in zip(at, qbs, dobs, scores, dps):
                    if masked:
                        sc = _causal_mask(sc)
                    p = jnp.exp2(sc - lse_ref[0, a, :][:, 0:1])
                    ds = (p * (dp - delta_ref[a, :][:, 0:1])).astype(BF16)
                    dv_part = lax.dot_general(p.astype(BF16), dob, tn, preferred_element_type=F32)
                    dk_part = lax.dot_general(ds, qb, tn, preferred_element_type=F32)
                    if masked:
                        dv_acc_ref[...] = dv_part
                        dk_ref[k_at, :] = dk_part
                    else:
                        dv_acc_ref[...] += dv_part
                        dk_ref[k_at, :] += dk_part
                    dq_ref[a, :] += jnp.dot(ds, kb, preferred_element_type=F32)

            def two_blocks(t, carry):
                steps([j + 1 + 2 * t, j + 2 + 2 * t], False)
                return carry

            steps([j], True)
            rest = nq - 1 - j
            lax.fori_loop(0, rest // 2, two_blocks, 0)

            @pl.when(rest % 2 == 1)
            def _():
                steps([nq - 1], False)

            dv_ref[k_at, :] = dv_acc_ref[...].astype(BF16)
            return carry

        lax.fori_loop(0, nq, kv_block, 0)

    qk_spec = pl.BlockSpec((T, HEAD_PAD), lambda h: (0, h))
    v_spec = pl.BlockSpec((T, V_HEAD), lambda h: (0, h))
    return pl.pallas_call(
        body, name=name, grid=(heads,),
        in_specs=[qk_spec, qk_spec, v_spec, v_spec, v_spec, pl.BlockSpec((1, T, LANES), lambda h: (h, 0, 0))]
        + [pl.BlockSpec(memory_space=pl.ANY)] * len(after),
        out_specs=(qk_spec, qk_spec, v_spec),
        out_shape=(jax.ShapeDtypeStruct((T, heads * HEAD_PAD), F32), jax.ShapeDtypeStruct((T, heads * HEAD_PAD), F32),
                   jax.ShapeDtypeStruct((T, heads * V_HEAD), BF16)),
        scratch_shapes=[pltpu.VMEM((T, LANES), F32), pltpu.VMEM((tq, V_HEAD), F32)],
        compiler_params=_params(("parallel",)),
    )(q, k, v, o, do, lse, *after)


def _merge_fwd(gl, pa, pb, pc, *, name, d, tm=384, tn=1024):
    T = pa.shape[0]
    tm, tn = _tile(T, tm, 16), _tile(d, tn)
    nb = d // tn

    def body(g0, g1, g2, a, b, c, o_ref):
        f = lambda ref: ref[...].astype(F32)
        o_ref[...] = (jax.nn.sigmoid(f(g0)) * f(a) + jax.nn.sigmoid(f(g1)) * f(b)
                      + jax.nn.sigmoid(f(g2)) * f(c)).astype(BF16)

    gate = lambda n: pl.BlockSpec((tm, tn), lambda i, j: (i, n * nb + j))
    blk = pl.BlockSpec((tm, tn), lambda i, j: (i, j))
    return pl.pallas_call(
        body, name=name, grid=(T // tm, nb), in_specs=[gate(0), gate(1), gate(2), blk, blk, blk],
        out_specs=blk, out_shape=jax.ShapeDtypeStruct((T, d), BF16),
        compiler_params=_params(("parallel", "parallel")),
    )(gl, gl, gl, pa, pb, pc)


def _merge_bwd(dm, gl, pa, pb, pc, *, name, d, tm=384, tn=1024):
    T = pa.shape[0]
    tm, tn = _tile(T, tm, 16), _tile(d, tn)
    nb = d // tn

    def body(dm_ref, g0, g1, g2, a, b, c, da, db, dc, dg0, dg1, dg2):
        dmv = dm_ref[...].astype(F32)
        for g_ref, p_ref, dp_ref, dg_ref in ((g0, a, da, dg0), (g1, b, db, dg1), (g2, c, dc, dg2)):
            sg = jax.nn.sigmoid(g_ref[...].astype(F32))
            dp_ref[...] = (dmv * sg).astype(BF16)
            dg_ref[...] = (dmv * p_ref[...].astype(F32) * sg * (1.0 - sg)).astype(BF16)

    gate = lambda n: pl.BlockSpec((tm, tn), lambda i, j: (i, n * nb + j))
    blk = pl.BlockSpec((tm, tn), lambda i, j: (i, j))
    return pl.pallas_call(
        body, name=name, grid=(T // tm, nb), in_specs=[blk, gate(0), gate(1), gate(2), blk, blk, blk],
        out_specs=(blk,) * 6, out_shape=(jax.ShapeDtypeStruct((T, d), BF16),) * 6,
        compiler_params=_params(("parallel", "parallel")),
    )(dm, gl, gl, gl, pa, pb, pc)


def _loss(y, target, *, name, first, last, tm=384):
    T, d = y.shape
    tm = _tile(T, tm, 16)

    def body(y_ref, t_ref, loss_ref, dy_ref, dyb_ref):
        i = pl.program_id(0)
        row = lax.broadcasted_iota(jnp.int32, (tm, 1), 0) + i * tm
        real = jnp.logical_and(row >= first, row < last)
        err = jnp.where(real, y_ref[...] - t_ref[...], 0.0)
        dy_ref[...] = err * (1.0 / d)
        dyb_ref[...] = (err * (1.0 / d)).astype(BF16)
        part = jnp.broadcast_to(jnp.sum(err * err, keepdims=True).reshape(1, 1), (1, LANES))

        @pl.when(i == 0)
        def _():
            loss_ref[...] = part

        @pl.when(i > 0)
        def _():
            loss_ref[...] += part

    blk = pl.BlockSpec((tm, d), lambda i: (i, 0))
    return pl.pallas_call(
        body, name=name, grid=(T // tm,), in_specs=[blk, blk],
        out_specs=(pl.BlockSpec((1, LANES), lambda i: (0, 0)), blk, blk),
        out_shape=(jax.ShapeDtypeStruct((1, LANES), F32), jax.ShapeDtypeStruct((T, d), F32),
                   jax.ShapeDtypeStruct((T, d), BF16)),
        compiler_params=_params(("arbitrary",)),
    )(y, target)


def _as3d(a):
    return a.reshape(a.shape[0], -1, a.shape[-1])


def _sum_stack(parts, *, name, out_dtype, rows=256):
    n, R, C = parts.shape
    tr = _tile(R, rows, 16)

    def body(p_ref, o_ref):
        acc = p_ref[0].astype(F32)
        for s in range(1, n):
            acc = acc + p_ref[s].astype(F32)
        o_ref[...] = acc.astype(out_dtype)

    return pl.pallas_call(
        body, name=name, grid=(R // tr,),
        in_specs=[pl.BlockSpec((n, tr, C), lambda i: (0, i, 0))],
        out_specs=pl.BlockSpec((tr, C), lambda i: (i, 0)),
        out_shape=jax.ShapeDtypeStruct((R, C), out_dtype),
        compiler_params=_params(("parallel",)),
    )(parts)


def _adamw(w, g, m, v, *, name, rows=128):
    R, C = w.shape
    tr = _tile(R, rows, 8)
    c1 = 1.0 - ADAM_B1 ** ADAM_STEP
    c2 = 1.0 - ADAM_B2 ** ADAM_STEP

    def body(w_ref, g_ref, m_ref, v_ref, d_ref, nm_ref, nv_ref):
        gv = g_ref[...]
        nm = ADAM_B1 * m_ref[...] + (1.0 - ADAM_B1) * gv
        nv = ADAM_B2 * v_ref[...] + (1.0 - ADAM_B2) * (gv * gv)
        nm_ref[...] = nm
        nv_ref[...] = nv
        d_ref[...] = -ADAM_LR * ((nm / c1) / (jnp.sqrt(nv / c2) + ADAM_EPS) + ADAM_WD * w_ref[...])

    blk = pl.BlockSpec((tr, C), lambda i: (i, 0))
    return pl.pallas_call(
        body, name=name, grid=(R // tr,), in_specs=[blk] * 4, out_specs=(blk,) * 3,
        out_shape=(jax.ShapeDtypeStruct((R, C), F32),) * 3,
        compiler_params=_params(("parallel",)),
    )(w, g, m, v)


def _one_hot(index, n):
    return jnp.broadcast_to((jnp.arange(n) == index).astype(F32)[:, None, None], (n, 8, LANES))


def _is_set(flags_ref, s):
    return flags_ref[s, 0:1, 0:1] > 0.5


def _rows_for(h, width, itemsize, n_stacked, budget, mult):
    return _tile(h, max(mult, budget // (n_stacked * width * itemsize)), mult)


def _pair_sum(pieces, recv, core_index, *, name):
    _, H, C = recv.shape
    tr = _rows_for(H, C, 2, 1, 2 << 20, 16)
    nh = H // tr
    halves_lead = pieces.ndim == 4

    def body(c_ref, mine_ref, r_ref, o_ref):
        mine = mine_ref[0, 0] if halves_lead else mine_ref[0]
        o_ref[0] = (mine.astype(F32) + r_ref[0].astype(F32)).astype(BF16)

    blk = pl.BlockSpec((1, tr, C), lambda j, i, c: (j, i, 0))
    if halves_lead:
        mine_spec = pl.BlockSpec((1, 1, tr, C), lambda j, i, c: (c[0], j, i, 0))
    else:
        mine_spec = pl.BlockSpec((1, tr, C), lambda j, i, c: (j, c[0] * nh + i, 0))
    return pl.pallas_call(
        body, name=name,
        grid_spec=pltpu.PrefetchScalarGridSpec(num_scalar_prefetch=1, grid=(4, nh), in_specs=[mine_spec, blk],
                                               out_specs=blk),
        out_shape=jax.ShapeDtypeStruct((4, H, C), BF16),
        compiler_params=_params(("parallel", "parallel")),
    )(core_index, pieces, recv)


def _chip_sum(pair, landed, chip_flags, *, name):
    _, H, C = pair.shape
    tr = _rows_for(H, C, 2, 4, 8 << 20, 16)

    def body(p_ref, l_ref, chip_ref, o_ref):
        acc = None
        for s in range(4):
            part = jnp.where(_is_set(chip_ref, s), p_ref[s], l_ref[s]).astype(F32)
            acc = part if acc is None else acc + part
        o_ref[...] = acc

    blk = pl.BlockSpec((4, tr, C), lambda i: (0, i, 0))
    return pl.pallas_call(
        body, name=name, grid=(H // tr,),
        in_specs=[blk, blk, pl.BlockSpec((4, 8, LANES), lambda i: (0, 0, 0))],
        out_specs=pl.BlockSpec((tr, C), lambda i: (i, 0)), out_shape=jax.ShapeDtypeStruct((H, C), F32),
        compiler_params=_params(("parallel",)),
    )(pair, landed, chip_flags)


def _adamw_layer(w, m, v, total, recv, core, layer, prev, *, name, col_halves=False, after=()):
    _, R, C = w.shape
    H, wd = total.shape
    tr = _rows_for(H, wd, 4, 1, 2 << 20, 8)
    nh = H // tr
    c1 = 1.0 - ADAM_B1 ** ADAM_STEP
    c2 = 1.0 - ADAM_B2 ** ADAM_STEP
    n_prev = 0 if prev is None else 4
    after = tuple(after)

    def body(*refs):
        w_ref, m_ref, v_ref, t_ref, r_ref, core_ref = refs[:6]
        g_ref, d_ref, nm_ref, nv_ref = refs[6 + n_prev + len(after):]
        half_is_mine = jnp.where(pl.program_id(0) == 0, core_ref[0, 0:1, 0:1], core_ref[1, 0:1, 0:1]) > 0.5
        gv = jnp.where(half_is_mine, t_ref[...], r_ref[...])
        nm = ADAM_B1 * m_ref[0] + (1.0 - ADAM_B1) * gv
        nv = ADAM_B2 * v_ref[0] + (1.0 - ADAM_B2) * (gv * gv)
        g_ref[0] = gv
        nm_ref[0] = nm
        nv_ref[0] = nv
        d_ref[0] = -ADAM_LR * ((nm / c1) / (jnp.sqrt(nv / c2) + ADAM_EPS) + ADAM_WD * w_ref[0])

    if col_halves:
        lay = pl.BlockSpec((1, tr, wd), lambda hf, i: (layer, i, hf))
    else:
        lay = pl.BlockSpec((1, tr, wd), lambda hf, i: (layer, hf * nh + i, 0))
    one = pl.BlockSpec((tr, wd), lambda hf, i: (i, 0))
    operands = [w, m, v, total, recv, core] + ([] if prev is None else list(prev)) + list(after)
    return pl.pallas_call(
        body, name=name, grid=(2, nh),
        in_specs=[lay, lay, lay, one, one, pl.BlockSpec((2, 8, LANES), lambda hf, i: (0, 0, 0))]
        + [ANY] * (n_prev + len(after)),
        out_specs=(lay,) * 4, out_shape=(jax.ShapeDtypeStruct((2, R, C), F32),) * 4,
        input_output_aliases={6 + i: i for i in range(n_prev)},
        compiler_params=_params(("parallel", "parallel")),
    )(*operands)


ANY = pl.BlockSpec(memory_space=pl.ANY)


def _coords():
    return lax.axis_index("x"), lax.axis_index("y"), lax.axis_index("c")


HBM = pl.BlockSpec(memory_space=pltpu.HBM)
SEM = pl.BlockSpec(memory_space=pltpu.SEMAPHORE)
EFFECT = pltpu.SideEffectType.DATAFLOW_SIDE_EFFECTING


def _copies(plan, bufs, send_sems, recv_sems):
    return [pltpu.make_async_remote_copy(src_ref=s, dst_ref=d, send_sem=send_sems.at[i], recv_sem=recv_sems.at[i],
                                         device_id=to, device_id_type=MESH)
            for i, (s, d, to) in enumerate(plan(bufs))]


def _start_copies(bufs, groups, *, name):
    nb, ng = len(bufs), len(groups)

    def body(*refs):
        buf_refs = refs[:nb]
        sems = refs[nb:nb + 2 * ng]
        token = refs[-1]
        for g, (plan, _) in enumerate(groups):
            for cp in _copies(plan, buf_refs, sems[2 * g], sems[2 * g + 1]):
                cp.start()
        token[...] = jnp.zeros_like(token)

    sem_shapes = []
    for _, n in groups:
        sem_shapes += [pltpu.SemaphoreType.DMA((n,)), pltpu.SemaphoreType.DMA((n,))]
    out = pl.pallas_call(
        body, name=name, in_specs=[HBM] * nb,
        out_specs=tuple([SEM] * (2 * ng) + [HBM] * nb + [pl.BlockSpec(memory_space=pltpu.VMEM)]),
        out_shape=tuple(sem_shapes + [pltpu.HBM(b.shape, b.dtype) for b in bufs] + [jax.ShapeDtypeStruct((8, LANES), F32)]),
        input_output_aliases={i: 2 * ng + i for i in range(nb)},
        compiler_params=pltpu.CompilerParams(has_side_effects=EFFECT),
    )(*[pltpu.with_memory_space_constraint(b, pltpu.HBM) for b in bufs])
    sems = [(out[2 * g], out[2 * g + 1]) for g in range(ng)]
    return sems, list(out[2 * ng:2 * ng + nb]), out[-1]


def _wait_copies(bufs, sems, plan, after, *, name):
    nb = len(bufs)

    def body(*refs):
        buf_refs = refs[:nb]
        for cp in _copies(plan, buf_refs, refs[nb], refs[nb + 1]):
            cp.wait_send()
            cp.wait_recv()

    out = pl.pallas_call(
        body, name=name, in_specs=[HBM] * nb + [SEM, SEM, ANY], out_specs=tuple([HBM] * nb),
        out_shape=tuple(pltpu.HBM(b.shape, b.dtype) for b in bufs),
        input_output_aliases={i: i for i in range(nb)},
        compiler_params=pltpu.CompilerParams(has_side_effects=EFFECT),
    )(*bufs, sems[0], sems[1], after)
    return list(out)


def _half(ref, c):
    h = ref.shape[0] // 2
    return ref.at[pl.ds(c * h, h)]


def _ici_gather_plan(pairs):
    def plan(refs):
        x, y, c = _coords()
        me = 2 * x + y
        out = []
        for s, d in pairs:
            for cx, cy in [(1 - x, y), (x, 1 - y), (1 - x, 1 - y)]:
                out.append((_half(refs[s], c), _half(refs[d].at[me], c), (cx, cy, c)))
            out.append((refs[s], refs[d].at[me], (x, y, 1 - c)))
        return out
    return plan, 4 * len(pairs)


def _d2d_forward_plan(lands):
    def plan(refs):
        x, y, c = _coords()
        out = []
        for d in lands:
            for cx, cy in [(1 - x, y), (x, 1 - y), (1 - x, 1 - y)]:
                got = _half(refs[d].at[2 * cx + cy], c)
                out.append((got, got, (x, y, 1 - c)))
        return out
    return plan, 3 * len(lands)


def _swap_half_plan(pairs):
    def plan(refs):
        x, y, c = _coords()
        out = []
        for s, d in pairs:
            h = refs[d].shape[1]
            other = refs[s].at[1 - c] if len(refs[s].shape) == 4 else refs[s].at[:, pl.ds((1 - c) * h, h)]
            out.append((other, refs[d], (x, y, 1 - c)))
        return out
    return plan, len(pairs)


def _scatter_plan(pairs):
    def plan(refs):
        x, y, c = _coords()
        me = 2 * x + y
        out = []
        for s, d in pairs:
            for cx, cy in [(1 - x, y), (x, 1 - y), (1 - x, 1 - y)]:
                out.append((refs[s].at[2 * cx + cy], refs[d].at[me], (cx, cy, c)))
        return out
    return plan, 3 * len(pairs)


def _swap_total_plan(pairs):
    def plan(refs):
        x, y, c = _coords()
        return [(refs[s], refs[d], (x, y, 1 - c)) for s, d in pairs]
    return plan, len(pairs)


def _gather_all(block, *, name, after=()):
    after = tuple(after)

    def body(src, *rest):
        out, send_sems, recv_sems, local_sem = rest[len(after):]
        x, y, c = _coords()
        me = 4 * x + 2 * y + c
        flips = [(fx, fy, fc) for fx in (0, 1) for fy in (0, 1) for fc in (0, 1)][1:]
        mine = pltpu.make_async_copy(src, out.at[me], local_sem)
        mine.start()
        peers = [(x ^ fx, y ^ fy, c ^ fc) for fx, fy, fc in flips]
        cps = [pltpu.make_async_remote_copy(src_ref=src, dst_ref=out.at[me], send_sem=send_sems.at[k],
                                            recv_sem=recv_sems.at[k], device_id=peer, device_id_type=MESH)
               for k, peer in enumerate(peers)]
        for cp in cps:
            cp.start()
        for k, (px, py, pc) in enumerate(peers):
            slot = out.at[4 * px + 2 * py + pc]
            pltpu.make_async_remote_copy(src_ref=slot, dst_ref=slot, send_sem=send_sems.at[k], recv_sem=recv_sems.at[k],
                                         device_id=(px, py, pc), device_id_type=MESH).wait_recv()
        for cp in cps:
            cp.wait_send()
        mine.wait()

    return pl.pallas_call(
        body, name=name, in_specs=[ANY] * (1 + len(after)), out_specs=ANY,
        out_shape=jax.ShapeDtypeStruct((8,) + block.shape, block.dtype),
        scratch_shapes=[pltpu.SemaphoreType.DMA((7,)), pltpu.SemaphoreType.DMA((7,)), pltpu.SemaphoreType.DMA],
    )(block, *after)


def _cols(o):
    return jnp.transpose(o, (1, 0, 2)).reshape(o.shape[1], -1)


def _uncols(full):
    return jnp.transpose(full.reshape(full.shape[0], 4, -1), (1, 0, 2))


def _rope_pad(x1, x2):
    z = jnp.zeros_like(x1)
    return jnp.concatenate([x1, z, x2, z], axis=-1)


def _head_pad(w, heads):
    r = w.reshape(w.shape[0], heads, QK_HEAD)
    half = QK_ROPE // 2
    out = jnp.concatenate([r[..., :QK_NOPE], _rope_pad(r[..., QK_NOPE:QK_NOPE + half], r[..., QK_NOPE + half:])], axis=-1)
    return out.reshape(w.shape[0], heads * HEAD_PAD)


def _head_unpad(w, heads):
    r = w.reshape(w.shape[0], heads, HEAD_PAD)
    half = QK_ROPE // 2
    out = jnp.concatenate([r[..., :QK_NOPE], r[..., QK_NOPE:QK_NOPE + half],
                           r[..., QK_NOPE + 2 * half:QK_NOPE + 3 * half]], axis=-1)
    return out.reshape(w.shape[0], heads * QK_HEAD)


class _Dims:
    def __init__(self, d, seq):
        self.d = d
        self.seq = seq
        self.t_real = N_META + seq
        self.t = -(-self.t_real // LANES) * LANES
        self.dc = d // 2
        self.dp = d // 2
        self.pg = self.dp // len(POOL_WINDOWS)
        self.heads = d // 128
        self.dff = 4 * d
        self.a_end = 3 * self.dc
        self.q_end = self.a_end + Q_LORA
        self.kv_end = self.q_end + KV_LORA
        self.kr_end = self.kv_end + QK_ROPE
        self.pool_end = self.kr_end + self.dp
        self.d_in = self.pool_end + 3 * d
        self.r_pool = 3 * self.dc
        self.r_q = self.r_pool + self.dp
        self.r_kv = self.r_q + Q_LORA
        self.r_kr = self.r_kv + KV_LORA
        self.r_width = self.r_kr + HEAD_PAD


def _split_cols(a):
    return jnp.moveaxis(a.reshape(a.shape[:-1] + (2, a.shape[-1] // 2)), -2, -3)


def _join_cols(a):
    a = jnp.moveaxis(a, -3, -2)
    return a.reshape(a.shape[:-2] + (a.shape[-2] * a.shape[-1],))


def _in_weights(dm, pieces):
    w_t = _join_cols(pieces).reshape(dm.d_in, dm.d)
    half = QK_ROPE // 2
    kr = w_t[dm.kv_end:dm.kr_end]
    zeros = jnp.zeros((half, dm.d), BF16)
    kr_p = jnp.concatenate([kr[:half], zeros, kr[half:], zeros, jnp.zeros((HEAD_PAD - LANES, dm.d), BF16)], axis=0)
    return dict(
        wg_t=w_t[dm.pool_end:],
        wr_t=jnp.concatenate([w_t[:dm.a_end], w_t[dm.kr_end:dm.pool_end], w_t[dm.a_end:dm.kv_end], kr_p], axis=0))


def _other_weights(dm, g):
    out = {}
    if "w_ukv" in g:
        w_ukv = _cols(g["w_ukv"]).reshape(KV_LORA, dm.heads, QK_NOPE + V_HEAD)
        out["wkn"] = w_ukv[:, :, :QK_NOPE].reshape(KV_LORA, dm.heads * QK_NOPE)
        out["wv"] = w_ukv[:, :, QK_NOPE:].reshape(KV_LORA, dm.heads * V_HEAD)
    if "w_uq" in g:
        out["wuq"] = _head_pad(_cols(g["w_uq"]), dm.heads)
    if "pool_w" in g:
        out["wp"] = jnp.transpose(g["pool_w"], (1, 0, 2, 3)).reshape(len(POOL_WINDOWS), dm.pg, dm.pg)
    for name, key in (("w_branch_a", "wba"), ("w_branch_c", "wbc"), ("w_up", "wup")):
        if name in g:
            out[key] = _cols(g[name])
    for name, key in (("w_branch_b", "wbb"), ("w_o", "wo"), ("w_down", "wdn")):
        if name in g:
            out[key] = g[name].reshape(-1, dm.d)
    return out


def _small_weights(small):
    return dict(
        conv_w=small["conv_w"],
        attn_norm=small["attn_norm"][None], mlp_norm=small["mlp_norm"][None],
        q_lat_norm=small["q_lat_norm"][None], kv_lat_norm=small["kv_lat_norm"][None],
        q_norm=_head_pad(small["q_norm"][None], 1), k_norm=_head_pad(small["k_norm"][None], 1),
        pool_scale=small["pool_scale"][None],
    )


def _grad_piece(dm, dw, name):
    half = QK_ROPE // 2
    rows = lambda a: a.reshape((4, a.shape[0] // 4) + a.shape[1:])
    if name == "w_in":
        dwr, dwg = dw["wr_t"], dw["wg_t"]
        d_t = jnp.concatenate([
            dwr[:, :dm.r_pool], dwr[:, dm.r_q:dm.r_kr], dwr[:, dm.r_kr:dm.r_kr + half],
            dwr[:, dm.r_kr + 2 * half:dm.r_kr + 3 * half], dwr[:, dm.r_pool:dm.r_q], dwg], axis=1)
        out = d_t.reshape(2, 4, d_t.shape[1] // 4, d_t.shape[2])
    elif name == "w_ukv":
        out = _uncols(jnp.concatenate([dw["wkn"].reshape(KV_LORA, dm.heads, QK_NOPE),
                                       dw["wv"].reshape(KV_LORA, dm.heads, V_HEAD)], axis=-1).reshape(KV_LORA, -1))
    elif name == "w_uq":
        out = _uncols(_head_unpad(dw["wuq"], dm.heads))
    elif name == "pool_w":
        out = jnp.transpose(dw["wp"].reshape(len(POOL_WINDOWS), 4, dm.pg // 4, dm.pg), (1, 0, 2, 3))
    elif name in ("w_branch_a", "w_branch_c", "w_up"):
        out = dw[{"w_branch_a": "wba", "w_branch_c": "wbc", "w_up": "wup"}[name]]
    else:
        out = rows(dw[{"w_branch_b": "wbb", "w_o": "wo", "w_down": "wdn"}[name]])
    return out.astype(BF16)


def _layer_fwd(dm, W, x, cos_t, sin_t, tag, more=None, h=None):
    n = lambda s: f"{s}_{tag}"
    if h is None:
        h = _rms_fwd(x, W["attn_norm"], name=n("attn_norm"))
    gl = _mm(h, W["wg_t"], name=n("proj_gates"), tb=True, out_dtype=BF16)
    rest = _mm(h, W["wr_t"], name=n("proj_rest"), tb=True)
    if more is not None:
        W.update(more("after_proj", rest))
    y_a = _conv_fwd(rest, W["conv_w"], name=n("conv"), dc=dm.dc)
    y_c = _pool_fwd(rest, W["wp"], W["pool_scale"], name=n("pool"), seg0=dm.r_pool // dm.pg, pg=dm.pg)
    q_lat = _rms_fwd(rest, W["q_lat_norm"], name=n("q_lat_norm"), width=Q_LORA, seg=dm.r_q // Q_LORA)
    kv_lat = _rms_fwd(rest, W["kv_lat_norm"], name=n("kv_lat_norm"), width=KV_LORA, seg=dm.r_kv // KV_LORA)
    q_raw = _mm(q_lat, W["wuq"], name=n("up_q"), out_dtype=BF16)
    k_nope = _mm(kv_lat, W["wkn"], name=n("up_k"), out_dtype=BF16)
    v = _mm(kv_lat, W["wv"], name=n("up_v"), out_dtype=BF16)
    q, k = _qk_fwd(q_raw, k_nope, rest, cos_t, sin_t, W["q_norm"], W["k_norm"], name=n("qk_norm_rope"),
                   heads=dm.heads, kr_seg=dm.r_kr // HEAD_PAD)
    y_b, lse = _flash_fwd(q, k, v, name=n("attention"), heads=dm.heads)
    if more is not None:
        W.update(more("after_attention", y_b))
    pa = _mm(y_a, W["wba"], name=n("branch_a"), out_dtype=BF16)
    pb = _mm(y_b, W["wbb"], name=n("branch_b"), out_dtype=BF16, after=W.pop("pin", ()))
    pc = _mm(y_c, W["wbc"], name=n("branch_c"), out_dtype=BF16)
    merged = _merge_fwd(gl, pa, pb, pc, name=n("merge"), d=dm.d)
    x1 = _mm(merged, W["wo"], name=n("out_proj"), add=x)
    h2 = _rms_fwd(x1, W["mlp_norm"], name=n("mlp_norm"))
    if more is not None:
        W.update(more("before_mlp", h2))
    up, act = _mm(h2, W["wup"], name=n("mlp_up"), epi="relu2")
    x2 = _mm(act, W["wdn"], name=n("mlp_down"), add=x1, tm=704, tk=4096)
    saved = dict(x=x, h=h, gl=gl, rest=rest, y_a=y_a, y_c=y_c, q_lat=q_lat, kv_lat=kv_lat, q_raw=q_raw, k_nope=k_nope,
                 v=v, q=q, k=k, y_b=y_b, lse=lse, pa=pa, pb=pb, pc=pc, merged=merged, x1=x1, h2=h2, up=up, act=act)
    return x2, saved


def _layer_bwd(dm, W, S, dx2, dx2_b, cos_t, sin_t, tag, hook=None):
    n = lambda s: f"{s}_{tag}"
    dw, ds = {}, {}
    if hook is None:
        hook = lambda point, t, dw_so_far: ()
    dup = _mm(dx2_b, W["wdn"], name=n("d_mlp_down"), tb=True, aux=S["up"], epi="drelu2", out_dtype=BF16,
              after=hook("start", dx2, dw))
    dw["wdn"] = _mm(S["act"], dx2_b, name=n("dw_mlp_down"), ta=True, tm=512, out_dtype=BF16)
    dh2 = _mm(dup, W["wup"], name=n("d_mlp_up"), tb=True, tm=704, tk=4096)
    dw["wup"] = _mm(S["h2"], dup, name=n("dw_mlp_up"), ta=True, tm=512, out_dtype=BF16, pieces=4)
    dx1, dx1_b, ds["mlp_norm"] = _rms_bwd(dh2, S["x1"], W["mlp_norm"], name=n("d_mlp_norm"), res=dx2, bf16_copy=True)
    dmerged = _mm(dx1_b, W["wo"], name=n("d_out_proj"), tb=True, after=hook("after_mlp", dx1, dw))
    dw["wo"] = _mm(S["merged"], dx1_b, name=n("dw_out_proj"), ta=True, tm=512, out_dtype=BF16)
    dpa, dpb, dpc, dg0, dg1, dg2 = _merge_bwd(dmerged, S["gl"], S["pa"], S["pb"], S["pc"], name=n("d_merge"), d=dm.d)
    dgl = jnp.concatenate([dg0, dg1, dg2], axis=1)
    dy_a = _mm(dpa, W["wba"], name=n("d_branch_a"), tb=True)
    dw["wba"] = _mm(S["y_a"], dpa, name=n("dw_branch_a"), ta=True, tm=512, out_dtype=BF16, pieces=4)
    dy_b = _mm(dpb, W["wbb"], name=n("d_branch_b"), tb=True, out_dtype=BF16)
    dw["wbb"] = _mm(S["y_b"], dpb, name=n("dw_branch_b"), ta=True, tm=512, out_dtype=BF16)
    dy_c = _mm(dpc, W["wbc"], name=n("d_branch_c"), tb=True)
    dw["wbc"] = _mm(S["y_c"], dpc, name=n("dw_branch_c"), ta=True, tm=512, out_dtype=BF16, pieces=4)
    dq, dk, dv = _flash_bwd(S["q"], S["k"], S["v"], S["y_b"], dy_b, S["lse"], name=n("d_attention"), heads=dm.heads,
                            after=hook("before_attention", dw["wbc"], dw))
    after_attention = hook("after_attention", dq, dw)
    dq_raw, dk_nope, dk_rope, dgq, dgk = _qk_bwd(
        dq, dk, S["q_raw"], S["k_nope"], S["rest"], cos_t, sin_t, W["q_norm"], W["k_norm"], name=n("d_qk_norm_rope"),
        heads=dm.heads, kr_seg=dm.r_kr // HEAD_PAD, after=after_attention)
    ds["q_norm"] = _head_unpad(dgq, 1)
    ds["k_norm"] = _head_unpad(dgk, 1)
    dkv_v = _mm(dv, W["wv"], name=n("d_up_v"), tb=True)
    dq_lat_n = _mm(dq_raw, W["wuq"], name=n("d_up_q"), tb=True, after=hook("after_qk", dq_raw, dw))
    dw["wuq"] = _mm(S["q_lat"], dq_raw, name=n("dw_up_q"), ta=True, tm=512)
    dkv_lat_n = _mm(dk_nope, W["wkn"], name=n("d_up_k"), tb=True, add=dkv_v)
    dw["wkn"] = _mm(S["kv_lat"], dk_nope, name=n("dw_up_k"), ta=True, tm=512)
    dw["wv"] = _mm(S["kv_lat"], dv, name=n("dw_up_v"), ta=True, tm=512)
    dq_lat, ds["q_lat_norm"] = _rms_bwd(dq_lat_n, S["rest"], W["q_lat_norm"], name=n("d_q_lat_norm"), width=Q_LORA,
                                        seg=dm.r_q // Q_LORA, out_dtype=BF16)
    dkv_lat, ds["kv_lat_norm"] = _rms_bwd(dkv_lat_n, S["rest"], W["kv_lat_norm"], name=n("d_kv_lat_norm"), width=KV_LORA,
                                          seg=dm.r_kv // KV_LORA, out_dtype=BF16)
    du, db, dc, ds["conv_w"] = _conv_bwd(S["rest"], W["conv_w"], dy_a, name=n("d_conv"), dc=dm.dc)
    dpool, dw["wp"], ds["pool_scale"] = _pool_bwd(S["rest"], W["wp"], W["pool_scale"], dy_c, name=n("d_pool"),
                                                  seg0=dm.r_pool // dm.pg, pg=dm.pg)
    drest = jnp.concatenate([du, db, dc, dpool, dq_lat, dkv_lat, dk_rope], axis=1)
    dw["wg_t"] = _mm(dgl, S["h"], name=n("dw_proj_gates"), ta=True, tm=512, out_dtype=BF16, pieces=2)
    dw["wr_t"] = _mm(drest, S["h"], name=n("dw_proj_rest"), ta=True, tm=512, out_dtype=BF16, pieces=2)
    dh_g = _mm(dgl, W["wg_t"], name=n("d_proj_gates"), tm=704, tk=3072, after=hook("after_dw_in", dw["wr_t"], dw))
    dh = _mm(drest, W["wr_t"], name=n("d_proj_rest"), add=dh_g, tm=704, tk=2688, after=hook("after_dh_gates", dh_g, dw))
    dx, dx_b, ds["attn_norm"] = _rms_bwd(dh, S["x"], W["attn_norm"], name=n("d_attn_norm"), res=dx1, bf16_copy=True)
    return dx, dx_b, dw, ds


BIG = ("w_in", "w_uq", "w_ukv", "pool_w", "w_branch_a", "w_branch_b", "w_branch_c", "w_o", "w_up", "w_down")
REPLICATED = ("attn_norm", "q_lat_norm", "kv_lat_norm", "q_norm", "k_norm", "pool_scale", "mlp_norm")
WEIGHTS = ("meta_tokens", "attn_norm", "w_in", "conv_w", "q_lat_norm", "kv_lat_norm", "w_uq", "w_ukv", "q_norm",
           "k_norm", "pool_w", "pool_scale", "w_branch_a", "w_branch_b", "w_branch_c", "w_o", "mlp_norm", "w_up",
           "w_down")


def _pack(arrays):
    flat = jnp.concatenate([a.reshape(-1).astype(F32) for a in arrays])
    pad = (-flat.shape[0]) % (8 * LANES)
    return jnp.pad(flat, (0, pad)).reshape(-1, LANES)


def _unpack(flat, shapes):
    out, pos = [], 0
    flat = flat.reshape(-1)
    for shp in shapes:
        size = math.prod(shp)
        out.append(flat[pos:pos + size].reshape(shp))
        pos += size
    return out


def _update(w, g, m, v, name):
    shp = w.shape
    to2 = lambda a: a.reshape(-1, shp[-1])
    delta, nm, nv = _adamw(to2(w), to2(g), to2(m), to2(v), name=name)
    return delta.reshape(shp), nm.reshape(shp), nv.reshape(shp)


def _step(args):
    x = args["x"][0]
    seq, d = x.shape
    dm = _Dims(d, seq)
    xi, yi, ci = _coords()
    chip = 2 * xi + yi

    small_w = _gather_all(_pack([args["conv_w"], args["meta_tokens"]]), name="gather_small_weights")
    args = dict(args)
    for p in ("", "m_", "v_"):
        args[p + "w_in"] = jnp.swapaxes(args[p + "w_in"], 1, 2)
    order = [(k, l) for l in range(2) for k in BIG]
    last = ("w_up", "w_down")
    group_names = [[("w_in", 0)], [(k, 0) for k in BIG[1:] if k not in last], [(k, 0) for k in last],
                   [(k, 1) for k in BIG]]
    first, others = order[0], order[1:]
    shards = {first: _split_cols(args["w_in"][0].astype(BF16))}
    lands = {first: lax.empty((4,) + shards[first].shape, BF16)}
    sems, thru, token = _start_copies([shards[first], lands[first], small_w], [_ici_gather_plan([(0, 1)])],
                                      name="start_gather_ici_first")
    shards[first], lands[first], small_w = thru
    zero = token[0, 0]
    for n in others:
        shards[n] = (args[n[0]][n[1]] + zero).astype(BF16)
        if n[0] == "w_in":
            shards[n] = _split_cols(shards[n])
        lands[n] = lax.empty((4,) + shards[n].shape, BF16)
    at = {n: i for i, n in enumerate(others)}
    sems_b, thru, token_b = _start_copies(
        [shards[n] for n in others] + [lands[n] for n in others] + [token],
        [_ici_gather_plan([(at[n], len(others) + at[n]) for n in g]) for g in group_names[1:]], name="start_gather_ici")
    sems = sems + sems_b
    for i, n in enumerate(others):
        shards[n], lands[n] = thru[i], thru[len(others) + i]

    def finish_gather(g, after, tag):
        names = group_names[g]
        k = len(names)
        plan, _ = _ici_gather_plan([(i, k + i) for i in range(k)])
        got = _wait_copies([shards[n] for n in names] + [lands[n] for n in names], sems[g], plan, after,
                           name=f"wait_gather_ici_{tag}")
        for i, n in enumerate(names):
            shards[n] = got[i]
        fwd = _d2d_forward_plan(list(range(k)))
        sems2, bufs2, tok2 = _start_copies(got[k:], [fwd], name=f"start_gather_d2d_{tag}")
        return names, bufs2, sems2[0], fwd[0], tok2

    def land_gather(pending, after, tag):
        names, bufs2, sems2, plan, tok2 = pending
        done = _wait_copies(bufs2, sems2, plan, tok2 if after is None else after, name=f"wait_gather_d2d_{tag}")
        return {n[0]: buf for n, buf in zip(names, done)}

    conv_shape, meta_shape = args["conv_w"].shape, args["meta_tokens"].shape
    per_chip = [_unpack(small_w[2 * j], [conv_shape, meta_shape]) for j in range(4)]
    conv_full = jnp.concatenate([p[0] for p in per_chip], axis=-1)
    meta_full = jnp.concatenate([p[1] for p in per_chip], axis=-1)

    layers = []
    for l in range(2):
        small = {k: args[k][l] for k in REPLICATED}
        small["conv_w"] = conv_full[l]
        layers.append(_small_weights(small))

    pos = jnp.arange(dm.t, dtype=F32)
    inv = ROPE_THETA ** (-jnp.arange(0, QK_ROPE, 2, dtype=F32) / QK_ROPE)
    ang = pos[:, None] * inv[None, :]
    cos_t = _rope_pad(jnp.cos(ang), jnp.cos(ang))
    sin_t = _rope_pad(-jnp.sin(ang), jnp.sin(ang))
    tail = jnp.zeros((dm.t - dm.t_real, d), F32) + zero
    h0 = jnp.concatenate([meta_full, x, tail], axis=0)
    target = jnp.concatenate([jnp.zeros((N_META, d), F32), args["loss_target"][0], tail], axis=0)

    h_first = _rms_fwd(h0, layers[0]["attn_norm"], name="attn_norm_l0", after=(token, token_b))
    layers[0].update(_in_weights(dm, land_gather(finish_gather(0, h_first, "l0_in"), None, "l0_in")["w_in"]))
    pending = {}

    def rest_of_layer0(point, after):
        if point == "after_proj":
            return _other_weights(dm, land_gather(finish_gather(1, after, "l0_mid"), None, "l0_mid"))
        if point == "after_attention":
            pending["mlp"] = finish_gather(2, after, "l0_mlp")
            return {"pin": (pending["mlp"][4],)}
        return _other_weights(dm, land_gather(pending["mlp"], after, "l0_mlp"))

    h1, saved0 = _layer_fwd(dm, layers[0], h0, cos_t, sin_t, "l0", more=rest_of_layer0, h=h_first)
    g1 = land_gather(finish_gather(3, saved0["y_b"], "l1"), h1, "l1")
    layers[1].update(_in_weights(dm, g1["w_in"]))
    layers[1].update(_other_weights(dm, g1))
    h2, saved1 = _layer_fwd(dm, layers[1], h1, cos_t, sin_t, "l1")
    sq, dy, dy_b = _loss(h2, target, name="loss_head", first=N_META, last=dm.t_real)
    loss = lax.psum(0.5 / d * sq[0, 0], ("x", "y", "c"))
    core, chip_flags = _one_hot(ci, 2), _one_hot(chip, 4)
    core_index = jnp.reshape(ci, (1,)).astype(jnp.int32)

    class Reduce:
        def __init__(self, names, dw, tag):
            self.names, self.tag, self.nb = names, tag, len(names)
            self.idx = [(i, self.nb + i) for i in range(self.nb)]
            parts = [_grad_piece(dm, dw, k) for k in names]
            parts = [p if k == "w_in" else _as3d(p) for p, k in zip(parts, names)]
            recv = [lax.empty((4,) + p.shape[2:] if k == "w_in" else (4, p.shape[1] // 2, p.shape[2]), BF16)
                    for p, k in zip(parts, names)]
            self.plan = _swap_half_plan(self.idx)
            self.sems, self.bufs, self.token = _start_copies(parts + recv, [self.plan], name=f"start_swap_{tag}")

        def _land(self, after, what):
            return _wait_copies(self.bufs, self.sems[0], self.plan[0], self.token if after is None else after,
                                name=f"wait_{what}_{self.tag}")

        def scatter(self, after=None):
            got = self._land(after, "swap")
            pairs = [_pair_sum(got[i], got[j], core_index, name=f"pair_sum_{k}_{self.tag}")
                     for (i, j), k in zip(self.idx, self.names)]
            self.plan = _scatter_plan(self.idx)
            self.sems, self.bufs, self.token = _start_copies(pairs + [lax.empty(p.shape, BF16) for p in pairs],
                                                             [self.plan], name=f"start_scatter_{self.tag}")
            return self.token

        def totals(self, after=None):
            got = self._land(after, "scatter")
            sums = [_chip_sum(got[i], got[j], chip_flags, name=f"chip_sum_{k}_{self.tag}")
                    for (i, j), k in zip(self.idx, self.names)]
            self.plan = _swap_total_plan(self.idx)
            self.sems, self.bufs, self.token = _start_copies(sums + [lax.empty(t.shape, F32) for t in sums],
                                                             [self.plan], name=f"start_swap_total_{self.tag}")
            return self.token

        def finish(self, after=None):
            got = self._land(after, "swap_total")
            return {k: (got[i], got[j]) for (i, j), k in zip(self.idx, self.names)}

    dh1, dh1_b, dw1, ds1 = _layer_bwd(dm, layers[1], saved1, dy, dy_b, cos_t, sin_t, "l1",
                               hook=lambda point, t, dw: (loss.reshape(1, 1),) if point == "start" else ())
    early = ("w_down", "w_up", "w_o", "w_branch_a", "w_branch_b", "w_branch_c")
    late = tuple(k for k in BIG if k not in early)
    stage = {}

    def during_layer0(point, t, dw):
        if point == "start":
            stage["l1"] = Reduce(BIG, dw1, "l1")
            return (stage["l1"].token,)
        if point == "after_mlp":
            return (stage["l1"].scatter(after=t),)
        if point == "before_attention":
            stage["l0a"] = Reduce(early, dw, "l0a")
            return (stage["l0a"].token,)
        if point == "after_attention":
            return (stage["l1"].totals(after=t), stage["l0a"].scatter(after=t))
        if point == "after_qk":
            stage["red1"] = stage["l1"].finish(after=t)
            return ()
        if point == "after_dw_in":
            tok = stage["l0a"].totals(after=t)
            stage["l0b"] = Reduce(late, dw, "l0b")
            return (tok, stage["l0b"].token)
        return (stage["l0b"].scatter(after=t),)

    dh0, _, dw0, ds0 = _layer_bwd(dm, layers[0], saved0, dh1, dh1_b, cos_t, sin_t, "l0", hook=during_layer0)
    grad_x = dh0[N_META:dm.t_real][None]
    red1 = stage["red1"]
    grads, delta, new_m, new_v = {}, {}, {}, {}

    def adamw_big(k, layer, red, prev, after):
        shp = args[k].shape
        wmv = [args[p + k].reshape(2, -1, shp[-1]) for p in ("", "m_", "v_")]
        return _adamw_layer(*wmv, *red[k], core, layer, prev, name=f"adamw_{k}_l{layer}", col_halves=k == "w_in",
                            after=after)

    def keep(k, out):
        shp = args[k].shape
        out = [o.reshape(shp) for o in out]
        grads[k], delta[k], new_m[k], new_v[k] = [jnp.swapaxes(o, 1, 2) for o in out] if k == "w_in" else out

    half_done = {}
    pin = dh0
    for k in BIG:
        half_done[k] = adamw_big(k, 1, red1, None, (pin,))
        pin = half_done[k][0]
    red0a = stage["l0a"].finish(after=pin)
    for k in early:
        out = adamw_big(k, 0, red0a, half_done[k], ())
        keep(k, out)
        pin = out[0]

    small_names = REPLICATED + ("conv_w",)
    small_parts = [jnp.stack([ds0[k].reshape(ds0[k].shape[-2:] if k == "conv_w" else (-1,)),
                              ds1[k].reshape(ds1[k].shape[-2:] if k == "conv_w" else (-1,))]) for k in small_names]
    small_parts.append(dh0[:N_META])
    small_all = _gather_all(_pack(small_parts), name="gather_small_grads", after=(pin,))
    small_sum = _sum_stack(small_all, name="sum_small_grads", out_dtype=F32)
    small_g = dict(zip(small_names + ("meta_tokens",), _unpack(small_sum, [p.shape for p in small_parts])))
    for k in REPLICATED:
        grads[k] = small_g[k]
    dcw = conv_shape[-1]
    grads["conv_w"] = lax.dynamic_slice_in_dim(small_g["conv_w"], chip * dcw, dcw, axis=2)
    dmeta = meta_shape[-1]
    grads["meta_tokens"] = lax.dynamic_slice_in_dim(small_g["meta_tokens"], chip * dmeta, dmeta, axis=1)

    stage["l0b"].totals(after=small_sum)
    red0b = stage["l0b"].finish()
    for k in late:
        keep(k, adamw_big(k, 0, red0b, half_done[k], ()))
    for k in WEIGHTS:
        if k not in BIG:
            grads[k] = grads[k].reshape(args[k].shape)
            delta[k], new_m[k], new_v[k] = _update(args[k], grads[k], args["m_" + k], args["v_" + k], f"adamw_{k}")
    return (loss, grad_x, *[grads[k] for k in WEIGHTS], *[delta[k] for k in WEIGHTS],
            *[new_m[k] for k in WEIGHTS], *[new_v[k] for k in WEIGHTS])


def kernel(x, meta_tokens, attn_norm, w_in, conv_w, q_lat_norm, kv_lat_norm, w_uq, w_ukv, q_norm, k_norm, pool_w, pool_scale, w_branch_a, w_branch_b, w_branch_c, w_o, mlp_norm, w_up, w_down, loss_target, m_meta_tokens, m_attn_norm, m_w_in, m_conv_w, m_q_lat_norm, m_kv_lat_norm, m_w_uq, m_w_ukv, m_q_norm, m_k_norm, m_pool_w, m_pool_scale, m_w_branch_a, m_w_branch_b, m_w_branch_c, m_w_o, m_mlp_norm, m_w_up, m_w_down, v_meta_tokens, v_attn_norm, v_w_in, v_conv_w, v_q_lat_norm, v_kv_lat_norm, v_w_uq, v_w_ukv, v_q_norm, v_k_norm, v_pool_w, v_pool_scale, v_w_branch_a, v_w_branch_b, v_w_branch_c, v_w_o, v_mlp_norm, v_w_up, v_w_down):
    return _step(dict(locals()))
```

```python
import functools
import math

import jax
import jax.numpy as jnp
from jax import lax
from jax.experimental import pallas as pl
from jax.experimental.pallas import tpu as pltpu

F32 = jnp.float32
BF16 = jnp.bfloat16
MESH = pl.DeviceIdType.MESH

EPS = 1e-6
N_META = 16
QK_NOPE = 128
QK_ROPE = 64
QK_HEAD = QK_NOPE + QK_ROPE
V_HEAD = 128
HEAD_PAD = 256
Q_LORA = 512
KV_LORA = 512
ROPE_THETA = 10000.0
POOL_WINDOWS = (2, 4, 8, 16)
HALO = 16
LANES = 128
ADAM_LR = 0.001
ADAM_B1 = 0.9
ADAM_B2 = 0.999
ADAM_EPS = 1e-08
ADAM_WD = 0.01
ADAM_STEP = 10
VMEM_LIMIT = 52 * 1024 * 1024
NEG = -1e30
ATTN_SCALE = QK_HEAD ** -0.5
LOG2_E = 1.4426950408889634
Q_FOLD = ATTN_SCALE * LOG2_E


def _tile(n, target, mult=LANES):
    best = None
    for t in range(mult, min(n, target) + 1, mult):
        if n % t == 0:
            best = t
    return n if best is None else best


def _params(sem=None):
    return pltpu.CompilerParams(dimension_semantics=sem, vmem_limit_bytes=VMEM_LIMIT)


def _mm(a, b, *, name, ta=False, tb=False, add=None, aux=None, epi=None, out_dtype=F32,
        tm=1056, tn=1024, tk=None, after=(), pieces=None):
    if ta:
        K, M = a.shape
    else:
        M, K = a.shape
    if tb:
        N, kb = b.shape
    else:
        kb, N = b.shape
    assert K == kb, (a.shape, b.shape, ta, tb)
    tm = _tile(M, tm, LANES if ta else 16)
    tn = _tile(N if pieces is None else N // pieces, tn, LANES)
    tk = K if tk is None else _tile(K, tk, LANES if (not ta or tb) else 16)
    nk = K // tk
    a_bytes, b_bytes = a.size * a.dtype.itemsize, b.size * b.dtype.itemsize
    j_outer = nk == 1 and a_bytes * (N // tn) + b_bytes < a_bytes + b_bytes * (M // tm)
    grid = (N // tn, M // tm, nk) if j_outer else (M // tm, N // tn, nk)
    row = (lambda g0, g1: g1) if j_outer else (lambda g0, g1: g0)
    col = (lambda g0, g1: g0) if j_outer else (lambda g0, g1: g1)

    if ta:
        a_spec = pl.BlockSpec((tk, tm), lambda g0, g1, k: (k, row(g0, g1)))
    else:
        a_spec = pl.BlockSpec((tm, tk), lambda g0, g1, k: (row(g0, g1), k))
    if tb:
        b_spec = pl.BlockSpec((tn, tk), lambda g0, g1, k: (col(g0, g1), k))
    else:
        b_spec = pl.BlockSpec((tk, tn), lambda g0, g1, k: (k, col(g0, g1)))
    o_spec = pl.BlockSpec((tm, tn), lambda g0, g1, k: (row(g0, g1), col(g0, g1)))
    per = None if pieces is None else N // pieces // tn
    in_specs = [a_spec, b_spec]
    operands = [a, b]
    if add is not None:
        in_specs.append(o_spec)
        operands.append(add)
    if aux is not None:
        in_specs.append(o_spec)
        operands.append(aux)
    after = tuple(after)
    in_specs += [pl.BlockSpec(memory_space=pl.ANY)] * len(after)
    operands += list(after)
    if epi == "relu2":
        out_shape = (jax.ShapeDtypeStruct((M, N), BF16), jax.ShapeDtypeStruct((M, N), BF16))
        out_specs = (o_spec, o_spec)
    elif pieces is not None:
        out_shape = jax.ShapeDtypeStruct((pieces, M, N // pieces), out_dtype)
        out_specs = pl.BlockSpec((1, tm, tn), lambda g0, g1, k: (col(g0, g1) // per, row(g0, g1), col(g0, g1) % per))
    else:
        out_shape = jax.ShapeDtypeStruct((M, N), out_dtype)
        out_specs = o_spec
    dims =(((0 if ta else 1,), (1 if tb else 0,)), ((), ()))
    has_add, has_aux = add is not None, aux is not None

    def body(*refs):
        a_ref, b_ref = refs[0], refs[1]
        pos = 2
        add_ref = aux_ref = None
        if has_add:
            add_ref = refs[pos]
            pos += 1
        if has_aux:
            aux_ref = refs[pos]
            pos += 1
        pos += len(after)
        n_out = 2 if epi == "relu2" else 1
        out_refs = refs[pos:pos + n_out]
        acc_ref = refs[pos + n_out] if nk > 1 else None

        part = lax.dot_general(a_ref[...].astype(BF16), b_ref[...].astype(BF16), dims,
                               preferred_element_type=F32)

        def finish(acc):
            if has_add:
                acc = acc + add_ref[...].astype(F32)
            if epi == "relu2":
                r = jnp.maximum(acc, 0.0)
                out_refs[0][...] = acc.astype(BF16)
                out_refs[1][...] = (r * r).astype(BF16)
            elif epi == "drelu2":
                u = aux_ref[...].astype(F32)
                out_refs[0][...] = (acc * (2.0 * jnp.maximum(u, 0.0))).astype(out_dtype)
            else:
                out_refs[0][...] = acc.astype(out_dtype).reshape(out_refs[0].shape)

        if nk == 1:
            finish(part)
        else:
            k = pl.program_id(2)

            @pl.when(k == 0)
            def _():
                acc_ref[...] = part

            @pl.when(k > 0)
            def _():
                acc_ref[...] += part

            @pl.when(k == nk - 1)
            def _():
                finish(acc_ref[...])

    scratch = [pltpu.VMEM((tm, tn), F32)] if nk > 1 else []
    return pl.pallas_call(
        body, name=name, grid=grid, in_specs=in_specs, out_specs=out_specs, out_shape=out_shape,
        scratch_shapes=scratch, compiler_params=_params(("parallel", "parallel", "arbitrary")),
    )(*operands)


def _rms_fwd(x, g, *, name, width=None, seg=0, tm=384, after=()):
    T = x.shape[0]
    width = x.shape[1] if width is None else width
    tm = _tile(T, tm, 16)
    after = tuple(after)

    def body(x_ref, g_ref, *rest):
        xf = x_ref[...].astype(F32)
        r = lax.rsqrt(jnp.mean(xf * xf, axis=-1, keepdims=True) + EPS)
        rest[-1][...] = (xf * r * g_ref[...]).astype(BF16)

    return pl.pallas_call(
        body, name=name, grid=(T // tm,),
        in_specs=[pl.BlockSpec((tm, width), lambda i: (i, seg)), pl.BlockSpec((1, width), lambda i: (0, 0))]
        + [pl.BlockSpec(memory_space=pl.ANY)] * len(after),
        out_specs=pl.BlockSpec((tm, width), lambda i: (i, 0)),
        out_shape=jax.ShapeDtypeStruct((T, width), BF16),
        compiler_params=_params(("parallel",)),
    )(x, g, *after)


def _rms_bwd(dy, x, g, *, name, width=None, seg=0, res=None, out_dtype=F32, tm=384, bf16_copy=False):
    T = x.shape[0]
    width = x.shape[1] if width is None else width
    tm = _tile(T, tm, 16)
    has_res = res is not None

    def body(*refs):
        dy_ref, x_ref, g_ref = refs[:3]
        res_ref = refs[3] if has_res else None
        dx_ref, dg_ref = refs[4 if has_res else 3], refs[-1]
        xf = x_ref[...].astype(F32)
        dyf = dy_ref[...].astype(F32)
        r = lax.rsqrt(jnp.mean(xf * xf, axis=-1, keepdims=True) + EPS)
        xhat = xf * r
        dyh = dyf * g_ref[...]
        dx = r * (dyh - xhat * jnp.mean(dyh * xhat, axis=-1, keepdims=True))
        if has_res:
            dx = dx + res_ref[...].astype(F32)
        dx_ref[...] = dx.astype(out_dtype)
        if bf16_copy:
            refs[-2][...] = dx.astype(BF16)
        part = jnp.sum(dyf * xhat, axis=0, keepdims=True)

        @pl.when(pl.program_id(0) == 0)
        def _():
            dg_ref[...] = part

        @pl.when(pl.program_id(0) > 0)
        def _():
            dg_ref[...] += part

    row = pl.BlockSpec((tm, width), lambda i: (i, 0))
    in_specs = [row, pl.BlockSpec((tm, width), lambda i: (i, seg)), pl.BlockSpec((1, width), lambda i: (0, 0))]
    operands = [dy, x, g]
    if has_res:
        in_specs.append(row)
        operands.append(res)
    vec = pl.BlockSpec((1, width), lambda i: (0, 0))
    full = [jax.ShapeDtypeStruct((T, width), out_dtype)] + ([jax.ShapeDtypeStruct((T, width), BF16)] if bf16_copy else [])
    return pl.pallas_call(
        body, name=name, grid=(T // tm,), in_specs=in_specs,
        out_specs=tuple([row] * len(full) + [vec]),
        out_shape=tuple(full + [jax.ShapeDtypeStruct((1, width), F32)]),
        compiler_params=_params(("arbitrary",)),
    )(*operands)


def _down(ext, k):
    return pltpu.roll(ext, k, 0)


def _up(ext, k):
    return pltpu.roll(ext, ext.shape[0] - k, 0)


def _pre_halo(ref, r, R):
    start = pl.multiple_of(jnp.maximum(r * R - HALO, 0), 8)
    keep = (r > 0).astype(F32)
    return ref[pl.ds(start, HALO), :].astype(F32) * keep


def _post_halo(ref, r, R, n_chunks):
    start = pl.multiple_of(jnp.minimum(r * R + R, (n_chunks - 1) * R + R - HALO), 8)
    keep = (r < n_chunks - 1).astype(F32)
    return ref[pl.ds(start, HALO), :].astype(F32) * keep


def _chunk(ref, r, R):
    return ref[pl.ds(pl.multiple_of(r * R, 8), R), :].astype(F32)


def _conv_fwd(rest, conv_w, *, name, dc, tc=128, rows=1056):
    T = rest.shape[0]
    tc = _tile(dc, tc)
    nb = dc // tc
    R = _tile(T, rows, 16)
    n_chunks = T // R

    def body(u_ref, b_ref, c_ref, w_ref, y_ref):
        w0, w1, w2 = w_ref[0:1, :], w_ref[1:2, :], w_ref[2:3, :]

        def chunk(r, carry):
            cu = _chunk(c_ref, r, R) * _chunk(u_ref, r, R)
            ext = jnp.concatenate([_pre_halo(c_ref, r, R) * _pre_halo(u_ref, r, R), cu], axis=0)
            conv = w0 * _down(ext, 2)[HALO:] + w1 * _down(ext, 1)[HALO:] + w2 * cu
            y_ref[pl.ds(pl.multiple_of(r * R, 8), R), :] = (_chunk(b_ref, r, R) * conv).astype(BF16)
            return carry

        lax.fori_loop(0, n_chunks, chunk, 0)

    col = lambda off: pl.BlockSpec((T, tc), lambda j: (0, off * nb + j))
    return pl.pallas_call(
        body, name=name, grid=(nb,),
        in_specs=[col(0), col(1), col(2), pl.BlockSpec((3, tc), lambda j: (0, j))],
        out_specs=pl.BlockSpec((T, tc), lambda j: (0, j)),
        out_shape=jax.ShapeDtypeStruct((T, dc), BF16),
        compiler_params=_params(("parallel",)),
    )(rest, rest, rest, conv_w)


def _conv_bwd(rest, conv_w, dy, *, name, dc, tc=128, rows=1056):
    T = rest.shape[0]
    tc = _tile(dc, tc)
    nb = dc // tc
    R = _tile(T, rows, 16)
    n_chunks = T // R

    def body(u_ref, b_ref, c_ref, w_ref, dy_ref, du_ref, db_ref, dc_ref, dw_ref):
        w0, w1, w2 = w_ref[0:1, :], w_ref[1:2, :], w_ref[2:3, :]

        def chunk(r, carry):
            a0, a1, a2 = carry
            u, b, c = _chunk(u_ref, r, R), _chunk(b_ref, r, R), _chunk(c_ref, r, R)
            dy_c = _chunk(dy_ref, r, R)
            cu = c * u
            ext = jnp.concatenate([_pre_halo(c_ref, r, R) * _pre_halo(u_ref, r, R), cu], axis=0)
            cu1, cu2 = _down(ext, 1)[HALO:], _down(ext, 2)[HALO:]
            conv = w0 * cu2 + w1 * cu1 + w2 * cu
            dconv = dy_c * b
            dext = jnp.concatenate(
                [dconv, _post_halo(dy_ref, r, R, n_chunks) * _post_halo(b_ref, r, R, n_chunks)], axis=0)
            dcu = w2 * dconv + w1 * _up(dext, 1)[:R] + w0 * _up(dext, 2)[:R]
            rows_at = pl.ds(pl.multiple_of(r * R, 8), R)
            db_ref[rows_at, :] = (dy_c * conv).astype(BF16)
            du_ref[rows_at, :] = (dcu * c).astype(BF16)
            dc_ref[rows_at, :] = (dcu * u).astype(BF16)
            return (a0 + jnp.sum(dconv * cu2, axis=0, keepdims=True),
                    a1 + jnp.sum(dconv * cu1, axis=0, keepdims=True),
                    a2 + jnp.sum(dconv * cu, axis=0, keepdims=True))

        zero = jnp.zeros((1, tc), F32)
        a0, a1, a2 = lax.fori_loop(0, n_chunks, chunk, (zero, zero, zero))
        dw_ref[0:1, :] = a0
        dw_ref[1:2, :] = a1
        dw_ref[2:3, :] = a2

    col = lambda off: pl.BlockSpec((T, tc), lambda j: (0, off * nb + j))
    own = pl.BlockSpec((T, tc), lambda j: (0, j))
    return pl.pallas_call(
        body, name=name, grid=(nb,),
        in_specs=[col(0), col(1), col(2), pl.BlockSpec((3, tc), lambda j: (0, j)), own],
        out_specs=(own, own, own, pl.BlockSpec((3, tc), lambda j: (0, j))),
        out_shape=(jax.ShapeDtypeStruct((T, dc), BF16),) * 3 + (jax.ShapeDtypeStruct((3, dc), F32),),
        compiler_params=_params(("parallel",)),
    )(rest, rest, rest, conv_w, dy)


def _window_count(r, R, n_rows, w, first_row_offset):
    t = lax.broadcasted_iota(jnp.int32, (n_rows, 1), 0) + (r * R + first_row_offset)
    return jnp.minimum(t + 1, w).astype(F32)


def _pool_fwd(rest, pool_w, pool_scale, *, name, seg0, pg, rows=1056):
    T = rest.shape[0]
    R = _tile(T, rows, 16)
    n_chunks = T // R
    n_groups = len(POOL_WINDOWS)

    def body(x_ref, w_ref, s_ref, y_ref):
        def run(window):
            def chunk(r, carry):
                g = _chunk(x_ref, r, R)
                s = jnp.concatenate([_pre_halo(x_ref, r, R), g], axis=0)
                k = 1
                while k < window:
                    s = s + _down(s, k)
                    k *= 2
                pooled = s[HALO:] / _window_count(r, R, R, window, 0) - g
                mixed = jnp.dot(pooled.astype(BF16), w_ref[0], preferred_element_type=F32)
                y_ref[pl.ds(pl.multiple_of(r * R, 8), R), :] = (mixed * s_ref[...]).astype(BF16)
                return carry

            lax.fori_loop(0, n_chunks, chunk, 0)

        for gi, window in enumerate(POOL_WINDOWS):
            pl.when(pl.program_id(0) == gi)(functools.partial(run, window))

    return pl.pallas_call(
        body, name=name, grid=(n_groups,),
        in_specs=[pl.BlockSpec((T, pg), lambda g: (0, seg0 + g)),
                  pl.BlockSpec((1, pg, pg), lambda g: (g, 0, 0)),
                  pl.BlockSpec((1, pg), lambda g: (0, g))],
        out_specs=pl.BlockSpec((T, pg), lambda g: (0, g)),
        out_shape=jax.ShapeDtypeStruct((T, n_groups * pg), BF16),
        compiler_params=_params(("parallel",)),
    )(rest, pool_w, pool_scale)


def _pool_bwd(rest, pool_w, pool_scale, dy, *, name, seg0, pg, rows=1056):
    T = rest.shape[0]
    R = _tile(T, rows, 16)
    n_chunks = T // R
    n_groups = len(POOL_WINDOWS)

    def body(x_ref, w_ref, s_ref, dy_ref, dx_ref, dw_ref, ds_ref):
        def run(window):
            def chunk(r, carry):
                dw_acc, ds_acc = carry
                g = _chunk(x_ref, r, R)
                s = jnp.concatenate([_pre_halo(x_ref, r, R), g], axis=0)
                k = 1
                while k < window:
                    s = s + _down(s, k)
                    k *= 2
                pooled = (s[HALO:] / _window_count(r, R, R, window, 0) - g).astype(BF16)
                mixed = jnp.dot(pooled, w_ref[0], preferred_element_type=F32)
                dy_c = _chunk(dy_ref, r, R)
                dm_ext = (jnp.concatenate([dy_c, _post_halo(dy_ref, r, R, n_chunks)], axis=0)
                          * s_ref[...]).astype(BF16)
                dpool_ext = lax.dot_general(dm_ext, w_ref[0], (((1,), (1,)), ((), ())),
                                            preferred_element_type=F32)
                a = dpool_ext / _window_count(r, R, R + HALO, window, 0)
                k = 1
                while k < window:
                    a = a + _up(a, k)
                    k *= 2
                dx_ref[pl.ds(pl.multiple_of(r * R, 8), R), :] = (a[:R] - dpool_ext[:R]).astype(BF16)
                dw_acc = dw_acc + lax.dot_general(pooled, dm_ext[:R], (((0,), (0,)), ((), ())),
                                                  preferred_element_type=F32)
                ds_acc = ds_acc + jnp.sum(dy_c * mixed, axis=0, keepdims=True)
                return dw_acc, ds_acc

            dw_acc, ds_acc = lax.fori_loop(0, n_chunks, chunk,
                                           (jnp.zeros((pg, pg), F32), jnp.zeros((1, pg), F32)))
            dw_ref[0] = dw_acc
            ds_ref[...] = ds_acc

        for gi, window in enumerate(POOL_WINDOWS):
            pl.when(pl.program_id(0) == gi)(functools.partial(run, window))

    own = pl.BlockSpec((T, pg), lambda g: (0, g))
    return pl.pallas_call(
        body, name=name, grid=(n_groups,),
        in_specs=[pl.BlockSpec((T, pg), lambda g: (0, seg0 + g)),
                  pl.BlockSpec((1, pg, pg), lambda g: (g, 0, 0)),
                  pl.BlockSpec((1, pg), lambda g: (0, g)), own],
        out_specs=(own, pl.BlockSpec((1, pg, pg), lambda g: (g, 0, 0)), pl.BlockSpec((1, pg), lambda g: (0, g))),
        out_shape=(jax.ShapeDtypeStruct((T, n_groups * pg), BF16),
                   jax.ShapeDtypeStruct((n_groups, pg, pg), F32),
                   jax.ShapeDtypeStruct((1, n_groups * pg), F32)),
        compiler_params=_params(("parallel",)),
    )(rest, pool_w, pool_scale, dy)


def _rope(r, cos_t, sin_t):
    return r * cos_t + pltpu.roll(r, LANES // 2, 1) * sin_t


def _rope_t(d, cos_t, sin_t):
    return d * cos_t + pltpu.roll(d * sin_t, LANES // 2, 1)


def _qk_fwd(q_raw, k_nope, rest, cos_t, sin_t, q_norm, k_norm, *, name, heads, kr_seg, tm=192):
    T = q_raw.shape[0]
    tm = _tile(T, tm, 16)

    def body(q_ref, kn_ref, kr_ref, c_ref, s_ref, gq_ref, gk_ref, qo_ref, ko_ref):
        cos_b, sin_b = c_ref[...], s_ref[...]
        kr = kr_ref[:, 0:LANES]
        kr_ss = jnp.sum(kr * kr, axis=-1, keepdims=True)
        gq, gk = gq_ref[...], gk_ref[...]
        for h in range(heads):
            lo = h * HEAD_PAD
            q = q_ref[:, lo:lo + HEAD_PAD].astype(F32)
            rq = lax.rsqrt(jnp.sum(q * q, axis=-1, keepdims=True) / QK_HEAD + EPS)
            qn = q * (rq * Q_FOLD) * gq
            qo_ref[:, lo:lo + LANES] = qn[:, :LANES].astype(BF16)
            qo_ref[:, lo + LANES:lo + HEAD_PAD] = _rope(qn[:, LANES:], cos_b, sin_b).astype(BF16)
            kn = kn_ref[:, h * LANES:(h + 1) * LANES].astype(F32)
            rk = lax.rsqrt((jnp.sum(kn * kn, axis=-1, keepdims=True) + kr_ss) / QK_HEAD + EPS)
            ko_ref[:, lo:lo + LANES] = (kn * rk * gk[:, :LANES]).astype(BF16)
            ko_ref[:, lo + LANES:lo + HEAD_PAD] = _rope(kr * rk * gk[:, LANES:], cos_b, sin_b).astype(BF16)

    wq, wk = heads * HEAD_PAD, heads * LANES
    return pl.pallas_call(
        body, name=name, grid=(T // tm,),
        in_specs=[pl.BlockSpec((tm, wq), lambda i: (i, 0)), pl.BlockSpec((tm, wk), lambda i: (i, 0)),
                  pl.BlockSpec((tm, HEAD_PAD), lambda i: (i, kr_seg)),
                  pl.BlockSpec((tm, LANES), lambda i: (i, 0)), pl.BlockSpec((tm, LANES), lambda i: (i, 0)),
                  pl.BlockSpec((1, HEAD_PAD), lambda i: (0, 0)), pl.BlockSpec((1, HEAD_PAD), lambda i: (0, 0))],
        out_specs=(pl.BlockSpec((tm, wq), lambda i: (i, 0)), pl.BlockSpec((tm, wq), lambda i: (i, 0))),
        out_shape=(jax.ShapeDtypeStruct((T, wq), BF16), jax.ShapeDtypeStruct((T, wq), BF16)),
        compiler_params=_params(("parallel",)),
    )(q_raw, k_nope, rest, cos_t, sin_t, q_norm, k_norm)


def _qk_bwd(dq, dk, q_raw, k_nope, rest, cos_t, sin_t, q_norm, k_norm, *, name, heads, kr_seg, tm=128, after=()):
    T = q_raw.shape[0]
    tm = _tile(T, tm, 16)
    after = tuple(after)

    def body(dq_ref, dk_ref, q_ref, kn_ref, kr_ref, c_ref, s_ref, gq_ref, gk_ref, *rest_refs):
        dqr_ref, dkn_ref, dkr_ref, dgq_ref, dgk_ref = rest_refs[len(after):]
        cos_b, sin_b = c_ref[...], s_ref[...]
        kr = kr_ref[:, 0:LANES]
        kr_ss = jnp.sum(kr * kr, axis=-1, keepdims=True)
        gq, gk = gq_ref[...], gk_ref[...]
        dgq = jnp.zeros((1, HEAD_PAD), F32)
        dgk_n = jnp.zeros((1, LANES), F32)
        dgk_r = jnp.zeros((1, LANES), F32)
        dkr = jnp.zeros((tm, LANES), F32)
        for h in range(heads):
            lo = h * HEAD_PAD
            q = q_ref[:, lo:lo + HEAD_PAD].astype(F32)
            rq = lax.rsqrt(jnp.sum(q * q, axis=-1, keepdims=True) / QK_HEAD + EPS)
            qhat = q * rq
            dqn = jnp.concatenate([dq_ref[:, lo:lo + LANES],
                                   _rope_t(dq_ref[:, lo + LANES:lo + HEAD_PAD], cos_b, sin_b)], axis=1) * ATTN_SCALE
            dgq = dgq + jnp.sum(dqn * qhat, axis=0, keepdims=True)
            dqh = dqn * gq
            dqr_ref[:, lo:lo + HEAD_PAD] = (
                rq * (dqh - qhat * (jnp.sum(dqh * qhat, axis=-1, keepdims=True) / QK_HEAD))).astype(BF16)
            kn = kn_ref[:, h * LANES:(h + 1) * LANES].astype(F32)
            rk = lax.rsqrt((jnp.sum(kn * kn, axis=-1, keepdims=True) + kr_ss) / QK_HEAD + EPS)
            khat_n, khat_r = kn * rk, kr * rk
            dkn_n = dk_ref[:, lo:lo + LANES] * (1.0 / LOG2_E)
            dkn_r = _rope_t(dk_ref[:, lo + LANES:lo + HEAD_PAD], cos_b, sin_b) * (1.0 / LOG2_E)
            dgk_n = dgk_n + jnp.sum(dkn_n * khat_n, axis=0, keepdims=True)
            dgk_r = dgk_r + jnp.sum(dkn_r * khat_r, axis=0, keepdims=True)
            dkh_n, dkh_r = dkn_n * gk[:, :LANES], dkn_r * gk[:, LANES:]
            proj = (jnp.sum(dkh_n * khat_n, axis=-1, keepdims=True)
                    + jnp.sum(dkh_r * khat_r, axis=-1, keepdims=True)) / QK_HEAD
            dkn_ref[:, h * LANES:(h + 1) * LANES] = (rk * (dkh_n - khat_n * proj)).astype(BF16)
            dkr = dkr + rk * (dkh_r - khat_r * proj)
        dkr_ref[:, 0:LANES] = dkr.astype(BF16)
        dkr_ref[:, LANES:HEAD_PAD] = jnp.zeros((tm, HEAD_PAD - LANES), BF16)
        dgk = jnp.concatenate([dgk_n, dgk_r], axis=1)

        @pl.when(pl.program_id(0) == 0)
        def _():
            dgq_ref[...] = dgq
            dgk_ref[...] = dgk

        @pl.when(pl.program_id(0) > 0)
        def _():
            dgq_ref[...] += dgq
            dgk_ref[...] += dgk

    wq, wk = heads * HEAD_PAD, heads * LANES
    row = lambda w: pl.BlockSpec((tm, w), lambda i: (i, 0))
    vec = pl.BlockSpec((1, HEAD_PAD), lambda i: (0, 0))
    return pl.pallas_call(
        body, name=name, grid=(T // tm,),
        in_specs=[row(wq), row(wq), row(wq), row(wk), pl.BlockSpec((tm, HEAD_PAD), lambda i: (i, kr_seg)),
                  row(LANES), row(LANES), vec, vec] + [pl.BlockSpec(memory_space=pl.ANY)] * len(after),
        out_specs=(row(wq), row(wk), row(HEAD_PAD), vec, vec),
        out_shape=(jax.ShapeDtypeStruct((T, wq), BF16), jax.ShapeDtypeStruct((T, wk), BF16),
                   jax.ShapeDtypeStruct((T, HEAD_PAD), BF16),
                   jax.ShapeDtypeStruct((1, HEAD_PAD), F32), jax.ShapeDtypeStruct((1, HEAD_PAD), F32)),
        compiler_params=_params(("arbitrary",)),
    )(dq, dk, q_raw, k_nope, rest, cos_t, sin_t, q_norm, k_norm, *after)


def _causal_mask(s):
    row = lax.broadcasted_iota(jnp.int32, s.shape, 0)
    col = lax.broadcasted_iota(jnp.int32, s.shape, 1)
    return jnp.where(row >= col, s, NEG)


def _flash_fwd(q, k, v, *, name, heads, tq=384, hp=2, parts=2):
    T = q.shape[0]
    tq = _tile(T, tq, LANES)
    nq = T // tq
    tr = tq // parts
    nt = (((1,), (1,)), ((), ()))
    chains = [(h, r) for h in range(hp) for r in range(parts)]

    def body(q_ref, k_ref, v_ref, o_ref, lse_ref, acc_ref):
        def q_block(i, carry):
            rows_at = [pl.ds(pl.multiple_of(i * tq + r * tr, tr), tr) for r in range(parts)]
            qbs = [q_ref[rows_at[r], h * HEAD_PAD:(h + 1) * HEAD_PAD] for h, r in chains]
            for c in range(len(chains)):
                acc_ref[c] = jnp.zeros((tr, V_HEAD), F32)

            def step(j, state, masked):
                k_at = pl.ds(pl.multiple_of(j * tq, tq), tq)
                new = []
                scores = [lax.dot_general(qb, k_ref[k_at, h * HEAD_PAD:(h + 1) * HEAD_PAD], nt,
                                          preferred_element_type=F32) for qb, (h, r) in zip(qbs, chains)]
                for c, (s, (h, r)) in enumerate(zip(scores, chains)):
                    m, l = state[c]
                    if masked:
                        row = lax.broadcasted_iota(jnp.int32, s.shape, 0) + r * tr
                        s = jnp.where(row >= lax.broadcasted_iota(jnp.int32, s.shape, 1), s, NEG)
                    m_new = jnp.maximum(m, jnp.max(s, axis=-1, keepdims=True))
                    p = jnp.exp2(s - m_new)
                    alpha = jnp.exp2(m - m_new)
                    new.append((m_new, alpha * l + jnp.sum(p, axis=-1, keepdims=True)))
                    acc_ref[c] = alpha * acc_ref[c] + jnp.dot(p.astype(BF16), v_ref[k_at, h * V_HEAD:(h + 1) * V_HEAD],
                                                              preferred_element_type=F32)
                return tuple(new)

            init = tuple((jnp.full((tr, 1), NEG, F32), jnp.zeros((tr, 1), F32)) for _ in chains)
            state = lax.fori_loop(0, i, lambda j, st: step(j, st, False), init)
            state = step(i, state, True)
            for c, ((m, l), (h, r)) in enumerate(zip(state, chains)):
                o_ref[rows_at[r], h * V_HEAD:(h + 1) * V_HEAD] = (acc_ref[c] / l).astype(BF16)
                lse_ref[h, rows_at[r], :] = jnp.broadcast_to(m + jnp.log2(l), (tr, LANES))
            return carry

        lax.fori_loop(0, nq, q_block, 0)

    qk_spec = pl.BlockSpec((T, hp * HEAD_PAD), lambda g: (0, g))
    v_spec = pl.BlockSpec((T, hp * V_HEAD), lambda g: (0, g))
    return pl.pallas_call(
        body, name=name, grid=(heads // hp,), in_specs=[qk_spec, qk_spec, v_spec],
        out_specs=(v_spec, pl.BlockSpec((hp, T, LANES), lambda g: (g, 0, 0))),
        out_shape=(jax.ShapeDtypeStruct((T, heads * V_HEAD), BF16), jax.ShapeDtypeStruct((heads, T, LANES), F32)),
        scratch_shapes=[pltpu.VMEM((len(chains), tr, V_HEAD), F32)],
        compiler_params=_params(("parallel",)),
    )(q, k, v)


def _flash_bwd(q, k, v, o, do, lse, *, name, heads, tq=384, after=()):
    T = q.shape[0]
    tq = _tile(T, tq, LANES)
    nq = T // tq
    nt = (((1,), (1,)), ((), ()))
    tn = (((0,), (0,)), ((), ()))

    after = tuple(after)

    def body(q_ref, k_ref, v_ref, o_ref, do_ref, lse_ref, *rest):
        dq_ref, dk_ref, dv_ref, delta_ref, dv_acc_ref = rest[len(after):]
        def fill_delta(i, carry):
            at = pl.ds(pl.multiple_of(i * tq, tq), tq)
            d = jnp.sum(o_ref[at, :].astype(F32) * do_ref[at, :].astype(F32), axis=-1, keepdims=True)
            delta_ref[at, :] = jnp.broadcast_to(d, (tq, LANES))
            dq_ref[at, :] = jnp.zeros((tq, HEAD_PAD), F32)
            return carry

        lax.fori_loop(0, nq, fill_delta, 0)

        def kv_block(j, carry):
            k_at = pl.ds(pl.multiple_of(j * tq, tq), tq)
            kb, vb = k_ref[k_at, :], v_ref[k_at, :]

            def steps(blocks, masked):
                at = [pl.ds(pl.multiple_of(i * tq, tq), tq) for i in blocks]
                qbs = [q_ref[a, :] for a in at]
                dobs = [do_ref[a, :] for a in at]
                scores = [lax.dot_general(qb, kb, nt, preferred_element_type=F32) for qb in qbs]
                dps = [lax.dot_general(dob, vb, nt, preferred_element_type=F32) for dob in dobs]
                for a, qb, dob, sc, dp in zip(at, qbs, dobs, scores, dps):
                    if masked:
                        sc = _causal_mask(sc)
                    p = jnp.exp2(sc - lse_ref[0, a, :][:, 0:1])
                    ds = (p * (dp - delta_ref[a, :][:, 0:1])).astype(BF16)
                    dv_part = lax.dot_general(p.astype(BF16), dob, tn, preferred_element_type=F32)
                    dk_part = lax.dot_general(ds, qb, tn, preferred_element_type=F32)
                    if masked:
                        dv_acc_ref[...] = dv_part
                        dk_ref[k_at, :] = dk_part
                    else:
                        dv_acc_ref[...] += dv_part
                        dk_ref[k_at, :] += dk_part
                    dq_ref[a, :] += jnp.dot(ds, kb, preferred_element_type=F32)

            def two_blocks(t, carry):
                steps([j + 1 + 2 * t, j + 2 + 2 * t], False)
                return carry

            steps([j], True)
            rest = nq - 1 - j
            lax.fori_loop(0, rest // 2, two_blocks, 0)

            @pl.when(rest % 2 == 1)
            def _():
                steps([nq - 1], False)

            dv_ref[k_at, :] = dv_acc_ref[...].astype(BF16)
            return carry

        lax.fori_loop(0, nq, kv_block, 0)

    qk_spec = pl.BlockSpec((T, HEAD_PAD), lambda h: (0, h))
    v_spec = pl.BlockSpec((T, V_HEAD), lambda h: (0, h))
    return pl.pallas_call(
        body, name=name, grid=(heads,),
        in_specs=[qk_spec, qk_spec, v_spec, v_spec, v_spec, pl.BlockSpec((1, T, LANES), lambda h: (h, 0, 0))]
        + [pl.BlockSpec(memory_space=pl.ANY)] * len(after),
        out_specs=(qk_spec, qk_spec, v_spec),
        out_shape=(jax.ShapeDtypeStruct((T, heads * HEAD_PAD), F32), jax.ShapeDtypeStruct((T, heads * HEAD_PAD), F32),
                   jax.ShapeDtypeStruct((T, heads * V_HEAD), BF16)),
        scratch_shapes=[pltpu.VMEM((T, LANES), F32), pltpu.VMEM((tq, V_HEAD), F32)],
        compiler_params=_params(("parallel",)),
    )(q, k, v, o, do, lse, *after)


def _merge_fwd(gl, pa, pb, pc, *, name, d, tm=384, tn=1024):
    T = pa.shape[0]
    tm, tn = _tile(T, tm, 16), _tile(d, tn)
    nb = d // tn

    def body(g0, g1, g2, a, b, c, o_ref):
        f = lambda ref: ref[...].astype(F32)
        o_ref[...] = (jax.nn.sigmoid(f(g0)) * f(a) + jax.nn.sigmoid(f(g1)) * f(b)
                      + jax.nn.sigmoid(f(g2)) * f(c)).astype(BF16)

    gate = lambda n: pl.BlockSpec((tm, tn), lambda i, j: (i, n * nb + j))
    blk = pl.BlockSpec((tm, tn), lambda i, j: (i, j))
    return pl.pallas_call(
        body, name=name, grid=(T // tm, nb), in_specs=[gate(0), gate(1), gate(2), blk, blk, blk],
        out_specs=blk, out_shape=jax.ShapeDtypeStruct((T, d), BF16),
        compiler_params=_params(("parallel", "parallel")),
    )(gl, gl, gl, pa, pb, pc)


def _merge_bwd(dm, gl, pa, pb, pc, *, name, d, tm=384, tn=1024):
    T = pa.shape[0]
    tm, tn = _tile(T, tm, 16), _tile(d, tn)
    nb = d // tn

    def body(dm_ref, g0, g1, g2, a, b, c, da, db, dc, dg0, dg1, dg2):
        dmv = dm_ref[...].astype(F32)
        for g_ref, p_ref, dp_ref, dg_ref in ((g0, a, da, dg0), (g1, b, db, dg1), (g2, c, dc, dg2)):
            sg = jax.nn.sigmoid(g_ref[...].astype(F32))
            dp_ref[...] = (dmv * sg).astype(BF16)
            dg_ref[...] = (dmv * p_ref[...].astype(F32) * sg * (1.0 - sg)).astype(BF16)

    gate = lambda n: pl.BlockSpec((tm, tn), lambda i, j: (i, n * nb + j))
    blk = pl.BlockSpec((tm, tn), lambda i, j: (i, j))
    return pl.pallas_call(
        body, name=name, grid=(T // tm, nb), in_specs=[blk, gate(0), gate(1), gate(2), blk, blk, blk],
        out_specs=(blk,) * 6, out_shape=(jax.ShapeDtypeStruct((T, d), BF16),) * 6,
        compiler_params=_params(("parallel", "parallel")),
    )(dm, gl, gl, gl, pa, pb, pc)


def _loss(y, target, *, name, first, last, tm=384):
    T, d = y.shape
    tm = _tile(T, tm, 16)

    def body(y_ref, t_ref, loss_ref, dy_ref, dyb_ref):
        i = pl.program_id(0)
        row = lax.broadcasted_iota(jnp.int32, (tm, 1), 0) + i * tm
        real = jnp.logical_and(row >= first, row < last)
        err = jnp.where(real, y_ref[...] - t_ref[...], 0.0)
        dy_ref[...] = err * (1.0 / d)
        dyb_ref[...] = (err * (1.0 / d)).astype(BF16)
        part = jnp.broadcast_to(jnp.sum(err * err, keepdims=True).reshape(1, 1), (1, LANES))

        @pl.when(i == 0)
        def _():
            loss_ref[...] = part

        @pl.when(i > 0)
        def _():
            loss_ref[...] += part

    blk = pl.BlockSpec((tm, d), lambda i: (i, 0))
    return pl.pallas_call(
        body, name=name, grid=(T // tm,), in_specs=[blk, blk],
        out_specs=(pl.BlockSpec((1, LANES), lambda i: (0, 0)), blk, blk),
        out_shape=(jax.ShapeDtypeStruct((1, LANES), F32), jax.ShapeDtypeStruct((T, d), F32),
                   jax.ShapeDtypeStruct((T, d), BF16)),
        compiler_params=_params(("arbitrary",)),
    )(y, target)


def _as3d(a):
    return a.reshape(a.shape[0], -1, a.shape[-1])


def _sum_stack(parts, *, name, out_dtype, rows=256):
    n, R, C = parts.shape
    tr = _tile(R, rows, 16)

    def body(p_ref, o_ref):
        acc = p_ref[0].astype(F32)
        for s in range(1, n):
            acc = acc + p_ref[s].astype(F32)
        o_ref[...] = acc.astype(out_dtype)

    return pl.pallas_call(
        body, name=name, grid=(R // tr,),
        in_specs=[pl.BlockSpec((n, tr, C), lambda i: (0, i, 0))],
        out_specs=pl.BlockSpec((tr, C), lambda i: (i, 0)),
        out_shape=jax.ShapeDtypeStruct((R, C), out_dtype),
        compiler_params=_params(("parallel",)),
    )(parts)


def _adamw(w, g, m, v, *, name, rows=128):
    R, C = w.shape
    tr = _tile(R, rows, 8)
    c1 = 1.0 - ADAM_B1 ** ADAM_STEP
    c2 = 1.0 - ADAM_B2 ** ADAM_STEP

    def body(w_ref, g_ref, m_ref, v_ref, d_ref, nm_ref, nv_ref):
        gv = g_ref[...]
        nm = ADAM_B1 * m_ref[...] + (1.0 - ADAM_B1) * gv
        nv = ADAM_B2 * v_ref[...] + (1.0 - ADAM_B2) * (gv * gv)
        nm_ref[...] = nm
        nv_ref[...] = nv
        d_ref[...] = -ADAM_LR * ((nm / c1) / (jnp.sqrt(nv / c2) + ADAM_EPS) + ADAM_WD * w_ref[...])

    blk = pl.BlockSpec((tr, C), lambda i: (i, 0))
    return pl.pallas_call(
        body, name=name, grid=(R // tr,), in_specs=[blk] * 4, out_specs=(blk,) * 3,
        out_shape=(jax.ShapeDtypeStruct((R, C), F32),) * 3,
        compiler_params=_params(("parallel",)),
    )(w, g, m, v)


def _one_hot(index, n):
    return jnp.broadcast_to((jnp.arange(n) == index).astype(F32)[:, None, None], (n, 8, LANES))


def _is_set(flags_ref, s):
    return flags_ref[s, 0:1, 0:1] > 0.5


def _rows_for(h, width, itemsize, n_stacked, budget, mult):
    return _tile(h, max(mult, budget // (n_stacked * width * itemsize)), mult)


def _pair_sum(pieces, recv, core_index, *, name):
    _, H, C = recv.shape
    tr = _rows_for(H, C, 2, 1, 2 << 20, 16)
    nh = H // tr
    halves_lead = pieces.ndim == 4

    def body(c_ref, mine_ref, r_ref, o_ref):
        mine = mine_ref[0, 0] if halves_lead else mine_ref[0]
        o_ref[0] = (mine.astype(F32) + r_ref[0].astype(F32)).astype(BF16)

    blk = pl.BlockSpec((1, tr, C), lambda j, i, c: (j, i, 0))
    if halves_lead:
        mine_spec = pl.BlockSpec((1, 1, tr, C), lambda j, i, c: (c[0], j, i, 0))
    else:
        mine_spec = pl.BlockSpec((1, tr, C), lambda j, i, c: (j, c[0] * nh + i, 0))
    return pl.pallas_call(
        body, name=name,
        grid_spec=pltpu.PrefetchScalarGridSpec(num_scalar_prefetch=1, grid=(4, nh), in_specs=[mine_spec, blk],
                                               out_specs=blk),
        out_shape=jax.ShapeDtypeStruct((4, H, C), BF16),
        compiler_params=_params(("parallel", "parallel")),
    )(core_index, pieces, recv)


def _chip_sum(pair, landed, chip_flags, chip_index, *, name):
    _, H, C = pair.shape
    tr = _rows_for(H, C, 2, 4, 8 << 20, 16)

    def body(me_ref, own_ref, l_ref, chip_ref, o_ref):
        acc = None
        for s in range(4):
            part = jnp.where(_is_set(chip_ref, s), own_ref[0], l_ref[s]).astype(F32)
            acc = part if acc is None else acc + part
        o_ref[...] = acc

    return pl.pallas_call(
        body, name=name,
        grid_spec=pltpu.PrefetchScalarGridSpec(
            num_scalar_prefetch=1, grid=(H // tr,),
            in_specs=[pl.BlockSpec((1, tr, C), lambda i, me: (me[0], i, 0)),
                      pl.BlockSpec((4, tr, C), lambda i, me: (0, i, 0)),
                      pl.BlockSpec((4, 8, LANES), lambda i, me: (0, 0, 0))],
            out_specs=pl.BlockSpec((tr, C), lambda i, me: (i, 0))),
        out_shape=jax.ShapeDtypeStruct((H, C), F32),
        compiler_params=_params(("parallel",)),
    )(chip_index, pair, landed, chip_flags)


def _adamw_layer(w, m, v, total, recv, core, layer, prev, *, name, col_halves=False, after=()):
    _, R, C = w.shape
    H, wd = total.shape
    tr = _rows_for(H, wd, 4, 1, 2 << 20, 8)
    nh = H // tr
    c1 = 1.0 - ADAM_B1 ** ADAM_STEP
    c2 = 1.0 - ADAM_B2 ** ADAM_STEP
    n_prev = 0 if prev is None else 4
    after = tuple(after)

    def body(*refs):
        w_ref, m_ref, v_ref, t_ref, r_ref, core_ref = refs[:6]
        g_ref, d_ref, nm_ref, nv_ref = refs[6 + n_prev + len(after):]
        half_is_mine = jnp.where(pl.program_id(0) == 0, core_ref[0, 0:1, 0:1], core_ref[1, 0:1, 0:1]) > 0.5
        gv = jnp.where(half_is_mine, t_ref[...], r_ref[...])
        nm = ADAM_B1 * m_ref[0] + (1.0 - ADAM_B1) * gv
        nv = ADAM_B2 * v_ref[0] + (1.0 - ADAM_B2) * (gv * gv)
        g_ref[0] = gv
        nm_ref[0] = nm
        nv_ref[0] = nv
        d_ref[0] = -ADAM_LR * ((nm / c1) / (jnp.sqrt(nv / c2) + ADAM_EPS) + ADAM_WD * w_ref[0])

    if col_halves:
        lay = pl.BlockSpec((1, tr, wd), lambda hf, i: (layer, i, hf))
    else:
        lay = pl.BlockSpec((1, tr, wd), lambda hf, i: (layer, hf * nh + i, 0))
    one = pl.BlockSpec((tr, wd), lambda hf, i: (i, 0))
    operands = [w, m, v, total, recv, core] + ([] if prev is None else list(prev)) + list(after)
    return pl.pallas_call(
        body, name=name, grid=(2, nh),
        in_specs=[lay, lay, lay, one, one, pl.BlockSpec((2, 8, LANES), lambda hf, i: (0, 0, 0))]
        + [ANY] * (n_prev + len(after)),
        out_specs=(lay,) * 4, out_shape=(jax.ShapeDtypeStruct((2, R, C), F32),) * 4,
        input_output_aliases={6 + i: i for i in range(n_prev)},
        compiler_params=_params(("parallel", "parallel")),
    )(*operands)


ANY = pl.BlockSpec(memory_space=pl.ANY)


def _coords():
    return lax.axis_index("x"), lax.axis_index("y"), lax.axis_index("c")


HBM = pl.BlockSpec(memory_space=pltpu.HBM)
SEM = pl.BlockSpec(memory_space=pltpu.SEMAPHORE)
EFFECT = pltpu.SideEffectType.DATAFLOW_SIDE_EFFECTING


def _copies(plan, bufs, send_sems, recv_sems):
    return [pltpu.make_async_remote_copy(src_ref=s, dst_ref=d, send_sem=send_sems.at[i], recv_sem=recv_sems.at[i],
                                         device_id=to, device_id_type=MESH)
            for i, (s, d, to) in enumerate(plan(bufs))]


def _start_copies(bufs, groups, *, name):
    nb, ng = len(bufs), len(groups)

    def body(*refs):
        buf_refs = refs[:nb]
        sems = refs[nb:nb + 2 * ng]
        token = refs[-1]
        for g, (plan, _) in enumerate(groups):
            for cp in _copies(plan, buf_refs, sems[2 * g], sems[2 * g + 1]):
                cp.start()
        token[...] = jnp.zeros_like(token)

    sem_shapes = []
    for _, n in groups:
        sem_shapes += [pltpu.SemaphoreType.DMA((n,)), pltpu.SemaphoreType.DMA((n,))]
    out = pl.pallas_call(
        body, name=name, in_specs=[HBM] * nb,
        out_specs=tuple([SEM] * (2 * ng) + [HBM] * nb + [pl.BlockSpec(memory_space=pltpu.VMEM)]),
        out_shape=tuple(sem_shapes + [pltpu.HBM(b.shape, b.dtype) for b in bufs] + [jax.ShapeDtypeStruct((8, LANES), F32)]),
        input_output_aliases={i: 2 * ng + i for i in range(nb)},
        compiler_params=pltpu.CompilerParams(has_side_effects=EFFECT),
    )(*[pltpu.with_memory_space_constraint(b, pltpu.HBM) for b in bufs])
    sems = [(out[2 * g], out[2 * g + 1]) for g in range(ng)]
    return sems, list(out[2 * ng:2 * ng + nb]), out[-1]


def _wait_copies(bufs, sems, plan, after, *, name):
    nb = len(bufs)

    def body(*refs):
        buf_refs = refs[:nb]
        for cp in _copies(plan, buf_refs, refs[nb], refs[nb + 1]):
            cp.wait_send()
            cp.wait_recv()

    out = pl.pallas_call(
        body, name=name, in_specs=[HBM] * nb + [SEM, SEM, ANY], out_specs=tuple([HBM] * nb),
        out_shape=tuple(pltpu.HBM(b.shape, b.dtype) for b in bufs),
        input_output_aliases={i: i for i in range(nb)},
        compiler_params=pltpu.CompilerParams(has_side_effects=EFFECT),
    )(*bufs, sems[0], sems[1], after)
    return list(out)


def _half(ref, c):
    h = ref.shape[0] // 2
    return ref.at[pl.ds(c * h, h)]


def _ici_gather_plan(pairs):
    def plan(refs):
        x, y, c = _coords()
        me = 2 * x + y
        out = []
        for s, d in pairs:
            for cx, cy in [(1 - x, y), (x, 1 - y), (1 - x, 1 - y)]:
                out.append((_half(refs[s], c), _half(refs[d].at[me], c), (cx, cy, c)))
            out.append((refs[s], refs[d].at[me], (x, y, 1 - c)))
        return out
    return plan, 4 * len(pairs)


def _d2d_forward_plan(lands):
    def plan(refs):
        x, y, c = _coords()
        out = []
        for d in lands:
            for cx, cy in [(1 - x, y), (x, 1 - y), (1 - x, 1 - y)]:
                got = _half(refs[d].at[2 * cx + cy], c)
                out.append((got, got, (x, y, 1 - c)))
        return out
    return plan, 3 * len(lands)


def _swap_half_plan(pairs):
    def plan(refs):
        x, y, c = _coords()
        out = []
        for s, d in pairs:
            h = refs[d].shape[1]
            other = refs[s].at[1 - c] if len(refs[s].shape) == 4 else refs[s].at[:, pl.ds((1 - c) * h, h)]
            out.append((other, refs[d], (x, y, 1 - c)))
        return out
    return plan, len(pairs)


def _scatter_plan(pairs):
    def plan(refs):
        x, y, c = _coords()
        me = 2 * x + y
        out = []
        for s, d in pairs:
            for cx, cy in [(1 - x, y), (x, 1 - y), (1 - x, 1 - y)]:
                out.append((refs[s].at[2 * cx + cy], refs[d].at[me], (cx, cy, c)))
        return out
    return plan, 3 * len(pairs)


def _swap_total_plan(pairs):
    def plan(refs):
        x, y, c = _coords()
        return [(refs[s], refs[d], (x, y, 1 - c)) for s, d in pairs]
    return plan, len(pairs)


def _gather_all(block, *, name, after=()):
    after = tuple(after)

    def body(src, *rest):
        out, send_sems, recv_sems, local_sem = rest[len(after):]
        x, y, c = _coords()
        me = 4 * x + 2 * y + c
        flips = [(fx, fy, fc) for fx in (0, 1) for fy in (0, 1) for fc in (0, 1)][1:]
        mine = pltpu.make_async_copy(src, out.at[me], local_sem)
        mine.start()
        peers = [(x ^ fx, y ^ fy, c ^ fc) for fx, fy, fc in flips]
        cps = [pltpu.make_async_remote_copy(src_ref=src, dst_ref=out.at[me], send_sem=send_sems.at[k],
                                            recv_sem=recv_sems.at[k], device_id=peer, device_id_type=MESH)
               for k, peer in enumerate(peers)]
        for cp in cps:
            cp.start()
        for k, (px, py, pc) in enumerate(peers):
            slot = out.at[4 * px + 2 * py + pc]
            pltpu.make_async_remote_copy(src_ref=slot, dst_ref=slot, send_sem=send_sems.at[k], recv_sem=recv_sems.at[k],
                                         device_id=(px, py, pc), device_id_type=MESH).wait_recv()
        for cp in cps:
            cp.wait_send()
        mine.wait()

    return pl.pallas_call(
        body, name=name, in_specs=[ANY] * (1 + len(after)), out_specs=ANY,
        out_shape=jax.ShapeDtypeStruct((8,) + block.shape, block.dtype),
        scratch_shapes=[pltpu.SemaphoreType.DMA((7,)), pltpu.SemaphoreType.DMA((7,)), pltpu.SemaphoreType.DMA],
    )(block, *after)


def _cols(o):
    return jnp.transpose(o, (1, 0, 2)).reshape(o.shape[1], -1)


def _uncols(full):
    return jnp.transpose(full.reshape(full.shape[0], 4, -1), (1, 0, 2))


def _rope_pad(x1, x2):
    z = jnp.zeros_like(x1)
    return jnp.concatenate([x1, z, x2, z], axis=-1)


def _head_pad(w, heads):
    r = w.reshape(w.shape[0], heads, QK_HEAD)
    half = QK_ROPE // 2
    out = jnp.concatenate([r[..., :QK_NOPE], _rope_pad(r[..., QK_NOPE:QK_NOPE + half], r[..., QK_NOPE + half:])], axis=-1)
    return out.reshape(w.shape[0], heads * HEAD_PAD)


def _head_unpad(w, heads):
    r = w.reshape(w.shape[0], heads, HEAD_PAD)
    half = QK_ROPE // 2
    out = jnp.concatenate([r[..., :QK_NOPE], r[..., QK_NOPE:QK_NOPE + half],
                           r[..., QK_NOPE + 2 * half:QK_NOPE + 3 * half]], axis=-1)
    return out.reshape(w.shape[0], heads * QK_HEAD)


class _Dims:
    def __init__(self, d, seq):
        self.d = d
        self.seq = seq
        self.t_real = N_META + seq
        self.t = -(-self.t_real // LANES) * LANES
        self.dc = d // 2
        self.dp = d // 2
        self.pg = self.dp // len(POOL_WINDOWS)
        self.heads = d // 128
        self.dff = 4 * d
        self.a_end = 3 * self.dc
        self.q_end = self.a_end + Q_LORA
        self.kv_end = self.q_end + KV_LORA
        self.kr_end = self.kv_end + QK_ROPE
        self.pool_end = self.kr_end + self.dp
        self.d_in = self.pool_end + 3 * d
        self.r_pool = 3 * self.dc
        self.r_q = self.r_pool + self.dp
        self.r_kv = self.r_q + Q_LORA
        self.r_kr = self.r_kv + KV_LORA
        self.r_width = self.r_kr + HEAD_PAD


def _split_cols(a):
    return jnp.moveaxis(a.reshape(a.shape[:-1] + (2, a.shape[-1] // 2)), -2, -3)


def _join_cols(a):
    a = jnp.moveaxis(a, -3, -2)
    return a.reshape(a.shape[:-2] + (a.shape[-2] * a.shape[-1],))


def _in_weights(dm, pieces):
    w_t = _join_cols(pieces).reshape(dm.d_in, dm.d)
    half = QK_ROPE // 2
    kr = w_t[dm.kv_end:dm.kr_end]
    zeros = jnp.zeros((half, dm.d), BF16)
    kr_p = jnp.concatenate([kr[:half], zeros, kr[half:], zeros, jnp.zeros((HEAD_PAD - LANES, dm.d), BF16)], axis=0)
    return dict(
        wg_t=w_t[dm.pool_end:],
        wr_t=jnp.concatenate([w_t[:dm.a_end], w_t[dm.kr_end:dm.pool_end], w_t[dm.a_end:dm.kv_end], kr_p], axis=0))


def _other_weights(dm, g):
    out = {}
    if "w_ukv" in g:
        w_ukv = _cols(g["w_ukv"]).reshape(KV_LORA, dm.heads, QK_NOPE + V_HEAD)
        out["wkn"] = w_ukv[:, :, :QK_NOPE].reshape(KV_LORA, dm.heads * QK_NOPE)
        out["wv"] = w_ukv[:, :, QK_NOPE:].reshape(KV_LORA, dm.heads * V_HEAD)
    if "w_uq" in g:
        out["wuq"] = _head_pad(_cols(g["w_uq"]), dm.heads)
    if "pool_w" in g:
        out["wp"] = jnp.transpose(g["pool_w"], (1, 0, 2, 3)).reshape(len(POOL_WINDOWS), dm.pg, dm.pg)
    for name, key in (("w_branch_a", "wba"), ("w_branch_c", "wbc"), ("w_up", "wup")):
        if name in g:
            out[key] = _cols(g[name])
    for name, key in (("w_branch_b", "wbb"), ("w_o", "wo"), ("w_down", "wdn")):
        if name in g:
            out[key] = g[name].reshape(-1, dm.d)
    return out


def _small_weights(small):
    return dict(
        conv_w=small["conv_w"],
        attn_norm=small["attn_norm"][None], mlp_norm=small["mlp_norm"][None],
        q_lat_norm=small["q_lat_norm"][None], kv_lat_norm=small["kv_lat_norm"][None],
        q_norm=_head_pad(small["q_norm"][None], 1), k_norm=_head_pad(small["k_norm"][None], 1),
        pool_scale=small["pool_scale"][None],
    )


def _grad_piece(dm, dw, name):
    half = QK_ROPE // 2
    rows = lambda a: a.reshape((4, a.shape[0] // 4) + a.shape[1:])
    if name == "w_in":
        dwr, dwg = dw["wr_t"], dw["wg_t"]
        d_t = jnp.concatenate([
            dwr[:, :dm.r_pool], dwr[:, dm.r_q:dm.r_kr], dwr[:, dm.r_kr:dm.r_kr + half],
            dwr[:, dm.r_kr + 2 * half:dm.r_kr + 3 * half], dwr[:, dm.r_pool:dm.r_q], dwg], axis=1)
        out = d_t.reshape(2, 4, d_t.shape[1] // 4, d_t.shape[2])
    elif name == "w_ukv":
        out = _uncols(jnp.concatenate([dw["wkn"].reshape(KV_LORA, dm.heads, QK_NOPE),
                                       dw["wv"].reshape(KV_LORA, dm.heads, V_HEAD)], axis=-1).reshape(KV_LORA, -1))
    elif name == "w_uq":
        out = _uncols(_head_unpad(dw["wuq"], dm.heads))
    elif name == "pool_w":
        out = jnp.transpose(dw["wp"].reshape(len(POOL_WINDOWS), 4, dm.pg // 4, dm.pg), (1, 0, 2, 3))
    elif name in ("w_branch_a", "w_branch_c", "w_up"):
        out = dw[{"w_branch_a": "wba", "w_branch_c": "wbc", "w_up": "wup"}[name]]
    else:
        out = rows(dw[{"w_branch_b": "wbb", "w_o": "wo", "w_down": "wdn"}[name]])
    return out.astype(BF16)


def _layer_fwd(dm, W, x, cos_t, sin_t, tag, more=None, h=None):
    n = lambda s: f"{s}_{tag}"
    if h is None:
        h = _rms_fwd(x, W["attn_norm"], name=n("attn_norm"))
    gl = _mm(h, W["wg_t"], name=n("proj_gates"), tb=True, out_dtype=BF16)
    rest = _mm(h, W["wr_t"], name=n("proj_rest"), tb=True)
    if more is not None:
        W.update(more("after_proj", rest))
    y_a = _conv_fwd(rest, W["conv_w"], name=n("conv"), dc=dm.dc)
    y_c = _pool_fwd(rest, W["wp"], W["pool_scale"], name=n("pool"), seg0=dm.r_pool // dm.pg, pg=dm.pg)
    q_lat = _rms_fwd(rest, W["q_lat_norm"], name=n("q_lat_norm"), width=Q_LORA, seg=dm.r_q // Q_LORA)
    kv_lat = _rms_fwd(rest, W["kv_lat_norm"], name=n("kv_lat_norm"), width=KV_LORA, seg=dm.r_kv // KV_LORA)
    q_raw = _mm(q_lat, W["wuq"], name=n("up_q"), out_dtype=BF16)
    k_nope = _mm(kv_lat, W["wkn"], name=n("up_k"), out_dtype=BF16)
    v = _mm(kv_lat, W["wv"], name=n("up_v"), out_dtype=BF16)
    q, k = _qk_fwd(q_raw, k_nope, rest, cos_t, sin_t, W["q_norm"], W["k_norm"], name=n("qk_norm_rope"),
                   heads=dm.heads, kr_seg=dm.r_kr // HEAD_PAD)
    y_b, lse = _flash_fwd(q, k, v, name=n("attention"), heads=dm.heads)
    if more is not None:
        W.update(more("after_attention", y_b))
    pa = _mm(y_a, W["wba"], name=n("branch_a"), out_dtype=BF16)
    pb = _mm(y_b, W["wbb"], name=n("branch_b"), out_dtype=BF16, after=W.pop("pin", ()))
    pc = _mm(y_c, W["wbc"], name=n("branch_c"), out_dtype=BF16)
    merged = _merge_fwd(gl, pa, pb, pc, name=n("merge"), d=dm.d)
    x1 = _mm(merged, W["wo"], name=n("out_proj"), add=x)
    h2 = _rms_fwd(x1, W["mlp_norm"], name=n("mlp_norm"))
    if more is not None:
        W.update(more("before_mlp", h2))
    up, act = _mm(h2, W["wup"], name=n("mlp_up"), epi="relu2")
    x2 = _mm(act, W["wdn"], name=n("mlp_down"), add=x1, tm=704, tk=4096)
    saved = dict(x=x, h=h, gl=gl, rest=rest, y_a=y_a, y_c=y_c, q_lat=q_lat, kv_lat=kv_lat, q_raw=q_raw, k_nope=k_nope,
                 v=v, q=q, k=k, y_b=y_b, lse=lse, pa=pa, pb=pb, pc=pc, merged=merged, x1=x1, h2=h2, up=up, act=act)
    return x2, saved


def _layer_bwd(dm, W, S, dx2, dx2_b, cos_t, sin_t, tag, hook=None):
    n = lambda s: f"{s}_{tag}"
    dw, ds = {}, {}
    if hook is None:
        hook = lambda point, t, dw_so_far: ()
    dup = _mm(dx2_b, W["wdn"], name=n("d_mlp_down"), tb=True, aux=S["up"], epi="drelu2", out_dtype=BF16,
              after=hook("start", dx2, dw))
    dw["wdn"] = _mm(S["act"], dx2_b, name=n("dw_mlp_down"), ta=True, tm=512, out_dtype=BF16)
    dh2 = _mm(dup, W["wup"], name=n("d_mlp_up"), tb=True, tm=704, tk=4096)
    dw["wup"] = _mm(S["h2"], dup, name=n("dw_mlp_up"), ta=True, tm=512, out_dtype=BF16, pieces=4)
    dx1, dx1_b, ds["mlp_norm"] = _rms_bwd(dh2, S["x1"], W["mlp_norm"], name=n("d_mlp_norm"), res=dx2, bf16_copy=True)
    dmerged = _mm(dx1_b, W["wo"], name=n("d_out_proj"), tb=True, after=hook("after_mlp", dx1, dw))
    dw["wo"] = _mm(S["merged"], dx1_b, name=n("dw_out_proj"), ta=True, tm=512, out_dtype=BF16)
    dpa, dpb, dpc, dg0, dg1, dg2 = _merge_bwd(dmerged, S["gl"], S["pa"], S["pb"], S["pc"], name=n("d_merge"), d=dm.d)
    dgl = jnp.concatenate([dg0, dg1, dg2], axis=1)
    dy_a = _mm(dpa, W["wba"], name=n("d_branch_a"), tb=True)
    dw["wba"] = _mm(S["y_a"], dpa, name=n("dw_branch_a"), ta=True, tm=512, out_dtype=BF16, pieces=4)
    dy_b = _mm(dpb, W["wbb"], name=n("d_branch_b"), tb=True, out_dtype=BF16)
    dw["wbb"] = _mm(S["y_b"], dpb, name=n("dw_branch_b"), ta=True, tm=512, out_dtype=BF16)
    dy_c = _mm(dpc, W["wbc"], name=n("d_branch_c"), tb=True)
    dw["wbc"] = _mm(S["y_c"], dpc, name=n("dw_branch_c"), ta=True, tm=512, out_dtype=BF16, pieces=4)
    dq, dk, dv = _flash_bwd(S["q"], S["k"], S["v"], S["y_b"], dy_b, S["lse"], name=n("d_attention"), heads=dm.heads,
                            after=hook("before_attention", dw["wbc"], dw))
    after_attention = hook("after_attention", dq, dw)
    dq_raw, dk_nope, dk_rope, dgq, dgk = _qk_bwd(
        dq, dk, S["q_raw"], S["k_nope"], S["rest"], cos_t, sin_t, W["q_norm"], W["k_norm"], name=n("d_qk_norm_rope"),
        heads=dm.heads, kr_seg=dm.r_kr // HEAD_PAD, after=after_attention)
    ds["q_norm"] = _head_unpad(dgq, 1)
    ds["k_norm"] = _head_unpad(dgk, 1)
    dkv_v = _mm(dv, W["wv"], name=n("d_up_v"), tb=True)
    dq_lat_n = _mm(dq_raw, W["wuq"], name=n("d_up_q"), tb=True, after=hook("after_qk", dq_raw, dw))
    dw["wuq"] = _mm(S["q_lat"], dq_raw, name=n("dw_up_q"), ta=True, tm=512)
    dkv_lat_n = _mm(dk_nope, W["wkn"], name=n("d_up_k"), tb=True, add=dkv_v)
    dw["wkn"] = _mm(S["kv_lat"], dk_nope, name=n("dw_up_k"), ta=True, tm=512)
    dw["wv"] = _mm(S["kv_lat"], dv, name=n("dw_up_v"), ta=True, tm=512)
    dq_lat, ds["q_lat_norm"] = _rms_bwd(dq_lat_n, S["rest"], W["q_lat_norm"], name=n("d_q_lat_norm"), width=Q_LORA,
                                        seg=dm.r_q // Q_LORA, out_dtype=BF16)
    dkv_lat, ds["kv_lat_norm"] = _rms_bwd(dkv_lat_n, S["rest"], W["kv_lat_norm"], name=n("d_kv_lat_norm"), width=KV_LORA,
                                          seg=dm.r_kv // KV_LORA, out_dtype=BF16)
    du, db, dc, ds["conv_w"] = _conv_bwd(S["rest"], W["conv_w"], dy_a, name=n("d_conv"), dc=dm.dc)
    dpool, dw["wp"], ds["pool_scale"] = _pool_bwd(S["rest"], W["wp"], W["pool_scale"], dy_c, name=n("d_pool"),
                                                  seg0=dm.r_pool // dm.pg, pg=dm.pg)
    drest = jnp.concatenate([du, db, dc, dpool, dq_lat, dkv_lat, dk_rope], axis=1)
    dw["wg_t"] = _mm(dgl, S["h"], name=n("dw_proj_gates"), ta=True, tm=512, out_dtype=BF16, pieces=2)
    dw["wr_t"] = _mm(drest, S["h"], name=n("dw_proj_rest"), ta=True, tm=512, out_dtype=BF16, pieces=2)
    dh_g = _mm(dgl, W["wg_t"], name=n("d_proj_gates"), tm=704, tk=3072, after=hook("after_dw_in", dw["wr_t"], dw))
    dh = _mm(drest, W["wr_t"], name=n("d_proj_rest"), add=dh_g, tm=704, tk=2688, after=hook("after_dh_gates", dh_g, dw))
    dx, dx_b, ds["attn_norm"] = _rms_bwd(dh, S["x"], W["attn_norm"], name=n("d_attn_norm"), res=dx1, bf16_copy=True)
    return dx, dx_b, dw, ds


BIG = ("w_in", "w_uq", "w_ukv", "pool_w", "w_branch_a", "w_branch_b", "w_branch_c", "w_o", "w_up", "w_down")
REPLICATED = ("attn_norm", "q_lat_norm", "kv_lat_norm", "q_norm", "k_norm", "pool_scale", "mlp_norm")
WEIGHTS = ("meta_tokens", "attn_norm", "w_in", "conv_w", "q_lat_norm", "kv_lat_norm", "w_uq", "w_ukv", "q_norm",
           "k_norm", "pool_w", "pool_scale", "w_branch_a", "w_branch_b", "w_branch_c", "w_o", "mlp_norm", "w_up",
           "w_down")


def _pack(arrays):
    flat = jnp.concatenate([a.reshape(-1).astype(F32) for a in arrays])
    pad = (-flat.shape[0]) % (8 * LANES)
    return jnp.pad(flat, (0, pad)).reshape(-1, LANES)


def _unpack(flat, shapes):
    out, pos = [], 0
    flat = flat.reshape(-1)
    for shp in shapes:
        size = math.prod(shp)
        out.append(flat[pos:pos + size].reshape(shp))
        pos += size
    return out


def _update(w, g, m, v, name):
    shp = w.shape
    to2 = lambda a: a.reshape(-1, shp[-1])
    delta, nm, nv = _adamw(to2(w), to2(g), to2(m), to2(v), name=name)
    return delta.reshape(shp), nm.reshape(shp), nv.reshape(shp)


def _step(args):
    x = args["x"][0]
    seq, d = x.shape
    dm = _Dims(d, seq)
    xi, yi, ci = _coords()
    chip = 2 * xi + yi

    small_w = _gather_all(_pack([args["conv_w"], args["meta_tokens"]]), name="gather_small_weights")
    args = dict(args)
    for p in ("", "m_", "v_"):
        args[p + "w_in"] = jnp.swapaxes(args[p + "w_in"], 1, 2)
    order = [(k, l) for l in range(2) for k in BIG]
    last = ("w_up", "w_down")
    group_names = [[("w_in", 0)], [(k, 0) for k in BIG[1:] if k not in last], [(k, 0) for k in last],
                   [(k, 1) for k in BIG]]
    first, others = order[0], order[1:]
    shards = {first: _split_cols(args["w_in"][0].astype(BF16))}
    lands = {first: lax.empty((4,) + shards[first].shape, BF16)}
    sems, thru, token = _start_copies([shards[first], lands[first], small_w], [_ici_gather_plan([(0, 1)])],
                                      name="start_gather_ici_first")
    shards[first], lands[first], small_w = thru
    zero = token[0, 0]
    for n in others:
        shards[n] = (args[n[0]][n[1]] + zero).astype(BF16)
        if n[0] == "w_in":
            shards[n] = _split_cols(shards[n])
        lands[n] = lax.empty((4,) + shards[n].shape, BF16)
    at = {n: i for i, n in enumerate(others)}
    sems_b, thru, token_b = _start_copies(
        [shards[n] for n in others] + [lands[n] for n in others] + [token],
        [_ici_gather_plan([(at[n], len(others) + at[n]) for n in g]) for g in group_names[1:]], name="start_gather_ici")
    sems = sems + sems_b
    for i, n in enumerate(others):
        shards[n], lands[n] = thru[i], thru[len(others) + i]

    def finish_gather(g, after, tag):
        names = group_names[g]
        k = len(names)
        plan, _ = _ici_gather_plan([(i, k + i) for i in range(k)])
        got = _wait_copies([shards[n] for n in names] + [lands[n] for n in names], sems[g], plan, after,
                           name=f"wait_gather_ici_{tag}")
        for i, n in enumerate(names):
            shards[n] = got[i]
        fwd = _d2d_forward_plan(list(range(k)))
        sems2, bufs2, tok2 = _start_copies(got[k:], [fwd], name=f"start_gather_d2d_{tag}")
        return names, bufs2, sems2[0], fwd[0], tok2

    def land_gather(pending, after, tag):
        names, bufs2, sems2, plan, tok2 = pending
        done = _wait_copies(bufs2, sems2, plan, tok2 if after is None else after, name=f"wait_gather_d2d_{tag}")
        return {n[0]: buf for n, buf in zip(names, done)}

    conv_shape, meta_shape = args["conv_w"].shape, args["meta_tokens"].shape
    per_chip = [_unpack(small_w[2 * j], [conv_shape, meta_shape]) for j in range(4)]
    conv_full = jnp.concatenate([p[0] for p in per_chip], axis=-1)
    meta_full = jnp.concatenate([p[1] for p in per_chip], axis=-1)

    layers = []
    for l in range(2):
        small = {k: args[k][l] for k in REPLICATED}
        small["conv_w"] = conv_full[l]
        layers.append(_small_weights(small))

    pos = jnp.arange(dm.t, dtype=F32)
    inv = ROPE_THETA ** (-jnp.arange(0, QK_ROPE, 2, dtype=F32) / QK_ROPE)
    ang = pos[:, None] * inv[None, :]
    cos_t = _rope_pad(jnp.cos(ang), jnp.cos(ang))
    sin_t = _rope_pad(-jnp.sin(ang), jnp.sin(ang))
    tail = jnp.zeros((dm.t - dm.t_real, d), F32) + zero
    h0 = jnp.concatenate([meta_full, x, tail], axis=0)
    target = jnp.concatenate([jnp.zeros((N_META, d), F32), args["loss_target"][0], tail], axis=0)

    h_first = _rms_fwd(h0, layers[0]["attn_norm"], name="attn_norm_l0", after=(token, token_b))
    layers[0].update(_in_weights(dm, land_gather(finish_gather(0, h_first, "l0_in"), None, "l0_in")["w_in"]))
    pending = {}

    def rest_of_layer0(point, after):
        if point == "after_proj":
            return _other_weights(dm, land_gather(finish_gather(1, after, "l0_mid"), None, "l0_mid"))
        if point == "after_attention":
            pending["mlp"] = finish_gather(2, after, "l0_mlp")
            return {"pin": (pending["mlp"][4],)}
        return _other_weights(dm, land_gather(pending["mlp"], after, "l0_mlp"))

    h1, saved0 = _layer_fwd(dm, layers[0], h0, cos_t, sin_t, "l0", more=rest_of_layer0, h=h_first)
    g1 = land_gather(finish_gather(3, saved0["y_b"], "l1"), h1, "l1")
    layers[1].update(_in_weights(dm, g1["w_in"]))
    layers[1].update(_other_weights(dm, g1))
    h2, saved1 = _layer_fwd(dm, layers[1], h1, cos_t, sin_t, "l1")
    sq, dy, dy_b = _loss(h2, target, name="loss_head", first=N_META, last=dm.t_real)
    loss = lax.psum(0.5 / d * sq[0, 0], ("x", "y", "c"))
    core, chip_flags = _one_hot(ci, 2), _one_hot(chip, 4)
    core_index = jnp.reshape(ci, (1,)).astype(jnp.int32)
    chip_index = jnp.reshape(chip, (1,)).astype(jnp.int32)

    class Reduce:
        def __init__(self, names, dw, tag):
            self.names, self.tag, self.nb = names, tag, len(names)
            self.idx = [(i, self.nb + i) for i in range(self.nb)]
            parts = [_grad_piece(dm, dw, k) for k in names]
            parts = [p if k == "w_in" else _as3d(p) for p, k in zip(parts, names)]
            recv = [lax.empty((4,) + p.shape[2:] if k == "w_in" else (4, p.shape[1] // 2, p.shape[2]), BF16)
                    for p, k in zip(parts, names)]
            self.plan = _swap_half_plan(self.idx)
            self.sems, self.bufs, self.token = _start_copies(parts + recv, [self.plan], name=f"start_swap_{tag}")

        def _land(self, after, what):
            return _wait_copies(self.bufs, self.sems[0], self.plan[0], self.token if after is None else after,
                                name=f"wait_{what}_{self.tag}")

        def scatter(self, after=None):
            got = self._land(after, "swap")
            pairs = [_pair_sum(got[i], got[j], core_index, name=f"pair_sum_{k}_{self.tag}")
                     for (i, j), k in zip(self.idx, self.names)]
            self.plan = _scatter_plan(self.idx)
            self.sems, self.bufs, self.token = _start_copies(pairs + [lax.empty(p.shape, BF16) for p in pairs],
                                                             [self.plan], name=f"start_scatter_{self.tag}")
            return self.token

        def totals(self, after=None):
            got = self._land(after, "scatter")
            sums = [_chip_sum(got[i], got[j], chip_flags, chip_index, name=f"chip_sum_{k}_{self.tag}")
                    for (i, j), k in zip(self.idx, self.names)]
            self.plan = _swap_total_plan(self.idx)
            self.sems, self.bufs, self.token = _start_copies(sums + [lax.empty(t.shape, F32) for t in sums],
                                                             [self.plan], name=f"start_swap_total_{self.tag}")
            return self.token

        def finish(self, after=None):
            got = self._land(after, "swap_total")
            return {k: (got[i], got[j]) for (i, j), k in zip(self.idx, self.names)}

    dh1, dh1_b, dw1, ds1 = _layer_bwd(dm, layers[1], saved1, dy, dy_b, cos_t, sin_t, "l1",
                               hook=lambda point, t, dw: (loss.reshape(1, 1),) if point == "start" else ())
    early = ("w_down", "w_up", "w_o", "w_branch_a", "w_branch_b", "w_branch_c")
    late = tuple(k for k in BIG if k not in early)
    stage = {}

    def during_layer0(point, t, dw):
        if point == "start":
            stage["l1"] = Reduce(BIG, dw1, "l1")
            return (stage["l1"].token,)
        if point == "after_mlp":
            return (stage["l1"].scatter(after=t),)
        if point == "before_attention":
            stage["l0a"] = Reduce(early, dw, "l0a")
            return (stage["l0a"].token,)
        if point == "after_attention":
            return (stage["l1"].totals(after=t), stage["l0a"].scatter(after=t))
        if point == "after_qk":
            stage["red1"] = stage["l1"].finish(after=t)
            return ()
        if point == "after_dw_in":
            tok = stage["l0a"].totals(after=t)
            stage["l0b"] = Reduce(late, dw, "l0b")
            return (tok, stage["l0b"].token)
        return (stage["l0b"].scatter(after=t),)

    dh0, _, dw0, ds0 = _layer_bwd(dm, layers[0], saved0, dh1, dh1_b, cos_t, sin_t, "l0", hook=during_layer0)
    grad_x = dh0[N_META:dm.t_real][None]
    red1 = stage["red1"]
    grads, delta, new_m, new_v = {}, {}, {}, {}

    def adamw_big(k, layer, red, prev, after):
        shp = args[k].shape
        wmv = [args[p + k].reshape(2, -1, shp[-1]) for p in ("", "m_", "v_")]
        return _adamw_layer(*wmv, *red[k], core, layer, prev, name=f"adamw_{k}_l{layer}", col_halves=k == "w_in",
                            after=after)

    def keep(k, out):
        shp = args[k].shape
        out = [o.reshape(shp) for o in out]
        grads[k], delta[k], new_m[k], new_v[k] = [jnp.swapaxes(o, 1, 2) for o in out] if k == "w_in" else out

    half_done = {}
    pin = dh0
    for k in BIG:
        half_done[k] = adamw_big(k, 1, red1, None, (pin,))
        pin = half_done[k][0]
    red0a = stage["l0a"].finish(after=pin)
    for k in early:
        out = adamw_big(k, 0, red0a, half_done[k], ())
        keep(k, out)
        pin = out[0]

    small_names = REPLICATED + ("conv_w",)
    small_parts = [jnp.stack([ds0[k].reshape(ds0[k].shape[-2:] if k == "conv_w" else (-1,)),
                              ds1[k].reshape(ds1[k].shape[-2:] if k == "conv_w" else (-1,))]) for k in small_names]
    small_parts.append(dh0[:N_META])
    small_all = _gather_all(_pack(small_parts), name="gather_small_grads", after=(pin,))
    small_sum = _sum_stack(small_all, name="sum_small_grads", out_dtype=F32)
    small_g = dict(zip(small_names + ("meta_tokens",), _unpack(small_sum, [p.shape for p in small_parts])))
    for k in REPLICATED:
        grads[k] = small_g[k]
    dcw = conv_shape[-1]
    grads["conv_w"] = lax.dynamic_slice_in_dim(small_g["conv_w"], chip * dcw, dcw, axis=2)
    dmeta = meta_shape[-1]
    grads["meta_tokens"] = lax.dynamic_slice_in_dim(small_g["meta_tokens"], chip * dmeta, dmeta, axis=1)

    stage["l0b"].totals(after=small_sum)
    red0b = stage["l0b"].finish()
    for k in late:
        keep(k, adamw_big(k, 0, red0b, half_done[k], ()))
    for k in WEIGHTS:
        if k not in BIG:
            grads[k] = grads[k].reshape(args[k].shape)
            delta[k], new_m[k], new_v[k] = _update(args[k], grads[k], args["m_" + k], args["v_" + k], f"adamw_{k}")
    return (loss, grad_x, *[grads[k] for k in WEIGHTS], *[delta[k] for k in WEIGHTS],
            *[new_m[k] for k in WEIGHTS], *[new_v[k] for k in WEIGHTS])


def kernel(x, meta_tokens, attn_norm, w_in, conv_w, q_lat_norm, kv_lat_norm, w_uq, w_ukv, q_norm, k_norm, pool_w, pool_scale, w_branch_a, w_branch_b, w_branch_c, w_o, mlp_norm, w_up, w_down, loss_target, m_meta_tokens, m_attn_norm, m_w_in, m_conv_w, m_q_lat_norm, m_kv_lat_norm, m_w_uq, m_w_ukv, m_q_norm, m_k_norm, m_pool_w, m_pool_scale, m_w_branch_a, m_w_branch_b, m_w_branch_c, m_w_o, m_mlp_norm, m_w_up, m_w_down, v_meta_tokens, v_attn_norm, v_w_in, v_conv_w, v_q_lat_norm, v_kv_lat_norm, v_w_uq, v_w_ukv, v_q_norm, v_k_norm, v_pool_w, v_pool_scale, v_w_branch_a, v_w_branch_b, v_w_branch_c, v_w_o, v_mlp_norm, v_w_up, v_w_down):
    return _step(dict(locals()))
```

```python
import functools
import math

import jax
import jax.numpy as jnp
from jax import lax
from jax.experimental import pallas as pl
from jax.experimental.pallas import tpu as pltpu

F32 = jnp.float32
BF16 = jnp.bfloat16
MESH = pl.DeviceIdType.MESH

EPS = 1e-6
N_META = 16
QK_NOPE = 128
QK_ROPE = 64
QK_HEAD = QK_NOPE + QK_ROPE
V_HEAD = 128
HEAD_PAD = 256
Q_LORA = 512
KV_LORA = 512
ROPE_THETA = 10000.0
POOL_WINDOWS = (2, 4, 8, 16)
HALO = 16
LANES = 128
ADAM_LR = 0.001
ADAM_B1 = 0.9
ADAM_B2 = 0.999
ADAM_EPS = 1e-08
ADAM_WD = 0.01
ADAM_STEP = 10
VMEM_LIMIT = 52 * 1024 * 1024
NEG = -1e30
ATTN_SCALE = QK_HEAD ** -0.5
LOG2_E = 1.4426950408889634
Q_FOLD = ATTN_SCALE * LOG2_E


def _tile(n, target, mult=LANES):
    best = None
    for t in range(mult, min(n, target) + 1, mult):
        if n % t == 0:
            best = t
    return n if best is None else best


def _params(sem=None):
    return pltpu.CompilerParams(dimension_semantics=sem, vmem_limit_bytes=VMEM_LIMIT)


def _mm(a, b, *, name, ta=False, tb=False, add=None, aux=None, epi=None, out_dtype=F32,
        tm=1056, tn=1024, tk=None, after=(), pieces=None):
    if ta:
        K, M = a.shape
    else:
        M, K = a.shape
    if tb:
        N, kb = b.shape
    else:
        kb, N = b.shape
    assert K == kb, (a.shape, b.shape, ta, tb)
    tm = _tile(M, tm, LANES if ta else 16)
    tn = _tile(N if pieces is None else N // pieces, tn, LANES)
    tk = K if tk is None else _tile(K, tk, LANES if (not ta or tb) else 16)
    nk = K // tk
    a_bytes, b_bytes = a.size * a.dtype.itemsize, b.size * b.dtype.itemsize
    j_outer = nk == 1 and a_bytes * (N // tn) + b_bytes < a_bytes + b_bytes * (M // tm)
    grid = (N // tn, M // tm, nk) if j_outer else (M // tm, N // tn, nk)
    row = (lambda g0, g1: g1) if j_outer else (lambda g0, g1: g0)
    col = (lambda g0, g1: g0) if j_outer else (lambda g0, g1: g1)

    if ta:
        a_spec = pl.BlockSpec((tk, tm), lambda g0, g1, k: (k, row(g0, g1)))
    else:
        a_spec = pl.BlockSpec((tm, tk), lambda g0, g1, k: (row(g0, g1), k))
    if tb:
        b_spec = pl.BlockSpec((tn, tk), lambda g0, g1, k: (col(g0, g1), k))
    else:
        b_spec = pl.BlockSpec((tk, tn), lambda g0, g1, k: (k, col(g0, g1)))
    o_spec = pl.BlockSpec((tm, tn), lambda g0, g1, k: (row(g0, g1), col(g0, g1)))
    per = None if pieces is None else N // pieces // tn
    in_specs = [a_spec, b_spec]
    operands = [a, b]
    if add is not None:
        in_specs.append(o_spec)
        operands.append(add)
    if aux is not None:
        in_specs.append(o_spec)
        operands.append(aux)
    after = tuple(after)
    in_specs += [pl.BlockSpec(memory_space=pl.ANY)] * len(after)
    operands += list(after)
    if epi == "relu2":
        out_shape = (jax.ShapeDtypeStruct((M, N), BF16), jax.ShapeDtypeStruct((M, N), BF16))
        out_specs = (o_spec, o_spec)
    elif pieces is not None:
        out_shape = jax.ShapeDtypeStruct((pieces, M, N // pieces), out_dtype)
        out_specs = pl.BlockSpec((1, tm, tn), lambda g0, g1, k: (col(g0, g1) // per, row(g0, g1), col(g0, g1) % per))
    else:
        out_shape = jax.ShapeDtypeStruct((M, N), out_dtype)
        out_specs = o_spec
    dims =(((0 if ta else 1,), (1 if tb else 0,)), ((), ()))
    has_add, has_aux = add is not None, aux is not None

    def body(*refs):
        a_ref, b_ref = refs[0], refs[1]
        pos = 2
        add_ref = aux_ref = None
        if has_add:
            add_ref = refs[pos]
            pos += 1
        if has_aux:
            aux_ref = refs[pos]
            pos += 1
        pos += len(after)
        n_out = 2 if epi == "relu2" else 1
        out_refs = refs[pos:pos + n_out]
        acc_ref = refs[pos + n_out] if nk > 1 else None

        part = lax.dot_general(a_ref[...].astype(BF16), b_ref[...].astype(BF16), dims,
                               preferred_element_type=F32)

        def finish(acc):
            if has_add:
                acc = acc + add_ref[...].astype(F32)
            if epi == "relu2":
                r = jnp.maximum(acc, 0.0)
                out_refs[0][...] = acc.astype(BF16)
                out_refs[1][...] = (r * r).astype(BF16)
            elif epi == "drelu2":
                u = aux_ref[...].astype(F32)
                out_refs[0][...] = (acc * (2.0 * jnp.maximum(u, 0.0))).astype(out_dtype)
            else:
                out_refs[0][...] = acc.astype(out_dtype).reshape(out_refs[0].shape)

        if nk == 1:
            finish(part)
        else:
            k = pl.program_id(2)

            @pl.when(k == 0)
            def _():
                acc_ref[...] = part

            @pl.when(k > 0)
            def _():
                acc_ref[...] += part

            @pl.when(k == nk - 1)
            def _():
                finish(acc_ref[...])

    scratch = [pltpu.VMEM((tm, tn), F32)] if nk > 1 else []
    return pl.pallas_call(
        body, name=name, grid=grid, in_specs=in_specs, out_specs=out_specs, out_shape=out_shape,
        scratch_shapes=scratch, compiler_params=_params(("parallel", "parallel", "arbitrary")),
    )(*operands)


def _rms_fwd(x, g, *, name, width=None, seg=0, tm=384, after=()):
    T = x.shape[0]
    width = x.shape[1] if width is None else width
    tm = _tile(T, tm, 16)
    after = tuple(after)

    def body(x_ref, g_ref, *rest):
        xf = x_ref[...].astype(F32)
        r = lax.rsqrt(jnp.mean(xf * xf, axis=-1, keepdims=True) + EPS)
        rest[-1][...] = (xf * r * g_ref[...]).astype(BF16)

    return pl.pallas_call(
        body, name=name, grid=(T // tm,),
        in_specs=[pl.BlockSpec((tm, width), lambda i: (i, seg)), pl.BlockSpec((1, width), lambda i: (0, 0))]
        + [pl.BlockSpec(memory_space=pl.ANY)] * len(after),
        out_specs=pl.BlockSpec((tm, width), lambda i: (i, 0)),
        out_shape=jax.ShapeDtypeStruct((T, width), BF16),
        compiler_params=_params(("parallel",)),
    )(x, g, *after)


def _rms_bwd(dy, x, g, *, name, width=None, seg=0, res=None, out_dtype=F32, tm=384, bf16_copy=False):
    T = x.shape[0]
    width = x.shape[1] if width is None else width
    tm = _tile(T, tm, 16)
    has_res = res is not None

    def body(*refs):
        dy_ref, x_ref, g_ref = refs[:3]
        res_ref = refs[3] if has_res else None
        dx_ref, dg_ref = refs[4 if has_res else 3], refs[-1]
        xf = x_ref[...].astype(F32)
        dyf = dy_ref[...].astype(F32)
        r = lax.rsqrt(jnp.mean(xf * xf, axis=-1, keepdims=True) + EPS)
        xhat = xf * r
        dyh = dyf * g_ref[...]
        dx = r * (dyh - xhat * jnp.mean(dyh * xhat, axis=-1, keepdims=True))
        if has_res:
            dx = dx + res_ref[...].astype(F32)
        dx_ref[...] = dx.astype(out_dtype)
        if bf16_copy:
            refs[-2][...] = dx.astype(BF16)
        part = jnp.sum(dyf * xhat, axis=0, keepdims=True)

        @pl.when(pl.program_id(0) == 0)
        def _():
            dg_ref[...] = part

        @pl.when(pl.program_id(0) > 0)
        def _():
            dg_ref[...] += part

    row = pl.BlockSpec((tm, width), lambda i: (i, 0))
    in_specs = [row, pl.BlockSpec((tm, width), lambda i: (i, seg)), pl.BlockSpec((1, width), lambda i: (0, 0))]
    operands = [dy, x, g]
    if has_res:
        in_specs.append(row)
        operands.append(res)
    vec = pl.BlockSpec((1, width), lambda i: (0, 0))
    full = [jax.ShapeDtypeStruct((T, width), out_dtype)] + ([jax.ShapeDtypeStruct((T, width), BF16)] if bf16_copy else [])
    return pl.pallas_call(
        body, name=name, grid=(T // tm,), in_specs=in_specs,
        out_specs=tuple([row] * len(full) + [vec]),
        out_shape=tuple(full + [jax.ShapeDtypeStruct((1, width), F32)]),
        compiler_params=_params(("arbitrary",)),
    )(*operands)


def _down(ext, k):
    return pltpu.roll(ext, k, 0)


def _up(ext, k):
    return pltpu.roll(ext, ext.shape[0] - k, 0)


def _pre_halo(ref, r, R):
    start = pl.multiple_of(jnp.maximum(r * R - HALO, 0), 8)
    keep = (r > 0).astype(F32)
    return ref[pl.ds(start, HALO), :].astype(F32) * keep


def _post_halo(ref, r, R, n_chunks):
    start = pl.multiple_of(jnp.minimum(r * R + R, (n_chunks - 1) * R + R - HALO), 8)
    keep = (r < n_chunks - 1).astype(F32)
    return ref[pl.ds(start, HALO), :].astype(F32) * keep


def _chunk(ref, r, R):
    return ref[pl.ds(pl.multiple_of(r * R, 8), R), :].astype(F32)


def _conv_fwd(rest, conv_w, *, name, dc, tc=128, rows=1056):
    T = rest.shape[0]
    tc = _tile(dc, tc)
    nb = dc // tc
    R = _tile(T, rows, 16)
    n_chunks = T // R

    def body(u_ref, b_ref, c_ref, w_ref, y_ref):
        w0, w1, w2 = w_ref[0:1, :], w_ref[1:2, :], w_ref[2:3, :]

        def chunk(r, carry):
            cu = _chunk(c_ref, r, R) * _chunk(u_ref, r, R)
            ext = jnp.concatenate([_pre_halo(c_ref, r, R) * _pre_halo(u_ref, r, R), cu], axis=0)
            conv = w0 * _down(ext, 2)[HALO:] + w1 * _down(ext, 1)[HALO:] + w2 * cu
            y_ref[pl.ds(pl.multiple_of(r * R, 8), R), :] = (_chunk(b_ref, r, R) * conv).astype(BF16)
            return carry

        lax.fori_loop(0, n_chunks, chunk, 0)

    col = lambda off: pl.BlockSpec((T, tc), lambda j: (0, off * nb + j))
    return pl.pallas_call(
        body, name=name, grid=(nb,),
        in_specs=[col(0), col(1), col(2), pl.BlockSpec((3, tc), lambda j: (0, j))],
        out_specs=pl.BlockSpec((T, tc), lambda j: (0, j)),
        out_shape=jax.ShapeDtypeStruct((T, dc), BF16),
        compiler_params=_params(("parallel",)),
    )(rest, rest, rest, conv_w)


def _conv_bwd(rest, conv_w, dy, *, name, dc, tc=128, rows=1056):
    T = rest.shape[0]
    tc = _tile(dc, tc)
    nb = dc // tc
    R = _tile(T, rows, 16)
    n_chunks = T // R

    def body(u_ref, b_ref, c_ref, w_ref, dy_ref, du_ref, db_ref, dc_ref, dw_ref):
        w0, w1, w2 = w_ref[0:1, :], w_ref[1:2, :], w_ref[2:3, :]

        def chunk(r, carry):
            a0, a1, a2 = carry
            u, b, c = _chunk(u_ref, r, R), _chunk(b_ref, r, R), _chunk(c_ref, r, R)
            dy_c = _chunk(dy_ref, r, R)
            cu = c * u
            ext = jnp.concatenate([_pre_halo(c_ref, r, R) * _pre_halo(u_ref, r, R), cu], axis=0)
            cu1, cu2 = _down(ext, 1)[HALO:], _down(ext, 2)[HALO:]
            conv = w0 * cu2 + w1 * cu1 + w2 * cu
            dconv = dy_c * b
            dext = jnp.concatenate(
                [dconv, _post_halo(dy_ref, r, R, n_chunks) * _post_halo(b_ref, r, R, n_chunks)], axis=0)
            dcu = w2 * dconv + w1 * _up(dext, 1)[:R] + w0 * _up(dext, 2)[:R]
            rows_at = pl.ds(pl.multiple_of(r * R, 8), R)
            db_ref[rows_at, :] = (dy_c * conv).astype(BF16)
            du_ref[rows_at, :] = (dcu * c).astype(BF16)
            dc_ref[rows_at, :] = (dcu * u).astype(BF16)
            return (a0 + jnp.sum(dconv * cu2, axis=0, keepdims=True),
                    a1 + jnp.sum(dconv * cu1, axis=0, keepdims=True),
                    a2 + jnp.sum(dconv * cu, axis=0, keepdims=True))

        zero = jnp.zeros((1, tc), F32)
        a0, a1, a2 = lax.fori_loop(0, n_chunks, chunk, (zero, zero, zero))
        dw_ref[0:1, :] = a0
        dw_ref[1:2, :] = a1
        dw_ref[2:3, :] = a2

    col = lambda off: pl.BlockSpec((T, tc), lambda j: (0, off * nb + j))
    own = pl.BlockSpec((T, tc), lambda j: (0, j))
    return pl.pallas_call(
        body, name=name, grid=(nb,),
        in_specs=[col(0), col(1), col(2), pl.BlockSpec((3, tc), lambda j: (0, j)), own],
        out_specs=(own, own, own, pl.BlockSpec((3, tc), lambda j: (0, j))),
        out_shape=(jax.ShapeDtypeStruct((T, dc), BF16),) * 3 + (jax.ShapeDtypeStruct((3, dc), F32),),
        compiler_params=_params(("parallel",)),
    )(rest, rest, rest, conv_w, dy)


def _window_count(r, R, n_rows, w, first_row_offset):
    t = lax.broadcasted_iota(jnp.int32, (n_rows, 1), 0) + (r * R + first_row_offset)
    return jnp.minimum(t + 1, w).astype(F32)


def _pool_fwd(rest, pool_w, pool_scale, *, name, seg0, pg, rows=1056):
    T = rest.shape[0]
    R = _tile(T, rows, 16)
    n_chunks = T // R
    n_groups = len(POOL_WINDOWS)

    def body(x_ref, w_ref, s_ref, y_ref):
        def run(window):
            def chunk(r, carry):
                g = _chunk(x_ref, r, R)
                s = jnp.concatenate([_pre_halo(x_ref, r, R), g], axis=0)
                k = 1
                while k < window:
                    s = s + _down(s, k)
                    k *= 2
                pooled = s[HALO:] / _window_count(r, R, R, window, 0) - g
                mixed = jnp.dot(pooled.astype(BF16), w_ref[0], preferred_element_type=F32)
                y_ref[pl.ds(pl.multiple_of(r * R, 8), R), :] = (mixed * s_ref[...]).astype(BF16)
                return carry

            lax.fori_loop(0, n_chunks, chunk, 0)

        for gi, window in enumerate(POOL_WINDOWS):
            pl.when(pl.program_id(0) == gi)(functools.partial(run, window))

    return pl.pallas_call(
        body, name=name, grid=(n_groups,),
        in_specs=[pl.BlockSpec((T, pg), lambda g: (0, seg0 + g)),
                  pl.BlockSpec((1, pg, pg), lambda g: (g, 0, 0)),
                  pl.BlockSpec((1, pg), lambda g: (0, g))],
        out_specs=pl.BlockSpec((T, pg), lambda g: (0, g)),
        out_shape=jax.ShapeDtypeStruct((T, n_groups * pg), BF16),
        compiler_params=_params(("parallel",)),
    )(rest, pool_w, pool_scale)


def _pool_bwd(rest, pool_w, pool_scale, dy, *, name, seg0, pg, rows=1056):
    T = rest.shape[0]
    R = _tile(T, rows, 16)
    n_chunks = T // R
    n_groups = len(POOL_WINDOWS)

    def body(x_ref, w_ref, s_ref, dy_ref, dx_ref, dw_ref, ds_ref):
        def run(window):
            def chunk(r, carry):
                dw_acc, ds_acc = carry
                g = _chunk(x_ref, r, R)
                s = jnp.concatenate([_pre_halo(x_ref, r, R), g], axis=0)
                k = 1
                while k < window:
                    s = s + _down(s, k)
                    k *= 2
                pooled = (s[HALO:] / _window_count(r, R, R, window, 0) - g).astype(BF16)
                mixed = jnp.dot(pooled, w_ref[0], preferred_element_type=F32)
                dy_c = _chunk(dy_ref, r, R)
                dm_ext = (jnp.concatenate([dy_c, _post_halo(dy_ref, r, R, n_chunks)], axis=0)
                          * s_ref[...]).astype(BF16)
                dpool_ext = lax.dot_general(dm_ext, w_ref[0], (((1,), (1,)), ((), ())),
                                            preferred_element_type=F32)
                a = dpool_ext / _window_count(r, R, R + HALO, window, 0)
                k = 1
                while k < window:
                    a = a + _up(a, k)
                    k *= 2
                dx_ref[pl.ds(pl.multiple_of(r * R, 8), R), :] = (a[:R] - dpool_ext[:R]).astype(BF16)
                dw_acc = dw_acc + lax.dot_general(pooled, dm_ext[:R], (((0,), (0,)), ((), ())),
                                                  preferred_element_type=F32)
                ds_acc = ds_acc + jnp.sum(dy_c * mixed, axis=0, keepdims=True)
                return dw_acc, ds_acc

            dw_acc, ds_acc = lax.fori_loop(0, n_chunks, chunk,
                                           (jnp.zeros((pg, pg), F32), jnp.zeros((1, pg), F32)))
            dw_ref[0] = dw_acc
            ds_ref[...] = ds_acc

        for gi, window in enumerate(POOL_WINDOWS):
            pl.when(pl.program_id(0) == gi)(functools.partial(run, window))

    own = pl.BlockSpec((T, pg), lambda g: (0, g))
    return pl.pallas_call(
        body, name=name, grid=(n_groups,),
        in_specs=[pl.BlockSpec((T, pg), lambda g: (0, seg0 + g)),
                  pl.BlockSpec((1, pg, pg), lambda g: (g, 0, 0)),
                  pl.BlockSpec((1, pg), lambda g: (0, g)), own],
        out_specs=(own, pl.BlockSpec((1, pg, pg), lambda g: (g, 0, 0)), pl.BlockSpec((1, pg), lambda g: (0, g))),
        out_shape=(jax.ShapeDtypeStruct((T, n_groups * pg), BF16),
                   jax.ShapeDtypeStruct((n_groups, pg, pg), F32),
                   jax.ShapeDtypeStruct((1, n_groups * pg), F32)),
        compiler_params=_params(("parallel",)),
    )(rest, pool_w, pool_scale, dy)


def _rope(r, cos_t, sin_t):
    return r * cos_t + pltpu.roll(r, LANES // 2, 1) * sin_t


def _rope_t(d, cos_t, sin_t):
    return d * cos_t + pltpu.roll(d * sin_t, LANES // 2, 1)


def _qk_fwd(q_raw, k_nope, rest, cos_t, sin_t, q_norm, k_norm, *, name, heads, kr_seg, tm=192):
    T = q_raw.shape[0]
    tm = _tile(T, tm, 16)

    def body(q_ref, kn_ref, kr_ref, c_ref, s_ref, gq_ref, gk_ref, qo_ref, ko_ref):
        cos_b, sin_b = c_ref[...], s_ref[...]
        kr = kr_ref[:, 0:LANES]
        kr_ss = jnp.sum(kr * kr, axis=-1, keepdims=True)
        gq, gk = gq_ref[...], gk_ref[...]
        for h in range(heads):
            lo = h * HEAD_PAD
            q = q_ref[:, lo:lo + HEAD_PAD].astype(F32)
            rq = lax.rsqrt(jnp.sum(q * q, axis=-1, keepdims=True) / QK_HEAD + EPS)
            qn = q * (rq * Q_FOLD) * gq
            qo_ref[:, lo:lo + LANES] = qn[:, :LANES].astype(BF16)
            qo_ref[:, lo + LANES:lo + HEAD_PAD] = _rope(qn[:, LANES:], cos_b, sin_b).astype(BF16)
            kn = kn_ref[:, h * LANES:(h + 1) * LANES].astype(F32)
            rk = lax.rsqrt((jnp.sum(kn * kn, axis=-1, keepdims=True) + kr_ss) / QK_HEAD + EPS)
            ko_ref[:, lo:lo + LANES] = (kn * rk * gk[:, :LANES]).astype(BF16)
            ko_ref[:, lo + LANES:lo + HEAD_PAD] = _rope(kr * rk * gk[:, LANES:], cos_b, sin_b).astype(BF16)

    wq, wk = heads * HEAD_PAD, heads * LANES
    return pl.pallas_call(
        body, name=name, grid=(T // tm,),
        in_specs=[pl.BlockSpec((tm, wq), lambda i: (i, 0)), pl.BlockSpec((tm, wk), lambda i: (i, 0)),
                  pl.BlockSpec((tm, HEAD_PAD), lambda i: (i, kr_seg)),
                  pl.BlockSpec((tm, LANES), lambda i: (i, 0)), pl.BlockSpec((tm, LANES), lambda i: (i, 0)),
                  pl.BlockSpec((1, HEAD_PAD), lambda i: (0, 0)), pl.BlockSpec((1, HEAD_PAD), lambda i: (0, 0))],
        out_specs=(pl.BlockSpec((tm, wq), lambda i: (i, 0)), pl.BlockSpec((tm, wq), lambda i: (i, 0))),
        out_shape=(jax.ShapeDtypeStruct((T, wq), BF16), jax.ShapeDtypeStruct((T, wq), BF16)),
        compiler_params=_params(("parallel",)),
    )(q_raw, k_nope, rest, cos_t, sin_t, q_norm, k_norm)


def _qk_bwd(dq, dk, q_raw, k_nope, rest, cos_t, sin_t, q_norm, k_norm, *, name, heads, kr_seg, tm=128, after=()):
    T = q_raw.shape[0]
    tm = _tile(T, tm, 16)
    after = tuple(after)

    def body(dq_ref, dk_ref, q_ref, kn_ref, kr_ref, c_ref, s_ref, gq_ref, gk_ref, *rest_refs):
        dqr_ref, dkn_ref, dkr_ref, dgq_ref, dgk_ref = rest_refs[len(after):]
        cos_b, sin_b = c_ref[...], s_ref[...]
        kr = kr_ref[:, 0:LANES]
        kr_ss = jnp.sum(kr * kr, axis=-1, keepdims=True)
        gq, gk = gq_ref[...], gk_ref[...]
        dgq = jnp.zeros((1, HEAD_PAD), F32)
        dgk_n = jnp.zeros((1, LANES), F32)
        dgk_r = jnp.zeros((1, LANES), F32)
        dkr = jnp.zeros((tm, LANES), F32)
        for h in range(heads):
            lo = h * HEAD_PAD
            q = q_ref[:, lo:lo + HEAD_PAD].astype(F32)
            rq = lax.rsqrt(jnp.sum(q * q, axis=-1, keepdims=True) / QK_HEAD + EPS)
            qhat = q * rq
            dqn = jnp.concatenate([dq_ref[:, lo:lo + LANES],
                                   _rope_t(dq_ref[:, lo + LANES:lo + HEAD_PAD], cos_b, sin_b)], axis=1) * ATTN_SCALE
            dgq = dgq + jnp.sum(dqn * qhat, axis=0, keepdims=True)
            dqh = dqn * gq
            dqr_ref[:, lo:lo + HEAD_PAD] = (
                rq * (dqh - qhat * (jnp.sum(dqh * qhat, axis=-1, keepdims=True) / QK_HEAD))).astype(BF16)
            kn = kn_ref[:, h * LANES:(h + 1) * LANES].astype(F32)
            rk = lax.rsqrt((jnp.sum(kn * kn, axis=-1, keepdims=True) + kr_ss) / QK_HEAD + EPS)
            khat_n, khat_r = kn * rk, kr * rk
            dkn_n = dk_ref[:, lo:lo + LANES] * (1.0 / LOG2_E)
            dkn_r = _rope_t(dk_ref[:, lo + LANES:lo + HEAD_PAD], cos_b, sin_b) * (1.0 / LOG2_E)
            dgk_n = dgk_n + jnp.sum(dkn_n * khat_n, axis=0, keepdims=True)
            dgk_r = dgk_r + jnp.sum(dkn_r * khat_r, axis=0, keepdims=True)
            dkh_n, dkh_r = dkn_n * gk[:, :LANES], dkn_r * gk[:, LANES:]
            proj = (jnp.sum(dkh_n * khat_n, axis=-1, keepdims=True)
                    + jnp.sum(dkh_r * khat_r, axis=-1, keepdims=True)) / QK_HEAD
            dkn_ref[:, h * LANES:(h + 1) * LANES] = (rk * (dkh_n - khat_n * proj)).astype(BF16)
            dkr = dkr + rk * (dkh_r - khat_r * proj)
        dkr_ref[:, 0:LANES] = dkr.astype(BF16)
        dkr_ref[:, LANES:HEAD_PAD] = jnp.zeros((tm, HEAD_PAD - LANES), BF16)
        dgk = jnp.concatenate([dgk_n, dgk_r], axis=1)

        @pl.when(pl.program_id(0) == 0)
        def _():
            dgq_ref[...] = dgq
            dgk_ref[...] = dgk

        @pl.when(pl.program_id(0) > 0)
        def _():
            dgq_ref[...] += dgq
            dgk_ref[...] += dgk

    wq, wk = heads * HEAD_PAD, heads * LANES
    row = lambda w: pl.BlockSpec((tm, w), lambda i: (i, 0))
    vec = pl.BlockSpec((1, HEAD_PAD), lambda i: (0, 0))
    return pl.pallas_call(
        body, name=name, grid=(T // tm,),
        in_specs=[row(wq), row(wq), row(wq), row(wk), pl.BlockSpec((tm, HEAD_PAD), lambda i: (i, kr_seg)),
                  row(LANES), row(LANES), vec, vec] + [pl.BlockSpec(memory_space=pl.ANY)] * len(after),
        out_specs=(row(wq), row(wk), row(HEAD_PAD), vec, vec),
        out_shape=(jax.ShapeDtypeStruct((T, wq), BF16), jax.ShapeDtypeStruct((T, wk), BF16),
                   jax.ShapeDtypeStruct((T, HEAD_PAD), BF16),
                   jax.ShapeDtypeStruct((1, HEAD_PAD), F32), jax.ShapeDtypeStruct((1, HEAD_PAD), F32)),
        compiler_params=_params(("arbitrary",)),
    )(dq, dk, q_raw, k_nope, rest, cos_t, sin_t, q_norm, k_norm, *after)


def _causal_mask(s):
    row = lax.broadcasted_iota(jnp.int32, s.shape, 0)
    col = lax.broadcasted_iota(jnp.int32, s.shape, 1)
    return jnp.where(row >= col, s, NEG)


def _flash_fwd(q, k, v, *, name, heads, tq=384, hp=2, parts=2):
    T = q.shape[0]
    tq = _tile(T, tq, LANES)
    nq = T // tq
    tr = tq // parts
    nt = (((1,), (1,)), ((), ()))
    chains = [(h, r) for h in range(hp) for r in range(parts)]

    def body(q_ref, k_ref, v_ref, o_ref, lse_ref, acc_ref):
        def q_block(i, carry):
            rows_at = [pl.ds(pl.multiple_of(i * tq + r * tr, tr), tr) for r in range(parts)]
            for c in range(len(chains)):
                acc_ref[c] = jnp.zeros((tr, V_HEAD), F32)

            def step(j, state, masked):
                k_at = pl.ds(pl.multiple_of(j * tq, tq), tq)
                new = []
                scores = [lax.dot_general(q_ref[rows_at[r], h * HEAD_PAD:(h + 1) * HEAD_PAD],
                                          k_ref[k_at, h * HEAD_PAD:(h + 1) * HEAD_PAD], nt,
                                          preferred_element_type=F32) for h, r in chains]
                for c, (s, (h, r)) in enumerate(zip(scores, chains)):
                    m, l = state[c]
                    if masked:
                        row = lax.broadcasted_iota(jnp.int32, s.shape, 0) + r * tr
                        s = jnp.where(row >= lax.broadcasted_iota(jnp.int32, s.shape, 1), s, NEG)
                    m_new = jnp.maximum(m, jnp.max(s, axis=-1, keepdims=True))
                    p = jnp.exp2(s - m_new)
                    alpha = jnp.exp2(m - m_new)
                    new.append((m_new, alpha * l + jnp.sum(p, axis=-1, keepdims=True)))
                    acc_ref[c] = alpha * acc_ref[c] + jnp.dot(p.astype(BF16), v_ref[k_at, h * V_HEAD:(h + 1) * V_HEAD],
                                                              preferred_element_type=F32)
                return tuple(new)

            init = tuple((jnp.full((tr, 1), NEG, F32), jnp.zeros((tr, 1), F32)) for _ in chains)
            state = lax.fori_loop(0, i, lambda j, st: step(j, st, False), init)
            state = step(i, state, True)
            for c, ((m, l), (h, r)) in enumerate(zip(state, chains)):
                o_ref[rows_at[r], h * V_HEAD:(h + 1) * V_HEAD] = (acc_ref[c] / l).astype(BF16)
                lse_ref[h, rows_at[r], :] = jnp.broadcast_to(m + jnp.log2(l), (tr, LANES))
            return carry

        lax.fori_loop(0, nq, q_block, 0)

    qk_spec = pl.BlockSpec((T, hp * HEAD_PAD), lambda g: (0, g))
    v_spec = pl.BlockSpec((T, hp * V_HEAD), lambda g: (0, g))
    return pl.pallas_call(
        body, name=name, grid=(heads // hp,), in_specs=[qk_spec, qk_spec, v_spec],
        out_specs=(v_spec, pl.BlockSpec((hp, T, LANES), lambda g: (g, 0, 0))),
        out_shape=(jax.ShapeDtypeStruct((T, heads * V_HEAD), BF16), jax.ShapeDtypeStruct((heads, T, LANES), F32)),
        scratch_shapes=[pltpu.VMEM((len(chains), tr, V_HEAD), F32)],
        compiler_params=_params(("parallel",)),
    )(q, k, v)


def _flash_bwd(q, k, v, o, do, lse, *, name, heads, tq=384, after=()):
    T = q.shape[0]
    tq = _tile(T, tq, LANES)
    nq = T // tq
    nt = (((1,), (1,)), ((), ()))
    tn = (((0,), (0,)), ((), ()))

    after = tuple(after)

    def body(q_ref, k_ref, v_ref, o_ref, do_ref, lse_ref, *rest):
        dq_ref, dk_ref, dv_ref, delta_ref, dv_acc_ref = rest[len(after):]
        def fill_delta(i, carry):
            at = pl.ds(pl.multiple_of(i * tq, tq), tq)
            d = jnp.sum(o_ref[at, :].astype(F32) * do_ref[at, :].astype(F32), axis=-1, keepdims=True)
            delta_ref[at, :] = jnp.broadcast_to(d, (tq, LANES))
            dq_ref[at, :] = jnp.zeros((tq, HEAD_PAD), F32)
            return carry

        lax.fori_loop(0, nq, fill_delta, 0)

        def kv_block(j, carry):
            k_at = pl.ds(pl.multiple_of(j * tq, tq), tq)

            def steps(blocks, masked):
                at = [pl.ds(pl.multiple_of(i * tq, tq), tq) for i in blocks]
                qbs = [q_ref[a, :] for a in at]
                dobs = [do_ref[a, :] for a in at]
                scores = [lax.dot_general(qb, k_ref[k_at, :], nt, preferred_element_type=F32) for qb in qbs]
                dps = [lax.dot_general(dob, v_ref[k_at, :], nt, preferred_element_type=F32) for dob in dobs]
                for a, qb, dob, sc, dp in zip(at, qbs, dobs, scores, dps):
                    if masked:
                        sc = _causal_mask(sc)
                    p = jnp.exp2(sc - lse_ref[0, a, :][:, 0:1])
                    ds = (p * (dp - delta_ref[a, :][:, 0:1])).astype(BF16)
                    dv_part = lax.dot_general(p.astype(BF16), dob, tn, preferred_element_type=F32)
                    dk_part = lax.dot_general(ds, qb, tn, preferred_element_type=F32)
                    if masked:
                        dv_acc_ref[...] = dv_part
                        dk_ref[k_at, :] = dk_part
                    else:
                        dv_acc_ref[...] += dv_part
                        dk_ref[k_at, :] += dk_part
                    dq_ref[a, :] += jnp.dot(ds, k_ref[k_at, :], preferred_element_type=F32)

            def two_blocks(t, carry):
                steps([j + 1 + 2 * t, j + 2 + 2 * t], False)
                return carry

            steps([j], True)
            rest = nq - 1 - j
            lax.fori_loop(0, rest // 2, two_blocks, 0)

            @pl.when(rest % 2 == 1)
            def _():
                steps([nq - 1], False)

            dv_ref[k_at, :] = dv_acc_ref[...].astype(BF16)
            return carry

        lax.fori_loop(0, nq, kv_block, 0)

    qk_spec = pl.BlockSpec((T, HEAD_PAD), lambda h: (0, h))
    v_spec = pl.BlockSpec((T, V_HEAD), lambda h: (0, h))
    return pl.pallas_call(
        body, name=name, grid=(heads,),
        in_specs=[qk_spec, qk_spec, v_spec, v_spec, v_spec, pl.BlockSpec((1, T, LANES), lambda h: (h, 0, 0))]
        + [pl.BlockSpec(memory_space=pl.ANY)] * len(after),
        out_specs=(qk_spec, qk_spec, v_spec),
        out_shape=(jax.ShapeDtypeStruct((T, heads * HEAD_PAD), F32), jax.ShapeDtypeStruct((T, heads * HEAD_PAD), F32),
                   jax.ShapeDtypeStruct((T, heads * V_HEAD), BF16)),
        scratch_shapes=[pltpu.VMEM((T, LANES), F32), pltpu.VMEM((tq, V_HEAD), F32)],
        compiler_params=_params(("parallel",)),
    )(q, k, v, o, do, lse, *after)


def _merge_fwd(gl, pa, pb, pc, *, name, d, tm=384, tn=1024):
    T = pa.shape[0]
    tm, tn = _tile(T, tm, 16), _tile(d, tn)
    nb = d // tn

    def body(g0, g1, g2, a, b, c, o_ref):
        f = lambda ref: ref[...].astype(F32)
        o_ref[...] = (jax.nn.sigmoid(f(g0)) * f(a) + jax.nn.sigmoid(f(g1)) * f(b)
                      + jax.nn.sigmoid(f(g2)) * f(c)).astype(BF16)

    gate = lambda n: pl.BlockSpec((tm, tn), lambda i, j: (i, n * nb + j))
    blk = pl.BlockSpec((tm, tn), lambda i, j: (i, j))
    return pl.pallas_call(
        body, name=name, grid=(T // tm, nb), in_specs=[gate(0), gate(1), gate(2), blk, blk, blk],
        out_specs=blk, out_shape=jax.ShapeDtypeStruct((T, d), BF16),
        compiler_params=_params(("parallel", "parallel")),
    )(gl, gl, gl, pa, pb, pc)


def _merge_bwd(dm, gl, pa, pb, pc, *, name, d, tm=384, tn=1024):
    T = pa.shape[0]
    tm, tn = _tile(T, tm, 16), _tile(d, tn)
    nb = d // tn

    def body(dm_ref, g0, g1, g2, a, b, c, da, db, dc, dg0, dg1, dg2):
        dmv = dm_ref[...].astype(F32)
        for g_ref, p_ref, dp_ref, dg_ref in ((g0, a, da, dg0), (g1, b, db, dg1), (g2, c, dc, dg2)):
            sg = jax.nn.sigmoid(g_ref[...].astype(F32))
            dp_ref[...] = (dmv * sg).astype(BF16)
            dg_ref[...] = (dmv * p_ref[...].astype(F32) * sg * (1.0 - sg)).astype(BF16)

    gate = lambda n: pl.BlockSpec((tm, tn), lambda i, j: (i, n * nb + j))
    blk = pl.BlockSpec((tm, tn), lambda i, j: (i, j))
    return pl.pallas_call(
        body, name=name, grid=(T // tm, nb), in_specs=[blk, gate(0), gate(1), gate(2), blk, blk, blk],
        out_specs=(blk,) * 6, out_shape=(jax.ShapeDtypeStruct((T, d), BF16),) * 6,
        compiler_params=_params(("parallel", "parallel")),
    )(dm, gl, gl, gl, pa, pb, pc)


def _loss(y, target, *, name, first, last, tm=384):
    T, d = y.shape
    tm = _tile(T, tm, 16)

    def body(y_ref, t_ref, loss_ref, dy_ref, dyb_ref):
        i = pl.program_id(0)
        row = lax.broadcasted_iota(jnp.int32, (tm, 1), 0) + i * tm
        real = jnp.logical_and(row >= first, row < last)
        err = jnp.where(real, y_ref[...] - t_ref[...], 0.0)
        dy_ref[...] = err * (1.0 / d)
        dyb_ref[...] = (err * (1.0 / d)).astype(BF16)
        part = jnp.broadcast_to(jnp.sum(err * err, keepdims=True).reshape(1, 1), (1, LANES))

        @pl.when(i == 0)
        def _():
            loss_ref[...] = part

        @pl.when(i > 0)
        def _():
            loss_ref[...] += part

    blk = pl.BlockSpec((tm, d), lambda i: (i, 0))
    return pl.pallas_call(
        body, name=name, grid=(T // tm,), in_specs=[blk, blk],
        out_specs=(pl.BlockSpec((1, LANES), lambda i: (0, 0)), blk, blk),
        out_shape=(jax.ShapeDtypeStruct((1, LANES), F32), jax.ShapeDtypeStruct((T, d), F32),
                   jax.ShapeDtypeStruct((T, d), BF16)),
        compiler_params=_params(("arbitrary",)),
    )(y, target)


def _as3d(a):
    return a.reshape(a.shape[0], -1, a.shape[-1])


def _sum_stack(parts, *, name, out_dtype, rows=256):
    n, R, C = parts.shape
    tr = _tile(R, rows, 16)

    def body(p_ref, o_ref):
        acc = p_ref[0].astype(F32)
        for s in range(1, n):
            acc = acc + p_ref[s].astype(F32)
        o_ref[...] = acc.astype(out_dtype)

    return pl.pallas_call(
        body, name=name, grid=(R // tr,),
        in_specs=[pl.BlockSpec((n, tr, C), lambda i: (0, i, 0))],
        out_specs=pl.BlockSpec((tr, C), lambda i: (i, 0)),
        out_shape=jax.ShapeDtypeStruct((R, C), out_dtype),
        compiler_params=_params(("parallel",)),
    )(parts)


def _adamw(w, g, m, v, *, name, rows=128):
    R, C = w.shape
    tr = _tile(R, rows, 8)
    c1 = 1.0 - ADAM_B1 ** ADAM_STEP
    c2 = 1.0 - ADAM_B2 ** ADAM_STEP

    def body(w_ref, g_ref, m_ref, v_ref, d_ref, nm_ref, nv_ref):
        gv = g_ref[...]
        nm = ADAM_B1 * m_ref[...] + (1.0 - ADAM_B1) * gv
        nv = ADAM_B2 * v_ref[...] + (1.0 - ADAM_B2) * (gv * gv)
        nm_ref[...] = nm
        nv_ref[...] = nv
        d_ref[...] = -ADAM_LR * ((nm / c1) / (jnp.sqrt(nv / c2) + ADAM_EPS) + ADAM_WD * w_ref[...])

    blk = pl.BlockSpec((tr, C), lambda i: (i, 0))
    return pl.pallas_call(
        body, name=name, grid=(R // tr,), in_specs=[blk] * 4, out_specs=(blk,) * 3,
        out_shape=(jax.ShapeDtypeStruct((R, C), F32),) * 3,
        compiler_params=_params(("parallel",)),
    )(w, g, m, v)


def _one_hot(index, n):
    return jnp.broadcast_to((jnp.arange(n) == index).astype(F32)[:, None, None], (n, 8, LANES))


def _is_set(flags_ref, s):
    return flags_ref[s, 0:1, 0:1] > 0.5


def _rows_for(h, width, itemsize, n_stacked, budget, mult):
    return _tile(h, max(mult, budget // (n_stacked * width * itemsize)), mult)


def _pair_sum(pieces, recv, core_index, *, name):
    _, H, C = recv.shape
    tr = _rows_for(H, C, 2, 1, 2 << 20, 16)
    nh = H // tr
    halves_lead = pieces.ndim == 4

    def body(c_ref, mine_ref, r_ref, o_ref):
        mine = mine_ref[0, 0] if halves_lead else mine_ref[0]
        o_ref[0] = (mine.astype(F32) + r_ref[0].astype(F32)).astype(BF16)

    blk = pl.BlockSpec((1, tr, C), lambda j, i, c: (j, i, 0))
    if halves_lead:
        mine_spec = pl.BlockSpec((1, 1, tr, C), lambda j, i, c: (c[0], j, i, 0))
    else:
        mine_spec = pl.BlockSpec((1, tr, C), lambda j, i, c: (j, c[0] * nh + i, 0))
    return pl.pallas_call(
        body, name=name,
        grid_spec=pltpu.PrefetchScalarGridSpec(num_scalar_prefetch=1, grid=(4, nh), in_specs=[mine_spec, blk],
                                               out_specs=blk),
        out_shape=jax.ShapeDtypeStruct((4, H, C), BF16),
        compiler_params=_params(("parallel", "parallel")),
    )(core_index, pieces, recv)


def _chip_sum(pair, landed, chip_flags, chip_index, *, name):
    _, H, C = pair.shape
    tr = _rows_for(H, C, 2, 4, 8 << 20, 16)

    def body(me_ref, own_ref, l_ref, chip_ref, o_ref):
        acc = None
        for s in range(4):
            part = jnp.where(_is_set(chip_ref, s), own_ref[0], l_ref[s]).astype(F32)
            acc = part if acc is None else acc + part
        o_ref[...] = acc

    return pl.pallas_call(
        body, name=name,
        grid_spec=pltpu.PrefetchScalarGridSpec(
            num_scalar_prefetch=1, grid=(H // tr,),
            in_specs=[pl.BlockSpec((1, tr, C), lambda i, me: (me[0], i, 0)),
                      pl.BlockSpec((4, tr, C), lambda i, me: (0, i, 0)),
                      pl.BlockSpec((4, 8, LANES), lambda i, me: (0, 0, 0))],
            out_specs=pl.BlockSpec((tr, C), lambda i, me: (i, 0))),
        out_shape=jax.ShapeDtypeStruct((H, C), F32),
        compiler_params=_params(("parallel",)),
    )(chip_index, pair, landed, chip_flags)


def _adamw_layer(w, m, v, total, recv, core, layer, prev, *, name, col_halves=False, after=()):
    _, R, C = w.shape
    H, wd = total.shape
    tr = _rows_for(H, wd, 4, 1, 2 << 20, 8)
    nh = H // tr
    c1 = 1.0 - ADAM_B1 ** ADAM_STEP
    c2 = 1.0 - ADAM_B2 ** ADAM_STEP
    n_prev = 0 if prev is None else 4
    after = tuple(after)

    def body(*refs):
        w_ref, m_ref, v_ref, t_ref, r_ref, core_ref = refs[:6]
        g_ref, d_ref, nm_ref, nv_ref = refs[6 + n_prev + len(after):]
        half_is_mine = jnp.where(pl.program_id(0) == 0, core_ref[0, 0:1, 0:1], core_ref[1, 0:1, 0:1]) > 0.5
        gv = jnp.where(half_is_mine, t_ref[...], r_ref[...])
        nm = ADAM_B1 * m_ref[0] + (1.0 - ADAM_B1) * gv
        nv = ADAM_B2 * v_ref[0] + (1.0 - ADAM_B2) * (gv * gv)
        g_ref[0] = gv
        nm_ref[0] = nm
        nv_ref[0] = nv
        d_ref[0] = -ADAM_LR * ((nm / c1) / (jnp.sqrt(nv / c2) + ADAM_EPS) + ADAM_WD * w_ref[0])

    if col_halves:
        lay = pl.BlockSpec((1, tr, wd), lambda hf, i: (layer, i, hf))
    else:
        lay = pl.BlockSpec((1, tr, wd), lambda hf, i: (layer, hf * nh + i, 0))
    one = pl.BlockSpec((tr, wd), lambda hf, i: (i, 0))
    operands = [w, m, v, total, recv, core] + ([] if prev is None else list(prev)) + list(after)
    return pl.pallas_call(
        body, name=name, grid=(2, nh),
        in_specs=[lay, lay, lay, one, one, pl.BlockSpec((2, 8, LANES), lambda hf, i: (0, 0, 0))]
        + [ANY] * (n_prev + len(after)),
        out_specs=(lay,) * 4, out_shape=(jax.ShapeDtypeStruct((2, R, C), F32),) * 4,
        input_output_aliases={6 + i: i for i in range(n_prev)},
        compiler_params=_params(("parallel", "parallel")),
    )(*operands)


ANY = pl.BlockSpec(memory_space=pl.ANY)


def _coords():
    return lax.axis_index("x"), lax.axis_index("y"), lax.axis_index("c")


HBM = pl.BlockSpec(memory_space=pltpu.HBM)
SEM = pl.BlockSpec(memory_space=pltpu.SEMAPHORE)
EFFECT = pltpu.SideEffectType.DATAFLOW_SIDE_EFFECTING


def _copies(plan, bufs, send_sems, recv_sems):
    return [pltpu.make_async_remote_copy(src_ref=s, dst_ref=d, send_sem=send_sems.at[i], recv_sem=recv_sems.at[i],
                                         device_id=to, device_id_type=MESH)
            for i, (s, d, to) in enumerate(plan(bufs))]


def _start_copies(bufs, groups, *, name):
    nb, ng = len(bufs), len(groups)

    def body(*refs):
        buf_refs = refs[:nb]
        sems = refs[nb:nb + 2 * ng]
        token = refs[-1]
        for g, (plan, _) in enumerate(groups):
            for cp in _copies(plan, buf_refs, sems[2 * g], sems[2 * g + 1]):
                cp.start()
        token[...] = jnp.zeros_like(token)

    sem_shapes = []
    for _, n in groups:
        sem_shapes += [pltpu.SemaphoreType.DMA((n,)), pltpu.SemaphoreType.DMA((n,))]
    out = pl.pallas_call(
        body, name=name, in_specs=[HBM] * nb,
        out_specs=tuple([SEM] * (2 * ng) + [HBM] * nb + [pl.BlockSpec(memory_space=pltpu.VMEM)]),
        out_shape=tuple(sem_shapes + [pltpu.HBM(b.shape, b.dtype) for b in bufs] + [jax.ShapeDtypeStruct((8, LANES), F32)]),
        input_output_aliases={i: 2 * ng + i for i in range(nb)},
        compiler_params=pltpu.CompilerParams(has_side_effects=EFFECT),
    )(*[pltpu.with_memory_space_constraint(b, pltpu.HBM) for b in bufs])
    sems = [(out[2 * g], out[2 * g + 1]) for g in range(ng)]
    return sems, list(out[2 * ng:2 * ng + nb]), out[-1]


def _wait_copies(bufs, sems, plan, after, *, name):
    nb = len(bufs)

    def body(*refs):
        buf_refs = refs[:nb]
        for cp in _copies(plan, buf_refs, refs[nb], refs[nb + 1]):
            cp.wait_send()
            cp.wait_recv()

    out = pl.pallas_call(
        body, name=name, in_specs=[HBM] * nb + [SEM, SEM, ANY], out_specs=tuple([HBM] * nb),
        out_shape=tuple(pltpu.HBM(b.shape, b.dtype) for b in bufs),
        input_output_aliases={i: i for i in range(nb)},
        compiler_params=pltpu.CompilerParams(has_side_effects=EFFECT),
    )(*bufs, sems[0], sems[1], after)
    return list(out)


def _half(ref, c):
    h = ref.shape[0] // 2
    return ref.at[pl.ds(c * h, h)]


def _ici_gather_plan(pairs):
    def plan(refs):
        x, y, c = _coords()
        me = 2 * x + y
        out = []
        for s, d in pairs:
            for cx, cy in [(1 - x, y), (x, 1 - y), (1 - x, 1 - y)]:
                out.append((_half(refs[s], c), _half(refs[d].at[me], c), (cx, cy, c)))
            out.append((refs[s], refs[d].at[me], (x, y, 1 - c)))
        return out
    return plan, 4 * len(pairs)


def _d2d_forward_plan(lands):
    def plan(refs):
        x, y, c = _coords()
        out = []
        for d in lands:
            for cx, cy in [(1 - x, y), (x, 1 - y), (1 - x, 1 - y)]:
                got = _half(refs[d].at[2 * cx + cy], c)
                out.append((got, got, (x, y, 1 - c)))
        return out
    return plan, 3 * len(lands)


def _swap_half_plan(pairs):
    def plan(refs):
        x, y, c = _coords()
        out = []
        for s, d in pairs:
            h = refs[d].shape[1]
            other = refs[s].at[1 - c] if len(refs[s].shape) == 4 else refs[s].at[:, pl.ds((1 - c) * h, h)]
            out.append((other, refs[d], (x, y, 1 - c)))
        return out
    return plan, len(pairs)


def _scatter_plan(pairs):
    def plan(refs):
        x, y, c = _coords()
        me = 2 * x + y
        out = []
        for s, d in pairs:
            for cx, cy in [(1 - x, y), (x, 1 - y), (1 - x, 1 - y)]:
                out.append((refs[s].at[2 * cx + cy], refs[d].at[me], (cx, cy, c)))
        return out
    return plan, 3 * len(pairs)


def _swap_total_plan(pairs):
    def plan(refs):
        x, y, c = _coords()
        return [(refs[s], refs[d], (x, y, 1 - c)) for s, d in pairs]
    return plan, len(pairs)


def _gather_all(block, *, name, after=()):
    after = tuple(after)

    def body(src, *rest):
        out, send_sems, recv_sems, local_sem = rest[len(after):]
        x, y, c = _coords()
        me = 4 * x + 2 * y + c
        flips = [(fx, fy, fc) for fx in (0, 1) for fy in (0, 1) for fc in (0, 1)][1:]
        mine = pltpu.make_async_copy(src, out.at[me], local_sem)
        mine.start()
        peers = [(x ^ fx, y ^ fy, c ^ fc) for fx, fy, fc in flips]
        cps = [pltpu.make_async_remote_copy(src_ref=src, dst_ref=out.at[me], send_sem=send_sems.at[k],
                                            recv_sem=recv_sems.at[k], device_id=peer, device_id_type=MESH)
               for k, peer in enumerate(peers)]
        for cp in cps:
            cp.start()
        for k, (px, py, pc) in enumerate(peers):
            slot = out.at[4 * px + 2 * py + pc]
            pltpu.make_async_remote_copy(src_ref=slot, dst_ref=slot, send_sem=send_sems.at[k], recv_sem=recv_sems.at[k],
                                         device_id=(px, py, pc), device_id_type=MESH).wait_recv()
        for cp in cps:
            cp.wait_send()
        mine.wait()

    return pl.pallas_call(
        body, name=name, in_specs=[ANY] * (1 + len(after)), out_specs=ANY,
        out_shape=jax.ShapeDtypeStruct((8,) + block.shape, block.dtype),
        scratch_shapes=[pltpu.SemaphoreType.DMA((7,)), pltpu.SemaphoreType.DMA((7,)), pltpu.SemaphoreType.DMA],
    )(block, *after)


def _cols(o):
    return jnp.transpose(o, (1, 0, 2)).reshape(o.shape[1], -1)


def _uncols(full):
    return jnp.transpose(full.reshape(full.shape[0], 4, -1), (1, 0, 2))


def _rope_pad(x1, x2):
    z = jnp.zeros_like(x1)
    return jnp.concatenate([x1, z, x2, z], axis=-1)


def _head_pad(w, heads):
    r = w.reshape(w.shape[0], heads, QK_HEAD)
    half = QK_ROPE // 2
    out = jnp.concatenate([r[..., :QK_NOPE], _rope_pad(r[..., QK_NOPE:QK_NOPE + half], r[..., QK_NOPE + half:])], axis=-1)
    return out.reshape(w.shape[0], heads * HEAD_PAD)


def _head_unpad(w, heads):
    r = w.reshape(w.shape[0], heads, HEAD_PAD)
    half = QK_ROPE // 2
    out = jnp.concatenate([r[..., :QK_NOPE], r[..., QK_NOPE:QK_NOPE + half],
                           r[..., QK_NOPE + 2 * half:QK_NOPE + 3 * half]], axis=-1)
    return out.reshape(w.shape[0], heads * QK_HEAD)


class _Dims:
    def __init__(self, d, seq):
        self.d = d
        self.seq = seq
        self.t_real = N_META + seq
        self.t = -(-self.t_real // LANES) * LANES
        self.dc = d // 2
        self.dp = d // 2
        self.pg = self.dp // len(POOL_WINDOWS)
        self.heads = d // 128
        self.dff = 4 * d
        self.a_end = 3 * self.dc
        self.q_end = self.a_end + Q_LORA
        self.kv_end = self.q_end + KV_LORA
        self.kr_end = self.kv_end + QK_ROPE
        self.pool_end = self.kr_end + self.dp
        self.d_in = self.pool_end + 3 * d
        self.r_pool = 3 * self.dc
        self.r_q = self.r_pool + self.dp
        self.r_kv = self.r_q + Q_LORA
        self.r_kr = self.r_kv + KV_LORA
        self.r_width = self.r_kr + HEAD_PAD


def _split_cols(a):
    return jnp.moveaxis(a.reshape(a.shape[:-1] + (2, a.shape[-1] // 2)), -2, -3)


def _join_cols(a):
    a = jnp.moveaxis(a, -3, -2)
    return a.reshape(a.shape[:-2] + (a.shape[-2] * a.shape[-1],))


def _in_weights(dm, pieces):
    w_t = _join_cols(pieces).reshape(dm.d_in, dm.d)
    half = QK_ROPE // 2
    kr = w_t[dm.kv_end:dm.kr_end]
    zeros = jnp.zeros((half, dm.d), BF16)
    kr_p = jnp.concatenate([kr[:half], zeros, kr[half:], zeros, jnp.zeros((HEAD_PAD - LANES, dm.d), BF16)], axis=0)
    return dict(
        wg_t=w_t[dm.pool_end:],
        wr_t=jnp.concatenate([w_t[:dm.a_end], w_t[dm.kr_end:dm.pool_end], w_t[dm.a_end:dm.kv_end], kr_p], axis=0))


def _other_weights(dm, g):
    out = {}
    if "w_ukv" in g:
        w_ukv = _cols(g["w_ukv"]).reshape(KV_LORA, dm.heads, QK_NOPE + V_HEAD)
        out["wkn"] = w_ukv[:, :, :QK_NOPE].reshape(KV_LORA, dm.heads * QK_NOPE)
        out["wv"] = w_ukv[:, :, QK_NOPE:].reshape(KV_LORA, dm.heads * V_HEAD)
    if "w_uq" in g:
        out["wuq"] = _head_pad(_cols(g["w_uq"]), dm.heads)
    if "pool_w" in g:
        out["wp"] = jnp.transpose(g["pool_w"], (1, 0, 2, 3)).reshape(len(POOL_WINDOWS), dm.pg, dm.pg)
    for name, key in (("w_branch_a", "wba"), ("w_branch_c", "wbc"), ("w_up", "wup")):
        if name in g:
            out[key] = _cols(g[name])
    for name, key in (("w_branch_b", "wbb"), ("w_o", "wo"), ("w_down", "wdn")):
        if name in g:
            out[key] = g[name].reshape(-1, dm.d)
    return out


def _small_weights(small):
    return dict(
        conv_w=small["conv_w"],
        attn_norm=small["attn_norm"][None], mlp_norm=small["mlp_norm"][None],
        q_lat_norm=small["q_lat_norm"][None], kv_lat_norm=small["kv_lat_norm"][None],
        q_norm=_head_pad(small["q_norm"][None], 1), k_norm=_head_pad(small["k_norm"][None], 1),
        pool_scale=small["pool_scale"][None],
    )


def _grad_piece(dm, dw, name):
    half = QK_ROPE // 2
    rows = lambda a: a.reshape((4, a.shape[0] // 4) + a.shape[1:])
    if name == "w_in":
        dwr, dwg = dw["wr_t"], dw["wg_t"]
        d_t = jnp.concatenate([
            dwr[:, :dm.r_pool], dwr[:, dm.r_q:dm.r_kr], dwr[:, dm.r_kr:dm.r_kr + half],
            dwr[:, dm.r_kr + 2 * half:dm.r_kr + 3 * half], dwr[:, dm.r_pool:dm.r_q], dwg], axis=1)
        out = d_t.reshape(2, 4, d_t.shape[1] // 4, d_t.shape[2])
    elif name == "w_ukv":
        out = _uncols(jnp.concatenate([dw["wkn"].reshape(KV_LORA, dm.heads, QK_NOPE),
                                       dw["wv"].reshape(KV_LORA, dm.heads, V_HEAD)], axis=-1).reshape(KV_LORA, -1))
    elif name == "w_uq":
        out = _uncols(_head_unpad(dw["wuq"], dm.heads))
    elif name == "pool_w":
        out = jnp.transpose(dw["wp"].reshape(len(POOL_WINDOWS), 4, dm.pg // 4, dm.pg), (1, 0, 2, 3))
    elif name in ("w_branch_a", "w_branch_c", "w_up"):
        out = dw[{"w_branch_a": "wba", "w_branch_c": "wbc", "w_up": "wup"}[name]]
    else:
        out = rows(dw[{"w_branch_b": "wbb", "w_o": "wo", "w_down": "wdn"}[name]])
    return out.astype(BF16)


def _layer_fwd(dm, W, x, cos_t, sin_t, tag, more=None, h=None):
    n = lambda s: f"{s}_{tag}"
    if h is None:
        h = _rms_fwd(x, W["attn_norm"], name=n("attn_norm"))
    gl = _mm(h, W["wg_t"], name=n("proj_gates"), tb=True, out_dtype=BF16)
    rest = _mm(h, W["wr_t"], name=n("proj_rest"), tb=True)
    if more is not None:
        W.update(more("after_proj", rest))
    y_a = _conv_fwd(rest, W["conv_w"], name=n("conv"), dc=dm.dc)
    y_c = _pool_fwd(rest, W["wp"], W["pool_scale"], name=n("pool"), seg0=dm.r_pool // dm.pg, pg=dm.pg)
    q_lat = _rms_fwd(rest, W["q_lat_norm"], name=n("q_lat_norm"), width=Q_LORA, seg=dm.r_q // Q_LORA)
    kv_lat = _rms_fwd(rest, W["kv_lat_norm"], name=n("kv_lat_norm"), width=KV_LORA, seg=dm.r_kv // KV_LORA)
    q_raw = _mm(q_lat, W["wuq"], name=n("up_q"), out_dtype=BF16)
    k_nope = _mm(kv_lat, W["wkn"], name=n("up_k"), out_dtype=BF16)
    v = _mm(kv_lat, W["wv"], name=n("up_v"), out_dtype=BF16)
    q, k = _qk_fwd(q_raw, k_nope, rest, cos_t, sin_t, W["q_norm"], W["k_norm"], name=n("qk_norm_rope"),
                   heads=dm.heads, kr_seg=dm.r_kr // HEAD_PAD)
    y_b, lse = _flash_fwd(q, k, v, name=n("attention"), heads=dm.heads)
    if more is not None:
        W.update(more("after_attention", y_b))
    pa = _mm(y_a, W["wba"], name=n("branch_a"), out_dtype=BF16)
    pb = _mm(y_b, W["wbb"], name=n("branch_b"), out_dtype=BF16, after=W.pop("pin", ()))
    pc = _mm(y_c, W["wbc"], name=n("branch_c"), out_dtype=BF16)
    merged = _merge_fwd(gl, pa, pb, pc, name=n("merge"), d=dm.d)
    x1 = _mm(merged, W["wo"], name=n("out_proj"), add=x)
    h2 = _rms_fwd(x1, W["mlp_norm"], name=n("mlp_norm"))
    if more is not None:
        W.update(more("before_mlp", h2))
    up, act = _mm(h2, W["wup"], name=n("mlp_up"), epi="relu2")
    x2 = _mm(act, W["wdn"], name=n("mlp_down"), add=x1, tm=704, tk=4096)
    saved = dict(x=x, h=h, gl=gl, rest=rest, y_a=y_a, y_c=y_c, q_lat=q_lat, kv_lat=kv_lat, q_raw=q_raw, k_nope=k_nope,
                 v=v, q=q, k=k, y_b=y_b, lse=lse, pa=pa, pb=pb, pc=pc, merged=merged, x1=x1, h2=h2, up=up, act=act)
    return x2, saved


def _layer_bwd(dm, W, S, dx2, dx2_b, cos_t, sin_t, tag, hook=None):
    n = lambda s: f"{s}_{tag}"
    dw, ds = {}, {}
    if hook is None:
        hook = lambda point, t, dw_so_far: ()
    dup = _mm(dx2_b, W["wdn"], name=n("d_mlp_down"), tb=True, aux=S["up"], epi="drelu2", out_dtype=BF16,
              after=hook("start", dx2, dw))
    dw["wdn"] = _mm(S["act"], dx2_b, name=n("dw_mlp_down"), ta=True, tm=512, out_dtype=BF16)
    dh2 = _mm(dup, W["wup"], name=n("d_mlp_up"), tb=True, tm=704, tk=4096)
    dw["wup"] = _mm(S["h2"], dup, name=n("dw_mlp_up"), ta=True, tm=512, out_dtype=BF16, pieces=4)
    dx1, dx1_b, ds["mlp_norm"] = _rms_bwd(dh2, S["x1"], W["mlp_norm"], name=n("d_mlp_norm"), res=dx2, bf16_copy=True)
    dmerged = _mm(dx1_b, W["wo"], name=n("d_out_proj"), tb=True, after=hook("after_mlp", dx1, dw))
    dw["wo"] = _mm(S["merged"], dx1_b, name=n("dw_out_proj"), ta=True, tm=512, out_dtype=BF16)
    dpa, dpb, dpc, dg0, dg1, dg2 = _merge_bwd(dmerged, S["gl"], S["pa"], S["pb"], S["pc"], name=n("d_merge"), d=dm.d)
    dgl = jnp.concatenate([dg0, dg1, dg2], axis=1)
    dy_a = _mm(dpa, W["wba"], name=n("d_branch_a"), tb=True)
    dw["wba"] = _mm(S["y_a"], dpa, name=n("dw_branch_a"), ta=True, tm=512, out_dtype=BF16, pieces=4)
    dy_b = _mm(dpb, W["wbb"], name=n("d_branch_b"), tb=True, out_dtype=BF16)
    dw["wbb"] = _mm(S["y_b"], dpb, name=n("dw_branch_b"), ta=True, tm=512, out_dtype=BF16)
    dy_c = _mm(dpc, W["wbc"], name=n("d_branch_c"), tb=True)
    dw["wbc"] = _mm(S["y_c"], dpc, name=n("dw_branch_c"), ta=True, tm=512, out_dtype=BF16, pieces=4)
    dq, dk, dv = _flash_bwd(S["q"], S["k"], S["v"], S["y_b"], dy_b, S["lse"], name=n("d_attention"), heads=dm.heads,
                            after=hook("before_attention", dw["wbc"], dw))
    after_attention = hook("after_attention", dq, dw)
    dq_raw, dk_nope, dk_rope, dgq, dgk = _qk_bwd(
        dq, dk, S["q_raw"], S["k_nope"], S["rest"], cos_t, sin_t, W["q_norm"], W["k_norm"], name=n("d_qk_norm_rope"),
        heads=dm.heads, kr_seg=dm.r_kr // HEAD_PAD, after=after_attention)
    ds["q_norm"] = _head_unpad(dgq, 1)
    ds["k_norm"] = _head_unpad(dgk, 1)
    dkv_v = _mm(dv, W["wv"], name=n("d_up_v"), tb=True)
    dq_lat_n = _mm(dq_raw, W["wuq"], name=n("d_up_q"), tb=True, after=hook("after_qk", dq_raw, dw))
    dw["wuq"] = _mm(S["q_lat"], dq_raw, name=n("dw_up_q"), ta=True, tm=512)
    dkv_lat_n = _mm(dk_nope, W["wkn"], name=n("d_up_k"), tb=True, add=dkv_v)
    dw["wkn"] = _mm(S["kv_lat"], dk_nope, name=n("dw_up_k"), ta=True, tm=512)
    dw["wv"] = _mm(S["kv_lat"], dv, name=n("dw_up_v"), ta=True, tm=512)
    dq_lat, ds["q_lat_norm"] = _rms_bwd(dq_lat_n, S["rest"], W["q_lat_norm"], name=n("d_q_lat_norm"), width=Q_LORA,
                                        seg=dm.r_q // Q_LORA, out_dtype=BF16)
    dkv_lat, ds["kv_lat_norm"] = _rms_bwd(dkv_lat_n, S["rest"], W["kv_lat_norm"], name=n("d_kv_lat_norm"), width=KV_LORA,
                                          seg=dm.r_kv // KV_LORA, out_dtype=BF16)
    du, db, dc, ds["conv_w"] = _conv_bwd(S["rest"], W["conv_w"], dy_a, name=n("d_conv"), dc=dm.dc)
    dpool, dw["wp"], ds["pool_scale"] = _pool_bwd(S["rest"], W["wp"], W["pool_scale"], dy_c, name=n("d_pool"),
                                                  seg0=dm.r_pool // dm.pg, pg=dm.pg)
    drest = jnp.concatenate([du, db, dc, dpool, dq_lat, dkv_lat, dk_rope], axis=1)
    dw["wg_t"] = _mm(dgl, S["h"], name=n("dw_proj_gates"), ta=True, tm=512, out_dtype=BF16, pieces=2)
    dw["wr_t"] = _mm(drest, S["h"], name=n("dw_proj_rest"), ta=True, tm=512, out_dtype=BF16, pieces=2)
    dh_g = _mm(dgl, W["wg_t"], name=n("d_proj_gates"), tm=704, tk=3072, after=hook("after_dw_in", dw["wr_t"], dw))
    dh = _mm(drest, W["wr_t"], name=n("d_proj_rest"), add=dh_g, tm=704, tk=2688, after=hook("after_dh_gates", dh_g, dw))
    dx, dx_b, ds["attn_norm"] = _rms_bwd(dh, S["x"], W["attn_norm"], name=n("d_attn_norm"), res=dx1, bf16_copy=True)
    return dx, dx_b, dw, ds


BIG = ("w_in", "w_uq", "w_ukv", "pool_w", "w_branch_a", "w_branch_b", "w_branch_c", "w_o", "w_up", "w_down")
REPLICATED = ("attn_norm", "q_lat_norm", "kv_lat_norm", "q_norm", "k_norm", "pool_scale", "mlp_norm")
WEIGHTS = ("meta_tokens", "attn_norm", "w_in", "conv_w", "q_lat_norm", "kv_lat_norm", "w_uq", "w_ukv", "q_norm",
           "k_norm", "pool_w", "pool_scale", "w_branch_a", "w_branch_b", "w_branch_c", "w_o", "mlp_norm", "w_up",
           "w_down")


def _pack(arrays):
    flat = jnp.concatenate([a.reshape(-1).astype(F32) for a in arrays])
    pad = (-flat.shape[0]) % (8 * LANES)
    return jnp.pad(flat, (0, pad)).reshape(-1, LANES)


def _unpack(flat, shapes):
    out, pos = [], 0
    flat = flat.reshape(-1)
    for shp in shapes:
        size = math.prod(shp)
        out.append(flat[pos:pos + size].reshape(shp))
        pos += size
    return out


def _update(w, g, m, v, name):
    shp = w.shape
    to2 = lambda a: a.reshape(-1, shp[-1])
    delta, nm, nv = _adamw(to2(w), to2(g), to2(m), to2(v), name=name)
    return delta.reshape(shp), nm.reshape(shp), nv.reshape(shp)


def _step(args):
    x = args["x"][0]
    seq, d = x.shape
    dm = _Dims(d, seq)
    xi, yi, ci = _coords()
    chip = 2 * xi + yi

    small_w = _gather_all(_pack([args["conv_w"], args["meta_tokens"]]), name="gather_small_weights")
    args = dict(args)
    for p in ("", "m_", "v_"):
        args[p + "w_in"] = jnp.swapaxes(args[p + "w_in"], 1, 2)
    order = [(k, l) for l in range(2) for k in BIG]
    last = ("w_up", "w_down")
    group_names = [[("w_in", 0)], [(k, 0) for k in BIG[1:] if k not in last], [(k, 0) for k in last],
                   [(k, 1) for k in BIG]]
    first, others = order[0], order[1:]
    shards = {first: _split_cols(args["w_in"][0].astype(BF16))}
    lands = {first: lax.empty((4,) + shards[first].shape, BF16)}
    sems, thru, token = _start_copies([shards[first], lands[first], small_w], [_ici_gather_plan([(0, 1)])],
                                      name="start_gather_ici_first")
    shards[first], lands[first], small_w = thru
    zero = token[0, 0]
    for n in others:
        shards[n] = (args[n[0]][n[1]] + zero).astype(BF16)
        if n[0] == "w_in":
            shards[n] = _split_cols(shards[n])
        lands[n] = lax.empty((4,) + shards[n].shape, BF16)
    at = {n: i for i, n in enumerate(others)}
    sems_b, thru, token_b = _start_copies(
        [shards[n] for n in others] + [lands[n] for n in others] + [token],
        [_ici_gather_plan([(at[n], len(others) + at[n]) for n in g]) for g in group_names[1:]], name="start_gather_ici")
    sems = sems + sems_b
    for i, n in enumerate(others):
        shards[n], lands[n] = thru[i], thru[len(others) + i]

    def finish_gather(g, after, tag):
        names = group_names[g]
        k = len(names)
        plan, _ = _ici_gather_plan([(i, k + i) for i in range(k)])
        got = _wait_copies([shards[n] for n in names] + [lands[n] for n in names], sems[g], plan, after,
                           name=f"wait_gather_ici_{tag}")
        for i, n in enumerate(names):
            shards[n] = got[i]
        fwd = _d2d_forward_plan(list(range(k)))
        sems2, bufs2, tok2 = _start_copies(got[k:], [fwd], name=f"start_gather_d2d_{tag}")
        return names, bufs2, sems2[0], fwd[0], tok2

    def land_gather(pending, after, tag):
        names, bufs2, sems2, plan, tok2 = pending
        done = _wait_copies(bufs2, sems2, plan, tok2 if after is None else after, name=f"wait_gather_d2d_{tag}")
        return {n[0]: buf for n, buf in zip(names, done)}

    conv_shape, meta_shape = args["conv_w"].shape, args["meta_tokens"].shape
    per_chip = [_unpack(small_w[2 * j], [conv_shape, meta_shape]) for j in range(4)]
    conv_full = jnp.concatenate([p[0] for p in per_chip], axis=-1)
    meta_full = jnp.concatenate([p[1] for p in per_chip], axis=-1)

    layers = []
    for l in range(2):
        small = {k: args[k][l] for k in REPLICATED}
        small["conv_w"] = conv_full[l]
        layers.append(_small_weights(small))

    pos = jnp.arange(dm.t, dtype=F32)
    inv = ROPE_THETA ** (-jnp.arange(0, QK_ROPE, 2, dtype=F32) / QK_ROPE)
    ang = pos[:, None] * inv[None, :]
    cos_t = _rope_pad(jnp.cos(ang), jnp.cos(ang))
    sin_t = _rope_pad(-jnp.sin(ang), jnp.sin(ang))
    tail = jnp.zeros((dm.t - dm.t_real, d), F32) + zero
    h0 = jnp.concatenate([meta_full, x, tail], axis=0)
    target = jnp.concatenate([jnp.zeros((N_META, d), F32), args["loss_target"][0], tail], axis=0)

    h_first = _rms_fwd(h0, layers[0]["attn_norm"], name="attn_norm_l0", after=(token, token_b))
    layers[0].update(_in_weights(dm, land_gather(finish_gather(0, h_first, "l0_in"), None, "l0_in")["w_in"]))
    pending = {}

    def rest_of_layer0(point, after):
        if point == "after_proj":
            return _other_weights(dm, land_gather(finish_gather(1, after, "l0_mid"), None, "l0_mid"))
        if point == "after_attention":
            pending["mlp"] = finish_gather(2, after, "l0_mlp")
            return {"pin": (pending["mlp"][4],)}
        return _other_weights(dm, land_gather(pending["mlp"], after, "l0_mlp"))

    h1, saved0 = _layer_fwd(dm, layers[0], h0, cos_t, sin_t, "l0", more=rest_of_layer0, h=h_first)
    g1 = land_gather(finish_gather(3, saved0["y_b"], "l1"), h1, "l1")
    layers[1].update(_in_weights(dm, g1["w_in"]))
    layers[1].update(_other_weights(dm, g1))
    h2, saved1 = _layer_fwd(dm, layers[1], h1, cos_t, sin_t, "l1")
    sq, dy, dy_b = _loss(h2, target, name="loss_head", first=N_META, last=dm.t_real)
    loss = lax.psum(0.5 / d * sq[0, 0], ("x", "y", "c"))
    core, chip_flags = _one_hot(ci, 2), _one_hot(chip, 4)
    core_index = jnp.reshape(ci, (1,)).astype(jnp.int32)
    chip_index = jnp.reshape(chip, (1,)).astype(jnp.int32)

    class Reduce:
        def __init__(self, names, dw, tag):
            self.names, self.tag, self.nb = names, tag, len(names)
            self.idx = [(i, self.nb + i) for i in range(self.nb)]
            parts = [_grad_piece(dm, dw, k) for k in names]
            parts = [p if k == "w_in" else _as3d(p) for p, k in zip(parts, names)]
            recv = [lax.empty((4,) + p.shape[2:] if k == "w_in" else (4, p.shape[1] // 2, p.shape[2]), BF16)
                    for p, k in zip(parts, names)]
            self.plan = _swap_half_plan(self.idx)
            self.sems, self.bufs, self.token = _start_copies(parts + recv, [self.plan], name=f"start_swap_{tag}")

        def _land(self, after, what):
            return _wait_copies(self.bufs, self.sems[0], self.plan[0], self.token if after is None else after,
                                name=f"wait_{what}_{self.tag}")

        def scatter(self, after=None):
            got = self._land(after, "swap")
            pairs = [_pair_sum(got[i], got[j], core_index, name=f"pair_sum_{k}_{self.tag}")
                     for (i, j), k in zip(self.idx, self.names)]
            self.plan = _scatter_plan(self.idx)
            self.sems, self.bufs, self.token = _start_copies(pairs + [lax.empty(p.shape, BF16) for p in pairs],
                                                             [self.plan], name=f"start_scatter_{self.tag}")
            return self.token

        def totals(self, after=None):
            got = self._land(after, "scatter")
            sums = [_chip_sum(got[i], got[j], chip_flags, chip_index, name=f"chip_sum_{k}_{self.tag}")
                    for (i, j), k in zip(self.idx, self.names)]
            self.plan = _swap_total_plan(self.idx)
            self.sems, self.bufs, self.token = _start_copies(sums + [lax.empty(t.shape, F32) for t in sums],
                                                             [self.plan], name=f"start_swap_total_{self.tag}")
            return self.token

        def finish(self, after=None):
            got = self._land(after, "swap_total")
            return {k: (got[i], got[j]) for (i, j), k in zip(self.idx, self.names)}

    dh1, dh1_b, dw1, ds1 = _layer_bwd(dm, layers[1], saved1, dy, dy_b, cos_t, sin_t, "l1",
                               hook=lambda point, t, dw: (loss.reshape(1, 1),) if point == "start" else ())
    early = ("w_down", "w_up", "w_o", "w_branch_a", "w_branch_b", "w_branch_c")
    late = tuple(k for k in BIG if k not in early)
    stage = {}

    def during_layer0(point, t, dw):
        if point == "start":
            stage["l1"] = Reduce(BIG, dw1, "l1")
            return (stage["l1"].token,)
        if point == "after_mlp":
            return (stage["l1"].scatter(after=t),)
        if point == "before_attention":
            stage["l0a"] = Reduce(early, dw, "l0a")
            return (stage["l0a"].token,)
        if point == "after_attention":
            return (stage["l1"].totals(after=t), stage["l0a"].scatter(after=t))
        if point == "after_qk":
            stage["red1"] = stage["l1"].finish(after=t)
            return ()
        if point == "after_dw_in":
            tok = stage["l0a"].totals(after=t)
            stage["l0b"] = Reduce(late, dw, "l0b")
            return (tok, stage["l0b"].token)
        return (stage["l0b"].scatter(after=t),)

    dh0, _, dw0, ds0 = _layer_bwd(dm, layers[0], saved0, dh1, dh1_b, cos_t, sin_t, "l0", hook=during_layer0)
    grad_x = dh0[N_META:dm.t_real][None]
    red1 = stage["red1"]
    grads, delta, new_m, new_v = {}, {}, {}, {}

    def adamw_big(k, layer, red, prev, after):
        shp = args[k].shape
        wmv = [args[p + k].reshape(2, -1, shp[-1]) for p in ("", "m_", "v_")]
        return _adamw_layer(*wmv, *red[k], core, layer, prev, name=f"adamw_{k}_l{layer}", col_halves=k == "w_in",
                            after=after)

    def keep(k, out):
        shp = args[k].shape
        out = [o.reshape(shp) for o in out]
        grads[k], delta[k], new_m[k], new_v[k] = [jnp.swapaxes(o, 1, 2) for o in out] if k == "w_in" else out

    half_done = {}
    pin = dh0
    for k in BIG:
        half_done[k] = adamw_big(k, 1, red1, None, (pin,))
        pin = half_done[k][0]
    red0a = stage["l0a"].finish(after=pin)
    for k in early:
        out = adamw_big(k, 0, red0a, half_done[k], ())
        keep(k, out)
        pin = out[0]

    small_names = REPLICATED + ("conv_w",)
    small_parts = [jnp.stack([ds0[k].reshape(ds0[k].shape[-2:] if k == "conv_w" else (-1,)),
                              ds1[k].reshape(ds1[k].shape[-2:] if k == "conv_w" else (-1,))]) for k in small_names]
    small_parts.append(dh0[:N_META])
    small_all = _gather_all(_pack(small_parts), name="gather_small_grads", after=(pin,))
    small_sum = _sum_stack(small_all, name="sum_small_grads", out_dtype=F32)
    small_g = dict(zip(small_names + ("meta_tokens",), _unpack(small_sum, [p.shape for p in small_parts])))
    for k in REPLICATED:
        grads[k] = small_g[k]
    dcw = conv_shape[-1]
    grads["conv_w"] = lax.dynamic_slice_in_dim(small_g["conv_w"], chip * dcw, dcw, axis=2)
    dmeta = meta_shape[-1]
    grads["meta_tokens"] = lax.dynamic_slice_in_dim(small_g["meta_tokens"], chip * dmeta, dmeta, axis=1)

    stage["l0b"].totals(after=small_sum)
    red0b = stage["l0b"].finish()
    for k in late:
        keep(k, adamw_big(k, 0, red0b, half_done[k], ()))
    for k in WEIGHTS:
        if k not in BIG:
            grads[k] = grads[k].reshape(args[k].shape)
            delta[k], new_m[k], new_v[k] = _update(args[k], grads[k], args["m_" + k], args["v_" + k], f"adamw_{k}")
    return (loss, grad_x, *[grads[k] for k in WEIGHTS], *[delta[k] for k in WEIGHTS],
            *[new_m[k] for k in WEIGHTS], *[new_v[k] for k in WEIGHTS])


def kernel(x, meta_tokens, attn_norm, w_in, conv_w, q_lat_norm, kv_lat_norm, w_uq, w_ukv, q_norm, k_norm, pool_w, pool_scale, w_branch_a, w_branch_b, w_branch_c, w_o, mlp_norm, w_up, w_down, loss_target, m_meta_tokens, m_attn_norm, m_w_in, m_conv_w, m_q_lat_norm, m_kv_lat_norm, m_w_uq, m_w_ukv, m_q_norm, m_k_norm, m_pool_w, m_pool_scale, m_w_branch_a, m_w_branch_b, m_w_branch_c, m_w_o, m_mlp_norm, m_w_up, m_w_down, v_meta_tokens, v_attn_norm, v_w_in, v_conv_w, v_q_lat_norm, v_kv_lat_norm, v_w_uq, v_w_ukv, v_q_norm, v_k_norm, v_pool_w, v_pool_scale, v_w_branch_a, v_w_branch_b, v_w_branch_c, v_w_o, v_mlp_norm, v_w_up, v_w_down):
    return _step(dict(locals()))
```
